```python
import math
import jax
import jax.numpy as jnp
from jax import lax
import numpy as np

D_MODEL = 1024
BATCH = 8
SEQ = 8192
DEPTH = 1

GRID_W = 64
D_MIX = 1024
EPS = 1e-6

HG_HEADS = 4
HG_DK = 128
HG_DV = 128
HG_KW = HG_HEADS * HG_DK
HG_VW = HG_HEADS * HG_DV
CHUNK = 64

ATT_HEADS = 8
ATT_KV_HEADS = 2
ATT_DH = 64
ATT_GROUP = ATT_HEADS // ATT_KV_HEADS
ATT_QW = ATT_HEADS * ATT_DH
ATT_KVW = ATT_KV_HEADS * ATT_DH
ROPE_THETA = 10000.0
Q_BLOCK = 128

D_IN = 2 * HG_KW + HG_KW + 2 * HG_VW + ATT_QW + 2 * ATT_KVW

D_FF = -(-8 * D_MODEL // (3 * 256)) * 256

kernel_name = "hybrid_hgrn2_axial_gqa_encoder"


def _rmsnorm(x, w):
    xf = x.astype(jnp.float32)
    y = xf * lax.rsqrt(jnp.mean(xf * xf, axis=-1, keepdims=True) + EPS)
    return (y * w.astype(jnp.float32)).astype(x.dtype)


def _gla_chunk_scan(q, k, v, log_f):
    B, L, H, DK = q.shape
    DV = v.shape[-1]
    n = L // CHUNK

    def to_chunks(a):
        return a.reshape(B, n, CHUNK, H, a.shape[-1]).transpose(1, 0, 3, 2, 4)

    qc, kc, vc, gc = to_chunks(q), to_chunks(k), to_chunks(v), to_chunks(log_f)
    b = jnp.cumsum(gc, axis=3)
    b_ref = b[:, :, :, CHUNK // 2 - 1:CHUNK // 2, :]
    b_last = b[:, :, :, -1:, :]
    q_in = qc * jnp.exp(b - b_ref)
    k_in = kc * jnp.exp(b_ref - b)
    scores = jnp.einsum('nbhtk,nbhsk->nbhts', q_in, k_in)
    causal_in_scan = jnp.tril(jnp.ones((CHUNK, CHUNK), dtype=bool))
    scores = jnp.where(causal_in_scan, scores, 0.0)
    o_intra = jnp.einsum('nbhts,nbhsv->nbhtv', scores, vc)
    contrib = jnp.einsum('nbhsk,nbhsv->nbhkv', kc * jnp.exp(b_last - b), vc).astype(jnp.float32)
    decay = jnp.exp(b_last[:, :, :, 0, :]).astype(jnp.float32)

    def step(S, inp):
        d, c = inp
        return d[..., None] * S + c, S

    S0 = jnp.zeros((B, H, DK, DV), jnp.float32)
    _, S_prev = lax.scan(step, S0, (decay, contrib))
    o_inter = jnp.einsum('nbhtk,nbhkv->nbhtv', qc * jnp.exp(b), S_prev)
    o = (o_intra + o_inter).transpose(1, 0, 3, 2, 4).reshape(B, L, H, DV)
    return o.astype(v.dtype)


def _hgrn2_group(u_q, u_ff, u_fb, u_i, u_g, lb_fwd, lb_bwd, norm_w):
    B, L, _ = u_q.shape
    q = jax.nn.silu(u_q).reshape(B, L, HG_HEADS, HG_DK)
    i = u_i.reshape(B, L, HG_HEADS, HG_DV)

    def gates(z, lb):
        zf = z.astype(jnp.float32)
        f = lb + (1.0 - lb) * jax.nn.sigmoid(zf)
        k = (1.0 - lb) * jax.nn.sigmoid(-zf)
        return (jnp.log(f).reshape(B, L, HG_HEADS, HG_DK),
                k.reshape(B, L, HG_HEADS, HG_DK).astype(z.dtype))

    logf_f, k_f = gates(u_ff, lb_fwd)
    logf_b, k_b = gates(u_fb, lb_bwd)
    o_fwd = _gla_chunk_scan(q, k_f, i, logf_f)
    flip = lambda a: jnp.flip(a, axis=1)
    o_bwd = flip(_gla_chunk_scan(flip(q), flip(k_b), flip(i), flip(logf_b)))
    o = _rmsnorm(o_fwd + o_bwd, norm_w)
    o = o * jax.nn.silu(u_g.reshape(B, L, HG_HEADS, HG_DV))
    return o.reshape(B, L, HG_VW)


def _axial_rope_tables(L):
    rows = L // GRID_W
    row = jnp.repeat(jnp.arange(rows), GRID_W).astype(jnp.float32)
    col = jnp.tile(jnp.arange(GRID_W), rows).astype(jnp.float32)
    axis_dim = ATT_DH // 2
    freqs = ROPE_THETA ** (-jnp.arange(0, axis_dim, 2, dtype=jnp.float32) / axis_dim)
    ang = jnp.concatenate([row[:, None] * freqs, col[:, None] * freqs], axis=-1)
    return jnp.cos(ang), jnp.sin(ang)


def _apply_rope(x, cos, sin):
    xf = x.astype(jnp.float32)
    x1, x2 = xf[..., 0::2], xf[..., 1::2]
    c, s = cos[None, :, None, :], sin[None, :, None, :]
    out = jnp.stack([x1 * c - x2 * s, x1 * s + x2 * c], axis=-1).reshape(x.shape)
    return out.astype(x.dtype)


def _block_attention(q, k, v):
    B, L, _, dh = q.shape
    nq = L // Q_BLOCK
    qb = q.reshape(B, nq, Q_BLOCK, ATT_KV_HEADS, ATT_GROUP, dh).transpose(1, 0, 3, 4, 2, 5)
    kt = k.transpose(0, 2, 1, 3)
    vt = v.transpose(0, 2, 1, 3)
    scale = dh ** -0.5

    def one_block(qblk):
        s = jnp.einsum('bhgqd,bhkd->bhgqk', qblk, kt).astype(jnp.float32) * scale
        p = jax.nn.softmax(s, axis=-1)
        return jnp.einsum('bhgqk,bhkd->bhgqd', p.astype(vt.dtype), vt)

    o = lax.map(one_block, qb)
    return o.transpose(1, 0, 4, 2, 3, 5).reshape(B, L, ATT_QW)


def _attention_group(u_q, u_k, u_v, q_norm_w, k_norm_w, out_norm_w):
    B, L, _ = u_q.shape
    q = _rmsnorm(u_q.reshape(B, L, ATT_HEADS, ATT_DH), q_norm_w)
    k = _rmsnorm(u_k.reshape(B, L, ATT_KV_HEADS, ATT_DH), k_norm_w)
    v = u_v.reshape(B, L, ATT_KV_HEADS, ATT_DH)
    cos, sin = _axial_rope_tables(L)
    q, k = _apply_rope(q, cos, sin), _apply_rope(k, cos, sin)
    o = _block_attention(q, k, v)
    return _rmsnorm(o, out_norm_w)


def _fwd_setup_inputs(seed: int = 0) -> dict:
    key = jax.random.key(seed)
    ks = jax.random.split(key, 16)
    f32 = jnp.float32
    gain = lambda k, shape: 1.0 + 0.02 * jax.random.normal(k, shape, f32)
    return {
        "x": jax.random.normal(ks[0], (BATCH, SEQ, D_MODEL), f32),
        "norm1_w": gain(ks[1], (DEPTH, D_MODEL)),
        "w_in": jax.random.normal(ks[2], (DEPTH, D_MODEL, D_IN), f32) * D_MODEL ** -0.5,
        "lb_logits": 0.5 * jax.random.normal(ks[3], (2, DEPTH + 1, HG_KW), f32),
        "hg_norm_w": gain(ks[4], (DEPTH, HG_DV)),
        "q_norm_w": gain(ks[5], (DEPTH, ATT_DH)),
        "k_norm_w": gain(ks[6], (DEPTH, ATT_DH)),
        "att_norm_w": gain(ks[7], (DEPTH, ATT_QW)),
        "w_out": jax.random.normal(ks[8], (DEPTH, D_MIX, D_MODEL), f32) * D_MIX ** -0.5,
        "norm2_w": gain(ks[9], (DEPTH, D_MODEL)),
        "w_gate_up": jax.random.normal(ks[10], (DEPTH, D_MODEL, 2 * D_FF), f32) * D_MODEL ** -0.5,
        "w_down": jax.random.normal(ks[11], (DEPTH, D_FF, D_MODEL), f32) * D_FF ** -0.5,
        "final_norm_w": gain(ks[12], (D_MODEL,)),
    }


def _fwd_reference(x, norm1_w, w_in, lb_logits, hg_norm_w, q_norm_w, k_norm_w, att_norm_w,
              w_out, norm2_w, w_gate_up, w_down, final_norm_w):
    lb_all = jnp.cumsum(jax.nn.softmax(lb_logits.astype(jnp.float32), axis=1), axis=1)
    splits = np.cumsum([HG_KW, HG_KW, HG_KW, HG_VW, HG_VW, ATT_QW, ATT_KVW]).tolist()
    for l in range(DEPTH):
        h = _rmsnorm(x, norm1_w[l])
        u = jnp.einsum('bld,de->ble', h, w_in[l])
        u_q, u_ff, u_fb, u_i, u_g, a_q, a_k, a_v = jnp.split(u, splits, axis=-1)
        o_hg = _hgrn2_group(u_q, u_ff, u_fb, u_i, u_g, lb_all[0, l], lb_all[1, l], hg_norm_w[l])
        o_att = _attention_group(a_q, a_k, a_v, q_norm_w[l], k_norm_w[l], att_norm_w[l])
        mix = jnp.concatenate([o_hg, o_att], axis=-1)
        x = x + jnp.einsum('ble,ed->bld', mix, w_out[l])
        h2 = _rmsnorm(x, norm2_w[l])
        gate, up = jnp.split(jnp.einsum('bld,df->blf', h2, w_gate_up[l]), 2, axis=-1)
        x = x + jnp.einsum('blf,fd->bld', jax.nn.silu(gate) * up, w_down[l])
    return _rmsnorm(x, final_norm_w)


import jax as _jax
import jax.numpy as _jnp

TWIN_FORMAT = 'train_step'
FWD_PARAMS = ['x', 'norm1_w', 'w_in', 'lb_logits', 'hg_norm_w', 'q_norm_w', 'k_norm_w', 'att_norm_w', 'w_out', 'norm2_w', 'w_gate_up', 'w_down', 'final_norm_w']
TWIN_WEIGHTS = ['norm1_w', 'w_in', 'lb_logits', 'hg_norm_w', 'q_norm_w', 'k_norm_w', 'att_norm_w', 'w_out', 'norm2_w', 'w_gate_up', 'w_down', 'final_norm_w']
TWIN_DIFF_INPUT = 'x'
TWIN_INPUTS = ['x', 'norm1_w', 'w_in', 'lb_logits', 'hg_norm_w', 'q_norm_w', 'k_norm_w', 'att_norm_w', 'w_out', 'norm2_w', 'w_gate_up', 'w_down', 'final_norm_w', 'loss_target', 'm_norm1_w', 'm_w_in', 'm_lb_logits', 'm_hg_norm_w', 'm_q_norm_w', 'm_k_norm_w', 'm_att_norm_w', 'm_w_out', 'm_norm2_w', 'm_w_gate_up', 'm_w_down', 'm_final_norm_w', 'v_norm1_w', 'v_w_in', 'v_lb_logits', 'v_hg_norm_w', 'v_q_norm_w', 'v_k_norm_w', 'v_att_norm_w', 'v_w_out', 'v_norm2_w', 'v_w_gate_up', 'v_w_down', 'v_final_norm_w']
TWIN_OUTPUTS = ['loss', 'grad_x', 'grad_norm1_w', 'grad_w_in', 'grad_lb_logits', 'grad_hg_norm_w', 'grad_q_norm_w', 'grad_k_norm_w', 'grad_att_norm_w', 'grad_w_out', 'grad_norm2_w', 'grad_w_gate_up', 'grad_w_down', 'grad_final_norm_w', 'delta_norm1_w', 'delta_w_in', 'delta_lb_logits', 'delta_hg_norm_w', 'delta_q_norm_w', 'delta_k_norm_w', 'delta_att_norm_w', 'delta_w_out', 'delta_norm2_w', 'delta_w_gate_up', 'delta_w_down', 'delta_final_norm_w', 'new_m_norm1_w', 'new_m_w_in', 'new_m_lb_logits', 'new_m_hg_norm_w', 'new_m_q_norm_w', 'new_m_k_norm_w', 'new_m_att_norm_w', 'new_m_w_out', 'new_m_norm2_w', 'new_m_w_gate_up', 'new_m_w_down', 'new_m_final_norm_w', 'new_v_norm1_w', 'new_v_w_in', 'new_v_lb_logits', 'new_v_hg_norm_w', 'new_v_q_norm_w', 'new_v_k_norm_w', 'new_v_att_norm_w', 'new_v_w_out', 'new_v_norm2_w', 'new_v_w_gate_up', 'new_v_w_down', 'new_v_final_norm_w']
TWIN_LEAF_KINDS = {'loss': 'loss', 'grad_x': 'grad_x', 'grad_norm1_w': 'grad_w', 'grad_w_in': 'grad_w', 'grad_lb_logits': 'grad_w', 'grad_hg_norm_w': 'grad_w', 'grad_q_norm_w': 'grad_w', 'grad_k_norm_w': 'grad_w', 'grad_att_norm_w': 'grad_w', 'grad_w_out': 'grad_w', 'grad_norm2_w': 'grad_w', 'grad_w_gate_up': 'grad_w', 'grad_w_down': 'grad_w', 'grad_final_norm_w': 'grad_w', 'delta_norm1_w': 'delta_w', 'delta_w_in': 'delta_w', 'delta_lb_logits': 'delta_w', 'delta_hg_norm_w': 'delta_w', 'delta_q_norm_w': 'delta_w', 'delta_k_norm_w': 'delta_w', 'delta_att_norm_w': 'delta_w', 'delta_w_out': 'delta_w', 'delta_norm2_w': 'delta_w', 'delta_w_gate_up': 'delta_w', 'delta_w_down': 'delta_w', 'delta_final_norm_w': 'delta_w', 'new_m_norm1_w': 'new_m', 'new_m_w_in': 'new_m', 'new_m_lb_logits': 'new_m', 'new_m_hg_norm_w': 'new_m', 'new_m_q_norm_w': 'new_m', 'new_m_k_norm_w': 'new_m', 'new_m_att_norm_w': 'new_m', 'new_m_w_out': 'new_m', 'new_m_norm2_w': 'new_m', 'new_m_w_gate_up': 'new_m', 'new_m_w_down': 'new_m', 'new_m_final_norm_w': 'new_m', 'new_v_norm1_w': 'new_v', 'new_v_w_in': 'new_v', 'new_v_lb_logits': 'new_v', 'new_v_hg_norm_w': 'new_v', 'new_v_q_norm_w': 'new_v', 'new_v_k_norm_w': 'new_v', 'new_v_att_norm_w': 'new_v', 'new_v_w_out': 'new_v', 'new_v_norm2_w': 'new_v', 'new_v_w_gate_up': 'new_v', 'new_v_w_down': 'new_v', 'new_v_final_norm_w': 'new_v'}


def _forward(args):
    return _fwd_reference(*[args[k] for k in FWD_PARAMS])


def _output_shape():
    def fwd():
        inp = _fwd_setup_inputs(0)
        return _fwd_reference(*[inp[k] for k in FWD_PARAMS])
    out = _jax.eval_shape(fwd)
    return out.shape, out.dtype

N_MICROBATCH = 1
ADAM_LR = 0.001
ADAM_B1 = 0.9
ADAM_B2 = 0.999
ADAM_EPS = 1e-08
ADAM_WD = 0.01
ADAM_STEP = 10
PER_EXAMPLE_BATCH_AXIS = {'x': 0, 'loss_target': 0}
SHARED_INPUTS = []
_WEIGHT_DTYPES = {'norm1_w': _jnp.float32, 'w_in': _jnp.float32, 'lb_logits': _jnp.float32, 'hg_norm_w': _jnp.float32, 'q_norm_w': _jnp.float32, 'k_norm_w': _jnp.float32, 'att_norm_w': _jnp.float32, 'w_out': _jnp.float32, 'norm2_w': _jnp.float32, 'w_gate_up': _jnp.float32, 'w_down': _jnp.float32, 'final_norm_w': _jnp.float32}
MOMENT_SCALE = {'norm1_w': 2.943459e-01, 'w_in': 1.614034e-01, 'lb_logits': 6.398348e-03, 'hg_norm_w': 2.482544e-01, 'q_norm_w': 6.148160e-01, 'k_norm_w': 5.924796e-01, 'att_norm_w': 2.439325e-01, 'w_out': 1.780079e-01, 'norm2_w': 1.520219e-01, 'w_gate_up': 6.435699e-02, 'w_down': 1.049891e-01, 'final_norm_w': 6.411296e+01}


def _to_microbatches(a, axis):
    t = _jnp.moveaxis(a, axis, 0)
    t = t.reshape((N_MICROBATCH, t.shape[0] // N_MICROBATCH) + t.shape[1:])
    return _jnp.moveaxis(t, 1, axis + 1)


def setup_inputs(seed: int = 0) -> dict:
    inp = _fwd_setup_inputs(seed)
    key = _jax.random.fold_in(_jax.random.key(seed), 7919)
    shape, _ = _output_shape()
    out = dict(inp)
    out["loss_target"] = _jax.random.normal(_jax.random.fold_in(key, 0), shape, _jnp.float32)
    for i, name in enumerate(TWIN_WEIGHTS):
        w = inp[name].astype(_jnp.float32)
        if MOMENT_SCALE is None:
            s = _jnp.sqrt(_jnp.mean(_jnp.square(w)) + 1e-30)
        else:
            s = MOMENT_SCALE[name]
        km, kv = _jax.random.split(_jax.random.fold_in(key, i + 1))
        out[name] = w
        out["m_" + name] = s * _jax.random.normal(km, w.shape, _jnp.float32)
        out["v_" + name] = (s * s) * _jax.random.uniform(kv, w.shape, _jnp.float32, 0.5, 1.5)
    if N_MICROBATCH > 1:
        for name, axis in PER_EXAMPLE_BATCH_AXIS.items():
            out[name] = _to_microbatches(out[name], axis)
    return {'x': out['x'], 'norm1_w': out['norm1_w'], 'w_in': out['w_in'], 'lb_logits': out['lb_logits'], 'hg_norm_w': out['hg_norm_w'], 'q_norm_w': out['q_norm_w'], 'k_norm_w': out['k_norm_w'], 'att_norm_w': out['att_norm_w'], 'w_out': out['w_out'], 'norm2_w': out['norm2_w'], 'w_gate_up': out['w_gate_up'], 'w_down': out['w_down'], 'final_norm_w': out['final_norm_w'], 'loss_target': out['loss_target'], 'm_norm1_w': out['m_norm1_w'], 'm_w_in': out['m_w_in'], 'm_lb_logits': out['m_lb_logits'], 'm_hg_norm_w': out['m_hg_norm_w'], 'm_q_norm_w': out['m_q_norm_w'], 'm_k_norm_w': out['m_k_norm_w'], 'm_att_norm_w': out['m_att_norm_w'], 'm_w_out': out['m_w_out'], 'm_norm2_w': out['m_norm2_w'], 'm_w_gate_up': out['m_w_gate_up'], 'm_w_down': out['m_w_down'], 'm_final_norm_w': out['m_final_norm_w'], 'v_norm1_w': out['v_norm1_w'], 'v_w_in': out['v_w_in'], 'v_lb_logits': out['v_lb_logits'], 'v_hg_norm_w': out['v_hg_norm_w'], 'v_q_norm_w': out['v_q_norm_w'], 'v_k_norm_w': out['v_k_norm_w'], 'v_att_norm_w': out['v_att_norm_w'], 'v_w_out': out['v_w_out'], 'v_norm2_w': out['v_norm2_w'], 'v_w_gate_up': out['v_w_gate_up'], 'v_w_down': out['v_w_down'], 'v_final_norm_w': out['v_final_norm_w']}


def _loss(weights, diff, rest, loss_target):
    with _jax.named_scope("forward"):
        args = {**rest, TWIN_DIFF_INPUT: diff, **{k: w.astype(_WEIGHT_DTYPES[k]) for k, w in weights.items()}}
        y = _forward(args)
    with _jax.named_scope("loss_head"):
        err = _jnp.square(y.astype(_jnp.float32) - loss_target)
        return 0.5 * _jnp.sum(_jnp.mean(err, axis=-1)) if err.ndim else 0.5 * err


def _adamw(w, g, m, v):
    m = ADAM_B1 * m + (1.0 - ADAM_B1) * g
    v = ADAM_B2 * v + (1.0 - ADAM_B2) * _jnp.square(g)
    m_hat = m / (1.0 - ADAM_B1 ** ADAM_STEP)
    v_hat = v / (1.0 - ADAM_B2 ** ADAM_STEP)
    delta = -ADAM_LR * (m_hat / (_jnp.sqrt(v_hat) + ADAM_EPS) + ADAM_WD * w)
    return delta, m, v


def reference(x, norm1_w, w_in, lb_logits, hg_norm_w, q_norm_w, k_norm_w, att_norm_w, w_out, norm2_w, w_gate_up, w_down, final_norm_w, loss_target, m_norm1_w, m_w_in, m_lb_logits, m_hg_norm_w, m_q_norm_w, m_k_norm_w, m_att_norm_w, m_w_out, m_norm2_w, m_w_gate_up, m_w_down, m_final_norm_w, v_norm1_w, v_w_in, v_lb_logits, v_hg_norm_w, v_q_norm_w, v_k_norm_w, v_att_norm_w, v_w_out, v_norm2_w, v_w_gate_up, v_w_down, v_final_norm_w):
    given = dict(x=x, norm1_w=norm1_w, w_in=w_in, lb_logits=lb_logits, hg_norm_w=hg_norm_w, q_norm_w=q_norm_w, k_norm_w=k_norm_w, att_norm_w=att_norm_w, w_out=w_out, norm2_w=norm2_w, w_gate_up=w_gate_up, w_down=w_down, final_norm_w=final_norm_w, loss_target=loss_target, m_norm1_w=m_norm1_w, m_w_in=m_w_in, m_lb_logits=m_lb_logits, m_hg_norm_w=m_hg_norm_w, m_q_norm_w=m_q_norm_w, m_k_norm_w=m_k_norm_w, m_att_norm_w=m_att_norm_w, m_w_out=m_w_out, m_norm2_w=m_norm2_w, m_w_gate_up=m_w_gate_up, m_w_down=m_w_down, m_final_norm_w=m_final_norm_w, v_norm1_w=v_norm1_w, v_w_in=v_w_in, v_lb_logits=v_lb_logits, v_hg_norm_w=v_hg_norm_w, v_q_norm_w=v_q_norm_w, v_k_norm_w=v_k_norm_w, v_att_norm_w=v_att_norm_w, v_w_out=v_w_out, v_norm2_w=v_norm2_w, v_w_gate_up=v_w_gate_up, v_w_down=v_w_down, v_final_norm_w=v_final_norm_w)
    weights = {n: given[n] for n in TWIN_WEIGHTS}
    shared = {n: given[n] for n in SHARED_INPUTS}
    per_example = {n: given[n] for n in ['x']}
    grad_fn = _jax.value_and_grad(_loss, argnums=(0, 1))

    def one_microbatch(ex, loss_target):
        ex = dict(ex)
        diff = ex.pop(TWIN_DIFF_INPUT)
        return grad_fn(weights, diff, {**shared, **ex}, loss_target)

    if N_MICROBATCH == 1:
        loss, (grad_w, grad_x) = one_microbatch(per_example, given["loss_target"])
    else:
        def body(carry, xs):
            loss_sum, grad_sum = carry
            l_k, (gw_k, gx_k) = one_microbatch(xs[0], xs[1])
            with _jax.named_scope("update"):
                return (loss_sum + l_k, _jax.tree.map(_jnp.add, grad_sum, gw_k)), gx_k

        init = (_jnp.zeros((), _jnp.float32), _jax.tree.map(_jnp.zeros_like, weights))
        (loss, grad_w), grad_x = _jax.lax.scan(body, init, (per_example, given["loss_target"]))
    with _jax.named_scope("update"):
        delta_w, new_m, new_v = {}, {}, {}
        for n in TWIN_WEIGHTS:
            delta_w[n], new_m[n], new_v[n] = _adamw(weights[n], grad_w[n], given["m_" + n], given["v_" + n])
    return (loss, grad_x, *[grad_w[n] for n in TWIN_WEIGHTS], *[delta_w[n] for n in TWIN_WEIGHTS],
            *[new_m[n] for n in TWIN_WEIGHTS], *[new_v[n] for n in TWIN_WEIGHTS])
```

```python
import functools

import numpy as np
import jax
import jax.numpy as jnp
from jax import lax
from jax.experimental import pallas as pl
from jax.experimental.pallas import tpu as pltpu

F32 = jnp.float32
BF16 = jnp.bfloat16
MESH = pl.DeviceIdType.MESH
HIGHEST = lax.Precision.HIGHEST

EPS = 1e-6
GRID_W = 64
HG_HEADS = 4
HG_D = 128
HG_W = HG_HEADS * HG_D
CHUNK = 64
ATT_HEADS = 8
ATT_KV = 2
ATT_DH = 64
ATT_QW = ATT_HEADS * ATT_DH
ATT_KVW = ATT_KV * ATT_DH
ROPE_THETA = 10000.0
N_CHIPS = 4
N_DEV = 8

ADAM_LR = 0.001
ADAM_B1 = 0.9
ADAM_B2 = 0.999
ADAM_EPS = 1e-08
ADAM_WD = 0.01
ADAM_STEP = 10

VMEM_LIMIT = 52 * 1024 * 1024
LANES = 128


def _params(sem=None):
    return pltpu.CompilerParams(dimension_semantics=sem, vmem_limit_bytes=VMEM_LIMIT)


def _dg(a, b, ca, cb):
    return lax.dot_general(a.astype(BF16), b.astype(BF16), (((ca,), (cb,)), ((), ())),
                           preferred_element_type=F32)


@jax.custom_vjp
def dot_nn(a, b):
    return _dg(a, b, 1, 0)


def _dot_nn_fwd(a, b):
    return _dg(a, b, 1, 0), (a, b)


def _dot_nn_bwd(res, g):
    a, b = res
    return _dg(g, b, 1, 1), _dg(a, g, 0, 0)


dot_nn.defvjp(_dot_nn_fwd, _dot_nn_bwd)


@jax.custom_vjp
def dot_nt(a, b):
    return _dg(a, b, 1, 1)


def _dot_nt_fwd(a, b):
    return _dg(a, b, 1, 1), (a, b)


def _dot_nt_bwd(res, g):
    a, b = res
    return _dg(g, b, 1, 0), _dg(g, a, 0, 0)


dot_nt.defvjp(_dot_nt_fwd, _dot_nt_bwd)


@jax.custom_vjp
def dot_tn(a, b):
    return _dg(a, b, 0, 0)


def _dot_tn_fwd(a, b):
    return _dg(a, b, 0, 0), (a, b)


def _dot_tn_bwd(res, g):
    a, b = res
    return _dg(b, g, 1, 1), _dg(a, g, 1, 0)


dot_tn.defvjp(_dot_tn_fwd, _dot_tn_bwd)


def _xdot(a, m):
    return jnp.dot(a, m, precision=HIGHEST, preferred_element_type=F32)


@jax.custom_vjp
def swap_pairs(y):
    n = y.shape[-1]
    lane = lax.broadcasted_iota(jnp.int32, y.shape, 1)
    nxt = pltpu.roll(y, n - 1, 1)
    prv = pltpu.roll(y, 1, 1)
    return jnp.where(lane % 2 == 0, nxt, prv)


def _swap_fwd(y):
    return swap_pairs(y), None


def _swap_bwd(_, g):
    return (swap_pairs(g),)


swap_pairs.defvjp(_swap_fwd, _swap_bwd)


def _rms(x, w):
    return x * lax.rsqrt(jnp.mean(x * x, axis=-1, keepdims=True) + EPS) * w


def _sigmoid(x):
    return jax.nn.sigmoid(x)


def _rows(fn, name, rows, consts, outs, accs=(), tm=512):
    t = rows[0][0].shape[0]
    tm = min(tm, t)
    n_r, n_c, n_o, n_a = len(rows), len(consts), len(outs), len(accs)

    def body(*refs):
        r = refs[:n_r]
        c = refs[n_r:n_r + n_c]
        o = refs[n_r + n_c:n_r + n_c + n_o]
        a = refs[n_r + n_c + n_o:]
        ro, ao = fn(*[x[...] for x in r], *[x[...] for x in c])
        for ref, val in zip(o, ro):
            ref[...] = val.astype(ref.dtype)
        if n_a:
            @pl.when(pl.program_id(0) == 0)
            def _():
                for ref in a:
                    ref[...] = jnp.zeros(ref.shape, F32)
            for ref, val in zip(a, ao):
                ref[...] += val

    in_specs = [pl.BlockSpec((tm, w), lambda i, cb=cb: (i, cb)) for _, w, cb in rows]
    in_specs += [pl.BlockSpec(c.shape, lambda i, nd=c.ndim: (0,) * nd) for c in consts]
    out_specs = [pl.BlockSpec((tm, w), lambda i: (i, 0)) for w, _ in outs]
    out_specs += [pl.BlockSpec((1, w), lambda i: (0, 0)) for w in accs]
    out_shape = [jax.ShapeDtypeStruct((t, w), dt) for w, dt in outs]
    out_shape += [jax.ShapeDtypeStruct((1, w), F32) for w in accs]
    res = pl.pallas_call(
        body, name=name, grid=(t // tm,), in_specs=in_specs, out_specs=out_specs, out_shape=out_shape,
        compiler_params=_params(("arbitrary",)),
    )(*[a for a, _, _ in rows], *consts)
    return res


def _full(a):
    return (a, a.shape[1], 0)


def _pick(n, cap, mult):
    best = None
    for d in range(mult, min(n, cap) + 1, mult):
        if n % d == 0:
            best = d
    return best if best is not None else n


def mm_rows(a, b, name, trans_b=False, res=None, out_dtype=F32):
    m, k = a.shape
    n = b.shape[0] if trans_b else b.shape[1]
    tn = _pick(n, 3328, LANES)
    tm = _pick(m, 512 if k <= 3072 else 256, 8)
    has_res = res is not None

    def body(*refs):
        if has_res:
            a_ref, b_ref, r_ref, o_ref = refs
        else:
            a_ref, b_ref, o_ref = refs
        acc = _dg(a_ref[...], b_ref[...], 1, 1 if trans_b else 0)
        if has_res:
            acc = acc + r_ref[...]
        o_ref[...] = acc.astype(o_ref.dtype)

    in_specs = [pl.BlockSpec((tm, k), lambda j, i: (i, 0))]
    if trans_b:
        in_specs.append(pl.BlockSpec((tn, k), lambda j, i: (j, 0)))
    else:
        in_specs.append(pl.BlockSpec((k, tn), lambda j, i: (0, j)))
    args = [a, b]
    if has_res:
        in_specs.append(pl.BlockSpec((tm, tn), lambda j, i: (i, j)))
        args.append(res)
    return pl.pallas_call(
        body, name=name, grid=(n // tn, m // tm), in_specs=in_specs,
        out_specs=pl.BlockSpec((tm, tn), lambda j, i: (i, j)),
        out_shape=jax.ShapeDtypeStruct((m, n), out_dtype),
        compiler_params=_params(("arbitrary", "arbitrary")),
    )(*args)


def mm_tn(a, b, name):
    t, m = a.shape
    n = b.shape[1]
    tm = _pick(m, 512, LANES)
    tn = _pick(n, 3328, LANES)
    tk = _pick(t, 1024, 8)

    def body(a_ref, b_ref, o_ref):
        @pl.when(pl.program_id(2) == 0)
        def _():
            o_ref[...] = jnp.zeros(o_ref.shape, F32)
        o_ref[...] += _dg(a_ref[...], b_ref[...], 0, 0)

    return pl.pallas_call(
        body, name=name, grid=(m // tm, n // tn, t // tk),
        in_specs=[pl.BlockSpec((tk, tm), lambda i, j, kk: (kk, i)),
                  pl.BlockSpec((tk, tn), lambda i, j, kk: (kk, j))],
        out_specs=pl.BlockSpec((tm, tn), lambda i, j, kk: (i, j)),
        out_shape=jax.ShapeDtypeStruct((m, n), F32),
        compiler_params=_params(("arbitrary", "arbitrary", "arbitrary")),
    )(a, b)


def _lower_bound(l0, l1):
    m = jnp.maximum(l0, l1)
    e0 = jnp.exp(l0 - m)
    e1 = jnp.exp(l1 - m)
    return e0 / (e0 + e1)


def _gla_consts(rev):
    ri = lax.broadcasted_iota(jnp.int32, (CHUNK, CHUNK), 0)
    ci = lax.broadcasted_iota(jnp.int32, (CHUNK, CHUNK), 1)
    keep = (ci >= ri) if rev else (ci <= ri)
    tm = keep.astype(F32)
    if rev:
        rvec = (ci >= CHUNK // 2).astype(F32)
    else:
        rvec = (ci <= CHUNK // 2 - 1).astype(F32)
    m3 = jnp.concatenate([tm, tm - rvec, 1.0 - tm], axis=0)
    return keep, m3


def _gla_chunk(uq, uf, ui, l0, l1, st, keep, m3):
    lb = _lower_bound(l0, l1)
    q = uq * _sigmoid(uq)
    f = lb + (1.0 - lb) * _sigmoid(uf)
    k = (1.0 - lb) * _sigmoid(-uf)
    g = jnp.log(f)
    b3 = _xdot(m3, g)
    b = b3[0:CHUNK]
    bmr = b3[CHUNK:2 * CHUNK]
    lmb = b3[2 * CHUNK:3 * CHUNK]
    blast = jnp.sum(g, axis=0, keepdims=True)
    scores = dot_nt(q * jnp.exp(bmr), k * jnp.exp(-bmr))
    scores = jnp.where(keep, scores, 0.0)
    o = dot_nn(scores, ui) + dot_nt(q * jnp.exp(b), st)
    st_new = st * jnp.exp(blast) + dot_tn(ui, k * jnp.exp(lmb))
    return o, st_new


def _gla_blocks(t):
    tb = min(512, t)
    return tb, tb // CHUNK, t // tb


def gla_fwd(u, l0, l1, fcol, rev, name):
    t = u.shape[0]
    tb, ncb, nb = _gla_blocks(t)

    def blk(i):
        return (nb - 1 - i) if rev else i

    def body(uq_ref, uf_ref, ui_ref, l0_ref, l1_ref, o_ref, ss_ref, st_ref):
        @pl.when(pl.program_id(0) == 0)
        def _():
            st_ref[...] = jnp.zeros(st_ref.shape, F32)
        keep, m3 = _gla_consts(rev)

        def chunk(ci, carry):
            c = (ncb - 1 - ci) if rev else ci
            r0 = pl.multiple_of(c * CHUNK, CHUNK)
            for h in range(HG_HEADS):
                sl = slice(h * HG_D, (h + 1) * HG_D)
                st = st_ref[h]
                ss_ref[c, h] = st
                o, st_new = _gla_chunk(uq_ref[pl.ds(r0, CHUNK), sl], uf_ref[pl.ds(r0, CHUNK), sl],
                                       ui_ref[pl.ds(r0, CHUNK), sl], l0_ref[:, sl], l1_ref[:, sl], st, keep, m3)
                o_ref[pl.ds(r0, CHUNK), sl] = o
                st_ref[h] = st_new
            return carry

        lax.fori_loop(0, ncb, chunk, 0)

    row = lambda cb: pl.BlockSpec((tb, HG_W), lambda i: (blk(i), cb))
    vec = pl.BlockSpec((1, HG_W), lambda i: (0, 0))
    return pl.pallas_call(
        body, name=name, grid=(nb,),
        in_specs=[row(0), row(fcol), row(3), vec, vec],
        out_specs=[pl.BlockSpec((tb, HG_W), lambda i: (blk(i), 0)),
                   pl.BlockSpec((ncb, HG_HEADS, HG_D, HG_D), lambda i: (blk(i), 0, 0, 0))],
        out_shape=[jax.ShapeDtypeStruct((t, HG_W), F32),
                   jax.ShapeDtypeStruct((t // CHUNK, HG_HEADS, HG_D, HG_D), F32)],
        scratch_shapes=[pltpu.VMEM((HG_HEADS, HG_D, HG_D), F32)],
        compiler_params=_params(("arbitrary",)),
    )(u, u, u, l0, l1)


def gla_bwd(u, l0, l1, ss, do, fcol, rev, name):
    t = u.shape[0]
    tb, ncb, nb = _gla_blocks(t)

    def blk(i):
        return i if rev else (nb - 1 - i)

    def body(uq_ref, uf_ref, ui_ref, l0_ref, l1_ref, ss_ref, do_ref,
             dq_ref, df_ref, di_ref, dl0_ref, dl1_ref, dst_ref):
        @pl.when(pl.program_id(0) == 0)
        def _():
            dst_ref[...] = jnp.zeros(dst_ref.shape, F32)
            dl0_ref[...] = jnp.zeros(dl0_ref.shape, F32)
            dl1_ref[...] = jnp.zeros(dl1_ref.shape, F32)
        keep, m3 = _gla_consts(rev)

        def chunk(ci, carry):
            c = ci if rev else (ncb - 1 - ci)
            r0 = pl.multiple_of(c * CHUNK, CHUNK)
            for h in range(HG_HEADS):
                sl = slice(h * HG_D, (h + 1) * HG_D)
                fn = functools.partial(_gla_chunk, keep=keep, m3=m3)
                _, vjp = jax.vjp(fn, uq_ref[pl.ds(r0, CHUNK), sl], uf_ref[pl.ds(r0, CHUNK), sl],
                                 ui_ref[pl.ds(r0, CHUNK), sl], l0_ref[:, sl], l1_ref[:, sl], ss_ref[c, h])
                dq, df, di, dl0, dl1, dst = vjp((do_ref[pl.ds(r0, CHUNK), sl], dst_ref[h]))
                dq_ref[pl.ds(r0, CHUNK), sl] = dq
                df_ref[pl.ds(r0, CHUNK), sl] = df
                di_ref[pl.ds(r0, CHUNK), sl] = di
                dl0_ref[:, sl] += dl0
                dl1_ref[:, sl] += dl1
                dst_ref[h] = dst
            return carry

        lax.fori_loop(0, ncb, chunk, 0)

    row = lambda cb: pl.BlockSpec((tb, HG_W), lambda i: (blk(i), cb))
    vec = pl.BlockSpec((1, HG_W), lambda i: (0, 0))
    orow = pl.BlockSpec((tb, HG_W), lambda i: (blk(i), 0))
    return pl.pallas_call(
        body, name=name, grid=(nb,),
        in_specs=[row(0), row(fcol), row(3), vec, vec,
                  pl.BlockSpec((ncb, HG_HEADS, HG_D, HG_D), lambda i: (blk(i), 0, 0, 0)), orow],
        out_specs=[orow, orow, orow, vec, vec],
        out_shape=[jax.ShapeDtypeStruct((t, HG_W), F32)] * 3 + [jax.ShapeDtypeStruct((1, HG_W), F32)] * 2,
        scratch_shapes=[pltpu.VMEM((HG_HEADS, HG_D, HG_D), F32)],
        compiler_params=_params(("arbitrary",)),
    )(u, u, u, l0, l1, ss, do)


def _rope_tables(t):
    rows = t // GRID_W
    row = jnp.repeat(jnp.arange(rows), GRID_W).astype(F32)
    col = jnp.tile(jnp.arange(GRID_W), rows).astype(F32)
    axis_dim = ATT_DH // 2
    freqs = ROPE_THETA ** (-jnp.arange(0, axis_dim, 2, dtype=F32) / axis_dim)
    ang = jnp.concatenate([row[:, None] * freqs, col[:, None] * freqs], axis=-1)
    cos2 = jnp.repeat(jnp.cos(ang), 2, axis=-1)
    sin2 = jnp.repeat(jnp.sin(ang), 2, axis=-1) * jnp.tile(jnp.array([-1.0, 1.0], F32), ATT_DH // 2)
    return cos2, sin2


def _group_sum_matrix(width):
    idx = np.arange(width) // ATT_DH
    return jnp.asarray((idx[:, None] == idx[None, :]).astype(np.float32))


def _tile_matrix(width):
    m = np.zeros((LANES, width), np.float32)
    m[np.arange(width) % ATT_DH, np.arange(width)] = 1.0
    return jnp.asarray(m)


def _tile_w(w128, tile_m):
    w8 = jnp.broadcast_to(w128, (8, LANES))
    return jnp.sum(_xdot(w8, tile_m), axis=0, keepdims=True) * 0.125


def _head_norm_rope(a, w128, cos_t, sin_t, gsum, tile_m, scale):
    ssq = _xdot(a * a, gsum)
    y = a * lax.rsqrt(ssq * (1.0 / ATT_DH) + EPS) * _tile_w(w128, tile_m)
    return (y * cos_t + swap_pairs(y) * sin_t) * scale


def _att_prep_fn(aq, ak, cq, sq, ck, sk, qw, kw, gq, gk, tq, tk):
    q = _head_norm_rope(aq, qw, cq, sq, gq, tq, ATT_DH ** -0.5)
    k = _head_norm_rope(ak, kw, ck, sk, gk, tk, 1.0)
    return q, k


def _fa_blocks(lq, lk):
    return min(1024, lq), min(512, lk)


def fa_fwd(q, k, v):
    _, lq, dh = q.shape
    lk = k.shape[1]
    bq, bk = _fa_blocks(lq, lk)
    nk = lk // bk

    def body(q_ref, k_ref, v_ref, o_ref, lse_ref):
        qb = q_ref[0]

        def step(j, carry):
            m, l, acc = carry
            r0 = pl.multiple_of(j * bk, bk)
            kb = k_ref[0, pl.ds(r0, bk), :]
            vb = v_ref[0, pl.ds(r0, bk), :]
            s = _dg(qb, kb, 1, 1)
            m_new = jnp.maximum(m, jnp.max(s, axis=-1, keepdims=True))
            p = jnp.exp(s - m_new)
            alpha = jnp.exp(m - m_new)
            l = alpha * l + jnp.sum(p, axis=-1, keepdims=True)
            acc = alpha * acc + _dg(p, vb, 1, 0)
            return m_new, l, acc

        m0 = jnp.full((bq, 1), -jnp.inf, F32)
        m, l, acc = lax.fori_loop(0, nk, step, (m0, jnp.zeros((bq, 1), F32), jnp.zeros((bq, dh), F32)))
        o_ref[0] = acc / l
        lse_ref[0] = jnp.broadcast_to(m + jnp.log(l), (bq, LANES))

    kv = pl.BlockSpec((1, lk, dh), lambda g, i: (g, 0, 0))
    return pl.pallas_call(
        body, name="fa_fwd", grid=(ATT_KV, lq // bq),
        in_specs=[pl.BlockSpec((1, bq, dh), lambda g, i: (g, i, 0)), kv, kv],
        out_specs=[pl.BlockSpec((1, bq, dh), lambda g, i: (g, i, 0)),
                   pl.BlockSpec((1, bq, LANES), lambda g, i: (g, i, 0))],
        out_shape=[jax.ShapeDtypeStruct((ATT_KV, lq, dh), F32), jax.ShapeDtypeStruct((ATT_KV, lq, LANES), F32)],
        compiler_params=_params(("arbitrary", "arbitrary")),
    )(q, k, v)


def fa_dq(q, k, v, do, lse, dsum):
    _, lq, dh = q.shape
    lk = k.shape[1]
    bq, bk = _fa_blocks(lq, lk)
    nk = lk // bk

    def body(q_ref, k_ref, v_ref, do_ref, lse_ref, d_ref, dq_ref):
        qb = q_ref[0]
        dob = do_ref[0]
        lse_c = lse_ref[0][:, 0:1]
        d_c = d_ref[0][:, 0:1]

        def step(j, acc):
            r0 = pl.multiple_of(j * bk, bk)
            kb = k_ref[0, pl.ds(r0, bk), :]
            vb = v_ref[0, pl.ds(r0, bk), :]
            p = jnp.exp(_dg(qb, kb, 1, 1) - lse_c)
            ds = p * (_dg(dob, vb, 1, 1) - d_c)
            return acc + _dg(ds, kb, 1, 0)

        dq_ref[0] = lax.fori_loop(0, nk, step, jnp.zeros((bq, dh), F32))

    kv = pl.BlockSpec((1, lk, dh), lambda g, i: (g, 0, 0))
    qs = pl.BlockSpec((1, bq, dh), lambda g, i: (g, i, 0))
    return pl.pallas_call(
        body, name="fa_dq", grid=(ATT_KV, lq // bq),
        in_specs=[qs, kv, kv, qs, pl.BlockSpec((1, bq, LANES), lambda g, i: (g, i, 0)), qs],
        out_specs=qs,
        out_shape=jax.ShapeDtypeStruct((ATT_KV, lq, dh), F32),
        compiler_params=_params(("arbitrary", "arbitrary")),
    )(q, k, v, do, lse, dsum)


def fa_dkdv(q, k, v, do, lse_row, d_row):
    _, lq, dh = q.shape
    lk = k.shape[1]
    bq, bk = _fa_blocks(lq, lk)
    nq = lq // bq

    def body(q_ref, k_ref, v_ref, do_ref, lse_ref, d_ref, dk_ref, dv_ref):
        @pl.when(pl.program_id(2) == 0)
        def _():
            dk_ref[...] = jnp.zeros(dk_ref.shape, F32)
            dv_ref[...] = jnp.zeros(dv_ref.shape, F32)
        qb = q_ref[0]
        dob = do_ref[0]
        pt = jnp.exp(_dg(k_ref[0], qb, 1, 1) - lse_ref[0])
        dv_ref[0] += _dg(pt, dob, 1, 0)
        dst = pt * (_dg(v_ref[0], dob, 1, 1) - d_ref[0])
        dk_ref[0] += _dg(dst, qb, 1, 0)

    kv = pl.BlockSpec((1, bk, dh), lambda g, j, i: (g, j, 0))
    qs = pl.BlockSpec((1, bq, dh), lambda g, j, i: (g, i, 0))
    rowv = pl.BlockSpec((1, 1, bq), lambda g, j, i: (g, 0, i))
    return pl.pallas_call(
        body, name="fa_dkdv", grid=(ATT_KV, lk // bk, nq),
        in_specs=[qs, kv, kv, qs, rowv, rowv],
        out_specs=[kv, kv],
        out_shape=[jax.ShapeDtypeStruct((ATT_KV, lk, dh), F32)] * 2,
        compiler_params=_params(("arbitrary", "arbitrary", "arbitrary")),
    )(q, k, v, do, lse_row, d_row)


def _heads_major(a, heads):
    t = a.shape[0]
    return a.reshape(t, heads, ATT_DH).transpose(1, 0, 2).reshape(ATT_KV, (heads // ATT_KV) * t, ATT_DH)


def _tokens_major(a, heads, t):
    return a.reshape(heads, t, ATT_DH).transpose(1, 0, 2).reshape(t, heads * ATT_DH)


def _post_mix_fn(of, ob, ug, oa, hgw, attw):
    o = of + ob
    parts = []
    for h in range(HG_HEADS):
        parts.append(_rms(o[:, h * HG_D:(h + 1) * HG_D], hgw))
    hg = jnp.concatenate(parts, axis=1) * (ug * _sigmoid(ug))
    return jnp.concatenate([hg, _rms(oa, attw)], axis=1)


def _swiglu_fn(gate, up):
    return gate * _sigmoid(gate) * up


def _loss_fn(x2, w, tgt):
    e = _rms(x2, w) - tgt
    return 0.5 * jnp.sum(jnp.mean(e * e, axis=-1, keepdims=True), axis=0, keepdims=True)


def _place():
    return lax.axis_index("x"), lax.axis_index("y"), lax.axis_index("c")


def _other_chips(x, y):
    return [(1 - x, y), (x, 1 - y), (1 - x, 1 - y)]


def allgather_shards(shard):
    r = shard.shape[0]
    hr = r // 2

    def body(src, out, ssem, rsem, lsem):
        x, y, c = _place()
        k = 2 * x + y
        sib = (x, y, 1 - c)
        chips = _other_chips(x, y)

        def half(kk, cc):
            return out.at[kk, pl.ds(cc * hr, hr), :]

        def copy(j, src_ref, dst_ref, to):
            return pltpu.make_async_remote_copy(src_ref=src_ref, dst_ref=dst_ref, send_sem=ssem.at[j],
                                                recv_sem=rsem.at[j], device_id=to, device_id_type=MESH)

        mine = pltpu.make_async_copy(src, out.at[k], lsem)
        mine.start()
        my_half = src.at[pl.ds(c * hr, hr), :]
        first = [copy(j, my_half, half(k, c), (px, py, c)) for j, (px, py) in enumerate(chips)]
        for cp in first:
            cp.start()
        passed = [copy(3 + j, half(2 * px + py, c), half(2 * px + py, c), sib) for j, (px, py) in enumerate(chips)]
        for j, (px, py) in enumerate(chips):
            copy(j, my_half, half(2 * px + py, c), (px, py, c)).wait_recv()
            passed[j].start()
        for j, (px, py) in enumerate(chips):
            copy(3 + j, my_half, half(2 * px + py, 1 - c), sib).wait_recv()
        for cp in first + passed:
            cp.wait_send()
        mine.wait()

    return pl.pallas_call(
        body, name="allgather_shards",
        in_specs=[pl.BlockSpec(memory_space=pl.ANY)],
        out_specs=pl.BlockSpec(memory_space=pl.ANY),
        out_shape=jax.ShapeDtypeStruct((N_CHIPS, r, LANES), shard.dtype),
        scratch_shapes=[pltpu.SemaphoreType.DMA((6,)), pltpu.SemaphoreType.DMA((6,)), pltpu.SemaphoreType.DMA],
    )(shard)


def rs_siblings(g):
    _, r, _ = g.shape
    hr = r // 2

    def body(src, out, ssem, rsem):
        x, y, c = _place()
        cp = pltpu.make_async_remote_copy(src_ref=src.at[:, pl.ds((1 - c) * hr, hr), :], dst_ref=out,
                                          send_sem=ssem, recv_sem=rsem, device_id=(x, y, 1 - c), device_id_type=MESH)
        cp.start()
        cp.wait()

    return pl.pallas_call(
        body, name="rs_siblings",
        in_specs=[pl.BlockSpec(memory_space=pl.ANY)],
        out_specs=pl.BlockSpec(memory_space=pl.ANY),
        out_shape=jax.ShapeDtypeStruct((N_CHIPS, hr, LANES), F32),
        scratch_shapes=[pltpu.SemaphoreType.DMA, pltpu.SemaphoreType.DMA],
    )(g)


def rs_chips(pa):
    def body(src, out, ssem, rsem, lsem):
        x, y, c = _place()
        k = 2 * x + y
        chips = _other_chips(x, y)
        mine = pltpu.make_async_copy(src.at[k], out.at[k], lsem)
        mine.start()
        sends = []
        for j, (px, py) in enumerate(chips):
            cp = pltpu.make_async_remote_copy(src_ref=src.at[2 * px + py], dst_ref=out.at[k], send_sem=ssem.at[j],
                                              recv_sem=rsem.at[j], device_id=(px, py, c), device_id_type=MESH)
            cp.start()
            sends.append(cp)
        for j, (px, py) in enumerate(chips):
            pltpu.make_async_remote_copy(src_ref=src.at[k], dst_ref=out.at[2 * px + py], send_sem=ssem.at[j],
                                         recv_sem=rsem.at[j], device_id=(px, py, c), device_id_type=MESH).wait_recv()
        for cp in sends:
            cp.wait_send()
        mine.wait()

    return pl.pallas_call(
        body, name="rs_chips",
        in_specs=[pl.BlockSpec(memory_space=pl.ANY)],
        out_specs=pl.BlockSpec(memory_space=pl.ANY),
        out_shape=jax.ShapeDtypeStruct(pa.shape, F32),
        scratch_shapes=[pltpu.SemaphoreType.DMA((3,)), pltpu.SemaphoreType.DMA((3,)), pltpu.SemaphoreType.DMA],
    )(pa)


def share_halves(rh):
    hr = rh.shape[0]

    def body(src, out, ssem, rsem, lsem):
        x, y, c = _place()
        dst = out.at[pl.ds(c * hr, hr), :]
        mine = pltpu.make_async_copy(src, dst, lsem)
        mine.start()
        cp = pltpu.make_async_remote_copy(src_ref=src, dst_ref=dst, send_sem=ssem, recv_sem=rsem,
                                          device_id=(x, y, 1 - c), device_id_type=MESH)
        cp.start()
        pltpu.make_async_remote_copy(src_ref=src, dst_ref=out.at[pl.ds((1 - c) * hr, hr), :], send_sem=ssem,
                                     recv_sem=rsem, device_id=(x, y, 1 - c), device_id_type=MESH).wait_recv()
        cp.wait_send()
        mine.wait()

    return pl.pallas_call(
        body, name="share_halves",
        in_specs=[pl.BlockSpec(memory_space=pl.ANY)],
        out_specs=pl.BlockSpec(memory_space=pl.ANY),
        out_shape=jax.ShapeDtypeStruct((2 * hr, LANES), F32),
        scratch_shapes=[pltpu.SemaphoreType.DMA, pltpu.SemaphoreType.DMA, pltpu.SemaphoreType.DMA],
    )(rh)


def allreduce_small(p, name):
    rows, width = p.shape

    def body(p_ref, s_ref, gath, ssem, rsem):
        x, y, c = _place()
        me = 4 * x + 2 * y + c
        copies = []
        for d in range(1, N_DEV):
            dx, dy, dc = (d >> 2) & 1, (d >> 1) & 1, d & 1
            peer = (x ^ dx, y ^ dy, c ^ dc)
            cp = pltpu.make_async_remote_copy(src_ref=p_ref, dst_ref=gath.at[me], send_sem=ssem.at[d - 1],
                                              recv_sem=rsem.at[d - 1], device_id=peer, device_id_type=MESH)
            cp.start()
            copies.append(cp)
        gath[me] = p_ref[...]
        for d, cp in enumerate(copies, start=1):
            dx, dy, dc = (d >> 2) & 1, (d >> 1) & 1, d & 1
            peer_slot = 4 * (x ^ dx) + 2 * (y ^ dy) + (c ^ dc)
            pltpu.make_async_remote_copy(src_ref=p_ref, dst_ref=gath.at[peer_slot], send_sem=ssem.at[d - 1],
                                         recv_sem=rsem.at[d - 1], device_id=(x ^ dx, y ^ dy, c ^ dc),
                                         device_id_type=MESH).wait_recv()
        for cp in copies:
            cp.wait_send()
        acc = gath[0]
        for d in range(1, N_DEV):
            acc = acc + gath[d]
        s_ref[...] = acc

    return pl.pallas_call(
        body, name=name,
        in_specs=[pl.BlockSpec(memory_space=pltpu.VMEM)],
        out_specs=pl.BlockSpec(memory_space=pltpu.VMEM),
        out_shape=jax.ShapeDtypeStruct((rows, width), F32),
        scratch_shapes=[pltpu.VMEM((N_DEV, rows, width), F32), pltpu.SemaphoreType.DMA((N_DEV - 1,)),
                        pltpu.SemaphoreType.DMA((N_DEV - 1,))],
    )(p)


def add_my_half(g, recv):
    _, r, _ = g.shape
    hr = r // 2
    tb = _pick(hr, 1600, 8)
    nb = hr // tb
    c_arr = lax.axis_index("c").astype(jnp.int32).reshape(1)

    def body(c_ref, g_ref, r_ref, o_ref):
        o_ref[...] = g_ref[...] + r_ref[...]

    return pl.pallas_call(
        body, name="add_my_half",
        grid_spec=pltpu.PrefetchScalarGridSpec(
            num_scalar_prefetch=1, grid=(N_CHIPS, nb),
            in_specs=[pl.BlockSpec((1, tb, LANES), lambda k, i, c_ref: (k, c_ref[0] * nb + i, 0)),
                      pl.BlockSpec((1, tb, LANES), lambda k, i, c_ref: (k, i, 0))],
            out_specs=pl.BlockSpec((1, tb, LANES), lambda k, i, c_ref: (k, i, 0))),
        out_shape=jax.ShapeDtypeStruct((N_CHIPS, hr, LANES), F32),
        compiler_params=_params(("arbitrary", "arbitrary")),
    )(c_arr, g, recv)


def sum_chips(parts):
    _, hr, _ = parts.shape
    tb = _pick(hr, 1600, 8)

    def body(p_ref, o_ref):
        o_ref[...] = ((p_ref[0] + p_ref[1]) + p_ref[2]) + p_ref[3]

    return pl.pallas_call(
        body, name="sum_chips", grid=(hr // tb,),
        in_specs=[pl.BlockSpec((N_CHIPS, tb, LANES), lambda i: (0, i, 0))],
        out_specs=pl.BlockSpec((tb, LANES), lambda i: (i, 0)),
        out_shape=jax.ShapeDtypeStruct((hr, LANES), F32),
        compiler_params=_params(("arbitrary",)),
    )(parts)


def adamw(w, g, m, v, name):
    rows, width = w.shape
    tb = _pick(rows, 1600, 8)

    def body(w_ref, g_ref, m_ref, v_ref, d_ref, mo_ref, vo_ref):
        gg = g_ref[...]
        m_new = ADAM_B1 * m_ref[...] + (1.0 - ADAM_B1) * gg
        v_new = ADAM_B2 * v_ref[...] + (1.0 - ADAM_B2) * (gg * gg)
        m_hat = m_new / (1.0 - ADAM_B1 ** ADAM_STEP)
        v_hat = v_new / (1.0 - ADAM_B2 ** ADAM_STEP)
        d_ref[...] = -ADAM_LR * (m_hat / (jnp.sqrt(v_hat) + ADAM_EPS) + ADAM_WD * w_ref[...])
        mo_ref[...] = m_new
        vo_ref[...] = v_new

    spec = pl.BlockSpec((tb, width), lambda i: (i, 0))
    return pl.pallas_call(
        body, name=name, grid=(rows // tb,), in_specs=[spec] * 4, out_specs=[spec] * 3,
        out_shape=[jax.ShapeDtypeStruct((rows, width), F32)] * 3,
        compiler_params=_params(("arbitrary",)),
    )(w, g, m, v)


_BIG = (("w_in", 1), ("w_out", 0), ("w_gate_up", 1), ("w_down", 0))


def _flat_shards(full, axis):
    r, c = full.shape
    if axis == 0:
        return full.reshape(N_CHIPS, (r // N_CHIPS) * c)
    return full.reshape(r, N_CHIPS, c // N_CHIPS).transpose(1, 0, 2).reshape(N_CHIPS, r * (c // N_CHIPS))


def _unflat_shards(flat, shape, axis):
    r, c = shape
    if axis == 0:
        return flat.reshape(r, c)
    return flat.reshape(N_CHIPS, r, c // N_CHIPS).transpose(1, 0, 2).reshape(r, c)


def _pack_rows(vecs, width=1024):
    rows, cur, used = [], [], 0
    for v in vecs:
        n = v.shape[1]
        if used + n > width:
            cur.append(jnp.zeros((1, width - used), F32))
            rows.append(jnp.concatenate(cur, axis=1))
            cur, used = [], 0
        cur.append(v)
        used += n
    cur.append(jnp.zeros((1, width - used), F32))
    rows.append(jnp.concatenate(cur, axis=1))
    return rows


def _pad128(v):
    return jnp.pad(v, ((0, 0), (0, LANES - v.shape[1])))


def kernel(x, norm1_w, w_in, lb_logits, hg_norm_w, q_norm_w, k_norm_w, att_norm_w, w_out, norm2_w, w_gate_up, w_down, final_norm_w, loss_target, m_norm1_w, m_w_in, m_lb_logits, m_hg_norm_w, m_q_norm_w, m_k_norm_w, m_att_norm_w, m_w_out, m_norm2_w, m_w_gate_up, m_w_down, m_final_norm_w, v_norm1_w, v_w_in, v_lb_logits, v_hg_norm_w, v_q_norm_w, v_k_norm_w, v_att_norm_w, v_w_out, v_norm2_w, v_w_gate_up, v_w_down, v_final_norm_w):
    t, d = x.shape[1], x.shape[2]
    xi, yi, ci = _place()
    chip = 2 * xi + yi
    x2d = x.reshape(t, d)
    tgt = loss_target.reshape(t, d)
    big_w = {"w_in": w_in[0], "w_out": w_out[0], "w_gate_up": w_gate_up[0], "w_down": w_down[0]}
    big_m = {"w_in": m_w_in[0], "w_out": m_w_out[0], "w_gate_up": m_w_gate_up[0], "w_down": m_w_down[0]}
    big_v = {"w_in": v_w_in[0], "w_out": v_w_out[0], "w_gate_up": v_w_gate_up[0], "w_down": v_w_down[0]}
    sizes = [big_w[n].size for n, _ in _BIG]
    offs = np.concatenate([[0], np.cumsum(sizes)]).tolist()
    full_shape = {n: ((big_w[n].shape[0] * (N_CHIPS if ax == 0 else 1), big_w[n].shape[1] * (N_CHIPS if ax == 1 else 1)))
                  for n, ax in _BIG}

    def flat(tree, dtype):
        return jnp.concatenate([tree[n].astype(dtype).reshape(-1) for n, _ in _BIG]).reshape(-1, LANES)

    gathered = allgather_shards(flat(big_w, BF16)).reshape(N_CHIPS, -1)
    wfull = {n: _unflat_shards(gathered[:, offs[i]:offs[i + 1]], full_shape[n], ax) for i, (n, ax) in enumerate(_BIG)}
    lb_rows = lb_logits.reshape(4, LANES) * (ci == 0).astype(F32)
    lb_pad = lax.dynamic_update_slice(jnp.zeros((8, 1024), F32), lb_rows, (0, chip * LANES))
    lb_full = allreduce_small(lb_pad, "gather_lb")[:4, :HG_W]
    l_f0, l_f1, l_b0, l_b1 = (lb_full[i:i + 1] for i in range(4))

    n1 = norm1_w.reshape(1, d)
    n2 = norm2_w.reshape(1, d)
    nf = final_norm_w.reshape(1, d)
    (h1,) = _rows(lambda a, w: ((_rms(a, w),), ()), "norm1", [_full(x2d)], [n1], [(d, BF16)])
    u = mm_rows(h1, wfull["w_in"], "mm_in")
    o_f, ss_f = gla_fwd(u, l_f0, l_f1, 1, False, "gla_fwd_f")
    o_b, ss_b = gla_fwd(u, l_b0, l_b1, 2, True, "gla_fwd_b")

    cos2, sin2 = _rope_tables(t)
    cq, sq = jnp.tile(cos2, (1, ATT_HEADS)), jnp.tile(sin2, (1, ATT_HEADS))
    ck, sk = jnp.tile(cos2, (1, ATT_KV)), jnp.tile(sin2, (1, ATT_KV))
    qw, kw = _pad128(q_norm_w.reshape(1, ATT_DH)), _pad128(k_norm_w.reshape(1, ATT_DH))
    gq, gk = _group_sum_matrix(ATT_QW), _group_sum_matrix(ATT_KVW)
    tq, tk = _tile_matrix(ATT_QW), _tile_matrix(ATT_KVW)
    prep_rows = [(u, ATT_QW, 5), (u, ATT_KVW, 24), _full(cq), _full(sq), _full(ck), _full(sk)]
    prep_consts = [qw, kw, gq, gk, tq, tk]
    q_rot, k_rot, v_b = _rows(
        lambda aq, ak, av, *rest: ((*_att_prep_fn(aq, ak, *rest), av), ()), "att_prep",
        [prep_rows[0], prep_rows[1], (u, ATT_KVW, 25)] + prep_rows[2:], prep_consts,
        [(ATT_QW, BF16), (ATT_KVW, BF16), (ATT_KVW, BF16)], tm=256)
    q8, k2, v2 = _heads_major(q_rot, ATT_HEADS), _heads_major(k_rot, ATT_KV), _heads_major(v_b, ATT_KV)
    o8, lse = fa_fwd(q8, k2, v2)
    o_att = _tokens_major(o8, ATT_HEADS, t)

    hgw = hg_norm_w.reshape(1, HG_D)
    attw = att_norm_w.reshape(1, ATT_QW)
    mix_rows = [_full(o_f), _full(o_b), (u, HG_W, 4), _full(o_att)]
    (mix,) = _rows(lambda *a: ((_post_mix_fn(*a),), ()), "post_mix", mix_rows, [hgw, attw], [(d, BF16)], tm=256)
    x1 = mm_rows(mix, wfull["w_out"], "mm_out", res=x2d)
    (h2,) = _rows(lambda a, w: ((_rms(a, w),), ()), "norm2", [_full(x1)], [n2], [(d, BF16)])
    gu = mm_rows(h2, wfull["w_gate_up"], "mm_gate_up")
    dff = gu.shape[1] // 2
    (act,) = _rows(lambda g_, u_: ((_swiglu_fn(g_, u_),), ()), "swiglu", [(gu, dff, 0), (gu, dff, 1)], [],
                   [(dff, BF16)], tm=256)
    x2 = mm_rows(act, wfull["w_down"], "mm_down", res=x1)

    def loss_bwd_fn(a, tg, w):
        val, vjp = jax.vjp(lambda a_, w_: _loss_fn(a_, w_, tg), a, w)
        da, dw = vjp(jnp.ones((1, 1), F32))
        return (da, da), (dw, jnp.broadcast_to(val, (1, LANES)))

    dx2, dx2b, g_final, loss_part = _rows(loss_bwd_fn, "loss_bwd", [_full(x2), _full(tgt)], [nf],
                                          [(d, F32), (d, BF16)], [d, LANES], tm=256)
    dact = mm_rows(dx2b, wfull["w_down"], "mm_down_dx", trans_b=True)
    g_w_down = mm_tn(act, dx2b, "mm_down_dw")

    def swiglu_bwd_fn(g_, u_, da):
        _, vjp = jax.vjp(_swiglu_fn, g_, u_)
        dg, du_ = vjp(da)
        return (jnp.concatenate([dg, du_], axis=1),), ()

    (dgu,) = _rows(swiglu_bwd_fn, "swiglu_bwd", [(gu, dff, 0), (gu, dff, 1), _full(dact)], [],
                   [(2 * dff, BF16)], tm=256)
    dh2 = mm_rows(dgu, wfull["w_gate_up"], "mm_gate_up_dx", trans_b=True)
    g_w_gate_up = mm_tn(h2, dgu, "mm_gate_up_dw")

    def norm_bwd_fn(a, dh, dres, w):
        _, vjp = jax.vjp(_rms, a, w)
        da, dw = vjp(dh)
        da = da + dres
        return (da, da), (dw,)

    dx1, dx1b, g_norm2 = _rows(norm_bwd_fn, "norm2_bwd", [_full(x1), _full(dh2), _full(dx2)], [n2],
                               [(d, F32), (d, BF16)], [d], tm=256)
    dmix = mm_rows(dx1b, wfull["w_out"], "mm_out_dx", trans_b=True)
    g_w_out = mm_tn(mix, dx1b, "mm_out_dw")

    def post_mix_bwd_fn(of, ob, ug, oa, dm, hgw_, attw_, gq_):
        _, vjp = jax.vjp(_post_mix_fn, of, ob, ug, oa, hgw_, attw_)
        dof, _, dug, doa, dhgw, dattw = vjp(dm)
        return (dof, dug, doa, _xdot(doa * oa, gq_)), (dhgw, dattw)

    do_hg, du_g, do_att, dsum, g_hg, g_att = _rows(
        post_mix_bwd_fn, "post_mix_bwd", mix_rows + [_full(dmix)], [hgw, attw, gq],
        [(HG_W, F32), (HG_W, BF16), (ATT_QW, BF16), (ATT_QW, F32)], [HG_D, ATT_QW], tm=256)

    do8 = _heads_major(do_att, ATT_HEADS)
    d8 = _heads_major(dsum, ATT_HEADS)
    lq = do8.shape[1]
    dq8 = fa_dq(q8, k2, v2, do8, lse, d8)
    dk2, dv2 = fa_dkdv(q8, k2, v2, do8, lse[:, :, 0].reshape(ATT_KV, 1, lq), d8[:, :, 0].reshape(ATT_KV, 1, lq))
    dq_rot = _tokens_major(dq8, ATT_HEADS, t)
    dk_rot = _tokens_major(dk2, ATT_KV, t)
    dv_tok = _tokens_major(dv2, ATT_KV, t)

    def att_prep_bwd_fn(aq, ak, cq_, sq_, ck_, sk_, dq, dk, qw_, kw_, gq_, gk_, tq_, tk_):
        _, vjp = jax.vjp(lambda a, b, c_, e: _att_prep_fn(a, b, cq_, sq_, ck_, sk_, c_, e, gq_, gk_, tq_, tk_),
                         aq, ak, qw_, kw_)
        daq, dak, dqw, dkw = vjp((dq, dk))
        return (daq, dak), (dqw, dkw)

    da_q, da_k, g_q, g_k = _rows(att_prep_bwd_fn, "att_prep_bwd", prep_rows + [_full(dq_rot), _full(dk_rot)],
                                 prep_consts, [(ATT_QW, BF16), (ATT_KVW, BF16)], [LANES, LANES], tm=256)

    dq_f, df_f, di_f, dl_f0, dl_f1 = gla_bwd(u, l_f0, l_f1, ss_f, do_hg, 1, False, "gla_bwd_f")
    dq_b, df_b, di_b, dl_b0, dl_b1 = gla_bwd(u, l_b0, l_b1, ss_b, do_hg, 2, True, "gla_bwd_b")

    def assemble_fn(qf, qb, ff, fb, i_f, i_b, dg, aq, ak, av):
        parts = [qf + qb, ff, fb, i_f + i_b, dg.astype(F32), aq.astype(F32), ak.astype(F32), av]
        return (jnp.concatenate(parts, axis=1),), ()

    (du,) = _rows(assemble_fn, "assemble_du",
                  [_full(a) for a in (dq_f, dq_b, df_f, df_b, di_f, di_b, du_g, da_q, da_k, dv_tok)], [],
                  [(u.shape[1], BF16)], tm=256)
    dh1 = mm_rows(du, wfull["w_in"], "mm_in_dx", trans_b=True)
    g_w_in = mm_tn(h1, du, "mm_in_dw")
    grad_x, _, g_norm1 = _rows(norm_bwd_fn, "norm1_bwd", [_full(x2d), _full(dh1), _full(dx1)], [n1],
                               [(d, F32), (d, BF16)], [d], tm=256)

    gfull = {"w_in": g_w_in, "w_out": g_w_out, "w_gate_up": g_w_gate_up, "w_down": g_w_down}
    gflat = jnp.concatenate([_flat_shards(gfull[n], ax) for n, ax in _BIG], axis=1).reshape(N_CHIPS, -1, LANES)
    chip_part = add_my_half(gflat, rs_siblings(gflat))
    g_shard = share_halves(sum_chips(rs_chips(chip_part)))
    d_shard, m_shard, v_shard = adamw(flat(big_w, F32), g_shard, flat(big_m, F32), flat(big_v, F32), "adamw_big")

    def unflat(a):
        a = a.reshape(-1)
        return {n: a[offs[i]:offs[i + 1]].reshape((1,) + big_w[n].shape) for i, (n, _) in enumerate(_BIG)}

    g_big, d_big, m_big, v_big = unflat(g_shard), unflat(d_shard), unflat(m_shard), unflat(v_shard)

    small = [g_norm1, g_norm2, g_final, g_att, g_hg, g_q, g_k, loss_part, dl_f0, dl_f1, dl_b0, dl_b1]
    packed = _pack_rows(small)
    packed += [jnp.zeros((1, 1024), F32)] * (8 - len(packed))
    tot = allreduce_small(jnp.concatenate(packed, axis=0), "allreduce_small")
    s_norm1, s_norm2, s_final = tot[0:1], tot[1:2], tot[2:3]
    s_att, s_hg, s_q, s_k = tot[3:4, 0:512], tot[3:4, 512:640], tot[3:4, 640:704], tot[3:4, 768:832]
    loss = tot[3, 896]
    s_lb = jnp.concatenate([tot[4:5, 0:512], tot[4:5, 512:1024], tot[5:6, 0:512], tot[5:6, 512:1024]], axis=0)
    s_lb = lax.dynamic_slice(s_lb, (0, chip * LANES), (4, LANES)).reshape(1, 512)

    names = ["norm1_w", "lb_logits", "hg_norm_w", "q_norm_w", "k_norm_w", "att_norm_w", "norm2_w", "final_norm_w"]
    g_small = dict(zip(names, [s_norm1, s_lb, s_hg, s_q, s_k, s_att, s_norm2, s_final]))
    w_small = dict(zip(names, [norm1_w, lb_logits, hg_norm_w, q_norm_w, k_norm_w, att_norm_w, norm2_w, final_norm_w]))
    m_small = dict(zip(names, [m_norm1_w, m_lb_logits, m_hg_norm_w, m_q_norm_w, m_k_norm_w, m_att_norm_w, m_norm2_w, m_final_norm_w]))
    v_small = dict(zip(names, [v_norm1_w, v_lb_logits, v_hg_norm_w, v_q_norm_w, v_k_norm_w, v_att_norm_w, v_norm2_w, v_final_norm_w]))

    def pack_small(tree):
        rows = _pack_rows([tree[n].reshape(1, -1) for n in names])
        rows += [jnp.zeros((1, 1024), F32)] * (8 - len(rows))
        return jnp.concatenate(rows, axis=0)

    d_s, m_s, v_s = adamw(pack_small(w_small), pack_small(g_small), pack_small(m_small), pack_small(v_small), "adamw_small")

    def unpack_small(a):
        out, r, used = {}, 0, 0
        for n in names:
            size = w_small[n].size
            if used + size > 1024:
                r, used = r + 1, 0
            out[n] = a[r, used:used + size].reshape(w_small[n].shape)
            used += size
        return out

    d_sm, m_sm, v_sm = unpack_small(d_s), unpack_small(m_s), unpack_small(v_s)
    g_sm = {n: g_small[n].reshape(w_small[n].shape) for n in names}

    order = ["norm1_w", "w_in", "lb_logits", "hg_norm_w", "q_norm_w", "k_norm_w", "att_norm_w", "w_out", "norm2_w",
             "w_gate_up", "w_down", "final_norm_w"]

    def pick(small_tree, big_tree):
        return [big_tree[n] if n in big_tree else small_tree[n] for n in order]

    return (loss, grad_x.reshape(x.shape), *pick(g_sm, g_big), *pick(d_sm, d_big), *pick(m_sm, m_big),
            *pick(v_sm, v_big))
```

```python
import functools

import numpy as np
import jax
import jax.numpy as jnp
from jax import lax
from jax.experimental import pallas as pl
from jax.experimental.pallas import tpu as pltpu

F32 = jnp.float32
BF16 = jnp.bfloat16
MESH = pl.DeviceIdType.MESH
HIGHEST = lax.Precision.HIGHEST

EPS = 1e-6
GRID_W = 64
HG_HEADS = 4
HG_D = 128
HG_W = HG_HEADS * HG_D
CHUNK = 64
ATT_HEADS = 8
ATT_KV = 2
ATT_DH = 64
ATT_QW = ATT_HEADS * ATT_DH
ATT_KVW = ATT_KV * ATT_DH
ROPE_THETA = 10000.0
N_CHIPS = 4
N_DEV = 8

ADAM_LR = 0.001
ADAM_B1 = 0.9
ADAM_B2 = 0.999
ADAM_EPS = 1e-08
ADAM_WD = 0.01
ADAM_STEP = 10

VMEM_LIMIT = 52 * 1024 * 1024
LANES = 128


def _params(sem=None):
    return pltpu.CompilerParams(dimension_semantics=sem, vmem_limit_bytes=VMEM_LIMIT)


def _dg(a, b, ca, cb):
    return lax.dot_general(a.astype(BF16), b.astype(BF16), (((ca,), (cb,)), ((), ())),
                           preferred_element_type=F32)


@jax.custom_vjp
def dot_nn(a, b):
    return _dg(a, b, 1, 0)


def _dot_nn_fwd(a, b):
    return _dg(a, b, 1, 0), (a, b)


def _dot_nn_bwd(res, g):
    a, b = res
    return _dg(g, b, 1, 1), _dg(a, g, 0, 0)


dot_nn.defvjp(_dot_nn_fwd, _dot_nn_bwd)


@jax.custom_vjp
def dot_nt(a, b):
    return _dg(a, b, 1, 1)


def _dot_nt_fwd(a, b):
    return _dg(a, b, 1, 1), (a, b)


def _dot_nt_bwd(res, g):
    a, b = res
    return _dg(g, b, 1, 0), _dg(g, a, 0, 0)


dot_nt.defvjp(_dot_nt_fwd, _dot_nt_bwd)


@jax.custom_vjp
def dot_tn(a, b):
    return _dg(a, b, 0, 0)


def _dot_tn_fwd(a, b):
    return _dg(a, b, 0, 0), (a, b)


def _dot_tn_bwd(res, g):
    a, b = res
    return _dg(b, g, 1, 1), _dg(a, g, 1, 0)


dot_tn.defvjp(_dot_tn_fwd, _dot_tn_bwd)


def _xdot(a, m):
    return jnp.dot(a, m, precision=HIGHEST, preferred_element_type=F32)


@jax.custom_vjp
def swap_pairs(y):
    n = y.shape[-1]
    lane = lax.broadcasted_iota(jnp.int32, y.shape, 1)
    nxt = pltpu.roll(y, n - 1, 1)
    prv = pltpu.roll(y, 1, 1)
    return jnp.where(lane % 2 == 0, nxt, prv)


def _swap_fwd(y):
    return swap_pairs(y), None


def _swap_bwd(_, g):
    return (swap_pairs(g),)


swap_pairs.defvjp(_swap_fwd, _swap_bwd)


def _rms(x, w):
    return x * lax.rsqrt(jnp.mean(x * x, axis=-1, keepdims=True) + EPS) * w


def _sigmoid(x):
    return jax.nn.sigmoid(x)


def _rows(fn, name, rows, consts, outs, accs=(), tm=512):
    t = rows[0][0].shape[0]
    tm = min(tm, t)
    n_r, n_c, n_o, n_a = len(rows), len(consts), len(outs), len(accs)

    def body(*refs):
        r = refs[:n_r]
        c = refs[n_r:n_r + n_c]
        o = refs[n_r + n_c:n_r + n_c + n_o]
        a = refs[n_r + n_c + n_o:]
        ro, ao = fn(*[x[...] for x in r], *[x[...] for x in c])
        for ref, val in zip(o, ro):
            ref[...] = val.astype(ref.dtype)
        if n_a:
            @pl.when(pl.program_id(0) == 0)
            def _():
                for ref in a:
                    ref[...] = jnp.zeros(ref.shape, F32)
            for ref, val in zip(a, ao):
                ref[...] += val

    in_specs = [pl.BlockSpec((tm, w), lambda i, cb=cb: (i, cb)) for _, w, cb in rows]
    in_specs += [pl.BlockSpec(c.shape, lambda i, nd=c.ndim: (0,) * nd) for c in consts]
    out_specs = [pl.BlockSpec((tm, w), lambda i: (i, 0)) for w, _ in outs]
    out_specs += [pl.BlockSpec((1, w), lambda i: (0, 0)) for w in accs]
    out_shape = [jax.ShapeDtypeStruct((t, w), dt) for w, dt in outs]
    out_shape += [jax.ShapeDtypeStruct((1, w), F32) for w in accs]
    res = pl.pallas_call(
        body, name=name, grid=(t // tm,), in_specs=in_specs, out_specs=out_specs, out_shape=out_shape,
        compiler_params=_params(("arbitrary",)),
    )(*[a for a, _, _ in rows], *consts)
    return res


def _full(a):
    return (a, a.shape[1], 0)


def _pick(n, cap, mult):
    best = None
    for d in range(mult, min(n, cap) + 1, mult):
        if n % d == 0:
            best = d
    return best if best is not None else n


def mm_rows(a, b, name, trans_b=False, res=None, out_dtype=F32):
    m, k = a.shape
    n = b.shape[0] if trans_b else b.shape[1]
    tn = _pick(n, 3328, LANES)
    tm = _pick(m, 512 if k <= 3072 else 256, 8)
    has_res = res is not None

    def body(*refs):
        if has_res:
            a_ref, b_ref, r_ref, o_ref = refs
        else:
            a_ref, b_ref, o_ref = refs
        acc = _dg(a_ref[...], b_ref[...], 1, 1 if trans_b else 0)
        if has_res:
            acc = acc + r_ref[...]
        o_ref[...] = acc.astype(o_ref.dtype)

    in_specs = [pl.BlockSpec((tm, k), lambda j, i: (i, 0))]
    if trans_b:
        in_specs.append(pl.BlockSpec((tn, k), lambda j, i: (j, 0)))
    else:
        in_specs.append(pl.BlockSpec((k, tn), lambda j, i: (0, j)))
    args = [a, b]
    if has_res:
        in_specs.append(pl.BlockSpec((tm, tn), lambda j, i: (i, j)))
        args.append(res)
    return pl.pallas_call(
        body, name=name, grid=(n // tn, m // tm), in_specs=in_specs,
        out_specs=pl.BlockSpec((tm, tn), lambda j, i: (i, j)),
        out_shape=jax.ShapeDtypeStruct((m, n), out_dtype),
        compiler_params=_params(("arbitrary", "arbitrary")),
    )(*args)


def mm_tn(a, b, name):
    t, m = a.shape
    n = b.shape[1]
    tm = _pick(m, 512, LANES)
    tn = _pick(n, 3328, LANES)
    tk = _pick(t, 1024, 8)

    def body(a_ref, b_ref, o_ref):
        @pl.when(pl.program_id(2) == 0)
        def _():
            o_ref[...] = jnp.zeros(o_ref.shape, F32)
        o_ref[...] += _dg(a_ref[...], b_ref[...], 0, 0)

    return pl.pallas_call(
        body, name=name, grid=(m // tm, n // tn, t // tk),
        in_specs=[pl.BlockSpec((tk, tm), lambda i, j, kk: (kk, i)),
                  pl.BlockSpec((tk, tn), lambda i, j, kk: (kk, j))],
        out_specs=pl.BlockSpec((tm, tn), lambda i, j, kk: (i, j)),
        out_shape=jax.ShapeDtypeStruct((m, n), F32),
        compiler_params=_params(("arbitrary", "arbitrary", "arbitrary")),
    )(a, b)


def _lower_bound(l0, l1):
    m = jnp.maximum(l0, l1)
    e0 = jnp.exp(l0 - m)
    e1 = jnp.exp(l1 - m)
    return e0 / (e0 + e1)


def _gla_consts(rev):
    ri = lax.broadcasted_iota(jnp.int32, (CHUNK, CHUNK), 0)
    ci = lax.broadcasted_iota(jnp.int32, (CHUNK, CHUNK), 1)
    keep = (ci >= ri) if rev else (ci <= ri)
    tm = keep.astype(F32)
    if rev:
        rvec = (ci >= CHUNK // 2).astype(F32)
    else:
        rvec = (ci <= CHUNK // 2 - 1).astype(F32)
    m3 = jnp.concatenate([tm, tm - rvec, 1.0 - tm], axis=0)
    return keep, m3


def _gla_chunk(uq, uf, ui, l0, l1, st, keep, m3):
    lb = _lower_bound(l0, l1)
    q = uq * _sigmoid(uq)
    f = lb + (1.0 - lb) * _sigmoid(uf)
    k = (1.0 - lb) * _sigmoid(-uf)
    g = jnp.log(f)
    b3 = _xdot(m3, g)
    b = b3[0:CHUNK]
    bmr = b3[CHUNK:2 * CHUNK]
    lmb = b3[2 * CHUNK:3 * CHUNK]
    blast = jnp.sum(g, axis=0, keepdims=True)
    scores = dot_nt(q * jnp.exp(bmr), k * jnp.exp(-bmr))
    scores = jnp.where(keep, scores, 0.0)
    o = dot_nn(scores, ui) + dot_nt(q * jnp.exp(b), st)
    st_new = st * jnp.exp(blast) + dot_tn(ui, k * jnp.exp(lmb))
    return o, st_new


def _gla_blocks(t):
    tb = min(512, t)
    return tb, tb // CHUNK, t // tb


def gla_fwd(u, l0, l1, fcol, rev, name):
    t = u.shape[0]
    tb, ncb, nb = _gla_blocks(t)

    def blk(i):
        return (nb - 1 - i) if rev else i

    def body(uq_ref, uf_ref, ui_ref, l0_ref, l1_ref, o_ref, ss_ref, st_ref):
        @pl.when(pl.program_id(0) == 0)
        def _():
            st_ref[...] = jnp.zeros(st_ref.shape, F32)
        keep, m3 = _gla_consts(rev)

        def chunk(ci, carry):
            c = (ncb - 1 - ci) if rev else ci
            r0 = pl.multiple_of(c * CHUNK, CHUNK)
            for h in range(HG_HEADS):
                sl = slice(h * HG_D, (h + 1) * HG_D)
                st = st_ref[h]
                ss_ref[c, h] = st
                o, st_new = _gla_chunk(uq_ref[pl.ds(r0, CHUNK), sl], uf_ref[pl.ds(r0, CHUNK), sl],
                                       ui_ref[pl.ds(r0, CHUNK), sl], l0_ref[:, sl], l1_ref[:, sl], st, keep, m3)
                o_ref[pl.ds(r0, CHUNK), sl] = o
                st_ref[h] = st_new
            return carry

        lax.fori_loop(0, ncb, chunk, 0)

    row = lambda cb: pl.BlockSpec((tb, HG_W), lambda i: (blk(i), cb))
    vec = pl.BlockSpec((1, HG_W), lambda i: (0, 0))
    return pl.pallas_call(
        body, name=name, grid=(nb,),
        in_specs=[row(0), row(fcol), row(3), vec, vec],
        out_specs=[pl.BlockSpec((tb, HG_W), lambda i: (blk(i), 0)),
                   pl.BlockSpec((ncb, HG_HEADS, HG_D, HG_D), lambda i: (blk(i), 0, 0, 0))],
        out_shape=[jax.ShapeDtypeStruct((t, HG_W), F32),
                   jax.ShapeDtypeStruct((t // CHUNK, HG_HEADS, HG_D, HG_D), F32)],
        scratch_shapes=[pltpu.VMEM((HG_HEADS, HG_D, HG_D), F32)],
        compiler_params=_params(("arbitrary",)),
    )(u, u, u, l0, l1)


def gla_bwd(u, l0, l1, ss, do, fcol, rev, name):
    t = u.shape[0]
    tb, ncb, nb = _gla_blocks(t)

    def blk(i):
        return i if rev else (nb - 1 - i)

    def body(uq_ref, uf_ref, ui_ref, l0_ref, l1_ref, ss_ref, do_ref,
             dq_ref, df_ref, di_ref, dl0_ref, dl1_ref, dst_ref):
        @pl.when(pl.program_id(0) == 0)
        def _():
            dst_ref[...] = jnp.zeros(dst_ref.shape, F32)
            dl0_ref[...] = jnp.zeros(dl0_ref.shape, F32)
            dl1_ref[...] = jnp.zeros(dl1_ref.shape, F32)
        keep, m3 = _gla_consts(rev)

        def chunk(ci, carry):
            c = ci if rev else (ncb - 1 - ci)
            r0 = pl.multiple_of(c * CHUNK, CHUNK)
            for h in range(HG_HEADS):
                sl = slice(h * HG_D, (h + 1) * HG_D)
                fn = functools.partial(_gla_chunk, keep=keep, m3=m3)
                _, vjp = jax.vjp(fn, uq_ref[pl.ds(r0, CHUNK), sl], uf_ref[pl.ds(r0, CHUNK), sl],
                                 ui_ref[pl.ds(r0, CHUNK), sl], l0_ref[:, sl], l1_ref[:, sl], ss_ref[c, h])
                dq, df, di, dl0, dl1, dst = vjp((do_ref[pl.ds(r0, CHUNK), sl], dst_ref[h]))
                dq_ref[pl.ds(r0, CHUNK), sl] = dq
                df_ref[pl.ds(r0, CHUNK), sl] = df
                di_ref[pl.ds(r0, CHUNK), sl] = di
                dl0_ref[:, sl] += dl0
                dl1_ref[:, sl] += dl1
                dst_ref[h] = dst
            return carry

        lax.fori_loop(0, ncb, chunk, 0)

    row = lambda cb: pl.BlockSpec((tb, HG_W), lambda i: (blk(i), cb))
    vec = pl.BlockSpec((1, HG_W), lambda i: (0, 0))
    orow = pl.BlockSpec((tb, HG_W), lambda i: (blk(i), 0))
    return pl.pallas_call(
        body, name=name, grid=(nb,),
        in_specs=[row(0), row(fcol), row(3), vec, vec,
                  pl.BlockSpec((ncb, HG_HEADS, HG_D, HG_D), lambda i: (blk(i), 0, 0, 0)), orow],
        out_specs=[orow, orow, orow, vec, vec],
        out_shape=[jax.ShapeDtypeStruct((t, HG_W), F32)] * 3 + [jax.ShapeDtypeStruct((1, HG_W), F32)] * 2,
        scratch_shapes=[pltpu.VMEM((HG_HEADS, HG_D, HG_D), F32)],
        compiler_params=_params(("arbitrary",)),
    )(u, u, u, l0, l1, ss, do)


def _rope_tables(t):
    rows = t // GRID_W
    row = jnp.repeat(jnp.arange(rows), GRID_W).astype(F32)
    col = jnp.tile(jnp.arange(GRID_W), rows).astype(F32)
    axis_dim = ATT_DH // 2
    freqs = ROPE_THETA ** (-jnp.arange(0, axis_dim, 2, dtype=F32) / axis_dim)
    ang = jnp.concatenate([row[:, None] * freqs, col[:, None] * freqs], axis=-1)
    cos2 = jnp.repeat(jnp.cos(ang), 2, axis=-1)
    sin2 = jnp.repeat(jnp.sin(ang), 2, axis=-1) * jnp.tile(jnp.array([-1.0, 1.0], F32), ATT_DH // 2)
    return cos2, sin2


def _group_sum_matrix(width):
    idx = np.arange(width) // ATT_DH
    return jnp.asarray((idx[:, None] == idx[None, :]).astype(np.float32))


def _tile_matrix(width):
    m = np.zeros((LANES, width), np.float32)
    m[np.arange(width) % ATT_DH, np.arange(width)] = 1.0
    return jnp.asarray(m)


def _tile_w(w128, tile_m):
    w8 = jnp.broadcast_to(w128, (8, LANES))
    return jnp.sum(_xdot(w8, tile_m), axis=0, keepdims=True) * 0.125


def _head_norm_rope(a, w128, cos_t, sin_t, gsum, tile_m, scale):
    ssq = _xdot(a * a, gsum)
    y = a * lax.rsqrt(ssq * (1.0 / ATT_DH) + EPS) * _tile_w(w128, tile_m)
    return (y * cos_t + swap_pairs(y) * sin_t) * scale


def _att_prep_fn(aq, ak, cq, sq, ck, sk, qw, kw, gq, gk, tq, tk):
    q = _head_norm_rope(aq, qw, cq, sq, gq, tq, ATT_DH ** -0.5)
    k = _head_norm_rope(ak, kw, ck, sk, gk, tk, 1.0)
    return q, k


FA_Q = 256
FA_K_FWD = 1024
FA_K_BWD = 512
FA_STRIPS = 4


def _strips(a):
    g, l, dh = a.shape
    nq = min(FA_Q, l)
    return a.reshape(g, l // nq, nq, dh)


def _strips_t(a):
    return _strips(a).transpose(0, 1, 3, 2)


def _blocks_t(a, bk):
    g, l, dh = a.shape
    bk = min(bk, l)
    return a.reshape(g, l // bk, bk, dh).transpose(0, 1, 3, 2)


def fa_fwd(qt, k, vt):
    _, ns, dh, nq = qt.shape
    lk = k.shape[1]
    bk = vt.shape[3]
    nk = lk // bk
    spg = min(FA_STRIPS, ns)

    def body(q_ref, k_ref, v_ref, o_ref, lse_ref):
        qs = [q_ref[0, c] for c in range(spg)]

        def keys(j):
            return k_ref[0, pl.ds(pl.multiple_of(j * bk, bk), bk), :]

        def step(j, carry):
            st0, stats = carry
            kb = keys(j)
            vb = v_ref[0, j]
            sts = [st0] + [_dg(kb, qs[c], 1, 0) for c in range(1, spg)]
            out = []
            for c in range(spg):
                m, l, acc = stats[c]
                m_new = jnp.maximum(m, jnp.max(sts[c], axis=0, keepdims=True))
                alpha = jnp.exp(m - m_new)
                p = jnp.exp(sts[c] - m_new)
                l = alpha * l + jnp.sum(p, axis=0, keepdims=True)
                if c == spg - 1:
                    st0 = _dg(keys(jnp.minimum(j + 1, nk - 1)), qs[0], 1, 0)
                acc = alpha * acc + _dg(vb, p, 1, 0)
                out.append((m_new, l, acc))
            return st0, tuple(out)

        init = tuple((jnp.full((1, nq), -jnp.inf, F32), jnp.zeros((1, nq), F32), jnp.zeros((dh, nq), F32))
                     for _ in range(spg))
        _, res = lax.fori_loop(0, nk, step, (_dg(keys(0), qs[0], 1, 0), init))
        for c in range(spg):
            m, l, acc = res[c]
            o_ref[0, c] = acc / l
            lse_ref[0, c] = m + jnp.log(l)

    return pl.pallas_call(
        body, name="fa_fwd", grid=(ATT_KV, ns // spg),
        in_specs=[pl.BlockSpec((1, spg, dh, nq), lambda g, i: (g, i, 0, 0)),
                  pl.BlockSpec((1, lk, dh), lambda g, i: (g, 0, 0)),
                  pl.BlockSpec((1, nk, dh, bk), lambda g, i: (g, 0, 0, 0))],
        out_specs=[pl.BlockSpec((1, spg, dh, nq), lambda g, i: (g, i, 0, 0)),
                   pl.BlockSpec((1, spg, 1, nq), lambda g, i: (g, i, 0, 0))],
        out_shape=[jax.ShapeDtypeStruct((ATT_KV, ns, dh, nq), F32), jax.ShapeDtypeStruct((ATT_KV, ns, 1, nq), F32)],
        compiler_params=_params(("arbitrary", "arbitrary")),
    )(qt, k, vt)


def _unblocks_t(a):
    g, nb, dh, bk = a.shape
    return a.transpose(0, 1, 3, 2).reshape(g, nb * bk, dh)


def fa_bwd(qs, qt, dos, dot_, lse, dsum, k, kt, vt):
    _, ns, nq, dh = qs.shape
    lk = k.shape[1]
    bk = kt.shape[3]
    nk = lk // bk
    spg = min(FA_STRIPS, ns)
    nc = bk // LANES

    def body(qs_ref, qt_ref, dos_ref, dot_ref, lse_ref, d_ref, k_ref, kt_ref, vt_ref, dq_ref, dk_ref, dv_ref):
        @pl.when(pl.program_id(1) == 0)
        def _():
            dk_ref[...] = jnp.zeros(dk_ref.shape, F32)
            dv_ref[...] = jnp.zeros(dv_ref.shape, F32)

        strips = range(spg)
        lse_b = [jnp.broadcast_to(lse_ref[0, c], (nq, LANES)) for c in strips]
        d_b = [jnp.broadcast_to(d_ref[0, c], (nq, LANES)) for c in strips]

        def step(j, dqs):
            ktb, vtb = kt_ref[0, j], vt_ref[0, j]
            kb = k_ref[0, pl.ds(pl.multiple_of(j * bk, bk), bk), :]
            prods = [(_dg(qs_ref[0, c], ktb, 1, 0), _dg(dos_ref[0, c], vtb, 1, 0)) for c in strips]
            out = []
            for c in strips:
                s, dp = prods[c]
                ps, dss = [], []
                for cc in range(nc):
                    sl = slice(cc * LANES, (cc + 1) * LANES)
                    pc = jnp.exp(s[:, sl] - lse_b[c])
                    ps.append(pc.astype(BF16))
                    dss.append((pc * (dp[:, sl] - d_b[c])).astype(BF16))
                p, ds = jnp.concatenate(ps, axis=1), jnp.concatenate(dss, axis=1)
                dv_ref[0, j] += _dg(dot_ref[0, c], p, 1, 0)
                dk_ref[0, j] += _dg(qt_ref[0, c], ds, 1, 0)
                out.append(dqs[c] + _dg(ds, kb, 1, 0))
            return tuple(out)

        dqs = lax.fori_loop(0, nk, step, tuple(jnp.zeros((nq, dh), F32) for _ in strips))
        for c in strips:
            dq_ref[0, c] = dqs[c]

    sspec = pl.BlockSpec((1, spg, nq, dh), lambda g, i: (g, i, 0, 0))
    tspec = pl.BlockSpec((1, spg, dh, nq), lambda g, i: (g, i, 0, 0))
    cspec = pl.BlockSpec((1, spg, nq, 1), lambda g, i: (g, i, 0, 0))
    bspec = pl.BlockSpec((1, nk, dh, bk), lambda g, i: (g, 0, 0, 0))
    return pl.pallas_call(
        body, name="fa_bwd", grid=(ATT_KV, ns // spg),
        in_specs=[sspec, tspec, sspec, tspec, cspec, cspec, pl.BlockSpec((1, lk, dh), lambda g, i: (g, 0, 0)),
                  bspec, bspec],
        out_specs=[sspec, bspec, bspec],
        out_shape=[jax.ShapeDtypeStruct((ATT_KV, ns, nq, dh), F32), jax.ShapeDtypeStruct((ATT_KV, nk, dh, bk), F32),
                   jax.ShapeDtypeStruct((ATT_KV, nk, dh, bk), F32)],
        compiler_params=_params(("arbitrary", "arbitrary")),
    )(qs, qt, dos, dot_, lse, dsum, k, kt, vt)


def _heads_major(a, heads):
    t = a.shape[0]
    return a.reshape(t, heads, ATT_DH).transpose(1, 0, 2).reshape(ATT_KV, (heads // ATT_KV) * t, ATT_DH)


def _tokens_major(a, heads, t):
    return a.reshape(heads, t, ATT_DH).transpose(1, 0, 2).reshape(t, heads * ATT_DH)


def _post_mix_fn(of, ob, ug, oa, hgw, attw):
    o = of + ob
    parts = []
    for h in range(HG_HEADS):
        parts.append(_rms(o[:, h * HG_D:(h + 1) * HG_D], hgw))
    hg = jnp.concatenate(parts, axis=1) * (ug * _sigmoid(ug))
    return jnp.concatenate([hg, _rms(oa, attw)], axis=1)


def _swiglu_fn(gate, up):
    return gate * _sigmoid(gate) * up


def _loss_fn(x2, w, tgt):
    e = _rms(x2, w) - tgt
    return 0.5 * jnp.sum(jnp.mean(e * e, axis=-1, keepdims=True), axis=0, keepdims=True)


def _place():
    return lax.axis_index("x"), lax.axis_index("y"), lax.axis_index("c")


def _other_chips(x, y):
    return [(1 - x, y), (x, 1 - y), (1 - x, 1 - y)]


def allgather_shards(shard):
    r = shard.shape[0]
    hr = r // 2

    def body(src, out, ssem, rsem, lsem):
        x, y, c = _place()
        k = 2 * x + y
        sib = (x, y, 1 - c)
        chips = _other_chips(x, y)

        def half(kk, cc):
            return out.at[kk, pl.ds(cc * hr, hr), :]

        def copy(j, src_ref, dst_ref, to):
            return pltpu.make_async_remote_copy(src_ref=src_ref, dst_ref=dst_ref, send_sem=ssem.at[j],
                                                recv_sem=rsem.at[j], device_id=to, device_id_type=MESH)

        mine = pltpu.make_async_copy(src, out.at[k], lsem)
        mine.start()
        my_half = src.at[pl.ds(c * hr, hr), :]
        first = [copy(j, my_half, half(k, c), (px, py, c)) for j, (px, py) in enumerate(chips)]
        for cp in first:
            cp.start()
        passed = [copy(3 + j, half(2 * px + py, c), half(2 * px + py, c), sib) for j, (px, py) in enumerate(chips)]
        for j, (px, py) in enumerate(chips):
            copy(j, my_half, half(2 * px + py, c), (px, py, c)).wait_recv()
            passed[j].start()
        for j, (px, py) in enumerate(chips):
            copy(3 + j, my_half, half(2 * px + py, 1 - c), sib).wait_recv()
        for cp in first + passed:
            cp.wait_send()
        mine.wait()

    return pl.pallas_call(
        body, name="allgather_shards",
        in_specs=[pl.BlockSpec(memory_space=pl.ANY)],
        out_specs=pl.BlockSpec(memory_space=pl.ANY),
        out_shape=jax.ShapeDtypeStruct((N_CHIPS, r, LANES), shard.dtype),
        scratch_shapes=[pltpu.SemaphoreType.DMA((6,)), pltpu.SemaphoreType.DMA((6,)), pltpu.SemaphoreType.DMA],
    )(shard)


def rs_siblings(g):
    _, r, _ = g.shape
    hr = r // 2

    def body(src, out, ssem, rsem):
        x, y, c = _place()
        cp = pltpu.make_async_remote_copy(src_ref=src.at[:, pl.ds((1 - c) * hr, hr), :], dst_ref=out,
                                          send_sem=ssem, recv_sem=rsem, device_id=(x, y, 1 - c), device_id_type=MESH)
        cp.start()
        cp.wait()

    return pl.pallas_call(
        body, name="rs_siblings",
        in_specs=[pl.BlockSpec(memory_space=pl.ANY)],
        out_specs=pl.BlockSpec(memory_space=pl.ANY),
        out_shape=jax.ShapeDtypeStruct((N_CHIPS, hr, LANES), F32),
        scratch_shapes=[pltpu.SemaphoreType.DMA, pltpu.SemaphoreType.DMA],
    )(g)


def rs_chips(pa):
    def body(src, out, ssem, rsem, lsem):
        x, y, c = _place()
        k = 2 * x + y
        chips = _other_chips(x, y)
        mine = pltpu.make_async_copy(src.at[k], out.at[k], lsem)
        mine.start()
        sends = []
        for j, (px, py) in enumerate(chips):
            cp = pltpu.make_async_remote_copy(src_ref=src.at[2 * px + py], dst_ref=out.at[k], send_sem=ssem.at[j],
                                              recv_sem=rsem.at[j], device_id=(px, py, c), device_id_type=MESH)
            cp.start()
            sends.append(cp)
        for j, (px, py) in enumerate(chips):
            pltpu.make_async_remote_copy(src_ref=src.at[k], dst_ref=out.at[2 * px + py], send_sem=ssem.at[j],
                                         recv_sem=rsem.at[j], device_id=(px, py, c), device_id_type=MESH).wait_recv()
        for cp in sends:
            cp.wait_send()
        mine.wait()

    return pl.pallas_call(
        body, name="rs_chips",
        in_specs=[pl.BlockSpec(memory_space=pl.ANY)],
        out_specs=pl.BlockSpec(memory_space=pl.ANY),
        out_shape=jax.ShapeDtypeStruct(pa.shape, F32),
        scratch_shapes=[pltpu.SemaphoreType.DMA((3,)), pltpu.SemaphoreType.DMA((3,)), pltpu.SemaphoreType.DMA],
    )(pa)


def share_halves(rh):
    hr = rh.shape[0]

    def body(src, out, ssem, rsem, lsem):
        x, y, c = _place()
        dst = out.at[pl.ds(c * hr, hr), :]
        mine = pltpu.make_async_copy(src, dst, lsem)
        mine.start()
        cp = pltpu.make_async_remote_copy(src_ref=src, dst_ref=dst, send_sem=ssem, recv_sem=rsem,
                                          device_id=(x, y, 1 - c), device_id_type=MESH)
        cp.start()
        pltpu.make_async_remote_copy(src_ref=src, dst_ref=out.at[pl.ds((1 - c) * hr, hr), :], send_sem=ssem,
                                     recv_sem=rsem, device_id=(x, y, 1 - c), device_id_type=MESH).wait_recv()
        cp.wait_send()
        mine.wait()

    return pl.pallas_call(
        body, name="share_halves",
        in_specs=[pl.BlockSpec(memory_space=pl.ANY)],
        out_specs=pl.BlockSpec(memory_space=pl.ANY),
        out_shape=jax.ShapeDtypeStruct((2 * hr, LANES), F32),
        scratch_shapes=[pltpu.SemaphoreType.DMA, pltpu.SemaphoreType.DMA, pltpu.SemaphoreType.DMA],
    )(rh)


def allreduce_small(p, name):
    rows, width = p.shape

    def body(p_ref, s_ref, gath, ssem, rsem):
        x, y, c = _place()
        me = 4 * x + 2 * y + c
        copies = []
        for d in range(1, N_DEV):
            dx, dy, dc = (d >> 2) & 1, (d >> 1) & 1, d & 1
            peer = (x ^ dx, y ^ dy, c ^ dc)
            cp = pltpu.make_async_remote_copy(src_ref=p_ref, dst_ref=gath.at[me], send_sem=ssem.at[d - 1],
                                              recv_sem=rsem.at[d - 1], device_id=peer, device_id_type=MESH)
            cp.start()
            copies.append(cp)
        gath[me] = p_ref[...]
        for d, cp in enumerate(copies, start=1):
            dx, dy, dc = (d >> 2) & 1, (d >> 1) & 1, d & 1
            peer_slot = 4 * (x ^ dx) + 2 * (y ^ dy) + (c ^ dc)
            pltpu.make_async_remote_copy(src_ref=p_ref, dst_ref=gath.at[peer_slot], send_sem=ssem.at[d - 1],
                                         recv_sem=rsem.at[d - 1], device_id=(x ^ dx, y ^ dy, c ^ dc),
                                         device_id_type=MESH).wait_recv()
        for cp in copies:
            cp.wait_send()
        acc = gath[0]
        for d in range(1, N_DEV):
            acc = acc + gath[d]
        s_ref[...] = acc

    return pl.pallas_call(
        body, name=name,
        in_specs=[pl.BlockSpec(memory_space=pltpu.VMEM)],
        out_specs=pl.BlockSpec(memory_space=pltpu.VMEM),
        out_shape=jax.ShapeDtypeStruct((rows, width), F32),
        scratch_shapes=[pltpu.VMEM((N_DEV, rows, width), F32), pltpu.SemaphoreType.DMA((N_DEV - 1,)),
                        pltpu.SemaphoreType.DMA((N_DEV - 1,))],
    )(p)


def add_my_half(g, recv):
    _, r, _ = g.shape
    hr = r // 2
    tb = _pick(hr, 1600, 8)
    nb = hr // tb
    c_arr = lax.axis_index("c").astype(jnp.int32).reshape(1)

    def body(c_ref, g_ref, r_ref, o_ref):
        o_ref[...] = g_ref[...] + r_ref[...]

    return pl.pallas_call(
        body, name="add_my_half",
        grid_spec=pltpu.PrefetchScalarGridSpec(
            num_scalar_prefetch=1, grid=(N_CHIPS, nb),
            in_specs=[pl.BlockSpec((1, tb, LANES), lambda k, i, c_ref: (k, c_ref[0] * nb + i, 0)),
                      pl.BlockSpec((1, tb, LANES), lambda k, i, c_ref: (k, i, 0))],
            out_specs=pl.BlockSpec((1, tb, LANES), lambda k, i, c_ref: (k, i, 0))),
        out_shape=jax.ShapeDtypeStruct((N_CHIPS, hr, LANES), F32),
        compiler_params=_params(("arbitrary", "arbitrary")),
    )(c_arr, g, recv)


def sum_chips(parts):
    _, hr, _ = parts.shape
    tb = _pick(hr, 1600, 8)

    def body(p_ref, o_ref):
        o_ref[...] = ((p_ref[0] + p_ref[1]) + p_ref[2]) + p_ref[3]

    return pl.pallas_call(
        body, name="sum_chips", grid=(hr // tb,),
        in_specs=[pl.BlockSpec((N_CHIPS, tb, LANES), lambda i: (0, i, 0))],
        out_specs=pl.BlockSpec((tb, LANES), lambda i: (i, 0)),
        out_shape=jax.ShapeDtypeStruct((hr, LANES), F32),
        compiler_params=_params(("arbitrary",)),
    )(parts)


def adamw(w, g, m, v, name):
    rows, width = w.shape
    tb = _pick(rows, 1600, 8)

    def body(w_ref, g_ref, m_ref, v_ref, d_ref, mo_ref, vo_ref):
        gg = g_ref[...]
        m_new = ADAM_B1 * m_ref[...] + (1.0 - ADAM_B1) * gg
        v_new = ADAM_B2 * v_ref[...] + (1.0 - ADAM_B2) * (gg * gg)
        m_hat = m_new / (1.0 - ADAM_B1 ** ADAM_STEP)
        v_hat = v_new / (1.0 - ADAM_B2 ** ADAM_STEP)
        d_ref[...] = -ADAM_LR * (m_hat / (jnp.sqrt(v_hat) + ADAM_EPS) + ADAM_WD * w_ref[...])
        mo_ref[...] = m_new
        vo_ref[...] = v_new

    spec = pl.BlockSpec((tb, width), lambda i: (i, 0))
    return pl.pallas_call(
        body, name=name, grid=(rows // tb,), in_specs=[spec] * 4, out_specs=[spec] * 3,
        out_shape=[jax.ShapeDtypeStruct((rows, width), F32)] * 3,
        compiler_params=_params(("arbitrary",)),
    )(w, g, m, v)


_BIG = (("w_in", 1), ("w_out", 0), ("w_gate_up", 1), ("w_down", 0))


def _flat_shards(full, axis):
    r, c = full.shape
    if axis == 0:
        return full.reshape(N_CHIPS, (r // N_CHIPS) * c)
    return full.reshape(r, N_CHIPS, c // N_CHIPS).transpose(1, 0, 2).reshape(N_CHIPS, r * (c // N_CHIPS))


def _unflat_shards(flat, shape, axis):
    r, c = shape
    if axis == 0:
        return flat.reshape(r, c)
    return flat.reshape(N_CHIPS, r, c // N_CHIPS).transpose(1, 0, 2).reshape(r, c)


def _pack_rows(vecs, width=1024):
    rows, cur, used = [], [], 0
    for v in vecs:
        n = v.shape[1]
        if used + n > width:
            cur.append(jnp.zeros((1, width - used), F32))
            rows.append(jnp.concatenate(cur, axis=1))
            cur, used = [], 0
        cur.append(v)
        used += n
    cur.append(jnp.zeros((1, width - used), F32))
    rows.append(jnp.concatenate(cur, axis=1))
    return rows


def _pad128(v):
    return jnp.pad(v, ((0, 0), (0, LANES - v.shape[1])))


def kernel(x, norm1_w, w_in, lb_logits, hg_norm_w, q_norm_w, k_norm_w, att_norm_w, w_out, norm2_w, w_gate_up, w_down, final_norm_w, loss_target, m_norm1_w, m_w_in, m_lb_logits, m_hg_norm_w, m_q_norm_w, m_k_norm_w, m_att_norm_w, m_w_out, m_norm2_w, m_w_gate_up, m_w_down, m_final_norm_w, v_norm1_w, v_w_in, v_lb_logits, v_hg_norm_w, v_q_norm_w, v_k_norm_w, v_att_norm_w, v_w_out, v_norm2_w, v_w_gate_up, v_w_down, v_final_norm_w):
    t, d = x.shape[1], x.shape[2]
    xi, yi, ci = _place()
    chip = 2 * xi + yi
    x2d = x.reshape(t, d)
    tgt = loss_target.reshape(t, d)
    big_w = {"w_in": w_in[0], "w_out": w_out[0], "w_gate_up": w_gate_up[0], "w_down": w_down[0]}
    big_m = {"w_in": m_w_in[0], "w_out": m_w_out[0], "w_gate_up": m_w_gate_up[0], "w_down": m_w_down[0]}
    big_v = {"w_in": v_w_in[0], "w_out": v_w_out[0], "w_gate_up": v_w_gate_up[0], "w_down": v_w_down[0]}
    sizes = [big_w[n].size for n, _ in _BIG]
    offs = np.concatenate([[0], np.cumsum(sizes)]).tolist()
    full_shape = {n: ((big_w[n].shape[0] * (N_CHIPS if ax == 0 else 1), big_w[n].shape[1] * (N_CHIPS if ax == 1 else 1)))
                  for n, ax in _BIG}

    def flat(tree, dtype):
        return jnp.concatenate([tree[n].astype(dtype).reshape(-1) for n, _ in _BIG]).reshape(-1, LANES)

    gathered = allgather_shards(flat(big_w, BF16)).reshape(N_CHIPS, -1)
    wfull = {n: _unflat_shards(gathered[:, offs[i]:offs[i + 1]], full_shape[n], ax) for i, (n, ax) in enumerate(_BIG)}
    lb_rows = lb_logits.reshape(4, LANES) * (ci == 0).astype(F32)
    lb_pad = lax.dynamic_update_slice(jnp.zeros((8, 1024), F32), lb_rows, (0, chip * LANES))
    lb_full = allreduce_small(lb_pad, "gather_lb")[:4, :HG_W]
    l_f0, l_f1, l_b0, l_b1 = (lb_full[i:i + 1] for i in range(4))

    n1 = norm1_w.reshape(1, d)
    n2 = norm2_w.reshape(1, d)
    nf = final_norm_w.reshape(1, d)
    (h1,) = _rows(lambda a, w: ((_rms(a, w),), ()), "norm1", [_full(x2d)], [n1], [(d, BF16)])
    u = mm_rows(h1, wfull["w_in"], "mm_in")
    o_f, ss_f = gla_fwd(u, l_f0, l_f1, 1, False, "gla_fwd_f")
    o_b, ss_b = gla_fwd(u, l_b0, l_b1, 2, True, "gla_fwd_b")

    cos2, sin2 = _rope_tables(t)
    cq, sq = jnp.tile(cos2, (1, ATT_HEADS)), jnp.tile(sin2, (1, ATT_HEADS))
    ck, sk = jnp.tile(cos2, (1, ATT_KV)), jnp.tile(sin2, (1, ATT_KV))
    qw, kw = _pad128(q_norm_w.reshape(1, ATT_DH)), _pad128(k_norm_w.reshape(1, ATT_DH))
    gq, gk = _group_sum_matrix(ATT_QW), _group_sum_matrix(ATT_KVW)
    tq, tk = _tile_matrix(ATT_QW), _tile_matrix(ATT_KVW)
    prep_rows = [(u, ATT_QW, 5), (u, ATT_KVW, 24), _full(cq), _full(sq), _full(ck), _full(sk)]
    prep_consts = [qw, kw, gq, gk, tq, tk]
    q_rot, k_rot, v_b = _rows(
        lambda aq, ak, av, *rest: ((*_att_prep_fn(aq, ak, *rest), av), ()), "att_prep",
        [prep_rows[0], prep_rows[1], (u, ATT_KVW, 25)] + prep_rows[2:], prep_consts,
        [(ATT_QW, BF16), (ATT_KVW, BF16), (ATT_KVW, BF16)], tm=256)
    q8, k2, v2 = _heads_major(q_rot, ATT_HEADS), _heads_major(k_rot, ATT_KV), _heads_major(v_b, ATT_KV)
    q_t, q_s = _strips_t(q8), _strips(q8)
    o_t, lse = fa_fwd(q_t, k2, _blocks_t(v2, FA_K_FWD))
    o_att = _tokens_major(o_t.transpose(0, 1, 3, 2), ATT_HEADS, t)

    hgw = hg_norm_w.reshape(1, HG_D)
    attw = att_norm_w.reshape(1, ATT_QW)
    mix_rows = [_full(o_f), _full(o_b), (u, HG_W, 4), _full(o_att)]
    (mix,) = _rows(lambda *a: ((_post_mix_fn(*a),), ()), "post_mix", mix_rows, [hgw, attw], [(d, BF16)], tm=256)
    x1 = mm_rows(mix, wfull["w_out"], "mm_out", res=x2d)
    (h2,) = _rows(lambda a, w: ((_rms(a, w),), ()), "norm2", [_full(x1)], [n2], [(d, BF16)])
    gu = mm_rows(h2, wfull["w_gate_up"], "mm_gate_up")
    dff = gu.shape[1] // 2
    (act,) = _rows(lambda g_, u_: ((_swiglu_fn(g_, u_),), ()), "swiglu", [(gu, dff, 0), (gu, dff, 1)], [],
                   [(dff, BF16)], tm=256)
    x2 = mm_rows(act, wfull["w_down"], "mm_down", res=x1)

    def loss_bwd_fn(a, tg, w):
        val, vjp = jax.vjp(lambda a_, w_: _loss_fn(a_, w_, tg), a, w)
        da, dw = vjp(jnp.ones((1, 1), F32))
        return (da, da), (dw, jnp.broadcast_to(val, (1, LANES)))

    dx2, dx2b, g_final, loss_part = _rows(loss_bwd_fn, "loss_bwd", [_full(x2), _full(tgt)], [nf],
                                          [(d, F32), (d, BF16)], [d, LANES], tm=256)
    dact = mm_rows(dx2b, wfull["w_down"], "mm_down_dx", trans_b=True)
    g_w_down = mm_tn(act, dx2b, "mm_down_dw")

    def swiglu_bwd_fn(g_, u_, da):
        _, vjp = jax.vjp(_swiglu_fn, g_, u_)
        dg, du_ = vjp(da)
        return (jnp.concatenate([dg, du_], axis=1),), ()

    (dgu,) = _rows(swiglu_bwd_fn, "swiglu_bwd", [(gu, dff, 0), (gu, dff, 1), _full(dact)], [],
                   [(2 * dff, BF16)], tm=256)
    dh2 = mm_rows(dgu, wfull["w_gate_up"], "mm_gate_up_dx", trans_b=True)
    g_w_gate_up = mm_tn(h2, dgu, "mm_gate_up_dw")

    def norm_bwd_fn(a, dh, dres, w):
        _, vjp = jax.vjp(_rms, a, w)
        da, dw = vjp(dh)
        da = da + dres
        return (da, da), (dw,)

    dx1, dx1b, g_norm2 = _rows(norm_bwd_fn, "norm2_bwd", [_full(x1), _full(dh2), _full(dx2)], [n2],
                               [(d, F32), (d, BF16)], [d], tm=256)
    dmix = mm_rows(dx1b, wfull["w_out"], "mm_out_dx", trans_b=True)
    g_w_out = mm_tn(mix, dx1b, "mm_out_dw")

    def post_mix_bwd_fn(of, ob, ug, oa, dm, hgw_, attw_, gq_):
        _, vjp = jax.vjp(_post_mix_fn, of, ob, ug, oa, hgw_, attw_)
        dof, _, dug, doa, dhgw, dattw = vjp(dm)
        return (dof, dug, doa, _xdot(doa * oa, gq_)), (dhgw, dattw)

    do_hg, du_g, do_att, dsum, g_hg, g_att = _rows(
        post_mix_bwd_fn, "post_mix_bwd", mix_rows + [_full(dmix)], [hgw, attw, gq],
        [(HG_W, F32), (HG_W, BF16), (ATT_QW, BF16), (ATT_QW, F32)], [HG_D, ATT_QW], tm=256)

    do8 = _heads_major(do_att, ATT_HEADS)
    ns, nq = lse.shape[1], lse.shape[3]
    d_col = _heads_major(dsum, ATT_HEADS)[:, :, 0].reshape(ATT_KV, ns, nq, 1)
    dq_s, dk_t, dv_t = fa_bwd(q_s, q_t, _strips(do8), _strips_t(do8), lse.reshape(ATT_KV, ns, nq, 1), d_col, k2,
                              _blocks_t(k2, FA_K_BWD), _blocks_t(v2, FA_K_BWD))
    dq8, dk2, dv2 = dq_s.reshape(ATT_KV, ns * nq, ATT_DH), _unblocks_t(dk_t), _unblocks_t(dv_t)
    dq_rot = _tokens_major(dq8, ATT_HEADS, t)
    dk_rot = _tokens_major(dk2, ATT_KV, t)
    dv_tok = _tokens_major(dv2, ATT_KV, t)

    def att_prep_bwd_fn(aq, ak, cq_, sq_, ck_, sk_, dq, dk, qw_, kw_, gq_, gk_, tq_, tk_):
        _, vjp = jax.vjp(lambda a, b, c_, e: _att_prep_fn(a, b, cq_, sq_, ck_, sk_, c_, e, gq_, gk_, tq_, tk_),
                         aq, ak, qw_, kw_)
        daq, dak, dqw, dkw = vjp((dq, dk))
        return (daq, dak), (dqw, dkw)

    da_q, da_k, g_q, g_k = _rows(att_prep_bwd_fn, "att_prep_bwd", prep_rows + [_full(dq_rot), _full(dk_rot)],
                                 prep_consts, [(ATT_QW, BF16), (ATT_KVW, BF16)], [LANES, LANES], tm=256)

    dq_f, df_f, di_f, dl_f0, dl_f1 = gla_bwd(u, l_f0, l_f1, ss_f, do_hg, 1, False, "gla_bwd_f")
    dq_b, df_b, di_b, dl_b0, dl_b1 = gla_bwd(u, l_b0, l_b1, ss_b, do_hg, 2, True, "gla_bwd_b")

    def assemble_fn(qf, qb, ff, fb, i_f, i_b, dg, aq, ak, av):
        parts = [qf + qb, ff, fb, i_f + i_b, dg.astype(F32), aq.astype(F32), ak.astype(F32), av]
        return (jnp.concatenate(parts, axis=1),), ()

    (du,) = _rows(assemble_fn, "assemble_du",
                  [_full(a) for a in (dq_f, dq_b, df_f, df_b, di_f, di_b, du_g, da_q, da_k, dv_tok)], [],
                  [(u.shape[1], BF16)], tm=256)
    dh1 = mm_rows(du, wfull["w_in"], "mm_in_dx", trans_b=True)
    g_w_in = mm_tn(h1, du, "mm_in_dw")
    grad_x, _, g_norm1 = _rows(norm_bwd_fn, "norm1_bwd", [_full(x2d), _full(dh1), _full(dx1)], [n1],
                               [(d, F32), (d, BF16)], [d], tm=256)

    gfull = {"w_in": g_w_in, "w_out": g_w_out, "w_gate_up": g_w_gate_up, "w_down": g_w_down}
    gflat = jnp.concatenate([_flat_shards(gfull[n], ax) for n, ax in _BIG], axis=1).reshape(N_CHIPS, -1, LANES)
    chip_part = add_my_half(gflat, rs_siblings(gflat))
    g_shard = share_halves(sum_chips(rs_chips(chip_part)))
    d_shard, m_shard, v_shard = adamw(flat(big_w, F32), g_shard, flat(big_m, F32), flat(big_v, F32), "adamw_big")

    def unflat(a):
        a = a.reshape(-1)
        return {n: a[offs[i]:offs[i + 1]].reshape((1,) + big_w[n].shape) for i, (n, _) in enumerate(_BIG)}

    g_big, d_big, m_big, v_big = unflat(g_shard), unflat(d_shard), unflat(m_shard), unflat(v_shard)

    small = [g_norm1, g_norm2, g_final, g_att, g_hg, g_q, g_k, loss_part, dl_f0, dl_f1, dl_b0, dl_b1]
    packed = _pack_rows(small)
    packed += [jnp.zeros((1, 1024), F32)] * (8 - len(packed))
    tot = allreduce_small(jnp.concatenate(packed, axis=0), "allreduce_small")
    s_norm1, s_norm2, s_final = tot[0:1], tot[1:2], tot[2:3]
    s_att, s_hg, s_q, s_k = tot[3:4, 0:512], tot[3:4, 512:640], tot[3:4, 640:704], tot[3:4, 768:832]
    loss = tot[3, 896]
    s_lb = jnp.concatenate([tot[4:5, 0:512], tot[4:5, 512:1024], tot[5:6, 0:512], tot[5:6, 512:1024]], axis=0)
    s_lb = lax.dynamic_slice(s_lb, (0, chip * LANES), (4, LANES)).reshape(1, 512)

    names = ["norm1_w", "lb_logits", "hg_norm_w", "q_norm_w", "k_norm_w", "att_norm_w", "norm2_w", "final_norm_w"]
    g_small = dict(zip(names, [s_norm1, s_lb, s_hg, s_q, s_k, s_att, s_norm2, s_final]))
    w_small = dict(zip(names, [norm1_w, lb_logits, hg_norm_w, q_norm_w, k_norm_w, att_norm_w, norm2_w, final_norm_w]))
    m_small = dict(zip(names, [m_norm1_w, m_lb_logits, m_hg_norm_w, m_q_norm_w, m_k_norm_w, m_att_norm_w, m_norm2_w, m_final_norm_w]))
    v_small = dict(zip(names, [v_norm1_w, v_lb_logits, v_hg_norm_w, v_q_norm_w, v_k_norm_w, v_att_norm_w, v_norm2_w, v_final_norm_w]))

    def pack_small(tree):
        rows = _pack_rows([tree[n].reshape(1, -1) for n in names])
        rows += [jnp.zeros((1, 1024), F32)] * (8 - len(rows))
        return jnp.concatenate(rows, axis=0)

    d_s, m_s, v_s = adamw(pack_small(w_small), pack_small(g_small), pack_small(m_small), pack_small(v_small), "adamw_small")

    def unpack_small(a):
        out, r, used = {}, 0, 0
        for n in names:
            size = w_small[n].size
            if used + size > 1024:
                r, used = r + 1, 0
            out[n] = a[r, used:used + size].reshape(w_small[n].shape)
            used += size
        return out

    d_sm, m_sm, v_sm = unpack_small(d_s), unpack_small(m_s), unpack_small(v_s)
    g_sm = {n: g_small[n].reshape(w_small[n].shape) for n in names}

    order = ["norm1_w", "w_in", "lb_logits", "hg_norm_w", "q_norm_w", "k_norm_w", "att_norm_w", "w_out", "norm2_w",
             "w_gate_up", "w_down", "final_norm_w"]

    def pick(small_tree, big_tree):
        return [big_tree[n] if n in big_tree else small_tree[n] for n in order]

    return (loss, grad_x.reshape(x.shape), *pick(g_sm, g_big), *pick(d_sm, d_big), *pick(m_sm, m_big),
            *pick(v_sm, v_big))
```

```python
import functools

import numpy as np
import jax
import jax.numpy as jnp
from jax import lax
from jax.experimental import pallas as pl
from jax.experimental.pallas import tpu as pltpu

F32 = jnp.float32
BF16 = jnp.bfloat16
MESH = pl.DeviceIdType.MESH
HIGHEST = lax.Precision.HIGHEST

EPS = 1e-6
GRID_W = 64
HG_HEADS = 4
HG_D = 128
HG_W = HG_HEADS * HG_D
CHUNK = 64
ATT_HEADS = 8
ATT_KV = 2
ATT_DH = 64
ATT_QW = ATT_HEADS * ATT_DH
ATT_KVW = ATT_KV * ATT_DH
ROPE_THETA = 10000.0
N_CHIPS = 4
N_DEV = 8

ADAM_LR = 0.001
ADAM_B1 = 0.9
ADAM_B2 = 0.999
ADAM_EPS = 1e-08
ADAM_WD = 0.01
ADAM_STEP = 10

VMEM_LIMIT = 52 * 1024 * 1024
LANES = 128


def _params(sem=None):
    return pltpu.CompilerParams(dimension_semantics=sem, vmem_limit_bytes=VMEM_LIMIT)


def _dg(a, b, ca, cb):
    return lax.dot_general(a.astype(BF16), b.astype(BF16), (((ca,), (cb,)), ((), ())),
                           preferred_element_type=F32)


@jax.custom_vjp
def dot_nn(a, b):
    return _dg(a, b, 1, 0)


def _dot_nn_fwd(a, b):
    return _dg(a, b, 1, 0), (a, b)


def _dot_nn_bwd(res, g):
    a, b = res
    return _dg(g, b, 1, 1), _dg(a, g, 0, 0)


dot_nn.defvjp(_dot_nn_fwd, _dot_nn_bwd)


@jax.custom_vjp
def dot_nt(a, b):
    return _dg(a, b, 1, 1)


def _dot_nt_fwd(a, b):
    return _dg(a, b, 1, 1), (a, b)


def _dot_nt_bwd(res, g):
    a, b = res
    return _dg(g, b, 1, 0), _dg(g, a, 0, 0)


dot_nt.defvjp(_dot_nt_fwd, _dot_nt_bwd)


@jax.custom_vjp
def dot_tn(a, b):
    return _dg(a, b, 0, 0)


def _dot_tn_fwd(a, b):
    return _dg(a, b, 0, 0), (a, b)


def _dot_tn_bwd(res, g):
    a, b = res
    return _dg(b, g, 1, 1), _dg(a, g, 1, 0)


dot_tn.defvjp(_dot_tn_fwd, _dot_tn_bwd)


def _xdot(a, m):
    return jnp.dot(a, m, precision=HIGHEST, preferred_element_type=F32)


@jax.custom_vjp
def swap_pairs(y):
    n = y.shape[-1]
    lane = lax.broadcasted_iota(jnp.int32, y.shape, 1)
    nxt = pltpu.roll(y, n - 1, 1)
    prv = pltpu.roll(y, 1, 1)
    return jnp.where(lane % 2 == 0, nxt, prv)


def _swap_fwd(y):
    return swap_pairs(y), None


def _swap_bwd(_, g):
    return (swap_pairs(g),)


swap_pairs.defvjp(_swap_fwd, _swap_bwd)


def _rms(x, w):
    return x * lax.rsqrt(jnp.mean(x * x, axis=-1, keepdims=True) + EPS) * w


def _sigmoid(x):
    return jax.nn.sigmoid(x)


def _rows(fn, name, rows, consts, outs, accs=(), tm=512):
    t = rows[0][0].shape[0]
    tm = min(tm, t)
    n_r, n_c, n_o, n_a = len(rows), len(consts), len(outs), len(accs)

    def body(*refs):
        r = refs[:n_r]
        c = refs[n_r:n_r + n_c]
        o = refs[n_r + n_c:n_r + n_c + n_o]
        a = refs[n_r + n_c + n_o:]
        ro, ao = fn(*[x[...] for x in r], *[x[...] for x in c])
        for ref, val in zip(o, ro):
            ref[...] = val.astype(ref.dtype)
        if n_a:
            @pl.when(pl.program_id(0) == 0)
            def _():
                for ref in a:
                    ref[...] = jnp.zeros(ref.shape, F32)
            for ref, val in zip(a, ao):
                ref[...] += val

    in_specs = [pl.BlockSpec((tm, w), lambda i, cb=cb: (i, cb)) for _, w, cb in rows]
    in_specs += [pl.BlockSpec(c.shape, lambda i, nd=c.ndim: (0,) * nd) for c in consts]
    out_specs = [pl.BlockSpec((tm, w), lambda i: (i, 0)) for w, _ in outs]
    out_specs += [pl.BlockSpec((1, w), lambda i: (0, 0)) for w in accs]
    out_shape = [jax.ShapeDtypeStruct((t, w), dt) for w, dt in outs]
    out_shape += [jax.ShapeDtypeStruct((1, w), F32) for w in accs]
    res = pl.pallas_call(
        body, name=name, grid=(t // tm,), in_specs=in_specs, out_specs=out_specs, out_shape=out_shape,
        compiler_params=_params(("arbitrary",)),
    )(*[a for a, _, _ in rows], *consts)
    return res


def _full(a):
    return (a, a.shape[1], 0)


def _pick(n, cap, mult):
    best = None
    for d in range(mult, min(n, cap) + 1, mult):
        if n % d == 0:
            best = d
    return best if best is not None else n


def mm_rows(a, b, name, trans_b=False, res=None, out_dtype=F32):
    m, k = a.shape
    n = b.shape[0] if trans_b else b.shape[1]
    tn = _pick(n, 3328, LANES)
    tm = _pick(m, 512 if k <= 3072 else 256, 8)
    has_res = res is not None

    def body(*refs):
        if has_res:
            a_ref, b_ref, r_ref, o_ref = refs
        else:
            a_ref, b_ref, o_ref = refs
        acc = _dg(a_ref[...], b_ref[...], 1, 1 if trans_b else 0)
        if has_res:
            acc = acc + r_ref[...]
        o_ref[...] = acc.astype(o_ref.dtype)

    in_specs = [pl.BlockSpec((tm, k), lambda j, i: (i, 0))]
    if trans_b:
        in_specs.append(pl.BlockSpec((tn, k), lambda j, i: (j, 0)))
    else:
        in_specs.append(pl.BlockSpec((k, tn), lambda j, i: (0, j)))
    args = [a, b]
    if has_res:
        in_specs.append(pl.BlockSpec((tm, tn), lambda j, i: (i, j)))
        args.append(res)
    return pl.pallas_call(
        body, name=name, grid=(n // tn, m // tm), in_specs=in_specs,
        out_specs=pl.BlockSpec((tm, tn), lambda j, i: (i, j)),
        out_shape=jax.ShapeDtypeStruct((m, n), out_dtype),
        compiler_params=_params(("arbitrary", "arbitrary")),
    )(*args)


def mm_tn(a, b, name):
    t, m = a.shape
    n = b.shape[1]
    tm = _pick(m, 512, LANES)
    tn = _pick(n, 3328, LANES)
    tk = _pick(t, 1024, 8)

    def body(a_ref, b_ref, o_ref):
        @pl.when(pl.program_id(2) == 0)
        def _():
            o_ref[...] = jnp.zeros(o_ref.shape, F32)
        o_ref[...] += _dg(a_ref[...], b_ref[...], 0, 0)

    return pl.pallas_call(
        body, name=name, grid=(m // tm, n // tn, t // tk),
        in_specs=[pl.BlockSpec((tk, tm), lambda i, j, kk: (kk, i)),
                  pl.BlockSpec((tk, tn), lambda i, j, kk: (kk, j))],
        out_specs=pl.BlockSpec((tm, tn), lambda i, j, kk: (i, j)),
        out_shape=jax.ShapeDtypeStruct((m, n), F32),
        compiler_params=_params(("arbitrary", "arbitrary", "arbitrary")),
    )(a, b)


def _lower_bound(l0, l1):
    m = jnp.maximum(l0, l1)
    e0 = jnp.exp(l0 - m)
    e1 = jnp.exp(l1 - m)
    return e0 / (e0 + e1)


def _gla_consts(rev):
    ri = lax.broadcasted_iota(jnp.int32, (CHUNK, CHUNK), 0)
    ci = lax.broadcasted_iota(jnp.int32, (CHUNK, CHUNK), 1)
    keep = (ci >= ri) if rev else (ci <= ri)
    tm = keep.astype(F32)
    if rev:
        rvec = (ci >= CHUNK // 2).astype(F32)
    else:
        rvec = (ci <= CHUNK // 2 - 1).astype(F32)
    m3 = jnp.concatenate([tm, tm - rvec, 1.0 - tm], axis=0)
    return keep, m3


def _gla_block(uq, uf, ui, l0, l1, st_in, rev):
    ncb = uq.shape[0] // CHUNK
    heads = range(HG_HEADS)
    keep, m3 = _gla_consts(rev)
    lb = _lower_bound(l0, l1)
    q = uq * _sigmoid(uq)
    k = (1.0 - lb) * _sigmoid(-uf)
    g = jnp.log(lb + (1.0 - lb) * _sigmoid(uf))

    def rows(a, c):
        return a[c * CHUNK:(c + 1) * CHUNK]

    def head(a, h):
        return a[:, h * HG_D:(h + 1) * HG_D]

    b3 = [_xdot(m3, rows(g, c)) for c in range(ncb)]
    q_in, k_in, q_b, k_d, decay = [], [], [], [], []
    for c in range(ncb):
        b, bmr, lmb = b3[c][0:CHUNK], b3[c][CHUNK:2 * CHUNK], b3[c][2 * CHUNK:3 * CHUNK]
        qc, kc = rows(q, c), rows(k, c)
        q_in.append(qc * jnp.exp(bmr))
        k_in.append(kc * jnp.exp(-bmr))
        q_b.append(qc * jnp.exp(b))
        k_d.append(kc * jnp.exp(lmb))
        decay.append(jnp.exp(jnp.sum(rows(g, c), axis=0, keepdims=True)))
    scores = [[jnp.where(keep, dot_nt(head(q_in[c], h), head(k_in[c], h)), 0.0) for h in heads] for c in range(ncb)]
    o_intra = [[dot_nn(scores[c][h], head(rows(ui, c), h)) for h in heads] for c in range(ncb)]
    contrib = [[dot_tn(head(rows(ui, c), h), head(k_d[c], h)) for h in heads] for c in range(ncb)]
    st = list(st_in)
    o_rows = [None] * ncb
    for c in (reversed(range(ncb)) if rev else range(ncb)):
        parts = []
        for h in heads:
            parts.append(o_intra[c][h] + dot_nt(head(q_b[c], h), st[h]))
            st[h] = st[h] * head(decay[c], h) + contrib[c][h]
        o_rows[c] = jnp.concatenate(parts, axis=1)
    return jnp.concatenate(o_rows, axis=0), tuple(st)


def _gla_blocks(t):
    tb = min(512, t)
    return tb, tb // CHUNK, t // tb


def gla_fwd(u, l0, l1, fcol, rev, name):
    t = u.shape[0]
    tb, _, nb = _gla_blocks(t)

    def blk(i):
        return (nb - 1 - i) if rev else i

    def body(uq_ref, uf_ref, ui_ref, l0_ref, l1_ref, o_ref, ss_ref, st_ref):
        @pl.when(pl.program_id(0) == 0)
        def _():
            st_ref[...] = jnp.zeros(st_ref.shape, F32)
        ss_ref[0] = st_ref[...]
        o, st_out = _gla_block(uq_ref[...], uf_ref[...], ui_ref[...], l0_ref[...], l1_ref[...],
                               tuple(st_ref[h] for h in range(HG_HEADS)), rev)
        o_ref[...] = o
        for h in range(HG_HEADS):
            st_ref[h] = st_out[h]

    row = lambda cb: pl.BlockSpec((tb, HG_W), lambda i: (blk(i), cb))
    vec = pl.BlockSpec((1, HG_W), lambda i: (0, 0))
    return pl.pallas_call(
        body, name=name, grid=(nb,),
        in_specs=[row(0), row(fcol), row(3), vec, vec],
        out_specs=[pl.BlockSpec((tb, HG_W), lambda i: (blk(i), 0)),
                   pl.BlockSpec((1, HG_HEADS, HG_D, HG_D), lambda i: (blk(i), 0, 0, 0))],
        out_shape=[jax.ShapeDtypeStruct((t, HG_W), F32),
                   jax.ShapeDtypeStruct((nb, HG_HEADS, HG_D, HG_D), F32)],
        scratch_shapes=[pltpu.VMEM((HG_HEADS, HG_D, HG_D), F32)],
        compiler_params=_params(("arbitrary",)),
    )(u, u, u, l0, l1)


def gla_bwd(u, l0, l1, ss, do, fcol, rev, name):
    t = u.shape[0]
    tb, _, nb = _gla_blocks(t)

    def blk(i):
        return i if rev else (nb - 1 - i)

    def body(uq_ref, uf_ref, ui_ref, l0_ref, l1_ref, ss_ref, do_ref,
             dq_ref, df_ref, di_ref, dl0_ref, dl1_ref, dst_ref):
        @pl.when(pl.program_id(0) == 0)
        def _():
            dst_ref[...] = jnp.zeros(dst_ref.shape, F32)
            dl0_ref[...] = jnp.zeros(dl0_ref.shape, F32)
            dl1_ref[...] = jnp.zeros(dl1_ref.shape, F32)
        heads = range(HG_HEADS)
        _, vjp = jax.vjp(functools.partial(_gla_block, rev=rev), uq_ref[...], uf_ref[...], ui_ref[...],
                         l0_ref[...], l1_ref[...], tuple(ss_ref[0, h] for h in heads))
        dq, df, di, dl0, dl1, dst = vjp((do_ref[...], tuple(dst_ref[h] for h in heads)))
        dq_ref[...] = dq
        df_ref[...] = df
        di_ref[...] = di
        dl0_ref[...] += dl0
        dl1_ref[...] += dl1
        for h in heads:
            dst_ref[h] = dst[h]

    row = lambda cb: pl.BlockSpec((tb, HG_W), lambda i: (blk(i), cb))
    vec = pl.BlockSpec((1, HG_W), lambda i: (0, 0))
    orow = pl.BlockSpec((tb, HG_W), lambda i: (blk(i), 0))
    return pl.pallas_call(
        body, name=name, grid=(nb,),
        in_specs=[row(0), row(fcol), row(3), vec, vec,
                  pl.BlockSpec((1, HG_HEADS, HG_D, HG_D), lambda i: (blk(i), 0, 0, 0)), orow],
        out_specs=[orow, orow, orow, vec, vec],
        out_shape=[jax.ShapeDtypeStruct((t, HG_W), F32)] * 3 + [jax.ShapeDtypeStruct((1, HG_W), F32)] * 2,
        scratch_shapes=[pltpu.VMEM((HG_HEADS, HG_D, HG_D), F32)],
        compiler_params=_params(("arbitrary",)),
    )(u, u, u, l0, l1, ss, do)


def _rope_tables(t):
    rows = t // GRID_W
    row = jnp.repeat(jnp.arange(rows), GRID_W).astype(F32)
    col = jnp.tile(jnp.arange(GRID_W), rows).astype(F32)
    axis_dim = ATT_DH // 2
    freqs = ROPE_THETA ** (-jnp.arange(0, axis_dim, 2, dtype=F32) / axis_dim)
    ang = jnp.concatenate([row[:, None] * freqs, col[:, None] * freqs], axis=-1)
    cos2 = jnp.repeat(jnp.cos(ang), 2, axis=-1)
    sin2 = jnp.repeat(jnp.sin(ang), 2, axis=-1) * jnp.tile(jnp.array([-1.0, 1.0], F32), ATT_DH // 2)
    return cos2, sin2


def _group_sum_matrix(width):
    idx = np.arange(width) // ATT_DH
    return jnp.asarray((idx[:, None] == idx[None, :]).astype(np.float32))


def _tile_matrix(width):
    m = np.zeros((LANES, width), np.float32)
    m[np.arange(width) % ATT_DH, np.arange(width)] = 1.0
    return jnp.asarray(m)


def _tile_w(w128, tile_m):
    w8 = jnp.broadcast_to(w128, (8, LANES))
    return jnp.sum(_xdot(w8, tile_m), axis=0, keepdims=True) * 0.125


def _head_norm_rope(a, w128, cos_t, sin_t, gsum, tile_m, scale):
    ssq = _xdot(a * a, gsum)
    y = a * lax.rsqrt(ssq * (1.0 / ATT_DH) + EPS) * _tile_w(w128, tile_m)
    return (y * cos_t + swap_pairs(y) * sin_t) * scale


def _att_prep_fn(aq, ak, cq, sq, ck, sk, qw, kw, gq, gk, tq, tk):
    q = _head_norm_rope(aq, qw, cq, sq, gq, tq, ATT_DH ** -0.5)
    k = _head_norm_rope(ak, kw, ck, sk, gk, tk, 1.0)
    return q, k


FA_Q = 256
FA_K_FWD = 1024
FA_K_BWD = 512
FA_STRIPS = 4


def _strips(a):
    g, l, dh = a.shape
    nq = min(FA_Q, l)
    return a.reshape(g, l // nq, nq, dh)


def _strips_t(a):
    return _strips(a).transpose(0, 1, 3, 2)


def _blocks_t(a, bk):
    g, l, dh = a.shape
    bk = min(bk, l)
    return a.reshape(g, l // bk, bk, dh).transpose(0, 1, 3, 2)


def fa_fwd(qt, k, vt):
    _, ns, dh, nq = qt.shape
    lk = k.shape[1]
    bk = vt.shape[3]
    nk = lk // bk
    spg = min(FA_STRIPS, ns)

    def body(q_ref, k_ref, v_ref, o_ref, lse_ref):
        qs = [q_ref[0, c] for c in range(spg)]

        def keys(j):
            return k_ref[0, pl.ds(pl.multiple_of(j * bk, bk), bk), :]

        def step(j, carry):
            st0, stats = carry
            kb = keys(j)
            vb = v_ref[0, j]
            sts = [st0] + [_dg(kb, qs[c], 1, 0) for c in range(1, spg)]
            out = []
            for c in range(spg):
                m, l, acc = stats[c]
                m_new = jnp.maximum(m, jnp.max(sts[c], axis=0, keepdims=True))
                alpha = jnp.exp(m - m_new)
                p = jnp.exp(sts[c] - m_new)
                l = alpha * l + jnp.sum(p, axis=0, keepdims=True)
                if c == spg - 1:
                    st0 = _dg(keys(jnp.minimum(j + 1, nk - 1)), qs[0], 1, 0)
                acc = alpha * acc + _dg(vb, p, 1, 0)
                out.append((m_new, l, acc))
            return st0, tuple(out)

        init = tuple((jnp.full((1, nq), -jnp.inf, F32), jnp.zeros((1, nq), F32), jnp.zeros((dh, nq), F32))
                     for _ in range(spg))
        _, res = lax.fori_loop(0, nk, step, (_dg(keys(0), qs[0], 1, 0), init))
        for c in range(spg):
            m, l, acc = res[c]
            o_ref[0, c] = acc / l
            lse_ref[0, c] = m + jnp.log(l)

    return pl.pallas_call(
        body, name="fa_fwd", grid=(ATT_KV, ns // spg),
        in_specs=[pl.BlockSpec((1, spg, dh, nq), lambda g, i: (g, i, 0, 0)),
                  pl.BlockSpec((1, lk, dh), lambda g, i: (g, 0, 0)),
                  pl.BlockSpec((1, nk, dh, bk), lambda g, i: (g, 0, 0, 0))],
        out_specs=[pl.BlockSpec((1, spg, dh, nq), lambda g, i: (g, i, 0, 0)),
                   pl.BlockSpec((1, spg, 1, nq), lambda g, i: (g, i, 0, 0))],
        out_shape=[jax.ShapeDtypeStruct((ATT_KV, ns, dh, nq), F32), jax.ShapeDtypeStruct((ATT_KV, ns, 1, nq), F32)],
        compiler_params=_params(("arbitrary", "arbitrary")),
    )(qt, k, vt)


def _unblocks_t(a):
    g, nb, dh, bk = a.shape
    return a.transpose(0, 1, 3, 2).reshape(g, nb * bk, dh)


def fa_bwd(qs, qt, dos, dot_, lse, dsum, k, kt, vt):
    _, ns, nq, dh = qs.shape
    lk = k.shape[1]
    bk = kt.shape[3]
    nk = lk // bk
    spg = min(FA_STRIPS, ns)
    nc = bk // LANES

    def body(qs_ref, qt_ref, dos_ref, dot_ref, lse_ref, d_ref, k_ref, kt_ref, vt_ref, dq_ref, dk_ref, dv_ref):
        @pl.when(pl.program_id(1) == 0)
        def _():
            dk_ref[...] = jnp.zeros(dk_ref.shape, F32)
            dv_ref[...] = jnp.zeros(dv_ref.shape, F32)

        strips = range(spg)
        lse_b = [jnp.broadcast_to(lse_ref[0, c], (nq, LANES)) for c in strips]
        d_b = [jnp.broadcast_to(d_ref[0, c], (nq, LANES)) for c in strips]

        def step(j, dqs):
            ktb, vtb = kt_ref[0, j], vt_ref[0, j]
            kb = k_ref[0, pl.ds(pl.multiple_of(j * bk, bk), bk), :]
            prods = [(_dg(qs_ref[0, c], ktb, 1, 0), _dg(dos_ref[0, c], vtb, 1, 0)) for c in strips]
            out = []
            for c in strips:
                s, dp = prods[c]
                ps, dss = [], []
                for cc in range(nc):
                    sl = slice(cc * LANES, (cc + 1) * LANES)
                    pc = jnp.exp(s[:, sl] - lse_b[c])
                    ps.append(pc.astype(BF16))
                    dss.append((pc * (dp[:, sl] - d_b[c])).astype(BF16))
                p, ds = jnp.concatenate(ps, axis=1), jnp.concatenate(dss, axis=1)
                dv_ref[0, j] += _dg(dot_ref[0, c], p, 1, 0)
                dk_ref[0, j] += _dg(qt_ref[0, c], ds, 1, 0)
                out.append(dqs[c] + _dg(ds, kb, 1, 0))
            return tuple(out)

        dqs = lax.fori_loop(0, nk, step, tuple(jnp.zeros((nq, dh), F32) for _ in strips))
        for c in strips:
            dq_ref[0, c] = dqs[c]

    sspec = pl.BlockSpec((1, spg, nq, dh), lambda g, i: (g, i, 0, 0))
    tspec = pl.BlockSpec((1, spg, dh, nq), lambda g, i: (g, i, 0, 0))
    cspec = pl.BlockSpec((1, spg, nq, 1), lambda g, i: (g, i, 0, 0))
    bspec = pl.BlockSpec((1, nk, dh, bk), lambda g, i: (g, 0, 0, 0))
    return pl.pallas_call(
        body, name="fa_bwd", grid=(ATT_KV, ns // spg),
        in_specs=[sspec, tspec, sspec, tspec, cspec, cspec, pl.BlockSpec((1, lk, dh), lambda g, i: (g, 0, 0)),
                  bspec, bspec],
        out_specs=[sspec, bspec, bspec],
        out_shape=[jax.ShapeDtypeStruct((ATT_KV, ns, nq, dh), F32), jax.ShapeDtypeStruct((ATT_KV, nk, dh, bk), F32),
                   jax.ShapeDtypeStruct((ATT_KV, nk, dh, bk), F32)],
        compiler_params=_params(("arbitrary", "arbitrary")),
    )(qs, qt, dos, dot_, lse, dsum, k, kt, vt)


def _heads_major(a, heads):
    t = a.shape[0]
    return a.reshape(t, heads, ATT_DH).transpose(1, 0, 2).reshape(ATT_KV, (heads // ATT_KV) * t, ATT_DH)


def _tokens_major(a, heads, t):
    return a.reshape(heads, t, ATT_DH).transpose(1, 0, 2).reshape(t, heads * ATT_DH)


def _post_mix_fn(of, ob, ug, oa, hgw, attw):
    o = of + ob
    parts = []
    for h in range(HG_HEADS):
        parts.append(_rms(o[:, h * HG_D:(h + 1) * HG_D], hgw))
    hg = jnp.concatenate(parts, axis=1) * (ug * _sigmoid(ug))
    return jnp.concatenate([hg, _rms(oa, attw)], axis=1)


def _swiglu_fn(gate, up):
    return gate * _sigmoid(gate) * up


def _loss_fn(x2, w, tgt):
    e = _rms(x2, w) - tgt
    return 0.5 * jnp.sum(jnp.mean(e * e, axis=-1, keepdims=True), axis=0, keepdims=True)


def _place():
    return lax.axis_index("x"), lax.axis_index("y"), lax.axis_index("c")


def _other_chips(x, y):
    return [(1 - x, y), (x, 1 - y), (1 - x, 1 - y)]


def allgather_shards(shard):
    r = shard.shape[0]
    hr = r // 2

    def body(src, out, ssem, rsem, lsem):
        x, y, c = _place()
        k = 2 * x + y
        sib = (x, y, 1 - c)
        chips = _other_chips(x, y)

        def half(kk, cc):
            return out.at[kk, pl.ds(cc * hr, hr), :]

        def copy(j, src_ref, dst_ref, to):
            return pltpu.make_async_remote_copy(src_ref=src_ref, dst_ref=dst_ref, send_sem=ssem.at[j],
                                                recv_sem=rsem.at[j], device_id=to, device_id_type=MESH)

        mine = pltpu.make_async_copy(src, out.at[k], lsem)
        mine.start()
        my_half = src.at[pl.ds(c * hr, hr), :]
        first = [copy(j, my_half, half(k, c), (px, py, c)) for j, (px, py) in enumerate(chips)]
        for cp in first:
            cp.start()
        passed = [copy(3 + j, half(2 * px + py, c), half(2 * px + py, c), sib) for j, (px, py) in enumerate(chips)]
        for j, (px, py) in enumerate(chips):
            copy(j, my_half, half(2 * px + py, c), (px, py, c)).wait_recv()
            passed[j].start()
        for j, (px, py) in enumerate(chips):
            copy(3 + j, my_half, half(2 * px + py, 1 - c), sib).wait_recv()
        for cp in first + passed:
            cp.wait_send()
        mine.wait()

    return pl.pallas_call(
        body, name="allgather_shards",
        in_specs=[pl.BlockSpec(memory_space=pl.ANY)],
        out_specs=pl.BlockSpec(memory_space=pl.ANY),
        out_shape=jax.ShapeDtypeStruct((N_CHIPS, r, LANES), shard.dtype),
        scratch_shapes=[pltpu.SemaphoreType.DMA((6,)), pltpu.SemaphoreType.DMA((6,)), pltpu.SemaphoreType.DMA],
    )(shard)


def rs_siblings(g):
    _, r, _ = g.shape
    hr = r // 2

    def body(src, out, ssem, rsem):
        x, y, c = _place()
        cp = pltpu.make_async_remote_copy(src_ref=src.at[:, pl.ds((1 - c) * hr, hr), :], dst_ref=out,
                                          send_sem=ssem, recv_sem=rsem, device_id=(x, y, 1 - c), device_id_type=MESH)
        cp.start()
        cp.wait()

    return pl.pallas_call(
        body, name="rs_siblings",
        in_specs=[pl.BlockSpec(memory_space=pl.ANY)],
        out_specs=pl.BlockSpec(memory_space=pl.ANY),
        out_shape=jax.ShapeDtypeStruct((N_CHIPS, hr, LANES), F32),
        scratch_shapes=[pltpu.SemaphoreType.DMA, pltpu.SemaphoreType.DMA],
    )(g)


def rs_chips(pa):
    def body(src, out, ssem, rsem, lsem):
        x, y, c = _place()
        k = 2 * x + y
        chips = _other_chips(x, y)
        mine = pltpu.make_async_copy(src.at[k], out.at[k], lsem)
        mine.start()
        sends = []
        for j, (px, py) in enumerate(chips):
            cp = pltpu.make_async_remote_copy(src_ref=src.at[2 * px + py], dst_ref=out.at[k], send_sem=ssem.at[j],
                                              recv_sem=rsem.at[j], device_id=(px, py, c), device_id_type=MESH)
            cp.start()
            sends.append(cp)
        for j, (px, py) in enumerate(chips):
            pltpu.make_async_remote_copy(src_ref=src.at[k], dst_ref=out.at[2 * px + py], send_sem=ssem.at[j],
                                         recv_sem=rsem.at[j], device_id=(px, py, c), device_id_type=MESH).wait_recv()
        for cp in sends:
            cp.wait_send()
        mine.wait()

    return pl.pallas_call(
        body, name="rs_chips",
        in_specs=[pl.BlockSpec(memory_space=pl.ANY)],
        out_specs=pl.BlockSpec(memory_space=pl.ANY),
        out_shape=jax.ShapeDtypeStruct(pa.shape, F32),
        scratch_shapes=[pltpu.SemaphoreType.DMA((3,)), pltpu.SemaphoreType.DMA((3,)), pltpu.SemaphoreType.DMA],
    )(pa)


def share_halves(rh):
    hr = rh.shape[0]

    def body(src, out, ssem, rsem, lsem):
        x, y, c = _place()
        dst = out.at[pl.ds(c * hr, hr), :]
        mine = pltpu.make_async_copy(src, dst, lsem)
        mine.start()
        cp = pltpu.make_async_remote_copy(src_ref=src, dst_ref=dst, send_sem=ssem, recv_sem=rsem,
                                          device_id=(x, y, 1 - c), device_id_type=MESH)
        cp.start()
        pltpu.make_async_remote_copy(src_ref=src, dst_ref=out.at[pl.ds((1 - c) * hr, hr), :], send_sem=ssem,
                                     recv_sem=rsem, device_id=(x, y, 1 - c), device_id_type=MESH).wait_recv()
        cp.wait_send()
        mine.wait()

    return pl.pallas_call(
        body, name="share_halves",
        in_specs=[pl.BlockSpec(memory_space=pl.ANY)],
        out_specs=pl.BlockSpec(memory_space=pl.ANY),
        out_shape=jax.ShapeDtypeStruct((2 * hr, LANES), F32),
        scratch_shapes=[pltpu.SemaphoreType.DMA, pltpu.SemaphoreType.DMA, pltpu.SemaphoreType.DMA],
    )(rh)


def allreduce_small(p, name):
    rows, width = p.shape

    def body(p_ref, s_ref, gath, ssem, rsem):
        x, y, c = _place()
        me = 4 * x + 2 * y + c
        copies = []
        for d in range(1, N_DEV):
            dx, dy, dc = (d >> 2) & 1, (d >> 1) & 1, d & 1
            peer = (x ^ dx, y ^ dy, c ^ dc)
            cp = pltpu.make_async_remote_copy(src_ref=p_ref, dst_ref=gath.at[me], send_sem=ssem.at[d - 1],
                                              recv_sem=rsem.at[d - 1], device_id=peer, device_id_type=MESH)
            cp.start()
            copies.append(cp)
        gath[me] = p_ref[...]
        for d, cp in enumerate(copies, start=1):
            dx, dy, dc = (d >> 2) & 1, (d >> 1) & 1, d & 1
            peer_slot = 4 * (x ^ dx) + 2 * (y ^ dy) + (c ^ dc)
            pltpu.make_async_remote_copy(src_ref=p_ref, dst_ref=gath.at[peer_slot], send_sem=ssem.at[d - 1],
                                         recv_sem=rsem.at[d - 1], device_id=(x ^ dx, y ^ dy, c ^ dc),
                                         device_id_type=MESH).wait_recv()
        for cp in copies:
            cp.wait_send()
        acc = gath[0]
        for d in range(1, N_DEV):
            acc = acc + gath[d]
        s_ref[...] = acc

    return pl.pallas_call(
        body, name=name,
        in_specs=[pl.BlockSpec(memory_space=pltpu.VMEM)],
        out_specs=pl.BlockSpec(memory_space=pltpu.VMEM),
        out_shape=jax.ShapeDtypeStruct((rows, width), F32),
        scratch_shapes=[pltpu.VMEM((N_DEV, rows, width), F32), pltpu.SemaphoreType.DMA((N_DEV - 1,)),
                        pltpu.SemaphoreType.DMA((N_DEV - 1,))],
    )(p)


def add_my_half(g, recv):
    _, r, _ = g.shape
    hr = r // 2
    tb = _pick(hr, 1600, 8)
    nb = hr // tb
    c_arr = lax.axis_index("c").astype(jnp.int32).reshape(1)

    def body(c_ref, g_ref, r_ref, o_ref):
        o_ref[...] = g_ref[...] + r_ref[...]

    return pl.pallas_call(
        body, name="add_my_half",
        grid_spec=pltpu.PrefetchScalarGridSpec(
            num_scalar_prefetch=1, grid=(N_CHIPS, nb),
            in_specs=[pl.BlockSpec((1, tb, LANES), lambda k, i, c_ref: (k, c_ref[0] * nb + i, 0)),
                      pl.BlockSpec((1, tb, LANES), lambda k, i, c_ref: (k, i, 0))],
            out_specs=pl.BlockSpec((1, tb, LANES), lambda k, i, c_ref: (k, i, 0))),
        out_shape=jax.ShapeDtypeStruct((N_CHIPS, hr, LANES), F32),
        compiler_params=_params(("arbitrary", "arbitrary")),
    )(c_arr, g, recv)


def sum_chips(parts):
    _, hr, _ = parts.shape
    tb = _pick(hr, 1600, 8)

    def body(p_ref, o_ref):
        o_ref[...] = ((p_ref[0] + p_ref[1]) + p_ref[2]) + p_ref[3]

    return pl.pallas_call(
        body, name="sum_chips", grid=(hr // tb,),
        in_specs=[pl.BlockSpec((N_CHIPS, tb, LANES), lambda i: (0, i, 0))],
        out_specs=pl.BlockSpec((tb, LANES), lambda i: (i, 0)),
        out_shape=jax.ShapeDtypeStruct((hr, LANES), F32),
        compiler_params=_params(("arbitrary",)),
    )(parts)


def adamw(w, g, m, v, name):
    rows, width = w.shape
    tb = _pick(rows, 1600, 8)

    def body(w_ref, g_ref, m_ref, v_ref, d_ref, mo_ref, vo_ref):
        gg = g_ref[...]
        m_new = ADAM_B1 * m_ref[...] + (1.0 - ADAM_B1) * gg
        v_new = ADAM_B2 * v_ref[...] + (1.0 - ADAM_B2) * (gg * gg)
        m_hat = m_new / (1.0 - ADAM_B1 ** ADAM_STEP)
        v_hat = v_new / (1.0 - ADAM_B2 ** ADAM_STEP)
        d_ref[...] = -ADAM_LR * (m_hat / (jnp.sqrt(v_hat) + ADAM_EPS) + ADAM_WD * w_ref[...])
        mo_ref[...] = m_new
        vo_ref[...] = v_new

    spec = pl.BlockSpec((tb, width), lambda i: (i, 0))
    return pl.pallas_call(
        body, name=name, grid=(rows // tb,), in_specs=[spec] * 4, out_specs=[spec] * 3,
        out_shape=[jax.ShapeDtypeStruct((rows, width), F32)] * 3,
        compiler_params=_params(("arbitrary",)),
    )(w, g, m, v)


_BIG = (("w_in", 1), ("w_out", 0), ("w_gate_up", 1), ("w_down", 0))


def _flat_shards(full, axis):
    r, c = full.shape
    if axis == 0:
        return full.reshape(N_CHIPS, (r // N_CHIPS) * c)
    return full.reshape(r, N_CHIPS, c // N_CHIPS).transpose(1, 0, 2).reshape(N_CHIPS, r * (c // N_CHIPS))


def _unflat_shards(flat, shape, axis):
    r, c = shape
    if axis == 0:
        return flat.reshape(r, c)
    return flat.reshape(N_CHIPS, r, c // N_CHIPS).transpose(1, 0, 2).reshape(r, c)


def _pack_rows(vecs, width=1024):
    rows, cur, used = [], [], 0
    for v in vecs:
        n = v.shape[1]
        if used + n > width:
            cur.append(jnp.zeros((1, width - used), F32))
            rows.append(jnp.concatenate(cur, axis=1))
            cur, used = [], 0
        cur.append(v)
        used += n
    cur.append(jnp.zeros((1, width - used), F32))
    rows.append(jnp.concatenate(cur, axis=1))
    return rows


def _pad128(v):
    return jnp.pad(v, ((0, 0), (0, LANES - v.shape[1])))


def kernel(x, norm1_w, w_in, lb_logits, hg_norm_w, q_norm_w, k_norm_w, att_norm_w, w_out, norm2_w, w_gate_up, w_down, final_norm_w, loss_target, m_norm1_w, m_w_in, m_lb_logits, m_hg_norm_w, m_q_norm_w, m_k_norm_w, m_att_norm_w, m_w_out, m_norm2_w, m_w_gate_up, m_w_down, m_final_norm_w, v_norm1_w, v_w_in, v_lb_logits, v_hg_norm_w, v_q_norm_w, v_k_norm_w, v_att_norm_w, v_w_out, v_norm2_w, v_w_gate_up, v_w_down, v_final_norm_w):
    t, d = x.shape[1], x.shape[2]
    xi, yi, ci = _place()
    chip = 2 * xi + yi
    x2d = x.reshape(t, d)
    tgt = loss_target.reshape(t, d)
    big_w = {"w_in": w_in[0], "w_out": w_out[0], "w_gate_up": w_gate_up[0], "w_down": w_down[0]}
    big_m = {"w_in": m_w_in[0], "w_out": m_w_out[0], "w_gate_up": m_w_gate_up[0], "w_down": m_w_down[0]}
    big_v = {"w_in": v_w_in[0], "w_out": v_w_out[0], "w_gate_up": v_w_gate_up[0], "w_down": v_w_down[0]}
    sizes = [big_w[n].size for n, _ in _BIG]
    offs = np.concatenate([[0], np.cumsum(sizes)]).tolist()
    full_shape = {n: ((big_w[n].shape[0] * (N_CHIPS if ax == 0 else 1), big_w[n].shape[1] * (N_CHIPS if ax == 1 else 1)))
                  for n, ax in _BIG}

    def flat(tree, dtype):
        return jnp.concatenate([tree[n].astype(dtype).reshape(-1) for n, _ in _BIG]).reshape(-1, LANES)

    gathered = allgather_shards(flat(big_w, BF16)).reshape(N_CHIPS, -1)
    wfull = {n: _unflat_shards(gathered[:, offs[i]:offs[i + 1]], full_shape[n], ax) for i, (n, ax) in enumerate(_BIG)}
    lb_rows = lb_logits.reshape(4, LANES) * (ci == 0).astype(F32)
    lb_pad = lax.dynamic_update_slice(jnp.zeros((8, 1024), F32), lb_rows, (0, chip * LANES))
    lb_full = allreduce_small(lb_pad, "gather_lb")[:4, :HG_W]
    l_f0, l_f1, l_b0, l_b1 = (lb_full[i:i + 1] for i in range(4))

    n1 = norm1_w.reshape(1, d)
    n2 = norm2_w.reshape(1, d)
    nf = final_norm_w.reshape(1, d)
    (h1,) = _rows(lambda a, w: ((_rms(a, w),), ()), "norm1", [_full(x2d)], [n1], [(d, BF16)])
    u = mm_rows(h1, wfull["w_in"], "mm_in")
    o_f, ss_f = gla_fwd(u, l_f0, l_f1, 1, False, "gla_fwd_f")
    o_b, ss_b = gla_fwd(u, l_b0, l_b1, 2, True, "gla_fwd_b")

    cos2, sin2 = _rope_tables(t)
    cq, sq = jnp.tile(cos2, (1, ATT_HEADS)), jnp.tile(sin2, (1, ATT_HEADS))
    ck, sk = jnp.tile(cos2, (1, ATT_KV)), jnp.tile(sin2, (1, ATT_KV))
    qw, kw = _pad128(q_norm_w.reshape(1, ATT_DH)), _pad128(k_norm_w.reshape(1, ATT_DH))
    gq, gk = _group_sum_matrix(ATT_QW), _group_sum_matrix(ATT_KVW)
    tq, tk = _tile_matrix(ATT_QW), _tile_matrix(ATT_KVW)
    prep_rows = [(u, ATT_QW, 5), (u, ATT_KVW, 24), _full(cq), _full(sq), _full(ck), _full(sk)]
    prep_consts = [qw, kw, gq, gk, tq, tk]
    q_rot, k_rot, v_b = _rows(
        lambda aq, ak, av, *rest: ((*_att_prep_fn(aq, ak, *rest), av), ()), "att_prep",
        [prep_rows[0], prep_rows[1], (u, ATT_KVW, 25)] + prep_rows[2:], prep_consts,
        [(ATT_QW, BF16), (ATT_KVW, BF16), (ATT_KVW, BF16)], tm=256)
    q8, k2, v2 = _heads_major(q_rot, ATT_HEADS), _heads_major(k_rot, ATT_KV), _heads_major(v_b, ATT_KV)
    q_t, q_s = _strips_t(q8), _strips(q8)
    o_t, lse = fa_fwd(q_t, k2, _blocks_t(v2, FA_K_FWD))
    o_att = _tokens_major(o_t.transpose(0, 1, 3, 2), ATT_HEADS, t)

    hgw = hg_norm_w.reshape(1, HG_D)
    attw = att_norm_w.reshape(1, ATT_QW)
    mix_rows = [_full(o_f), _full(o_b), (u, HG_W, 4), _full(o_att)]
    (mix,) = _rows(lambda *a: ((_post_mix_fn(*a),), ()), "post_mix", mix_rows, [hgw, attw], [(d, BF16)], tm=256)
    x1 = mm_rows(mix, wfull["w_out"], "mm_out", res=x2d)
    (h2,) = _rows(lambda a, w: ((_rms(a, w),), ()), "norm2", [_full(x1)], [n2], [(d, BF16)])
    gu = mm_rows(h2, wfull["w_gate_up"], "mm_gate_up")
    dff = gu.shape[1] // 2
    (act,) = _rows(lambda g_, u_: ((_swiglu_fn(g_, u_),), ()), "swiglu", [(gu, dff, 0), (gu, dff, 1)], [],
                   [(dff, BF16)], tm=256)
    x2 = mm_rows(act, wfull["w_down"], "mm_down", res=x1)

    def loss_bwd_fn(a, tg, w):
        val, vjp = jax.vjp(lambda a_, w_: _loss_fn(a_, w_, tg), a, w)
        da, dw = vjp(jnp.ones((1, 1), F32))
        return (da, da), (dw, jnp.broadcast_to(val, (1, LANES)))

    dx2, dx2b, g_final, loss_part = _rows(loss_bwd_fn, "loss_bwd", [_full(x2), _full(tgt)], [nf],
                                          [(d, F32), (d, BF16)], [d, LANES], tm=256)
    dact = mm_rows(dx2b, wfull["w_down"], "mm_down_dx", trans_b=True)
    g_w_down = mm_tn(act, dx2b, "mm_down_dw")

    def swiglu_bwd_fn(g_, u_, da):
        _, vjp = jax.vjp(_swiglu_fn, g_, u_)
        dg, du_ = vjp(da)
        return (jnp.concatenate([dg, du_], axis=1),), ()

    (dgu,) = _rows(swiglu_bwd_fn, "swiglu_bwd", [(gu, dff, 0), (gu, dff, 1), _full(dact)], [],
                   [(2 * dff, BF16)], tm=256)
    dh2 = mm_rows(dgu, wfull["w_gate_up"], "mm_gate_up_dx", trans_b=True)
    g_w_gate_up = mm_tn(h2, dgu, "mm_gate_up_dw")

    def norm_bwd_fn(a, dh, dres, w):
        _, vjp = jax.vjp(_rms, a, w)
        da, dw = vjp(dh)
        da = da + dres
        return (da, da), (dw,)

    dx1, dx1b, g_norm2 = _rows(norm_bwd_fn, "norm2_bwd", [_full(x1), _full(dh2), _full(dx2)], [n2],
                               [(d, F32), (d, BF16)], [d], tm=256)
    dmix = mm_rows(dx1b, wfull["w_out"], "mm_out_dx", trans_b=True)
    g_w_out = mm_tn(mix, dx1b, "mm_out_dw")

    def post_mix_bwd_fn(of, ob, ug, oa, dm, hgw_, attw_, gq_):
        _, vjp = jax.vjp(_post_mix_fn, of, ob, ug, oa, hgw_, attw_)
        dof, _, dug, doa, dhgw, dattw = vjp(dm)
        return (dof, dug, doa, _xdot(doa * oa, gq_)), (dhgw, dattw)

    do_hg, du_g, do_att, dsum, g_hg, g_att = _rows(
        post_mix_bwd_fn, "post_mix_bwd", mix_rows + [_full(dmix)], [hgw, attw, gq],
        [(HG_W, F32), (HG_W, BF16), (ATT_QW, BF16), (ATT_QW, F32)], [HG_D, ATT_QW], tm=256)

    do8 = _heads_major(do_att, ATT_HEADS)
    ns, nq = lse.shape[1], lse.shape[3]
    d_col = _heads_major(dsum, ATT_HEADS)[:, :, 0].reshape(ATT_KV, ns, nq, 1)
    dq_s, dk_t, dv_t = fa_bwd(q_s, q_t, _strips(do8), _strips_t(do8), lse.reshape(ATT_KV, ns, nq, 1), d_col, k2,
                              _blocks_t(k2, FA_K_BWD), _blocks_t(v2, FA_K_BWD))
    dq8, dk2, dv2 = dq_s.reshape(ATT_KV, ns * nq, ATT_DH), _unblocks_t(dk_t), _unblocks_t(dv_t)
    dq_rot = _tokens_major(dq8, ATT_HEADS, t)
    dk_rot = _tokens_major(dk2, ATT_KV, t)
    dv_tok = _tokens_major(dv2, ATT_KV, t)

    def att_prep_bwd_fn(aq, ak, cq_, sq_, ck_, sk_, dq, dk, qw_, kw_, gq_, gk_, tq_, tk_):
        _, vjp = jax.vjp(lambda a, b, c_, e: _att_prep_fn(a, b, cq_, sq_, ck_, sk_, c_, e, gq_, gk_, tq_, tk_),
                         aq, ak, qw_, kw_)
        daq, dak, dqw, dkw = vjp((dq, dk))
        return (daq, dak), (dqw, dkw)

    da_q, da_k, g_q, g_k = _rows(att_prep_bwd_fn, "att_prep_bwd", prep_rows + [_full(dq_rot), _full(dk_rot)],
                                 prep_consts, [(ATT_QW, BF16), (ATT_KVW, BF16)], [LANES, LANES], tm=256)

    dq_f, df_f, di_f, dl_f0, dl_f1 = gla_bwd(u, l_f0, l_f1, ss_f, do_hg, 1, False, "gla_bwd_f")
    dq_b, df_b, di_b, dl_b0, dl_b1 = gla_bwd(u, l_b0, l_b1, ss_b, do_hg, 2, True, "gla_bwd_b")

    def assemble_fn(qf, qb, ff, fb, i_f, i_b, dg, aq, ak, av):
        parts = [qf + qb, ff, fb, i_f + i_b, dg.astype(F32), aq.astype(F32), ak.astype(F32), av]
        return (jnp.concatenate(parts, axis=1),), ()

    (du,) = _rows(assemble_fn, "assemble_du",
                  [_full(a) for a in (dq_f, dq_b, df_f, df_b, di_f, di_b, du_g, da_q, da_k, dv_tok)], [],
                  [(u.shape[1], BF16)], tm=256)
    dh1 = mm_rows(du, wfull["w_in"], "mm_in_dx", trans_b=True)
    g_w_in = mm_tn(h1, du, "mm_in_dw")
    grad_x, _, g_norm1 = _rows(norm_bwd_fn, "norm1_bwd", [_full(x2d), _full(dh1), _full(dx1)], [n1],
                               [(d, F32), (d, BF16)], [d], tm=256)

    gfull = {"w_in": g_w_in, "w_out": g_w_out, "w_gate_up": g_w_gate_up, "w_down": g_w_down}
    gflat = jnp.concatenate([_flat_shards(gfull[n], ax) for n, ax in _BIG], axis=1).reshape(N_CHIPS, -1, LANES)
    chip_part = add_my_half(gflat, rs_siblings(gflat))
    g_shard = share_halves(sum_chips(rs_chips(chip_part)))
    d_shard, m_shard, v_shard = adamw(flat(big_w, F32), g_shard, flat(big_m, F32), flat(big_v, F32), "adamw_big")

    def unflat(a):
        a = a.reshape(-1)
        return {n: a[offs[i]:offs[i + 1]].reshape((1,) + big_w[n].shape) for i, (n, _) in enumerate(_BIG)}

    g_big, d_big, m_big, v_big = unflat(g_shard), unflat(d_shard), unflat(m_shard), unflat(v_shard)

    small = [g_norm1, g_norm2, g_final, g_att, g_hg, g_q, g_k, loss_part, dl_f0, dl_f1, dl_b0, dl_b1]
    packed = _pack_rows(small)
    packed += [jnp.zeros((1, 1024), F32)] * (8 - len(packed))
    tot = allreduce_small(jnp.concatenate(packed, axis=0), "allreduce_small")
    s_norm1, s_norm2, s_final = tot[0:1], tot[1:2], tot[2:3]
    s_att, s_hg, s_q, s_k = tot[3:4, 0:512], tot[3:4, 512:640], tot[3:4, 640:704], tot[3:4, 768:832]
    loss = tot[3, 896]
    s_lb = jnp.concatenate([tot[4:5, 0:512], tot[4:5, 512:1024], tot[5:6, 0:512], tot[5:6, 512:1024]], axis=0)
    s_lb = lax.dynamic_slice(s_lb, (0, chip * LANES), (4, LANES)).reshape(1, 512)

    names = ["norm1_w", "lb_logits", "hg_norm_w", "q_norm_w", "k_norm_w", "att_norm_w", "norm2_w", "final_norm_w"]
    g_small = dict(zip(names, [s_norm1, s_lb, s_hg, s_q, s_k, s_att, s_norm2, s_final]))
    w_small = dict(zip(names, [norm1_w, lb_logits, hg_norm_w, q_norm_w, k_norm_w, att_norm_w, norm2_w, final_norm_w]))
    m_small = dict(zip(names, [m_norm1_w, m_lb_logits, m_hg_norm_w, m_q_norm_w, m_k_norm_w, m_att_norm_w, m_norm2_w, m_final_norm_w]))
    v_small = dict(zip(names, [v_norm1_w, v_lb_logits, v_hg_norm_w, v_q_norm_w, v_k_norm_w, v_att_norm_w, v_norm2_w, v_final_norm_w]))

    def pack_small(tree):
        rows = _pack_rows([tree[n].reshape(1, -1) for n in names])
        rows += [jnp.zeros((1, 1024), F32)] * (8 - len(rows))
        return jnp.concatenate(rows, axis=0)

    d_s, m_s, v_s = adamw(pack_small(w_small), pack_small(g_small), pack_small(m_small), pack_small(v_small), "adamw_small")

    def unpack_small(a):
        out, r, used = {}, 0, 0
        for n in names:
            size = w_small[n].size
            if used + size > 1024:
                r, used = r + 1, 0
            out[n] = a[r, used:used + size].reshape(w_small[n].shape)
            used += size
        return out

    d_sm, m_sm, v_sm = unpack_small(d_s), unpack_small(m_s), unpack_small(v_s)
    g_sm = {n: g_small[n].reshape(w_small[n].shape) for n in names}

    order = ["norm1_w", "w_in", "lb_logits", "hg_norm_w", "q_norm_w", "k_norm_w", "att_norm_w", "w_out", "norm2_w",
             "w_gate_up", "w_down", "final_norm_w"]

    def pick(small_tree, big_tree):
        return [big_tree[n] if n in big_tree else small_tree[n] for n in order]

    return (loss, grad_x.reshape(x.shape), *pick(g_sm, g_big), *pick(d_sm, d_big), *pick(m_sm, m_big),
            *pick(v_sm, v_big))
```

```python
import functools

import numpy as np
import jax
import jax.numpy as jnp
from jax import lax
from jax.experimental import pallas as pl
from jax.experimental.pallas import tpu as pltpu

F32 = jnp.float32
BF16 = jnp.bfloat16
MESH = pl.DeviceIdType.MESH
HIGHEST = lax.Precision.HIGHEST

EPS = 1e-6
GRID_W = 64
HG_HEADS = 4
HG_D = 128
HG_W = HG_HEADS * HG_D
CHUNK = 64
ATT_HEADS = 8
ATT_KV = 2
ATT_GROUP = ATT_HEADS // ATT_KV
ATT_DH = 64
ATT_QW = ATT_HEADS * ATT_DH
ATT_KVW = ATT_KV * ATT_DH
ROPE_THETA = 10000.0
N_CHIPS = 4
N_DEV = 8

ADAM_LR = 0.001
ADAM_B1 = 0.9
ADAM_B2 = 0.999
ADAM_EPS = 1e-08
ADAM_WD = 0.01
ADAM_STEP = 10

VMEM_LIMIT = 52 * 1024 * 1024
LANES = 128
TOK = 256


def _params(sem=None):
    return pltpu.CompilerParams(dimension_semantics=sem, vmem_limit_bytes=VMEM_LIMIT)


def _dg(a, b, ca, cb):
    return lax.dot_general(a.astype(BF16), b.astype(BF16), (((ca,), (cb,)), ((), ())),
                           preferred_element_type=F32)


@jax.custom_vjp
def dot_nn(a, b):
    return _dg(a, b, 1, 0)


def _dot_nn_fwd(a, b):
    return _dg(a, b, 1, 0), (a, b)


def _dot_nn_bwd(res, g):
    a, b = res
    return _dg(g, b, 1, 1), _dg(a, g, 0, 0)


dot_nn.defvjp(_dot_nn_fwd, _dot_nn_bwd)


@jax.custom_vjp
def dot_nt(a, b):
    return _dg(a, b, 1, 1)


def _dot_nt_fwd(a, b):
    return _dg(a, b, 1, 1), (a, b)


def _dot_nt_bwd(res, g):
    a, b = res
    return _dg(g, b, 1, 0), _dg(g, a, 0, 0)


dot_nt.defvjp(_dot_nt_fwd, _dot_nt_bwd)


@jax.custom_vjp
def dot_tn(a, b):
    return _dg(a, b, 0, 0)


def _dot_tn_fwd(a, b):
    return _dg(a, b, 0, 0), (a, b)


def _dot_tn_bwd(res, g):
    a, b = res
    return _dg(b, g, 1, 1), _dg(a, g, 1, 0)


dot_tn.defvjp(_dot_tn_fwd, _dot_tn_bwd)


def _xdot(a, m):
    return jnp.dot(a, m, precision=HIGHEST, preferred_element_type=F32)


@jax.custom_vjp
def swap_pairs(y):
    n = y.shape[-1]
    lane = lax.broadcasted_iota(jnp.int32, y.shape, 1)
    nxt = pltpu.roll(y, n - 1, 1)
    prv = pltpu.roll(y, 1, 1)
    return jnp.where(lane % 2 == 0, nxt, prv)


def _swap_fwd(y):
    return swap_pairs(y), None


def _swap_bwd(_, g):
    return (swap_pairs(g),)


swap_pairs.defvjp(_swap_fwd, _swap_bwd)


def _rms(x, w):
    return x * lax.rsqrt(jnp.mean(x * x, axis=-1, keepdims=True) + EPS) * w


def _sigmoid(x):
    return jax.nn.sigmoid(x)


def _rows(fn, name, t, tm, ins, consts, outs, accs=()):
    n_r, n_c, n_o, n_a = len(ins), len(consts), len(outs), len(accs)

    def body(*refs):
        r = refs[:n_r]
        c = refs[n_r:n_r + n_c]
        o = refs[n_r + n_c:n_r + n_c + n_o]
        a = refs[n_r + n_c + n_o:]
        ro, ao = fn(*[x[...] for x in r], *[x[...] for x in c])
        for ref, val in zip(o, ro):
            if isinstance(val, (list, tuple)):
                for h, piece in enumerate(val):
                    ref[h, 0] = piece.astype(ref.dtype)
            else:
                ref[...] = val.astype(ref.dtype)
        if n_a:
            @pl.when(pl.program_id(0) == 0)
            def _():
                for ref in a:
                    ref[...] = jnp.zeros(ref.shape, F32)
            for ref, val in zip(a, ao):
                ref[...] += val

    in_specs = [s for _, s in ins]
    in_specs += [pl.BlockSpec(c.shape, lambda i, nd=c.ndim: (0,) * nd) for c in consts]
    out_specs = [s for _, s in outs] + [pl.BlockSpec((1, w), lambda i: (0, 0)) for w in accs]
    out_shape = [s for s, _ in outs] + [jax.ShapeDtypeStruct((1, w), F32) for w in accs]
    return pl.pallas_call(
        body, name=name, grid=(t // tm,), in_specs=in_specs, out_specs=out_specs, out_shape=out_shape,
        compiler_params=_params(("arbitrary",)),
    )(*[a for a, _ in ins], *consts)


def _rin(a, tm, width=None, cb=0):
    width = a.shape[1] if width is None else width
    return a, pl.BlockSpec((tm, width), lambda i, cb=cb: (i, cb))


def _rout(t, tm, width, dtype):
    return jax.ShapeDtypeStruct((t, width), dtype), pl.BlockSpec((tm, width), lambda i: (i, 0))


def _tin(a):
    return a, pl.BlockSpec((a.shape[0], 1) + a.shape[2:], lambda i: (0, i, 0, 0))


def _tout(heads, nt, r, c, dtype):
    return jax.ShapeDtypeStruct((heads, nt, r, c), dtype), pl.BlockSpec((heads, 1, r, c), lambda i: (0, i, 0, 0))


def _pick(n, cap, mult):
    best = None
    for d in range(mult, min(n, cap) + 1, mult):
        if n % d == 0:
            best = d
    return best if best is not None else n


def mm_rows(a, b, name, trans_b=False, res=None, out_dtype=F32):
    m, k = a.shape
    stacked = b.ndim == 3
    if stacked:
        tn = b.shape[2]
        n = b.shape[0] * tn
    else:
        n = b.shape[0] if trans_b else b.shape[1]
        tn = _pick(n, 3328, LANES)
    tm = _pick(m, 512 if k <= 3072 else 256, 8)
    has_res = res is not None

    def body(*refs):
        if has_res:
            a_ref, b_ref, r_ref, o_ref = refs
        else:
            a_ref, b_ref, o_ref = refs
        bv = b_ref[0] if stacked else b_ref[...]
        acc = _dg(a_ref[...], bv, 1, 1 if trans_b else 0)
        if has_res:
            acc = acc + r_ref[...]
        o_ref[...] = acc.astype(o_ref.dtype)

    in_specs = [pl.BlockSpec((tm, k), lambda j, i: (i, 0))]
    if stacked:
        in_specs.append(pl.BlockSpec((1, k, tn), lambda j, i: (j, 0, 0)))
    elif trans_b:
        in_specs.append(pl.BlockSpec((tn, k), lambda j, i: (j, 0)))
    else:
        in_specs.append(pl.BlockSpec((k, tn), lambda j, i: (0, j)))
    args = [a, b]
    if has_res:
        in_specs.append(pl.BlockSpec((tm, tn), lambda j, i: (i, j)))
        args.append(res)
    return pl.pallas_call(
        body, name=name, grid=(n // tn, m // tm), in_specs=in_specs,
        out_specs=pl.BlockSpec((tm, tn), lambda j, i: (i, j)),
        out_shape=jax.ShapeDtypeStruct((m, n), out_dtype),
        compiler_params=_params(("arbitrary", "arbitrary")),
    )(*args)


def mm_rows_kstack(a, b, name):
    m = a.shape[0]
    s, n, ks = b.shape
    tm = _pick(m, 512, 8)

    def body(a_ref, b_ref, o_ref):
        @pl.when(pl.program_id(1) == 0)
        def _():
            o_ref[...] = jnp.zeros(o_ref.shape, F32)
        o_ref[...] += _dg(a_ref[...], b_ref[0], 1, 1)

    return pl.pallas_call(
        body, name=name, grid=(m // tm, s),
        in_specs=[pl.BlockSpec((tm, ks), lambda i, kk: (i, kk)), pl.BlockSpec((1, n, ks), lambda i, kk: (kk, 0, 0))],
        out_specs=pl.BlockSpec((tm, n), lambda i, kk: (i, 0)),
        out_shape=jax.ShapeDtypeStruct((m, n), F32),
        compiler_params=_params(("arbitrary", "arbitrary")),
    )(a, b)


def mm_tn(a, b, name, col_shards=None):
    t, m = a.shape
    n = b.shape[1]
    tm = _pick(m, 512, LANES)
    tn = n // col_shards if col_shards else _pick(n, 3328, LANES)
    tk = _pick(t, 1024, 8)

    def body(a_ref, b_ref, o_ref):
        @pl.when(pl.program_id(2) == 0)
        def _():
            o_ref[...] = jnp.zeros(o_ref.shape, F32)
        acc = _dg(a_ref[...], b_ref[...], 0, 0)
        if col_shards:
            o_ref[0] += acc
        else:
            o_ref[...] += acc

    if col_shards:
        out_spec = pl.BlockSpec((1, tm, tn), lambda i, j, kk: (j, i, 0))
        out_shape = jax.ShapeDtypeStruct((col_shards, m, tn), F32)
    else:
        out_spec = pl.BlockSpec((tm, tn), lambda i, j, kk: (i, j))
        out_shape = jax.ShapeDtypeStruct((m, n), F32)
    return pl.pallas_call(
        body, name=name, grid=(m // tm, n // tn, t // tk),
        in_specs=[pl.BlockSpec((tk, tm), lambda i, j, kk: (kk, i)),
                  pl.BlockSpec((tk, tn), lambda i, j, kk: (kk, j))],
        out_specs=out_spec, out_shape=out_shape,
        compiler_params=_params(("arbitrary", "arbitrary", "arbitrary")),
    )(a, b)


def _lower_bound(l0, l1):
    m = jnp.maximum(l0, l1)
    e0 = jnp.exp(l0 - m)
    e1 = jnp.exp(l1 - m)
    return e0 / (e0 + e1)


def _gla_consts(rev):
    ri = lax.broadcasted_iota(jnp.int32, (CHUNK, CHUNK), 0)
    ci = lax.broadcasted_iota(jnp.int32, (CHUNK, CHUNK), 1)
    keep = (ci >= ri) if rev else (ci <= ri)
    tm = keep.astype(F32)
    if rev:
        rvec = (ci >= CHUNK // 2).astype(F32)
    else:
        rvec = (ci <= CHUNK // 2 - 1).astype(F32)
    m3 = jnp.concatenate([tm, tm - rvec, 1.0 - tm], axis=0)
    return keep, m3


def _gla_block(uq, uf, ui, l0, l1, st_in, rev):
    ncb = uq.shape[0] // CHUNK
    heads = range(HG_HEADS)
    keep, m3 = _gla_consts(rev)
    lb = _lower_bound(l0, l1)
    q = uq * _sigmoid(uq)
    k = (1.0 - lb) * _sigmoid(-uf)
    g = jnp.log(lb + (1.0 - lb) * _sigmoid(uf))

    def rows(a, c):
        return a[c * CHUNK:(c + 1) * CHUNK]

    def head(a, h):
        return a[:, h * HG_D:(h + 1) * HG_D]

    b3 = [_xdot(m3, rows(g, c)) for c in range(ncb)]
    q_in, k_in, q_b, k_d, decay = [], [], [], [], []
    for c in range(ncb):
        b, bmr, lmb = b3[c][0:CHUNK], b3[c][CHUNK:2 * CHUNK], b3[c][2 * CHUNK:3 * CHUNK]
        qc, kc = rows(q, c), rows(k, c)
        q_in.append(qc * jnp.exp(bmr))
        k_in.append(kc * jnp.exp(-bmr))
        q_b.append(qc * jnp.exp(b))
        k_d.append(kc * jnp.exp(lmb))
        decay.append(jnp.exp(jnp.sum(rows(g, c), axis=0, keepdims=True)))
    scores = [[jnp.where(keep, dot_nt(head(q_in[c], h), head(k_in[c], h)), 0.0) for h in heads] for c in range(ncb)]
    o_intra = [[dot_nn(scores[c][h], head(rows(ui, c), h)) for h in heads] for c in range(ncb)]
    contrib = [[dot_tn(head(rows(ui, c), h), head(k_d[c], h)) for h in heads] for c in range(ncb)]
    st = list(st_in)
    o_rows = [None] * ncb
    for c in (reversed(range(ncb)) if rev else range(ncb)):
        parts = []
        for h in heads:
            parts.append(o_intra[c][h] + dot_nt(head(q_b[c], h), st[h]))
            st[h] = st[h] * head(decay[c], h) + contrib[c][h]
        o_rows[c] = jnp.concatenate(parts, axis=1)
    return jnp.concatenate(o_rows, axis=0), tuple(st)


def _gla_blocks(t):
    tb = min(512, t)
    return tb, t // tb


def gla_fwd(u, l0, l1, fcol, rev, name):
    t = u.shape[0]
    tb, nb = _gla_blocks(t)

    def blk(i):
        return (nb - 1 - i) if rev else i

    def body(uq_ref, uf_ref, ui_ref, l0_ref, l1_ref, o_ref, ss_ref, st_ref):
        @pl.when(pl.program_id(0) == 0)
        def _():
            st_ref[...] = jnp.zeros(st_ref.shape, F32)
        ss_ref[0] = st_ref[...]
        o, st_out = _gla_block(uq_ref[...], uf_ref[...], ui_ref[...], l0_ref[...], l1_ref[...],
                               tuple(st_ref[h] for h in range(HG_HEADS)), rev)
        o_ref[...] = o
        for h in range(HG_HEADS):
            st_ref[h] = st_out[h]

    row = lambda cb: pl.BlockSpec((tb, HG_W), lambda i: (blk(i), cb))
    vec = pl.BlockSpec((1, HG_W), lambda i: (0, 0))
    return pl.pallas_call(
        body, name=name, grid=(nb,),
        in_specs=[row(0), row(fcol), row(3), vec, vec],
        out_specs=[pl.BlockSpec((tb, HG_W), lambda i: (blk(i), 0)),
                   pl.BlockSpec((1, HG_HEADS, HG_D, HG_D), lambda i: (blk(i), 0, 0, 0))],
        out_shape=[jax.ShapeDtypeStruct((t, HG_W), F32),
                   jax.ShapeDtypeStruct((nb, HG_HEADS, HG_D, HG_D), F32)],
        scratch_shapes=[pltpu.VMEM((HG_HEADS, HG_D, HG_D), F32)],
        compiler_params=_params(("arbitrary",)),
    )(u, u, u, l0, l1)


def gla_bwd(u, l0, l1, ss, do, fcol, rev, name):
    t = u.shape[0]
    tb, nb = _gla_blocks(t)

    def blk(i):
        return i if rev else (nb - 1 - i)

    def body(uq_ref, uf_ref, ui_ref, l0_ref, l1_ref, ss_ref, do_ref,
             dq_ref, df_ref, di_ref, dl0_ref, dl1_ref, dst_ref):
        @pl.when(pl.program_id(0) == 0)
        def _():
            dst_ref[...] = jnp.zeros(dst_ref.shape, F32)
            dl0_ref[...] = jnp.zeros(dl0_ref.shape, F32)
            dl1_ref[...] = jnp.zeros(dl1_ref.shape, F32)
        heads = range(HG_HEADS)
        _, vjp = jax.vjp(functools.partial(_gla_block, rev=rev), uq_ref[...], uf_ref[...], ui_ref[...],
                         l0_ref[...], l1_ref[...], tuple(ss_ref[0, h] for h in heads))
        dq, df, di, dl0, dl1, dst = vjp((do_ref[...], tuple(dst_ref[h] for h in heads)))
        dq_ref[...] = dq
        df_ref[...] = df
        di_ref[...] = di
        dl0_ref[...] += dl0
        dl1_ref[...] += dl1
        for h in heads:
            dst_ref[h] = dst[h]

    row = lambda cb: pl.BlockSpec((tb, HG_W), lambda i: (blk(i), cb))
    vec = pl.BlockSpec((1, HG_W), lambda i: (0, 0))
    orow = pl.BlockSpec((tb, HG_W), lambda i: (blk(i), 0))
    return pl.pallas_call(
        body, name=name, grid=(nb,),
        in_specs=[row(0), row(fcol), row(3), vec, vec,
                  pl.BlockSpec((1, HG_HEADS, HG_D, HG_D), lambda i: (blk(i), 0, 0, 0)), orow],
        out_specs=[orow, orow, orow, vec, vec],
        out_shape=[jax.ShapeDtypeStruct((t, HG_W), F32)] * 3 + [jax.ShapeDtypeStruct((1, HG_W), F32)] * 2,
        scratch_shapes=[pltpu.VMEM((HG_HEADS, HG_D, HG_D), F32)],
        compiler_params=_params(("arbitrary",)),
    )(u, u, u, l0, l1, ss, do)


def _rope_tables(t):
    rows = t // GRID_W
    row = jnp.repeat(jnp.arange(rows), GRID_W).astype(F32)
    col = jnp.tile(jnp.arange(GRID_W), rows).astype(F32)
    axis_dim = ATT_DH // 2
    freqs = ROPE_THETA ** (-jnp.arange(0, axis_dim, 2, dtype=F32) / axis_dim)
    ang = jnp.concatenate([row[:, None] * freqs, col[:, None] * freqs], axis=-1)
    cos2 = jnp.repeat(jnp.cos(ang), 2, axis=-1)
    sin2 = jnp.repeat(jnp.sin(ang), 2, axis=-1) * jnp.tile(jnp.array([-1.0, 1.0], F32), ATT_DH // 2)
    return cos2, sin2


def _group_sum_matrix(width):
    idx = np.arange(width) // ATT_DH
    return jnp.asarray((idx[:, None] == idx[None, :]).astype(np.float32))


def _tile_matrix(width):
    m = np.zeros((LANES, width), np.float32)
    m[np.arange(width) % ATT_DH, np.arange(width)] = 1.0
    return jnp.asarray(m)


def _tile_w(w128, tile_m):
    w8 = jnp.broadcast_to(w128, (8, LANES))
    return jnp.sum(_xdot(w8, tile_m), axis=0, keepdims=True) * 0.125


def _head_norm_rope(a, w128, cos_t, sin_t, gsum, tile_m, scale):
    ssq = _xdot(a * a, gsum)
    y = a * lax.rsqrt(ssq * (1.0 / ATT_DH) + EPS) * _tile_w(w128, tile_m)
    return (y * cos_t + swap_pairs(y) * sin_t) * scale


def _att_prep_fn(aq, ak, cq, sq, ck, sk, qw, kw, gq, gk, tq, tk):
    q = _head_norm_rope(aq, qw, cq, sq, gq, tq, ATT_DH ** -0.5)
    k = _head_norm_rope(ak, kw, ck, sk, gk, tk, 1.0)
    return q, k


def _head_t(x, heads):
    xt = x.T
    return [xt[h * ATT_DH:(h + 1) * ATT_DH] for h in range(heads)]


def _head_s(x, heads):
    lane = lax.broadcasted_iota(jnp.int32, (x.shape[0], LANES), 1)
    out = []
    for h in range(heads):
        pair = x[:, (h // 2) * LANES:(h // 2 + 1) * LANES]
        if h % 2:
            pair = pltpu.roll(pair, ATT_DH, 1)
        out.append(jnp.where(lane < ATT_DH, pair, 0.0))
    return out


def _from_head_t(tile):
    return jnp.concatenate([tile[h, 0] for h in range(tile.shape[0])], axis=0).T


def _pad_rows(a):
    return jnp.concatenate([a, jnp.zeros(a.shape, a.dtype)], axis=0)


def _col_bcast(row_vec):
    return jnp.broadcast_to(row_vec, (LANES, row_vec.shape[1])).T


FA_K_FWD = 1024
FA_K_BWD = 512
FA_STRIPS = 4


def _key_block(ref, j, tiles):
    return jnp.concatenate([ref[0, j * tiles + i] for i in range(tiles)], axis=1)


def fa_fwd(qt, ks, vt):
    _, ns, dh, nq = qt.shape
    lk = ks.shape[1]
    bk = min(FA_K_FWD, lk)
    nk = lk // bk
    spg = min(FA_STRIPS, ns)

    def body(q_ref, k_ref, v_ref, o_ref, lse_ref):
        qs = [_pad_rows(q_ref[0, c]) for c in range(spg)]

        def keys(j):
            return k_ref[0, pl.ds(pl.multiple_of(j * bk, bk), bk), :]

        def step(j, carry):
            st0, stats = carry
            kb = keys(j)
            vb = _key_block(v_ref, j, bk // nq)
            sts = [st0] + [_dg(kb, qs[c], 1, 0) for c in range(1, spg)]
            out = []
            for c in range(spg):
                m, l, acc = stats[c]
                m_new = jnp.maximum(m, jnp.max(sts[c], axis=0, keepdims=True))
                alpha = jnp.exp(m - m_new)
                p = jnp.exp(sts[c] - m_new)
                l = alpha * l + jnp.sum(p, axis=0, keepdims=True)
                if c == spg - 1:
                    st0 = _dg(keys(jnp.minimum(j + 1, nk - 1)), qs[0], 1, 0)
                acc = alpha * acc + _dg(vb, p, 1, 0)
                out.append((m_new, l, acc))
            return st0, tuple(out)

        init = tuple((jnp.full((1, nq), -jnp.inf, F32), jnp.zeros((1, nq), F32), jnp.zeros((dh, nq), F32))
                     for _ in range(spg))
        _, res = lax.fori_loop(0, nk, step, (_dg(keys(0), qs[0], 1, 0), init))
        for c in range(spg):
            m, l, acc = res[c]
            o_ref[0, c] = acc / l
            lse_ref[0, c] = _col_bcast(m + jnp.log(l))

    return pl.pallas_call(
        body, name="fa_fwd", grid=(ATT_KV, ns // spg),
        in_specs=[pl.BlockSpec((1, spg, dh, nq), lambda g, i: (g, i, 0, 0)),
                  pl.BlockSpec((1, lk, LANES), lambda g, i: (g, 0, 0)),
                  pl.BlockSpec((1, lk // nq, dh, nq), lambda g, i: (g, 0, 0, 0))],
        out_specs=[pl.BlockSpec((1, spg, dh, nq), lambda g, i: (g, i, 0, 0)),
                   pl.BlockSpec((1, spg, nq, LANES), lambda g, i: (g, i, 0, 0))],
        out_shape=[jax.ShapeDtypeStruct((ATT_KV, ns, dh, nq), F32),
                   jax.ShapeDtypeStruct((ATT_KV, ns, nq, LANES), F32)],
        compiler_params=_params(("arbitrary", "arbitrary")),
    )(qt, ks, vt)


def fa_bwd(qs, qt, dos, dot_, ot, lse, ks, kt, vt):
    _, ns, dh, nq = qt.shape
    lk = ks.shape[1]
    bk = min(FA_K_BWD, lk)
    nk = lk // bk
    tiles = bk // nq
    spg = min(FA_STRIPS, ns)
    nc = bk // LANES

    def body(qs_ref, qt_ref, dos_ref, dot_ref, ot_ref, lse_ref, ks_ref, kt_ref, vt_ref, dq_ref, dk_ref, dv_ref):
        @pl.when(pl.program_id(1) == 0)
        def _():
            dk_ref[...] = jnp.zeros(dk_ref.shape, F32)
            dv_ref[...] = jnp.zeros(dv_ref.shape, F32)

        strips = range(spg)
        lse_b = [lse_ref[0, c] for c in strips]
        d_b = [_col_bcast(jnp.sum(dot_ref[0, c].astype(F32) * ot_ref[0, c], axis=0, keepdims=True)) for c in strips]

        def step(j, dqs):
            ktb, vtb = _pad_rows(_key_block(kt_ref, j, tiles)), _pad_rows(_key_block(vt_ref, j, tiles))
            kb = ks_ref[0, pl.ds(pl.multiple_of(j * bk, bk), bk), :]
            prods = [(_dg(qs_ref[0, c], ktb, 1, 0), _dg(dos_ref[0, c], vtb, 1, 0)) for c in strips]
            out = []
            for c in strips:
                s, dp = prods[c]
                ps, dss = [], []
                for cc in range(nc):
                    sl = slice(cc * LANES, (cc + 1) * LANES)
                    pc = jnp.exp(s[:, sl] - lse_b[c])
                    ps.append(pc.astype(BF16))
                    dss.append((pc * (dp[:, sl] - d_b[c])).astype(BF16))
                p, ds = jnp.concatenate(ps, axis=1), jnp.concatenate(dss, axis=1)
                dv = _dg(dot_ref[0, c], p, 1, 0)
                dk = _dg(qt_ref[0, c], ds, 1, 0)
                for i in range(tiles):
                    dv_ref[0, j * tiles + i] += dv[:, i * nq:(i + 1) * nq]
                    dk_ref[0, j * tiles + i] += dk[:, i * nq:(i + 1) * nq]
                out.append(dqs[c] + _dg(ds, kb, 1, 0))
            return tuple(out)

        dqs = lax.fori_loop(0, nk, step, tuple(jnp.zeros((nq, LANES), F32) for _ in strips))
        for c in strips:
            dq_ref[0, c] = dqs[c].T[:dh]

    sspec = pl.BlockSpec((1, spg, nq, LANES), lambda g, i: (g, i, 0, 0))
    tspec = pl.BlockSpec((1, spg, dh, nq), lambda g, i: (g, i, 0, 0))
    kspec = pl.BlockSpec((1, lk // nq, dh, nq), lambda g, i: (g, 0, 0, 0))
    return pl.pallas_call(
        body, name="fa_bwd", grid=(ATT_KV, ns // spg),
        in_specs=[sspec, tspec, sspec, tspec, tspec, sspec, pl.BlockSpec((1, lk, LANES), lambda g, i: (g, 0, 0)),
                  kspec, kspec],
        out_specs=[tspec, kspec, kspec],
        out_shape=[jax.ShapeDtypeStruct((ATT_KV, ns, dh, nq), F32),
                   jax.ShapeDtypeStruct((ATT_KV, lk // nq, dh, nq), F32),
                   jax.ShapeDtypeStruct((ATT_KV, lk // nq, dh, nq), F32)],
        compiler_params=_params(("arbitrary", "arbitrary")),
    )(qs, qt, dos, dot_, ot, lse, ks, kt, vt)


def _post_mix_fn(of, ob, ug, oa, hgw, attw):
    o = of + ob
    parts = []
    for h in range(HG_HEADS):
        parts.append(_rms(o[:, h * HG_D:(h + 1) * HG_D], hgw))
    hg = jnp.concatenate(parts, axis=1) * (ug * _sigmoid(ug))
    return jnp.concatenate([hg, _rms(oa, attw)], axis=1)


def _swiglu_fn(gate, up):
    return gate * _sigmoid(gate) * up


def _loss_fn(x2, w, tgt):
    e = _rms(x2, w) - tgt
    return 0.5 * jnp.sum(jnp.mean(e * e, axis=-1, keepdims=True), axis=0, keepdims=True)


def _place():
    return lax.axis_index("x"), lax.axis_index("y"), lax.axis_index("c")


def _other_chips(x, y):
    return [(1 - x, y), (x, 1 - y), (1 - x, 1 - y)]


def _hbm_specs(n):
    return [pl.BlockSpec(memory_space=pl.ANY)] * n


def allgather_weights(shards):
    n = len(shards)

    def body(*refs):
        srcs, outs = refs[:n], refs[n:2 * n]
        ssem, rsem, lsem = refs[2 * n:]
        x, y, c = _place()
        k = 2 * x + y
        sib = (x, y, 1 - c)
        chips = _other_chips(x, y)

        def half(a, kk, cc):
            hr = srcs[a].shape[0] // 2
            return outs[a].at[kk, pl.ds(cc * hr, hr), :]

        def my_half(a):
            hr = srcs[a].shape[0] // 2
            return srcs[a].at[pl.ds(c * hr, hr), :]

        def copy(a, j, src_ref, dst_ref, to):
            return pltpu.make_async_remote_copy(src_ref=src_ref, dst_ref=dst_ref, send_sem=ssem.at[6 * a + j],
                                                recv_sem=rsem.at[6 * a + j], device_id=to, device_id_type=MESH)

        mine = [pltpu.make_async_copy(srcs[a], outs[a].at[k], lsem.at[a]) for a in range(n)]
        for cp in mine:
            cp.start()
        first = [copy(a, j, my_half(a), half(a, k, c), (px, py, c)) for a in range(n) for j, (px, py) in enumerate(chips)]
        for cp in first:
            cp.start()
        passed = []
        for a in range(n):
            for j, (px, py) in enumerate(chips):
                copy(a, j, my_half(a), half(a, 2 * px + py, c), (px, py, c)).wait_recv()
                cp = copy(a, 3 + j, half(a, 2 * px + py, c), half(a, 2 * px + py, c), sib)
                cp.start()
                passed.append(cp)
        for a in range(n):
            for j, (px, py) in enumerate(chips):
                copy(a, 3 + j, my_half(a), half(a, 2 * px + py, 1 - c), sib).wait_recv()
        for cp in first + passed:
            cp.wait_send()
        for cp in mine:
            cp.wait()

    return pl.pallas_call(
        body, name="allgather_weights", in_specs=_hbm_specs(n), out_specs=_hbm_specs(n),
        out_shape=[jax.ShapeDtypeStruct((N_CHIPS,) + s.shape, s.dtype) for s in shards],
        scratch_shapes=[pltpu.SemaphoreType.DMA((6 * n,)), pltpu.SemaphoreType.DMA((6 * n,)),
                        pltpu.SemaphoreType.DMA((n,))],
    )(*shards)


def rs_siblings(gs):
    n = len(gs)

    def body(*refs):
        srcs, outs = refs[:n], refs[n:2 * n]
        ssem, rsem = refs[2 * n:]
        x, y, c = _place()
        cps = []
        for a in range(n):
            hr = srcs[a].shape[1] // 2
            cp = pltpu.make_async_remote_copy(src_ref=srcs[a].at[:, pl.ds((1 - c) * hr, hr), :], dst_ref=outs[a],
                                              send_sem=ssem.at[a], recv_sem=rsem.at[a], device_id=(x, y, 1 - c),
                                              device_id_type=MESH)
            cp.start()
            cps.append(cp)
        for cp in cps:
            cp.wait()

    return pl.pallas_call(
        body, name="rs_siblings", in_specs=_hbm_specs(n), out_specs=_hbm_specs(n),
        out_shape=[jax.ShapeDtypeStruct((N_CHIPS, g.shape[1] // 2, g.shape[2]), F32) for g in gs],
        scratch_shapes=[pltpu.SemaphoreType.DMA((n,)), pltpu.SemaphoreType.DMA((n,))],
    )(*gs)


def rs_chips(pas):
    n = len(pas)

    def body(*refs):
        srcs, outs = refs[:n], refs[n:2 * n]
        ssem, rsem, lsem = refs[2 * n:]
        x, y, c = _place()
        k = 2 * x + y
        chips = _other_chips(x, y)
        mine = [pltpu.make_async_copy(srcs[a].at[k], outs[a].at[k], lsem.at[a]) for a in range(n)]
        for cp in mine:
            cp.start()
        sends = []
        for a in range(n):
            for j, (px, py) in enumerate(chips):
                cp = pltpu.make_async_remote_copy(src_ref=srcs[a].at[2 * px + py], dst_ref=outs[a].at[k],
                                                  send_sem=ssem.at[3 * a + j], recv_sem=rsem.at[3 * a + j],
                                                  device_id=(px, py, c), device_id_type=MESH)
                cp.start()
                sends.append(cp)
        for a in range(n):
            for j, (px, py) in enumerate(chips):
                pltpu.make_async_remote_copy(src_ref=srcs[a].at[k], dst_ref=outs[a].at[2 * px + py],
                                             send_sem=ssem.at[3 * a + j], recv_sem=rsem.at[3 * a + j],
                                             device_id=(px, py, c), device_id_type=MESH).wait_recv()
        for cp in sends:
            cp.wait_send()
        for cp in mine:
            cp.wait()

    return pl.pallas_call(
        body, name="rs_chips", in_specs=_hbm_specs(n), out_specs=_hbm_specs(n),
        out_shape=[jax.ShapeDtypeStruct(p.shape, F32) for p in pas],
        scratch_shapes=[pltpu.SemaphoreType.DMA((3 * n,)), pltpu.SemaphoreType.DMA((3 * n,)),
                        pltpu.SemaphoreType.DMA((n,))],
    )(*pas)


def share_halves(rhs):
    n = len(rhs)

    def body(*refs):
        srcs, outs = refs[:n], refs[n:2 * n]
        ssem, rsem, lsem = refs[2 * n:]
        x, y, c = _place()
        cps, mine = [], []
        for a in range(n):
            hr = srcs[a].shape[0]
            dst = outs[a].at[pl.ds(c * hr, hr), :]
            lc = pltpu.make_async_copy(srcs[a], dst, lsem.at[a])
            lc.start()
            mine.append(lc)
            cp = pltpu.make_async_remote_copy(src_ref=srcs[a], dst_ref=dst, send_sem=ssem.at[a], recv_sem=rsem.at[a],
                                              device_id=(x, y, 1 - c), device_id_type=MESH)
            cp.start()
            cps.append(cp)
        for a in range(n):
            hr = srcs[a].shape[0]
            pltpu.make_async_remote_copy(src_ref=srcs[a], dst_ref=outs[a].at[pl.ds((1 - c) * hr, hr), :],
                                         send_sem=ssem.at[a], recv_sem=rsem.at[a], device_id=(x, y, 1 - c),
                                         device_id_type=MESH).wait_recv()
        for cp in cps:
            cp.wait_send()
        for cp in mine:
            cp.wait()

    return pl.pallas_call(
        body, name="share_halves", in_specs=_hbm_specs(n), out_specs=_hbm_specs(n),
        out_shape=[jax.ShapeDtypeStruct((2 * r.shape[0], r.shape[1]), F32) for r in rhs],
        scratch_shapes=[pltpu.SemaphoreType.DMA((n,)), pltpu.SemaphoreType.DMA((n,)), pltpu.SemaphoreType.DMA((n,))],
    )(*rhs)


def allreduce_small(p, name):
    rows, width = p.shape

    def body(p_ref, s_ref, gath, ssem, rsem):
        x, y, c = _place()
        me = 4 * x + 2 * y + c
        copies = []
        for d in range(1, N_DEV):
            dx, dy, dc = (d >> 2) & 1, (d >> 1) & 1, d & 1
            peer = (x ^ dx, y ^ dy, c ^ dc)
            cp = pltpu.make_async_remote_copy(src_ref=p_ref, dst_ref=gath.at[me], send_sem=ssem.at[d - 1],
                                              recv_sem=rsem.at[d - 1], device_id=peer, device_id_type=MESH)
            cp.start()
            copies.append(cp)
        gath[me] = p_ref[...]
        for d, cp in enumerate(copies, start=1):
            dx, dy, dc = (d >> 2) & 1, (d >> 1) & 1, d & 1
            peer_slot = 4 * (x ^ dx) + 2 * (y ^ dy) + (c ^ dc)
            pltpu.make_async_remote_copy(src_ref=p_ref, dst_ref=gath.at[peer_slot], send_sem=ssem.at[d - 1],
                                         recv_sem=rsem.at[d - 1], device_id=(x ^ dx, y ^ dy, c ^ dc),
                                         device_id_type=MESH).wait_recv()
        for cp in copies:
            cp.wait_send()
        acc = gath[0]
        for d in range(1, N_DEV):
            acc = acc + gath[d]
        s_ref[...] = acc

    return pl.pallas_call(
        body, name=name,
        in_specs=[pl.BlockSpec(memory_space=pltpu.VMEM)],
        out_specs=pl.BlockSpec(memory_space=pltpu.VMEM),
        out_shape=jax.ShapeDtypeStruct((rows, width), F32),
        scratch_shapes=[pltpu.VMEM((N_DEV, rows, width), F32), pltpu.SemaphoreType.DMA((N_DEV - 1,)),
                        pltpu.SemaphoreType.DMA((N_DEV - 1,))],
    )(p)


def add_my_half(g, recv, name):
    _, r, cols = g.shape
    hr = r // 2
    tb = _pick(hr, 512, 8)
    nb = hr // tb
    c_arr = lax.axis_index("c").astype(jnp.int32).reshape(1)

    def body(c_ref, g_ref, r_ref, o_ref):
        o_ref[...] = g_ref[...] + r_ref[...]

    return pl.pallas_call(
        body, name=name,
        grid_spec=pltpu.PrefetchScalarGridSpec(
            num_scalar_prefetch=1, grid=(N_CHIPS, nb),
            in_specs=[pl.BlockSpec((1, tb, cols), lambda k, i, c_ref: (k, c_ref[0] * nb + i, 0)),
                      pl.BlockSpec((1, tb, cols), lambda k, i, c_ref: (k, i, 0))],
            out_specs=pl.BlockSpec((1, tb, cols), lambda k, i, c_ref: (k, i, 0))),
        out_shape=jax.ShapeDtypeStruct((N_CHIPS, hr, cols), F32),
        compiler_params=_params(("arbitrary", "arbitrary")),
    )(c_arr, g, recv)


def sum_chips(parts, name):
    _, hr, cols = parts.shape
    tb = _pick(hr, 512, 8)

    def body(p_ref, o_ref):
        o_ref[...] = ((p_ref[0] + p_ref[1]) + p_ref[2]) + p_ref[3]

    return pl.pallas_call(
        body, name=name, grid=(hr // tb,),
        in_specs=[pl.BlockSpec((N_CHIPS, tb, cols), lambda i: (0, i, 0))],
        out_specs=pl.BlockSpec((tb, cols), lambda i: (i, 0)),
        out_shape=jax.ShapeDtypeStruct((hr, cols), F32),
        compiler_params=_params(("arbitrary",)),
    )(parts)


def adamw(w, g, m, v, name):
    rows, width = w.shape
    tb = _pick(rows, 512, 8)

    def body(w_ref, g_ref, m_ref, v_ref, d_ref, mo_ref, vo_ref):
        gg = g_ref[...]
        m_new = ADAM_B1 * m_ref[...] + (1.0 - ADAM_B1) * gg
        v_new = ADAM_B2 * v_ref[...] + (1.0 - ADAM_B2) * (gg * gg)
        m_hat = m_new / (1.0 - ADAM_B1 ** ADAM_STEP)
        v_hat = v_new / (1.0 - ADAM_B2 ** ADAM_STEP)
        d_ref[...] = -ADAM_LR * (m_hat / (jnp.sqrt(v_hat) + ADAM_EPS) + ADAM_WD * w_ref[...])
        mo_ref[...] = m_new
        vo_ref[...] = v_new

    spec = pl.BlockSpec((tb, width), lambda i: (i, 0))
    return pl.pallas_call(
        body, name=name, grid=(rows // tb,), in_specs=[spec] * 4, out_specs=[spec] * 3,
        out_shape=[jax.ShapeDtypeStruct((rows, width), F32)] * 3,
        compiler_params=_params(("arbitrary",)),
    )(w, g, m, v)


def _pack_rows(vecs, width=1024):
    rows, cur, used = [], [], 0
    for v in vecs:
        n = v.shape[1]
        if used + n > width:
            cur.append(jnp.zeros((1, width - used), F32))
            rows.append(jnp.concatenate(cur, axis=1))
            cur, used = [], 0
        cur.append(v)
        used += n
    cur.append(jnp.zeros((1, width - used), F32))
    rows.append(jnp.concatenate(cur, axis=1))
    return rows


def _pad128(v):
    return jnp.pad(v, ((0, 0), (0, LANES - v.shape[1])))


def kernel(x, norm1_w, w_in, lb_logits, hg_norm_w, q_norm_w, k_norm_w, att_norm_w, w_out, norm2_w, w_gate_up, w_down, final_norm_w, loss_target, m_norm1_w, m_w_in, m_lb_logits, m_hg_norm_w, m_q_norm_w, m_k_norm_w, m_att_norm_w, m_w_out, m_norm2_w, m_w_gate_up, m_w_down, m_final_norm_w, v_norm1_w, v_w_in, v_lb_logits, v_hg_norm_w, v_q_norm_w, v_k_norm_w, v_att_norm_w, v_w_out, v_norm2_w, v_w_gate_up, v_w_down, v_final_norm_w):
    t, d = x.shape[1], x.shape[2]
    xi, yi, ci = _place()
    chip = 2 * xi + yi
    x2d = x.reshape(t, d)
    tgt = loss_target.reshape(t, d)
    tok = min(TOK, t)
    nt = t // tok
    ns = ATT_GROUP * nt

    g_in, g_out, g_gu, g_down = allgather_weights([w_in[0].astype(BF16), w_out[0].astype(BF16),
                                                   w_gate_up[0].astype(BF16), w_down[0].astype(BF16)])
    wf_in = g_in.transpose(1, 0, 2).reshape(g_in.shape[1], -1)
    wf_out = g_out.reshape(-1, g_out.shape[2])
    wf_down = g_down.reshape(-1, g_down.shape[2])
    lb_rows = lb_logits.reshape(4, LANES) * (ci == 0).astype(F32)
    lb_pad = lax.dynamic_update_slice(jnp.zeros((8, 1024), F32), lb_rows, (0, chip * LANES))
    lb_full = allreduce_small(lb_pad, "gather_lb")[:4, :HG_W]
    l_f0, l_f1, l_b0, l_b1 = (lb_full[i:i + 1] for i in range(4))

    n1 = norm1_w.reshape(1, d)
    n2 = norm2_w.reshape(1, d)
    nf = final_norm_w.reshape(1, d)
    tm = min(512, t)
    (h1,) = _rows(lambda a, w: ((_rms(a, w),), ()), "norm1", t, tm, [_rin(x2d, tm)], [n1], [_rout(t, tm, d, BF16)])
    u = mm_rows(h1, wf_in, "mm_in")
    o_f, ss_f = gla_fwd(u, l_f0, l_f1, 1, False, "gla_fwd_f")
    o_b, ss_b = gla_fwd(u, l_b0, l_b1, 2, True, "gla_fwd_b")

    cos2, sin2 = _rope_tables(t)
    cq, sq = jnp.tile(cos2, (1, ATT_HEADS)), jnp.tile(sin2, (1, ATT_HEADS))
    ck, sk = jnp.tile(cos2, (1, ATT_KV)), jnp.tile(sin2, (1, ATT_KV))
    qw, kw = _pad128(q_norm_w.reshape(1, ATT_DH)), _pad128(k_norm_w.reshape(1, ATT_DH))
    gq, gk = _group_sum_matrix(ATT_QW), _group_sum_matrix(ATT_KVW)
    tq, tk = _tile_matrix(ATT_QW), _tile_matrix(ATT_KVW)
    prep_in = [_rin(u, tok, ATT_QW, 5), _rin(u, tok, ATT_KVW, 24), _rin(cq, tok), _rin(sq, tok), _rin(ck, tok),
               _rin(sk, tok)]
    prep_consts = [qw, kw, gq, gk, tq, tk]

    def att_prep_fn(aq, ak, av, *rest):
        q, k = _att_prep_fn(aq, ak, *rest)
        return (_head_t(q, ATT_HEADS), _head_s(q, ATT_HEADS), _head_s(k, ATT_KV), _head_t(k, ATT_KV),
                _head_t(av, ATT_KV)), ()

    q_t, q_s, k_s, k_t, v_t = _rows(
        att_prep_fn, "att_prep", t, tok, prep_in[:2] + [_rin(u, tok, ATT_KVW, 25)] + prep_in[2:], prep_consts,
        [_tout(ATT_HEADS, nt, ATT_DH, tok, BF16), _tout(ATT_HEADS, nt, tok, LANES, BF16),
         _tout(ATT_KV, nt, tok, LANES, BF16), _tout(ATT_KV, nt, ATT_DH, tok, BF16),
         _tout(ATT_KV, nt, ATT_DH, tok, BF16)])
    q_t = q_t.reshape(ATT_KV, ns, ATT_DH, tok)
    q_s = q_s.reshape(ATT_KV, ns, tok, LANES)
    k_s = k_s.reshape(ATT_KV, t, LANES)
    o_t, lse = fa_fwd(q_t, k_s, v_t)
    o_tiles = o_t.reshape(ATT_HEADS, nt, ATT_DH, tok)

    hgw = hg_norm_w.reshape(1, HG_D)
    attw = att_norm_w.reshape(1, ATT_QW)
    mix_in = [_rin(o_f, tok), _rin(o_b, tok), _rin(u, tok, HG_W, 4), _tin(o_tiles)]
    (mix,) = _rows(lambda of, ob, ug, ot, hw, aw: ((_post_mix_fn(of, ob, ug, _from_head_t(ot), hw, aw),), ()),
                   "post_mix", t, tok, mix_in, [hgw, attw], [_rout(t, tok, d, BF16)])
    x1 = mm_rows(mix, wf_out, "mm_out", res=x2d)
    (h2,) = _rows(lambda a, w: ((_rms(a, w),), ()), "norm2", t, tm, [_rin(x1, tm)], [n2], [_rout(t, tm, d, BF16)])
    gu = mm_rows(h2, g_gu, "mm_gate_up")
    dff = gu.shape[1] // 2
    (act,) = _rows(lambda g_, u_: ((_swiglu_fn(g_, u_),), ()), "swiglu", t, tok,
                   [_rin(gu, tok, dff, 0), _rin(gu, tok, dff, 1)], [], [_rout(t, tok, dff, BF16)])
    x2 = mm_rows(act, wf_down, "mm_down", res=x1)

    def loss_bwd_fn(a, tg, w):
        val, vjp = jax.vjp(lambda a_, w_: _loss_fn(a_, w_, tg), a, w)
        da, dw = vjp(jnp.ones((1, 1), F32))
        return (da, da), (dw, jnp.broadcast_to(val, (1, LANES)))

    dx2, dx2b, g_final, loss_part = _rows(loss_bwd_fn, "loss_bwd", t, tok, [_rin(x2, tok), _rin(tgt, tok)], [nf],
                                          [_rout(t, tok, d, F32), _rout(t, tok, d, BF16)], [d, LANES])
    dact = mm_rows(dx2b, wf_down, "mm_down_dx", trans_b=True)
    gw_down = mm_tn(act, dx2b, "mm_down_dw")

    def swiglu_bwd_fn(g_, u_, da):
        _, vjp = jax.vjp(_swiglu_fn, g_, u_)
        dg, du_ = vjp(da)
        return (jnp.concatenate([dg, du_], axis=1),), ()

    (dgu,) = _rows(swiglu_bwd_fn, "swiglu_bwd", t, tok, [_rin(gu, tok, dff, 0), _rin(gu, tok, dff, 1), _rin(dact, tok)],
                   [], [_rout(t, tok, 2 * dff, BF16)])
    dh2 = mm_rows_kstack(dgu, g_gu, "mm_gate_up_dx")
    gw_gu = mm_tn(h2, dgu, "mm_gate_up_dw", col_shards=N_CHIPS)

    def norm_bwd_fn(a, dh, dres, w):
        _, vjp = jax.vjp(_rms, a, w)
        da, dw = vjp(dh)
        da = da + dres
        return (da, da), (dw,)

    dx1, dx1b, g_norm2 = _rows(norm_bwd_fn, "norm2_bwd", t, tok, [_rin(x1, tok), _rin(dh2, tok), _rin(dx2, tok)], [n2],
                               [_rout(t, tok, d, F32), _rout(t, tok, d, BF16)], [d])
    dmix = mm_rows(dx1b, wf_out, "mm_out_dx", trans_b=True)
    gw_out = mm_tn(mix, dx1b, "mm_out_dw")

    def post_mix_bwd_fn(of, ob, ug, ot, dm, hgw_, attw_):
        _, vjp = jax.vjp(_post_mix_fn, of, ob, ug, _from_head_t(ot), hgw_, attw_)
        dof, _, dug, doa, dhgw, dattw = vjp(dm)
        return (dof, dug, _head_t(doa, ATT_HEADS), _head_s(doa, ATT_HEADS)), (dhgw, dattw)

    do_hg, du_g, do_t, do_s, g_hg, g_att = _rows(
        post_mix_bwd_fn, "post_mix_bwd", t, tok, mix_in + [_rin(dmix, tok)], [hgw, attw],
        [_rout(t, tok, HG_W, F32), _rout(t, tok, HG_W, BF16), _tout(ATT_HEADS, nt, ATT_DH, tok, BF16),
         _tout(ATT_HEADS, nt, tok, LANES, BF16)], [HG_D, ATT_QW])
    dq_t, dk_t, dv_t = fa_bwd(q_s, q_t, do_s.reshape(q_s.shape), do_t.reshape(q_t.shape), o_t, lse, k_s, k_t, v_t)

    def att_prep_bwd_fn(aq, ak, cq_, sq_, ck_, sk_, dqt, dkt, dvt, qw_, kw_, gq_, gk_, tq_, tk_):
        _, vjp = jax.vjp(lambda a, b, c_, e: _att_prep_fn(a, b, cq_, sq_, ck_, sk_, c_, e, gq_, gk_, tq_, tk_),
                         aq, ak, qw_, kw_)
        daq, dak, dqw, dkw = vjp((_from_head_t(dqt), _from_head_t(dkt)))
        return (daq, dak, _from_head_t(dvt)), (dqw, dkw)

    da_q, da_k, da_v, g_q, g_k = _rows(
        att_prep_bwd_fn, "att_prep_bwd", t, tok,
        prep_in + [_tin(dq_t.reshape(ATT_HEADS, nt, ATT_DH, tok)), _tin(dk_t), _tin(dv_t)], prep_consts,
        [_rout(t, tok, ATT_QW, BF16), _rout(t, tok, ATT_KVW, BF16), _rout(t, tok, ATT_KVW, BF16)], [LANES, LANES])

    dq_f, df_f, di_f, dl_f0, dl_f1 = gla_bwd(u, l_f0, l_f1, ss_f, do_hg, 1, False, "gla_bwd_f")
    dq_b, df_b, di_b, dl_b0, dl_b1 = gla_bwd(u, l_b0, l_b1, ss_b, do_hg, 2, True, "gla_bwd_b")

    def assemble_fn(qf, qb, ff, fb, i_f, i_b, dg, aq, ak, av):
        parts = [qf + qb, ff, fb, i_f + i_b, dg.astype(F32), aq.astype(F32), ak.astype(F32), av.astype(F32)]
        return (jnp.concatenate(parts, axis=1),), ()

    (du,) = _rows(assemble_fn, "assemble_du", t, tok,
                  [_rin(a, tok) for a in (dq_f, dq_b, df_f, df_b, di_f, di_b, du_g, da_q, da_k, da_v)], [],
                  [_rout(t, tok, u.shape[1], BF16)])
    dh1 = mm_rows(du, wf_in, "mm_in_dx", trans_b=True)
    gw_in_t = mm_tn(du, h1, "mm_in_dw")

    def norm1_bwd_fn(a, dh, dres, w):
        (da, _), (dw,) = norm_bwd_fn(a, dh, dres, w)
        return (da,), (dw,)

    grad_x, g_norm1 = _rows(norm1_bwd_fn, "norm1_bwd", t, tok, [_rin(x2d, tok), _rin(dh1, tok), _rin(dx1, tok)], [n1],
                            [_rout(t, tok, d, F32)], [d])

    names = ["w_in", "w_out", "w_gate_up", "w_down"]
    gsh = [gw_in_t.reshape(N_CHIPS, -1, d), gw_out.reshape(N_CHIPS, -1, d), gw_gu, gw_down.reshape(N_CHIPS, -1, d)]
    recv = rs_siblings(gsh)
    chip_part = [add_my_half(g, r, "add_my_half_" + n) for g, r, n in zip(gsh, recv, names)]
    parts = rs_chips(chip_part)
    g_shard = list(share_halves([sum_chips(p, "sum_chips_" + n) for p, n in zip(parts, names)]))
    g_shard[0] = g_shard[0].T
    big = {}
    for n, g, w, m, v in zip(names, g_shard, (w_in, w_out, w_gate_up, w_down), (m_w_in, m_w_out, m_w_gate_up, m_w_down),
                             (v_w_in, v_w_out, v_w_gate_up, v_w_down)):
        dlt, mn, vn = adamw(w[0], g, m[0], v[0], "adamw_" + n)
        big[n] = (g[None], dlt[None], mn[None], vn[None])

    small = [g_norm1, g_norm2, g_final, g_att, g_hg, g_q, g_k, loss_part, dl_f0, dl_f1, dl_b0, dl_b1]
    packed = _pack_rows(small)
    packed += [jnp.zeros((1, 1024), F32)] * (8 - len(packed))
    tot = allreduce_small(jnp.concatenate(packed, axis=0), "allreduce_small")
    s_norm1, s_norm2, s_final = tot[0:1], tot[1:2], tot[2:3]
    s_att, s_hg, s_q, s_k = tot[3:4, 0:512], tot[3:4, 512:640], tot[3:4, 640:704], tot[3:4, 768:832]
    loss = tot[3, 896]
    s_lb = jnp.concatenate([tot[4:5, 0:512], tot[4:5, 512:1024], tot[5:6, 0:512], tot[5:6, 512:1024]], axis=0)
    s_lb = lax.dynamic_slice(s_lb, (0, chip * LANES), (4, LANES)).reshape(1, 512)

    snames = ["norm1_w", "lb_logits", "hg_norm_w", "q_norm_w", "k_norm_w", "att_norm_w", "norm2_w", "final_norm_w"]
    g_small = dict(zip(snames, [s_norm1, s_lb, s_hg, s_q, s_k, s_att, s_norm2, s_final]))
    w_small = dict(zip(snames, [norm1_w, lb_logits, hg_norm_w, q_norm_w, k_norm_w, att_norm_w, norm2_w, final_norm_w]))
    m_small = dict(zip(snames, [m_norm1_w, m_lb_logits, m_hg_norm_w, m_q_norm_w, m_k_norm_w, m_att_norm_w, m_norm2_w, m_final_norm_w]))
    v_small = dict(zip(snames, [v_norm1_w, v_lb_logits, v_hg_norm_w, v_q_norm_w, v_k_norm_w, v_att_norm_w, v_norm2_w, v_final_norm_w]))

    def pack_small(tree):
        rows = _pack_rows([tree[n].reshape(1, -1) for n in snames])
        rows += [jnp.zeros((1, 1024), F32)] * (8 - len(rows))
        return jnp.concatenate(rows, axis=0)

    d_s, m_s, v_s = adamw(pack_small(w_small), pack_small(g_small), pack_small(m_small), pack_small(v_small), "adamw_small")

    def unpack_small(a):
        out, r, used = {}, 0, 0
        for n in snames:
            size = w_small[n].size
            if used + size > 1024:
                r, used = r + 1, 0
            out[n] = a[r, used:used + size].reshape(w_small[n].shape)
            used += size
        return out

    d_sm, m_sm, v_sm = unpack_small(d_s), unpack_small(m_s), unpack_small(v_s)
    g_sm = {n: g_small[n].reshape(w_small[n].shape) for n in snames}

    order = ["norm1_w", "w_in", "lb_logits", "hg_norm_w", "q_norm_w", "k_norm_w", "att_norm_w", "w_out", "norm2_w",
             "w_gate_up", "w_down", "final_norm_w"]

    def pick(small_tree, idx):
        return [big[n][idx] if n in big else small_tree[n] for n in order]

    return (loss, grad_x.reshape(x.shape), *pick(g_sm, 0), *pick(d_sm, 1), *pick(m_sm, 2), *pick(v_sm, 3))
```

```python
import functools

import numpy as np
import jax
import jax.numpy as jnp
from jax import lax
from jax.experimental import pallas as pl
from jax.experimental.pallas import tpu as pltpu

F32 = jnp.float32
BF16 = jnp.bfloat16
MESH = pl.DeviceIdType.MESH
HIGHEST = lax.Precision.HIGHEST

EPS = 1e-6
GRID_W = 64
HG_HEADS = 4
HG_D = 128
HG_W = HG_HEADS * HG_D
CHUNK = 64
ATT_HEADS = 8
ATT_KV = 2
ATT_GROUP = ATT_HEADS // ATT_KV
ATT_DH = 64
ATT_QW = ATT_HEADS * ATT_DH
ATT_KVW = ATT_KV * ATT_DH
ROPE_THETA = 10000.0
N_CHIPS = 4
N_DEV = 8

ADAM_LR = 0.001
ADAM_B1 = 0.9
ADAM_B2 = 0.999
ADAM_EPS = 1e-08
ADAM_WD = 0.01
ADAM_STEP = 10

VMEM_LIMIT = 52 * 1024 * 1024
LANES = 128
TOK = 256


def _params(sem=None):
    return pltpu.CompilerParams(dimension_semantics=sem, vmem_limit_bytes=VMEM_LIMIT)


def _dg(a, b, ca, cb):
    return lax.dot_general(a.astype(BF16), b.astype(BF16), (((ca,), (cb,)), ((), ())),
                           preferred_element_type=F32)


@jax.custom_vjp
def dot_nn(a, b):
    return _dg(a, b, 1, 0)


def _dot_nn_fwd(a, b):
    return _dg(a, b, 1, 0), (a, b)


def _dot_nn_bwd(res, g):
    a, b = res
    return _dg(g, b, 1, 1), _dg(a, g, 0, 0)


dot_nn.defvjp(_dot_nn_fwd, _dot_nn_bwd)


@jax.custom_vjp
def dot_nt(a, b):
    return _dg(a, b, 1, 1)


def _dot_nt_fwd(a, b):
    return _dg(a, b, 1, 1), (a, b)


def _dot_nt_bwd(res, g):
    a, b = res
    return _dg(g, b, 1, 0), _dg(g, a, 0, 0)


dot_nt.defvjp(_dot_nt_fwd, _dot_nt_bwd)


@jax.custom_vjp
def dot_tn(a, b):
    return _dg(a, b, 0, 0)


def _dot_tn_fwd(a, b):
    return _dg(a, b, 0, 0), (a, b)


def _dot_tn_bwd(res, g):
    a, b = res
    return _dg(b, g, 1, 1), _dg(a, g, 1, 0)


dot_tn.defvjp(_dot_tn_fwd, _dot_tn_bwd)


def _xdot(a, m):
    return jnp.dot(a, m, precision=HIGHEST, preferred_element_type=F32)


@jax.custom_vjp
def swap_pairs(y):
    n = y.shape[-1]
    lane = lax.broadcasted_iota(jnp.int32, y.shape, 1)
    nxt = pltpu.roll(y, n - 1, 1)
    prv = pltpu.roll(y, 1, 1)
    return jnp.where(lane % 2 == 0, nxt, prv)


def _swap_fwd(y):
    return swap_pairs(y), None


def _swap_bwd(_, g):
    return (swap_pairs(g),)


swap_pairs.defvjp(_swap_fwd, _swap_bwd)


def _rms(x, w):
    return x * lax.rsqrt(jnp.mean(x * x, axis=-1, keepdims=True) + EPS) * w


def _sigmoid(x):
    return jax.nn.sigmoid(x)


def _rows(fn, name, t, tm, ins, consts, outs, accs=()):
    n_r, n_c, n_o, n_a = len(ins), len(consts), len(outs), len(accs)

    def body(*refs):
        r = refs[:n_r]
        c = refs[n_r:n_r + n_c]
        o = refs[n_r + n_c:n_r + n_c + n_o]
        a = refs[n_r + n_c + n_o:]
        ro, ao = fn(*[x[...] for x in r], *[x[...] for x in c])
        for ref, val in zip(o, ro):
            if isinstance(val, (list, tuple)):
                for h, piece in enumerate(val):
                    ref[h, 0] = piece.astype(ref.dtype)
            else:
                ref[...] = val.astype(ref.dtype)
        if n_a:
            @pl.when(pl.program_id(0) == 0)
            def _():
                for ref in a:
                    ref[...] = jnp.zeros(ref.shape, F32)
            for ref, val in zip(a, ao):
                ref[...] += val

    in_specs = [s for _, s in ins]
    in_specs += [pl.BlockSpec(c.shape, lambda i, nd=c.ndim: (0,) * nd) for c in consts]
    out_specs = [s for _, s in outs] + [pl.BlockSpec((1, w), lambda i: (0, 0)) for w in accs]
    out_shape = [s for s, _ in outs] + [jax.ShapeDtypeStruct((1, w), F32) for w in accs]
    return pl.pallas_call(
        body, name=name, grid=(t // tm,), in_specs=in_specs, out_specs=out_specs, out_shape=out_shape,
        compiler_params=_params(("arbitrary",)),
    )(*[a for a, _ in ins], *consts)


def _rin(a, tm, width=None, cb=0):
    width = a.shape[1] if width is None else width
    return a, pl.BlockSpec((tm, width), lambda i, cb=cb: (i, cb))


def _rout(t, tm, width, dtype):
    return jax.ShapeDtypeStruct((t, width), dtype), pl.BlockSpec((tm, width), lambda i: (i, 0))


def _tin(a):
    return a, pl.BlockSpec((a.shape[0], 1) + a.shape[2:], lambda i: (0, i, 0, 0))


def _tout(heads, nt, r, c, dtype):
    return jax.ShapeDtypeStruct((heads, nt, r, c), dtype), pl.BlockSpec((heads, 1, r, c), lambda i: (0, i, 0, 0))


def _pick(n, cap, mult):
    best = None
    for d in range(mult, min(n, cap) + 1, mult):
        if n % d == 0:
            best = d
    return best if best is not None else n


def mm_rows(a, b, name, trans_b=False, res=None, out_dtype=F32):
    m, k = a.shape
    n = b.shape[0] if trans_b else b.shape[1]
    tn = _pick(n, 3328, LANES)
    tm = _pick(m, 512 if k <= 3072 else 256, 8)
    has_res = res is not None

    def body(*refs):
        if has_res:
            a_ref, b_ref, r_ref, o_ref = refs
        else:
            a_ref, b_ref, o_ref = refs
        acc = _dg(a_ref[...], b_ref[...], 1, 1 if trans_b else 0)
        if has_res:
            acc = acc + r_ref[...]
        o_ref[...] = acc.astype(o_ref.dtype)

    in_specs = [pl.BlockSpec((tm, k), lambda j, i: (i, 0))]
    if trans_b:
        in_specs.append(pl.BlockSpec((tn, k), lambda j, i: (j, 0)))
    else:
        in_specs.append(pl.BlockSpec((k, tn), lambda j, i: (0, j)))
    args = [a, b]
    if has_res:
        in_specs.append(pl.BlockSpec((tm, tn), lambda j, i: (i, j)))
        args.append(res)
    return pl.pallas_call(
        body, name=name, grid=(n // tn, m // tm), in_specs=in_specs,
        out_specs=pl.BlockSpec((tm, tn), lambda j, i: (i, j)),
        out_shape=jax.ShapeDtypeStruct((m, n), out_dtype),
        compiler_params=_params(("arbitrary", "arbitrary")),
    )(*args)


def mm_tn(a, b, name, col_shards=None, transpose_out=False):
    t, m = a.shape
    parts = b.shape[0] if b.ndim == 3 else 1
    n = parts * b.shape[-1]
    tm = _pick(m, 512, LANES)
    tn = n // col_shards if col_shards else _pick(n, 3328, LANES)
    tk = _pick(t, 1024, 8)
    nkk = t // tk
    per_part = b.shape[-1] // tn

    def body(a_ref, b_ref, o_ref, *scratch):
        acc_ref = scratch[0] if transpose_out else o_ref

        @pl.when(pl.program_id(2) == 0)
        def _():
            acc_ref[...] = jnp.zeros(acc_ref.shape, F32)
        acc = _dg(a_ref[...], b_ref[0] if b.ndim == 3 else b_ref[...], 0, 0)
        if col_shards:
            acc_ref[0] += acc
        else:
            acc_ref[...] += acc
        if transpose_out:
            @pl.when(pl.program_id(2) == nkk - 1)
            def _():
                o_ref[...] = acc_ref[...].T

    if col_shards:
        out_spec = pl.BlockSpec((1, tm, tn), lambda i, j, kk: (j, i, 0))
        out_shape = jax.ShapeDtypeStruct((col_shards, m, tn), F32)
    elif transpose_out:
        out_spec = pl.BlockSpec((tn, tm), lambda i, j, kk: (j, i))
        out_shape = jax.ShapeDtypeStruct((n, m), F32)
    else:
        out_spec = pl.BlockSpec((tm, tn), lambda i, j, kk: (i, j))
        out_shape = jax.ShapeDtypeStruct((m, n), F32)
    if b.ndim == 3:
        b_spec = pl.BlockSpec((1, tk, tn), lambda i, j, kk: (j // per_part, kk, j % per_part))
    else:
        b_spec = pl.BlockSpec((tk, tn), lambda i, j, kk: (kk, j))
    return pl.pallas_call(
        body, name=name, grid=(m // tm, n // tn, nkk),
        in_specs=[pl.BlockSpec((tk, tm), lambda i, j, kk: (kk, i)), b_spec],
        out_specs=out_spec, out_shape=out_shape,
        scratch_shapes=[pltpu.VMEM((tm, tn), F32)] if transpose_out else [],
        compiler_params=_params(("arbitrary", "arbitrary", "arbitrary")),
    )(a, b)


def ffn_up(h2, w_gu):
    t, k = h2.shape
    dff = w_gu.shape[1] // 2
    tn = _pick(dff, 1408, LANES)
    ncol = dff // tn
    tm = _pick(t, 512, 8)

    def body(a_ref, wg_ref, wu_ref, gu_ref, act_ref):
        a = a_ref[...]
        g = _dg(a, wg_ref[...], 1, 0)
        u = _dg(a, wu_ref[...], 1, 0)
        gu_ref[0] = g
        gu_ref[1] = u
        act_ref[...] = _swiglu_fn(g, u).astype(act_ref.dtype)

    return pl.pallas_call(
        body, name="ffn_up", grid=(ncol, t // tm),
        in_specs=[pl.BlockSpec((tm, k), lambda s, i: (i, 0)), pl.BlockSpec((k, tn), lambda s, i: (0, s)),
                  pl.BlockSpec((k, tn), lambda s, i: (0, s + ncol))],
        out_specs=[pl.BlockSpec((2, tm, tn), lambda s, i: (0, i, s)), pl.BlockSpec((tm, tn), lambda s, i: (i, s))],
        out_shape=[jax.ShapeDtypeStruct((2, t, dff), F32), jax.ShapeDtypeStruct((t, dff), BF16)],
        compiler_params=_params(("arbitrary", "arbitrary")),
    )(h2, w_gu, w_gu)


def ffn_down_dx(dx2b, w_down, gu2):
    t, k = dx2b.shape
    dff = w_down.shape[0]
    tn = _pick(dff, 1408, LANES)
    tm = _pick(t, 512, 8)

    def body(a_ref, w_ref, gu_ref, o_ref):
        dact = _dg(a_ref[...], w_ref[...], 1, 1)
        _, vjp = jax.vjp(_swiglu_fn, gu_ref[0], gu_ref[1])
        dg, du = vjp(dact)
        o_ref[0] = dg.astype(o_ref.dtype)
        o_ref[1] = du.astype(o_ref.dtype)

    blk = pl.BlockSpec((2, tm, tn), lambda s, i: (0, i, s))
    return pl.pallas_call(
        body, name="ffn_down_dx", grid=(dff // tn, t // tm),
        in_specs=[pl.BlockSpec((tm, k), lambda s, i: (i, 0)), pl.BlockSpec((tn, k), lambda s, i: (s, 0)), blk],
        out_specs=blk, out_shape=jax.ShapeDtypeStruct((2, t, dff), BF16),
        compiler_params=_params(("arbitrary", "arbitrary")),
    )(dx2b, w_down, gu2)


def mm_res_norm(a, b, res, w, name):
    t, k = a.shape
    d = b.shape[1]
    tm = _pick(t, 512, 8)

    def body(a_ref, b_ref, r_ref, w_ref, x_ref, h_ref):
        xv = _dg(a_ref[...], b_ref[...], 1, 0) + r_ref[...]
        x_ref[...] = xv
        h_ref[...] = _rms(xv, w_ref[...]).astype(h_ref.dtype)

    row = pl.BlockSpec((tm, d), lambda i: (i, 0))
    return pl.pallas_call(
        body, name=name, grid=(t // tm,),
        in_specs=[pl.BlockSpec((tm, k), lambda i: (i, 0)), pl.BlockSpec((k, d), lambda i: (0, 0)), row,
                  pl.BlockSpec((1, d), lambda i: (0, 0))],
        out_specs=[row, row],
        out_shape=[jax.ShapeDtypeStruct((t, d), F32), jax.ShapeDtypeStruct((t, d), BF16)],
        compiler_params=_params(("arbitrary",)),
    )(a, b, res, w)


def mm_res_loss(a, b, res, tgt, w, name):
    t, k = a.shape
    d = b.shape[1]
    tm = _pick(t, 256, 8)

    def body(a_ref, b_ref, r_ref, t_ref, w_ref, dx_ref, dxb_ref, dw_ref, loss_ref):
        @pl.when(pl.program_id(0) == 0)
        def _():
            dw_ref[...] = jnp.zeros(dw_ref.shape, F32)
            loss_ref[...] = jnp.zeros(loss_ref.shape, F32)
        xv = _dg(a_ref[...], b_ref[...], 1, 0) + r_ref[...]
        tg = t_ref[...]
        val, vjp = jax.vjp(lambda x_, w_: _loss_fn(x_, w_, tg), xv, w_ref[...])
        dx, dw = vjp(jnp.ones((1, 1), F32))
        dx_ref[...] = dx
        dxb_ref[...] = dx.astype(dxb_ref.dtype)
        dw_ref[...] += dw
        loss_ref[...] += jnp.broadcast_to(val, (1, LANES))

    row = pl.BlockSpec((tm, d), lambda i: (i, 0))
    vec = pl.BlockSpec((1, d), lambda i: (0, 0))
    return pl.pallas_call(
        body, name=name, grid=(t // tm,),
        in_specs=[pl.BlockSpec((tm, k), lambda i: (i, 0)), pl.BlockSpec((k, d), lambda i: (0, 0)), row, row, vec],
        out_specs=[row, row, vec, pl.BlockSpec((1, LANES), lambda i: (0, 0))],
        out_shape=[jax.ShapeDtypeStruct((t, d), F32), jax.ShapeDtypeStruct((t, d), BF16),
                   jax.ShapeDtypeStruct((1, d), F32), jax.ShapeDtypeStruct((1, LANES), F32)],
        compiler_params=_params(("arbitrary",)),
    )(a, b, res, tgt, w)


def mm_nt_norm_bwd(a, wmat, x, w, dres, name, with_bf16):
    parts = a.shape[0] if a.ndim == 3 else 1
    t, kp = a.shape[-2], a.shape[-1]
    d = wmat.shape[0]
    tm = _pick(t, 256, 8)

    def body(*refs):
        a_ref, w_refs = refs[0], refs[1:1 + parts]
        x_ref, nw_ref, r_ref, dx_ref = refs[1 + parts:5 + parts]
        dw_ref = refs[-1]

        @pl.when(pl.program_id(0) == 0)
        def _():
            dw_ref[...] = jnp.zeros(dw_ref.shape, F32)
        dh = None
        for p in range(parts):
            term = _dg(a_ref[p] if a.ndim == 3 else a_ref[...], w_refs[p][...], 1, 1)
            dh = term if dh is None else dh + term
        _, vjp = jax.vjp(_rms, x_ref[...], nw_ref[...])
        dx, dw = vjp(dh)
        dx = dx + r_ref[...]
        dx_ref[...] = dx
        if with_bf16:
            refs[5 + parts][...] = dx.astype(BF16)
        dw_ref[...] += dw

    row = pl.BlockSpec((tm, d), lambda i: (i, 0))
    vec = pl.BlockSpec((1, d), lambda i: (0, 0))
    a_spec = pl.BlockSpec((parts, tm, kp), lambda i: (0, i, 0)) if a.ndim == 3 else pl.BlockSpec((tm, kp), lambda i: (i, 0))
    w_specs = [pl.BlockSpec((d, kp), lambda i, p=p: (0, p)) for p in range(parts)]
    outs = [row] + ([row] if with_bf16 else []) + [vec]
    shapes = [jax.ShapeDtypeStruct((t, d), F32)] + ([jax.ShapeDtypeStruct((t, d), BF16)] if with_bf16 else [])
    return pl.pallas_call(
        body, name=name, grid=(t // tm,),
        in_specs=[a_spec] + w_specs + [row, vec, row],
        out_specs=outs, out_shape=shapes + [jax.ShapeDtypeStruct((1, d), F32)],
        compiler_params=_params(("arbitrary",)),
    )(a, *([wmat] * parts), x, w, dres)


def _lower_bound(l0, l1):
    m = jnp.maximum(l0, l1)
    e0 = jnp.exp(l0 - m)
    e1 = jnp.exp(l1 - m)
    return e0 / (e0 + e1)


def _gla_consts(rev):
    ri = lax.broadcasted_iota(jnp.int32, (CHUNK, CHUNK), 0)
    ci = lax.broadcasted_iota(jnp.int32, (CHUNK, CHUNK), 1)
    keep = (ci >= ri) if rev else (ci <= ri)
    tm = keep.astype(F32)
    if rev:
        rvec = (ci >= CHUNK // 2).astype(F32)
    else:
        rvec = (ci <= CHUNK // 2 - 1).astype(F32)
    m3 = jnp.concatenate([tm, tm - rvec, 1.0 - tm], axis=0)
    return keep, m3


def _gla_block(uq, uf, ui, l0, l1, st_in, rev):
    ncb = uq.shape[0] // CHUNK
    heads = range(HG_HEADS)
    keep, m3 = _gla_consts(rev)
    lb = _lower_bound(l0, l1)
    q = uq * _sigmoid(uq)
    k = (1.0 - lb) * _sigmoid(-uf)
    g = jnp.log(lb + (1.0 - lb) * _sigmoid(uf))

    def rows(a, c):
        return a[c * CHUNK:(c + 1) * CHUNK]

    def head(a, h):
        return a[:, h * HG_D:(h + 1) * HG_D]

    b3 = [_xdot(m3, rows(g, c)) for c in range(ncb)]
    q_in, k_in, q_b, k_d, decay = [], [], [], [], []
    for c in range(ncb):
        b, bmr, lmb = b3[c][0:CHUNK], b3[c][CHUNK:2 * CHUNK], b3[c][2 * CHUNK:3 * CHUNK]
        qc, kc = rows(q, c), rows(k, c)
        q_in.append(qc * jnp.exp(bmr))
        k_in.append(kc * jnp.exp(-bmr))
        q_b.append(qc * jnp.exp(b))
        k_d.append(kc * jnp.exp(lmb))
        decay.append(jnp.exp(jnp.sum(rows(g, c), axis=0, keepdims=True)))
    scores = [[jnp.where(keep, dot_nt(head(q_in[c], h), head(k_in[c], h)), 0.0) for h in heads] for c in range(ncb)]
    o_intra = [[dot_nn(scores[c][h], head(rows(ui, c), h)) for h in heads] for c in range(ncb)]
    contrib = [[dot_tn(head(rows(ui, c), h), head(k_d[c], h)) for h in heads] for c in range(ncb)]
    st = list(st_in)
    o_rows = [None] * ncb
    for c in (reversed(range(ncb)) if rev else range(ncb)):
        parts = []
        for h in heads:
            parts.append(o_intra[c][h] + dot_nt(head(q_b[c], h), st[h]))
            st[h] = st[h] * head(decay[c], h) + contrib[c][h]
        o_rows[c] = jnp.concatenate(parts, axis=1)
    return jnp.concatenate(o_rows, axis=0), tuple(st)


def _gla_blocks(t):
    tb = min(512, t)
    return tb, t // tb


def gla_fwd(u, l0, l1, fcol, rev, name):
    t = u.shape[0]
    tb, nb = _gla_blocks(t)

    def blk(i):
        return (nb - 1 - i) if rev else i

    def body(uq_ref, uf_ref, ui_ref, l0_ref, l1_ref, o_ref, ss_ref, st_ref):
        @pl.when(pl.program_id(0) == 0)
        def _():
            st_ref[...] = jnp.zeros(st_ref.shape, F32)
        ss_ref[0] = st_ref[...]
        o, st_out = _gla_block(uq_ref[...], uf_ref[...], ui_ref[...], l0_ref[...], l1_ref[...],
                               tuple(st_ref[h] for h in range(HG_HEADS)), rev)
        o_ref[...] = o
        for h in range(HG_HEADS):
            st_ref[h] = st_out[h]

    row = lambda cb: pl.BlockSpec((tb, HG_W), lambda i: (blk(i), cb))
    vec = pl.BlockSpec((1, HG_W), lambda i: (0, 0))
    return pl.pallas_call(
        body, name=name, grid=(nb,),
        in_specs=[row(0), row(fcol), row(3), vec, vec],
        out_specs=[pl.BlockSpec((tb, HG_W), lambda i: (blk(i), 0)),
                   pl.BlockSpec((1, HG_HEADS, HG_D, HG_D), lambda i: (blk(i), 0, 0, 0))],
        out_shape=[jax.ShapeDtypeStruct((t, HG_W), F32),
                   jax.ShapeDtypeStruct((nb, HG_HEADS, HG_D, HG_D), F32)],
        scratch_shapes=[pltpu.VMEM((HG_HEADS, HG_D, HG_D), F32)],
        compiler_params=_params(("arbitrary",)),
    )(u, u, u, l0, l1)


def gla_bwd(u, l0, l1, ss, do, fcol, rev, name):
    t = u.shape[0]
    tb, nb = _gla_blocks(t)

    def blk(i):
        return i if rev else (nb - 1 - i)

    def body(uq_ref, uf_ref, ui_ref, l0_ref, l1_ref, ss_ref, do_ref,
             dq_ref, df_ref, di_ref, dl0_ref, dl1_ref, dst_ref):
        @pl.when(pl.program_id(0) == 0)
        def _():
            dst_ref[...] = jnp.zeros(dst_ref.shape, F32)
            dl0_ref[...] = jnp.zeros(dl0_ref.shape, F32)
            dl1_ref[...] = jnp.zeros(dl1_ref.shape, F32)
        heads = range(HG_HEADS)
        _, vjp = jax.vjp(functools.partial(_gla_block, rev=rev), uq_ref[...], uf_ref[...], ui_ref[...],
                         l0_ref[...], l1_ref[...], tuple(ss_ref[0, h] for h in heads))
        dq, df, di, dl0, dl1, dst = vjp((do_ref[...], tuple(dst_ref[h] for h in heads)))
        dq_ref[...] = dq
        df_ref[...] = df
        di_ref[...] = di
        dl0_ref[...] += dl0
        dl1_ref[...] += dl1
        for h in heads:
            dst_ref[h] = dst[h]

    row = lambda cb: pl.BlockSpec((tb, HG_W), lambda i: (blk(i), cb))
    vec = pl.BlockSpec((1, HG_W), lambda i: (0, 0))
    orow = pl.BlockSpec((tb, HG_W), lambda i: (blk(i), 0))
    return pl.pallas_call(
        body, name=name, grid=(nb,),
        in_specs=[row(0), row(fcol), row(3), vec, vec,
                  pl.BlockSpec((1, HG_HEADS, HG_D, HG_D), lambda i: (blk(i), 0, 0, 0)), orow],
        out_specs=[orow, orow, orow, vec, vec],
        out_shape=[jax.ShapeDtypeStruct((t, HG_W), F32)] * 3 + [jax.ShapeDtypeStruct((1, HG_W), F32)] * 2,
        scratch_shapes=[pltpu.VMEM((HG_HEADS, HG_D, HG_D), F32)],
        compiler_params=_params(("arbitrary",)),
    )(u, u, u, l0, l1, ss, do)


def _rope_tables(t):
    rows = t // GRID_W
    row = jnp.repeat(jnp.arange(rows), GRID_W).astype(F32)
    col = jnp.tile(jnp.arange(GRID_W), rows).astype(F32)
    axis_dim = ATT_DH // 2
    freqs = ROPE_THETA ** (-jnp.arange(0, axis_dim, 2, dtype=F32) / axis_dim)
    ang = jnp.concatenate([row[:, None] * freqs, col[:, None] * freqs], axis=-1)
    cos2 = jnp.repeat(jnp.cos(ang), 2, axis=-1)
    sin2 = jnp.repeat(jnp.sin(ang), 2, axis=-1) * jnp.tile(jnp.array([-1.0, 1.0], F32), ATT_DH // 2)
    return cos2, sin2


def _group_sum_matrix(width):
    idx = np.arange(width) // ATT_DH
    return jnp.asarray((idx[:, None] == idx[None, :]).astype(np.float32))


def _tile_matrix(width):
    m = np.zeros((LANES, width), np.float32)
    m[np.arange(width) % ATT_DH, np.arange(width)] = 1.0
    return jnp.asarray(m)


def _tile_w(w128, tile_m):
    w8 = jnp.broadcast_to(w128, (8, LANES))
    return jnp.sum(_xdot(w8, tile_m), axis=0, keepdims=True) * 0.125


def _head_norm_rope(a, w128, cos_t, sin_t, gsum, tile_m, scale):
    ssq = _xdot(a * a, gsum)
    y = a * lax.rsqrt(ssq * (1.0 / ATT_DH) + EPS) * _tile_w(w128, tile_m)
    return (y * cos_t + swap_pairs(y) * sin_t) * scale


def _att_prep_fn(aq, ak, cq, sq, ck, sk, qw, kw, gq, gk, tq, tk):
    q = _head_norm_rope(aq, qw, cq, sq, gq, tq, ATT_DH ** -0.5)
    k = _head_norm_rope(ak, kw, ck, sk, gk, tk, 1.0)
    return q, k


def _head_t(x, heads):
    xt = x.T
    return [xt[h * ATT_DH:(h + 1) * ATT_DH] for h in range(heads)]


def _head_s(x, heads):
    lane = lax.broadcasted_iota(jnp.int32, (x.shape[0], LANES), 1)
    out = []
    for h in range(heads):
        pair = x[:, (h // 2) * LANES:(h // 2 + 1) * LANES]
        if h % 2:
            pair = pltpu.roll(pair, ATT_DH, 1)
        out.append(jnp.where(lane < ATT_DH, pair, 0.0))
    return out


def _from_head_t(tile):
    return jnp.concatenate([tile[h, 0] for h in range(tile.shape[0])], axis=0).T


def _pad_rows(a):
    return jnp.concatenate([a, jnp.zeros(a.shape, a.dtype)], axis=0)


def _col_bcast(row_vec):
    return jnp.broadcast_to(row_vec, (LANES, row_vec.shape[1])).T


FA_K_FWD = 1024
FA_K_BWD = 512
FA_STRIPS = 4


def _key_block(ref, j, tiles):
    return jnp.concatenate([ref[0, j * tiles + i] for i in range(tiles)], axis=1)


def fa_fwd(qt, ks, vt):
    _, ns, dh, nq = qt.shape
    lk = ks.shape[1]
    bk = min(FA_K_FWD, lk)
    nk = lk // bk
    spg = min(FA_STRIPS, ns)

    def body(q_ref, k_ref, v_ref, o_ref, lse_ref):
        qs = [_pad_rows(q_ref[0, c]) for c in range(spg)]

        def keys(j):
            return k_ref[0, pl.ds(pl.multiple_of(j * bk, bk), bk), :]

        def step(j, carry):
            st0, stats = carry
            kb = keys(j)
            vb = _key_block(v_ref, j, bk // nq)
            sts = [st0] + [_dg(kb, qs[c], 1, 0) for c in range(1, spg)]
            out = []
            for c in range(spg):
                m, l, acc = stats[c]
                m_new = jnp.maximum(m, jnp.max(sts[c], axis=0, keepdims=True))
                alpha = jnp.exp(m - m_new)
                p = jnp.exp(sts[c] - m_new)
                l = alpha * l + jnp.sum(p, axis=0, keepdims=True)
                if c == spg - 1:
                    st0 = _dg(keys(jnp.minimum(j + 1, nk - 1)), qs[0], 1, 0)
                acc = alpha * acc + _dg(vb, p, 1, 0)
                out.append((m_new, l, acc))
            return st0, tuple(out)

        init = tuple((jnp.full((1, nq), -jnp.inf, F32), jnp.zeros((1, nq), F32), jnp.zeros((dh, nq), F32))
                     for _ in range(spg))
        _, res = lax.fori_loop(0, nk, step, (_dg(keys(0), qs[0], 1, 0), init))
        for c in range(spg):
            m, l, acc = res[c]
            o_ref[0, c] = acc / l
            lse_ref[0, c] = _col_bcast(m + jnp.log(l))

    return pl.pallas_call(
        body, name="fa_fwd", grid=(ATT_KV, ns // spg),
        in_specs=[pl.BlockSpec((1, spg, dh, nq), lambda g, i: (g, i, 0, 0)),
                  pl.BlockSpec((1, lk, LANES), lambda g, i: (g, 0, 0)),
                  pl.BlockSpec((1, lk // nq, dh, nq), lambda g, i: (g, 0, 0, 0))],
        out_specs=[pl.BlockSpec((1, spg, dh, nq), lambda g, i: (g, i, 0, 0)),
                   pl.BlockSpec((1, spg, nq, LANES), lambda g, i: (g, i, 0, 0))],
        out_shape=[jax.ShapeDtypeStruct((ATT_KV, ns, dh, nq), F32),
                   jax.ShapeDtypeStruct((ATT_KV, ns, nq, LANES), F32)],
        compiler_params=_params(("arbitrary", "arbitrary")),
    )(qt, ks, vt)


def fa_bwd(qs, qt, dos, dot_, ot, lse, ks, kt, vt):
    _, ns, dh, nq = qt.shape
    lk = ks.shape[1]
    bk = min(FA_K_BWD, lk)
    nk = lk // bk
    tiles = bk // nq
    spg = min(FA_STRIPS, ns)
    nc = bk // LANES

    def body(qs_ref, qt_ref, dos_ref, dot_ref, ot_ref, lse_ref, ks_ref, kt_ref, vt_ref, dq_ref, dk_ref, dv_ref):
        @pl.when(pl.program_id(1) == 0)
        def _():
            dk_ref[...] = jnp.zeros(dk_ref.shape, F32)
            dv_ref[...] = jnp.zeros(dv_ref.shape, F32)

        strips = range(spg)
        lse_b = [lse_ref[0, c] for c in strips]
        d_b = [_col_bcast(jnp.sum(dot_ref[0, c].astype(F32) * ot_ref[0, c], axis=0, keepdims=True)) for c in strips]

        def step(j, dqs):
            ktb, vtb = _pad_rows(_key_block(kt_ref, j, tiles)), _pad_rows(_key_block(vt_ref, j, tiles))
            kb = ks_ref[0, pl.ds(pl.multiple_of(j * bk, bk), bk), :]
            prods = [(_dg(qs_ref[0, c], ktb, 1, 0), _dg(dos_ref[0, c], vtb, 1, 0)) for c in strips]
            out = []
            for c in strips:
                s, dp = prods[c]
                ps, dss = [], []
                for cc in range(nc):
                    sl = slice(cc * LANES, (cc + 1) * LANES)
                    pc = jnp.exp(s[:, sl] - lse_b[c])
                    ps.append(pc.astype(BF16))
                    dss.append((pc * (dp[:, sl] - d_b[c])).astype(BF16))
                p, ds = jnp.concatenate(ps, axis=1), jnp.concatenate(dss, axis=1)
                dv = _dg(dot_ref[0, c], p, 1, 0)
                dk = _dg(qt_ref[0, c], ds, 1, 0)
                for i in range(tiles):
                    dv_ref[0, j * tiles + i] += dv[:, i * nq:(i + 1) * nq]
                    dk_ref[0, j * tiles + i] += dk[:, i * nq:(i + 1) * nq]
                out.append(dqs[c] + _dg(ds, kb, 1, 0))
            return tuple(out)

        dqs = lax.fori_loop(0, nk, step, tuple(jnp.zeros((nq, LANES), F32) for _ in strips))
        for c in strips:
            dq_ref[0, c] = dqs[c].T[:dh]

    sspec = pl.BlockSpec((1, spg, nq, LANES), lambda g, i: (g, i, 0, 0))
    tspec = pl.BlockSpec((1, spg, dh, nq), lambda g, i: (g, i, 0, 0))
    kspec = pl.BlockSpec((1, lk // nq, dh, nq), lambda g, i: (g, 0, 0, 0))
    return pl.pallas_call(
        body, name="fa_bwd", grid=(ATT_KV, ns // spg),
        in_specs=[sspec, tspec, sspec, tspec, tspec, sspec, pl.BlockSpec((1, lk, LANES), lambda g, i: (g, 0, 0)),
                  kspec, kspec],
        out_specs=[tspec, kspec, kspec],
        out_shape=[jax.ShapeDtypeStruct((ATT_KV, ns, dh, nq), F32),
                   jax.ShapeDtypeStruct((ATT_KV, lk // nq, dh, nq), F32),
                   jax.ShapeDtypeStruct((ATT_KV, lk // nq, dh, nq), F32)],
        compiler_params=_params(("arbitrary", "arbitrary")),
    )(qs, qt, dos, dot_, ot, lse, ks, kt, vt)


def _post_mix_fn(of, ob, ug, oa, hgw, attw):
    o = of + ob
    parts = []
    for h in range(HG_HEADS):
        parts.append(_rms(o[:, h * HG_D:(h + 1) * HG_D], hgw))
    hg = jnp.concatenate(parts, axis=1) * (ug * _sigmoid(ug))
    return jnp.concatenate([hg, _rms(oa, attw)], axis=1)


def _swiglu_fn(gate, up):
    return gate * _sigmoid(gate) * up


def _loss_fn(x2, w, tgt):
    e = _rms(x2, w) - tgt
    return 0.5 * jnp.sum(jnp.mean(e * e, axis=-1, keepdims=True), axis=0, keepdims=True)


def _place():
    return lax.axis_index("x"), lax.axis_index("y"), lax.axis_index("c")


def _other_chips(x, y):
    return [(1 - x, y), (x, 1 - y), (1 - x, 1 - y)]


def _hbm_specs(n):
    return [pl.BlockSpec(memory_space=pl.ANY)] * n


def allgather_weights(shards):
    n = len(shards)

    def body(*refs):
        srcs, outs = refs[:n], refs[n:2 * n]
        ssem, rsem, lsem = refs[2 * n:]
        x, y, c = _place()
        k = 2 * x + y
        sib = (x, y, 1 - c)
        chips = _other_chips(x, y)

        def half(a, kk, cc):
            hr = srcs[a].shape[0] // 2
            return outs[a].at[kk, pl.ds(cc * hr, hr), :]

        def my_half(a):
            hr = srcs[a].shape[0] // 2
            return srcs[a].at[pl.ds(c * hr, hr), :]

        def copy(a, j, src_ref, dst_ref, to):
            return pltpu.make_async_remote_copy(src_ref=src_ref, dst_ref=dst_ref, send_sem=ssem.at[6 * a + j],
                                                recv_sem=rsem.at[6 * a + j], device_id=to, device_id_type=MESH)

        mine = [pltpu.make_async_copy(srcs[a], outs[a].at[k], lsem.at[a]) for a in range(n)]
        for cp in mine:
            cp.start()
        first = [copy(a, j, my_half(a), half(a, k, c), (px, py, c)) for a in range(n) for j, (px, py) in enumerate(chips)]
        for cp in first:
            cp.start()
        passed = []
        for a in range(n):
            for j, (px, py) in enumerate(chips):
                copy(a, j, my_half(a), half(a, 2 * px + py, c), (px, py, c)).wait_recv()
                cp = copy(a, 3 + j, half(a, 2 * px + py, c), half(a, 2 * px + py, c), sib)
                cp.start()
                passed.append(cp)
        for a in range(n):
            for j, (px, py) in enumerate(chips):
                copy(a, 3 + j, my_half(a), half(a, 2 * px + py, 1 - c), sib).wait_recv()
        for cp in first + passed:
            cp.wait_send()
        for cp in mine:
            cp.wait()

    return pl.pallas_call(
        body, name="allgather_weights", in_specs=_hbm_specs(n), out_specs=_hbm_specs(n),
        out_shape=[jax.ShapeDtypeStruct((N_CHIPS,) + s.shape, s.dtype) for s in shards],
        scratch_shapes=[pltpu.SemaphoreType.DMA((6 * n,)), pltpu.SemaphoreType.DMA((6 * n,)),
                        pltpu.SemaphoreType.DMA((n,))],
    )(*shards)


def rs_siblings(gs):
    n = len(gs)

    def body(*refs):
        srcs, outs = refs[:n], refs[n:2 * n]
        ssem, rsem = refs[2 * n:]
        x, y, c = _place()
        cps = []
        for a in range(n):
            hr = srcs[a].shape[1] // 2
            cp = pltpu.make_async_remote_copy(src_ref=srcs[a].at[:, pl.ds((1 - c) * hr, hr), :], dst_ref=outs[a],
                                              send_sem=ssem.at[a], recv_sem=rsem.at[a], device_id=(x, y, 1 - c),
                                              device_id_type=MESH)
            cp.start()
            cps.append(cp)
        for cp in cps:
            cp.wait()

    return pl.pallas_call(
        body, name="rs_siblings", in_specs=_hbm_specs(n), out_specs=_hbm_specs(n),
        out_shape=[jax.ShapeDtypeStruct((N_CHIPS, g.shape[1] // 2, g.shape[2]), F32) for g in gs],
        scratch_shapes=[pltpu.SemaphoreType.DMA((n,)), pltpu.SemaphoreType.DMA((n,))],
    )(*gs)


def rs_chips(pas):
    n = len(pas)

    def body(*refs):
        srcs, outs = refs[:n], refs[n:2 * n]
        ssem, rsem, lsem = refs[2 * n:]
        x, y, c = _place()
        k = 2 * x + y
        sib = (x, y, 1 - c)
        chips = _other_chips(x, y)

        def half(a, kk, cc):
            hr = srcs[a].shape[1]
            return outs[a].at[kk, pl.ds(cc * hr, hr), :]

        def copy(a, j, src_ref, dst_ref, to):
            return pltpu.make_async_remote_copy(src_ref=src_ref, dst_ref=dst_ref, send_sem=ssem.at[7 * a + j],
                                                recv_sem=rsem.at[7 * a + j], device_id=to, device_id_type=MESH)

        mine = [pltpu.make_async_copy(srcs[a].at[k], half(a, k, c), lsem.at[a]) for a in range(n)]
        for cp in mine:
            cp.start()
        first = [copy(a, j, srcs[a].at[2 * px + py], half(a, k, c), (px, py, c))
                 for a in range(n) for j, (px, py) in enumerate(chips)]
        first += [copy(a, 6, srcs[a].at[k], half(a, k, c), sib) for a in range(n)]
        for cp in first:
            cp.start()
        passed = []
        for a in range(n):
            for j, (px, py) in enumerate(chips):
                copy(a, j, srcs[a].at[k], half(a, 2 * px + py, c), (px, py, c)).wait_recv()
                cp = copy(a, 3 + j, half(a, 2 * px + py, c), half(a, 2 * px + py, c), sib)
                cp.start()
                passed.append(cp)
        for a in range(n):
            copy(a, 6, srcs[a].at[k], half(a, k, 1 - c), sib).wait_recv()
            for j, (px, py) in enumerate(chips):
                copy(a, 3 + j, srcs[a].at[k], half(a, 2 * px + py, 1 - c), sib).wait_recv()
        for cp in first + passed:
            cp.wait_send()
        for cp in mine:
            cp.wait()

    return pl.pallas_call(
        body, name="rs_chips", in_specs=_hbm_specs(n), out_specs=_hbm_specs(n),
        out_shape=[jax.ShapeDtypeStruct((N_CHIPS, 2 * p.shape[1], p.shape[2]), p.dtype) for p in pas],
        scratch_shapes=[pltpu.SemaphoreType.DMA((7 * n,)), pltpu.SemaphoreType.DMA((7 * n,)),
                        pltpu.SemaphoreType.DMA((n,))],
    )(*pas)


def allreduce_small(p, name):
    rows, width = p.shape

    def body(p_ref, s_ref, gath, ssem, rsem):
        x, y, c = _place()
        me = 4 * x + 2 * y + c
        copies = []
        for d in range(1, N_DEV):
            dx, dy, dc = (d >> 2) & 1, (d >> 1) & 1, d & 1
            peer = (x ^ dx, y ^ dy, c ^ dc)
            cp = pltpu.make_async_remote_copy(src_ref=p_ref, dst_ref=gath.at[me], send_sem=ssem.at[d - 1],
                                              recv_sem=rsem.at[d - 1], device_id=peer, device_id_type=MESH)
            cp.start()
            copies.append(cp)
        gath[me] = p_ref[...]
        for d, cp in enumerate(copies, start=1):
            dx, dy, dc = (d >> 2) & 1, (d >> 1) & 1, d & 1
            peer_slot = 4 * (x ^ dx) + 2 * (y ^ dy) + (c ^ dc)
            pltpu.make_async_remote_copy(src_ref=p_ref, dst_ref=gath.at[peer_slot], send_sem=ssem.at[d - 1],
                                         recv_sem=rsem.at[d - 1], device_id=(x ^ dx, y ^ dy, c ^ dc),
                                         device_id_type=MESH).wait_recv()
        for cp in copies:
            cp.wait_send()
        acc = gath[0]
        for d in range(1, N_DEV):
            acc = acc + gath[d]
        s_ref[...] = acc

    return pl.pallas_call(
        body, name=name,
        in_specs=[pl.BlockSpec(memory_space=pltpu.VMEM)],
        out_specs=pl.BlockSpec(memory_space=pltpu.VMEM),
        out_shape=jax.ShapeDtypeStruct((rows, width), F32),
        scratch_shapes=[pltpu.VMEM((N_DEV, rows, width), F32), pltpu.SemaphoreType.DMA((N_DEV - 1,)),
                        pltpu.SemaphoreType.DMA((N_DEV - 1,))],
    )(p)


def add_my_half(g, recv, name):
    _, r, cols = g.shape
    hr = r // 2
    tb = _pick(hr, 512, 8)
    nb = hr // tb
    c_arr = lax.axis_index("c").astype(jnp.int32).reshape(1)

    def body(c_ref, g_ref, r_ref, o_ref):
        o_ref[...] = (g_ref[...] + r_ref[...]).astype(o_ref.dtype)

    return pl.pallas_call(
        body, name=name,
        grid_spec=pltpu.PrefetchScalarGridSpec(
            num_scalar_prefetch=1, grid=(N_CHIPS, nb),
            in_specs=[pl.BlockSpec((1, tb, cols), lambda k, i, c_ref: (k, c_ref[0] * nb + i, 0)),
                      pl.BlockSpec((1, tb, cols), lambda k, i, c_ref: (k, i, 0))],
            out_specs=pl.BlockSpec((1, tb, cols), lambda k, i, c_ref: (k, i, 0))),
        out_shape=jax.ShapeDtypeStruct((N_CHIPS, hr, cols), BF16),
        compiler_params=_params(("arbitrary", "arbitrary")),
    )(c_arr, g, recv)


def sum_chips(parts, name):
    _, hr, cols = parts.shape
    tb = _pick(hr, 512, 8)

    def body(p_ref, o_ref):
        o_ref[...] = ((p_ref[0].astype(F32) + p_ref[1].astype(F32)) + p_ref[2].astype(F32)) + p_ref[3].astype(F32)

    return pl.pallas_call(
        body, name=name, grid=(hr // tb,),
        in_specs=[pl.BlockSpec((N_CHIPS, tb, cols), lambda i: (0, i, 0))],
        out_specs=pl.BlockSpec((tb, cols), lambda i: (i, 0)),
        out_shape=jax.ShapeDtypeStruct((hr, cols), F32),
        compiler_params=_params(("arbitrary",)),
    )(parts)


def adamw(w, g, m, v, name):
    rows, width = w.shape
    tb = _pick(rows, 512, 8)

    def body(w_ref, g_ref, m_ref, v_ref, d_ref, mo_ref, vo_ref):
        gg = g_ref[...]
        m_new = ADAM_B1 * m_ref[...] + (1.0 - ADAM_B1) * gg
        v_new = ADAM_B2 * v_ref[...] + (1.0 - ADAM_B2) * (gg * gg)
        m_hat = m_new / (1.0 - ADAM_B1 ** ADAM_STEP)
        v_hat = v_new / (1.0 - ADAM_B2 ** ADAM_STEP)
        d_ref[...] = -ADAM_LR * (m_hat / (jnp.sqrt(v_hat) + ADAM_EPS) + ADAM_WD * w_ref[...])
        mo_ref[...] = m_new
        vo_ref[...] = v_new

    spec = pl.BlockSpec((tb, width), lambda i: (i, 0))
    return pl.pallas_call(
        body, name=name, grid=(rows // tb,), in_specs=[spec] * 4, out_specs=[spec] * 3,
        out_shape=[jax.ShapeDtypeStruct((rows, width), F32)] * 3,
        compiler_params=_params(("arbitrary",)),
    )(w, g, m, v)


def _pack_rows(vecs, width=1024):
    rows, cur, used = [], [], 0
    for v in vecs:
        n = v.shape[1]
        if used + n > width:
            cur.append(jnp.zeros((1, width - used), F32))
            rows.append(jnp.concatenate(cur, axis=1))
            cur, used = [], 0
        cur.append(v)
        used += n
    cur.append(jnp.zeros((1, width - used), F32))
    rows.append(jnp.concatenate(cur, axis=1))
    return rows


def _pad128(v):
    return jnp.pad(v, ((0, 0), (0, LANES - v.shape[1])))


def kernel(x, norm1_w, w_in, lb_logits, hg_norm_w, q_norm_w, k_norm_w, att_norm_w, w_out, norm2_w, w_gate_up, w_down, final_norm_w, loss_target, m_norm1_w, m_w_in, m_lb_logits, m_hg_norm_w, m_q_norm_w, m_k_norm_w, m_att_norm_w, m_w_out, m_norm2_w, m_w_gate_up, m_w_down, m_final_norm_w, v_norm1_w, v_w_in, v_lb_logits, v_hg_norm_w, v_q_norm_w, v_k_norm_w, v_att_norm_w, v_w_out, v_norm2_w, v_w_gate_up, v_w_down, v_final_norm_w):
    t, d = x.shape[1], x.shape[2]
    xi, yi, ci = _place()
    chip = 2 * xi + yi
    x2d = x.reshape(t, d)
    tgt = loss_target.reshape(t, d)
    tok = min(TOK, t)
    nt = t // tok
    ns = ATT_GROUP * nt

    g_in, g_out, g_gu, g_down = allgather_weights([w_in[0].astype(BF16), w_out[0].astype(BF16),
                                                   w_gate_up[0].astype(BF16), w_down[0].astype(BF16)])
    wf_in = g_in.transpose(1, 0, 2).reshape(g_in.shape[1], -1)
    wf_gu = g_gu.transpose(1, 0, 2).reshape(g_gu.shape[1], -1)
    wf_out = g_out.reshape(-1, g_out.shape[2])
    wf_down = g_down.reshape(-1, g_down.shape[2])
    lb_rows = lb_logits.reshape(4, LANES) * (ci == 0).astype(F32)
    lb_pad = lax.dynamic_update_slice(jnp.zeros((8, 1024), F32), lb_rows, (0, chip * LANES))
    lb_full = allreduce_small(lb_pad, "gather_lb")[:4, :HG_W]
    l_f0, l_f1, l_b0, l_b1 = (lb_full[i:i + 1] for i in range(4))

    n1 = norm1_w.reshape(1, d)
    n2 = norm2_w.reshape(1, d)
    nf = final_norm_w.reshape(1, d)
    tm = min(512, t)
    (h1,) = _rows(lambda a, w: ((_rms(a, w),), ()), "norm1", t, tm, [_rin(x2d, tm)], [n1], [_rout(t, tm, d, BF16)])
    u = mm_rows(h1, wf_in, "mm_in")
    o_f, ss_f = gla_fwd(u, l_f0, l_f1, 1, False, "gla_fwd_f")
    o_b, ss_b = gla_fwd(u, l_b0, l_b1, 2, True, "gla_fwd_b")

    cos2, sin2 = _rope_tables(t)
    cq, sq = jnp.tile(cos2, (1, ATT_HEADS)), jnp.tile(sin2, (1, ATT_HEADS))
    ck, sk = jnp.tile(cos2, (1, ATT_KV)), jnp.tile(sin2, (1, ATT_KV))
    qw, kw = _pad128(q_norm_w.reshape(1, ATT_DH)), _pad128(k_norm_w.reshape(1, ATT_DH))
    gq, gk = _group_sum_matrix(ATT_QW), _group_sum_matrix(ATT_KVW)
    tq, tk = _tile_matrix(ATT_QW), _tile_matrix(ATT_KVW)
    prep_in = [_rin(u, tok, ATT_QW, 5), _rin(u, tok, ATT_KVW, 24), _rin(cq, tok), _rin(sq, tok), _rin(ck, tok),
               _rin(sk, tok)]
    prep_consts = [qw, kw, gq, gk, tq, tk]

    def att_prep_fn(aq, ak, av, *rest):
        q, k = _att_prep_fn(aq, ak, *rest)
        return (_head_t(q, ATT_HEADS), _head_s(q, ATT_HEADS), _head_s(k, ATT_KV), _head_t(k, ATT_KV),
                _head_t(av, ATT_KV)), ()

    q_t, q_s, k_s, k_t, v_t = _rows(
        att_prep_fn, "att_prep", t, tok, prep_in[:2] + [_rin(u, tok, ATT_KVW, 25)] + prep_in[2:], prep_consts,
        [_tout(ATT_HEADS, nt, ATT_DH, tok, BF16), _tout(ATT_HEADS, nt, tok, LANES, BF16),
         _tout(ATT_KV, nt, tok, LANES, BF16), _tout(ATT_KV, nt, ATT_DH, tok, BF16),
         _tout(ATT_KV, nt, ATT_DH, tok, BF16)])
    q_t = q_t.reshape(ATT_KV, ns, ATT_DH, tok)
    q_s = q_s.reshape(ATT_KV, ns, tok, LANES)
    k_s = k_s.reshape(ATT_KV, t, LANES)
    o_t, lse = fa_fwd(q_t, k_s, v_t)
    o_tiles = o_t.reshape(ATT_HEADS, nt, ATT_DH, tok)

    hgw = hg_norm_w.reshape(1, HG_D)
    attw = att_norm_w.reshape(1, ATT_QW)
    mix_in = [_rin(o_f, tok), _rin(o_b, tok), _rin(u, tok, HG_W, 4), _tin(o_tiles)]
    (mix,) = _rows(lambda of, ob, ug, ot, hw, aw: ((_post_mix_fn(of, ob, ug, _from_head_t(ot), hw, aw),), ()),
                   "post_mix", t, tok, mix_in, [hgw, attw], [_rout(t, tok, d, BF16)])
    x1, h2 = mm_res_norm(mix, wf_out, x2d, n2, "mm_out")
    gu2, act = ffn_up(h2, wf_gu)

    dx2, dx2b, g_final, loss_part = mm_res_loss(act, wf_down, x1, tgt, nf, "mm_down_loss")
    dgu2 = ffn_down_dx(dx2b, wf_down, gu2)
    gw_down = mm_tn(act, dx2b, "mm_down_dw")
    dx1, dx1b, g_norm2 = mm_nt_norm_bwd(dgu2, wf_gu, x1, n2, dx2, "mm_gate_up_dx", True)
    gw_gu = mm_tn(h2, dgu2, "mm_gate_up_dw", col_shards=N_CHIPS)
    dmix = mm_rows(dx1b, wf_out, "mm_out_dx", trans_b=True)
    gw_out = mm_tn(mix, dx1b, "mm_out_dw")

    def post_mix_bwd_fn(of, ob, ug, ot, dm, hgw_, attw_):
        _, vjp = jax.vjp(_post_mix_fn, of, ob, ug, _from_head_t(ot), hgw_, attw_)
        dof, _, dug, doa, dhgw, dattw = vjp(dm)
        return (dof, dug, _head_t(doa, ATT_HEADS), _head_s(doa, ATT_HEADS)), (dhgw, dattw)

    do_hg, du_g, do_t, do_s, g_hg, g_att = _rows(
        post_mix_bwd_fn, "post_mix_bwd", t, tok, mix_in + [_rin(dmix, tok)], [hgw, attw],
        [_rout(t, tok, HG_W, F32), _rout(t, tok, HG_W, BF16), _tout(ATT_HEADS, nt, ATT_DH, tok, BF16),
         _tout(ATT_HEADS, nt, tok, LANES, BF16)], [HG_D, ATT_QW])
    dq_t, dk_t, dv_t = fa_bwd(q_s, q_t, do_s.reshape(q_s.shape), do_t.reshape(q_t.shape), o_t, lse, k_s, k_t, v_t)

    def att_prep_bwd_fn(aq, ak, cq_, sq_, ck_, sk_, dqt, dkt, dvt, qw_, kw_, gq_, gk_, tq_, tk_):
        _, vjp = jax.vjp(lambda a, b, c_, e: _att_prep_fn(a, b, cq_, sq_, ck_, sk_, c_, e, gq_, gk_, tq_, tk_),
                         aq, ak, qw_, kw_)
        daq, dak, dqw, dkw = vjp((_from_head_t(dqt), _from_head_t(dkt)))
        return (daq, dak, _from_head_t(dvt)), (dqw, dkw)

    da_q, da_k, da_v, g_q, g_k = _rows(
        att_prep_bwd_fn, "att_prep_bwd", t, tok,
        prep_in + [_tin(dq_t.reshape(ATT_HEADS, nt, ATT_DH, tok)), _tin(dk_t), _tin(dv_t)], prep_consts,
        [_rout(t, tok, ATT_QW, BF16), _rout(t, tok, ATT_KVW, BF16), _rout(t, tok, ATT_KVW, BF16)], [LANES, LANES])

    dq_f, df_f, di_f, dl_f0, dl_f1 = gla_bwd(u, l_f0, l_f1, ss_f, do_hg, 1, False, "gla_bwd_f")
    dq_b, df_b, di_b, dl_b0, dl_b1 = gla_bwd(u, l_b0, l_b1, ss_b, do_hg, 2, True, "gla_bwd_b")

    def assemble_fn(qf, qb, ff, fb, i_f, i_b, dg, aq, ak, av):
        parts = [qf + qb, ff, fb, i_f + i_b, dg.astype(F32), aq.astype(F32), ak.astype(F32), av.astype(F32)]
        return (jnp.concatenate(parts, axis=1),), ()

    (du,) = _rows(assemble_fn, "assemble_du", t, tok,
                  [_rin(a, tok) for a in (dq_f, dq_b, df_f, df_b, di_f, di_b, du_g, da_q, da_k, da_v)], [],
                  [_rout(t, tok, u.shape[1], BF16)])
    grad_x, g_norm1 = mm_nt_norm_bwd(du, wf_in, x2d, n1, dx1, "mm_in_dx", False)
    gw_in_t = mm_tn(h1, du, "mm_in_dw", transpose_out=True)

    names = ["w_in", "w_out", "w_gate_up", "w_down"]
    gsh = [gw_in_t.reshape(N_CHIPS, -1, d), gw_out.reshape(N_CHIPS, -1, d), gw_gu, gw_down.reshape(N_CHIPS, -1, d)]
    recv = rs_siblings(gsh)
    chip_part = [add_my_half(g, r, "add_my_half_" + n) for g, r, n in zip(gsh, recv, names)]
    parts = rs_chips(chip_part)
    g_shard = [sum_chips(p, "sum_chips_" + n) for p, n in zip(parts, names)]
    g_shard[0] = g_shard[0].T
    big = {}
    for n, g, w, m, v in zip(names, g_shard, (w_in, w_out, w_gate_up, w_down), (m_w_in, m_w_out, m_w_gate_up, m_w_down),
                             (v_w_in, v_w_out, v_w_gate_up, v_w_down)):
        dlt, mn, vn = adamw(w[0], g, m[0], v[0], "adamw_" + n)
        big[n] = (g[None], dlt[None], mn[None], vn[None])

    small = [g_norm1, g_norm2, g_final, g_att, g_hg, g_q, g_k, loss_part, dl_f0, dl_f1, dl_b0, dl_b1]
    packed = _pack_rows(small)
    packed += [jnp.zeros((1, 1024), F32)] * (8 - len(packed))
    tot = allreduce_small(jnp.concatenate(packed, axis=0), "allreduce_small")
    s_norm1, s_norm2, s_final = tot[0:1], tot[1:2], tot[2:3]
    s_att, s_hg, s_q, s_k = tot[3:4, 0:512], tot[3:4, 512:640], tot[3:4, 640:704], tot[3:4, 768:832]
    loss = tot[3, 896]
    s_lb = jnp.concatenate([tot[4:5, 0:512], tot[4:5, 512:1024], tot[5:6, 0:512], tot[5:6, 512:1024]], axis=0)
    s_lb = lax.dynamic_slice(s_lb, (0, chip * LANES), (4, LANES)).reshape(1, 512)

    snames = ["norm1_w", "lb_logits", "hg_norm_w", "q_norm_w", "k_norm_w", "att_norm_w", "norm2_w", "final_norm_w"]
    g_small = dict(zip(snames, [s_norm1, s_lb, s_hg, s_q, s_k, s_att, s_norm2, s_final]))
    w_small = dict(zip(snames, [norm1_w, lb_logits, hg_norm_w, q_norm_w, k_norm_w, att_norm_w, norm2_w, final_norm_w]))
    m_small = dict(zip(snames, [m_norm1_w, m_lb_logits, m_hg_norm_w, m_q_norm_w, m_k_norm_w, m_att_norm_w, m_norm2_w, m_final_norm_w]))
    v_small = dict(zip(snames, [v_norm1_w, v_lb_logits, v_hg_norm_w, v_q_norm_w, v_k_norm_w, v_att_norm_w, v_norm2_w, v_final_norm_w]))

    def pack_small(tree):
        rows = _pack_rows([tree[n].reshape(1, -1) for n in snames])
        rows += [jnp.zeros((1, 1024), F32)] * (8 - len(rows))
        return jnp.concatenate(rows, axis=0)

    d_s, m_s, v_s = adamw(pack_small(w_small), pack_small(g_small), pack_small(m_small), pack_small(v_small), "adamw_small")

    def unpack_small(a):
        out, r, used = {}, 0, 0
        for n in snames:
            size = w_small[n].size
            if used + size > 1024:
                r, used = r + 1, 0
            out[n] = a[r, used:used + size].reshape(w_small[n].shape)
            used += size
        return out

    d_sm, m_sm, v_sm = unpack_small(d_s), unpack_small(m_s), unpack_small(v_s)
    g_sm = {n: g_small[n].reshape(w_small[n].shape) for n in snames}

    order = ["norm1_w", "w_in", "lb_logits", "hg_norm_w", "q_norm_w", "k_norm_w", "att_norm_w", "w_out", "norm2_w",
             "w_gate_up", "w_down", "final_norm_w"]

    def pick(small_tree, idx):
        return [big[n][idx] if n in big else small_tree[n] for n in order]

    return (loss, grad_x.reshape(x.shape), *pick(g_sm, 0), *pick(d_sm, 1), *pick(m_sm, 2), *pick(v_sm, 3))
```

```python
import functools

import numpy as np
import jax
import jax.numpy as jnp
from jax import lax
from jax.experimental import pallas as pl
from jax.experimental.pallas import tpu as pltpu

F32 = jnp.float32
BF16 = jnp.bfloat16
MESH = pl.DeviceIdType.MESH

EPS = 1e-6
GRID_W = 64
HG_HEADS = 4
HG_D = 128
HG_W = HG_HEADS * HG_D
CHUNK = 64
ATT_HEADS = 8
ATT_KV = 2
ATT_GROUP = ATT_HEADS // ATT_KV
ATT_DH = 64
ATT_QW = ATT_HEADS * ATT_DH
ATT_KVW = ATT_KV * ATT_DH
ROPE_THETA = 10000.0
N_CHIPS = 4
N_DEV = 8

ADAM_LR = 0.001
ADAM_B1 = 0.9
ADAM_B2 = 0.999
ADAM_EPS = 1e-08
ADAM_WD = 0.01
ADAM_STEP = 10

VMEM_LIMIT = 52 * 1024 * 1024
LANES = 128
TOK = 256


def _params(sem=None):
    return pltpu.CompilerParams(dimension_semantics=sem, vmem_limit_bytes=VMEM_LIMIT)


def _dg(a, b, ca, cb):
    return lax.dot_general(a.astype(BF16), b.astype(BF16), (((ca,), (cb,)), ((), ())),
                           preferred_element_type=F32)


@jax.custom_vjp
def dot_nn(a, b):
    return _dg(a, b, 1, 0)


def _dot_nn_fwd(a, b):
    return _dg(a, b, 1, 0), (a, b)


def _dot_nn_bwd(res, g):
    a, b = res
    return _dg(g, b, 1, 1), _dg(a, g, 0, 0)


dot_nn.defvjp(_dot_nn_fwd, _dot_nn_bwd)


@jax.custom_vjp
def dot_nt(a, b):
    return _dg(a, b, 1, 1)


def _dot_nt_fwd(a, b):
    return _dg(a, b, 1, 1), (a, b)


def _dot_nt_bwd(res, g):
    a, b = res
    return _dg(g, b, 1, 0), _dg(g, a, 0, 0)


dot_nt.defvjp(_dot_nt_fwd, _dot_nt_bwd)


@jax.custom_vjp
def dot_tn(a, b):
    return _dg(a, b, 0, 0)


def _dot_tn_fwd(a, b):
    return _dg(a, b, 0, 0), (a, b)


def _dot_tn_bwd(res, g):
    a, b = res
    return _dg(b, g, 1, 1), _dg(a, g, 1, 0)


dot_tn.defvjp(_dot_tn_fwd, _dot_tn_bwd)


def _split3(a):
    hi = a.astype(BF16)
    r1 = a - hi.astype(F32)
    mid = r1.astype(BF16)
    lo = (r1 - mid.astype(F32)).astype(BF16)
    return lo, mid, hi


def _sum3(terms):
    lo, mid, hi = terms
    return (lo + mid) + hi


@jax.custom_vjp
def xdot_r(a, m):
    return _sum3([_dg(p, m, 1, 0) for p in _split3(a)])


def _xdot_r_fwd(a, m):
    return xdot_r(a, m), m


def _xdot_r_bwd(m, g):
    return _sum3([_dg(p, m, 1, 1) for p in _split3(g)]), jnp.zeros_like(m)


xdot_r.defvjp(_xdot_r_fwd, _xdot_r_bwd)


@jax.custom_vjp
def xdot_l(m, a):
    return _sum3([_dg(m, p, 1, 0) for p in _split3(a)])


def _xdot_l_fwd(m, a):
    return xdot_l(m, a), m


def _xdot_l_bwd(m, g):
    return jnp.zeros_like(m), _sum3([_dg(m, p, 0, 0) for p in _split3(g)])


xdot_l.defvjp(_xdot_l_fwd, _xdot_l_bwd)


@jax.custom_vjp
def swap_pairs(y):
    n = y.shape[-1]
    lane = lax.broadcasted_iota(jnp.int32, y.shape, 1)
    nxt = pltpu.roll(y, n - 1, 1)
    prv = pltpu.roll(y, 1, 1)
    return jnp.where(lane % 2 == 0, nxt, prv)


def _swap_fwd(y):
    return swap_pairs(y), None


def _swap_bwd(_, g):
    return (swap_pairs(g),)


swap_pairs.defvjp(_swap_fwd, _swap_bwd)


def _rms(x, w):
    return x * lax.rsqrt(jnp.mean(x * x, axis=-1, keepdims=True) + EPS) * w


def _sigmoid(x):
    return jax.nn.sigmoid(x)


def _rows(fn, name, t, tm, ins, consts, outs, accs=()):
    n_r, n_c, n_o, n_a = len(ins), len(consts), len(outs), len(accs)

    def body(*refs):
        r = refs[:n_r]
        c = refs[n_r:n_r + n_c]
        o = refs[n_r + n_c:n_r + n_c + n_o]
        a = refs[n_r + n_c + n_o:]
        ro, ao = fn(*[x[...] for x in r], *[x[...] for x in c])
        for ref, val in zip(o, ro):
            if isinstance(val, (list, tuple)):
                for h, piece in enumerate(val):
                    ref[h, 0] = piece.astype(ref.dtype)
            else:
                ref[...] = val.astype(ref.dtype)
        if n_a:
            @pl.when(pl.program_id(0) == 0)
            def _():
                for ref in a:
                    ref[...] = jnp.zeros(ref.shape, F32)
            for ref, val in zip(a, ao):
                ref[...] += val

    in_specs = [s for _, s in ins]
    in_specs += [pl.BlockSpec(c.shape, lambda i, nd=c.ndim: (0,) * nd) for c in consts]
    out_specs = [s for _, s in outs] + [pl.BlockSpec((1, w), lambda i: (0, 0)) for w in accs]
    out_shape = [s for s, _ in outs] + [jax.ShapeDtypeStruct((1, w), F32) for w in accs]
    return pl.pallas_call(
        body, name=name, grid=(t // tm,), in_specs=in_specs, out_specs=out_specs, out_shape=out_shape,
        compiler_params=_params(("arbitrary",)),
    )(*[a for a, _ in ins], *consts)


def _rin(a, tm, width=None, cb=0):
    width = a.shape[1] if width is None else width
    return a, pl.BlockSpec((tm, width), lambda i, cb=cb: (i, cb))


def _rout(t, tm, width, dtype):
    return jax.ShapeDtypeStruct((t, width), dtype), pl.BlockSpec((tm, width), lambda i: (i, 0))


def _tin(a):
    return a, pl.BlockSpec((a.shape[0], 1) + a.shape[2:], lambda i: (0, i, 0, 0))


def _tout(heads, nt, r, c, dtype):
    return jax.ShapeDtypeStruct((heads, nt, r, c), dtype), pl.BlockSpec((heads, 1, r, c), lambda i: (0, i, 0, 0))


def _pick(n, cap, mult):
    best = None
    for d in range(mult, min(n, cap) + 1, mult):
        if n % d == 0:
            best = d
    return best if best is not None else n


def mm_rows(a, b, name, trans_b=False, res=None, out_dtype=F32):
    m, k = a.shape
    n = b.shape[0] if trans_b else b.shape[1]
    tn = _pick(n, 3328, LANES)
    tm = _pick(m, 512 if k <= 3072 else 256, 8)
    has_res = res is not None

    def body(*refs):
        if has_res:
            a_ref, b_ref, r_ref, o_ref = refs
        else:
            a_ref, b_ref, o_ref = refs
        acc = _dg(a_ref[...], b_ref[...], 1, 1 if trans_b else 0)
        if has_res:
            acc = acc + r_ref[...]
        o_ref[...] = acc.astype(o_ref.dtype)

    in_specs = [pl.BlockSpec((tm, k), lambda j, i: (i, 0))]
    if trans_b:
        in_specs.append(pl.BlockSpec((tn, k), lambda j, i: (j, 0)))
    else:
        in_specs.append(pl.BlockSpec((k, tn), lambda j, i: (0, j)))
    args = [a, b]
    if has_res:
        in_specs.append(pl.BlockSpec((tm, tn), lambda j, i: (i, j)))
        args.append(res)
    return pl.pallas_call(
        body, name=name, grid=(n // tn, m // tm), in_specs=in_specs,
        out_specs=pl.BlockSpec((tm, tn), lambda j, i: (i, j)),
        out_shape=jax.ShapeDtypeStruct((m, n), out_dtype),
        compiler_params=_params(("arbitrary", "arbitrary")),
    )(*args)


def mm_tn(a, b, name, col_shards=None, transpose_out=False):
    t, m = a.shape
    parts = b.shape[0] if b.ndim == 3 else 1
    n = parts * b.shape[-1]
    tm = _pick(m, 512, LANES)
    tn = n // col_shards if col_shards else _pick(n, 3328, LANES)
    tk = _pick(t, 1024, 8)
    nkk = t // tk
    per_part = b.shape[-1] // tn

    def body(a_ref, b_ref, o_ref, *scratch):
        acc_ref = scratch[0] if transpose_out else o_ref

        @pl.when(pl.program_id(2) == 0)
        def _():
            acc_ref[...] = jnp.zeros(acc_ref.shape, F32)
        acc = _dg(a_ref[...], b_ref[0] if b.ndim == 3 else b_ref[...], 0, 0)
        if col_shards:
            acc_ref[0] += acc
        else:
            acc_ref[...] += acc
        if transpose_out:
            @pl.when(pl.program_id(2) == nkk - 1)
            def _():
                o_ref[...] = acc_ref[...].T

    if col_shards:
        out_spec = pl.BlockSpec((1, tm, tn), lambda i, j, kk: (j, i, 0))
        out_shape = jax.ShapeDtypeStruct((col_shards, m, tn), F32)
    elif transpose_out:
        out_spec = pl.BlockSpec((tn, tm), lambda i, j, kk: (j, i))
        out_shape = jax.ShapeDtypeStruct((n, m), F32)
    else:
        out_spec = pl.BlockSpec((tm, tn), lambda i, j, kk: (i, j))
        out_shape = jax.ShapeDtypeStruct((m, n), F32)
    if b.ndim == 3:
        b_spec = pl.BlockSpec((1, tk, tn), lambda i, j, kk: (j // per_part, kk, j % per_part))
    else:
        b_spec = pl.BlockSpec((tk, tn), lambda i, j, kk: (kk, j))
    return pl.pallas_call(
        body, name=name, grid=(m // tm, n // tn, nkk),
        in_specs=[pl.BlockSpec((tk, tm), lambda i, j, kk: (kk, i)), b_spec],
        out_specs=out_spec, out_shape=out_shape,
        scratch_shapes=[pltpu.VMEM((tm, tn), F32)] if transpose_out else [],
        compiler_params=_params(("arbitrary", "arbitrary", "arbitrary")),
    )(a, b)


def ffn_up(h2, w_gu):
    t, k = h2.shape
    dff = w_gu.shape[1] // 2
    tn = _pick(dff, 1408, LANES)
    ncol = dff // tn
    tm = _pick(t, 512, 8)

    def body(a_ref, wg_ref, wu_ref, gu_ref, act_ref):
        a = a_ref[...]
        g = _dg(a, wg_ref[...], 1, 0)
        u = _dg(a, wu_ref[...], 1, 0)
        gu_ref[0] = g
        gu_ref[1] = u
        act_ref[...] = _swiglu_fn(g, u).astype(act_ref.dtype)

    return pl.pallas_call(
        body, name="ffn_up", grid=(ncol, t // tm),
        in_specs=[pl.BlockSpec((tm, k), lambda s, i: (i, 0)), pl.BlockSpec((k, tn), lambda s, i: (0, s)),
                  pl.BlockSpec((k, tn), lambda s, i: (0, s + ncol))],
        out_specs=[pl.BlockSpec((2, tm, tn), lambda s, i: (0, i, s)), pl.BlockSpec((tm, tn), lambda s, i: (i, s))],
        out_shape=[jax.ShapeDtypeStruct((2, t, dff), F32), jax.ShapeDtypeStruct((t, dff), BF16)],
        compiler_params=_params(("arbitrary", "arbitrary")),
    )(h2, w_gu, w_gu)


def ffn_down_dx(dx2b, w_down, gu2):
    t, k = dx2b.shape
    dff = w_down.shape[0]
    tn = _pick(dff, 1408, LANES)
    tm = _pick(t, 512, 8)

    def body(a_ref, w_ref, gu_ref, o_ref):
        dact = _dg(a_ref[...], w_ref[...], 1, 1)
        _, vjp = jax.vjp(_swiglu_fn, gu_ref[0], gu_ref[1])
        dg, du = vjp(dact)
        o_ref[0] = dg.astype(o_ref.dtype)
        o_ref[1] = du.astype(o_ref.dtype)

    blk = pl.BlockSpec((2, tm, tn), lambda s, i: (0, i, s))
    return pl.pallas_call(
        body, name="ffn_down_dx", grid=(dff // tn, t // tm),
        in_specs=[pl.BlockSpec((tm, k), lambda s, i: (i, 0)), pl.BlockSpec((tn, k), lambda s, i: (s, 0)), blk],
        out_specs=blk, out_shape=jax.ShapeDtypeStruct((2, t, dff), BF16),
        compiler_params=_params(("arbitrary", "arbitrary")),
    )(dx2b, w_down, gu2)


def mm_res_norm(a, b, res, w, name):
    t, k = a.shape
    d = b.shape[1]
    tm = _pick(t, 512, 8)

    def body(a_ref, b_ref, r_ref, w_ref, x_ref, h_ref):
        xv = _dg(a_ref[...], b_ref[...], 1, 0) + r_ref[...]
        x_ref[...] = xv
        h_ref[...] = _rms(xv, w_ref[...]).astype(h_ref.dtype)

    row = pl.BlockSpec((tm, d), lambda i: (i, 0))
    return pl.pallas_call(
        body, name=name, grid=(t // tm,),
        in_specs=[pl.BlockSpec((tm, k), lambda i: (i, 0)), pl.BlockSpec((k, d), lambda i: (0, 0)), row,
                  pl.BlockSpec((1, d), lambda i: (0, 0))],
        out_specs=[row, row],
        out_shape=[jax.ShapeDtypeStruct((t, d), F32), jax.ShapeDtypeStruct((t, d), BF16)],
        compiler_params=_params(("arbitrary",)),
    )(a, b, res, w)


def mm_res_loss(a, b, res, tgt, w, name):
    t, k = a.shape
    d = b.shape[1]
    tm = _pick(t, 256, 8)

    def body(a_ref, b_ref, r_ref, t_ref, w_ref, dx_ref, dxb_ref, dw_ref, loss_ref):
        @pl.when(pl.program_id(0) == 0)
        def _():
            dw_ref[...] = jnp.zeros(dw_ref.shape, F32)
            loss_ref[...] = jnp.zeros(loss_ref.shape, F32)
        xv = _dg(a_ref[...], b_ref[...], 1, 0) + r_ref[...]
        tg = t_ref[...]
        val, vjp = jax.vjp(lambda x_, w_: _loss_fn(x_, w_, tg), xv, w_ref[...])
        dx, dw = vjp(jnp.ones((1, 1), F32))
        dx_ref[...] = dx
        dxb_ref[...] = dx.astype(dxb_ref.dtype)
        dw_ref[...] += dw
        loss_ref[...] += jnp.broadcast_to(val, (1, LANES))

    row = pl.BlockSpec((tm, d), lambda i: (i, 0))
    vec = pl.BlockSpec((1, d), lambda i: (0, 0))
    return pl.pallas_call(
        body, name=name, grid=(t // tm,),
        in_specs=[pl.BlockSpec((tm, k), lambda i: (i, 0)), pl.BlockSpec((k, d), lambda i: (0, 0)), row, row, vec],
        out_specs=[row, row, vec, pl.BlockSpec((1, LANES), lambda i: (0, 0))],
        out_shape=[jax.ShapeDtypeStruct((t, d), F32), jax.ShapeDtypeStruct((t, d), BF16),
                   jax.ShapeDtypeStruct((1, d), F32), jax.ShapeDtypeStruct((1, LANES), F32)],
        compiler_params=_params(("arbitrary",)),
    )(a, b, res, tgt, w)


def mm_nt_norm_bwd(a, wmat, x, w, dres, name, with_bf16):
    parts = a.shape[0] if a.ndim == 3 else 1
    t, kp = a.shape[-2], a.shape[-1]
    d = wmat.shape[0]
    tm = _pick(t, 256, 8)

    def body(*refs):
        a_ref, w_refs = refs[0], refs[1:1 + parts]
        x_ref, nw_ref, r_ref, dx_ref = refs[1 + parts:5 + parts]
        dw_ref = refs[-1]

        @pl.when(pl.program_id(0) == 0)
        def _():
            dw_ref[...] = jnp.zeros(dw_ref.shape, F32)
        dh = None
        for p in range(parts):
            term = _dg(a_ref[p] if a.ndim == 3 else a_ref[...], w_refs[p][...], 1, 1)
            dh = term if dh is None else dh + term
        _, vjp = jax.vjp(_rms, x_ref[...], nw_ref[...])
        dx, dw = vjp(dh)
        dx = dx + r_ref[...]
        dx_ref[...] = dx
        if with_bf16:
            refs[5 + parts][...] = dx.astype(BF16)
        dw_ref[...] += dw

    row = pl.BlockSpec((tm, d), lambda i: (i, 0))
    vec = pl.BlockSpec((1, d), lambda i: (0, 0))
    a_spec = pl.BlockSpec((parts, tm, kp), lambda i: (0, i, 0)) if a.ndim == 3 else pl.BlockSpec((tm, kp), lambda i: (i, 0))
    w_specs = [pl.BlockSpec((d, kp), lambda i, p=p: (0, p)) for p in range(parts)]
    outs = [row] + ([row] if with_bf16 else []) + [vec]
    shapes = [jax.ShapeDtypeStruct((t, d), F32)] + ([jax.ShapeDtypeStruct((t, d), BF16)] if with_bf16 else [])
    return pl.pallas_call(
        body, name=name, grid=(t // tm,),
        in_specs=[a_spec] + w_specs + [row, vec, row],
        out_specs=outs, out_shape=shapes + [jax.ShapeDtypeStruct((1, d), F32)],
        compiler_params=_params(("arbitrary",)),
    )(a, *([wmat] * parts), x, w, dres)


def _lower_bound(l0, l1):
    m = jnp.maximum(l0, l1)
    e0 = jnp.exp(l0 - m)
    e1 = jnp.exp(l1 - m)
    return e0 / (e0 + e1)


def _gla_consts(rev):
    ri = lax.broadcasted_iota(jnp.int32, (CHUNK, CHUNK), 0)
    ci = lax.broadcasted_iota(jnp.int32, (CHUNK, CHUNK), 1)
    keep = (ci >= ri) if rev else (ci <= ri)
    tm = keep.astype(F32)
    if rev:
        rvec = (ci >= CHUNK // 2).astype(F32)
    else:
        rvec = (ci <= CHUNK // 2 - 1).astype(F32)
    m3 = jnp.concatenate([tm, tm - rvec, 1.0 - tm], axis=0)
    return keep, m3


def _gla_block(uq, uf, ui, l0, l1, st_in, rev):
    ncb = uq.shape[0] // CHUNK
    heads = range(HG_HEADS)
    keep, m3 = _gla_consts(rev)
    lb = _lower_bound(l0, l1)
    q = uq * _sigmoid(uq)
    k = (1.0 - lb) * _sigmoid(-uf)
    g = jnp.log(lb + (1.0 - lb) * _sigmoid(uf))

    def rows(a, c):
        return a[c * CHUNK:(c + 1) * CHUNK]

    def head(a, h):
        return a[:, h * HG_D:(h + 1) * HG_D]

    b3 = [xdot_l(m3, rows(g, c)) for c in range(ncb)]
    q_in, k_in, q_b, k_d, decay = [], [], [], [], []
    for c in range(ncb):
        b, bmr, lmb = b3[c][0:CHUNK], b3[c][CHUNK:2 * CHUNK], b3[c][2 * CHUNK:3 * CHUNK]
        qc, kc = rows(q, c), rows(k, c)
        q_in.append(qc * jnp.exp(bmr))
        k_in.append(kc * jnp.exp(-bmr))
        q_b.append(qc * jnp.exp(b))
        k_d.append(kc * jnp.exp(lmb))
        decay.append(jnp.exp(jnp.sum(rows(g, c), axis=0, keepdims=True)))
    scores = [[jnp.where(keep, dot_nt(head(q_in[c], h), head(k_in[c], h)), 0.0) for h in heads] for c in range(ncb)]
    o_intra = [[dot_nn(scores[c][h], head(rows(ui, c), h)) for h in heads] for c in range(ncb)]
    contrib = [[dot_tn(head(rows(ui, c), h), head(k_d[c], h)) for h in heads] for c in range(ncb)]
    st = list(st_in)
    o_rows = [None] * ncb
    for c in (reversed(range(ncb)) if rev else range(ncb)):
        parts = []
        for h in heads:
            parts.append(o_intra[c][h] + dot_nt(head(q_b[c], h), st[h]))
            st[h] = st[h] * head(decay[c], h) + contrib[c][h]
        o_rows[c] = jnp.concatenate(parts, axis=1)
    return jnp.concatenate(o_rows, axis=0), tuple(st)


def _gla_blocks(t):
    tb = min(512, t)
    return tb, t // tb


def gla_fwd(u, l0, l1, fcol, rev, name):
    t = u.shape[0]
    tb, nb = _gla_blocks(t)

    def blk(i):
        return (nb - 1 - i) if rev else i

    def body(uq_ref, uf_ref, ui_ref, l0_ref, l1_ref, o_ref, ss_ref, st_ref):
        @pl.when(pl.program_id(0) == 0)
        def _():
            st_ref[...] = jnp.zeros(st_ref.shape, F32)
        ss_ref[0] = st_ref[...]
        o, st_out = _gla_block(uq_ref[...], uf_ref[...], ui_ref[...], l0_ref[...], l1_ref[...],
                               tuple(st_ref[h] for h in range(HG_HEADS)), rev)
        o_ref[...] = o
        for h in range(HG_HEADS):
            st_ref[h] = st_out[h]

    row = lambda cb: pl.BlockSpec((tb, HG_W), lambda i: (blk(i), cb))
    vec = pl.BlockSpec((1, HG_W), lambda i: (0, 0))
    return pl.pallas_call(
        body, name=name, grid=(nb,),
        in_specs=[row(0), row(fcol), row(3), vec, vec],
        out_specs=[pl.BlockSpec((tb, HG_W), lambda i: (blk(i), 0)),
                   pl.BlockSpec((1, HG_HEADS, HG_D, HG_D), lambda i: (blk(i), 0, 0, 0))],
        out_shape=[jax.ShapeDtypeStruct((t, HG_W), F32),
                   jax.ShapeDtypeStruct((nb, HG_HEADS, HG_D, HG_D), F32)],
        scratch_shapes=[pltpu.VMEM((HG_HEADS, HG_D, HG_D), F32)],
        compiler_params=_params(("arbitrary",)),
    )(u, u, u, l0, l1)


def gla_bwd(u, l0, l1, ss, do, fcol, rev, name):
    t = u.shape[0]
    tb, nb = _gla_blocks(t)

    def blk(i):
        return i if rev else (nb - 1 - i)

    def body(uq_ref, uf_ref, ui_ref, l0_ref, l1_ref, ss_ref, do_ref,
             dq_ref, df_ref, di_ref, dl0_ref, dl1_ref, dst_ref):
        @pl.when(pl.program_id(0) == 0)
        def _():
            dst_ref[...] = jnp.zeros(dst_ref.shape, F32)
            dl0_ref[...] = jnp.zeros(dl0_ref.shape, F32)
            dl1_ref[...] = jnp.zeros(dl1_ref.shape, F32)
        heads = range(HG_HEADS)
        _, vjp = jax.vjp(functools.partial(_gla_block, rev=rev), uq_ref[...], uf_ref[...], ui_ref[...],
                         l0_ref[...], l1_ref[...], tuple(ss_ref[0, h] for h in heads))
        dq, df, di, dl0, dl1, dst = vjp((do_ref[...], tuple(dst_ref[h] for h in heads)))
        dq_ref[...] = dq
        df_ref[...] = df
        di_ref[...] = di
        dl0_ref[...] += dl0
        dl1_ref[...] += dl1
        for h in heads:
            dst_ref[h] = dst[h]

    row = lambda cb: pl.BlockSpec((tb, HG_W), lambda i: (blk(i), cb))
    vec = pl.BlockSpec((1, HG_W), lambda i: (0, 0))
    orow = pl.BlockSpec((tb, HG_W), lambda i: (blk(i), 0))
    return pl.pallas_call(
        body, name=name, grid=(nb,),
        in_specs=[row(0), row(fcol), row(3), vec, vec,
                  pl.BlockSpec((1, HG_HEADS, HG_D, HG_D), lambda i: (blk(i), 0, 0, 0)), orow],
        out_specs=[orow, orow, orow, vec, vec],
        out_shape=[jax.ShapeDtypeStruct((t, HG_W), F32)] * 3 + [jax.ShapeDtypeStruct((1, HG_W), F32)] * 2,
        scratch_shapes=[pltpu.VMEM((HG_HEADS, HG_D, HG_D), F32)],
        compiler_params=_params(("arbitrary",)),
    )(u, u, u, l0, l1, ss, do)


def _rope_tables(t):
    rows = t // GRID_W
    row = jnp.repeat(jnp.arange(rows), GRID_W).astype(F32)
    col = jnp.tile(jnp.arange(GRID_W), rows).astype(F32)
    axis_dim = ATT_DH // 2
    freqs = ROPE_THETA ** (-jnp.arange(0, axis_dim, 2, dtype=F32) / axis_dim)
    ang = jnp.concatenate([row[:, None] * freqs, col[:, None] * freqs], axis=-1)
    cos2 = jnp.repeat(jnp.cos(ang), 2, axis=-1)
    sin2 = jnp.repeat(jnp.sin(ang), 2, axis=-1) * jnp.tile(jnp.array([-1.0, 1.0], F32), ATT_DH // 2)
    return cos2, sin2


def _group_sum_matrix(width):
    idx = np.arange(width) // ATT_DH
    return jnp.asarray((idx[:, None] == idx[None, :]).astype(np.float32))


def _tile_matrix(width):
    m = np.zeros((LANES, width), np.float32)
    m[np.arange(width) % ATT_DH, np.arange(width)] = 1.0
    return jnp.asarray(m)


def _tile_w(w128, tile_m):
    w8 = jnp.broadcast_to(w128, (8, LANES))
    return jnp.sum(xdot_r(w8, tile_m), axis=0, keepdims=True) * 0.125


def _head_norm_rope(a, w128, cos_t, sin_t, gsum, tile_m, scale):
    ssq = xdot_r(a * a, gsum)
    y = a * lax.rsqrt(ssq * (1.0 / ATT_DH) + EPS) * _tile_w(w128, tile_m)
    return (y * cos_t + swap_pairs(y) * sin_t) * scale


def _att_prep_fn(aq, ak, cq, sq, ck, sk, qw, kw, gq, gk, tq, tk):
    q = _head_norm_rope(aq, qw, cq, sq, gq, tq, ATT_DH ** -0.5)
    k = _head_norm_rope(ak, kw, ck, sk, gk, tk, 1.0)
    return q, k


def _head_t(x, heads):
    xt = x.T
    return [xt[h * ATT_DH:(h + 1) * ATT_DH] for h in range(heads)]


def _head_s(x, heads):
    lane = lax.broadcasted_iota(jnp.int32, (x.shape[0], LANES), 1)
    out = []
    for h in range(heads):
        pair = x[:, (h // 2) * LANES:(h // 2 + 1) * LANES]
        if h % 2:
            pair = pltpu.roll(pair, ATT_DH, 1)
        out.append(jnp.where(lane < ATT_DH, pair, 0.0))
    return out


def _from_head_t(tile):
    return jnp.concatenate([tile[h, 0] for h in range(tile.shape[0])], axis=0).T


def _pad_rows(a):
    return jnp.concatenate([a, jnp.zeros(a.shape, a.dtype)], axis=0)


def _col_bcast(row_vec):
    return jnp.broadcast_to(row_vec, (LANES, row_vec.shape[1])).T


FA_K_FWD = 1024
FA_K_BWD = 512
FA_STRIPS = 4


def _key_block(ref, j, tiles):
    return jnp.concatenate([ref[0, j * tiles + i] for i in range(tiles)], axis=1)


def fa_fwd(qt, ks, vt):
    _, ns, dh, nq = qt.shape
    lk = ks.shape[1]
    bk = min(FA_K_FWD, lk)
    nk = lk // bk
    spg = min(FA_STRIPS, ns)

    def body(q_ref, k_ref, v_ref, o_ref, lse_ref):
        qs = [_pad_rows(q_ref[0, c]) for c in range(spg)]

        def keys(j):
            return k_ref[0, pl.ds(pl.multiple_of(j * bk, bk), bk), :]

        def step(j, carry):
            st0, stats = carry
            kb = keys(j)
            vb = _key_block(v_ref, j, bk // nq)
            sts = [st0] + [_dg(kb, qs[c], 1, 0) for c in range(1, spg)]
            out = []
            for c in range(spg):
                m, l, acc = stats[c]
                m_new = jnp.maximum(m, jnp.max(sts[c], axis=0, keepdims=True))
                alpha = jnp.exp(m - m_new)
                p = jnp.exp(sts[c] - m_new)
                l = alpha * l + jnp.sum(p, axis=0, keepdims=True)
                if c == spg - 1:
                    st0 = _dg(keys(jnp.minimum(j + 1, nk - 1)), qs[0], 1, 0)
                acc = alpha * acc + _dg(vb, p, 1, 0)
                out.append((m_new, l, acc))
            return st0, tuple(out)

        init = tuple((jnp.full((1, nq), -jnp.inf, F32), jnp.zeros((1, nq), F32), jnp.zeros((dh, nq), F32))
                     for _ in range(spg))
        _, res = lax.fori_loop(0, nk, step, (_dg(keys(0), qs[0], 1, 0), init))
        for c in range(spg):
            m, l, acc = res[c]
            o_ref[0, c] = acc / l
            lse_ref[0, c] = _col_bcast(m + jnp.log(l))

    return pl.pallas_call(
        body, name="fa_fwd", grid=(ATT_KV, ns // spg),
        in_specs=[pl.BlockSpec((1, spg, dh, nq), lambda g, i: (g, i, 0, 0)),
                  pl.BlockSpec((1, lk, LANES), lambda g, i: (g, 0, 0)),
                  pl.BlockSpec((1, lk // nq, dh, nq), lambda g, i: (g, 0, 0, 0))],
        out_specs=[pl.BlockSpec((1, spg, dh, nq), lambda g, i: (g, i, 0, 0)),
                   pl.BlockSpec((1, spg, nq, LANES), lambda g, i: (g, i, 0, 0))],
        out_shape=[jax.ShapeDtypeStruct((ATT_KV, ns, dh, nq), F32),
                   jax.ShapeDtypeStruct((ATT_KV, ns, nq, LANES), F32)],
        compiler_params=_params(("arbitrary", "arbitrary")),
    )(qt, ks, vt)


def fa_bwd(qs, qt, dos, dot_, ot, lse, ks, kt, vt):
    _, ns, dh, nq = qt.shape
    lk = ks.shape[1]
    bk = min(FA_K_BWD, lk)
    nk = lk // bk
    tiles = bk // nq
    spg = min(FA_STRIPS, ns)
    nc = bk // LANES

    def body(qs_ref, qt_ref, dos_ref, dot_ref, ot_ref, lse_ref, ks_ref, kt_ref, vt_ref, dq_ref, dk_ref, dv_ref):
        @pl.when(pl.program_id(1) == 0)
        def _():
            dk_ref[...] = jnp.zeros(dk_ref.shape, F32)
            dv_ref[...] = jnp.zeros(dv_ref.shape, F32)

        strips = range(spg)
        lse_b = [lse_ref[0, c] for c in strips]
        d_b = [_col_bcast(jnp.sum(dot_ref[0, c].astype(F32) * ot_ref[0, c], axis=0, keepdims=True)) for c in strips]

        def step(j, dqs):
            ktb, vtb = _pad_rows(_key_block(kt_ref, j, tiles)), _pad_rows(_key_block(vt_ref, j, tiles))
            kb = ks_ref[0, pl.ds(pl.multiple_of(j * bk, bk), bk), :]
            prods = [(_dg(qs_ref[0, c], ktb, 1, 0), _dg(dos_ref[0, c], vtb, 1, 0)) for c in strips]
            out = []
            for c in strips:
                s, dp = prods[c]
                ps, dss = [], []
                for cc in range(nc):
                    sl = slice(cc * LANES, (cc + 1) * LANES)
                    pc = jnp.exp(s[:, sl] - lse_b[c])
                    ps.append(pc.astype(BF16))
                    dss.append((pc * (dp[:, sl] - d_b[c])).astype(BF16))
                p, ds = jnp.concatenate(ps, axis=1), jnp.concatenate(dss, axis=1)
                dv = _dg(dot_ref[0, c], p, 1, 0)
                dk = _dg(qt_ref[0, c], ds, 1, 0)
                for i in range(tiles):
                    dv_ref[0, j * tiles + i] += dv[:, i * nq:(i + 1) * nq]
                    dk_ref[0, j * tiles + i] += dk[:, i * nq:(i + 1) * nq]
                out.append(dqs[c] + _dg(ds, kb, 1, 0))
            return tuple(out)

        dqs = lax.fori_loop(0, nk, step, tuple(jnp.zeros((nq, LANES), F32) for _ in strips))
        for c in strips:
            dq_ref[0, c] = dqs[c].T[:dh]

    sspec = pl.BlockSpec((1, spg, nq, LANES), lambda g, i: (g, i, 0, 0))
    tspec = pl.BlockSpec((1, spg, dh, nq), lambda g, i: (g, i, 0, 0))
    kspec = pl.BlockSpec((1, lk // nq, dh, nq), lambda g, i: (g, 0, 0, 0))
    return pl.pallas_call(
        body, name="fa_bwd", grid=(ATT_KV, ns // spg),
        in_specs=[sspec, tspec, sspec, tspec, tspec, sspec, pl.BlockSpec((1, lk, LANES), lambda g, i: (g, 0, 0)),
                  kspec, kspec],
        out_specs=[tspec, kspec, kspec],
        out_shape=[jax.ShapeDtypeStruct((ATT_KV, ns, dh, nq), F32),
                   jax.ShapeDtypeStruct((ATT_KV, lk // nq, dh, nq), F32),
                   jax.ShapeDtypeStruct((ATT_KV, lk // nq, dh, nq), F32)],
        compiler_params=_params(("arbitrary", "arbitrary")),
    )(qs, qt, dos, dot_, ot, lse, ks, kt, vt)


def _post_mix_fn(of, ob, ug, oa, hgw, attw):
    o = of + ob
    parts = []
    for h in range(HG_HEADS):
        parts.append(_rms(o[:, h * HG_D:(h + 1) * HG_D], hgw))
    hg = jnp.concatenate(parts, axis=1) * (ug * _sigmoid(ug))
    return jnp.concatenate([hg, _rms(oa, attw)], axis=1)


def _swiglu_fn(gate, up):
    return gate * _sigmoid(gate) * up


def _loss_fn(x2, w, tgt):
    e = _rms(x2, w) - tgt
    return 0.5 * jnp.sum(jnp.mean(e * e, axis=-1, keepdims=True), axis=0, keepdims=True)


def _place():
    return lax.axis_index("x"), lax.axis_index("y"), lax.axis_index("c")


def _other_chips(x, y):
    return [(1 - x, y), (x, 1 - y), (1 - x, 1 - y)]


def _hbm_specs(n):
    return [pl.BlockSpec(memory_space=pl.ANY)] * n


def allgather_weights(shards):
    n = len(shards)

    def body(*refs):
        srcs, outs = refs[:n], refs[n:2 * n]
        ssem, rsem, lsem = refs[2 * n:]
        x, y, c = _place()
        k = 2 * x + y
        sib = (x, y, 1 - c)
        chips = _other_chips(x, y)

        def half(a, kk, cc):
            hr = srcs[a].shape[0] // 2
            return outs[a].at[kk, pl.ds(cc * hr, hr), :]

        def my_half(a):
            hr = srcs[a].shape[0] // 2
            return srcs[a].at[pl.ds(c * hr, hr), :]

        def copy(a, j, src_ref, dst_ref, to):
            return pltpu.make_async_remote_copy(src_ref=src_ref, dst_ref=dst_ref, send_sem=ssem.at[6 * a + j],
                                                recv_sem=rsem.at[6 * a + j], device_id=to, device_id_type=MESH)

        mine = [pltpu.make_async_copy(srcs[a], outs[a].at[k], lsem.at[a]) for a in range(n)]
        for cp in mine:
            cp.start()
        first = [copy(a, j, my_half(a), half(a, k, c), (px, py, c)) for a in range(n) for j, (px, py) in enumerate(chips)]
        for cp in first:
            cp.start()
        passed = []
        for a in range(n):
            for j, (px, py) in enumerate(chips):
                copy(a, j, my_half(a), half(a, 2 * px + py, c), (px, py, c)).wait_recv()
                cp = copy(a, 3 + j, half(a, 2 * px + py, c), half(a, 2 * px + py, c), sib)
                cp.start()
                passed.append(cp)
        for a in range(n):
            for j, (px, py) in enumerate(chips):
                copy(a, 3 + j, my_half(a), half(a, 2 * px + py, 1 - c), sib).wait_recv()
        for cp in first + passed:
            cp.wait_send()
        for cp in mine:
            cp.wait()

    return pl.pallas_call(
        body, name="allgather_weights", in_specs=_hbm_specs(n), out_specs=_hbm_specs(n),
        out_shape=[jax.ShapeDtypeStruct((N_CHIPS,) + s.shape, s.dtype) for s in shards],
        scratch_shapes=[pltpu.SemaphoreType.DMA((6 * n,)), pltpu.SemaphoreType.DMA((6 * n,)),
                        pltpu.SemaphoreType.DMA((n,))],
    )(*shards)


def rs_siblings(gs):
    n = len(gs)

    def body(*refs):
        srcs, outs = refs[:n], refs[n:2 * n]
        ssem, rsem = refs[2 * n:]
        x, y, c = _place()
        cps = []
        for a in range(n):
            hr = srcs[a].shape[1] // 2
            cp = pltpu.make_async_remote_copy(src_ref=srcs[a].at[:, pl.ds((1 - c) * hr, hr), :], dst_ref=outs[a],
                                              send_sem=ssem.at[a], recv_sem=rsem.at[a], device_id=(x, y, 1 - c),
                                              device_id_type=MESH)
            cp.start()
            cps.append(cp)
        for cp in cps:
            cp.wait()

    return pl.pallas_call(
        body, name="rs_siblings", in_specs=_hbm_specs(n), out_specs=_hbm_specs(n),
        out_shape=[jax.ShapeDtypeStruct((N_CHIPS, g.shape[1] // 2, g.shape[2]), F32) for g in gs],
        scratch_shapes=[pltpu.SemaphoreType.DMA((n,)), pltpu.SemaphoreType.DMA((n,))],
    )(*gs)


def rs_chips(pas):
    n = len(pas)

    def body(*refs):
        srcs, outs = refs[:n], refs[n:2 * n]
        ssem, rsem, lsem = refs[2 * n:]
        x, y, c = _place()
        k = 2 * x + y
        sib = (x, y, 1 - c)
        chips = _other_chips(x, y)

        def half(a, kk, cc):
            hr = srcs[a].shape[1]
            return outs[a].at[kk, pl.ds(cc * hr, hr), :]

        def copy(a, j, src_ref, dst_ref, to):
            return pltpu.make_async_remote_copy(src_ref=src_ref, dst_ref=dst_ref, send_sem=ssem.at[7 * a + j],
                                                recv_sem=rsem.at[7 * a + j], device_id=to, device_id_type=MESH)

        mine = [pltpu.make_async_copy(srcs[a].at[k], half(a, k, c), lsem.at[a]) for a in range(n)]
        for cp in mine:
            cp.start()
        first = [copy(a, j, srcs[a].at[2 * px + py], half(a, k, c), (px, py, c))
                 for a in range(n) for j, (px, py) in enumerate(chips)]
        first += [copy(a, 6, srcs[a].at[k], half(a, k, c), sib) for a in range(n)]
        for cp in first:
            cp.start()
        passed = []
        for a in range(n):
            for j, (px, py) in enumerate(chips):
                copy(a, j, srcs[a].at[k], half(a, 2 * px + py, c), (px, py, c)).wait_recv()
                cp = copy(a, 3 + j, half(a, 2 * px + py, c), half(a, 2 * px + py, c), sib)
                cp.start()
                passed.append(cp)
        for a in range(n):
            copy(a, 6, srcs[a].at[k], half(a, k, 1 - c), sib).wait_recv()
            for j, (px, py) in enumerate(chips):
                copy(a, 3 + j, srcs[a].at[k], half(a, 2 * px + py, 1 - c), sib).wait_recv()
        for cp in first + passed:
            cp.wait_send()
        for cp in mine:
            cp.wait()

    return pl.pallas_call(
        body, name="rs_chips", in_specs=_hbm_specs(n), out_specs=_hbm_specs(n),
        out_shape=[jax.ShapeDtypeStruct((N_CHIPS, 2 * p.shape[1], p.shape[2]), p.dtype) for p in pas],
        scratch_shapes=[pltpu.SemaphoreType.DMA((7 * n,)), pltpu.SemaphoreType.DMA((7 * n,)),
                        pltpu.SemaphoreType.DMA((n,))],
    )(*pas)


def allreduce_small(p, name):
    rows, width = p.shape

    def body(p_ref, s_ref, gath, ssem, rsem):
        x, y, c = _place()
        me = 4 * x + 2 * y + c
        copies = []
        for d in range(1, N_DEV):
            dx, dy, dc = (d >> 2) & 1, (d >> 1) & 1, d & 1
            peer = (x ^ dx, y ^ dy, c ^ dc)
            cp = pltpu.make_async_remote_copy(src_ref=p_ref, dst_ref=gath.at[me], send_sem=ssem.at[d - 1],
                                              recv_sem=rsem.at[d - 1], device_id=peer, device_id_type=MESH)
            cp.start()
            copies.append(cp)
        gath[me] = p_ref[...]
        for d, cp in enumerate(copies, start=1):
            dx, dy, dc = (d >> 2) & 1, (d >> 1) & 1, d & 1
            peer_slot = 4 * (x ^ dx) + 2 * (y ^ dy) + (c ^ dc)
            pltpu.make_async_remote_copy(src_ref=p_ref, dst_ref=gath.at[peer_slot], send_sem=ssem.at[d - 1],
                                         recv_sem=rsem.at[d - 1], device_id=(x ^ dx, y ^ dy, c ^ dc),
                                         device_id_type=MESH).wait_recv()
        for cp in copies:
            cp.wait_send()
        acc = gath[0]
        for d in range(1, N_DEV):
            acc = acc + gath[d]
        s_ref[...] = acc

    return pl.pallas_call(
        body, name=name,
        in_specs=[pl.BlockSpec(memory_space=pltpu.VMEM)],
        out_specs=pl.BlockSpec(memory_space=pltpu.VMEM),
        out_shape=jax.ShapeDtypeStruct((rows, width), F32),
        scratch_shapes=[pltpu.VMEM((N_DEV, rows, width), F32), pltpu.SemaphoreType.DMA((N_DEV - 1,)),
                        pltpu.SemaphoreType.DMA((N_DEV - 1,))],
    )(p)


def add_my_half(g, recv, name):
    _, r, cols = g.shape
    hr = r // 2
    tb = _pick(hr, 512, 8)
    nb = hr // tb
    c_arr = lax.axis_index("c").astype(jnp.int32).reshape(1)

    def body(c_ref, g_ref, r_ref, o_ref):
        o_ref[...] = (g_ref[...] + r_ref[...]).astype(o_ref.dtype)

    return pl.pallas_call(
        body, name=name,
        grid_spec=pltpu.PrefetchScalarGridSpec(
            num_scalar_prefetch=1, grid=(N_CHIPS, nb),
            in_specs=[pl.BlockSpec((1, tb, cols), lambda k, i, c_ref: (k, c_ref[0] * nb + i, 0)),
                      pl.BlockSpec((1, tb, cols), lambda k, i, c_ref: (k, i, 0))],
            out_specs=pl.BlockSpec((1, tb, cols), lambda k, i, c_ref: (k, i, 0))),
        out_shape=jax.ShapeDtypeStruct((N_CHIPS, hr, cols), BF16),
        compiler_params=_params(("arbitrary", "arbitrary")),
    )(c_arr, g, recv)


def sum_chips(parts, name):
    _, hr, cols = parts.shape
    tb = _pick(hr, 512, 8)

    def body(p_ref, o_ref):
        o_ref[...] = ((p_ref[0].astype(F32) + p_ref[1].astype(F32)) + p_ref[2].astype(F32)) + p_ref[3].astype(F32)

    return pl.pallas_call(
        body, name=name, grid=(hr // tb,),
        in_specs=[pl.BlockSpec((N_CHIPS, tb, cols), lambda i: (0, i, 0))],
        out_specs=pl.BlockSpec((tb, cols), lambda i: (i, 0)),
        out_shape=jax.ShapeDtypeStruct((hr, cols), F32),
        compiler_params=_params(("arbitrary",)),
    )(parts)


def adamw(w, g, m, v, name):
    rows, width = w.shape
    tb = _pick(rows, 512, 8)

    def body(w_ref, g_ref, m_ref, v_ref, d_ref, mo_ref, vo_ref):
        gg = g_ref[...]
        m_new = ADAM_B1 * m_ref[...] + (1.0 - ADAM_B1) * gg
        v_new = ADAM_B2 * v_ref[...] + (1.0 - ADAM_B2) * (gg * gg)
        m_hat = m_new / (1.0 - ADAM_B1 ** ADAM_STEP)
        v_hat = v_new / (1.0 - ADAM_B2 ** ADAM_STEP)
        d_ref[...] = -ADAM_LR * (m_hat / (jnp.sqrt(v_hat) + ADAM_EPS) + ADAM_WD * w_ref[...])
        mo_ref[...] = m_new
        vo_ref[...] = v_new

    spec = pl.BlockSpec((tb, width), lambda i: (i, 0))
    return pl.pallas_call(
        body, name=name, grid=(rows // tb,), in_specs=[spec] * 4, out_specs=[spec] * 3,
        out_shape=[jax.ShapeDtypeStruct((rows, width), F32)] * 3,
        compiler_params=_params(("arbitrary",)),
    )(w, g, m, v)


def _pack_rows(vecs, width=1024):
    rows, cur, used = [], [], 0
    for v in vecs:
        n = v.shape[1]
        if used + n > width:
            cur.append(jnp.zeros((1, width - used), F32))
            rows.append(jnp.concatenate(cur, axis=1))
            cur, used = [], 0
        cur.append(v)
        used += n
    cur.append(jnp.zeros((1, width - used), F32))
    rows.append(jnp.concatenate(cur, axis=1))
    return rows


def _pad128(v):
    return jnp.pad(v, ((0, 0), (0, LANES - v.shape[1])))


def kernel(x, norm1_w, w_in, lb_logits, hg_norm_w, q_norm_w, k_norm_w, att_norm_w, w_out, norm2_w, w_gate_up, w_down, final_norm_w, loss_target, m_norm1_w, m_w_in, m_lb_logits, m_hg_norm_w, m_q_norm_w, m_k_norm_w, m_att_norm_w, m_w_out, m_norm2_w, m_w_gate_up, m_w_down, m_final_norm_w, v_norm1_w, v_w_in, v_lb_logits, v_hg_norm_w, v_q_norm_w, v_k_norm_w, v_att_norm_w, v_w_out, v_norm2_w, v_w_gate_up, v_w_down, v_final_norm_w):
    t, d = x.shape[1], x.shape[2]
    xi, yi, ci = _place()
    chip = 2 * xi + yi
    x2d = x.reshape(t, d)
    tgt = loss_target.reshape(t, d)
    tok = min(TOK, t)
    nt = t // tok
    ns = ATT_GROUP * nt

    g_in, g_out, g_gu, g_down = allgather_weights([w_in[0].astype(BF16), w_out[0].astype(BF16),
                                                   w_gate_up[0].astype(BF16), w_down[0].astype(BF16)])
    wf_in = g_in.transpose(1, 0, 2).reshape(g_in.shape[1], -1)
    wf_gu = g_gu.transpose(1, 0, 2).reshape(g_gu.shape[1], -1)
    wf_out = g_out.reshape(-1, g_out.shape[2])
    wf_down = g_down.reshape(-1, g_down.shape[2])
    lb_rows = lb_logits.reshape(4, LANES) * (ci == 0).astype(F32)
    lb_pad = lax.dynamic_update_slice(jnp.zeros((8, 1024), F32), lb_rows, (0, chip * LANES))
    lb_full = allreduce_small(lb_pad, "gather_lb")[:4, :HG_W]
    l_f0, l_f1, l_b0, l_b1 = (lb_full[i:i + 1] for i in range(4))

    n1 = norm1_w.reshape(1, d)
    n2 = norm2_w.reshape(1, d)
    nf = final_norm_w.reshape(1, d)
    tm = min(512, t)
    (h1,) = _rows(lambda a, w: ((_rms(a, w),), ()), "norm1", t, tm, [_rin(x2d, tm)], [n1], [_rout(t, tm, d, BF16)])
    u = mm_rows(h1, wf_in, "mm_in")
    o_f, ss_f = gla_fwd(u, l_f0, l_f1, 1, False, "gla_fwd_f")
    o_b, ss_b = gla_fwd(u, l_b0, l_b1, 2, True, "gla_fwd_b")

    cos2, sin2 = _rope_tables(t)
    cq, sq = jnp.tile(cos2, (1, ATT_HEADS)), jnp.tile(sin2, (1, ATT_HEADS))
    ck, sk = jnp.tile(cos2, (1, ATT_KV)), jnp.tile(sin2, (1, ATT_KV))
    qw, kw = _pad128(q_norm_w.reshape(1, ATT_DH)), _pad128(k_norm_w.reshape(1, ATT_DH))
    gq, gk = _group_sum_matrix(ATT_QW), _group_sum_matrix(ATT_KVW)
    tq, tk = _tile_matrix(ATT_QW), _tile_matrix(ATT_KVW)
    prep_in = [_rin(u, tok, ATT_QW, 5), _rin(u, tok, ATT_KVW, 24), _rin(cq, tok), _rin(sq, tok), _rin(ck, tok),
               _rin(sk, tok)]
    prep_consts = [qw, kw, gq, gk, tq, tk]

    def att_prep_fn(aq, ak, av, *rest):
        q, k = _att_prep_fn(aq, ak, *rest)
        return (_head_t(q, ATT_HEADS), _head_s(q, ATT_HEADS), _head_s(k, ATT_KV), _head_t(k, ATT_KV),
                _head_t(av, ATT_KV)), ()

    q_t, q_s, k_s, k_t, v_t = _rows(
        att_prep_fn, "att_prep", t, tok, prep_in[:2] + [_rin(u, tok, ATT_KVW, 25)] + prep_in[2:], prep_consts,
        [_tout(ATT_HEADS, nt, ATT_DH, tok, BF16), _tout(ATT_HEADS, nt, tok, LANES, BF16),
         _tout(ATT_KV, nt, tok, LANES, BF16), _tout(ATT_KV, nt, ATT_DH, tok, BF16),
         _tout(ATT_KV, nt, ATT_DH, tok, BF16)])
    q_t = q_t.reshape(ATT_KV, ns, ATT_DH, tok)
    q_s = q_s.reshape(ATT_KV, ns, tok, LANES)
    k_s = k_s.reshape(ATT_KV, t, LANES)
    o_t, lse = fa_fwd(q_t, k_s, v_t)
    o_tiles = o_t.reshape(ATT_HEADS, nt, ATT_DH, tok)

    hgw = hg_norm_w.reshape(1, HG_D)
    attw = att_norm_w.reshape(1, ATT_QW)
    mix_in = [_rin(o_f, tok), _rin(o_b, tok), _rin(u, tok, HG_W, 4), _tin(o_tiles)]
    (mix,) = _rows(lambda of, ob, ug, ot, hw, aw: ((_post_mix_fn(of, ob, ug, _from_head_t(ot), hw, aw),), ()),
                   "post_mix", t, tok, mix_in, [hgw, attw], [_rout(t, tok, d, BF16)])
    x1, h2 = mm_res_norm(mix, wf_out, x2d, n2, "mm_out")
    gu2, act = ffn_up(h2, wf_gu)

    dx2, dx2b, g_final, loss_part = mm_res_loss(act, wf_down, x1, tgt, nf, "mm_down_loss")
    dgu2 = ffn_down_dx(dx2b, wf_down, gu2)
    gw_down = mm_tn(act, dx2b, "mm_down_dw")
    dx1, dx1b, g_norm2 = mm_nt_norm_bwd(dgu2, wf_gu, x1, n2, dx2, "mm_gate_up_dx", True)
    gw_gu = mm_tn(h2, dgu2, "mm_gate_up_dw", col_shards=N_CHIPS)
    dmix = mm_rows(dx1b, wf_out, "mm_out_dx", trans_b=True)
    gw_out = mm_tn(mix, dx1b, "mm_out_dw")

    def post_mix_bwd_fn(of, ob, ug, ot, dm, hgw_, attw_):
        _, vjp = jax.vjp(_post_mix_fn, of, ob, ug, _from_head_t(ot), hgw_, attw_)
        dof, _, dug, doa, dhgw, dattw = vjp(dm)
        return (dof, dug, _head_t(doa, ATT_HEADS), _head_s(doa, ATT_HEADS)), (dhgw, dattw)

    do_hg, du_g, do_t, do_s, g_hg, g_att = _rows(
        post_mix_bwd_fn, "post_mix_bwd", t, tok, mix_in + [_rin(dmix, tok)], [hgw, attw],
        [_rout(t, tok, HG_W, F32), _rout(t, tok, HG_W, BF16), _tout(ATT_HEADS, nt, ATT_DH, tok, BF16),
         _tout(ATT_HEADS, nt, tok, LANES, BF16)], [HG_D, ATT_QW])
    dq_t, dk_t, dv_t = fa_bwd(q_s, q_t, do_s.reshape(q_s.shape), do_t.reshape(q_t.shape), o_t, lse, k_s, k_t, v_t)

    def att_prep_bwd_fn(aq, ak, cq_, sq_, ck_, sk_, dqt, dkt, dvt, qw_, kw_, gq_, gk_, tq_, tk_):
        _, vjp = jax.vjp(lambda a, b, c_, e: _att_prep_fn(a, b, cq_, sq_, ck_, sk_, c_, e, gq_, gk_, tq_, tk_),
                         aq, ak, qw_, kw_)
        daq, dak, dqw, dkw = vjp((_from_head_t(dqt), _from_head_t(dkt)))
        return (daq, dak, _from_head_t(dvt)), (dqw, dkw)

    da_q, da_k, da_v, g_q, g_k = _rows(
        att_prep_bwd_fn, "att_prep_bwd", t, tok,
        prep_in + [_tin(dq_t.reshape(ATT_HEADS, nt, ATT_DH, tok)), _tin(dk_t), _tin(dv_t)], prep_consts,
        [_rout(t, tok, ATT_QW, BF16), _rout(t, tok, ATT_KVW, BF16), _rout(t, tok, ATT_KVW, BF16)], [LANES, LANES])

    dq_f, df_f, di_f, dl_f0, dl_f1 = gla_bwd(u, l_f0, l_f1, ss_f, do_hg, 1, False, "gla_bwd_f")
    dq_b, df_b, di_b, dl_b0, dl_b1 = gla_bwd(u, l_b0, l_b1, ss_b, do_hg, 2, True, "gla_bwd_b")

    def assemble_fn(qf, qb, ff, fb, i_f, i_b, dg, aq, ak, av):
        parts = [qf + qb, ff, fb, i_f + i_b, dg.astype(F32), aq.astype(F32), ak.astype(F32), av.astype(F32)]
        return (jnp.concatenate(parts, axis=1),), ()

    (du,) = _rows(assemble_fn, "assemble_du", t, tok,
                  [_rin(a, tok) for a in (dq_f, dq_b, df_f, df_b, di_f, di_b, du_g, da_q, da_k, da_v)], [],
                  [_rout(t, tok, u.shape[1], BF16)])
    grad_x, g_norm1 = mm_nt_norm_bwd(du, wf_in, x2d, n1, dx1, "mm_in_dx", False)
    gw_in_t = mm_tn(h1, du, "mm_in_dw", transpose_out=True)

    names = ["w_in", "w_out", "w_gate_up", "w_down"]
    gsh = [gw_in_t.reshape(N_CHIPS, -1, d), gw_out.reshape(N_CHIPS, -1, d), gw_gu, gw_down.reshape(N_CHIPS, -1, d)]
    recv = rs_siblings(gsh)
    chip_part = [add_my_half(g, r, "add_my_half_" + n) for g, r, n in zip(gsh, recv, names)]
    parts = rs_chips(chip_part)
    g_shard = [sum_chips(p, "sum_chips_" + n) for p, n in zip(parts, names)]
    g_shard[0] = g_shard[0].T
    big = {}
    for n, g, w, m, v in zip(names, g_shard, (w_in, w_out, w_gate_up, w_down), (m_w_in, m_w_out, m_w_gate_up, m_w_down),
                             (v_w_in, v_w_out, v_w_gate_up, v_w_down)):
        dlt, mn, vn = adamw(w[0], g, m[0], v[0], "adamw_" + n)
        big[n] = (g[None], dlt[None], mn[None], vn[None])

    small = [g_norm1, g_norm2, g_final, g_att, g_hg, g_q, g_k, loss_part, dl_f0, dl_f1, dl_b0, dl_b1]
    packed = _pack_rows(small)
    packed += [jnp.zeros((1, 1024), F32)] * (8 - len(packed))
    tot = allreduce_small(jnp.concatenate(packed, axis=0), "allreduce_small")
    s_norm1, s_norm2, s_final = tot[0:1], tot[1:2], tot[2:3]
    s_att, s_hg, s_q, s_k = tot[3:4, 0:512], tot[3:4, 512:640], tot[3:4, 640:704], tot[3:4, 768:832]
    loss = tot[3, 896]
    s_lb = jnp.concatenate([tot[4:5, 0:512], tot[4:5, 512:1024], tot[5:6, 0:512], tot[5:6, 512:1024]], axis=0)
    s_lb = lax.dynamic_slice(s_lb, (0, chip * LANES), (4, LANES)).reshape(1, 512)

    snames = ["norm1_w", "lb_logits", "hg_norm_w", "q_norm_w", "k_norm_w", "att_norm_w", "norm2_w", "final_norm_w"]
    g_small = dict(zip(snames, [s_norm1, s_lb, s_hg, s_q, s_k, s_att, s_norm2, s_final]))
    w_small = dict(zip(snames, [norm1_w, lb_logits, hg_norm_w, q_norm_w, k_norm_w, att_norm_w, norm2_w, final_norm_w]))
    m_small = dict(zip(snames, [m_norm1_w, m_lb_logits, m_hg_norm_w, m_q_norm_w, m_k_norm_w, m_att_norm_w, m_norm2_w, m_final_norm_w]))
    v_small = dict(zip(snames, [v_norm1_w, v_lb_logits, v_hg_norm_w, v_q_norm_w, v_k_norm_w, v_att_norm_w, v_norm2_w, v_final_norm_w]))

    def pack_small(tree):
        rows = _pack_rows([tree[n].reshape(1, -1) for n in snames])
        rows += [jnp.zeros((1, 1024), F32)] * (8 - len(rows))
        return jnp.concatenate(rows, axis=0)

    d_s, m_s, v_s = adamw(pack_small(w_small), pack_small(g_small), pack_small(m_small), pack_small(v_small), "adamw_small")

    def unpack_small(a):
        out, r, used = {}, 0, 0
        for n in snames:
            size = w_small[n].size
            if used + size > 1024:
                r, used = r + 1, 0
            out[n] = a[r, used:used + size].reshape(w_small[n].shape)
            used += size
        return out

    d_sm, m_sm, v_sm = unpack_small(d_s), unpack_small(m_s), unpack_small(v_s)
    g_sm = {n: g_small[n].reshape(w_small[n].shape) for n in snames}

    order = ["norm1_w", "w_in", "lb_logits", "hg_norm_w", "q_norm_w", "k_norm_w", "att_norm_w", "w_out", "norm2_w",
             "w_gate_up", "w_down", "final_norm_w"]

    def pick(small_tree, idx):
        return [big[n][idx] if n in big else small_tree[n] for n in order]

    return (loss, grad_x.reshape(x.shape), *pick(g_sm, 0), *pick(d_sm, 1), *pick(m_sm, 2), *pick(v_sm, 3))
```

```python
import functools

import numpy as np
import jax
import jax.numpy as jnp
from jax import lax
from jax.experimental import pallas as pl
from jax.experimental.pallas import tpu as pltpu

F32 = jnp.float32
BF16 = jnp.bfloat16
MESH = pl.DeviceIdType.MESH

EPS = 1e-6
GRID_W = 64
HG_HEADS = 4
HG_D = 128
HG_W = HG_HEADS * HG_D
CHUNK = 64
ATT_HEADS = 8
ATT_KV = 2
ATT_GROUP = ATT_HEADS // ATT_KV
ATT_DH = 64
ATT_QW = ATT_HEADS * ATT_DH
ATT_KVW = ATT_KV * ATT_DH
ROPE_THETA = 10000.0
N_CHIPS = 4
N_DEV = 8

ADAM_LR = 0.001
ADAM_B1 = 0.9
ADAM_B2 = 0.999
ADAM_EPS = 1e-08
ADAM_WD = 0.01
ADAM_STEP = 10

VMEM_LIMIT = 52 * 1024 * 1024
LANES = 128
TOK = 256


def _params(sem=None):
    return pltpu.CompilerParams(dimension_semantics=sem, vmem_limit_bytes=VMEM_LIMIT)


def _dg(a, b, ca, cb):
    return lax.dot_general(a.astype(BF16), b.astype(BF16), (((ca,), (cb,)), ((), ())),
                           preferred_element_type=F32)


@jax.custom_vjp
def dot_nn(a, b):
    return _dg(a, b, 1, 0)


def _dot_nn_fwd(a, b):
    return _dg(a, b, 1, 0), (a, b)


def _dot_nn_bwd(res, g):
    a, b = res
    return _dg(g, b, 1, 1), _dg(a, g, 0, 0)


dot_nn.defvjp(_dot_nn_fwd, _dot_nn_bwd)


@jax.custom_vjp
def dot_nt(a, b):
    return _dg(a, b, 1, 1)


def _dot_nt_fwd(a, b):
    return _dg(a, b, 1, 1), (a, b)


def _dot_nt_bwd(res, g):
    a, b = res
    return _dg(g, b, 1, 0), _dg(g, a, 0, 0)


dot_nt.defvjp(_dot_nt_fwd, _dot_nt_bwd)


@jax.custom_vjp
def dot_tn(a, b):
    return _dg(a, b, 0, 0)


def _dot_tn_fwd(a, b):
    return _dg(a, b, 0, 0), (a, b)


def _dot_tn_bwd(res, g):
    a, b = res
    return _dg(b, g, 1, 1), _dg(a, g, 1, 0)


dot_tn.defvjp(_dot_tn_fwd, _dot_tn_bwd)


def _split3(a):
    hi = a.astype(BF16)
    r1 = a - hi.astype(F32)
    mid = r1.astype(BF16)
    lo = (r1 - mid.astype(F32)).astype(BF16)
    return lo, mid, hi


def _sum3(terms):
    lo, mid, hi = terms
    return (lo + mid) + hi


@jax.custom_vjp
def xdot_r(a, m):
    return _sum3([_dg(p, m, 1, 0) for p in _split3(a)])


def _xdot_r_fwd(a, m):
    return xdot_r(a, m), m


def _xdot_r_bwd(m, g):
    return _sum3([_dg(p, m, 1, 1) for p in _split3(g)]), jnp.zeros_like(m)


xdot_r.defvjp(_xdot_r_fwd, _xdot_r_bwd)


@jax.custom_vjp
def xdot_l(m, a):
    return _sum3([_dg(m, p, 1, 0) for p in _split3(a)])


def _xdot_l_fwd(m, a):
    return xdot_l(m, a), m


def _xdot_l_bwd(m, g):
    return jnp.zeros_like(m), _sum3([_dg(m, p, 0, 0) for p in _split3(g)])


xdot_l.defvjp(_xdot_l_fwd, _xdot_l_bwd)


@jax.custom_vjp
def swap_pairs(y):
    n = y.shape[-1]
    lane = lax.broadcasted_iota(jnp.int32, y.shape, 1)
    nxt = pltpu.roll(y, n - 1, 1)
    prv = pltpu.roll(y, 1, 1)
    return jnp.where(lane % 2 == 0, nxt, prv)


def _swap_fwd(y):
    return swap_pairs(y), None


def _swap_bwd(_, g):
    return (swap_pairs(g),)


swap_pairs.defvjp(_swap_fwd, _swap_bwd)


def _rms(x, w):
    return x * lax.rsqrt(jnp.mean(x * x, axis=-1, keepdims=True) + EPS) * w


def _sigmoid(x):
    return jax.nn.sigmoid(x)


def _rows(fn, name, t, tm, ins, consts, outs, accs=()):
    n_r, n_c, n_o, n_a = len(ins), len(consts), len(outs), len(accs)

    def body(*refs):
        r = refs[:n_r]
        c = refs[n_r:n_r + n_c]
        o = refs[n_r + n_c:n_r + n_c + n_o]
        a = refs[n_r + n_c + n_o:]
        ro, ao = fn(*[x[...] for x in r], *[x[...] for x in c])
        for ref, val in zip(o, ro):
            if isinstance(val, (list, tuple)):
                for h, piece in enumerate(val):
                    ref[h, 0] = piece.astype(ref.dtype)
            else:
                ref[...] = val.astype(ref.dtype)
        if n_a:
            @pl.when(pl.program_id(0) == 0)
            def _():
                for ref in a:
                    ref[...] = jnp.zeros(ref.shape, F32)
            for ref, val in zip(a, ao):
                ref[...] += val

    in_specs = [s for _, s in ins]
    in_specs += [pl.BlockSpec(c.shape, lambda i, nd=c.ndim: (0,) * nd) for c in consts]
    out_specs = [s for _, s in outs] + [pl.BlockSpec((1, w), lambda i: (0, 0)) for w in accs]
    out_shape = [s for s, _ in outs] + [jax.ShapeDtypeStruct((1, w), F32) for w in accs]
    return pl.pallas_call(
        body, name=name, grid=(t // tm,), in_specs=in_specs, out_specs=out_specs, out_shape=out_shape,
        compiler_params=_params(("arbitrary",)),
    )(*[a for a, _ in ins], *consts)


def _rin(a, tm, width=None, cb=0):
    width = a.shape[1] if width is None else width
    return a, pl.BlockSpec((tm, width), lambda i, cb=cb: (i, cb))


def _rout(t, tm, width, dtype):
    return jax.ShapeDtypeStruct((t, width), dtype), pl.BlockSpec((tm, width), lambda i: (i, 0))


def _tin(a):
    return a, pl.BlockSpec((a.shape[0], 1) + a.shape[2:], lambda i: (0, i, 0, 0))


def _tout(heads, nt, r, c, dtype):
    return jax.ShapeDtypeStruct((heads, nt, r, c), dtype), pl.BlockSpec((heads, 1, r, c), lambda i: (0, i, 0, 0))


def _pick(n, cap, mult):
    best = None
    for d in range(mult, min(n, cap) + 1, mult):
        if n % d == 0:
            best = d
    return best if best is not None else n


def mm_rows(a, b, name, trans_b=False, res=None, out_dtype=F32):
    m, k = a.shape
    n = b.shape[0] if trans_b else b.shape[1]
    tn = _pick(n, 3328, LANES)
    tm = _pick(m, 512 if k <= 3072 else 256, 8)
    has_res = res is not None

    def body(*refs):
        if has_res:
            a_ref, b_ref, r_ref, o_ref = refs
        else:
            a_ref, b_ref, o_ref = refs
        acc = _dg(a_ref[...], b_ref[...], 1, 1 if trans_b else 0)
        if has_res:
            acc = acc + r_ref[...]
        o_ref[...] = acc.astype(o_ref.dtype)

    in_specs = [pl.BlockSpec((tm, k), lambda j, i: (i, 0))]
    if trans_b:
        in_specs.append(pl.BlockSpec((tn, k), lambda j, i: (j, 0)))
    else:
        in_specs.append(pl.BlockSpec((k, tn), lambda j, i: (0, j)))
    args = [a, b]
    if has_res:
        in_specs.append(pl.BlockSpec((tm, tn), lambda j, i: (i, j)))
        args.append(res)
    return pl.pallas_call(
        body, name=name, grid=(n // tn, m // tm), in_specs=in_specs,
        out_specs=pl.BlockSpec((tm, tn), lambda j, i: (i, j)),
        out_shape=jax.ShapeDtypeStruct((m, n), out_dtype),
        compiler_params=_params(("arbitrary", "arbitrary")),
    )(*args)


def mm_tn(a, b, name, col_shards=None, transpose_out=False):
    t, m = a.shape
    parts = b.shape[0] if b.ndim == 3 else 1
    n = parts * b.shape[-1]
    tm = _pick(m, 512, LANES)
    tn = n // col_shards if col_shards else _pick(n, 3328, LANES)
    tk = _pick(t, 1024, 8)
    nkk = t // tk
    per_part = b.shape[-1] // tn

    def body(a_ref, b_ref, o_ref, *scratch):
        acc_ref = scratch[0] if transpose_out else o_ref

        @pl.when(pl.program_id(2) == 0)
        def _():
            acc_ref[...] = jnp.zeros(acc_ref.shape, F32)
        acc = _dg(a_ref[...], b_ref[0] if b.ndim == 3 else b_ref[...], 0, 0)
        if col_shards:
            acc_ref[0] += acc
        else:
            acc_ref[...] += acc
        if transpose_out:
            @pl.when(pl.program_id(2) == nkk - 1)
            def _():
                o_ref[...] = acc_ref[...].T

    if col_shards:
        out_spec = pl.BlockSpec((1, tm, tn), lambda i, j, kk: (j, i, 0))
        out_shape = jax.ShapeDtypeStruct((col_shards, m, tn), F32)
    elif transpose_out:
        out_spec = pl.BlockSpec((tn, tm), lambda i, j, kk: (j, i))
        out_shape = jax.ShapeDtypeStruct((n, m), F32)
    else:
        out_spec = pl.BlockSpec((tm, tn), lambda i, j, kk: (i, j))
        out_shape = jax.ShapeDtypeStruct((m, n), F32)
    if b.ndim == 3:
        b_spec = pl.BlockSpec((1, tk, tn), lambda i, j, kk: (j // per_part, kk, j % per_part))
    else:
        b_spec = pl.BlockSpec((tk, tn), lambda i, j, kk: (kk, j))
    return pl.pallas_call(
        body, name=name, grid=(m // tm, n // tn, nkk),
        in_specs=[pl.BlockSpec((tk, tm), lambda i, j, kk: (kk, i)), b_spec],
        out_specs=out_spec, out_shape=out_shape,
        scratch_shapes=[pltpu.VMEM((tm, tn), F32)] if transpose_out else [],
        compiler_params=_params(("arbitrary", "arbitrary", "arbitrary")),
    )(a, b)


def ffn_up(h2, w_gu):
    t, k = h2.shape
    dff = w_gu.shape[1] // 2
    tn = _pick(dff, 1408, LANES)
    ncol = dff // tn
    tm = _pick(t, 512, 8)

    def body(a_ref, wg_ref, wu_ref, gu_ref, act_ref):
        a = a_ref[...]
        g = _dg(a, wg_ref[...], 1, 0)
        u = _dg(a, wu_ref[...], 1, 0)
        gu_ref[0] = g
        gu_ref[1] = u
        act_ref[...] = _swiglu_fn(g, u).astype(act_ref.dtype)

    return pl.pallas_call(
        body, name="ffn_up", grid=(ncol, t // tm),
        in_specs=[pl.BlockSpec((tm, k), lambda s, i: (i, 0)), pl.BlockSpec((k, tn), lambda s, i: (0, s)),
                  pl.BlockSpec((k, tn), lambda s, i: (0, s + ncol))],
        out_specs=[pl.BlockSpec((2, tm, tn), lambda s, i: (0, i, s)), pl.BlockSpec((tm, tn), lambda s, i: (i, s))],
        out_shape=[jax.ShapeDtypeStruct((2, t, dff), F32), jax.ShapeDtypeStruct((t, dff), BF16)],
        compiler_params=_params(("arbitrary", "arbitrary")),
    )(h2, w_gu, w_gu)


def ffn_down_dx(dx2b, w_down, gu2):
    t, k = dx2b.shape
    dff = w_down.shape[0]
    tn = _pick(dff, 1408, LANES)
    tm = _pick(t, 512, 8)

    def body(a_ref, w_ref, gu_ref, o_ref):
        dact = _dg(a_ref[...], w_ref[...], 1, 1)
        _, vjp = jax.vjp(_swiglu_fn, gu_ref[0], gu_ref[1])
        dg, du = vjp(dact)
        o_ref[0] = dg.astype(o_ref.dtype)
        o_ref[1] = du.astype(o_ref.dtype)

    blk = pl.BlockSpec((2, tm, tn), lambda s, i: (0, i, s))
    return pl.pallas_call(
        body, name="ffn_down_dx", grid=(dff // tn, t // tm),
        in_specs=[pl.BlockSpec((tm, k), lambda s, i: (i, 0)), pl.BlockSpec((tn, k), lambda s, i: (s, 0)), blk],
        out_specs=blk, out_shape=jax.ShapeDtypeStruct((2, t, dff), BF16),
        compiler_params=_params(("arbitrary", "arbitrary")),
    )(dx2b, w_down, gu2)


def mm_res_norm(a, b, res, w, name):
    t, k = a.shape
    d = b.shape[1]
    tm = _pick(t, 512, 8)

    def body(a_ref, b_ref, r_ref, w_ref, x_ref, h_ref):
        xv = _dg(a_ref[...], b_ref[...], 1, 0) + r_ref[...]
        x_ref[...] = xv
        h_ref[...] = _rms(xv, w_ref[...]).astype(h_ref.dtype)

    row = pl.BlockSpec((tm, d), lambda i: (i, 0))
    return pl.pallas_call(
        body, name=name, grid=(t // tm,),
        in_specs=[pl.BlockSpec((tm, k), lambda i: (i, 0)), pl.BlockSpec((k, d), lambda i: (0, 0)), row,
                  pl.BlockSpec((1, d), lambda i: (0, 0))],
        out_specs=[row, row],
        out_shape=[jax.ShapeDtypeStruct((t, d), F32), jax.ShapeDtypeStruct((t, d), BF16)],
        compiler_params=_params(("arbitrary",)),
    )(a, b, res, w)


def mm_res_loss(a, b, res, tgt, w, name):
    t, k = a.shape
    d = b.shape[1]
    tm = _pick(t, 256, 8)

    def body(a_ref, b_ref, r_ref, t_ref, w_ref, dx_ref, dxb_ref, dw_ref, loss_ref):
        @pl.when(pl.program_id(0) == 0)
        def _():
            dw_ref[...] = jnp.zeros(dw_ref.shape, F32)
            loss_ref[...] = jnp.zeros(loss_ref.shape, F32)
        xv = _dg(a_ref[...], b_ref[...], 1, 0) + r_ref[...]
        tg = t_ref[...]
        val, vjp = jax.vjp(lambda x_, w_: _loss_fn(x_, w_, tg), xv, w_ref[...])
        dx, dw = vjp(jnp.ones((1, 1), F32))
        dx_ref[...] = dx
        dxb_ref[...] = dx.astype(dxb_ref.dtype)
        dw_ref[...] += dw
        loss_ref[...] += jnp.broadcast_to(val, (1, LANES))

    row = pl.BlockSpec((tm, d), lambda i: (i, 0))
    vec = pl.BlockSpec((1, d), lambda i: (0, 0))
    return pl.pallas_call(
        body, name=name, grid=(t // tm,),
        in_specs=[pl.BlockSpec((tm, k), lambda i: (i, 0)), pl.BlockSpec((k, d), lambda i: (0, 0)), row, row, vec],
        out_specs=[row, row, vec, pl.BlockSpec((1, LANES), lambda i: (0, 0))],
        out_shape=[jax.ShapeDtypeStruct((t, d), F32), jax.ShapeDtypeStruct((t, d), BF16),
                   jax.ShapeDtypeStruct((1, d), F32), jax.ShapeDtypeStruct((1, LANES), F32)],
        compiler_params=_params(("arbitrary",)),
    )(a, b, res, tgt, w)


def mm_nt_norm_bwd(a, wmat, x, w, dres, name, with_bf16):
    parts = a.shape[0] if a.ndim == 3 else 1
    t, kp = a.shape[-2], a.shape[-1]
    d = wmat.shape[0]
    tm = _pick(t, 256, 8)

    def body(*refs):
        a_ref, w_refs = refs[0], refs[1:1 + parts]
        x_ref, nw_ref, r_ref, dx_ref = refs[1 + parts:5 + parts]
        dw_ref = refs[-1]

        @pl.when(pl.program_id(0) == 0)
        def _():
            dw_ref[...] = jnp.zeros(dw_ref.shape, F32)
        dh = None
        for p in range(parts):
            term = _dg(a_ref[p] if a.ndim == 3 else a_ref[...], w_refs[p][...], 1, 1)
            dh = term if dh is None else dh + term
        _, vjp = jax.vjp(_rms, x_ref[...], nw_ref[...])
        dx, dw = vjp(dh)
        dx = dx + r_ref[...]
        dx_ref[...] = dx
        if with_bf16:
            refs[5 + parts][...] = dx.astype(BF16)
        dw_ref[...] += dw

    row = pl.BlockSpec((tm, d), lambda i: (i, 0))
    vec = pl.BlockSpec((1, d), lambda i: (0, 0))
    a_spec = pl.BlockSpec((parts, tm, kp), lambda i: (0, i, 0)) if a.ndim == 3 else pl.BlockSpec((tm, kp), lambda i: (i, 0))
    w_specs = [pl.BlockSpec((d, kp), lambda i, p=p: (0, p)) for p in range(parts)]
    outs = [row] + ([row] if with_bf16 else []) + [vec]
    shapes = [jax.ShapeDtypeStruct((t, d), F32)] + ([jax.ShapeDtypeStruct((t, d), BF16)] if with_bf16 else [])
    return pl.pallas_call(
        body, name=name, grid=(t // tm,),
        in_specs=[a_spec] + w_specs + [row, vec, row],
        out_specs=outs, out_shape=shapes + [jax.ShapeDtypeStruct((1, d), F32)],
        compiler_params=_params(("arbitrary",)),
    )(a, *([wmat] * parts), x, w, dres)


def _lower_bound(l0, l1):
    m = jnp.maximum(l0, l1)
    e0 = jnp.exp(l0 - m)
    e1 = jnp.exp(l1 - m)
    return e0 / (e0 + e1)


def _gla_consts(rev):
    ri = lax.broadcasted_iota(jnp.int32, (CHUNK, CHUNK), 0)
    ci = lax.broadcasted_iota(jnp.int32, (CHUNK, CHUNK), 1)
    keep = (ci >= ri) if rev else (ci <= ri)
    tm = keep.astype(F32)
    if rev:
        rvec = (ci >= CHUNK // 2).astype(F32)
    else:
        rvec = (ci <= CHUNK // 2 - 1).astype(F32)
    m3 = jnp.concatenate([tm, tm - rvec, 1.0 - tm], axis=0)
    return keep, m3


def _gla_block(uq, uf, ui, l0, l1, st_in, rev):
    ncb = uq.shape[0] // CHUNK
    heads = range(HG_HEADS)
    keep, m3 = _gla_consts(rev)
    lb = _lower_bound(l0, l1)
    q = uq * _sigmoid(uq)
    k = (1.0 - lb) * _sigmoid(-uf)
    g = jnp.log(lb + (1.0 - lb) * _sigmoid(uf))

    def rows(a, c):
        return a[c * CHUNK:(c + 1) * CHUNK]

    def head(a, h):
        return a[:, h * HG_D:(h + 1) * HG_D]

    b3 = [xdot_l(m3, rows(g, c)) for c in range(ncb)]
    q_in, k_in, q_b, k_d, decay = [], [], [], [], []
    for c in range(ncb):
        b, bmr, lmb = b3[c][0:CHUNK], b3[c][CHUNK:2 * CHUNK], b3[c][2 * CHUNK:3 * CHUNK]
        qc, kc = rows(q, c), rows(k, c)
        q_in.append(qc * jnp.exp(bmr))
        k_in.append(kc * jnp.exp(-bmr))
        q_b.append(qc * jnp.exp(b))
        k_d.append(kc * jnp.exp(lmb))
        decay.append(jnp.exp(jnp.sum(rows(g, c), axis=0, keepdims=True)))
    scores = [[jnp.where(keep, dot_nt(head(q_in[c], h), head(k_in[c], h)), 0.0) for h in heads] for c in range(ncb)]
    o_intra = [[dot_nn(scores[c][h], head(rows(ui, c), h)) for h in heads] for c in range(ncb)]
    contrib = [[dot_tn(head(rows(ui, c), h), head(k_d[c], h)) for h in heads] for c in range(ncb)]
    st = list(st_in)
    o_rows = [None] * ncb
    for c in (reversed(range(ncb)) if rev else range(ncb)):
        parts = []
        for h in heads:
            parts.append(o_intra[c][h] + dot_nt(head(q_b[c], h), st[h]))
            st[h] = st[h] * head(decay[c], h) + contrib[c][h]
        o_rows[c] = jnp.concatenate(parts, axis=1)
    return jnp.concatenate(o_rows, axis=0), tuple(st)


def _gla_blocks(t):
    tb = min(512, t)
    return tb, t // tb


def gla_fwd(u, l0, l1, fcol, rev, name):
    t = u.shape[0]
    tb, nb = _gla_blocks(t)

    def blk(i):
        return (nb - 1 - i) if rev else i

    def body(uq_ref, uf_ref, ui_ref, l0_ref, l1_ref, o_ref, ss_ref, st_ref):
        @pl.when(pl.program_id(0) == 0)
        def _():
            st_ref[...] = jnp.zeros(st_ref.shape, F32)
        ss_ref[0] = st_ref[...]
        o, st_out = _gla_block(uq_ref[...], uf_ref[...], ui_ref[...], l0_ref[...], l1_ref[...],
                               tuple(st_ref[h] for h in range(HG_HEADS)), rev)
        o_ref[...] = o
        for h in range(HG_HEADS):
            st_ref[h] = st_out[h]

    row = lambda cb: pl.BlockSpec((tb, HG_W), lambda i: (blk(i), cb))
    vec = pl.BlockSpec((1, HG_W), lambda i: (0, 0))
    return pl.pallas_call(
        body, name=name, grid=(nb,),
        in_specs=[row(0), row(fcol), row(3), vec, vec],
        out_specs=[pl.BlockSpec((tb, HG_W), lambda i: (blk(i), 0)),
                   pl.BlockSpec((1, HG_HEADS, HG_D, HG_D), lambda i: (blk(i), 0, 0, 0))],
        out_shape=[jax.ShapeDtypeStruct((t, HG_W), F32),
                   jax.ShapeDtypeStruct((nb, HG_HEADS, HG_D, HG_D), F32)],
        scratch_shapes=[pltpu.VMEM((HG_HEADS, HG_D, HG_D), F32)],
        compiler_params=_params(("arbitrary",)),
    )(u, u, u, l0, l1)


def gla_bwd(u, l0, l1, ss, do, fcol, rev, name):
    t = u.shape[0]
    tb, nb = _gla_blocks(t)

    def blk(i):
        return i if rev else (nb - 1 - i)

    def body(uq_ref, uf_ref, ui_ref, l0_ref, l1_ref, ss_ref, do_ref,
             dq_ref, df_ref, di_ref, dl0_ref, dl1_ref, dst_ref):
        @pl.when(pl.program_id(0) == 0)
        def _():
            dst_ref[...] = jnp.zeros(dst_ref.shape, F32)
            dl0_ref[...] = jnp.zeros(dl0_ref.shape, F32)
            dl1_ref[...] = jnp.zeros(dl1_ref.shape, F32)
        heads = range(HG_HEADS)
        _, vjp = jax.vjp(functools.partial(_gla_block, rev=rev), uq_ref[...], uf_ref[...], ui_ref[...],
                         l0_ref[...], l1_ref[...], tuple(ss_ref[0, h] for h in heads))
        dq, df, di, dl0, dl1, dst = vjp((do_ref[...], tuple(dst_ref[h] for h in heads)))
        dq_ref[...] = dq
        df_ref[...] = df
        di_ref[...] = di
        dl0_ref[...] += dl0
        dl1_ref[...] += dl1
        for h in heads:
            dst_ref[h] = dst[h]

    row = lambda cb: pl.BlockSpec((tb, HG_W), lambda i: (blk(i), cb))
    vec = pl.BlockSpec((1, HG_W), lambda i: (0, 0))
    orow = pl.BlockSpec((tb, HG_W), lambda i: (blk(i), 0))
    return pl.pallas_call(
        body, name=name, grid=(nb,),
        in_specs=[row(0), row(fcol), row(3), vec, vec,
                  pl.BlockSpec((1, HG_HEADS, HG_D, HG_D), lambda i: (blk(i), 0, 0, 0)), orow],
        out_specs=[orow, orow, orow, vec, vec],
        out_shape=[jax.ShapeDtypeStruct((t, HG_W), F32)] * 3 + [jax.ShapeDtypeStruct((1, HG_W), F32)] * 2,
        scratch_shapes=[pltpu.VMEM((HG_HEADS, HG_D, HG_D), F32)],
        compiler_params=_params(("arbitrary",)),
    )(u, u, u, l0, l1, ss, do)


def _rope_tables(t):
    rows = t // GRID_W
    row = jnp.repeat(jnp.arange(rows), GRID_W).astype(F32)
    col = jnp.tile(jnp.arange(GRID_W), rows).astype(F32)
    axis_dim = ATT_DH // 2
    freqs = ROPE_THETA ** (-jnp.arange(0, axis_dim, 2, dtype=F32) / axis_dim)
    ang = jnp.concatenate([row[:, None] * freqs, col[:, None] * freqs], axis=-1)
    cos2 = jnp.repeat(jnp.cos(ang), 2, axis=-1)
    sin2 = jnp.repeat(jnp.sin(ang), 2, axis=-1) * jnp.tile(jnp.array([-1.0, 1.0], F32), ATT_DH // 2)
    return cos2, sin2


def _group_sum_matrix(width):
    idx = np.arange(width) // ATT_DH
    return jnp.asarray((idx[:, None] == idx[None, :]).astype(np.float32))


def _tile_matrix(width):
    m = np.zeros((LANES, width), np.float32)
    m[np.arange(width) % ATT_DH, np.arange(width)] = 1.0
    return jnp.asarray(m)


def _tile_w(w128, tile_m):
    w8 = jnp.broadcast_to(w128, (8, LANES))
    return jnp.sum(xdot_r(w8, tile_m), axis=0, keepdims=True) * 0.125


def _head_norm_rope(a, w128, cos_t, sin_t, gsum, tile_m, scale):
    ssq = xdot_r(a * a, gsum)
    y = a * lax.rsqrt(ssq * (1.0 / ATT_DH) + EPS) * _tile_w(w128, tile_m)
    return (y * cos_t + swap_pairs(y) * sin_t) * scale


def _att_prep_fn(aq, ak, cq, sq, ck, sk, qw, kw, gq, gk, tq, tk):
    q = _head_norm_rope(aq, qw, cq, sq, gq, tq, ATT_DH ** -0.5)
    k = _head_norm_rope(ak, kw, ck, sk, gk, tk, 1.0)
    return q, k


def _head_t(x, heads):
    xt = x.T
    return [xt[h * ATT_DH:(h + 1) * ATT_DH] for h in range(heads)]


def _head_s(x, heads):
    lane = lax.broadcasted_iota(jnp.int32, (x.shape[0], LANES), 1)
    out = []
    for h in range(heads):
        pair = x[:, (h // 2) * LANES:(h // 2 + 1) * LANES]
        if h % 2:
            pair = pltpu.roll(pair, ATT_DH, 1)
        out.append(jnp.where(lane < ATT_DH, pair, 0.0))
    return out


def _from_head_t(tile):
    return jnp.concatenate([tile[h, 0] for h in range(tile.shape[0])], axis=0).T


def _pad_rows(a):
    return jnp.concatenate([a, jnp.zeros(a.shape, a.dtype)], axis=0)


def _col_bcast(row_vec):
    return jnp.broadcast_to(row_vec, (LANES, row_vec.shape[1])).T


FA_K_FWD = 1024
FA_K_BWD = 512
FA_STRIPS = 4


def _key_block(ref, j, tiles):
    return jnp.concatenate([ref[0, j * tiles + i] for i in range(tiles)], axis=1)


def _riding(body, n_in, n_out, grid, ride):
    if ride is None:
        return body, [], [], [], [], []
    arrays, reduce = ride
    n = len(arrays)
    steps = grid[0] * grid[1]

    def riding_body(*refs):
        cuts = np.cumsum([0, n_in, n, n_out, n]).tolist()
        own_in, rin, own_out, rout = (refs[a:b] for a, b in zip(cuts[:-1], cuts[1:]))
        sems = refs[cuts[-1]:]
        start, forward, finish = _exchange_phases(rin, rout, *sems, reduce)
        step = pl.program_id(0) * grid[1] + pl.program_id(1)
        pl.when(step == 0)(start)
        pl.when(step == steps // 2)(forward)
        body(*own_in, *own_out)
        pl.when(step == steps - 1)(finish)

    return (riding_body, _hbm_specs(n), _hbm_specs(n), _exchange_shapes(arrays, reduce), _exchange_sems(n), list(arrays))


def fa_fwd(qt, ks, vt, ride=None):
    _, ns, dh, nq = qt.shape
    lk = ks.shape[1]
    bk = min(FA_K_FWD, lk)
    nk = lk // bk
    spg = min(FA_STRIPS, ns)
    grid = (ATT_KV, ns // spg)

    def body(q_ref, k_ref, v_ref, o_ref, lse_ref):
        qs = [_pad_rows(q_ref[0, c]) for c in range(spg)]

        def keys(j):
            return k_ref[0, pl.ds(pl.multiple_of(j * bk, bk), bk), :]

        def step(j, carry):
            st0, stats = carry
            kb = keys(j)
            vb = _key_block(v_ref, j, bk // nq)
            sts = [st0] + [_dg(kb, qs[c], 1, 0) for c in range(1, spg)]
            out = []
            for c in range(spg):
                m, l, acc = stats[c]
                m_new = jnp.maximum(m, jnp.max(sts[c], axis=0, keepdims=True))
                alpha = jnp.exp(m - m_new)
                p = jnp.exp(sts[c] - m_new)
                l = alpha * l + jnp.sum(p, axis=0, keepdims=True)
                if c == spg - 1:
                    st0 = _dg(keys(jnp.minimum(j + 1, nk - 1)), qs[0], 1, 0)
                acc = alpha * acc + _dg(vb, p, 1, 0)
                out.append((m_new, l, acc))
            return st0, tuple(out)

        init = tuple((jnp.full((1, nq), -jnp.inf, F32), jnp.zeros((1, nq), F32), jnp.zeros((dh, nq), F32))
                     for _ in range(spg))
        _, res = lax.fori_loop(0, nk, step, (_dg(keys(0), qs[0], 1, 0), init))
        for c in range(spg):
            m, l, acc = res[c]
            o_ref[0, c] = acc / l
            lse_ref[0, c] = _col_bcast(m + jnp.log(l))

    body, r_in, r_out, r_shapes, r_sems, r_args = _riding(body, 3, 2, grid, ride)
    return pl.pallas_call(
        body, name="fa_fwd", grid=grid,
        in_specs=[pl.BlockSpec((1, spg, dh, nq), lambda g, i: (g, i, 0, 0)),
                  pl.BlockSpec((1, lk, LANES), lambda g, i: (g, 0, 0)),
                  pl.BlockSpec((1, lk // nq, dh, nq), lambda g, i: (g, 0, 0, 0))] + r_in,
        out_specs=[pl.BlockSpec((1, spg, dh, nq), lambda g, i: (g, i, 0, 0)),
                   pl.BlockSpec((1, spg, nq, LANES), lambda g, i: (g, i, 0, 0))] + r_out,
        out_shape=[jax.ShapeDtypeStruct((ATT_KV, ns, dh, nq), F32),
                   jax.ShapeDtypeStruct((ATT_KV, ns, nq, LANES), F32)] + r_shapes,
        scratch_shapes=r_sems,
        compiler_params=_params(("arbitrary", "arbitrary")),
    )(qt, ks, vt, *r_args)


def fa_bwd(qs, qt, dos, dot_, ot, lse, ks, kt, vt, ride=None):
    _, ns, dh, nq = qt.shape
    lk = ks.shape[1]
    bk = min(FA_K_BWD, lk)
    nk = lk // bk
    tiles = bk // nq
    spg = min(FA_STRIPS, ns)
    nc = bk // LANES
    grid = (ATT_KV, ns // spg)

    def body(qs_ref, qt_ref, dos_ref, dot_ref, ot_ref, lse_ref, ks_ref, kt_ref, vt_ref, dq_ref, dk_ref, dv_ref):
        @pl.when(pl.program_id(1) == 0)
        def _():
            dk_ref[...] = jnp.zeros(dk_ref.shape, F32)
            dv_ref[...] = jnp.zeros(dv_ref.shape, F32)

        strips = range(spg)
        lse_b = [lse_ref[0, c] for c in strips]
        d_b = [_col_bcast(jnp.sum(dot_ref[0, c].astype(F32) * ot_ref[0, c], axis=0, keepdims=True)) for c in strips]

        def step(j, dqs):
            ktb, vtb = _pad_rows(_key_block(kt_ref, j, tiles)), _pad_rows(_key_block(vt_ref, j, tiles))
            kb = ks_ref[0, pl.ds(pl.multiple_of(j * bk, bk), bk), :]
            prods = [(_dg(qs_ref[0, c], ktb, 1, 0), _dg(dos_ref[0, c], vtb, 1, 0)) for c in strips]
            out = []
            for c in strips:
                s, dp = prods[c]
                ps, dss = [], []
                for cc in range(nc):
                    sl = slice(cc * LANES, (cc + 1) * LANES)
                    pc = jnp.exp(s[:, sl] - lse_b[c])
                    ps.append(pc.astype(BF16))
                    dss.append((pc * (dp[:, sl] - d_b[c])).astype(BF16))
                p, ds = jnp.concatenate(ps, axis=1), jnp.concatenate(dss, axis=1)
                dv = _dg(dot_ref[0, c], p, 1, 0)
                dk = _dg(qt_ref[0, c], ds, 1, 0)
                for i in range(tiles):
                    dv_ref[0, j * tiles + i] += dv[:, i * nq:(i + 1) * nq]
                    dk_ref[0, j * tiles + i] += dk[:, i * nq:(i + 1) * nq]
                out.append(dqs[c] + _dg(ds, kb, 1, 0))
            return tuple(out)

        dqs = lax.fori_loop(0, nk, step, tuple(jnp.zeros((nq, LANES), F32) for _ in strips))
        for c in strips:
            dq_ref[0, c] = dqs[c].T[:dh]

    sspec = pl.BlockSpec((1, spg, nq, LANES), lambda g, i: (g, i, 0, 0))
    tspec = pl.BlockSpec((1, spg, dh, nq), lambda g, i: (g, i, 0, 0))
    kspec = pl.BlockSpec((1, lk // nq, dh, nq), lambda g, i: (g, 0, 0, 0))
    body, r_in, r_out, r_shapes, r_sems, r_args = _riding(body, 9, 3, grid, ride)
    return pl.pallas_call(
        body, name="fa_bwd", grid=grid,
        in_specs=[sspec, tspec, sspec, tspec, tspec, sspec, pl.BlockSpec((1, lk, LANES), lambda g, i: (g, 0, 0)),
                  kspec, kspec] + r_in,
        out_specs=[tspec, kspec, kspec] + r_out,
        out_shape=[jax.ShapeDtypeStruct((ATT_KV, ns, dh, nq), F32),
                   jax.ShapeDtypeStruct((ATT_KV, lk // nq, dh, nq), F32),
                   jax.ShapeDtypeStruct((ATT_KV, lk // nq, dh, nq), F32)] + r_shapes,
        scratch_shapes=r_sems,
        compiler_params=_params(("arbitrary", "arbitrary")),
    )(qs, qt, dos, dot_, ot, lse, ks, kt, vt, *r_args)


def _post_mix_fn(of, ob, ug, oa, hgw, attw):
    o = of + ob
    parts = []
    for h in range(HG_HEADS):
        parts.append(_rms(o[:, h * HG_D:(h + 1) * HG_D], hgw))
    hg = jnp.concatenate(parts, axis=1) * (ug * _sigmoid(ug))
    return jnp.concatenate([hg, _rms(oa, attw)], axis=1)


def _swiglu_fn(gate, up):
    return gate * _sigmoid(gate) * up


def _loss_fn(x2, w, tgt):
    e = _rms(x2, w) - tgt
    return 0.5 * jnp.sum(jnp.mean(e * e, axis=-1, keepdims=True), axis=0, keepdims=True)


def _place():
    return lax.axis_index("x"), lax.axis_index("y"), lax.axis_index("c")


def _other_chips(x, y):
    return [(1 - x, y), (x, 1 - y), (1 - x, 1 - y)]


def _hbm_specs(n):
    return [pl.BlockSpec(memory_space=pl.ANY)] * n


SEMS_PER_ARRAY = 7


def _exchange_sems(n):
    return [pltpu.SemaphoreType.DMA((SEMS_PER_ARRAY * n,)), pltpu.SemaphoreType.DMA((SEMS_PER_ARRAY * n,)),
            pltpu.SemaphoreType.DMA((n,))]


def _exchange_phases(srcs, outs, ssem, rsem, lsem, reduce):
    n = len(srcs)
    x, y, c = _place()
    k = 2 * x + y
    sib = (x, y, 1 - c)
    chips = _other_chips(x, y)
    pairs = [(a, j) for a in range(n) for j in range(3)]

    def hrows(a):
        return srcs[a].shape[1] if reduce else srcs[a].shape[0] // 2

    def half(a, kk, cc):
        return outs[a].at[kk, pl.ds(cc * hrows(a), hrows(a)), :]

    def mine(a, kk):
        return srcs[a].at[kk] if reduce else srcs[a].at[pl.ds(c * hrows(a), hrows(a)), :]

    def copy(a, j, src_ref, dst_ref, to):
        return pltpu.make_async_remote_copy(src_ref=src_ref, dst_ref=dst_ref, send_sem=ssem.at[SEMS_PER_ARRAY * a + j],
                                            recv_sem=rsem.at[SEMS_PER_ARRAY * a + j], device_id=to, device_id_type=MESH)

    def local(a):
        if reduce:
            return pltpu.make_async_copy(srcs[a].at[k], half(a, k, c), lsem.at[a])
        return pltpu.make_async_copy(srcs[a], outs[a].at[k], lsem.at[a])

    def ici(a, j, arriving):
        px, py = chips[j]
        kk = 2 * px + py
        if arriving:
            return copy(a, j, mine(a, k), half(a, kk, c), (px, py, c))
        return copy(a, j, mine(a, kk), half(a, k, c), (px, py, c))

    def passed(a, j, arriving):
        px, py = chips[j]
        kk = 2 * px + py
        return copy(a, 3 + j, half(a, kk, c), half(a, kk, (1 - c) if arriving else c), sib)

    def own(a, arriving):
        return copy(a, 6, mine(a, k), half(a, k, (1 - c) if arriving else c), sib)

    def start():
        for a in range(n):
            local(a).start()
        for a, j in pairs:
            ici(a, j, False).start()
        if reduce:
            for a in range(n):
                own(a, False).start()

    def forward():
        for a, j in pairs:
            ici(a, j, True).wait_recv()
            passed(a, j, False).start()

    def finish():
        for a in range(n):
            if reduce:
                own(a, True).wait_recv()
            for j in range(3):
                passed(a, j, True).wait_recv()
        for a, j in pairs:
            ici(a, j, False).wait_send()
            passed(a, j, False).wait_send()
        for a in range(n):
            if reduce:
                own(a, False).wait_send()
            local(a).wait()

    return start, forward, finish


def _exchange_shapes(arrays, reduce):
    if reduce:
        return [jax.ShapeDtypeStruct((N_CHIPS, 2 * p.shape[1], p.shape[2]), p.dtype) for p in arrays]
    return [jax.ShapeDtypeStruct((N_CHIPS,) + s.shape, s.dtype) for s in arrays]


def exchange(arrays, reduce, name):
    n = len(arrays)

    def body(*refs):
        start, forward, finish = _exchange_phases(refs[:n], refs[n:2 * n], *refs[2 * n:], reduce)
        start()
        forward()
        finish()

    return pl.pallas_call(
        body, name=name, in_specs=_hbm_specs(n), out_specs=_hbm_specs(n),
        out_shape=_exchange_shapes(arrays, reduce), scratch_shapes=_exchange_sems(n),
    )(*arrays)


def rs_siblings(gs, name):
    n = len(gs)

    def body(*refs):
        srcs, outs = refs[:n], refs[n:2 * n]
        ssem, rsem = refs[2 * n:]
        x, y, c = _place()
        cps = []
        for a in range(n):
            hr = srcs[a].shape[1] // 2
            cp = pltpu.make_async_remote_copy(src_ref=srcs[a].at[:, pl.ds((1 - c) * hr, hr), :], dst_ref=outs[a],
                                              send_sem=ssem.at[a], recv_sem=rsem.at[a], device_id=(x, y, 1 - c),
                                              device_id_type=MESH)
            cp.start()
            cps.append(cp)
        for cp in cps:
            cp.wait()

    return pl.pallas_call(
        body, name=name, in_specs=_hbm_specs(n), out_specs=_hbm_specs(n),
        out_shape=[jax.ShapeDtypeStruct((N_CHIPS, g.shape[1] // 2, g.shape[2]), F32) for g in gs],
        scratch_shapes=[pltpu.SemaphoreType.DMA((n,)), pltpu.SemaphoreType.DMA((n,))],
    )(*gs)


def allreduce_small(p, name):
    rows, width = p.shape

    def body(p_ref, s_ref, gath, ssem, rsem):
        x, y, c = _place()
        me = 4 * x + 2 * y + c
        copies = []
        for d in range(1, N_DEV):
            dx, dy, dc = (d >> 2) & 1, (d >> 1) & 1, d & 1
            peer = (x ^ dx, y ^ dy, c ^ dc)
            cp = pltpu.make_async_remote_copy(src_ref=p_ref, dst_ref=gath.at[me], send_sem=ssem.at[d - 1],
                                              recv_sem=rsem.at[d - 1], device_id=peer, device_id_type=MESH)
            cp.start()
            copies.append(cp)
        gath[me] = p_ref[...]
        for d, cp in enumerate(copies, start=1):
            dx, dy, dc = (d >> 2) & 1, (d >> 1) & 1, d & 1
            peer_slot = 4 * (x ^ dx) + 2 * (y ^ dy) + (c ^ dc)
            pltpu.make_async_remote_copy(src_ref=p_ref, dst_ref=gath.at[peer_slot], send_sem=ssem.at[d - 1],
                                         recv_sem=rsem.at[d - 1], device_id=(x ^ dx, y ^ dy, c ^ dc),
                                         device_id_type=MESH).wait_recv()
        for cp in copies:
            cp.wait_send()
        acc = gath[0]
        for d in range(1, N_DEV):
            acc = acc + gath[d]
        s_ref[...] = acc

    return pl.pallas_call(
        body, name=name,
        in_specs=[pl.BlockSpec(memory_space=pltpu.VMEM)],
        out_specs=pl.BlockSpec(memory_space=pltpu.VMEM),
        out_shape=jax.ShapeDtypeStruct((rows, width), F32),
        scratch_shapes=[pltpu.VMEM((N_DEV, rows, width), F32), pltpu.SemaphoreType.DMA((N_DEV - 1,)),
                        pltpu.SemaphoreType.DMA((N_DEV - 1,))],
    )(p)


def add_my_half(g, recv, name):
    _, r, cols = g.shape
    hr = r // 2
    tb = _pick(hr, 512, 8)
    nb = hr // tb
    c_arr = lax.axis_index("c").astype(jnp.int32).reshape(1)

    def body(c_ref, g_ref, r_ref, o_ref):
        o_ref[...] = (g_ref[...] + r_ref[...]).astype(o_ref.dtype)

    return pl.pallas_call(
        body, name=name,
        grid_spec=pltpu.PrefetchScalarGridSpec(
            num_scalar_prefetch=1, grid=(N_CHIPS, nb),
            in_specs=[pl.BlockSpec((1, tb, cols), lambda k, i, c_ref: (k, c_ref[0] * nb + i, 0)),
                      pl.BlockSpec((1, tb, cols), lambda k, i, c_ref: (k, i, 0))],
            out_specs=pl.BlockSpec((1, tb, cols), lambda k, i, c_ref: (k, i, 0))),
        out_shape=jax.ShapeDtypeStruct((N_CHIPS, hr, cols), BF16),
        compiler_params=_params(("arbitrary", "arbitrary")),
    )(c_arr, g, recv)


def sum_chips(parts, name):
    _, hr, cols = parts.shape
    tb = _pick(hr, 512, 8)

    def body(p_ref, o_ref):
        o_ref[...] = ((p_ref[0].astype(F32) + p_ref[1].astype(F32)) + p_ref[2].astype(F32)) + p_ref[3].astype(F32)

    return pl.pallas_call(
        body, name=name, grid=(hr // tb,),
        in_specs=[pl.BlockSpec((N_CHIPS, tb, cols), lambda i: (0, i, 0))],
        out_specs=pl.BlockSpec((tb, cols), lambda i: (i, 0)),
        out_shape=jax.ShapeDtypeStruct((hr, cols), F32),
        compiler_params=_params(("arbitrary",)),
    )(parts)


def adamw(w, g, m, v, name):
    rows, width = w.shape
    tb = _pick(rows, 512, 8)

    def body(w_ref, g_ref, m_ref, v_ref, d_ref, mo_ref, vo_ref):
        gg = g_ref[...]
        m_new = ADAM_B1 * m_ref[...] + (1.0 - ADAM_B1) * gg
        v_new = ADAM_B2 * v_ref[...] + (1.0 - ADAM_B2) * (gg * gg)
        m_hat = m_new / (1.0 - ADAM_B1 ** ADAM_STEP)
        v_hat = v_new / (1.0 - ADAM_B2 ** ADAM_STEP)
        d_ref[...] = -ADAM_LR * (m_hat / (jnp.sqrt(v_hat) + ADAM_EPS) + ADAM_WD * w_ref[...])
        mo_ref[...] = m_new
        vo_ref[...] = v_new

    spec = pl.BlockSpec((tb, width), lambda i: (i, 0))
    return pl.pallas_call(
        body, name=name, grid=(rows // tb,), in_specs=[spec] * 4, out_specs=[spec] * 3,
        out_shape=[jax.ShapeDtypeStruct((rows, width), F32)] * 3,
        compiler_params=_params(("arbitrary",)),
    )(w, g, m, v)


def _pack_rows(vecs, width=1024):
    rows, cur, used = [], [], 0
    for v in vecs:
        n = v.shape[1]
        if used + n > width:
            cur.append(jnp.zeros((1, width - used), F32))
            rows.append(jnp.concatenate(cur, axis=1))
            cur, used = [], 0
        cur.append(v)
        used += n
    cur.append(jnp.zeros((1, width - used), F32))
    rows.append(jnp.concatenate(cur, axis=1))
    return rows


def _pad128(v):
    return jnp.pad(v, ((0, 0), (0, LANES - v.shape[1])))


def kernel(x, norm1_w, w_in, lb_logits, hg_norm_w, q_norm_w, k_norm_w, att_norm_w, w_out, norm2_w, w_gate_up, w_down, final_norm_w, loss_target, m_norm1_w, m_w_in, m_lb_logits, m_hg_norm_w, m_q_norm_w, m_k_norm_w, m_att_norm_w, m_w_out, m_norm2_w, m_w_gate_up, m_w_down, m_final_norm_w, v_norm1_w, v_w_in, v_lb_logits, v_hg_norm_w, v_q_norm_w, v_k_norm_w, v_att_norm_w, v_w_out, v_norm2_w, v_w_gate_up, v_w_down, v_final_norm_w):
    t, d = x.shape[1], x.shape[2]
    xi, yi, ci = _place()
    chip = 2 * xi + yi
    x2d = x.reshape(t, d)
    tgt = loss_target.reshape(t, d)
    tok = min(TOK, t)
    nt = t // tok
    ns = ATT_GROUP * nt

    (g_in,) = exchange([w_in[0].astype(BF16)], False, "allgather_w_in")
    wf_in = g_in.transpose(1, 0, 2).reshape(g_in.shape[1], -1)
    late_w = [w_out[0].astype(BF16), w_gate_up[0].astype(BF16), w_down[0].astype(BF16)]
    lb_rows = lb_logits.reshape(4, LANES) * (ci == 0).astype(F32)
    lb_pad = lax.dynamic_update_slice(jnp.zeros((8, 1024), F32), lb_rows, (0, chip * LANES))
    lb_full = allreduce_small(lb_pad, "gather_lb")[:4, :HG_W]
    l_f0, l_f1, l_b0, l_b1 = (lb_full[i:i + 1] for i in range(4))

    n1 = norm1_w.reshape(1, d)
    n2 = norm2_w.reshape(1, d)
    nf = final_norm_w.reshape(1, d)
    tm = min(512, t)
    (h1,) = _rows(lambda a, w: ((_rms(a, w),), ()), "norm1", t, tm, [_rin(x2d, tm)], [n1], [_rout(t, tm, d, BF16)])
    u = mm_rows(h1, wf_in, "mm_in")
    o_f, ss_f = gla_fwd(u, l_f0, l_f1, 1, False, "gla_fwd_f")
    o_b, ss_b = gla_fwd(u, l_b0, l_b1, 2, True, "gla_fwd_b")

    cos2, sin2 = _rope_tables(t)
    cq, sq = jnp.tile(cos2, (1, ATT_HEADS)), jnp.tile(sin2, (1, ATT_HEADS))
    ck, sk = jnp.tile(cos2, (1, ATT_KV)), jnp.tile(sin2, (1, ATT_KV))
    qw, kw = _pad128(q_norm_w.reshape(1, ATT_DH)), _pad128(k_norm_w.reshape(1, ATT_DH))
    gq, gk = _group_sum_matrix(ATT_QW), _group_sum_matrix(ATT_KVW)
    tq, tk = _tile_matrix(ATT_QW), _tile_matrix(ATT_KVW)
    prep_in = [_rin(u, tok, ATT_QW, 5), _rin(u, tok, ATT_KVW, 24), _rin(cq, tok), _rin(sq, tok), _rin(ck, tok),
               _rin(sk, tok)]
    prep_consts = [qw, kw, gq, gk, tq, tk]

    def att_prep_fn(aq, ak, av, *rest):
        q, k = _att_prep_fn(aq, ak, *rest)
        return (_head_t(q, ATT_HEADS), _head_s(q, ATT_HEADS), _head_s(k, ATT_KV), _head_t(k, ATT_KV),
                _head_t(av, ATT_KV)), ()

    q_t, q_s, k_s, k_t, v_t = _rows(
        att_prep_fn, "att_prep", t, tok, prep_in[:2] + [_rin(u, tok, ATT_KVW, 25)] + prep_in[2:], prep_consts,
        [_tout(ATT_HEADS, nt, ATT_DH, tok, BF16), _tout(ATT_HEADS, nt, tok, LANES, BF16),
         _tout(ATT_KV, nt, tok, LANES, BF16), _tout(ATT_KV, nt, ATT_DH, tok, BF16),
         _tout(ATT_KV, nt, ATT_DH, tok, BF16)])
    q_t = q_t.reshape(ATT_KV, ns, ATT_DH, tok)
    q_s = q_s.reshape(ATT_KV, ns, tok, LANES)
    k_s = k_s.reshape(ATT_KV, t, LANES)
    o_t, lse, g_out, g_gu, g_down = fa_fwd(q_t, k_s, v_t, ride=(late_w, False))
    wf_gu = g_gu.transpose(1, 0, 2).reshape(g_gu.shape[1], -1)
    wf_out = g_out.reshape(-1, g_out.shape[2])
    wf_down = g_down.reshape(-1, g_down.shape[2])
    o_tiles = o_t.reshape(ATT_HEADS, nt, ATT_DH, tok)

    hgw = hg_norm_w.reshape(1, HG_D)
    attw = att_norm_w.reshape(1, ATT_QW)
    mix_in = [_rin(o_f, tok), _rin(o_b, tok), _rin(u, tok, HG_W, 4), _tin(o_tiles)]
    (mix,) = _rows(lambda of, ob, ug, ot, hw, aw: ((_post_mix_fn(of, ob, ug, _from_head_t(ot), hw, aw),), ()),
                   "post_mix", t, tok, mix_in, [hgw, attw], [_rout(t, tok, d, BF16)])
    x1, h2 = mm_res_norm(mix, wf_out, x2d, n2, "mm_out")
    gu2, act = ffn_up(h2, wf_gu)

    dx2, dx2b, g_final, loss_part = mm_res_loss(act, wf_down, x1, tgt, nf, "mm_down_loss")
    dgu2 = ffn_down_dx(dx2b, wf_down, gu2)
    gw_down = mm_tn(act, dx2b, "mm_down_dw")
    dx1, dx1b, g_norm2 = mm_nt_norm_bwd(dgu2, wf_gu, x1, n2, dx2, "mm_gate_up_dx", True)
    gw_gu = mm_tn(h2, dgu2, "mm_gate_up_dw", col_shards=N_CHIPS)
    dmix = mm_rows(dx1b, wf_out, "mm_out_dx", trans_b=True)
    gw_out = mm_tn(mix, dx1b, "mm_out_dw")

    def post_mix_bwd_fn(of, ob, ug, ot, dm, hgw_, attw_):
        _, vjp = jax.vjp(_post_mix_fn, of, ob, ug, _from_head_t(ot), hgw_, attw_)
        dof, _, dug, doa, dhgw, dattw = vjp(dm)
        return (dof, dug, _head_t(doa, ATT_HEADS), _head_s(doa, ATT_HEADS)), (dhgw, dattw)

    do_hg, du_g, do_t, do_s, g_hg, g_att = _rows(
        post_mix_bwd_fn, "post_mix_bwd", t, tok, mix_in + [_rin(dmix, tok)], [hgw, attw],
        [_rout(t, tok, HG_W, F32), _rout(t, tok, HG_W, BF16), _tout(ATT_HEADS, nt, ATT_DH, tok, BF16),
         _tout(ATT_HEADS, nt, tok, LANES, BF16)], [HG_D, ATT_QW])
    late = ["w_out", "w_gate_up", "w_down"]
    late_g = [gw_out.reshape(N_CHIPS, -1, d), gw_gu, gw_down.reshape(N_CHIPS, -1, d)]
    late_part = [add_my_half(g, r, "add_my_half_" + n) for g, r, n in zip(late_g, rs_siblings(late_g, "rs_siblings_late"), late)]
    dq_t, dk_t, dv_t, *late_parts = fa_bwd(q_s, q_t, do_s.reshape(q_s.shape), do_t.reshape(q_t.shape), o_t, lse, k_s, k_t,
                                           v_t, ride=(late_part, True))

    def att_prep_bwd_fn(aq, ak, cq_, sq_, ck_, sk_, dqt, dkt, dvt, qw_, kw_, gq_, gk_, tq_, tk_):
        _, vjp = jax.vjp(lambda a, b, c_, e: _att_prep_fn(a, b, cq_, sq_, ck_, sk_, c_, e, gq_, gk_, tq_, tk_),
                         aq, ak, qw_, kw_)
        daq, dak, dqw, dkw = vjp((_from_head_t(dqt), _from_head_t(dkt)))
        return (daq, dak, _from_head_t(dvt)), (dqw, dkw)

    da_q, da_k, da_v, g_q, g_k = _rows(
        att_prep_bwd_fn, "att_prep_bwd", t, tok,
        prep_in + [_tin(dq_t.reshape(ATT_HEADS, nt, ATT_DH, tok)), _tin(dk_t), _tin(dv_t)], prep_consts,
        [_rout(t, tok, ATT_QW, BF16), _rout(t, tok, ATT_KVW, BF16), _rout(t, tok, ATT_KVW, BF16)], [LANES, LANES])

    dq_f, df_f, di_f, dl_f0, dl_f1 = gla_bwd(u, l_f0, l_f1, ss_f, do_hg, 1, False, "gla_bwd_f")
    dq_b, df_b, di_b, dl_b0, dl_b1 = gla_bwd(u, l_b0, l_b1, ss_b, do_hg, 2, True, "gla_bwd_b")

    def assemble_fn(qf, qb, ff, fb, i_f, i_b, dg, aq, ak, av):
        parts = [qf + qb, ff, fb, i_f + i_b, dg.astype(F32), aq.astype(F32), ak.astype(F32), av.astype(F32)]
        return (jnp.concatenate(parts, axis=1),), ()

    (du,) = _rows(assemble_fn, "assemble_du", t, tok,
                  [_rin(a, tok) for a in (dq_f, dq_b, df_f, df_b, di_f, di_b, du_g, da_q, da_k, da_v)], [],
                  [_rout(t, tok, u.shape[1], BF16)])
    grad_x, g_norm1 = mm_nt_norm_bwd(du, wf_in, x2d, n1, dx1, "mm_in_dx", False)
    gw_in_t = mm_tn(h1, du, "mm_in_dw", transpose_out=True)

    names = ["w_in"] + late
    g_in4 = gw_in_t.reshape(N_CHIPS, -1, d)
    in_part = add_my_half(g_in4, rs_siblings([g_in4], "rs_siblings_w_in")[0], "add_my_half_w_in")
    parts = list(exchange([in_part], True, "rs_chips_w_in")) + late_parts
    g_shard = [sum_chips(p, "sum_chips_" + n) for p, n in zip(parts, names)]
    g_shard[0] = g_shard[0].T
    big = {}
    for n, g, w, m, v in zip(names, g_shard, (w_in, w_out, w_gate_up, w_down), (m_w_in, m_w_out, m_w_gate_up, m_w_down),
                             (v_w_in, v_w_out, v_w_gate_up, v_w_down)):
        dlt, mn, vn = adamw(w[0], g, m[0], v[0], "adamw_" + n)
        big[n] = (g[None], dlt[None], mn[None], vn[None])

    small = [g_norm1, g_norm2, g_final, g_att, g_hg, g_q, g_k, loss_part, dl_f0, dl_f1, dl_b0, dl_b1]
    packed = _pack_rows(small)
    packed += [jnp.zeros((1, 1024), F32)] * (8 - len(packed))
    tot = allreduce_small(jnp.concatenate(packed, axis=0), "allreduce_small")
    s_norm1, s_norm2, s_final = tot[0:1], tot[1:2], tot[2:3]
    s_att, s_hg, s_q, s_k = tot[3:4, 0:512], tot[3:4, 512:640], tot[3:4, 640:704], tot[3:4, 768:832]
    loss = tot[3, 896]
    s_lb = jnp.concatenate([tot[4:5, 0:512], tot[4:5, 512:1024], tot[5:6, 0:512], tot[5:6, 512:1024]], axis=0)
    s_lb = lax.dynamic_slice(s_lb, (0, chip * LANES), (4, LANES)).reshape(1, 512)

    snames = ["norm1_w", "lb_logits", "hg_norm_w", "q_norm_w", "k_norm_w", "att_norm_w", "norm2_w", "final_norm_w"]
    g_small = dict(zip(snames, [s_norm1, s_lb, s_hg, s_q, s_k, s_att, s_norm2, s_final]))
    w_small = dict(zip(snames, [norm1_w, lb_logits, hg_norm_w, q_norm_w, k_norm_w, att_norm_w, norm2_w, final_norm_w]))
    m_small = dict(zip(snames, [m_norm1_w, m_lb_logits, m_hg_norm_w, m_q_norm_w, m_k_norm_w, m_att_norm_w, m_norm2_w, m_final_norm_w]))
    v_small = dict(zip(snames, [v_norm1_w, v_lb_logits, v_hg_norm_w, v_q_norm_w, v_k_norm_w, v_att_norm_w, v_norm2_w, v_final_norm_w]))

    def pack_small(tree):
        rows = _pack_rows([tree[n].reshape(1, -1) for n in snames])
        rows += [jnp.zeros((1, 1024), F32)] * (8 - len(rows))
        return jnp.concatenate(rows, axis=0)

    d_s, m_s, v_s = adamw(pack_small(w_small), pack_small(g_small), pack_small(m_small), pack_small(v_small), "adamw_small")

    def unpack_small(a):
        out, r, used = {}, 0, 0
        for n in snames:
            size = w_small[n].size
            if used + size > 1024:
                r, used = r + 1, 0
            out[n] = a[r, used:used + size].reshape(w_small[n].shape)
            used += size
        return out

    d_sm, m_sm, v_sm = unpack_small(d_s), unpack_small(m_s), unpack_small(v_s)
    g_sm = {n: g_small[n].reshape(w_small[n].shape) for n in snames}

    order = ["norm1_w", "w_in", "lb_logits", "hg_norm_w", "q_norm_w", "k_norm_w", "att_norm_w", "w_out", "norm2_w",
             "w_gate_up", "w_down", "final_norm_w"]

    def pick(small_tree, idx):
        return [big[n][idx] if n in big else small_tree[n] for n in order]

    return (loss, grad_x.reshape(x.shape), *pick(g_sm, 0), *pick(d_sm, 1), *pick(m_sm, 2), *pick(v_sm, 3))
```

```python
import functools

import numpy as np
import jax
import jax.numpy as jnp
from jax import lax
from jax.experimental import pallas as pl
from jax.experimental.pallas import tpu as pltpu

F32 = jnp.float32
BF16 = jnp.bfloat16
MESH = pl.DeviceIdType.MESH

EPS = 1e-6
GRID_W = 64
HG_HEADS = 4
HG_D = 128
HG_W = HG_HEADS * HG_D
CHUNK = 64
ATT_HEADS = 8
ATT_KV = 2
ATT_GROUP = ATT_HEADS // ATT_KV
ATT_DH = 64
ATT_QW = ATT_HEADS * ATT_DH
ATT_KVW = ATT_KV * ATT_DH
ROPE_THETA = 10000.0
N_CHIPS = 4
N_DEV = 8

ADAM_LR = 0.001
ADAM_B1 = 0.9
ADAM_B2 = 0.999
ADAM_EPS = 1e-08
ADAM_WD = 0.01
ADAM_STEP = 10

VMEM_LIMIT = 52 * 1024 * 1024
LANES = 128
TOK = 256


def _params(sem=None):
    return pltpu.CompilerParams(dimension_semantics=sem, vmem_limit_bytes=VMEM_LIMIT)


def _dg(a, b, ca, cb):
    return lax.dot_general(a.astype(BF16), b.astype(BF16), (((ca,), (cb,)), ((), ())),
                           preferred_element_type=F32)


@jax.custom_vjp
def dot_nn(a, b):
    return _dg(a, b, 1, 0)


def _dot_nn_fwd(a, b):
    return _dg(a, b, 1, 0), (a, b)


def _dot_nn_bwd(res, g):
    a, b = res
    return _dg(g, b, 1, 1), _dg(a, g, 0, 0)


dot_nn.defvjp(_dot_nn_fwd, _dot_nn_bwd)


@jax.custom_vjp
def dot_nt(a, b):
    return _dg(a, b, 1, 1)


def _dot_nt_fwd(a, b):
    return _dg(a, b, 1, 1), (a, b)


def _dot_nt_bwd(res, g):
    a, b = res
    return _dg(g, b, 1, 0), _dg(g, a, 0, 0)


dot_nt.defvjp(_dot_nt_fwd, _dot_nt_bwd)


@jax.custom_vjp
def dot_tn(a, b):
    return _dg(a, b, 0, 0)


def _dot_tn_fwd(a, b):
    return _dg(a, b, 0, 0), (a, b)


def _dot_tn_bwd(res, g):
    a, b = res
    return _dg(b, g, 1, 1), _dg(a, g, 1, 0)


dot_tn.defvjp(_dot_tn_fwd, _dot_tn_bwd)


def _split3(a):
    hi = a.astype(BF16)
    r1 = a - hi.astype(F32)
    mid = r1.astype(BF16)
    lo = (r1 - mid.astype(F32)).astype(BF16)
    return lo, mid, hi


def _sum3(terms):
    lo, mid, hi = terms
    return (lo + mid) + hi


@jax.custom_vjp
def xdot_r(a, m):
    return _sum3([_dg(p, m, 1, 0) for p in _split3(a)])


def _xdot_r_fwd(a, m):
    return xdot_r(a, m), m


def _xdot_r_bwd(m, g):
    return _sum3([_dg(p, m, 1, 1) for p in _split3(g)]), jnp.zeros_like(m)


xdot_r.defvjp(_xdot_r_fwd, _xdot_r_bwd)


@jax.custom_vjp
def xdot_l(m, a):
    return _sum3([_dg(m, p, 1, 0) for p in _split3(a)])


def _xdot_l_fwd(m, a):
    return xdot_l(m, a), m


def _xdot_l_bwd(m, g):
    return jnp.zeros_like(m), _sum3([_dg(m, p, 0, 0) for p in _split3(g)])


xdot_l.defvjp(_xdot_l_fwd, _xdot_l_bwd)


@jax.custom_vjp
def swap_pairs(y):
    n = y.shape[-1]
    lane = lax.broadcasted_iota(jnp.int32, y.shape, 1)
    nxt = pltpu.roll(y, n - 1, 1)
    prv = pltpu.roll(y, 1, 1)
    return jnp.where(lane % 2 == 0, nxt, prv)


def _swap_fwd(y):
    return swap_pairs(y), None


def _swap_bwd(_, g):
    return (swap_pairs(g),)


swap_pairs.defvjp(_swap_fwd, _swap_bwd)


def _rms(x, w):
    return x * lax.rsqrt(jnp.mean(x * x, axis=-1, keepdims=True) + EPS) * w


def _sigmoid(x):
    return jax.nn.sigmoid(x)


def _rows(fn, name, t, tm, ins, consts, outs, accs=()):
    n_r, n_c, n_o, n_a = len(ins), len(consts), len(outs), len(accs)

    def body(*refs):
        r = refs[:n_r]
        c = refs[n_r:n_r + n_c]
        o = refs[n_r + n_c:n_r + n_c + n_o]
        a = refs[n_r + n_c + n_o:]
        ro, ao = fn(*[x[...] for x in r], *[x[...] for x in c])
        for ref, val in zip(o, ro):
            if isinstance(val, (list, tuple)):
                for h, piece in enumerate(val):
                    ref[h, 0] = piece.astype(ref.dtype)
            else:
                ref[...] = val.astype(ref.dtype)
        if n_a:
            @pl.when(pl.program_id(0) == 0)
            def _():
                for ref in a:
                    ref[...] = jnp.zeros(ref.shape, F32)
            for ref, val in zip(a, ao):
                ref[...] += val

    in_specs = [s for _, s in ins]
    in_specs += [pl.BlockSpec(c.shape, lambda i, nd=c.ndim: (0,) * nd) for c in consts]
    out_specs = [s for _, s in outs] + [pl.BlockSpec((1, w), lambda i: (0, 0)) for w in accs]
    out_shape = [s for s, _ in outs] + [jax.ShapeDtypeStruct((1, w), F32) for w in accs]
    return pl.pallas_call(
        body, name=name, grid=(t // tm,), in_specs=in_specs, out_specs=out_specs, out_shape=out_shape,
        compiler_params=_params(("arbitrary",)),
    )(*[a for a, _ in ins], *consts)


def _rin(a, tm, width=None, cb=0):
    width = a.shape[1] if width is None else width
    return a, pl.BlockSpec((tm, width), lambda i, cb=cb: (i, cb))


def _rout(t, tm, width, dtype):
    return jax.ShapeDtypeStruct((t, width), dtype), pl.BlockSpec((tm, width), lambda i: (i, 0))


def _tin(a):
    return a, pl.BlockSpec((a.shape[0], 1) + a.shape[2:], lambda i: (0, i, 0, 0))


def _tout(heads, nt, r, c, dtype):
    return jax.ShapeDtypeStruct((heads, nt, r, c), dtype), pl.BlockSpec((heads, 1, r, c), lambda i: (0, i, 0, 0))


def _pick(n, cap, mult):
    best = None
    for d in range(mult, min(n, cap) + 1, mult):
        if n % d == 0:
            best = d
    return best if best is not None else n


def mm_rows(a, b, name, trans_b=False, res=None, out_dtype=F32):
    m, k = a.shape
    n = b.shape[0] if trans_b else b.shape[1]
    tn = _pick(n, 3328, LANES)
    tm = _pick(m, 512 if k <= 3072 else 256, 8)
    has_res = res is not None

    def body(*refs):
        if has_res:
            a_ref, b_ref, r_ref, o_ref = refs
        else:
            a_ref, b_ref, o_ref = refs
        acc = _dg(a_ref[...], b_ref[...], 1, 1 if trans_b else 0)
        if has_res:
            acc = acc + r_ref[...]
        o_ref[...] = acc.astype(o_ref.dtype)

    in_specs = [pl.BlockSpec((tm, k), lambda j, i: (i, 0))]
    if trans_b:
        in_specs.append(pl.BlockSpec((tn, k), lambda j, i: (j, 0)))
    else:
        in_specs.append(pl.BlockSpec((k, tn), lambda j, i: (0, j)))
    args = [a, b]
    if has_res:
        in_specs.append(pl.BlockSpec((tm, tn), lambda j, i: (i, j)))
        args.append(res)
    return pl.pallas_call(
        body, name=name, grid=(n // tn, m // tm), in_specs=in_specs,
        out_specs=pl.BlockSpec((tm, tn), lambda j, i: (i, j)),
        out_shape=jax.ShapeDtypeStruct((m, n), out_dtype),
        compiler_params=_params(("arbitrary", "arbitrary")),
    )(*args)


def mm_tn(a, b, name, col_shards=None, transpose_out=False):
    t, m = a.shape
    parts = b.shape[0] if b.ndim == 3 else 1
    n = parts * b.shape[-1]
    tm = _pick(m, 512, LANES)
    tn = n // col_shards if col_shards else _pick(n, 3328, LANES)
    tk = _pick(t, 1024, 8)
    nkk = t // tk
    per_part = b.shape[-1] // tn

    def body(a_ref, b_ref, o_ref, *scratch):
        acc_ref = scratch[0] if transpose_out else o_ref

        @pl.when(pl.program_id(2) == 0)
        def _():
            acc_ref[...] = jnp.zeros(acc_ref.shape, F32)
        acc = _dg(a_ref[...], b_ref[0] if b.ndim == 3 else b_ref[...], 0, 0)
        if col_shards:
            acc_ref[0] += acc
        else:
            acc_ref[...] += acc
        if transpose_out:
            @pl.when(pl.program_id(2) == nkk - 1)
            def _():
                o_ref[...] = acc_ref[...].T

    if col_shards:
        out_spec = pl.BlockSpec((1, tm, tn), lambda i, j, kk: (j, i, 0))
        out_shape = jax.ShapeDtypeStruct((col_shards, m, tn), F32)
    elif transpose_out:
        out_spec = pl.BlockSpec((tn, tm), lambda i, j, kk: (j, i))
        out_shape = jax.ShapeDtypeStruct((n, m), F32)
    else:
        out_spec = pl.BlockSpec((tm, tn), lambda i, j, kk: (i, j))
        out_shape = jax.ShapeDtypeStruct((m, n), F32)
    if b.ndim == 3:
        b_spec = pl.BlockSpec((1, tk, tn), lambda i, j, kk: (j // per_part, kk, j % per_part))
    else:
        b_spec = pl.BlockSpec((tk, tn), lambda i, j, kk: (kk, j))
    return pl.pallas_call(
        body, name=name, grid=(m // tm, n // tn, nkk),
        in_specs=[pl.BlockSpec((tk, tm), lambda i, j, kk: (kk, i)), b_spec],
        out_specs=out_spec, out_shape=out_shape,
        scratch_shapes=[pltpu.VMEM((tm, tn), F32)] if transpose_out else [],
        compiler_params=_params(("arbitrary", "arbitrary", "arbitrary")),
    )(a, b)


def ffn_up(h2, w_gu):
    t, k = h2.shape
    dff = w_gu.shape[1] // 2
    tn = _pick(dff, 1408, LANES)
    ncol = dff // tn
    tm = _pick(t, 512, 8)

    def body(a_ref, wg_ref, wu_ref, gu_ref, act_ref):
        a = a_ref[...]
        g = _dg(a, wg_ref[...], 1, 0)
        u = _dg(a, wu_ref[...], 1, 0)
        gu_ref[0] = g
        gu_ref[1] = u
        act_ref[...] = _swiglu_fn(g, u).astype(act_ref.dtype)

    return pl.pallas_call(
        body, name="ffn_up", grid=(ncol, t // tm),
        in_specs=[pl.BlockSpec((tm, k), lambda s, i: (i, 0)), pl.BlockSpec((k, tn), lambda s, i: (0, s)),
                  pl.BlockSpec((k, tn), lambda s, i: (0, s + ncol))],
        out_specs=[pl.BlockSpec((2, tm, tn), lambda s, i: (0, i, s)), pl.BlockSpec((tm, tn), lambda s, i: (i, s))],
        out_shape=[jax.ShapeDtypeStruct((2, t, dff), F32), jax.ShapeDtypeStruct((t, dff), BF16)],
        compiler_params=_params(("arbitrary", "arbitrary")),
    )(h2, w_gu, w_gu)


def ffn_down_dx(dx2b, w_down, gu2):
    t, k = dx2b.shape
    dff = w_down.shape[0]
    tn = _pick(dff, 1408, LANES)
    tm = _pick(t, 512, 8)

    def body(a_ref, w_ref, gu_ref, o_ref):
        dact = _dg(a_ref[...], w_ref[...], 1, 1)
        _, vjp = jax.vjp(_swiglu_fn, gu_ref[0], gu_ref[1])
        dg, du = vjp(dact)
        o_ref[0] = dg.astype(o_ref.dtype)
        o_ref[1] = du.astype(o_ref.dtype)

    blk = pl.BlockSpec((2, tm, tn), lambda s, i: (0, i, s))
    return pl.pallas_call(
        body, name="ffn_down_dx", grid=(dff // tn, t // tm),
        in_specs=[pl.BlockSpec((tm, k), lambda s, i: (i, 0)), pl.BlockSpec((tn, k), lambda s, i: (s, 0)), blk],
        out_specs=blk, out_shape=jax.ShapeDtypeStruct((2, t, dff), BF16),
        compiler_params=_params(("arbitrary", "arbitrary")),
    )(dx2b, w_down, gu2)


def mm_res_norm(a, b, res, w, name):
    t, k = a.shape
    d = b.shape[1]
    tm = _pick(t, 512, 8)

    def body(a_ref, b_ref, r_ref, w_ref, x_ref, h_ref):
        xv = _dg(a_ref[...], b_ref[...], 1, 0) + r_ref[...]
        x_ref[...] = xv
        h_ref[...] = _rms(xv, w_ref[...]).astype(h_ref.dtype)

    row = pl.BlockSpec((tm, d), lambda i: (i, 0))
    return pl.pallas_call(
        body, name=name, grid=(t // tm,),
        in_specs=[pl.BlockSpec((tm, k), lambda i: (i, 0)), pl.BlockSpec((k, d), lambda i: (0, 0)), row,
                  pl.BlockSpec((1, d), lambda i: (0, 0))],
        out_specs=[row, row],
        out_shape=[jax.ShapeDtypeStruct((t, d), F32), jax.ShapeDtypeStruct((t, d), BF16)],
        compiler_params=_params(("arbitrary",)),
    )(a, b, res, w)


def mm_res_loss(a, b, res, tgt, w, name):
    t, k = a.shape
    d = b.shape[1]
    tm = _pick(t, 256, 8)

    def body(a_ref, b_ref, r_ref, t_ref, w_ref, dx_ref, dxb_ref, dw_ref, loss_ref):
        @pl.when(pl.program_id(0) == 0)
        def _():
            dw_ref[...] = jnp.zeros(dw_ref.shape, F32)
            loss_ref[...] = jnp.zeros(loss_ref.shape, F32)
        xv = _dg(a_ref[...], b_ref[...], 1, 0) + r_ref[...]
        tg = t_ref[...]
        val, vjp = jax.vjp(lambda x_, w_: _loss_fn(x_, w_, tg), xv, w_ref[...])
        dx, dw = vjp(jnp.ones((1, 1), F32))
        dx_ref[...] = dx
        dxb_ref[...] = dx.astype(dxb_ref.dtype)
        dw_ref[...] += dw
        loss_ref[...] += jnp.broadcast_to(val, (1, LANES))

    row = pl.BlockSpec((tm, d), lambda i: (i, 0))
    vec = pl.BlockSpec((1, d), lambda i: (0, 0))
    return pl.pallas_call(
        body, name=name, grid=(t // tm,),
        in_specs=[pl.BlockSpec((tm, k), lambda i: (i, 0)), pl.BlockSpec((k, d), lambda i: (0, 0)), row, row, vec],
        out_specs=[row, row, vec, pl.BlockSpec((1, LANES), lambda i: (0, 0))],
        out_shape=[jax.ShapeDtypeStruct((t, d), F32), jax.ShapeDtypeStruct((t, d), BF16),
                   jax.ShapeDtypeStruct((1, d), F32), jax.ShapeDtypeStruct((1, LANES), F32)],
        compiler_params=_params(("arbitrary",)),
    )(a, b, res, tgt, w)


def mm_nt_norm_bwd(a, wmat, x, w, dres, name, with_bf16):
    parts = a.shape[0] if a.ndim == 3 else 1
    t, kp = a.shape[-2], a.shape[-1]
    d = wmat.shape[0]
    tm = _pick(t, 256, 8)

    def body(*refs):
        a_ref, w_refs = refs[0], refs[1:1 + parts]
        x_ref, nw_ref, r_ref, dx_ref = refs[1 + parts:5 + parts]
        dw_ref = refs[-1]

        @pl.when(pl.program_id(0) == 0)
        def _():
            dw_ref[...] = jnp.zeros(dw_ref.shape, F32)
        dh = None
        for p in range(parts):
            term = _dg(a_ref[p] if a.ndim == 3 else a_ref[...], w_refs[p][...], 1, 1)
            dh = term if dh is None else dh + term
        _, vjp = jax.vjp(_rms, x_ref[...], nw_ref[...])
        dx, dw = vjp(dh)
        dx = dx + r_ref[...]
        dx_ref[...] = dx
        if with_bf16:
            refs[5 + parts][...] = dx.astype(BF16)
        dw_ref[...] += dw

    row = pl.BlockSpec((tm, d), lambda i: (i, 0))
    vec = pl.BlockSpec((1, d), lambda i: (0, 0))
    a_spec = pl.BlockSpec((parts, tm, kp), lambda i: (0, i, 0)) if a.ndim == 3 else pl.BlockSpec((tm, kp), lambda i: (i, 0))
    w_specs = [pl.BlockSpec((d, kp), lambda i, p=p: (0, p)) for p in range(parts)]
    outs = [row] + ([row] if with_bf16 else []) + [vec]
    shapes = [jax.ShapeDtypeStruct((t, d), F32)] + ([jax.ShapeDtypeStruct((t, d), BF16)] if with_bf16 else [])
    return pl.pallas_call(
        body, name=name, grid=(t // tm,),
        in_specs=[a_spec] + w_specs + [row, vec, row],
        out_specs=outs, out_shape=shapes + [jax.ShapeDtypeStruct((1, d), F32)],
        compiler_params=_params(("arbitrary",)),
    )(a, *([wmat] * parts), x, w, dres)


def _lower_bound(l0, l1):
    m = jnp.maximum(l0, l1)
    e0 = jnp.exp(l0 - m)
    e1 = jnp.exp(l1 - m)
    return e0 / (e0 + e1)


def _gla_consts(rev):
    ri = lax.broadcasted_iota(jnp.int32, (CHUNK, CHUNK), 0)
    ci = lax.broadcasted_iota(jnp.int32, (CHUNK, CHUNK), 1)
    keep = (ci >= ri) if rev else (ci <= ri)
    ref_mask = lax.broadcasted_iota(jnp.int32, (CHUNK, 1), 0) == (CHUNK // 2 if rev else CHUNK // 2 - 1)
    return keep, ref_mask


def _gla_block(uq, uf, ui, l0, l1, st_in, rev):
    ncb = uq.shape[0] // CHUNK
    heads = range(HG_HEADS)
    keep, ref_mask = _gla_consts(rev)
    cum = keep.astype(F32)
    lb = _lower_bound(l0, l1)
    q = uq * _sigmoid(uq)
    k = (1.0 - lb) * _sigmoid(-uf)
    g = jnp.log(lb + (1.0 - lb) * _sigmoid(uf))

    def rows(a, c):
        return a[c * CHUNK:(c + 1) * CHUNK]

    def head(a, h):
        return a[:, h * HG_D:(h + 1) * HG_D]

    bs = [xdot_l(cum, rows(g, c)) for c in range(ncb)]
    q_in, k_in, q_b, k_d, decay = [], [], [], [], []
    for c in range(ncb):
        b = bs[c]
        b_ref = jnp.sum(jnp.where(ref_mask, b, 0.0), axis=0, keepdims=True)
        b_last = jnp.sum(rows(g, c), axis=0, keepdims=True)
        qc, kc = rows(q, c), rows(k, c)
        q_in.append(qc * jnp.exp(b - b_ref))
        k_in.append(kc * jnp.exp(b_ref - b))
        q_b.append(qc * jnp.exp(b))
        k_d.append(kc * jnp.exp(b_last - b))
        decay.append(jnp.exp(b_last))
    scores = [[jnp.where(keep, dot_nt(head(q_in[c], h), head(k_in[c], h)), 0.0) for h in heads] for c in range(ncb)]
    o_intra = [[dot_nn(scores[c][h], head(rows(ui, c), h)) for h in heads] for c in range(ncb)]
    contrib = [[dot_tn(head(rows(ui, c), h), head(k_d[c], h)) for h in heads] for c in range(ncb)]
    st = list(st_in)
    o_rows = [None] * ncb
    for c in (reversed(range(ncb)) if rev else range(ncb)):
        parts = []
        for h in heads:
            parts.append(o_intra[c][h] + dot_nt(head(q_b[c], h), st[h]))
            st[h] = st[h] * head(decay[c], h) + contrib[c][h]
        o_rows[c] = jnp.concatenate(parts, axis=1)
    return jnp.concatenate(o_rows, axis=0), tuple(st)


def _gla_blocks(t):
    tb = min(512, t)
    return tb, t // tb


def gla_fwd(u, l0, l1, fcol, rev, name):
    t = u.shape[0]
    tb, nb = _gla_blocks(t)

    def blk(i):
        return (nb - 1 - i) if rev else i

    def body(uq_ref, uf_ref, ui_ref, l0_ref, l1_ref, o_ref, ss_ref, st_ref):
        @pl.when(pl.program_id(0) == 0)
        def _():
            st_ref[...] = jnp.zeros(st_ref.shape, F32)
        ss_ref[0] = st_ref[...]
        o, st_out = _gla_block(uq_ref[...], uf_ref[...], ui_ref[...], l0_ref[...], l1_ref[...],
                               tuple(st_ref[h] for h in range(HG_HEADS)), rev)
        o_ref[...] = o
        for h in range(HG_HEADS):
            st_ref[h] = st_out[h]

    row = lambda cb: pl.BlockSpec((tb, HG_W), lambda i: (blk(i), cb))
    vec = pl.BlockSpec((1, HG_W), lambda i: (0, 0))
    return pl.pallas_call(
        body, name=name, grid=(nb,),
        in_specs=[row(0), row(fcol), row(3), vec, vec],
        out_specs=[pl.BlockSpec((tb, HG_W), lambda i: (blk(i), 0)),
                   pl.BlockSpec((1, HG_HEADS, HG_D, HG_D), lambda i: (blk(i), 0, 0, 0))],
        out_shape=[jax.ShapeDtypeStruct((t, HG_W), F32),
                   jax.ShapeDtypeStruct((nb, HG_HEADS, HG_D, HG_D), F32)],
        scratch_shapes=[pltpu.VMEM((HG_HEADS, HG_D, HG_D), F32)],
        compiler_params=_params(("arbitrary",)),
    )(u, u, u, l0, l1)


def gla_bwd(u, l0, l1, ss, do, fcol, rev, name):
    t = u.shape[0]
    tb, nb = _gla_blocks(t)

    def blk(i):
        return i if rev else (nb - 1 - i)

    def body(uq_ref, uf_ref, ui_ref, l0_ref, l1_ref, ss_ref, do_ref,
             dq_ref, df_ref, di_ref, dl0_ref, dl1_ref, dst_ref):
        @pl.when(pl.program_id(0) == 0)
        def _():
            dst_ref[...] = jnp.zeros(dst_ref.shape, F32)
            dl0_ref[...] = jnp.zeros(dl0_ref.shape, F32)
            dl1_ref[...] = jnp.zeros(dl1_ref.shape, F32)
        heads = range(HG_HEADS)
        _, vjp = jax.vjp(functools.partial(_gla_block, rev=rev), uq_ref[...], uf_ref[...], ui_ref[...],
                         l0_ref[...], l1_ref[...], tuple(ss_ref[0, h] for h in heads))
        dq, df, di, dl0, dl1, dst = vjp((do_ref[...], tuple(dst_ref[h] for h in heads)))
        dq_ref[...] = dq
        df_ref[...] = df
        di_ref[...] = di
        dl0_ref[...] += dl0
        dl1_ref[...] += dl1
        for h in heads:
            dst_ref[h] = dst[h]

    row = lambda cb: pl.BlockSpec((tb, HG_W), lambda i: (blk(i), cb))
    vec = pl.BlockSpec((1, HG_W), lambda i: (0, 0))
    orow = pl.BlockSpec((tb, HG_W), lambda i: (blk(i), 0))
    return pl.pallas_call(
        body, name=name, grid=(nb,),
        in_specs=[row(0), row(fcol), row(3), vec, vec,
                  pl.BlockSpec((1, HG_HEADS, HG_D, HG_D), lambda i: (blk(i), 0, 0, 0)), orow],
        out_specs=[orow, orow, orow, vec, vec],
        out_shape=[jax.ShapeDtypeStruct((t, HG_W), F32)] * 3 + [jax.ShapeDtypeStruct((1, HG_W), F32)] * 2,
        scratch_shapes=[pltpu.VMEM((HG_HEADS, HG_D, HG_D), F32)],
        compiler_params=_params(("arbitrary",)),
    )(u, u, u, l0, l1, ss, do)


def _rope_tables(t):
    rows = t // GRID_W
    row = jnp.repeat(jnp.arange(rows), GRID_W).astype(F32)
    col = jnp.tile(jnp.arange(GRID_W), rows).astype(F32)
    axis_dim = ATT_DH // 2
    freqs = ROPE_THETA ** (-jnp.arange(0, axis_dim, 2, dtype=F32) / axis_dim)
    ang = jnp.concatenate([row[:, None] * freqs, col[:, None] * freqs], axis=-1)
    cos2 = jnp.repeat(jnp.cos(ang), 2, axis=-1)
    sin2 = jnp.repeat(jnp.sin(ang), 2, axis=-1) * jnp.tile(jnp.array([-1.0, 1.0], F32), ATT_DH // 2)
    return cos2, sin2


def _group_sum_matrix(width):
    idx = np.arange(width) // ATT_DH
    return jnp.asarray((idx[:, None] == idx[None, :]).astype(np.float32))


def _tile_matrix(width):
    m = np.zeros((LANES, width), np.float32)
    m[np.arange(width) % ATT_DH, np.arange(width)] = 1.0
    return jnp.asarray(m)


def _tile_w(w128, tile_m):
    w8 = jnp.broadcast_to(w128, (8, LANES))
    return jnp.sum(xdot_r(w8, tile_m), axis=0, keepdims=True) * 0.125


def _head_norm_rope(a, w128, cos_t, sin_t, gsum, tile_m, scale):
    ssq = xdot_r(a * a, gsum)
    y = a * lax.rsqrt(ssq * (1.0 / ATT_DH) + EPS) * _tile_w(w128, tile_m)
    return (y * cos_t + swap_pairs(y) * sin_t) * scale


def _att_prep_fn(aq, ak, cq, sq, ck, sk, qw, kw, gq, gk, tq, tk):
    q = _head_norm_rope(aq, qw, cq, sq, gq, tq, ATT_DH ** -0.5)
    k = _head_norm_rope(ak, kw, ck, sk, gk, tk, 1.0)
    return q, k


def _head_t(x, heads):
    xt = x.T
    return [xt[h * ATT_DH:(h + 1) * ATT_DH] for h in range(heads)]


def _head_s(x, heads):
    lane = lax.broadcasted_iota(jnp.int32, (x.shape[0], LANES), 1)
    out = []
    for h in range(heads):
        pair = x[:, (h // 2) * LANES:(h // 2 + 1) * LANES]
        if h % 2:
            pair = pltpu.roll(pair, ATT_DH, 1)
        out.append(jnp.where(lane < ATT_DH, pair, 0.0))
    return out


def _from_head_t(tile):
    return jnp.concatenate([tile[h, 0] for h in range(tile.shape[0])], axis=0).T


def _pad_rows(a):
    return jnp.concatenate([a, jnp.zeros(a.shape, a.dtype)], axis=0)


def _col_bcast(row_vec):
    return jnp.broadcast_to(row_vec, (LANES, row_vec.shape[1])).T


FA_K_FWD = 1024
FA_K_BWD = 512
FA_STRIPS_FWD = 4
FA_STRIPS_BWD = 8


def _key_block(ref, j, tiles):
    return jnp.concatenate([ref[0, j * tiles + i] for i in range(tiles)], axis=1)


def _riding(body, n_in, n_out, grid, ride):
    if ride is None:
        return body, [], [], [], [], []
    arrays, reduce = ride
    n = len(arrays)
    steps = grid[0] * grid[1]

    def riding_body(*refs):
        cuts = np.cumsum([0, n_in, n, n_out, n]).tolist()
        own_in, rin, own_out, rout = (refs[a:b] for a, b in zip(cuts[:-1], cuts[1:]))
        sems = refs[cuts[-1]:]
        start, forward, finish = _exchange_phases(rin, rout, *sems, reduce)
        step = pl.program_id(0) * grid[1] + pl.program_id(1)
        pl.when(step == 0)(start)
        pl.when(step == steps // 2)(forward)
        body(*own_in, *own_out)
        pl.when(step == steps - 1)(finish)

    return (riding_body, _hbm_specs(n), _hbm_specs(n), _exchange_shapes(arrays, reduce), _exchange_sems(n), list(arrays))


def fa_fwd(qt, ks, vt, ride=None):
    _, ns, dh, nq = qt.shape
    lk = ks.shape[1]
    bk = min(FA_K_FWD, lk)
    nk = lk // bk
    spg = min(FA_STRIPS_FWD, ns)
    grid = (ATT_KV, ns // spg)

    def body(q_ref, k_ref, v_ref, o_ref, lse_ref):
        qs = [_pad_rows(q_ref[0, c]) for c in range(spg)]

        def keys(j):
            return k_ref[0, pl.ds(pl.multiple_of(j * bk, bk), bk), :]

        def step(j, carry):
            st0, stats = carry
            kb = keys(j)
            vb = _key_block(v_ref, j, bk // nq)
            sts = [st0] + [_dg(kb, qs[c], 1, 0) for c in range(1, spg)]
            out = []
            for c in range(spg):
                m, l, acc = stats[c]
                m_new = jnp.maximum(m, jnp.max(sts[c], axis=0, keepdims=True))
                alpha = jnp.exp(m - m_new)
                p = jnp.exp(sts[c] - m_new)
                l = alpha * l + jnp.sum(p, axis=0, keepdims=True)
                if c == spg - 1:
                    st0 = _dg(keys(jnp.minimum(j + 1, nk - 1)), qs[0], 1, 0)
                acc = alpha * acc + _dg(vb, p, 1, 0)
                out.append((m_new, l, acc))
            return st0, tuple(out)

        init = tuple((jnp.full((1, nq), -jnp.inf, F32), jnp.zeros((1, nq), F32), jnp.zeros((dh, nq), F32))
                     for _ in range(spg))
        _, res = lax.fori_loop(0, nk, step, (_dg(keys(0), qs[0], 1, 0), init))
        for c in range(spg):
            m, l, acc = res[c]
            o_ref[0, c] = acc / l
            lse_ref[0, c] = _col_bcast(m + jnp.log(l))

    body, r_in, r_out, r_shapes, r_sems, r_args = _riding(body, 3, 2, grid, ride)
    return pl.pallas_call(
        body, name="fa_fwd", grid=grid,
        in_specs=[pl.BlockSpec((1, spg, dh, nq), lambda g, i: (g, i, 0, 0)),
                  pl.BlockSpec((1, lk, LANES), lambda g, i: (g, 0, 0)),
                  pl.BlockSpec((1, lk // nq, dh, nq), lambda g, i: (g, 0, 0, 0))] + r_in,
        out_specs=[pl.BlockSpec((1, spg, dh, nq), lambda g, i: (g, i, 0, 0)),
                   pl.BlockSpec((1, spg, nq, LANES), lambda g, i: (g, i, 0, 0))] + r_out,
        out_shape=[jax.ShapeDtypeStruct((ATT_KV, ns, dh, nq), F32),
                   jax.ShapeDtypeStruct((ATT_KV, ns, nq, LANES), F32)] + r_shapes,
        scratch_shapes=r_sems,
        compiler_params=_params(("arbitrary", "arbitrary")),
    )(qt, ks, vt, *r_args)


def fa_bwd(qs, qt, dos, dot_, ot, lse, ks, kt, vt, ride=None):
    _, ns, dh, nq = qt.shape
    lk = ks.shape[1]
    bk = min(FA_K_BWD, lk)
    nk = lk // bk
    tiles = bk // nq
    spg = min(FA_STRIPS_BWD, ns)
    nc = bk // LANES
    grid = (ATT_KV, ns // spg)

    def body(qs_ref, qt_ref, dos_ref, dot_ref, ot_ref, lse_ref, ks_ref, kt_ref, vt_ref, dq_ref, dk_ref, dv_ref):
        @pl.when(pl.program_id(1) == 0)
        def _():
            dk_ref[...] = jnp.zeros(dk_ref.shape, F32)
            dv_ref[...] = jnp.zeros(dv_ref.shape, F32)

        strips = range(spg)
        lse_b = [lse_ref[0, c] for c in strips]
        d_b = [_col_bcast(jnp.sum(dot_ref[0, c].astype(F32) * ot_ref[0, c], axis=0, keepdims=True)) for c in strips]

        def step(j, dqs):
            ktb, vtb = _pad_rows(_key_block(kt_ref, j, tiles)), _pad_rows(_key_block(vt_ref, j, tiles))
            kb = ks_ref[0, pl.ds(pl.multiple_of(j * bk, bk), bk), :]
            prods = [(_dg(qs_ref[0, c], ktb, 1, 0), _dg(dos_ref[0, c], vtb, 1, 0)) for c in strips]
            out = []
            for c in strips:
                s, dp = prods[c]
                ps, dss = [], []
                for cc in range(nc):
                    sl = slice(cc * LANES, (cc + 1) * LANES)
                    pc = jnp.exp(s[:, sl] - lse_b[c])
                    ps.append(pc.astype(BF16))
                    dss.append((pc * (dp[:, sl] - d_b[c])).astype(BF16))
                p, ds = jnp.concatenate(ps, axis=1), jnp.concatenate(dss, axis=1)
                dv = _dg(dot_ref[0, c], p, 1, 0)
                dk = _dg(qt_ref[0, c], ds, 1, 0)
                for i in range(tiles):
                    dv_ref[0, j * tiles + i] += dv[:, i * nq:(i + 1) * nq]
                    dk_ref[0, j * tiles + i] += dk[:, i * nq:(i + 1) * nq]
                out.append(dqs[c] + _dg(ds, kb, 1, 0))
            return tuple(out)

        dqs = lax.fori_loop(0, nk, step, tuple(jnp.zeros((nq, LANES), F32) for _ in strips))
        for c in strips:
            dq_ref[0, c] = dqs[c].T[:dh]

    sspec = pl.BlockSpec((1, spg, nq, LANES), lambda g, i: (g, i, 0, 0))
    tspec = pl.BlockSpec((1, spg, dh, nq), lambda g, i: (g, i, 0, 0))
    kspec = pl.BlockSpec((1, lk // nq, dh, nq), lambda g, i: (g, 0, 0, 0))
    body, r_in, r_out, r_shapes, r_sems, r_args = _riding(body, 9, 3, grid, ride)
    return pl.pallas_call(
        body, name="fa_bwd", grid=grid,
        in_specs=[sspec, tspec, sspec, tspec, tspec, sspec, pl.BlockSpec((1, lk, LANES), lambda g, i: (g, 0, 0)),
                  kspec, kspec] + r_in,
        out_specs=[tspec, kspec, kspec] + r_out,
        out_shape=[jax.ShapeDtypeStruct((ATT_KV, ns, dh, nq), F32),
                   jax.ShapeDtypeStruct((ATT_KV, lk // nq, dh, nq), F32),
                   jax.ShapeDtypeStruct((ATT_KV, lk // nq, dh, nq), F32)] + r_shapes,
        scratch_shapes=r_sems,
        compiler_params=_params(("arbitrary", "arbitrary")),
    )(qs, qt, dos, dot_, ot, lse, ks, kt, vt, *r_args)


def _post_mix_fn(of, ob, ug, oa, hgw, attw):
    o = of + ob
    parts = []
    for h in range(HG_HEADS):
        parts.append(_rms(o[:, h * HG_D:(h + 1) * HG_D], hgw))
    hg = jnp.concatenate(parts, axis=1) * (ug * _sigmoid(ug))
    return jnp.concatenate([hg, _rms(oa, attw)], axis=1)


def _swiglu_fn(gate, up):
    return gate * _sigmoid(gate) * up


def _loss_fn(x2, w, tgt):
    e = _rms(x2, w) - tgt
    return 0.5 * jnp.sum(jnp.mean(e * e, axis=-1, keepdims=True), axis=0, keepdims=True)


def _place():
    return lax.axis_index("x"), lax.axis_index("y"), lax.axis_index("c")


def _other_chips(x, y):
    return [(1 - x, y), (x, 1 - y), (1 - x, 1 - y)]


def _hbm_specs(n):
    return [pl.BlockSpec(memory_space=pl.ANY)] * n


SEMS_PER_ARRAY = 7


def _exchange_sems(n):
    return [pltpu.SemaphoreType.DMA((SEMS_PER_ARRAY * n,)), pltpu.SemaphoreType.DMA((SEMS_PER_ARRAY * n,)),
            pltpu.SemaphoreType.DMA((n,))]


def _exchange_phases(srcs, outs, ssem, rsem, lsem, reduce):
    n = len(srcs)
    x, y, c = _place()
    k = 2 * x + y
    sib = (x, y, 1 - c)
    chips = _other_chips(x, y)
    pairs = [(a, j) for a in range(n) for j in range(3)]

    def hrows(a):
        return srcs[a].shape[1] if reduce else srcs[a].shape[0] // 2

    def half(a, kk, cc):
        return outs[a].at[kk, pl.ds(cc * hrows(a), hrows(a)), :]

    def mine(a, kk):
        return srcs[a].at[kk] if reduce else srcs[a].at[pl.ds(c * hrows(a), hrows(a)), :]

    def copy(a, j, src_ref, dst_ref, to):
        return pltpu.make_async_remote_copy(src_ref=src_ref, dst_ref=dst_ref, send_sem=ssem.at[SEMS_PER_ARRAY * a + j],
                                            recv_sem=rsem.at[SEMS_PER_ARRAY * a + j], device_id=to, device_id_type=MESH)

    def local(a):
        if reduce:
            return pltpu.make_async_copy(srcs[a].at[k], half(a, k, c), lsem.at[a])
        return pltpu.make_async_copy(srcs[a], outs[a].at[k], lsem.at[a])

    def ici(a, j, arriving):
        px, py = chips[j]
        kk = 2 * px + py
        if arriving:
            return copy(a, j, mine(a, k), half(a, kk, c), (px, py, c))
        return copy(a, j, mine(a, kk), half(a, k, c), (px, py, c))

    def passed(a, j, arriving):
        px, py = chips[j]
        kk = 2 * px + py
        return copy(a, 3 + j, half(a, kk, c), half(a, kk, (1 - c) if arriving else c), sib)

    def own(a, arriving):
        return copy(a, 6, mine(a, k), half(a, k, (1 - c) if arriving else c), sib)

    def start():
        for a in range(n):
            local(a).start()
        for a, j in pairs:
            ici(a, j, False).start()
        if reduce:
            for a in range(n):
                own(a, False).start()

    def forward():
        for a, j in pairs:
            ici(a, j, True).wait_recv()
            passed(a, j, False).start()

    def finish():
        for a in range(n):
            if reduce:
                own(a, True).wait_recv()
            for j in range(3):
                passed(a, j, True).wait_recv()
        for a, j in pairs:
            ici(a, j, False).wait_send()
            passed(a, j, False).wait_send()
        for a in range(n):
            if reduce:
                own(a, False).wait_send()
            local(a).wait()

    return start, forward, finish


def _exchange_shapes(arrays, reduce):
    if reduce:
        return [jax.ShapeDtypeStruct((N_CHIPS, 2 * p.shape[1], p.shape[2]), p.dtype) for p in arrays]
    return [jax.ShapeDtypeStruct((N_CHIPS,) + s.shape, s.dtype) for s in arrays]


def exchange(arrays, reduce, name):
    n = len(arrays)

    def body(*refs):
        start, forward, finish = _exchange_phases(refs[:n], refs[n:2 * n], *refs[2 * n:], reduce)
        start()
        forward()
        finish()

    return pl.pallas_call(
        body, name=name, in_specs=_hbm_specs(n), out_specs=_hbm_specs(n),
        out_shape=_exchange_shapes(arrays, reduce), scratch_shapes=_exchange_sems(n),
    )(*arrays)


def rs_siblings(gs, name):
    n = len(gs)

    def body(*refs):
        srcs, outs = refs[:n], refs[n:2 * n]
        ssem, rsem = refs[2 * n:]
        x, y, c = _place()
        cps = []
        for a in range(n):
            hr = srcs[a].shape[1] // 2
            cp = pltpu.make_async_remote_copy(src_ref=srcs[a].at[:, pl.ds((1 - c) * hr, hr), :], dst_ref=outs[a],
                                              send_sem=ssem.at[a], recv_sem=rsem.at[a], device_id=(x, y, 1 - c),
                                              device_id_type=MESH)
            cp.start()
            cps.append(cp)
        for cp in cps:
            cp.wait()

    return pl.pallas_call(
        body, name=name, in_specs=_hbm_specs(n), out_specs=_hbm_specs(n),
        out_shape=[jax.ShapeDtypeStruct((N_CHIPS, g.shape[1] // 2, g.shape[2]), F32) for g in gs],
        scratch_shapes=[pltpu.SemaphoreType.DMA((n,)), pltpu.SemaphoreType.DMA((n,))],
    )(*gs)


def allreduce_small(p, name):
    rows, width = p.shape

    def body(p_ref, s_ref, gath, ssem, rsem):
        x, y, c = _place()
        me = 4 * x + 2 * y + c
        copies = []
        for d in range(1, N_DEV):
            dx, dy, dc = (d >> 2) & 1, (d >> 1) & 1, d & 1
            peer = (x ^ dx, y ^ dy, c ^ dc)
            cp = pltpu.make_async_remote_copy(src_ref=p_ref, dst_ref=gath.at[me], send_sem=ssem.at[d - 1],
                                              recv_sem=rsem.at[d - 1], device_id=peer, device_id_type=MESH)
            cp.start()
            copies.append(cp)
        gath[me] = p_ref[...]
        for d, cp in enumerate(copies, start=1):
            dx, dy, dc = (d >> 2) & 1, (d >> 1) & 1, d & 1
            peer_slot = 4 * (x ^ dx) + 2 * (y ^ dy) + (c ^ dc)
            pltpu.make_async_remote_copy(src_ref=p_ref, dst_ref=gath.at[peer_slot], send_sem=ssem.at[d - 1],
                                         recv_sem=rsem.at[d - 1], device_id=(x ^ dx, y ^ dy, c ^ dc),
                                         device_id_type=MESH).wait_recv()
        for cp in copies:
            cp.wait_send()
        acc = gath[0]
        for d in range(1, N_DEV):
            acc = acc + gath[d]
        s_ref[...] = acc

    return pl.pallas_call(
        body, name=name,
        in_specs=[pl.BlockSpec(memory_space=pltpu.VMEM)],
        out_specs=pl.BlockSpec(memory_space=pltpu.VMEM),
        out_shape=jax.ShapeDtypeStruct((rows, width), F32),
        scratch_shapes=[pltpu.VMEM((N_DEV, rows, width), F32), pltpu.SemaphoreType.DMA((N_DEV - 1,)),
                        pltpu.SemaphoreType.DMA((N_DEV - 1,))],
    )(p)


def add_my_half(g, recv, name):
    _, r, cols = g.shape
    hr = r // 2
    tb = _pick(hr, 512, 8)
    nb = hr // tb
    c_arr = lax.axis_index("c").astype(jnp.int32).reshape(1)

    def body(c_ref, g_ref, r_ref, o_ref):
        o_ref[...] = (g_ref[...] + r_ref[...]).astype(o_ref.dtype)

    return pl.pallas_call(
        body, name=name,
        grid_spec=pltpu.PrefetchScalarGridSpec(
            num_scalar_prefetch=1, grid=(N_CHIPS, nb),
            in_specs=[pl.BlockSpec((1, tb, cols), lambda k, i, c_ref: (k, c_ref[0] * nb + i, 0)),
                      pl.BlockSpec((1, tb, cols), lambda k, i, c_ref: (k, i, 0))],
            out_specs=pl.BlockSpec((1, tb, cols), lambda k, i, c_ref: (k, i, 0))),
        out_shape=jax.ShapeDtypeStruct((N_CHIPS, hr, cols), BF16),
        compiler_params=_params(("arbitrary", "arbitrary")),
    )(c_arr, g, recv)


def sum_chips(parts, name):
    _, hr, cols = parts.shape
    tb = _pick(hr, 512, 8)

    def body(p_ref, o_ref):
        o_ref[...] = ((p_ref[0].astype(F32) + p_ref[1].astype(F32)) + p_ref[2].astype(F32)) + p_ref[3].astype(F32)

    return pl.pallas_call(
        body, name=name, grid=(hr // tb,),
        in_specs=[pl.BlockSpec((N_CHIPS, tb, cols), lambda i: (0, i, 0))],
        out_specs=pl.BlockSpec((tb, cols), lambda i: (i, 0)),
        out_shape=jax.ShapeDtypeStruct((hr, cols), F32),
        compiler_params=_params(("arbitrary",)),
    )(parts)


def adamw(w, g, m, v, name):
    rows, width = w.shape
    tb = _pick(rows, 512, 8)

    def body(w_ref, g_ref, m_ref, v_ref, d_ref, mo_ref, vo_ref):
        gg = g_ref[...]
        m_new = ADAM_B1 * m_ref[...] + (1.0 - ADAM_B1) * gg
        v_new = ADAM_B2 * v_ref[...] + (1.0 - ADAM_B2) * (gg * gg)
        m_hat = m_new / (1.0 - ADAM_B1 ** ADAM_STEP)
        v_hat = v_new / (1.0 - ADAM_B2 ** ADAM_STEP)
        d_ref[...] = -ADAM_LR * (m_hat / (jnp.sqrt(v_hat) + ADAM_EPS) + ADAM_WD * w_ref[...])
        mo_ref[...] = m_new
        vo_ref[...] = v_new

    spec = pl.BlockSpec((tb, width), lambda i: (i, 0))
    return pl.pallas_call(
        body, name=name, grid=(rows // tb,), in_specs=[spec] * 4, out_specs=[spec] * 3,
        out_shape=[jax.ShapeDtypeStruct((rows, width), F32)] * 3,
        compiler_params=_params(("arbitrary",)),
    )(w, g, m, v)


def _pack_rows(vecs, width=1024):
    rows, cur, used = [], [], 0
    for v in vecs:
        n = v.shape[1]
        if used + n > width:
            cur.append(jnp.zeros((1, width - used), F32))
            rows.append(jnp.concatenate(cur, axis=1))
            cur, used = [], 0
        cur.append(v)
        used += n
    cur.append(jnp.zeros((1, width - used), F32))
    rows.append(jnp.concatenate(cur, axis=1))
    return rows


def _pad128(v):
    return jnp.pad(v, ((0, 0), (0, LANES - v.shape[1])))


def kernel(x, norm1_w, w_in, lb_logits, hg_norm_w, q_norm_w, k_norm_w, att_norm_w, w_out, norm2_w, w_gate_up, w_down, final_norm_w, loss_target, m_norm1_w, m_w_in, m_lb_logits, m_hg_norm_w, m_q_norm_w, m_k_norm_w, m_att_norm_w, m_w_out, m_norm2_w, m_w_gate_up, m_w_down, m_final_norm_w, v_norm1_w, v_w_in, v_lb_logits, v_hg_norm_w, v_q_norm_w, v_k_norm_w, v_att_norm_w, v_w_out, v_norm2_w, v_w_gate_up, v_w_down, v_final_norm_w):
    t, d = x.shape[1], x.shape[2]
    xi, yi, ci = _place()
    chip = 2 * xi + yi
    x2d = x.reshape(t, d)
    tgt = loss_target.reshape(t, d)
    tok = min(TOK, t)
    nt = t // tok
    ns = ATT_GROUP * nt

    (g_in,) = exchange([w_in[0].astype(BF16)], False, "allgather_w_in")
    wf_in = g_in.transpose(1, 0, 2).reshape(g_in.shape[1], -1)
    late_w = [w_out[0].astype(BF16), w_gate_up[0].astype(BF16), w_down[0].astype(BF16)]
    lb_rows = lb_logits.reshape(4, LANES) * (ci == 0).astype(F32)
    lb_pad = lax.dynamic_update_slice(jnp.zeros((8, 1024), F32), lb_rows, (0, chip * LANES))
    lb_full = allreduce_small(lb_pad, "gather_lb")[:4, :HG_W]
    l_f0, l_f1, l_b0, l_b1 = (lb_full[i:i + 1] for i in range(4))

    n1 = norm1_w.reshape(1, d)
    n2 = norm2_w.reshape(1, d)
    nf = final_norm_w.reshape(1, d)
    tm = min(512, t)
    (h1,) = _rows(lambda a, w: ((_rms(a, w),), ()), "norm1", t, tm, [_rin(x2d, tm)], [n1], [_rout(t, tm, d, BF16)])
    u = mm_rows(h1, wf_in, "mm_in")
    o_f, ss_f = gla_fwd(u, l_f0, l_f1, 1, False, "gla_fwd_f")
    o_b, ss_b = gla_fwd(u, l_b0, l_b1, 2, True, "gla_fwd_b")

    cos2, sin2 = _rope_tables(t)
    cq, sq = jnp.tile(cos2, (1, ATT_HEADS)), jnp.tile(sin2, (1, ATT_HEADS))
    ck, sk = jnp.tile(cos2, (1, ATT_KV)), jnp.tile(sin2, (1, ATT_KV))
    qw, kw = _pad128(q_norm_w.reshape(1, ATT_DH)), _pad128(k_norm_w.reshape(1, ATT_DH))
    gq, gk = _group_sum_matrix(ATT_QW), _group_sum_matrix(ATT_KVW)
    tq, tk = _tile_matrix(ATT_QW), _tile_matrix(ATT_KVW)
    prep_in = [_rin(u, tok, ATT_QW, 5), _rin(u, tok, ATT_KVW, 24), _rin(cq, tok), _rin(sq, tok), _rin(ck, tok),
               _rin(sk, tok)]
    prep_consts = [qw, kw, gq, gk, tq, tk]

    def att_prep_fn(aq, ak, av, *rest):
        q, k = _att_prep_fn(aq, ak, *rest)
        return (_head_t(q, ATT_HEADS), _head_s(q, ATT_HEADS), _head_s(k, ATT_KV), _head_t(k, ATT_KV),
                _head_t(av, ATT_KV)), ()

    q_t, q_s, k_s, k_t, v_t = _rows(
        att_prep_fn, "att_prep", t, tok, prep_in[:2] + [_rin(u, tok, ATT_KVW, 25)] + prep_in[2:], prep_consts,
        [_tout(ATT_HEADS, nt, ATT_DH, tok, BF16), _tout(ATT_HEADS, nt, tok, LANES, BF16),
         _tout(ATT_KV, nt, tok, LANES, BF16), _tout(ATT_KV, nt, ATT_DH, tok, BF16),
         _tout(ATT_KV, nt, ATT_DH, tok, BF16)])
    q_t = q_t.reshape(ATT_KV, ns, ATT_DH, tok)
    q_s = q_s.reshape(ATT_KV, ns, tok, LANES)
    k_s = k_s.reshape(ATT_KV, t, LANES)
    o_t, lse, g_out, g_gu, g_down = fa_fwd(q_t, k_s, v_t, ride=(late_w, False))
    wf_gu = g_gu.transpose(1, 0, 2).reshape(g_gu.shape[1], -1)
    wf_out = g_out.reshape(-1, g_out.shape[2])
    wf_down = g_down.reshape(-1, g_down.shape[2])
    o_tiles = o_t.reshape(ATT_HEADS, nt, ATT_DH, tok)

    hgw = hg_norm_w.reshape(1, HG_D)
    attw = att_norm_w.reshape(1, ATT_QW)
    mix_in = [_rin(o_f, tok), _rin(o_b, tok), _rin(u, tok, HG_W, 4), _tin(o_tiles)]
    (mix,) = _rows(lambda of, ob, ug, ot, hw, aw: ((_post_mix_fn(of, ob, ug, _from_head_t(ot), hw, aw),), ()),
                   "post_mix", t, tok, mix_in, [hgw, attw], [_rout(t, tok, d, BF16)])
    x1, h2 = mm_res_norm(mix, wf_out, x2d, n2, "mm_out")
    gu2, act = ffn_up(h2, wf_gu)

    dx2, dx2b, g_final, loss_part = mm_res_loss(act, wf_down, x1, tgt, nf, "mm_down_loss")
    dgu2 = ffn_down_dx(dx2b, wf_down, gu2)
    gw_down = mm_tn(act, dx2b, "mm_down_dw")
    dx1, dx1b, g_norm2 = mm_nt_norm_bwd(dgu2, wf_gu, x1, n2, dx2, "mm_gate_up_dx", True)
    gw_gu = mm_tn(h2, dgu2, "mm_gate_up_dw", col_shards=N_CHIPS)
    dmix = mm_rows(dx1b, wf_out, "mm_out_dx", trans_b=True)
    gw_out = mm_tn(mix, dx1b, "mm_out_dw")

    def post_mix_bwd_fn(of, ob, ug, ot, dm, hgw_, attw_):
        _, vjp = jax.vjp(_post_mix_fn, of, ob, ug, _from_head_t(ot), hgw_, attw_)
        dof, _, dug, doa, dhgw, dattw = vjp(dm)
        return (dof, dug, _head_t(doa, ATT_HEADS), _head_s(doa, ATT_HEADS)), (dhgw, dattw)

    do_hg, du_g, do_t, do_s, g_hg, g_att = _rows(
        post_mix_bwd_fn, "post_mix_bwd", t, tok, mix_in + [_rin(dmix, tok)], [hgw, attw],
        [_rout(t, tok, HG_W, F32), _rout(t, tok, HG_W, BF16), _tout(ATT_HEADS, nt, ATT_DH, tok, BF16),
         _tout(ATT_HEADS, nt, tok, LANES, BF16)], [HG_D, ATT_QW])
    late = ["w_out", "w_gate_up", "w_down"]
    late_g = [gw_out.reshape(N_CHIPS, -1, d), gw_gu, gw_down.reshape(N_CHIPS, -1, d)]
    late_part = [add_my_half(g, r, "add_my_half_" + n) for g, r, n in zip(late_g, rs_siblings(late_g, "rs_siblings_late"), late)]
    dq_t, dk_t, dv_t, *late_parts = fa_bwd(q_s, q_t, do_s.reshape(q_s.shape), do_t.reshape(q_t.shape), o_t, lse, k_s, k_t,
                                           v_t, ride=(late_part, True))

    def att_prep_bwd_fn(aq, ak, cq_, sq_, ck_, sk_, dqt, dkt, dvt, qw_, kw_, gq_, gk_, tq_, tk_):
        _, vjp = jax.vjp(lambda a, b, c_, e: _att_prep_fn(a, b, cq_, sq_, ck_, sk_, c_, e, gq_, gk_, tq_, tk_),
                         aq, ak, qw_, kw_)
        daq, dak, dqw, dkw = vjp((_from_head_t(dqt), _from_head_t(dkt)))
        return (daq, dak, _from_head_t(dvt)), (dqw, dkw)

    da_q, da_k, da_v, g_q, g_k = _rows(
        att_prep_bwd_fn, "att_prep_bwd", t, tok,
        prep_in + [_tin(dq_t.reshape(ATT_HEADS, nt, ATT_DH, tok)), _tin(dk_t), _tin(dv_t)], prep_consts,
        [_rout(t, tok, ATT_QW, BF16), _rout(t, tok, ATT_KVW, BF16), _rout(t, tok, ATT_KVW, BF16)], [LANES, LANES])

    dq_f, df_f, di_f, dl_f0, dl_f1 = gla_bwd(u, l_f0, l_f1, ss_f, do_hg, 1, False, "gla_bwd_f")
    dq_b, df_b, di_b, dl_b0, dl_b1 = gla_bwd(u, l_b0, l_b1, ss_b, do_hg, 2, True, "gla_bwd_b")

    def assemble_fn(qf, qb, ff, fb, i_f, i_b, dg, aq, ak, av):
        parts = [qf + qb, ff, fb, i_f + i_b, dg.astype(F32), aq.astype(F32), ak.astype(F32), av.astype(F32)]
        return (jnp.concatenate(parts, axis=1),), ()

    (du,) = _rows(assemble_fn, "assemble_du", t, tok,
                  [_rin(a, tok) for a in (dq_f, dq_b, df_f, df_b, di_f, di_b, du_g, da_q, da_k, da_v)], [],
                  [_rout(t, tok, u.shape[1], BF16)])
    grad_x, g_norm1 = mm_nt_norm_bwd(du, wf_in, x2d, n1, dx1, "mm_in_dx", False)
    gw_in_t = mm_tn(h1, du, "mm_in_dw", transpose_out=True)

    names = ["w_in"] + late
    g_in4 = gw_in_t.reshape(N_CHIPS, -1, d)
    in_part = add_my_half(g_in4, rs_siblings([g_in4], "rs_siblings_w_in")[0], "add_my_half_w_in")
    parts = list(exchange([in_part], True, "rs_chips_w_in")) + late_parts
    g_shard = [sum_chips(p, "sum_chips_" + n) for p, n in zip(parts, names)]
    g_shard[0] = g_shard[0].T
    big = {}
    for n, g, w, m, v in zip(names, g_shard, (w_in, w_out, w_gate_up, w_down), (m_w_in, m_w_out, m_w_gate_up, m_w_down),
                             (v_w_in, v_w_out, v_w_gate_up, v_w_down)):
        dlt, mn, vn = adamw(w[0], g, m[0], v[0], "adamw_" + n)
        big[n] = (g[None], dlt[None], mn[None], vn[None])

    small = [g_norm1, g_norm2, g_final, g_att, g_hg, g_q, g_k, loss_part, dl_f0, dl_f1, dl_b0, dl_b1]
    packed = _pack_rows(small)
    packed += [jnp.zeros((1, 1024), F32)] * (8 - len(packed))
    tot = allreduce_small(jnp.concatenate(packed, axis=0), "allreduce_small")
    s_norm1, s_norm2, s_final = tot[0:1], tot[1:2], tot[2:3]
    s_att, s_hg, s_q, s_k = tot[3:4, 0:512], tot[3:4, 512:640], tot[3:4, 640:704], tot[3:4, 768:832]
    loss = tot[3, 896]
    s_lb = jnp.concatenate([tot[4:5, 0:512], tot[4:5, 512:1024], tot[5:6, 0:512], tot[5:6, 512:1024]], axis=0)
    s_lb = lax.dynamic_slice(s_lb, (0, chip * LANES), (4, LANES)).reshape(1, 512)

    snames = ["norm1_w", "lb_logits", "hg_norm_w", "q_norm_w", "k_norm_w", "att_norm_w", "norm2_w", "final_norm_w"]
    g_small = dict(zip(snames, [s_norm1, s_lb, s_hg, s_q, s_k, s_att, s_norm2, s_final]))
    w_small = dict(zip(snames, [norm1_w, lb_logits, hg_norm_w, q_norm_w, k_norm_w, att_norm_w, norm2_w, final_norm_w]))
    m_small = dict(zip(snames, [m_norm1_w, m_lb_logits, m_hg_norm_w, m_q_norm_w, m_k_norm_w, m_att_norm_w, m_norm2_w, m_final_norm_w]))
    v_small = dict(zip(snames, [v_norm1_w, v_lb_logits, v_hg_norm_w, v_q_norm_w, v_k_norm_w, v_att_norm_w, v_norm2_w, v_final_norm_w]))

    def pack_small(tree):
        rows = _pack_rows([tree[n].reshape(1, -1) for n in snames])
        rows += [jnp.zeros((1, 1024), F32)] * (8 - len(rows))
        return jnp.concatenate(rows, axis=0)

    d_s, m_s, v_s = adamw(pack_small(w_small), pack_small(g_small), pack_small(m_small), pack_small(v_small), "adamw_small")

    def unpack_small(a):
        out, r, used = {}, 0, 0
        for n in snames:
            size = w_small[n].size
            if used + size > 1024:
                r, used = r + 1, 0
            out[n] = a[r, used:used + size].reshape(w_small[n].shape)
            used += size
        return out

    d_sm, m_sm, v_sm = unpack_small(d_s), unpack_small(m_s), unpack_small(v_s)
    g_sm = {n: g_small[n].reshape(w_small[n].shape) for n in snames}

    order = ["norm1_w", "w_in", "lb_logits", "hg_norm_w", "q_norm_w", "k_norm_w", "att_norm_w", "w_out", "norm2_w",
             "w_gate_up", "w_down", "final_norm_w"]

    def pick(small_tree, idx):
        return [big[n][idx] if n in big else small_tree[n] for n in order]

    return (loss, grad_x.reshape(x.shape), *pick(g_sm, 0), *pick(d_sm, 1), *pick(m_sm, 2), *pick(v_sm, 3))
```

```python
import functools

import numpy as np
import jax
import jax.numpy as jnp
from jax import lax
from jax.experimental import pallas as pl
from jax.experimental.pallas import tpu as pltpu

F32 = jnp.float32
BF16 = jnp.bfloat16
MESH = pl.DeviceIdType.MESH

EPS = 1e-6
GRID_W = 64
HG_HEADS = 4
HG_D = 128
HG_W = HG_HEADS * HG_D
CHUNK = 64
ATT_HEADS = 8
ATT_KV = 2
ATT_GROUP = ATT_HEADS // ATT_KV
ATT_DH = 64
ATT_QW = ATT_HEADS * ATT_DH
ATT_KVW = ATT_KV * ATT_DH
ROPE_THETA = 10000.0
N_CHIPS = 4
N_DEV = 8

ADAM_LR = 0.001
ADAM_B1 = 0.9
ADAM_B2 = 0.999
ADAM_EPS = 1e-08
ADAM_WD = 0.01
ADAM_STEP = 10

VMEM_LIMIT = 52 * 1024 * 1024
LANES = 128
TOK = 256


def _params(sem=None):
    return pltpu.CompilerParams(dimension_semantics=sem, vmem_limit_bytes=VMEM_LIMIT)


def _dg(a, b, ca, cb):
    return lax.dot_general(a.astype(BF16), b.astype(BF16), (((ca,), (cb,)), ((), ())),
                           preferred_element_type=F32)


@jax.custom_vjp
def dot_nn(a, b):
    return _dg(a, b, 1, 0)


def _dot_nn_fwd(a, b):
    return _dg(a, b, 1, 0), (a, b)


def _dot_nn_bwd(res, g):
    a, b = res
    return _dg(g, b, 1, 1), _dg(a, g, 0, 0)


dot_nn.defvjp(_dot_nn_fwd, _dot_nn_bwd)


@jax.custom_vjp
def dot_nt(a, b):
    return _dg(a, b, 1, 1)


def _dot_nt_fwd(a, b):
    return _dg(a, b, 1, 1), (a, b)


def _dot_nt_bwd(res, g):
    a, b = res
    return _dg(g, b, 1, 0), _dg(g, a, 0, 0)


dot_nt.defvjp(_dot_nt_fwd, _dot_nt_bwd)


@jax.custom_vjp
def dot_tn(a, b):
    return _dg(a, b, 0, 0)


def _dot_tn_fwd(a, b):
    return _dg(a, b, 0, 0), (a, b)


def _dot_tn_bwd(res, g):
    a, b = res
    return _dg(b, g, 1, 1), _dg(a, g, 1, 0)


dot_tn.defvjp(_dot_tn_fwd, _dot_tn_bwd)


def _split3(a):
    hi = a.astype(BF16)
    r1 = a - hi.astype(F32)
    mid = r1.astype(BF16)
    lo = (r1 - mid.astype(F32)).astype(BF16)
    return lo, mid, hi


def _sum3(terms):
    lo, mid, hi = terms
    return (lo + mid) + hi


@jax.custom_vjp
def xdot_r(a, m):
    return _sum3([_dg(p, m, 1, 0) for p in _split3(a)])


def _xdot_r_fwd(a, m):
    return xdot_r(a, m), m


def _xdot_r_bwd(m, g):
    return _sum3([_dg(p, m, 1, 1) for p in _split3(g)]), jnp.zeros_like(m)


xdot_r.defvjp(_xdot_r_fwd, _xdot_r_bwd)


@jax.custom_vjp
def xdot_l(m, a):
    return _sum3([_dg(m, p, 1, 0) for p in _split3(a)])


def _xdot_l_fwd(m, a):
    return xdot_l(m, a), m


def _xdot_l_bwd(m, g):
    return jnp.zeros_like(m), _sum3([_dg(m, p, 0, 0) for p in _split3(g)])


xdot_l.defvjp(_xdot_l_fwd, _xdot_l_bwd)


@jax.custom_vjp
def swap_pairs(y):
    n = y.shape[-1]
    lane = lax.broadcasted_iota(jnp.int32, y.shape, 1)
    nxt = pltpu.roll(y, n - 1, 1)
    prv = pltpu.roll(y, 1, 1)
    return jnp.where(lane % 2 == 0, nxt, prv)


def _swap_fwd(y):
    return swap_pairs(y), None


def _swap_bwd(_, g):
    return (swap_pairs(g),)


swap_pairs.defvjp(_swap_fwd, _swap_bwd)


def _rms(x, w):
    return x * lax.rsqrt(jnp.mean(x * x, axis=-1, keepdims=True) + EPS) * w


def _sigmoid(x):
    return jax.nn.sigmoid(x)


def _rows(fn, name, t, tm, ins, consts, outs, accs=()):
    n_r, n_c, n_o, n_a = len(ins), len(consts), len(outs), len(accs)

    def body(*refs):
        r = refs[:n_r]
        c = refs[n_r:n_r + n_c]
        o = refs[n_r + n_c:n_r + n_c + n_o]
        a = refs[n_r + n_c + n_o:]
        ro, ao = fn(*[x[...] for x in r], *[x[...] for x in c])
        for ref, val in zip(o, ro):
            if isinstance(val, (list, tuple)):
                for h, piece in enumerate(val):
                    ref[h, 0] = piece.astype(ref.dtype)
            else:
                ref[...] = val.astype(ref.dtype)
        if n_a:
            @pl.when(pl.program_id(0) == 0)
            def _():
                for ref in a:
                    ref[...] = jnp.zeros(ref.shape, F32)
            for ref, val in zip(a, ao):
                ref[...] += val

    in_specs = [s for _, s in ins]
    in_specs += [pl.BlockSpec(c.shape, lambda i, nd=c.ndim: (0,) * nd) for c in consts]
    out_specs = [s for _, s in outs] + [pl.BlockSpec((1, w), lambda i: (0, 0)) for w in accs]
    out_shape = [s for s, _ in outs] + [jax.ShapeDtypeStruct((1, w), F32) for w in accs]
    return pl.pallas_call(
        body, name=name, grid=(t // tm,), in_specs=in_specs, out_specs=out_specs, out_shape=out_shape,
        compiler_params=_params(("arbitrary",)),
    )(*[a for a, _ in ins], *consts)


def _rin(a, tm, width=None, cb=0):
    width = a.shape[1] if width is None else width
    return a, pl.BlockSpec((tm, width), lambda i, cb=cb: (i, cb))


def _rout(t, tm, width, dtype):
    return jax.ShapeDtypeStruct((t, width), dtype), pl.BlockSpec((tm, width), lambda i: (i, 0))


def _tin(a):
    return a, pl.BlockSpec((a.shape[0], 1) + a.shape[2:], lambda i: (0, i, 0, 0))


def _tout(heads, nt, r, c, dtype):
    return jax.ShapeDtypeStruct((heads, nt, r, c), dtype), pl.BlockSpec((heads, 1, r, c), lambda i: (0, i, 0, 0))


def _pick(n, cap, mult):
    best = None
    for d in range(mult, min(n, cap) + 1, mult):
        if n % d == 0:
            best = d
    return best if best is not None else n


def mm_rows(a, b, name, trans_b=False, res=None, out_dtype=F32):
    m, k = a.shape
    n = b.shape[0] if trans_b else b.shape[1]
    tn = _pick(n, 3328, LANES)
    tm = _pick(m, 512 if k <= 3072 else 256, 8)
    has_res = res is not None

    def body(*refs):
        if has_res:
            a_ref, b_ref, r_ref, o_ref = refs
        else:
            a_ref, b_ref, o_ref = refs
        acc = _dg(a_ref[...], b_ref[...], 1, 1 if trans_b else 0)
        if has_res:
            acc = acc + r_ref[...]
        o_ref[...] = acc.astype(o_ref.dtype)

    in_specs = [pl.BlockSpec((tm, k), lambda j, i: (i, 0))]
    if trans_b:
        in_specs.append(pl.BlockSpec((tn, k), lambda j, i: (j, 0)))
    else:
        in_specs.append(pl.BlockSpec((k, tn), lambda j, i: (0, j)))
    args = [a, b]
    if has_res:
        in_specs.append(pl.BlockSpec((tm, tn), lambda j, i: (i, j)))
        args.append(res)
    return pl.pallas_call(
        body, name=name, grid=(n // tn, m // tm), in_specs=in_specs,
        out_specs=pl.BlockSpec((tm, tn), lambda j, i: (i, j)),
        out_shape=jax.ShapeDtypeStruct((m, n), out_dtype),
        compiler_params=_params(("arbitrary", "arbitrary")),
    )(*args)


def mm_tn(a, b, name, col_shards=None, transpose_out=False):
    t, m = a.shape
    parts = b.shape[0] if b.ndim == 3 else 1
    n = parts * b.shape[-1]
    tm = _pick(m, 512, LANES)
    width = n // col_shards if col_shards else None
    group = max(1, 3328 // width) if col_shards else 1
    tn = width * group if col_shards else _pick(n, 3328, LANES)
    tk = _pick(t, 1024, 8)
    nkk = t // tk
    per_part = b.shape[-1] // tn

    def body(a_ref, b_ref, o_ref, *scratch):
        acc_ref = scratch[0] if transpose_out else o_ref

        @pl.when(pl.program_id(2) == 0)
        def _():
            acc_ref[...] = jnp.zeros(acc_ref.shape, F32)
        acc = _dg(a_ref[...], b_ref[0] if b.ndim == 3 else b_ref[...], 0, 0)
        if col_shards:
            for s in range(group):
                acc_ref[s] += acc[:, s * width:(s + 1) * width]
        else:
            acc_ref[...] += acc
        if transpose_out:
            @pl.when(pl.program_id(2) == nkk - 1)
            def _():
                o_ref[...] = acc_ref[...].T

    if col_shards:
        out_spec = pl.BlockSpec((group, tm, width), lambda i, j, kk: (j, i, 0))
        out_shape = jax.ShapeDtypeStruct((col_shards, m, width), F32)
    elif transpose_out:
        out_spec = pl.BlockSpec((tn, tm), lambda i, j, kk: (j, i))
        out_shape = jax.ShapeDtypeStruct((n, m), F32)
    else:
        out_spec = pl.BlockSpec((tm, tn), lambda i, j, kk: (i, j))
        out_shape = jax.ShapeDtypeStruct((m, n), F32)
    if b.ndim == 3:
        b_spec = pl.BlockSpec((1, tk, tn), lambda i, j, kk: (j // per_part, kk, j % per_part))
    else:
        b_spec = pl.BlockSpec((tk, tn), lambda i, j, kk: (kk, j))
    return pl.pallas_call(
        body, name=name, grid=(m // tm, n // tn, nkk),
        in_specs=[pl.BlockSpec((tk, tm), lambda i, j, kk: (kk, i)), b_spec],
        out_specs=out_spec, out_shape=out_shape,
        scratch_shapes=[pltpu.VMEM((tm, tn), F32)] if transpose_out else [],
        compiler_params=_params(("arbitrary", "arbitrary", "arbitrary")),
    )(a, b)


def ffn_up(h2, w_gu):
    t, k = h2.shape
    dff = w_gu.shape[1] // 2
    tn = _pick(dff, 1408, LANES)
    ncol = dff // tn
    tm = _pick(t, 512, 8)

    def body(a_ref, wg_ref, wu_ref, gu_ref, act_ref):
        a = a_ref[...]
        g = _dg(a, wg_ref[...], 1, 0)
        u = _dg(a, wu_ref[...], 1, 0)
        gu_ref[0] = g
        gu_ref[1] = u
        act_ref[...] = _swiglu_fn(g, u).astype(act_ref.dtype)

    return pl.pallas_call(
        body, name="ffn_up", grid=(ncol, t // tm),
        in_specs=[pl.BlockSpec((tm, k), lambda s, i: (i, 0)), pl.BlockSpec((k, tn), lambda s, i: (0, s)),
                  pl.BlockSpec((k, tn), lambda s, i: (0, s + ncol))],
        out_specs=[pl.BlockSpec((2, tm, tn), lambda s, i: (0, i, s)), pl.BlockSpec((tm, tn), lambda s, i: (i, s))],
        out_shape=[jax.ShapeDtypeStruct((2, t, dff), F32), jax.ShapeDtypeStruct((t, dff), BF16)],
        compiler_params=_params(("arbitrary", "arbitrary")),
    )(h2, w_gu, w_gu)


def ffn_down_dx(dx2b, w_down, gu2):
    t, k = dx2b.shape
    dff = w_down.shape[0]
    tn = _pick(dff, 1408, LANES)
    tm = _pick(t, 512, 8)

    def body(a_ref, w_ref, gu_ref, o_ref):
        dact = _dg(a_ref[...], w_ref[...], 1, 1)
        _, vjp = jax.vjp(_swiglu_fn, gu_ref[0], gu_ref[1])
        dg, du = vjp(dact)
        o_ref[0] = dg.astype(o_ref.dtype)
        o_ref[1] = du.astype(o_ref.dtype)

    blk = pl.BlockSpec((2, tm, tn), lambda s, i: (0, i, s))
    return pl.pallas_call(
        body, name="ffn_down_dx", grid=(dff // tn, t // tm),
        in_specs=[pl.BlockSpec((tm, k), lambda s, i: (i, 0)), pl.BlockSpec((tn, k), lambda s, i: (s, 0)), blk],
        out_specs=blk, out_shape=jax.ShapeDtypeStruct((2, t, dff), BF16),
        compiler_params=_params(("arbitrary", "arbitrary")),
    )(dx2b, w_down, gu2)


def mm_res_norm(a, b, res, w, name):
    t, k = a.shape
    d = b.shape[1]
    tm = _pick(t, 512, 8)

    def body(a_ref, b_ref, r_ref, w_ref, x_ref, h_ref):
        xv = _dg(a_ref[...], b_ref[...], 1, 0) + r_ref[...]
        x_ref[...] = xv
        h_ref[...] = _rms(xv, w_ref[...]).astype(h_ref.dtype)

    row = pl.BlockSpec((tm, d), lambda i: (i, 0))
    return pl.pallas_call(
        body, name=name, grid=(t // tm,),
        in_specs=[pl.BlockSpec((tm, k), lambda i: (i, 0)), pl.BlockSpec((k, d), lambda i: (0, 0)), row,
                  pl.BlockSpec((1, d), lambda i: (0, 0))],
        out_specs=[row, row],
        out_shape=[jax.ShapeDtypeStruct((t, d), F32), jax.ShapeDtypeStruct((t, d), BF16)],
        compiler_params=_params(("arbitrary",)),
    )(a, b, res, w)


def mm_res_loss(a, b, res, tgt, w, name):
    t, k = a.shape
    d = b.shape[1]
    tm = _pick(t, 256, 8)

    def body(a_ref, b_ref, r_ref, t_ref, w_ref, dx_ref, dxb_ref, dw_ref, loss_ref):
        @pl.when(pl.program_id(0) == 0)
        def _():
            dw_ref[...] = jnp.zeros(dw_ref.shape, F32)
            loss_ref[...] = jnp.zeros(loss_ref.shape, F32)
        xv = _dg(a_ref[...], b_ref[...], 1, 0) + r_ref[...]
        tg = t_ref[...]
        val, vjp = jax.vjp(lambda x_, w_: _loss_fn(x_, w_, tg), xv, w_ref[...])
        dx, dw = vjp(jnp.ones((1, 1), F32))
        dx_ref[...] = dx
        dxb_ref[...] = dx.astype(dxb_ref.dtype)
        dw_ref[...] += dw
        loss_ref[...] += jnp.broadcast_to(val, (1, LANES))

    row = pl.BlockSpec((tm, d), lambda i: (i, 0))
    vec = pl.BlockSpec((1, d), lambda i: (0, 0))
    return pl.pallas_call(
        body, name=name, grid=(t // tm,),
        in_specs=[pl.BlockSpec((tm, k), lambda i: (i, 0)), pl.BlockSpec((k, d), lambda i: (0, 0)), row, row, vec],
        out_specs=[row, row, vec, pl.BlockSpec((1, LANES), lambda i: (0, 0))],
        out_shape=[jax.ShapeDtypeStruct((t, d), F32), jax.ShapeDtypeStruct((t, d), BF16),
                   jax.ShapeDtypeStruct((1, d), F32), jax.ShapeDtypeStruct((1, LANES), F32)],
        compiler_params=_params(("arbitrary",)),
    )(a, b, res, tgt, w)


def mm_nt_norm_bwd(a, wmat, x, w, dres, name, with_bf16):
    parts = a.shape[0] if a.ndim == 3 else 1
    t, kp = a.shape[-2], a.shape[-1]
    d = wmat.shape[0]
    tm = _pick(t, 256, 8)

    def body(*refs):
        a_ref, w_refs = refs[0], refs[1:1 + parts]
        x_ref, nw_ref, r_ref, dx_ref = refs[1 + parts:5 + parts]
        dw_ref = refs[-1]

        @pl.when(pl.program_id(0) == 0)
        def _():
            dw_ref[...] = jnp.zeros(dw_ref.shape, F32)
        dh = None
        for p in range(parts):
            term = _dg(a_ref[p] if a.ndim == 3 else a_ref[...], w_refs[p][...], 1, 1)
            dh = term if dh is None else dh + term
        _, vjp = jax.vjp(_rms, x_ref[...], nw_ref[...])
        dx, dw = vjp(dh)
        dx = dx + r_ref[...]
        dx_ref[...] = dx
        if with_bf16:
            refs[5 + parts][...] = dx.astype(BF16)
        dw_ref[...] += dw

    row = pl.BlockSpec((tm, d), lambda i: (i, 0))
    vec = pl.BlockSpec((1, d), lambda i: (0, 0))
    a_spec = pl.BlockSpec((parts, tm, kp), lambda i: (0, i, 0)) if a.ndim == 3 else pl.BlockSpec((tm, kp), lambda i: (i, 0))
    w_specs = [pl.BlockSpec((d, kp), lambda i, p=p: (0, p)) for p in range(parts)]
    outs = [row] + ([row] if with_bf16 else []) + [vec]
    shapes = [jax.ShapeDtypeStruct((t, d), F32)] + ([jax.ShapeDtypeStruct((t, d), BF16)] if with_bf16 else [])
    return pl.pallas_call(
        body, name=name, grid=(t // tm,),
        in_specs=[a_spec] + w_specs + [row, vec, row],
        out_specs=outs, out_shape=shapes + [jax.ShapeDtypeStruct((1, d), F32)],
        compiler_params=_params(("arbitrary",)),
    )(a, *([wmat] * parts), x, w, dres)


def _lower_bound(l0, l1):
    m = jnp.maximum(l0, l1)
    e0 = jnp.exp(l0 - m)
    e1 = jnp.exp(l1 - m)
    return e0 / (e0 + e1)


def _gla_consts(rev):
    ri = lax.broadcasted_iota(jnp.int32, (CHUNK, CHUNK), 0)
    ci = lax.broadcasted_iota(jnp.int32, (CHUNK, CHUNK), 1)
    keep = (ci >= ri) if rev else (ci <= ri)
    ref_mask = lax.broadcasted_iota(jnp.int32, (CHUNK, 1), 0) == (CHUNK // 2 if rev else CHUNK // 2 - 1)
    return keep, ref_mask


def _gla_block(uq, uf, ui, l0, l1, st_in, rev):
    ncb = uq.shape[0] // CHUNK
    heads = range(HG_HEADS)
    keep, ref_mask = _gla_consts(rev)
    cum = keep.astype(F32)
    lb = _lower_bound(l0, l1)
    q = uq * _sigmoid(uq)
    k = (1.0 - lb) * _sigmoid(-uf)
    g = jnp.log(lb + (1.0 - lb) * _sigmoid(uf))

    def rows(a, c):
        return a[c * CHUNK:(c + 1) * CHUNK]

    def head(a, h):
        return a[:, h * HG_D:(h + 1) * HG_D]

    bs = [xdot_l(cum, rows(g, c)) for c in range(ncb)]
    q_in, k_in, q_b, k_d, decay = [], [], [], [], []
    for c in range(ncb):
        b = bs[c]
        b_ref = jnp.sum(jnp.where(ref_mask, b, 0.0), axis=0, keepdims=True)
        b_last = jnp.sum(rows(g, c), axis=0, keepdims=True)
        qc, kc = rows(q, c), rows(k, c)
        q_in.append(qc * jnp.exp(b - b_ref))
        k_in.append(kc * jnp.exp(b_ref - b))
        q_b.append(qc * jnp.exp(b))
        k_d.append(kc * jnp.exp(b_last - b))
        decay.append(jnp.exp(b_last))
    scores = [[jnp.where(keep, dot_nt(head(q_in[c], h), head(k_in[c], h)), 0.0) for h in heads] for c in range(ncb)]
    o_intra = [[dot_nn(scores[c][h], head(rows(ui, c), h)) for h in heads] for c in range(ncb)]
    contrib = [[dot_tn(head(rows(ui, c), h), head(k_d[c], h)) for h in heads] for c in range(ncb)]
    st = list(st_in)
    o_rows = [None] * ncb
    for c in (reversed(range(ncb)) if rev else range(ncb)):
        parts = []
        for h in heads:
            parts.append(o_intra[c][h] + dot_nt(head(q_b[c], h), st[h]))
            st[h] = st[h] * head(decay[c], h) + contrib[c][h]
        o_rows[c] = jnp.concatenate(parts, axis=1)
    return jnp.concatenate(o_rows, axis=0), tuple(st)


def _gla_blocks(t):
    tb = min(512, t)
    return tb, t // tb


def gla_fwd(u, l0, l1, fcol, rev, name):
    t = u.shape[0]
    tb, nb = _gla_blocks(t)

    def blk(i):
        return (nb - 1 - i) if rev else i

    def body(uq_ref, uf_ref, ui_ref, l0_ref, l1_ref, o_ref, ss_ref, st_ref):
        @pl.when(pl.program_id(0) == 0)
        def _():
            st_ref[...] = jnp.zeros(st_ref.shape, F32)
        ss_ref[0] = st_ref[...]
        o, st_out = _gla_block(uq_ref[...], uf_ref[...], ui_ref[...], l0_ref[...], l1_ref[...],
                               tuple(st_ref[h] for h in range(HG_HEADS)), rev)
        o_ref[...] = o
        for h in range(HG_HEADS):
            st_ref[h] = st_out[h]

    row = lambda cb: pl.BlockSpec((tb, HG_W), lambda i: (blk(i), cb))
    vec = pl.BlockSpec((1, HG_W), lambda i: (0, 0))
    return pl.pallas_call(
        body, name=name, grid=(nb,),
        in_specs=[row(0), row(fcol), row(3), vec, vec],
        out_specs=[pl.BlockSpec((tb, HG_W), lambda i: (blk(i), 0)),
                   pl.BlockSpec((1, HG_HEADS, HG_D, HG_D), lambda i: (blk(i), 0, 0, 0))],
        out_shape=[jax.ShapeDtypeStruct((t, HG_W), F32),
                   jax.ShapeDtypeStruct((nb, HG_HEADS, HG_D, HG_D), F32)],
        scratch_shapes=[pltpu.VMEM((HG_HEADS, HG_D, HG_D), F32)],
        compiler_params=_params(("arbitrary",)),
    )(u, u, u, l0, l1)


def gla_bwd(u, l0, l1, ss, do, fcol, rev, name):
    t = u.shape[0]
    tb, nb = _gla_blocks(t)

    def blk(i):
        return i if rev else (nb - 1 - i)

    def body(uq_ref, uf_ref, ui_ref, l0_ref, l1_ref, ss_ref, do_ref,
             dq_ref, df_ref, di_ref, dl0_ref, dl1_ref, dst_ref):
        @pl.when(pl.program_id(0) == 0)
        def _():
            dst_ref[...] = jnp.zeros(dst_ref.shape, F32)
            dl0_ref[...] = jnp.zeros(dl0_ref.shape, F32)
            dl1_ref[...] = jnp.zeros(dl1_ref.shape, F32)
        heads = range(HG_HEADS)
        _, vjp = jax.vjp(functools.partial(_gla_block, rev=rev), uq_ref[...], uf_ref[...], ui_ref[...],
                         l0_ref[...], l1_ref[...], tuple(ss_ref[0, h] for h in heads))
        dq, df, di, dl0, dl1, dst = vjp((do_ref[...], tuple(dst_ref[h] for h in heads)))
        dq_ref[...] = dq
        df_ref[...] = df
        di_ref[...] = di
        dl0_ref[...] += dl0
        dl1_ref[...] += dl1
        for h in heads:
            dst_ref[h] = dst[h]

    row = lambda cb: pl.BlockSpec((tb, HG_W), lambda i: (blk(i), cb))
    vec = pl.BlockSpec((1, HG_W), lambda i: (0, 0))
    orow = pl.BlockSpec((tb, HG_W), lambda i: (blk(i), 0))
    return pl.pallas_call(
        body, name=name, grid=(nb,),
        in_specs=[row(0), row(fcol), row(3), vec, vec,
                  pl.BlockSpec((1, HG_HEADS, HG_D, HG_D), lambda i: (blk(i), 0, 0, 0)), orow],
        out_specs=[orow, orow, orow, vec, vec],
        out_shape=[jax.ShapeDtypeStruct((t, HG_W), F32)] * 3 + [jax.ShapeDtypeStruct((1, HG_W), F32)] * 2,
        scratch_shapes=[pltpu.VMEM((HG_HEADS, HG_D, HG_D), F32)],
        compiler_params=_params(("arbitrary",)),
    )(u, u, u, l0, l1, ss, do)


def _rope_tables(t):
    rows = t // GRID_W
    row = jnp.repeat(jnp.arange(rows), GRID_W).astype(F32)
    col = jnp.tile(jnp.arange(GRID_W), rows).astype(F32)
    axis_dim = ATT_DH // 2
    freqs = ROPE_THETA ** (-jnp.arange(0, axis_dim, 2, dtype=F32) / axis_dim)
    ang = jnp.concatenate([row[:, None] * freqs, col[:, None] * freqs], axis=-1)
    cos2 = jnp.repeat(jnp.cos(ang), 2, axis=-1)
    sin2 = jnp.repeat(jnp.sin(ang), 2, axis=-1) * jnp.tile(jnp.array([-1.0, 1.0], F32), ATT_DH // 2)
    return cos2, sin2


def _group_sum_matrix(width):
    idx = np.arange(width) // ATT_DH
    return jnp.asarray((idx[:, None] == idx[None, :]).astype(np.float32))


def _tile_matrix(width):
    m = np.zeros((LANES, width), np.float32)
    m[np.arange(width) % ATT_DH, np.arange(width)] = 1.0
    return jnp.asarray(m)


def _tile_w(w128, tile_m):
    w8 = jnp.broadcast_to(w128, (8, LANES))
    return jnp.sum(xdot_r(w8, tile_m), axis=0, keepdims=True) * 0.125


def _head_norm_rope(a, w128, cos_t, sin_t, gsum, tile_m, scale):
    ssq = xdot_r(a * a, gsum)
    y = a * lax.rsqrt(ssq * (1.0 / ATT_DH) + EPS) * _tile_w(w128, tile_m)
    return (y * cos_t + swap_pairs(y) * sin_t) * scale


def _att_prep_fn(aq, ak, cq, sq, ck, sk, qw, kw, gq, gk, tq, tk):
    q = _head_norm_rope(aq, qw, cq, sq, gq, tq, ATT_DH ** -0.5)
    k = _head_norm_rope(ak, kw, ck, sk, gk, tk, 1.0)
    return q, k


def _head_t(x, heads):
    xt = x.T
    return [xt[h * ATT_DH:(h + 1) * ATT_DH] for h in range(heads)]


def _head_s(x, heads):
    lane = lax.broadcasted_iota(jnp.int32, (x.shape[0], LANES), 1)
    out = []
    for h in range(heads):
        pair = x[:, (h // 2) * LANES:(h // 2 + 1) * LANES]
        if h % 2:
            pair = pltpu.roll(pair, ATT_DH, 1)
        out.append(jnp.where(lane < ATT_DH, pair, 0.0))
    return out


def _from_head_t(tile):
    return jnp.concatenate([tile[h, 0] for h in range(tile.shape[0])], axis=0).T


def _pad_rows(a):
    return jnp.concatenate([a, jnp.zeros(a.shape, a.dtype)], axis=0)


def _col_bcast(row_vec):
    return jnp.broadcast_to(row_vec, (LANES, row_vec.shape[1])).T


FA_K_FWD = 1024
FA_K_BWD = 512
FA_STRIPS_FWD = 4
FA_STRIPS_BWD = 8


def _key_block(ref, j, tiles):
    return jnp.concatenate([ref[0, j * tiles + i] for i in range(tiles)], axis=1)


def _riding(body, n_in, n_out, grid, ride):
    if ride is None:
        return body, [], [], [], [], []
    arrays, reduce = ride
    n = len(arrays)
    steps = grid[0] * grid[1]

    def riding_body(*refs):
        cuts = np.cumsum([0, n_in, n, n_out, n]).tolist()
        own_in, rin, own_out, rout = (refs[a:b] for a, b in zip(cuts[:-1], cuts[1:]))
        sems = refs[cuts[-1]:]
        start, forward, finish = _exchange_phases(rin, rout, *sems, reduce)
        step = pl.program_id(0) * grid[1] + pl.program_id(1)
        pl.when(step == 0)(start)
        pl.when(step == steps // 2)(forward)
        body(*own_in, *own_out)
        pl.when(step == steps - 1)(finish)

    return (riding_body, _hbm_specs(n), _hbm_specs(n), _exchange_shapes(arrays, reduce), _exchange_sems(n), list(arrays))


def fa_fwd(qt, ks, vt, ride=None):
    _, ns, dh, nq = qt.shape
    lk = ks.shape[1]
    bk = min(FA_K_FWD, lk)
    nk = lk // bk
    spg = min(FA_STRIPS_FWD, ns)
    grid = (ATT_KV, ns // spg)

    def body(q_ref, k_ref, v_ref, o_ref, lse_ref):
        qs = [_pad_rows(q_ref[0, c]) for c in range(spg)]

        def keys(j):
            return k_ref[0, pl.ds(pl.multiple_of(j * bk, bk), bk), :]

        def step(j, carry):
            st0, stats = carry
            kb = keys(j)
            vb = _key_block(v_ref, j, bk // nq)
            sts = [st0] + [_dg(kb, qs[c], 1, 0) for c in range(1, spg)]
            out = []
            for c in range(spg):
                m, l, acc = stats[c]
                m_new = jnp.maximum(m, jnp.max(sts[c], axis=0, keepdims=True))
                alpha = jnp.exp(m - m_new)
                p = jnp.exp(sts[c] - m_new)
                l = alpha * l + jnp.sum(p, axis=0, keepdims=True)
                if c == spg - 1:
                    st0 = _dg(keys(jnp.minimum(j + 1, nk - 1)), qs[0], 1, 0)
                acc = alpha * acc + _dg(vb, p, 1, 0)
                out.append((m_new, l, acc))
            return st0, tuple(out)

        init = tuple((jnp.full((1, nq), -jnp.inf, F32), jnp.zeros((1, nq), F32), jnp.zeros((dh, nq), F32))
                     for _ in range(spg))
        _, res = lax.fori_loop(0, nk, step, (_dg(keys(0), qs[0], 1, 0), init))
        for c in range(spg):
            m, l, acc = res[c]
            o_ref[0, c] = acc / l
            lse_ref[0, c] = _col_bcast(m + jnp.log(l))

    body, r_in, r_out, r_shapes, r_sems, r_args = _riding(body, 3, 2, grid, ride)
    return pl.pallas_call(
        body, name="fa_fwd", grid=grid,
        in_specs=[pl.BlockSpec((1, spg, dh, nq), lambda g, i: (g, i, 0, 0)),
                  pl.BlockSpec((1, lk, LANES), lambda g, i: (g, 0, 0)),
                  pl.BlockSpec((1, lk // nq, dh, nq), lambda g, i: (g, 0, 0, 0))] + r_in,
        out_specs=[pl.BlockSpec((1, spg, dh, nq), lambda g, i: (g, i, 0, 0)),
                   pl.BlockSpec((1, spg, nq, LANES), lambda g, i: (g, i, 0, 0))] + r_out,
        out_shape=[jax.ShapeDtypeStruct((ATT_KV, ns, dh, nq), F32),
                   jax.ShapeDtypeStruct((ATT_KV, ns, nq, LANES), F32)] + r_shapes,
        scratch_shapes=r_sems,
        compiler_params=_params(("arbitrary", "arbitrary")),
    )(qt, ks, vt, *r_args)


def fa_bwd(qs, qt, dos, dot_, ot, lse, ks, kt, vt, ride=None):
    _, ns, dh, nq = qt.shape
    lk = ks.shape[1]
    bk = min(FA_K_BWD, lk)
    nk = lk // bk
    tiles = bk // nq
    spg = min(FA_STRIPS_BWD, ns)
    nc = bk // LANES
    grid = (ATT_KV, ns // spg)

    def body(qs_ref, qt_ref, dos_ref, dot_ref, ot_ref, lse_ref, ks_ref, kt_ref, vt_ref, dq_ref, dk_ref, dv_ref):
        @pl.when(pl.program_id(1) == 0)
        def _():
            dk_ref[...] = jnp.zeros(dk_ref.shape, F32)
            dv_ref[...] = jnp.zeros(dv_ref.shape, F32)

        strips = range(spg)
        lse_b = [lse_ref[0, c] for c in strips]
        d_b = [_col_bcast(jnp.sum(dot_ref[0, c].astype(F32) * ot_ref[0, c], axis=0, keepdims=True)) for c in strips]

        def step(j, dqs):
            ktb, vtb = _pad_rows(_key_block(kt_ref, j, tiles)), _pad_rows(_key_block(vt_ref, j, tiles))
            kb = ks_ref[0, pl.ds(pl.multiple_of(j * bk, bk), bk), :]
            prods = [(_dg(qs_ref[0, c], ktb, 1, 0), _dg(dos_ref[0, c], vtb, 1, 0)) for c in strips]
            out = []
            for c in strips:
                s, dp = prods[c]
                ps, dss = [], []
                for cc in range(nc):
                    sl = slice(cc * LANES, (cc + 1) * LANES)
                    pc = jnp.exp(s[:, sl] - lse_b[c])
                    ps.append(pc.astype(BF16))
                    dss.append((pc * (dp[:, sl] - d_b[c])).astype(BF16))
                p, ds = jnp.concatenate(ps, axis=1), jnp.concatenate(dss, axis=1)
                dv = _dg(dot_ref[0, c], p, 1, 0)
                dk = _dg(qt_ref[0, c], ds, 1, 0)
                for i in range(tiles):
                    dv_ref[0, j * tiles + i] += dv[:, i * nq:(i + 1) * nq]
                    dk_ref[0, j * tiles + i] += dk[:, i * nq:(i + 1) * nq]
                out.append(dqs[c] + _dg(ds, kb, 1, 0))
            return tuple(out)

        dqs = lax.fori_loop(0, nk, step, tuple(jnp.zeros((nq, LANES), F32) for _ in strips))
        for c in strips:
            dq_ref[0, c] = dqs[c].T[:dh]

    sspec = pl.BlockSpec((1, spg, nq, LANES), lambda g, i: (g, i, 0, 0))
    tspec = pl.BlockSpec((1, spg, dh, nq), lambda g, i: (g, i, 0, 0))
    kspec = pl.BlockSpec((1, lk // nq, dh, nq), lambda g, i: (g, 0, 0, 0))
    body, r_in, r_out, r_shapes, r_sems, r_args = _riding(body, 9, 3, grid, ride)
    return pl.pallas_call(
        body, name="fa_bwd", grid=grid,
        in_specs=[sspec, tspec, sspec, tspec, tspec, sspec, pl.BlockSpec((1, lk, LANES), lambda g, i: (g, 0, 0)),
                  kspec, kspec] + r_in,
        out_specs=[tspec, kspec, kspec] + r_out,
        out_shape=[jax.ShapeDtypeStruct((ATT_KV, ns, dh, nq), F32),
                   jax.ShapeDtypeStruct((ATT_KV, lk // nq, dh, nq), F32),
                   jax.ShapeDtypeStruct((ATT_KV, lk // nq, dh, nq), F32)] + r_shapes,
        scratch_shapes=r_sems,
        compiler_params=_params(("arbitrary", "arbitrary")),
    )(qs, qt, dos, dot_, ot, lse, ks, kt, vt, *r_args)


def _post_mix_fn(of, ob, ug, oa, hgw, attw):
    o = of + ob
    parts = []
    for h in range(HG_HEADS):
        parts.append(_rms(o[:, h * HG_D:(h + 1) * HG_D], hgw))
    hg = jnp.concatenate(parts, axis=1) * (ug * _sigmoid(ug))
    return jnp.concatenate([hg, _rms(oa, attw)], axis=1)


def _swiglu_fn(gate, up):
    return gate * _sigmoid(gate) * up


def _loss_fn(x2, w, tgt):
    e = _rms(x2, w) - tgt
    return 0.5 * jnp.sum(jnp.mean(e * e, axis=-1, keepdims=True), axis=0, keepdims=True)


def _place():
    return lax.axis_index("x"), lax.axis_index("y"), lax.axis_index("c")


def _other_chips(x, y):
    return [(1 - x, y), (x, 1 - y), (1 - x, 1 - y)]


def _hbm_specs(n):
    return [pl.BlockSpec(memory_space=pl.ANY)] * n


SEMS_PER_ARRAY = 7


def _exchange_sems(n):
    return [pltpu.SemaphoreType.DMA((SEMS_PER_ARRAY * n,)), pltpu.SemaphoreType.DMA((SEMS_PER_ARRAY * n,)),
            pltpu.SemaphoreType.DMA((n,))]


def _exchange_phases(srcs, outs, ssem, rsem, lsem, reduce):
    n = len(srcs)
    x, y, c = _place()
    k = 2 * x + y
    sib = (x, y, 1 - c)
    chips = _other_chips(x, y)
    pairs = [(a, j) for a in range(n) for j in range(3)]

    def hrows(a):
        return srcs[a].shape[1] if reduce else srcs[a].shape[0] // 2

    def half(a, kk, cc):
        return outs[a].at[kk, pl.ds(cc * hrows(a), hrows(a)), :]

    def mine(a, kk):
        return srcs[a].at[kk] if reduce else srcs[a].at[pl.ds(c * hrows(a), hrows(a)), :]

    def copy(a, j, src_ref, dst_ref, to):
        return pltpu.make_async_remote_copy(src_ref=src_ref, dst_ref=dst_ref, send_sem=ssem.at[SEMS_PER_ARRAY * a + j],
                                            recv_sem=rsem.at[SEMS_PER_ARRAY * a + j], device_id=to, device_id_type=MESH)

    def local(a):
        if reduce:
            return pltpu.make_async_copy(srcs[a].at[k], half(a, k, c), lsem.at[a])
        return pltpu.make_async_copy(srcs[a], outs[a].at[k], lsem.at[a])

    def ici(a, j, arriving):
        px, py = chips[j]
        kk = 2 * px + py
        if arriving:
            return copy(a, j, mine(a, k), half(a, kk, c), (px, py, c))
        return copy(a, j, mine(a, kk), half(a, k, c), (px, py, c))

    def passed(a, j, arriving):
        px, py = chips[j]
        kk = 2 * px + py
        return copy(a, 3 + j, half(a, kk, c), half(a, kk, (1 - c) if arriving else c), sib)

    def own(a, arriving):
        return copy(a, 6, mine(a, k), half(a, k, (1 - c) if arriving else c), sib)

    def start():
        for a in range(n):
            local(a).start()
        for a, j in pairs:
            ici(a, j, False).start()
        if reduce:
            for a in range(n):
                own(a, False).start()

    def forward():
        for a, j in pairs:
            ici(a, j, True).wait_recv()
            passed(a, j, False).start()

    def finish():
        for a in range(n):
            if reduce:
                own(a, True).wait_recv()
            for j in range(3):
                passed(a, j, True).wait_recv()
        for a, j in pairs:
            ici(a, j, False).wait_send()
            passed(a, j, False).wait_send()
        for a in range(n):
            if reduce:
                own(a, False).wait_send()
            local(a).wait()

    return start, forward, finish


def _exchange_shapes(arrays, reduce):
    if reduce:
        return [jax.ShapeDtypeStruct((N_CHIPS, 2 * p.shape[1], p.shape[2]), p.dtype) for p in arrays]
    return [jax.ShapeDtypeStruct((N_CHIPS,) + s.shape, s.dtype) for s in arrays]


def exchange(arrays, reduce, name):
    n = len(arrays)

    def body(*refs):
        start, forward, finish = _exchange_phases(refs[:n], refs[n:2 * n], *refs[2 * n:], reduce)
        start()
        forward()
        finish()

    return pl.pallas_call(
        body, name=name, in_specs=_hbm_specs(n), out_specs=_hbm_specs(n),
        out_shape=_exchange_shapes(arrays, reduce), scratch_shapes=_exchange_sems(n),
    )(*arrays)


def rs_siblings(gs, name):
    n = len(gs)

    def body(*refs):
        srcs, outs = refs[:n], refs[n:2 * n]
        ssem, rsem = refs[2 * n:]
        x, y, c = _place()
        cps = []
        for a in range(n):
            hr = srcs[a].shape[1] // 2
            cp = pltpu.make_async_remote_copy(src_ref=srcs[a].at[:, pl.ds((1 - c) * hr, hr), :], dst_ref=outs[a],
                                              send_sem=ssem.at[a], recv_sem=rsem.at[a], device_id=(x, y, 1 - c),
                                              device_id_type=MESH)
            cp.start()
            cps.append(cp)
        for cp in cps:
            cp.wait()

    return pl.pallas_call(
        body, name=name, in_specs=_hbm_specs(n), out_specs=_hbm_specs(n),
        out_shape=[jax.ShapeDtypeStruct((N_CHIPS, g.shape[1] // 2, g.shape[2]), F32) for g in gs],
        scratch_shapes=[pltpu.SemaphoreType.DMA((n,)), pltpu.SemaphoreType.DMA((n,))],
    )(*gs)


def allreduce_small(p, name):
    rows, width = p.shape

    def body(p_ref, s_ref, gath, ssem, rsem):
        x, y, c = _place()
        me = 4 * x + 2 * y + c
        copies = []
        for d in range(1, N_DEV):
            dx, dy, dc = (d >> 2) & 1, (d >> 1) & 1, d & 1
            peer = (x ^ dx, y ^ dy, c ^ dc)
            cp = pltpu.make_async_remote_copy(src_ref=p_ref, dst_ref=gath.at[me], send_sem=ssem.at[d - 1],
                                              recv_sem=rsem.at[d - 1], device_id=peer, device_id_type=MESH)
            cp.start()
            copies.append(cp)
        gath[me] = p_ref[...]
        for d, cp in enumerate(copies, start=1):
            dx, dy, dc = (d >> 2) & 1, (d >> 1) & 1, d & 1
            peer_slot = 4 * (x ^ dx) + 2 * (y ^ dy) + (c ^ dc)
            pltpu.make_async_remote_copy(src_ref=p_ref, dst_ref=gath.at[peer_slot], send_sem=ssem.at[d - 1],
                                         recv_sem=rsem.at[d - 1], device_id=(x ^ dx, y ^ dy, c ^ dc),
                                         device_id_type=MESH).wait_recv()
        for cp in copies:
            cp.wait_send()
        acc = gath[0]
        for d in range(1, N_DEV):
            acc = acc + gath[d]
        s_ref[...] = acc

    return pl.pallas_call(
        body, name=name,
        in_specs=[pl.BlockSpec(memory_space=pltpu.VMEM)],
        out_specs=pl.BlockSpec(memory_space=pltpu.VMEM),
        out_shape=jax.ShapeDtypeStruct((rows, width), F32),
        scratch_shapes=[pltpu.VMEM((N_DEV, rows, width), F32), pltpu.SemaphoreType.DMA((N_DEV - 1,)),
                        pltpu.SemaphoreType.DMA((N_DEV - 1,))],
    )(p)


def add_my_half(g, recv, name):
    _, r, cols = g.shape
    hr = r // 2
    tb = _pick(hr, 512, 8)
    nb = hr // tb
    c_arr = lax.axis_index("c").astype(jnp.int32).reshape(1)

    def body(c_ref, g_ref, r_ref, o_ref):
        o_ref[...] = (g_ref[...] + r_ref[...]).astype(o_ref.dtype)

    return pl.pallas_call(
        body, name=name,
        grid_spec=pltpu.PrefetchScalarGridSpec(
            num_scalar_prefetch=1, grid=(N_CHIPS, nb),
            in_specs=[pl.BlockSpec((1, tb, cols), lambda k, i, c_ref: (k, c_ref[0] * nb + i, 0)),
                      pl.BlockSpec((1, tb, cols), lambda k, i, c_ref: (k, i, 0))],
            out_specs=pl.BlockSpec((1, tb, cols), lambda k, i, c_ref: (k, i, 0))),
        out_shape=jax.ShapeDtypeStruct((N_CHIPS, hr, cols), BF16),
        compiler_params=_params(("arbitrary", "arbitrary")),
    )(c_arr, g, recv)


def sum_chips(parts, name):
    _, hr, cols = parts.shape
    tb = _pick(hr, 512, 8)

    def body(p_ref, o_ref):
        o_ref[...] = ((p_ref[0].astype(F32) + p_ref[1].astype(F32)) + p_ref[2].astype(F32)) + p_ref[3].astype(F32)

    return pl.pallas_call(
        body, name=name, grid=(hr // tb,),
        in_specs=[pl.BlockSpec((N_CHIPS, tb, cols), lambda i: (0, i, 0))],
        out_specs=pl.BlockSpec((tb, cols), lambda i: (i, 0)),
        out_shape=jax.ShapeDtypeStruct((hr, cols), F32),
        compiler_params=_params(("arbitrary",)),
    )(parts)


def adamw(w, g, m, v, name):
    rows, width = w.shape
    tb = _pick(rows, 512, 8)

    def body(w_ref, g_ref, m_ref, v_ref, d_ref, mo_ref, vo_ref):
        gg = g_ref[...]
        m_new = ADAM_B1 * m_ref[...] + (1.0 - ADAM_B1) * gg
        v_new = ADAM_B2 * v_ref[...] + (1.0 - ADAM_B2) * (gg * gg)
        m_hat = m_new / (1.0 - ADAM_B1 ** ADAM_STEP)
        v_hat = v_new / (1.0 - ADAM_B2 ** ADAM_STEP)
        d_ref[...] = -ADAM_LR * (m_hat / (jnp.sqrt(v_hat) + ADAM_EPS) + ADAM_WD * w_ref[...])
        mo_ref[...] = m_new
        vo_ref[...] = v_new

    spec = pl.BlockSpec((tb, width), lambda i: (i, 0))
    return pl.pallas_call(
        body, name=name, grid=(rows // tb,), in_specs=[spec] * 4, out_specs=[spec] * 3,
        out_shape=[jax.ShapeDtypeStruct((rows, width), F32)] * 3,
        compiler_params=_params(("arbitrary",)),
    )(w, g, m, v)


def _pack_rows(vecs, width=1024):
    rows, cur, used = [], [], 0
    for v in vecs:
        n = v.shape[1]
        if used + n > width:
            cur.append(jnp.zeros((1, width - used), F32))
            rows.append(jnp.concatenate(cur, axis=1))
            cur, used = [], 0
        cur.append(v)
        used += n
    cur.append(jnp.zeros((1, width - used), F32))
    rows.append(jnp.concatenate(cur, axis=1))
    return rows


def _pad128(v):
    return jnp.pad(v, ((0, 0), (0, LANES - v.shape[1])))


def kernel(x, norm1_w, w_in, lb_logits, hg_norm_w, q_norm_w, k_norm_w, att_norm_w, w_out, norm2_w, w_gate_up, w_down, final_norm_w, loss_target, m_norm1_w, m_w_in, m_lb_logits, m_hg_norm_w, m_q_norm_w, m_k_norm_w, m_att_norm_w, m_w_out, m_norm2_w, m_w_gate_up, m_w_down, m_final_norm_w, v_norm1_w, v_w_in, v_lb_logits, v_hg_norm_w, v_q_norm_w, v_k_norm_w, v_att_norm_w, v_w_out, v_norm2_w, v_w_gate_up, v_w_down, v_final_norm_w):
    t, d = x.shape[1], x.shape[2]
    xi, yi, ci = _place()
    chip = 2 * xi + yi
    x2d = x.reshape(t, d)
    tgt = loss_target.reshape(t, d)
    tok = min(TOK, t)
    nt = t // tok
    ns = ATT_GROUP * nt

    (g_in,) = exchange([w_in[0].astype(BF16)], False, "allgather_w_in")
    wf_in = g_in.transpose(1, 0, 2).reshape(g_in.shape[1], -1)
    late_w = [w_out[0].astype(BF16), w_gate_up[0].astype(BF16), w_down[0].astype(BF16)]
    lb_rows = lb_logits.reshape(4, LANES) * (ci == 0).astype(F32)
    lb_pad = lax.dynamic_update_slice(jnp.zeros((8, 1024), F32), lb_rows, (0, chip * LANES))
    lb_full = allreduce_small(lb_pad, "gather_lb")[:4, :HG_W]
    l_f0, l_f1, l_b0, l_b1 = (lb_full[i:i + 1] for i in range(4))

    n1 = norm1_w.reshape(1, d)
    n2 = norm2_w.reshape(1, d)
    nf = final_norm_w.reshape(1, d)
    tm = min(512, t)
    (h1,) = _rows(lambda a, w: ((_rms(a, w),), ()), "norm1", t, tm, [_rin(x2d, tm)], [n1], [_rout(t, tm, d, BF16)])
    u = mm_rows(h1, wf_in, "mm_in")
    o_f, ss_f = gla_fwd(u, l_f0, l_f1, 1, False, "gla_fwd_f")
    o_b, ss_b = gla_fwd(u, l_b0, l_b1, 2, True, "gla_fwd_b")

    cos2, sin2 = _rope_tables(t)
    cq, sq = jnp.tile(cos2, (1, ATT_HEADS)), jnp.tile(sin2, (1, ATT_HEADS))
    ck, sk = jnp.tile(cos2, (1, ATT_KV)), jnp.tile(sin2, (1, ATT_KV))
    qw, kw = _pad128(q_norm_w.reshape(1, ATT_DH)), _pad128(k_norm_w.reshape(1, ATT_DH))
    gq, gk = _group_sum_matrix(ATT_QW), _group_sum_matrix(ATT_KVW)
    tq, tk = _tile_matrix(ATT_QW), _tile_matrix(ATT_KVW)
    prep_in = [_rin(u, tok, ATT_QW, 5), _rin(u, tok, ATT_KVW, 24), _rin(cq, tok), _rin(sq, tok), _rin(ck, tok),
               _rin(sk, tok)]
    prep_consts = [qw, kw, gq, gk, tq, tk]

    def att_prep_fn(aq, ak, av, *rest):
        q, k = _att_prep_fn(aq, ak, *rest)
        return (_head_t(q, ATT_HEADS), _head_s(q, ATT_HEADS), _head_s(k, ATT_KV), _head_t(k, ATT_KV),
                _head_t(av, ATT_KV)), ()

    q_t, q_s, k_s, k_t, v_t = _rows(
        att_prep_fn, "att_prep", t, tok, prep_in[:2] + [_rin(u, tok, ATT_KVW, 25)] + prep_in[2:], prep_consts,
        [_tout(ATT_HEADS, nt, ATT_DH, tok, BF16), _tout(ATT_HEADS, nt, tok, LANES, BF16),
         _tout(ATT_KV, nt, tok, LANES, BF16), _tout(ATT_KV, nt, ATT_DH, tok, BF16),
         _tout(ATT_KV, nt, ATT_DH, tok, BF16)])
    q_t = q_t.reshape(ATT_KV, ns, ATT_DH, tok)
    q_s = q_s.reshape(ATT_KV, ns, tok, LANES)
    k_s = k_s.reshape(ATT_KV, t, LANES)
    o_t, lse, g_out, g_gu, g_down = fa_fwd(q_t, k_s, v_t, ride=(late_w, False))
    wf_gu = g_gu.transpose(1, 0, 2).reshape(g_gu.shape[1], -1)
    wf_out = g_out.reshape(-1, g_out.shape[2])
    wf_down = g_down.reshape(-1, g_down.shape[2])
    o_tiles = o_t.reshape(ATT_HEADS, nt, ATT_DH, tok)

    hgw = hg_norm_w.reshape(1, HG_D)
    attw = att_norm_w.reshape(1, ATT_QW)
    mix_in = [_rin(o_f, tok), _rin(o_b, tok), _rin(u, tok, HG_W, 4), _tin(o_tiles)]
    (mix,) = _rows(lambda of, ob, ug, ot, hw, aw: ((_post_mix_fn(of, ob, ug, _from_head_t(ot), hw, aw),), ()),
                   "post_mix", t, tok, mix_in, [hgw, attw], [_rout(t, tok, d, BF16)])
    x1, h2 = mm_res_norm(mix, wf_out, x2d, n2, "mm_out")
    gu2, act = ffn_up(h2, wf_gu)

    dx2, dx2b, g_final, loss_part = mm_res_loss(act, wf_down, x1, tgt, nf, "mm_down_loss")
    dgu2 = ffn_down_dx(dx2b, wf_down, gu2)
    gw_down = mm_tn(dx2b, act, "mm_down_dw", transpose_out=True)
    dx1, dx1b, g_norm2 = mm_nt_norm_bwd(dgu2, wf_gu, x1, n2, dx2, "mm_gate_up_dx", True)
    gw_gu = mm_tn(h2, dgu2, "mm_gate_up_dw", col_shards=N_CHIPS)
    dmix = mm_rows(dx1b, wf_out, "mm_out_dx", trans_b=True)
    gw_out = mm_tn(mix, dx1b, "mm_out_dw")

    def post_mix_bwd_fn(of, ob, ug, ot, dm, hgw_, attw_):
        _, vjp = jax.vjp(_post_mix_fn, of, ob, ug, _from_head_t(ot), hgw_, attw_)
        dof, _, dug, doa, dhgw, dattw = vjp(dm)
        return (dof, dug, _head_t(doa, ATT_HEADS), _head_s(doa, ATT_HEADS)), (dhgw, dattw)

    do_hg, du_g, do_t, do_s, g_hg, g_att = _rows(
        post_mix_bwd_fn, "post_mix_bwd", t, tok, mix_in + [_rin(dmix, tok)], [hgw, attw],
        [_rout(t, tok, HG_W, F32), _rout(t, tok, HG_W, BF16), _tout(ATT_HEADS, nt, ATT_DH, tok, BF16),
         _tout(ATT_HEADS, nt, tok, LANES, BF16)], [HG_D, ATT_QW])
    late = ["w_out", "w_gate_up", "w_down"]
    late_g = [gw_out.reshape(N_CHIPS, -1, d), gw_gu, gw_down.reshape(N_CHIPS, -1, d)]
    late_part = [add_my_half(g, r, "add_my_half_" + n) for g, r, n in zip(late_g, rs_siblings(late_g, "rs_siblings_late"), late)]
    dq_t, dk_t, dv_t, *late_parts = fa_bwd(q_s, q_t, do_s.reshape(q_s.shape), do_t.reshape(q_t.shape), o_t, lse, k_s, k_t,
                                           v_t, ride=(late_part, True))

    def att_prep_bwd_fn(aq, ak, cq_, sq_, ck_, sk_, dqt, dkt, dvt, qw_, kw_, gq_, gk_, tq_, tk_):
        _, vjp = jax.vjp(lambda a, b, c_, e: _att_prep_fn(a, b, cq_, sq_, ck_, sk_, c_, e, gq_, gk_, tq_, tk_),
                         aq, ak, qw_, kw_)
        daq, dak, dqw, dkw = vjp((_from_head_t(dqt), _from_head_t(dkt)))
        return (daq, dak, _from_head_t(dvt)), (dqw, dkw)

    da_q, da_k, da_v, g_q, g_k = _rows(
        att_prep_bwd_fn, "att_prep_bwd", t, tok,
        prep_in + [_tin(dq_t.reshape(ATT_HEADS, nt, ATT_DH, tok)), _tin(dk_t), _tin(dv_t)], prep_consts,
        [_rout(t, tok, ATT_QW, BF16), _rout(t, tok, ATT_KVW, BF16), _rout(t, tok, ATT_KVW, BF16)], [LANES, LANES])

    dq_f, df_f, di_f, dl_f0, dl_f1 = gla_bwd(u, l_f0, l_f1, ss_f, do_hg, 1, False, "gla_bwd_f")
    dq_b, df_b, di_b, dl_b0, dl_b1 = gla_bwd(u, l_b0, l_b1, ss_b, do_hg, 2, True, "gla_bwd_b")

    def assemble_fn(qf, qb, ff, fb, i_f, i_b, dg, aq, ak, av):
        parts = [qf + qb, ff, fb, i_f + i_b, dg.astype(F32), aq.astype(F32), ak.astype(F32), av.astype(F32)]
        return (jnp.concatenate(parts, axis=1),), ()

    (du,) = _rows(assemble_fn, "assemble_du", t, tok,
                  [_rin(a, tok) for a in (dq_f, dq_b, df_f, df_b, di_f, di_b, du_g, da_q, da_k, da_v)], [],
                  [_rout(t, tok, u.shape[1], BF16)])
    grad_x, g_norm1 = mm_nt_norm_bwd(du, wf_in, x2d, n1, dx1, "mm_in_dx", False)
    gw_in_t = mm_tn(h1, du, "mm_in_dw", transpose_out=True)

    names = ["w_in"] + late
    g_in4 = gw_in_t.reshape(N_CHIPS, -1, d)
    in_part = add_my_half(g_in4, rs_siblings([g_in4], "rs_siblings_w_in")[0], "add_my_half_w_in")
    parts = list(exchange([in_part], True, "rs_chips_w_in")) + late_parts
    g_shard = [sum_chips(p, "sum_chips_" + n) for p, n in zip(parts, names)]
    g_shard[0] = g_shard[0].T
    big = {}
    for n, g, w, m, v in zip(names, g_shard, (w_in, w_out, w_gate_up, w_down), (m_w_in, m_w_out, m_w_gate_up, m_w_down),
                             (v_w_in, v_w_out, v_w_gate_up, v_w_down)):
        dlt, mn, vn = adamw(w[0], g, m[0], v[0], "adamw_" + n)
        big[n] = (g[None], dlt[None], mn[None], vn[None])

    small = [g_norm1, g_norm2, g_final, g_att, g_hg, g_q, g_k, loss_part, dl_f0, dl_f1, dl_b0, dl_b1]
    packed = _pack_rows(small)
    packed += [jnp.zeros((1, 1024), F32)] * (8 - len(packed))
    tot = allreduce_small(jnp.concatenate(packed, axis=0), "allreduce_small")
    s_norm1, s_norm2, s_final = tot[0:1], tot[1:2], tot[2:3]
    s_att, s_hg, s_q, s_k = tot[3:4, 0:512], tot[3:4, 512:640], tot[3:4, 640:704], tot[3:4, 768:832]
    loss = tot[3, 896]
    s_lb = jnp.concatenate([tot[4:5, 0:512], tot[4:5, 512:1024], tot[5:6, 0:512], tot[5:6, 512:1024]], axis=0)
    s_lb = lax.dynamic_slice(s_lb, (0, chip * LANES), (4, LANES)).reshape(1, 512)

    snames = ["norm1_w", "lb_logits", "hg_norm_w", "q_norm_w", "k_norm_w", "att_norm_w", "norm2_w", "final_norm_w"]
    g_small = dict(zip(snames, [s_norm1, s_lb, s_hg, s_q, s_k, s_att, s_norm2, s_final]))
    w_small = dict(zip(snames, [norm1_w, lb_logits, hg_norm_w, q_norm_w, k_norm_w, att_norm_w, norm2_w, final_norm_w]))
    m_small = dict(zip(snames, [m_norm1_w, m_lb_logits, m_hg_norm_w, m_q_norm_w, m_k_norm_w, m_att_norm_w, m_norm2_w, m_final_norm_w]))
    v_small = dict(zip(snames, [v_norm1_w, v_lb_logits, v_hg_norm_w, v_q_norm_w, v_k_norm_w, v_att_norm_w, v_norm2_w, v_final_norm_w]))

    def pack_small(tree):
        rows = _pack_rows([tree[n].reshape(1, -1) for n in snames])
        rows += [jnp.zeros((1, 1024), F32)] * (8 - len(rows))
        return jnp.concatenate(rows, axis=0)

    d_s, m_s, v_s = adamw(pack_small(w_small), pack_small(g_small), pack_small(m_small), pack_small(v_small), "adamw_small")

    def unpack_small(a):
        out, r, used = {}, 0, 0
        for n in snames:
            size = w_small[n].size
            if used + size > 1024:
                r, used = r + 1, 0
            out[n] = a[r, used:used + size].reshape(w_small[n].shape)
            used += size
        return out

    d_sm, m_sm, v_sm = unpack_small(d_s), unpack_small(m_s), unpack_small(v_s)
    g_sm = {n: g_small[n].reshape(w_small[n].shape) for n in snames}

    order = ["norm1_w", "w_in", "lb_logits", "hg_norm_w", "q_norm_w", "k_norm_w", "att_norm_w", "w_out", "norm2_w",
             "w_gate_up", "w_down", "final_norm_w"]

    def pick(small_tree, idx):
        return [big[n][idx] if n in big else small_tree[n] for n in order]

    return (loss, grad_x.reshape(x.shape), *pick(g_sm, 0), *pick(d_sm, 1), *pick(m_sm, 2), *pick(v_sm, 3))
```

```python
import functools

import numpy as np
import jax
import jax.numpy as jnp
from jax import lax
from jax.experimental import pallas as pl
from jax.experimental.pallas import tpu as pltpu

F32 = jnp.float32
BF16 = jnp.bfloat16
MESH = pl.DeviceIdType.MESH

EPS = 1e-6
GRID_W = 64
HG_HEADS = 4
HG_D = 128
HG_W = HG_HEADS * HG_D
CHUNK = 64
ATT_HEADS = 8
ATT_KV = 2
ATT_GROUP = ATT_HEADS // ATT_KV
ATT_DH = 64
ATT_QW = ATT_HEADS * ATT_DH
ATT_KVW = ATT_KV * ATT_DH
ROPE_THETA = 10000.0
N_CHIPS = 4
N_DEV = 8

ADAM_LR = 0.001
ADAM_B1 = 0.9
ADAM_B2 = 0.999
ADAM_EPS = 1e-08
ADAM_WD = 0.01
ADAM_STEP = 10

VMEM_LIMIT = 52 * 1024 * 1024
LANES = 128
TOK = 256


def _params(sem=None):
    return pltpu.CompilerParams(dimension_semantics=sem, vmem_limit_bytes=VMEM_LIMIT)


def _dg(a, b, ca, cb):
    return lax.dot_general(a.astype(BF16), b.astype(BF16), (((ca,), (cb,)), ((), ())),
                           preferred_element_type=F32)


@jax.custom_vjp
def dot_nn(a, b):
    return _dg(a, b, 1, 0)


def _dot_nn_fwd(a, b):
    return _dg(a, b, 1, 0), (a, b)


def _dot_nn_bwd(res, g):
    a, b = res
    return _dg(g, b, 1, 1), _dg(a, g, 0, 0)


dot_nn.defvjp(_dot_nn_fwd, _dot_nn_bwd)


@jax.custom_vjp
def dot_nt(a, b):
    return _dg(a, b, 1, 1)


def _dot_nt_fwd(a, b):
    return _dg(a, b, 1, 1), (a, b)


def _dot_nt_bwd(res, g):
    a, b = res
    return _dg(g, b, 1, 0), _dg(g, a, 0, 0)


dot_nt.defvjp(_dot_nt_fwd, _dot_nt_bwd)


@jax.custom_vjp
def dot_tn(a, b):
    return _dg(a, b, 0, 0)


def _dot_tn_fwd(a, b):
    return _dg(a, b, 0, 0), (a, b)


def _dot_tn_bwd(res, g):
    a, b = res
    return _dg(b, g, 1, 1), _dg(a, g, 1, 0)


dot_tn.defvjp(_dot_tn_fwd, _dot_tn_bwd)


def _split3(a):
    hi = a.astype(BF16)
    r1 = a - hi.astype(F32)
    mid = r1.astype(BF16)
    lo = (r1 - mid.astype(F32)).astype(BF16)
    return lo, mid, hi


def _sum3(terms):
    lo, mid, hi = terms
    return (lo + mid) + hi


@jax.custom_vjp
def xdot_r(a, m):
    return _sum3([_dg(p, m, 1, 0) for p in _split3(a)])


def _xdot_r_fwd(a, m):
    return xdot_r(a, m), m


def _xdot_r_bwd(m, g):
    return _sum3([_dg(p, m, 1, 1) for p in _split3(g)]), jnp.zeros_like(m)


xdot_r.defvjp(_xdot_r_fwd, _xdot_r_bwd)


@jax.custom_vjp
def xdot_l(m, a):
    return _sum3([_dg(m, p, 1, 0) for p in _split3(a)])


def _xdot_l_fwd(m, a):
    return xdot_l(m, a), m


def _xdot_l_bwd(m, g):
    return jnp.zeros_like(m), _sum3([_dg(m, p, 0, 0) for p in _split3(g)])


xdot_l.defvjp(_xdot_l_fwd, _xdot_l_bwd)


@jax.custom_vjp
def swap_pairs(y):
    n = y.shape[-1]
    lane = lax.broadcasted_iota(jnp.int32, y.shape, 1)
    nxt = pltpu.roll(y, n - 1, 1)
    prv = pltpu.roll(y, 1, 1)
    return jnp.where(lane % 2 == 0, nxt, prv)


def _swap_fwd(y):
    return swap_pairs(y), None


def _swap_bwd(_, g):
    return (swap_pairs(g),)


swap_pairs.defvjp(_swap_fwd, _swap_bwd)


def _rms(x, w):
    return x * lax.rsqrt(jnp.mean(x * x, axis=-1, keepdims=True) + EPS) * w


def _sigmoid(x):
    return jax.nn.sigmoid(x)


def _rows(fn, name, t, tm, ins, consts, outs, accs=()):
    n_r, n_c, n_o, n_a = len(ins), len(consts), len(outs), len(accs)

    def body(*refs):
        r = refs[:n_r]
        c = refs[n_r:n_r + n_c]
        o = refs[n_r + n_c:n_r + n_c + n_o]
        a = refs[n_r + n_c + n_o:]
        ro, ao = fn(*[x[...] for x in r], *[x[...] for x in c])
        for ref, val in zip(o, ro):
            if isinstance(val, (list, tuple)):
                for h, piece in enumerate(val):
                    ref[h, 0] = piece.astype(ref.dtype)
            else:
                ref[...] = val.astype(ref.dtype)
        if n_a:
            @pl.when(pl.program_id(0) == 0)
            def _():
                for ref in a:
                    ref[...] = jnp.zeros(ref.shape, F32)
            for ref, val in zip(a, ao):
                ref[...] += val

    in_specs = [s for _, s in ins]
    in_specs += [pl.BlockSpec(c.shape, lambda i, nd=c.ndim: (0,) * nd) for c in consts]
    out_specs = [s for _, s in outs] + [pl.BlockSpec((1, w), lambda i: (0, 0)) for w in accs]
    out_shape = [s for s, _ in outs] + [jax.ShapeDtypeStruct((1, w), F32) for w in accs]
    return pl.pallas_call(
        body, name=name, grid=(t // tm,), in_specs=in_specs, out_specs=out_specs, out_shape=out_shape,
        compiler_params=_params(("arbitrary",)),
    )(*[a for a, _ in ins], *consts)


def _rin(a, tm, width=None, cb=0):
    width = a.shape[1] if width is None else width
    return a, pl.BlockSpec((tm, width), lambda i, cb=cb: (i, cb))


def _rout(t, tm, width, dtype):
    return jax.ShapeDtypeStruct((t, width), dtype), pl.BlockSpec((tm, width), lambda i: (i, 0))


def _tin(a):
    return a, pl.BlockSpec((a.shape[0], 1) + a.shape[2:], lambda i: (0, i, 0, 0))


def _tout(heads, nt, r, c, dtype):
    return jax.ShapeDtypeStruct((heads, nt, r, c), dtype), pl.BlockSpec((heads, 1, r, c), lambda i: (0, i, 0, 0))


def _pick(n, cap, mult):
    best = None
    for d in range(mult, min(n, cap) + 1, mult):
        if n % d == 0:
            best = d
    return best if best is not None else n


def mm_rows(a, b, name, trans_b=False, res=None, out_dtype=F32):
    m, k = a.shape
    n = b.shape[0] if trans_b else b.shape[1]
    tn = _pick(n, 3328, LANES)
    tm = _pick(m, 512 if k <= 3072 else 256, 8)
    has_res = res is not None

    def body(*refs):
        if has_res:
            a_ref, b_ref, r_ref, o_ref = refs
        else:
            a_ref, b_ref, o_ref = refs
        acc = _dg(a_ref[...], b_ref[...], 1, 1 if trans_b else 0)
        if has_res:
            acc = acc + r_ref[...]
        o_ref[...] = acc.astype(o_ref.dtype)

    in_specs = [pl.BlockSpec((tm, k), lambda j, i: (i, 0))]
    if trans_b:
        in_specs.append(pl.BlockSpec((tn, k), lambda j, i: (j, 0)))
    else:
        in_specs.append(pl.BlockSpec((k, tn), lambda j, i: (0, j)))
    args = [a, b]
    if has_res:
        in_specs.append(pl.BlockSpec((tm, tn), lambda j, i: (i, j)))
        args.append(res)
    return pl.pallas_call(
        body, name=name, grid=(n // tn, m // tm), in_specs=in_specs,
        out_specs=pl.BlockSpec((tm, tn), lambda j, i: (i, j)),
        out_shape=jax.ShapeDtypeStruct((m, n), out_dtype),
        compiler_params=_params(("arbitrary", "arbitrary")),
    )(*args)


def mm_tn(a, b, name, col_shards=None, transpose_out=False):
    t, m = a.shape
    parts = b.shape[0] if b.ndim == 3 else 1
    n = parts * b.shape[-1]
    tm = _pick(m, 512, LANES)
    width = n // col_shards if col_shards else None
    group = max(1, 3328 // width) if col_shards else 1
    tn = width * group if col_shards else _pick(n, 3328, LANES)
    tk = _pick(t, 1024, 8)
    nkk = t // tk
    per_part = b.shape[-1] // tn

    def body(a_ref, b_ref, o_ref, *scratch):
        acc_ref = scratch[0] if transpose_out else o_ref

        @pl.when(pl.program_id(2) == 0)
        def _():
            acc_ref[...] = jnp.zeros(acc_ref.shape, F32)
        acc = _dg(a_ref[...], b_ref[0] if b.ndim == 3 else b_ref[...], 0, 0)
        if col_shards:
            for s in range(group):
                acc_ref[s] += acc[:, s * width:(s + 1) * width]
        else:
            acc_ref[...] += acc
        if transpose_out:
            @pl.when(pl.program_id(2) == nkk - 1)
            def _():
                o_ref[...] = acc_ref[...].T

    if col_shards:
        out_spec = pl.BlockSpec((group, tm, width), lambda i, j, kk: (j, i, 0))
        out_shape = jax.ShapeDtypeStruct((col_shards, m, width), F32)
    elif transpose_out:
        out_spec = pl.BlockSpec((tn, tm), lambda i, j, kk: (j, i))
        out_shape = jax.ShapeDtypeStruct((n, m), F32)
    else:
        out_spec = pl.BlockSpec((tm, tn), lambda i, j, kk: (i, j))
        out_shape = jax.ShapeDtypeStruct((m, n), F32)
    if b.ndim == 3:
        b_spec = pl.BlockSpec((1, tk, tn), lambda i, j, kk: (j // per_part, kk, j % per_part))
    else:
        b_spec = pl.BlockSpec((tk, tn), lambda i, j, kk: (kk, j))
    return pl.pallas_call(
        body, name=name, grid=(m // tm, n // tn, nkk),
        in_specs=[pl.BlockSpec((tk, tm), lambda i, j, kk: (kk, i)), b_spec],
        out_specs=out_spec, out_shape=out_shape,
        scratch_shapes=[pltpu.VMEM((tm, tn), F32)] if transpose_out else [],
        compiler_params=_params(("arbitrary", "arbitrary", "arbitrary")),
    )(a, b)


def ffn_up(h2, w_gu):
    t, k = h2.shape
    dff = w_gu.shape[1] // 2
    tn = _pick(dff, 1408, LANES)
    ncol = dff // tn
    tm = _pick(t, 512, 8)

    def body(a_ref, wg_ref, wu_ref, gu_ref, act_ref):
        a = a_ref[...]
        g = _dg(a, wg_ref[...], 1, 0)
        u = _dg(a, wu_ref[...], 1, 0)
        gu_ref[0] = g
        gu_ref[1] = u
        act_ref[...] = _swiglu_fn(g, u).astype(act_ref.dtype)

    return pl.pallas_call(
        body, name="ffn_up", grid=(ncol, t // tm),
        in_specs=[pl.BlockSpec((tm, k), lambda s, i: (i, 0)), pl.BlockSpec((k, tn), lambda s, i: (0, s)),
                  pl.BlockSpec((k, tn), lambda s, i: (0, s + ncol))],
        out_specs=[pl.BlockSpec((2, tm, tn), lambda s, i: (0, i, s)), pl.BlockSpec((tm, tn), lambda s, i: (i, s))],
        out_shape=[jax.ShapeDtypeStruct((2, t, dff), F32), jax.ShapeDtypeStruct((t, dff), BF16)],
        compiler_params=_params(("arbitrary", "arbitrary")),
    )(h2, w_gu, w_gu)


def ffn_down_dx(dx2b, w_down, gu2):
    t, k = dx2b.shape
    dff = w_down.shape[0]
    tn = _pick(dff, 1408, LANES)
    tm = _pick(t, 512, 8)

    def body(a_ref, w_ref, gu_ref, o_ref):
        dact = _dg(a_ref[...], w_ref[...], 1, 1)
        _, vjp = jax.vjp(_swiglu_fn, gu_ref[0], gu_ref[1])
        dg, du = vjp(dact)
        o_ref[0] = dg.astype(o_ref.dtype)
        o_ref[1] = du.astype(o_ref.dtype)

    blk = pl.BlockSpec((2, tm, tn), lambda s, i: (0, i, s))
    return pl.pallas_call(
        body, name="ffn_down_dx", grid=(dff // tn, t // tm),
        in_specs=[pl.BlockSpec((tm, k), lambda s, i: (i, 0)), pl.BlockSpec((tn, k), lambda s, i: (s, 0)), blk],
        out_specs=blk, out_shape=jax.ShapeDtypeStruct((2, t, dff), BF16),
        compiler_params=_params(("arbitrary", "arbitrary")),
    )(dx2b, w_down, gu2)


def mm_res_norm(a, b, res, w, name):
    t, k = a.shape
    d = b.shape[1]
    tm = _pick(t, 512, 8)

    def body(a_ref, b_ref, r_ref, w_ref, x_ref, h_ref):
        xv = _dg(a_ref[...], b_ref[...], 1, 0) + r_ref[...]
        x_ref[...] = xv
        h_ref[...] = _rms(xv, w_ref[...]).astype(h_ref.dtype)

    row = pl.BlockSpec((tm, d), lambda i: (i, 0))
    return pl.pallas_call(
        body, name=name, grid=(t // tm,),
        in_specs=[pl.BlockSpec((tm, k), lambda i: (i, 0)), pl.BlockSpec((k, d), lambda i: (0, 0)), row,
                  pl.BlockSpec((1, d), lambda i: (0, 0))],
        out_specs=[row, row],
        out_shape=[jax.ShapeDtypeStruct((t, d), F32), jax.ShapeDtypeStruct((t, d), BF16)],
        compiler_params=_params(("arbitrary",)),
    )(a, b, res, w)


def mm_res_loss(a, b, res, tgt, w, name):
    t, k = a.shape
    d = b.shape[1]
    tm = _pick(t, 256, 8)

    def body(a_ref, b_ref, r_ref, t_ref, w_ref, dx_ref, dxb_ref, dw_ref, loss_ref):
        @pl.when(pl.program_id(0) == 0)
        def _():
            dw_ref[...] = jnp.zeros(dw_ref.shape, F32)
            loss_ref[...] = jnp.zeros(loss_ref.shape, F32)
        xv = _dg(a_ref[...], b_ref[...], 1, 0) + r_ref[...]
        tg = t_ref[...]
        val, vjp = jax.vjp(lambda x_, w_: _loss_fn(x_, w_, tg), xv, w_ref[...])
        dx, dw = vjp(jnp.ones((1, 1), F32))
        dx_ref[...] = dx
        dxb_ref[...] = dx.astype(dxb_ref.dtype)
        dw_ref[...] += dw
        loss_ref[...] += jnp.broadcast_to(val, (1, LANES))

    row = pl.BlockSpec((tm, d), lambda i: (i, 0))
    vec = pl.BlockSpec((1, d), lambda i: (0, 0))
    return pl.pallas_call(
        body, name=name, grid=(t // tm,),
        in_specs=[pl.BlockSpec((tm, k), lambda i: (i, 0)), pl.BlockSpec((k, d), lambda i: (0, 0)), row, row, vec],
        out_specs=[row, row, vec, pl.BlockSpec((1, LANES), lambda i: (0, 0))],
        out_shape=[jax.ShapeDtypeStruct((t, d), F32), jax.ShapeDtypeStruct((t, d), BF16),
                   jax.ShapeDtypeStruct((1, d), F32), jax.ShapeDtypeStruct((1, LANES), F32)],
        compiler_params=_params(("arbitrary",)),
    )(a, b, res, tgt, w)


def mm_nt_norm_bwd(a, wmat, x, w, dres, name, with_bf16):
    parts = a.shape[0] if a.ndim == 3 else 1
    t, kp = a.shape[-2], a.shape[-1]
    d = wmat.shape[0]
    tm = _pick(t, 256, 8)

    def body(*refs):
        a_ref, w_refs = refs[0], refs[1:1 + parts]
        x_ref, nw_ref, r_ref, dx_ref = refs[1 + parts:5 + parts]
        dw_ref = refs[-1]

        @pl.when(pl.program_id(0) == 0)
        def _():
            dw_ref[...] = jnp.zeros(dw_ref.shape, F32)
        dh = None
        for p in range(parts):
            term = _dg(a_ref[p] if a.ndim == 3 else a_ref[...], w_refs[p][...], 1, 1)
            dh = term if dh is None else dh + term
        _, vjp = jax.vjp(_rms, x_ref[...], nw_ref[...])
        dx, dw = vjp(dh)
        dx = dx + r_ref[...]
        dx_ref[...] = dx
        if with_bf16:
            refs[5 + parts][...] = dx.astype(BF16)
        dw_ref[...] += dw

    row = pl.BlockSpec((tm, d), lambda i: (i, 0))
    vec = pl.BlockSpec((1, d), lambda i: (0, 0))
    a_spec = pl.BlockSpec((parts, tm, kp), lambda i: (0, i, 0)) if a.ndim == 3 else pl.BlockSpec((tm, kp), lambda i: (i, 0))
    w_specs = [pl.BlockSpec((d, kp), lambda i, p=p: (0, p)) for p in range(parts)]
    outs = [row] + ([row] if with_bf16 else []) + [vec]
    shapes = [jax.ShapeDtypeStruct((t, d), F32)] + ([jax.ShapeDtypeStruct((t, d), BF16)] if with_bf16 else [])
    return pl.pallas_call(
        body, name=name, grid=(t // tm,),
        in_specs=[a_spec] + w_specs + [row, vec, row],
        out_specs=outs, out_shape=shapes + [jax.ShapeDtypeStruct((1, d), F32)],
        compiler_params=_params(("arbitrary",)),
    )(a, *([wmat] * parts), x, w, dres)


def _lower_bound(l0, l1):
    m = jnp.maximum(l0, l1)
    e0 = jnp.exp(l0 - m)
    e1 = jnp.exp(l1 - m)
    return e0 / (e0 + e1)


def _gla_consts(rev):
    ri = lax.broadcasted_iota(jnp.int32, (CHUNK, CHUNK), 0)
    ci = lax.broadcasted_iota(jnp.int32, (CHUNK, CHUNK), 1)
    keep = (ci >= ri) if rev else (ci <= ri)
    ref_mask = lax.broadcasted_iota(jnp.int32, (CHUNK, 1), 0) == (CHUNK // 2 if rev else CHUNK // 2 - 1)
    return keep, ref_mask


def _gla_block(uq, uf, ui, l0, l1, st_in, rev):
    ncb = uq.shape[0] // CHUNK
    heads = range(HG_HEADS)
    keep, ref_mask = _gla_consts(rev)
    cum = keep.astype(F32)
    lb = _lower_bound(l0, l1)
    q = uq * _sigmoid(uq)
    k = (1.0 - lb) * _sigmoid(-uf)
    g = jnp.log(lb + (1.0 - lb) * _sigmoid(uf))

    def rows(a, c):
        return a[c * CHUNK:(c + 1) * CHUNK]

    def head(a, h):
        return a[:, h * HG_D:(h + 1) * HG_D]

    bs = [xdot_l(cum, rows(g, c)) for c in range(ncb)]
    q_in, k_in, q_b, k_d, decay = [], [], [], [], []
    for c in range(ncb):
        b = bs[c]
        b_ref = jnp.sum(jnp.where(ref_mask, b, 0.0), axis=0, keepdims=True)
        b_last = jnp.sum(rows(g, c), axis=0, keepdims=True)
        qc, kc = rows(q, c), rows(k, c)
        q_in.append(qc * jnp.exp(b - b_ref))
        k_in.append(kc * jnp.exp(b_ref - b))
        q_b.append(qc * jnp.exp(b))
        k_d.append(kc * jnp.exp(b_last - b))
        decay.append(jnp.exp(b_last))
    scores = [[jnp.where(keep, dot_nt(head(q_in[c], h), head(k_in[c], h)), 0.0) for h in heads] for c in range(ncb)]
    o_intra = [[dot_nn(scores[c][h], head(rows(ui, c), h)) for h in heads] for c in range(ncb)]
    contrib = [[dot_tn(head(rows(ui, c), h), head(k_d[c], h)) for h in heads] for c in range(ncb)]
    st = list(st_in)
    o_rows = [None] * ncb
    for c in (reversed(range(ncb)) if rev else range(ncb)):
        parts = []
        for h in heads:
            parts.append(o_intra[c][h] + dot_nt(head(q_b[c], h), st[h]))
            st[h] = st[h] * head(decay[c], h) + contrib[c][h]
        o_rows[c] = jnp.concatenate(parts, axis=1)
    return jnp.concatenate(o_rows, axis=0), tuple(st)


def _gla_blocks(t):
    tb = min(512, t)
    return tb, t // tb


def gla_fwd(u, l0, l1, fcol, rev, name):
    t = u.shape[0]
    tb, nb = _gla_blocks(t)

    def blk(i):
        return (nb - 1 - i) if rev else i

    def body(uq_ref, uf_ref, ui_ref, l0_ref, l1_ref, o_ref, ss_ref, st_ref):
        @pl.when(pl.program_id(0) == 0)
        def _():
            st_ref[...] = jnp.zeros(st_ref.shape, F32)
        ss_ref[0] = st_ref[...]
        o, st_out = _gla_block(uq_ref[...], uf_ref[...], ui_ref[...], l0_ref[...], l1_ref[...],
                               tuple(st_ref[h] for h in range(HG_HEADS)), rev)
        o_ref[...] = o
        for h in range(HG_HEADS):
            st_ref[h] = st_out[h]

    row = lambda cb: pl.BlockSpec((tb, HG_W), lambda i: (blk(i), cb))
    vec = pl.BlockSpec((1, HG_W), lambda i: (0, 0))
    return pl.pallas_call(
        body, name=name, grid=(nb,),
        in_specs=[row(0), row(fcol), row(3), vec, vec],
        out_specs=[pl.BlockSpec((tb, HG_W), lambda i: (blk(i), 0)),
                   pl.BlockSpec((1, HG_HEADS, HG_D, HG_D), lambda i: (blk(i), 0, 0, 0))],
        out_shape=[jax.ShapeDtypeStruct((t, HG_W), F32),
                   jax.ShapeDtypeStruct((nb, HG_HEADS, HG_D, HG_D), F32)],
        scratch_shapes=[pltpu.VMEM((HG_HEADS, HG_D, HG_D), F32)],
        compiler_params=_params(("arbitrary",)),
    )(u, u, u, l0, l1)


def gla_bwd(u, l0, l1, ss, do, fcol, rev, name):
    t = u.shape[0]
    tb, nb = _gla_blocks(t)

    def blk(i):
        return i if rev else (nb - 1 - i)

    def body(uq_ref, uf_ref, ui_ref, l0_ref, l1_ref, ss_ref, do_ref,
             dq_ref, df_ref, di_ref, dl0_ref, dl1_ref, dst_ref):
        @pl.when(pl.program_id(0) == 0)
        def _():
            dst_ref[...] = jnp.zeros(dst_ref.shape, F32)
            dl0_ref[...] = jnp.zeros(dl0_ref.shape, F32)
            dl1_ref[...] = jnp.zeros(dl1_ref.shape, F32)
        heads = range(HG_HEADS)
        _, vjp = jax.vjp(functools.partial(_gla_block, rev=rev), uq_ref[...], uf_ref[...], ui_ref[...],
                         l0_ref[...], l1_ref[...], tuple(ss_ref[0, h] for h in heads))
        dq, df, di, dl0, dl1, dst = vjp((do_ref[...], tuple(dst_ref[h] for h in heads)))
        dq_ref[...] = dq
        df_ref[...] = df
        di_ref[...] = di
        dl0_ref[...] += dl0
        dl1_ref[...] += dl1
        for h in heads:
            dst_ref[h] = dst[h]

    row = lambda cb: pl.BlockSpec((tb, HG_W), lambda i: (blk(i), cb))
    vec = pl.BlockSpec((1, HG_W), lambda i: (0, 0))
    orow = pl.BlockSpec((tb, HG_W), lambda i: (blk(i), 0))
    return pl.pallas_call(
        body, name=name, grid=(nb,),
        in_specs=[row(0), row(fcol), row(3), vec, vec,
                  pl.BlockSpec((1, HG_HEADS, HG_D, HG_D), lambda i: (blk(i), 0, 0, 0)), orow],
        out_specs=[orow, orow, orow, vec, vec],
        out_shape=[jax.ShapeDtypeStruct((t, HG_W), F32)] * 3 + [jax.ShapeDtypeStruct((1, HG_W), F32)] * 2,
        scratch_shapes=[pltpu.VMEM((HG_HEADS, HG_D, HG_D), F32)],
        compiler_params=_params(("arbitrary",)),
    )(u, u, u, l0, l1, ss, do)


def _rope_tables(t):
    rows = t // GRID_W
    row = jnp.repeat(jnp.arange(rows), GRID_W).astype(F32)
    col = jnp.tile(jnp.arange(GRID_W), rows).astype(F32)
    axis_dim = ATT_DH // 2
    freqs = ROPE_THETA ** (-jnp.arange(0, axis_dim, 2, dtype=F32) / axis_dim)
    ang = jnp.concatenate([row[:, None] * freqs, col[:, None] * freqs], axis=-1)
    cos2 = jnp.repeat(jnp.cos(ang), 2, axis=-1)
    sin2 = jnp.repeat(jnp.sin(ang), 2, axis=-1) * jnp.tile(jnp.array([-1.0, 1.0], F32), ATT_DH // 2)
    return cos2, sin2


def _group_sum_matrix(width):
    idx = np.arange(width) // ATT_DH
    return jnp.asarray((idx[:, None] == idx[None, :]).astype(np.float32))


def _tile_matrix(width):
    m = np.zeros((LANES, width), np.float32)
    m[np.arange(width) % ATT_DH, np.arange(width)] = 1.0
    return jnp.asarray(m)


def _tile_w(w128, tile_m):
    w8 = jnp.broadcast_to(w128, (8, LANES))
    return jnp.sum(xdot_r(w8, tile_m), axis=0, keepdims=True) * 0.125


def _head_norm_rope(a, w128, cos_t, sin_t, gsum, tile_m, scale):
    ssq = xdot_r(a * a, gsum)
    y = a * lax.rsqrt(ssq * (1.0 / ATT_DH) + EPS) * _tile_w(w128, tile_m)
    return (y * cos_t + swap_pairs(y) * sin_t) * scale


def _att_prep_fn(aq, ak, cq, sq, ck, sk, qw, kw, gq, gk, tq, tk):
    q = _head_norm_rope(aq, qw, cq, sq, gq, tq, ATT_DH ** -0.5)
    k = _head_norm_rope(ak, kw, ck, sk, gk, tk, 1.0)
    return q, k


def _head_t(x, heads):
    xt = x.T
    return [xt[h * ATT_DH:(h + 1) * ATT_DH] for h in range(heads)]


def _head_s(x, heads):
    lane = lax.broadcasted_iota(jnp.int32, (x.shape[0], LANES), 1)
    out = []
    for h in range(heads):
        pair = x[:, (h // 2) * LANES:(h // 2 + 1) * LANES]
        if h % 2:
            pair = pltpu.roll(pair, ATT_DH, 1)
        out.append(jnp.where(lane < ATT_DH, pair, 0.0))
    return out


def _from_head_t(tile):
    return jnp.concatenate([tile[h, 0] for h in range(tile.shape[0])], axis=0).T


def _pad_rows(a):
    return jnp.concatenate([a, jnp.zeros(a.shape, a.dtype)], axis=0)


def _col_bcast(row_vec):
    return jnp.broadcast_to(row_vec, (LANES, row_vec.shape[1])).T


FA_K_FWD = 1024
FA_K_BWD = 512
FA_STRIPS_FWD = 4
FA_STRIPS_BWD = 8


def _key_block(ref, j, tiles):
    return jnp.concatenate([ref[0, j * tiles + i] for i in range(tiles)], axis=1)


def _riding(body, n_in, n_out, grid, ride):
    if ride is None:
        return body, [], [], [], [], []
    arrays, reduce = ride
    n = len(arrays)
    steps = grid[0] * grid[1]

    def riding_body(*refs):
        cuts = np.cumsum([0, n_in, n, n_out, n]).tolist()
        own_in, rin, own_out, rout = (refs[a:b] for a, b in zip(cuts[:-1], cuts[1:]))
        sems = refs[cuts[-1]:]
        start, forward, finish = _exchange_phases(rin, rout, *sems, reduce)
        step = pl.program_id(0) * grid[1] + pl.program_id(1)
        pl.when(step == 0)(start)
        pl.when(step == steps // 2)(forward)
        body(*own_in, *own_out)
        pl.when(step == steps - 1)(finish)

    return (riding_body, _hbm_specs(n), _hbm_specs(n), _exchange_shapes(arrays, reduce), _exchange_sems(n), list(arrays))


def fa_fwd(qt, ks, vt, ride=None):
    _, ns, dh, nq = qt.shape
    lk = ks.shape[1]
    bk = min(FA_K_FWD, lk)
    nk = lk // bk
    spg = min(FA_STRIPS_FWD, ns)
    grid = (ATT_KV, ns // spg)

    def body(q_ref, k_ref, v_ref, o_ref, lse_ref):
        qs = [_pad_rows(q_ref[0, c]) for c in range(spg)]

        def keys(j):
            return k_ref[0, j * bk:(j + 1) * bk, :]

        def step(j, carry):
            st0, stats = carry
            kb = keys(j)
            vb = _key_block(v_ref, j, bk // nq)
            sts = [st0] + [_dg(kb, qs[c], 1, 0) for c in range(1, spg)]
            out = []
            for c in range(spg):
                m, l, acc = stats[c]
                m_new = jnp.maximum(m, jnp.max(sts[c], axis=0, keepdims=True))
                alpha = jnp.exp(m - m_new)
                p = jnp.exp(sts[c] - m_new)
                l = alpha * l + jnp.sum(p, axis=0, keepdims=True)
                if c == spg - 1:
                    st0 = _dg(keys(min(j + 1, nk - 1)), qs[0], 1, 0)
                acc = alpha * acc + _dg(vb, p, 1, 0)
                out.append((m_new, l, acc))
            return st0, tuple(out)

        init = tuple((jnp.full((1, nq), -jnp.inf, F32), jnp.zeros((1, nq), F32), jnp.zeros((dh, nq), F32))
                     for _ in range(spg))
        carry = (_dg(keys(0), qs[0], 1, 0), init)
        for j in range(nk):
            carry = step(j, carry)
        _, res = carry
        for c in range(spg):
            m, l, acc = res[c]
            o_ref[0, c] = acc / l
            lse_ref[0, c] = _col_bcast(m + jnp.log(l))

    body, r_in, r_out, r_shapes, r_sems, r_args = _riding(body, 3, 2, grid, ride)
    return pl.pallas_call(
        body, name="fa_fwd", grid=grid,
        in_specs=[pl.BlockSpec((1, spg, dh, nq), lambda g, i: (g, i, 0, 0)),
                  pl.BlockSpec((1, lk, LANES), lambda g, i: (g, 0, 0)),
                  pl.BlockSpec((1, lk // nq, dh, nq), lambda g, i: (g, 0, 0, 0))] + r_in,
        out_specs=[pl.BlockSpec((1, spg, dh, nq), lambda g, i: (g, i, 0, 0)),
                   pl.BlockSpec((1, spg, nq, LANES), lambda g, i: (g, i, 0, 0))] + r_out,
        out_shape=[jax.ShapeDtypeStruct((ATT_KV, ns, dh, nq), F32),
                   jax.ShapeDtypeStruct((ATT_KV, ns, nq, LANES), F32)] + r_shapes,
        scratch_shapes=r_sems,
        compiler_params=_params(("arbitrary", "arbitrary")),
    )(qt, ks, vt, *r_args)


def fa_bwd(qs, qt, dos, dot_, ot, lse, ks, kt, vt, ride=None):
    _, ns, dh, nq = qt.shape
    lk = ks.shape[1]
    bk = min(FA_K_BWD, lk)
    nk = lk // bk
    tiles = bk // nq
    spg = min(FA_STRIPS_BWD, ns)
    nc = bk // LANES
    grid = (ATT_KV, ns // spg)

    def body(qs_ref, qt_ref, dos_ref, dot_ref, ot_ref, lse_ref, ks_ref, kt_ref, vt_ref, dq_ref, dk_ref, dv_ref):
        @pl.when(pl.program_id(1) == 0)
        def _():
            dk_ref[...] = jnp.zeros(dk_ref.shape, F32)
            dv_ref[...] = jnp.zeros(dv_ref.shape, F32)

        strips = range(spg)
        lse_b = [lse_ref[0, c] for c in strips]
        d_b = [_col_bcast(jnp.sum(dot_ref[0, c].astype(F32) * ot_ref[0, c], axis=0, keepdims=True)) for c in strips]

        def step(j, dqs):
            ktb, vtb = _pad_rows(_key_block(kt_ref, j, tiles)), _pad_rows(_key_block(vt_ref, j, tiles))
            kb = ks_ref[0, j * bk:(j + 1) * bk, :]
            prods = [(_dg(qs_ref[0, c], ktb, 1, 0), _dg(dos_ref[0, c], vtb, 1, 0)) for c in strips]
            out = []
            for c in strips:
                s, dp = prods[c]
                ps, dss = [], []
                for cc in range(nc):
                    sl = slice(cc * LANES, (cc + 1) * LANES)
                    pc = jnp.exp(s[:, sl] - lse_b[c])
                    ps.append(pc.astype(BF16))
                    dss.append((pc * (dp[:, sl] - d_b[c])).astype(BF16))
                p, ds = jnp.concatenate(ps, axis=1), jnp.concatenate(dss, axis=1)
                dv = _dg(dot_ref[0, c], p, 1, 0)
                dk = _dg(qt_ref[0, c], ds, 1, 0)
                for i in range(tiles):
                    dv_ref[0, j * tiles + i] += dv[:, i * nq:(i + 1) * nq]
                    dk_ref[0, j * tiles + i] += dk[:, i * nq:(i + 1) * nq]
                out.append(dqs[c] + _dg(ds, kb, 1, 0))
            return tuple(out)

        dqs = tuple(jnp.zeros((nq, LANES), F32) for _ in strips)
        for j in range(nk):
            dqs = step(j, dqs)
        for c in strips:
            dq_ref[0, c] = dqs[c].T[:dh]

    sspec = pl.BlockSpec((1, spg, nq, LANES), lambda g, i: (g, i, 0, 0))
    tspec = pl.BlockSpec((1, spg, dh, nq), lambda g, i: (g, i, 0, 0))
    kspec = pl.BlockSpec((1, lk // nq, dh, nq), lambda g, i: (g, 0, 0, 0))
    body, r_in, r_out, r_shapes, r_sems, r_args = _riding(body, 9, 3, grid, ride)
    return pl.pallas_call(
        body, name="fa_bwd", grid=grid,
        in_specs=[sspec, tspec, sspec, tspec, tspec, sspec, pl.BlockSpec((1, lk, LANES), lambda g, i: (g, 0, 0)),
                  kspec, kspec] + r_in,
        out_specs=[tspec, kspec, kspec] + r_out,
        out_shape=[jax.ShapeDtypeStruct((ATT_KV, ns, dh, nq), F32),
                   jax.ShapeDtypeStruct((ATT_KV, lk // nq, dh, nq), F32),
                   jax.ShapeDtypeStruct((ATT_KV, lk // nq, dh, nq), F32)] + r_shapes,
        scratch_shapes=r_sems,
        compiler_params=_params(("arbitrary", "arbitrary")),
    )(qs, qt, dos, dot_, ot, lse, ks, kt, vt, *r_args)


def _post_mix_fn(of, ob, ug, oa, hgw, attw):
    o = of + ob
    parts = []
    for h in range(HG_HEADS):
        parts.append(_rms(o[:, h * HG_D:(h + 1) * HG_D], hgw))
    hg = jnp.concatenate(parts, axis=1) * (ug * _sigmoid(ug))
    return jnp.concatenate([hg, _rms(oa, attw)], axis=1)


def _swiglu_fn(gate, up):
    return gate * _sigmoid(gate) * up


def _loss_fn(x2, w, tgt):
    e = _rms(x2, w) - tgt
    return 0.5 * jnp.sum(jnp.mean(e * e, axis=-1, keepdims=True), axis=0, keepdims=True)


def _place():
    return lax.axis_index("x"), lax.axis_index("y"), lax.axis_index("c")


def _other_chips(x, y):
    return [(1 - x, y), (x, 1 - y), (1 - x, 1 - y)]


def _hbm_specs(n):
    return [pl.BlockSpec(memory_space=pl.ANY)] * n


SEMS_PER_ARRAY = 7


def _exchange_sems(n):
    return [pltpu.SemaphoreType.DMA((SEMS_PER_ARRAY * n,)), pltpu.SemaphoreType.DMA((SEMS_PER_ARRAY * n,)),
            pltpu.SemaphoreType.DMA((n,))]


def _exchange_phases(srcs, outs, ssem, rsem, lsem, reduce):
    n = len(srcs)
    x, y, c = _place()
    k = 2 * x + y
    sib = (x, y, 1 - c)
    chips = _other_chips(x, y)
    pairs = [(a, j) for a in range(n) for j in range(3)]

    def hrows(a):
        return srcs[a].shape[1] if reduce else srcs[a].shape[0] // 2

    def half(a, kk, cc):
        return outs[a].at[kk, pl.ds(cc * hrows(a), hrows(a)), :]

    def mine(a, kk):
        return srcs[a].at[kk] if reduce else srcs[a].at[pl.ds(c * hrows(a), hrows(a)), :]

    def copy(a, j, src_ref, dst_ref, to):
        return pltpu.make_async_remote_copy(src_ref=src_ref, dst_ref=dst_ref, send_sem=ssem.at[SEMS_PER_ARRAY * a + j],
                                            recv_sem=rsem.at[SEMS_PER_ARRAY * a + j], device_id=to, device_id_type=MESH)

    def local(a):
        if reduce:
            return pltpu.make_async_copy(srcs[a].at[k], half(a, k, c), lsem.at[a])
        return pltpu.make_async_copy(srcs[a], outs[a].at[k], lsem.at[a])

    def ici(a, j, arriving):
        px, py = chips[j]
        kk = 2 * px + py
        if arriving:
            return copy(a, j, mine(a, k), half(a, kk, c), (px, py, c))
        return copy(a, j, mine(a, kk), half(a, k, c), (px, py, c))

    def passed(a, j, arriving):
        px, py = chips[j]
        kk = 2 * px + py
        return copy(a, 3 + j, half(a, kk, c), half(a, kk, (1 - c) if arriving else c), sib)

    def own(a, arriving):
        return copy(a, 6, mine(a, k), half(a, k, (1 - c) if arriving else c), sib)

    def start():
        for a in range(n):
            local(a).start()
        for a, j in pairs:
            ici(a, j, False).start()
        if reduce:
            for a in range(n):
                own(a, False).start()

    def forward():
        for a, j in pairs:
            ici(a, j, True).wait_recv()
            passed(a, j, False).start()

    def finish():
        for a in range(n):
            if reduce:
                own(a, True).wait_recv()
            for j in range(3):
                passed(a, j, True).wait_recv()
        for a, j in pairs:
            ici(a, j, False).wait_send()
            passed(a, j, False).wait_send()
        for a in range(n):
            if reduce:
                own(a, False).wait_send()
            local(a).wait()

    return start, forward, finish


def _exchange_shapes(arrays, reduce):
    if reduce:
        return [jax.ShapeDtypeStruct((N_CHIPS, 2 * p.shape[1], p.shape[2]), p.dtype) for p in arrays]
    return [jax.ShapeDtypeStruct((N_CHIPS,) + s.shape, s.dtype) for s in arrays]


def exchange(arrays, reduce, name):
    n = len(arrays)

    def body(*refs):
        start, forward, finish = _exchange_phases(refs[:n], refs[n:2 * n], *refs[2 * n:], reduce)
        start()
        forward()
        finish()

    return pl.pallas_call(
        body, name=name, in_specs=_hbm_specs(n), out_specs=_hbm_specs(n),
        out_shape=_exchange_shapes(arrays, reduce), scratch_shapes=_exchange_sems(n),
    )(*arrays)


def rs_siblings(gs, name):
    n = len(gs)

    def body(*refs):
        srcs, outs = refs[:n], refs[n:2 * n]
        ssem, rsem = refs[2 * n:]
        x, y, c = _place()
        cps = []
        for a in range(n):
            hr = srcs[a].shape[1] // 2
            cp = pltpu.make_async_remote_copy(src_ref=srcs[a].at[:, pl.ds((1 - c) * hr, hr), :], dst_ref=outs[a],
                                              send_sem=ssem.at[a], recv_sem=rsem.at[a], device_id=(x, y, 1 - c),
                                              device_id_type=MESH)
            cp.start()
            cps.append(cp)
        for cp in cps:
            cp.wait()

    return pl.pallas_call(
        body, name=name, in_specs=_hbm_specs(n), out_specs=_hbm_specs(n),
        out_shape=[jax.ShapeDtypeStruct((N_CHIPS, g.shape[1] // 2, g.shape[2]), F32) for g in gs],
        scratch_shapes=[pltpu.SemaphoreType.DMA((n,)), pltpu.SemaphoreType.DMA((n,))],
    )(*gs)


def allreduce_small(p, name):
    rows, width = p.shape

    def body(p_ref, s_ref, gath, ssem, rsem):
        x, y, c = _place()
        me = 4 * x + 2 * y + c
        copies = []
        for d in range(1, N_DEV):
            dx, dy, dc = (d >> 2) & 1, (d >> 1) & 1, d & 1
            peer = (x ^ dx, y ^ dy, c ^ dc)
            cp = pltpu.make_async_remote_copy(src_ref=p_ref, dst_ref=gath.at[me], send_sem=ssem.at[d - 1],
                                              recv_sem=rsem.at[d - 1], device_id=peer, device_id_type=MESH)
            cp.start()
            copies.append(cp)
        gath[me] = p_ref[...]
        for d, cp in enumerate(copies, start=1):
            dx, dy, dc = (d >> 2) & 1, (d >> 1) & 1, d & 1
            peer_slot = 4 * (x ^ dx) + 2 * (y ^ dy) + (c ^ dc)
            pltpu.make_async_remote_copy(src_ref=p_ref, dst_ref=gath.at[peer_slot], send_sem=ssem.at[d - 1],
                                         recv_sem=rsem.at[d - 1], device_id=(x ^ dx, y ^ dy, c ^ dc),
                                         device_id_type=MESH).wait_recv()
        for cp in copies:
            cp.wait_send()
        acc = gath[0]
        for d in range(1, N_DEV):
            acc = acc + gath[d]
        s_ref[...] = acc

    return pl.pallas_call(
        body, name=name,
        in_specs=[pl.BlockSpec(memory_space=pltpu.VMEM)],
        out_specs=pl.BlockSpec(memory_space=pltpu.VMEM),
        out_shape=jax.ShapeDtypeStruct((rows, width), F32),
        scratch_shapes=[pltpu.VMEM((N_DEV, rows, width), F32), pltpu.SemaphoreType.DMA((N_DEV - 1,)),
                        pltpu.SemaphoreType.DMA((N_DEV - 1,))],
    )(p)


def add_my_half(g, recv, name):
    _, r, cols = g.shape
    hr = r // 2
    tb = _pick(hr, 512, 8)
    nb = hr // tb
    c_arr = lax.axis_index("c").astype(jnp.int32).reshape(1)

    def body(c_ref, g_ref, r_ref, o_ref):
        o_ref[...] = (g_ref[...] + r_ref[...]).astype(o_ref.dtype)

    return pl.pallas_call(
        body, name=name,
        grid_spec=pltpu.PrefetchScalarGridSpec(
            num_scalar_prefetch=1, grid=(N_CHIPS, nb),
            in_specs=[pl.BlockSpec((1, tb, cols), lambda k, i, c_ref: (k, c_ref[0] * nb + i, 0)),
                      pl.BlockSpec((1, tb, cols), lambda k, i, c_ref: (k, i, 0))],
            out_specs=pl.BlockSpec((1, tb, cols), lambda k, i, c_ref: (k, i, 0))),
        out_shape=jax.ShapeDtypeStruct((N_CHIPS, hr, cols), BF16),
        compiler_params=_params(("arbitrary", "arbitrary")),
    )(c_arr, g, recv)


def sum_chips(parts, name):
    _, hr, cols = parts.shape
    tb = _pick(hr, 512, 8)

    def body(p_ref, o_ref):
        o_ref[...] = ((p_ref[0].astype(F32) + p_ref[1].astype(F32)) + p_ref[2].astype(F32)) + p_ref[3].astype(F32)

    return pl.pallas_call(
        body, name=name, grid=(hr // tb,),
        in_specs=[pl.BlockSpec((N_CHIPS, tb, cols), lambda i: (0, i, 0))],
        out_specs=pl.BlockSpec((tb, cols), lambda i: (i, 0)),
        out_shape=jax.ShapeDtypeStruct((hr, cols), F32),
        compiler_params=_params(("arbitrary",)),
    )(parts)


def adamw(w, g, m, v, name):
    rows, width = w.shape
    tb = _pick(rows, 512, 8)

    def body(w_ref, g_ref, m_ref, v_ref, d_ref, mo_ref, vo_ref):
        gg = g_ref[...]
        m_new = ADAM_B1 * m_ref[...] + (1.0 - ADAM_B1) * gg
        v_new = ADAM_B2 * v_ref[...] + (1.0 - ADAM_B2) * (gg * gg)
        m_hat = m_new / (1.0 - ADAM_B1 ** ADAM_STEP)
        v_hat = v_new / (1.0 - ADAM_B2 ** ADAM_STEP)
        d_ref[...] = -ADAM_LR * (m_hat / (jnp.sqrt(v_hat) + ADAM_EPS) + ADAM_WD * w_ref[...])
        mo_ref[...] = m_new
        vo_ref[...] = v_new

    spec = pl.BlockSpec((tb, width), lambda i: (i, 0))
    return pl.pallas_call(
        body, name=name, grid=(rows // tb,), in_specs=[spec] * 4, out_specs=[spec] * 3,
        out_shape=[jax.ShapeDtypeStruct((rows, width), F32)] * 3,
        compiler_params=_params(("arbitrary",)),
    )(w, g, m, v)


def _pack_rows(vecs, width=1024):
    rows, cur, used = [], [], 0
    for v in vecs:
        n = v.shape[1]
        if used + n > width:
            cur.append(jnp.zeros((1, width - used), F32))
            rows.append(jnp.concatenate(cur, axis=1))
            cur, used = [], 0
        cur.append(v)
        used += n
    cur.append(jnp.zeros((1, width - used), F32))
    rows.append(jnp.concatenate(cur, axis=1))
    return rows


def _pad128(v):
    return jnp.pad(v, ((0, 0), (0, LANES - v.shape[1])))


def kernel(x, norm1_w, w_in, lb_logits, hg_norm_w, q_norm_w, k_norm_w, att_norm_w, w_out, norm2_w, w_gate_up, w_down, final_norm_w, loss_target, m_norm1_w, m_w_in, m_lb_logits, m_hg_norm_w, m_q_norm_w, m_k_norm_w, m_att_norm_w, m_w_out, m_norm2_w, m_w_gate_up, m_w_down, m_final_norm_w, v_norm1_w, v_w_in, v_lb_logits, v_hg_norm_w, v_q_norm_w, v_k_norm_w, v_att_norm_w, v_w_out, v_norm2_w, v_w_gate_up, v_w_down, v_final_norm_w):
    t, d = x.shape[1], x.shape[2]
    xi, yi, ci = _place()
    chip = 2 * xi + yi
    x2d = x.reshape(t, d)
    tgt = loss_target.reshape(t, d)
    tok = min(TOK, t)
    nt = t // tok
    ns = ATT_GROUP * nt

    (g_in,) = exchange([w_in[0].astype(BF16)], False, "allgather_w_in")
    wf_in = g_in.transpose(1, 0, 2).reshape(g_in.shape[1], -1)
    late_w = [w_out[0].astype(BF16), w_gate_up[0].astype(BF16), w_down[0].astype(BF16)]
    lb_rows = lb_logits.reshape(4, LANES) * (ci == 0).astype(F32)
    lb_pad = lax.dynamic_update_slice(jnp.zeros((8, 1024), F32), lb_rows, (0, chip * LANES))
    lb_full = allreduce_small(lb_pad, "gather_lb")[:4, :HG_W]
    l_f0, l_f1, l_b0, l_b1 = (lb_full[i:i + 1] for i in range(4))

    n1 = norm1_w.reshape(1, d)
    n2 = norm2_w.reshape(1, d)
    nf = final_norm_w.reshape(1, d)
    tm = min(512, t)
    (h1,) = _rows(lambda a, w: ((_rms(a, w),), ()), "norm1", t, tm, [_rin(x2d, tm)], [n1], [_rout(t, tm, d, BF16)])
    u = mm_rows(h1, wf_in, "mm_in")
    o_f, ss_f = gla_fwd(u, l_f0, l_f1, 1, False, "gla_fwd_f")
    o_b, ss_b = gla_fwd(u, l_b0, l_b1, 2, True, "gla_fwd_b")

    cos2, sin2 = _rope_tables(t)
    cq, sq = jnp.tile(cos2, (1, ATT_HEADS)), jnp.tile(sin2, (1, ATT_HEADS))
    ck, sk = jnp.tile(cos2, (1, ATT_KV)), jnp.tile(sin2, (1, ATT_KV))
    qw, kw = _pad128(q_norm_w.reshape(1, ATT_DH)), _pad128(k_norm_w.reshape(1, ATT_DH))
    gq, gk = _group_sum_matrix(ATT_QW), _group_sum_matrix(ATT_KVW)
    tq, tk = _tile_matrix(ATT_QW), _tile_matrix(ATT_KVW)
    prep_in = [_rin(u, tok, ATT_QW, 5), _rin(u, tok, ATT_KVW, 24), _rin(cq, tok), _rin(sq, tok), _rin(ck, tok),
               _rin(sk, tok)]
    prep_consts = [qw, kw, gq, gk, tq, tk]

    def att_prep_fn(aq, ak, av, *rest):
        q, k = _att_prep_fn(aq, ak, *rest)
        return (_head_t(q, ATT_HEADS), _head_s(q, ATT_HEADS), _head_s(k, ATT_KV), _head_t(k, ATT_KV),
                _head_t(av, ATT_KV)), ()

    q_t, q_s, k_s, k_t, v_t = _rows(
        att_prep_fn, "att_prep", t, tok, prep_in[:2] + [_rin(u, tok, ATT_KVW, 25)] + prep_in[2:], prep_consts,
        [_tout(ATT_HEADS, nt, ATT_DH, tok, BF16), _tout(ATT_HEADS, nt, tok, LANES, BF16),
         _tout(ATT_KV, nt, tok, LANES, BF16), _tout(ATT_KV, nt, ATT_DH, tok, BF16),
         _tout(ATT_KV, nt, ATT_DH, tok, BF16)])
    q_t = q_t.reshape(ATT_KV, ns, ATT_DH, tok)
    q_s = q_s.reshape(ATT_KV, ns, tok, LANES)
    k_s = k_s.reshape(ATT_KV, t, LANES)
    o_t, lse, g_out, g_gu, g_down = fa_fwd(q_t, k_s, v_t, ride=(late_w, False))
    wf_gu = g_gu.transpose(1, 0, 2).reshape(g_gu.shape[1], -1)
    wf_out = g_out.reshape(-1, g_out.shape[2])
    wf_down = g_down.reshape(-1, g_down.shape[2])
    o_tiles = o_t.reshape(ATT_HEADS, nt, ATT_DH, tok)

    hgw = hg_norm_w.reshape(1, HG_D)
    attw = att_norm_w.reshape(1, ATT_QW)
    mix_in = [_rin(o_f, tok), _rin(o_b, tok), _rin(u, tok, HG_W, 4), _tin(o_tiles)]
    (mix,) = _rows(lambda of, ob, ug, ot, hw, aw: ((_post_mix_fn(of, ob, ug, _from_head_t(ot), hw, aw),), ()),
                   "post_mix", t, tok, mix_in, [hgw, attw], [_rout(t, tok, d, BF16)])
    x1, h2 = mm_res_norm(mix, wf_out, x2d, n2, "mm_out")
    gu2, act = ffn_up(h2, wf_gu)

    dx2, dx2b, g_final, loss_part = mm_res_loss(act, wf_down, x1, tgt, nf, "mm_down_loss")
    dgu2 = ffn_down_dx(dx2b, wf_down, gu2)
    gw_down = mm_tn(dx2b, act, "mm_down_dw", transpose_out=True)
    dx1, dx1b, g_norm2 = mm_nt_norm_bwd(dgu2, wf_gu, x1, n2, dx2, "mm_gate_up_dx", True)
    gw_gu = mm_tn(h2, dgu2, "mm_gate_up_dw", col_shards=N_CHIPS)
    dmix = mm_rows(dx1b, wf_out, "mm_out_dx", trans_b=True)
    gw_out = mm_tn(mix, dx1b, "mm_out_dw")

    def post_mix_bwd_fn(of, ob, ug, ot, dm, hgw_, attw_):
        _, vjp = jax.vjp(_post_mix_fn, of, ob, ug, _from_head_t(ot), hgw_, attw_)
        dof, _, dug, doa, dhgw, dattw = vjp(dm)
        return (dof, dug, _head_t(doa, ATT_HEADS), _head_s(doa, ATT_HEADS)), (dhgw, dattw)

    do_hg, du_g, do_t, do_s, g_hg, g_att = _rows(
        post_mix_bwd_fn, "post_mix_bwd", t, tok, mix_in + [_rin(dmix, tok)], [hgw, attw],
        [_rout(t, tok, HG_W, F32), _rout(t, tok, HG_W, BF16), _tout(ATT_HEADS, nt, ATT_DH, tok, BF16),
         _tout(ATT_HEADS, nt, tok, LANES, BF16)], [HG_D, ATT_QW])
    late = ["w_out", "w_gate_up", "w_down"]
    late_g = [gw_out.reshape(N_CHIPS, -1, d), gw_gu, gw_down.reshape(N_CHIPS, -1, d)]
    late_part = [add_my_half(g, r, "add_my_half_" + n) for g, r, n in zip(late_g, rs_siblings(late_g, "rs_siblings_late"), late)]
    dq_t, dk_t, dv_t, *late_parts = fa_bwd(q_s, q_t, do_s.reshape(q_s.shape), do_t.reshape(q_t.shape), o_t, lse, k_s, k_t,
                                           v_t, ride=(late_part, True))

    def att_prep_bwd_fn(aq, ak, cq_, sq_, ck_, sk_, dqt, dkt, dvt, qw_, kw_, gq_, gk_, tq_, tk_):
        _, vjp = jax.vjp(lambda a, b, c_, e: _att_prep_fn(a, b, cq_, sq_, ck_, sk_, c_, e, gq_, gk_, tq_, tk_),
                         aq, ak, qw_, kw_)
        daq, dak, dqw, dkw = vjp((_from_head_t(dqt), _from_head_t(dkt)))
        return (daq, dak, _from_head_t(dvt)), (dqw, dkw)

    da_q, da_k, da_v, g_q, g_k = _rows(
        att_prep_bwd_fn, "att_prep_bwd", t, tok,
        prep_in + [_tin(dq_t.reshape(ATT_HEADS, nt, ATT_DH, tok)), _tin(dk_t), _tin(dv_t)], prep_consts,
        [_rout(t, tok, ATT_QW, BF16), _rout(t, tok, ATT_KVW, BF16), _rout(t, tok, ATT_KVW, BF16)], [LANES, LANES])

    dq_f, df_f, di_f, dl_f0, dl_f1 = gla_bwd(u, l_f0, l_f1, ss_f, do_hg, 1, False, "gla_bwd_f")
    dq_b, df_b, di_b, dl_b0, dl_b1 = gla_bwd(u, l_b0, l_b1, ss_b, do_hg, 2, True, "gla_bwd_b")

    def assemble_fn(qf, qb, ff, fb, i_f, i_b, dg, aq, ak, av):
        parts = [qf + qb, ff, fb, i_f + i_b, dg.astype(F32), aq.astype(F32), ak.astype(F32), av.astype(F32)]
        return (jnp.concatenate(parts, axis=1),), ()

    (du,) = _rows(assemble_fn, "assemble_du", t, tok,
                  [_rin(a, tok) for a in (dq_f, dq_b, df_f, df_b, di_f, di_b, du_g, da_q, da_k, da_v)], [],
                  [_rout(t, tok, u.shape[1], BF16)])
    grad_x, g_norm1 = mm_nt_norm_bwd(du, wf_in, x2d, n1, dx1, "mm_in_dx", False)
    gw_in_t = mm_tn(h1, du, "mm_in_dw", transpose_out=True)

    names = ["w_in"] + late
    g_in4 = gw_in_t.reshape(N_CHIPS, -1, d)
    in_part = add_my_half(g_in4, rs_siblings([g_in4], "rs_siblings_w_in")[0], "add_my_half_w_in")
    parts = list(exchange([in_part], True, "rs_chips_w_in")) + late_parts
    g_shard = [sum_chips(p, "sum_chips_" + n) for p, n in zip(parts, names)]
    g_shard[0] = g_shard[0].T
    big = {}
    for n, g, w, m, v in zip(names, g_shard, (w_in, w_out, w_gate_up, w_down), (m_w_in, m_w_out, m_w_gate_up, m_w_down),
                             (v_w_in, v_w_out, v_w_gate_up, v_w_down)):
        dlt, mn, vn = adamw(w[0], g, m[0], v[0], "adamw_" + n)
        big[n] = (g[None], dlt[None], mn[None], vn[None])

    small = [g_norm1, g_norm2, g_final, g_att, g_hg, g_q, g_k, loss_part, dl_f0, dl_f1, dl_b0, dl_b1]
    packed = _pack_rows(small)
    packed += [jnp.zeros((1, 1024), F32)] * (8 - len(packed))
    tot = allreduce_small(jnp.concatenate(packed, axis=0), "allreduce_small")
    s_norm1, s_norm2, s_final = tot[0:1], tot[1:2], tot[2:3]
    s_att, s_hg, s_q, s_k = tot[3:4, 0:512], tot[3:4, 512:640], tot[3:4, 640:704], tot[3:4, 768:832]
    loss = tot[3, 896]
    s_lb = jnp.concatenate([tot[4:5, 0:512], tot[4:5, 512:1024], tot[5:6, 0:512], tot[5:6, 512:1024]], axis=0)
    s_lb = lax.dynamic_slice(s_lb, (0, chip * LANES), (4, LANES)).reshape(1, 512)

    snames = ["norm1_w", "lb_logits", "hg_norm_w", "q_norm_w", "k_norm_w", "att_norm_w", "norm2_w", "final_norm_w"]
    g_small = dict(zip(snames, [s_norm1, s_lb, s_hg, s_q, s_k, s_att, s_norm2, s_final]))
    w_small = dict(zip(snames, [norm1_w, lb_logits, hg_norm_w, q_norm_w, k_norm_w, att_norm_w, norm2_w, final_norm_w]))
    m_small = dict(zip(snames, [m_norm1_w, m_lb_logits, m_hg_norm_w, m_q_norm_w, m_k_norm_w, m_att_norm_w, m_norm2_w, m_final_norm_w]))
    v_small = dict(zip(snames, [v_norm1_w, v_lb_logits, v_hg_norm_w, v_q_norm_w, v_k_norm_w, v_att_norm_w, v_norm2_w, v_final_norm_w]))

    def pack_small(tree):
        rows = _pack_rows([tree[n].reshape(1, -1) for n in snames])
        rows += [jnp.zeros((1, 1024), F32)] * (8 - len(rows))
        return jnp.concatenate(rows, axis=0)

    d_s, m_s, v_s = adamw(pack_small(w_small), pack_small(g_small), pack_small(m_small), pack_small(v_small), "adamw_small")

    def unpack_small(a):
        out, r, used = {}, 0, 0
        for n in snames:
            size = w_small[n].size
            if used + size > 1024:
                r, used = r + 1, 0
            out[n] = a[r, used:used + size].reshape(w_small[n].shape)
            used += size
        return out

    d_sm, m_sm, v_sm = unpack_small(d_s), unpack_small(m_s), unpack_small(v_s)
    g_sm = {n: g_small[n].reshape(w_small[n].shape) for n in snames}

    order = ["norm1_w", "w_in", "lb_logits", "hg_norm_w", "q_norm_w", "k_norm_w", "att_norm_w", "w_out", "norm2_w",
             "w_gate_up", "w_down", "final_norm_w"]

    def pick(small_tree, idx):
        return [big[n][idx] if n in big else small_tree[n] for n in order]

    return (loss, grad_x.reshape(x.shape), *pick(g_sm, 0), *pick(d_sm, 1), *pick(m_sm, 2), *pick(v_sm, 3))
```

```python
import functools

import numpy as np
import jax
import jax.numpy as jnp
from jax import lax
from jax.experimental import pallas as pl
from jax.experimental.pallas import tpu as pltpu

F32 = jnp.float32
BF16 = jnp.bfloat16
MESH = pl.DeviceIdType.MESH

EPS = 1e-6
GRID_W = 64
HG_HEADS = 4
HG_D = 128
HG_W = HG_HEADS * HG_D
CHUNK = 64
ATT_HEADS = 8
ATT_KV = 2
ATT_GROUP = ATT_HEADS // ATT_KV
ATT_DH = 64
ATT_QW = ATT_HEADS * ATT_DH
ATT_KVW = ATT_KV * ATT_DH
ROPE_THETA = 10000.0
N_CHIPS = 4
N_DEV = 8

ADAM_LR = 0.001
ADAM_B1 = 0.9
ADAM_B2 = 0.999
ADAM_EPS = 1e-08
ADAM_WD = 0.01
ADAM_STEP = 10

VMEM_LIMIT = 52 * 1024 * 1024
LANES = 128
TOK = 256


def _params(sem=None):
    return pltpu.CompilerParams(dimension_semantics=sem, vmem_limit_bytes=VMEM_LIMIT)


def _dg(a, b, ca, cb):
    return lax.dot_general(a.astype(BF16), b.astype(BF16), (((ca,), (cb,)), ((), ())),
                           preferred_element_type=F32)


@jax.custom_vjp
def dot_nn(a, b):
    return _dg(a, b, 1, 0)


def _dot_nn_fwd(a, b):
    return _dg(a, b, 1, 0), (a, b)


def _dot_nn_bwd(res, g):
    a, b = res
    return _dg(g, b, 1, 1), _dg(a, g, 0, 0)


dot_nn.defvjp(_dot_nn_fwd, _dot_nn_bwd)


@jax.custom_vjp
def dot_nt(a, b):
    return _dg(a, b, 1, 1)


def _dot_nt_fwd(a, b):
    return _dg(a, b, 1, 1), (a, b)


def _dot_nt_bwd(res, g):
    a, b = res
    return _dg(g, b, 1, 0), _dg(g, a, 0, 0)


dot_nt.defvjp(_dot_nt_fwd, _dot_nt_bwd)


@jax.custom_vjp
def dot_tn(a, b):
    return _dg(a, b, 0, 0)


def _dot_tn_fwd(a, b):
    return _dg(a, b, 0, 0), (a, b)


def _dot_tn_bwd(res, g):
    a, b = res
    return _dg(b, g, 1, 1), _dg(a, g, 1, 0)


dot_tn.defvjp(_dot_tn_fwd, _dot_tn_bwd)


def _split3(a):
    hi = a.astype(BF16)
    r1 = a - hi.astype(F32)
    mid = r1.astype(BF16)
    lo = (r1 - mid.astype(F32)).astype(BF16)
    return lo, mid, hi


def _sum3(terms):
    lo, mid, hi = terms
    return (lo + mid) + hi


@jax.custom_vjp
def xdot_r(a, m):
    return _sum3([_dg(p, m, 1, 0) for p in _split3(a)])


def _xdot_r_fwd(a, m):
    return xdot_r(a, m), m


def _xdot_r_bwd(m, g):
    return _sum3([_dg(p, m, 1, 1) for p in _split3(g)]), jnp.zeros_like(m)


xdot_r.defvjp(_xdot_r_fwd, _xdot_r_bwd)


@jax.custom_vjp
def xdot_l(m, a):
    return _sum3([_dg(m, p, 1, 0) for p in _split3(a)])


def _xdot_l_fwd(m, a):
    return xdot_l(m, a), m


def _xdot_l_bwd(m, g):
    return jnp.zeros_like(m), _sum3([_dg(m, p, 0, 0) for p in _split3(g)])


xdot_l.defvjp(_xdot_l_fwd, _xdot_l_bwd)


@jax.custom_vjp
def swap_pairs(y):
    n = y.shape[-1]
    lane = lax.broadcasted_iota(jnp.int32, y.shape, 1)
    nxt = pltpu.roll(y, n - 1, 1)
    prv = pltpu.roll(y, 1, 1)
    return jnp.where(lane % 2 == 0, nxt, prv)


def _swap_fwd(y):
    return swap_pairs(y), None


def _swap_bwd(_, g):
    return (swap_pairs(g),)


swap_pairs.defvjp(_swap_fwd, _swap_bwd)


def _rms(x, w):
    return x * lax.rsqrt(jnp.mean(x * x, axis=-1, keepdims=True) + EPS) * w


def _sigmoid(x):
    return jax.nn.sigmoid(x)


def _rows(fn, name, t, tm, ins, consts, outs, accs=(), ride=None):
    n_r, n_c, n_o, n_a = len(ins), len(consts), len(outs), len(accs)

    def body(*refs):
        r = refs[:n_r]
        c = refs[n_r:n_r + n_c]
        o = refs[n_r + n_c:n_r + n_c + n_o]
        a = refs[n_r + n_c + n_o:]
        ro, ao = fn(*[x[...] for x in r], *[x[...] for x in c])
        for ref, val in zip(o, ro):
            if isinstance(val, (list, tuple)):
                for h, piece in enumerate(val):
                    ref[h, 0] = piece.astype(ref.dtype)
            else:
                ref[...] = val.astype(ref.dtype)
        if n_a:
            @pl.when(pl.program_id(0) == 0)
            def _():
                for ref in a:
                    ref[...] = jnp.zeros(ref.shape, F32)
            for ref, val in zip(a, ao):
                ref[...] += val

    in_specs = [s for _, s in ins]
    in_specs += [pl.BlockSpec(c.shape, lambda i, nd=c.ndim: (0,) * nd) for c in consts]
    out_specs = [s for _, s in outs] + [pl.BlockSpec((1, w), lambda i: (0, 0)) for w in accs]
    out_shape = [s for s, _ in outs] + [jax.ShapeDtypeStruct((1, w), F32) for w in accs]
    grid = (t // tm,)
    body, r_in, r_out, r_shapes, r_sems, r_args = _riding(body, n_r + n_c, n_o + n_a, grid, ride)
    return pl.pallas_call(
        body, name=name, grid=grid, in_specs=in_specs + r_in, out_specs=out_specs + r_out,
        out_shape=out_shape + r_shapes, scratch_shapes=r_sems, compiler_params=_params(("arbitrary",)),
    )(*[a for a, _ in ins], *consts, *r_args)


def _rin(a, tm, width=None, cb=0):
    width = a.shape[1] if width is None else width
    return a, pl.BlockSpec((tm, width), lambda i, cb=cb: (i, cb))


def _rout(t, tm, width, dtype):
    return jax.ShapeDtypeStruct((t, width), dtype), pl.BlockSpec((tm, width), lambda i: (i, 0))


def _tin(a):
    return a, pl.BlockSpec((a.shape[0], 1) + a.shape[2:], lambda i: (0, i, 0, 0))


def _tout(heads, nt, r, c, dtype):
    return jax.ShapeDtypeStruct((heads, nt, r, c), dtype), pl.BlockSpec((heads, 1, r, c), lambda i: (0, i, 0, 0))


def _pick(n, cap, mult):
    best = None
    for d in range(mult, min(n, cap) + 1, mult):
        if n % d == 0:
            best = d
    return best if best is not None else n


def mm_rows(a, b, name, trans_b=False, res=None, out_dtype=F32):
    m, k = a.shape
    n = b.shape[0] if trans_b else b.shape[1]
    tn = _pick(n, 3328, LANES)
    tm = _pick(m, 512 if k <= 3072 else 256, 8)
    has_res = res is not None

    def body(*refs):
        if has_res:
            a_ref, b_ref, r_ref, o_ref = refs
        else:
            a_ref, b_ref, o_ref = refs
        acc = _dg(a_ref[...], b_ref[...], 1, 1 if trans_b else 0)
        if has_res:
            acc = acc + r_ref[...]
        o_ref[...] = acc.astype(o_ref.dtype)

    in_specs = [pl.BlockSpec((tm, k), lambda j, i: (i, 0))]
    if trans_b:
        in_specs.append(pl.BlockSpec((tn, k), lambda j, i: (j, 0)))
    else:
        in_specs.append(pl.BlockSpec((k, tn), lambda j, i: (0, j)))
    args = [a, b]
    if has_res:
        in_specs.append(pl.BlockSpec((tm, tn), lambda j, i: (i, j)))
        args.append(res)
    return pl.pallas_call(
        body, name=name, grid=(n // tn, m // tm), in_specs=in_specs,
        out_specs=pl.BlockSpec((tm, tn), lambda j, i: (i, j)),
        out_shape=jax.ShapeDtypeStruct((m, n), out_dtype),
        compiler_params=_params(("arbitrary", "arbitrary")),
    )(*args)


def mm_tn(a, b, name, col_shards=None, transpose_out=False):
    t, m = a.shape
    parts = b.shape[0] if b.ndim == 3 else 1
    n = parts * b.shape[-1]
    tm = _pick(m, 512, LANES)
    width = n // col_shards if col_shards else None
    group = max(1, 3328 // width) if col_shards else 1
    tn = width * group if col_shards else _pick(n, 3328, LANES)
    tk = _pick(t, 1024, 8)
    nkk = t // tk
    per_part = b.shape[-1] // tn

    def body(a_ref, b_ref, o_ref, *scratch):
        acc_ref = scratch[0] if transpose_out else o_ref

        @pl.when(pl.program_id(2) == 0)
        def _():
            acc_ref[...] = jnp.zeros(acc_ref.shape, F32)
        acc = _dg(a_ref[...], b_ref[0] if b.ndim == 3 else b_ref[...], 0, 0)
        if col_shards:
            for s in range(group):
                acc_ref[s] += acc[:, s * width:(s + 1) * width]
        else:
            acc_ref[...] += acc
        if transpose_out:
            @pl.when(pl.program_id(2) == nkk - 1)
            def _():
                o_ref[...] = acc_ref[...].T

    if col_shards:
        out_spec = pl.BlockSpec((group, tm, width), lambda i, j, kk: (j, i, 0))
        out_shape = jax.ShapeDtypeStruct((col_shards, m, width), F32)
    elif transpose_out:
        out_spec = pl.BlockSpec((tn, tm), lambda i, j, kk: (j, i))
        out_shape = jax.ShapeDtypeStruct((n, m), F32)
    else:
        out_spec = pl.BlockSpec((tm, tn), lambda i, j, kk: (i, j))
        out_shape = jax.ShapeDtypeStruct((m, n), F32)
    if b.ndim == 3:
        b_spec = pl.BlockSpec((1, tk, tn), lambda i, j, kk: (j // per_part, kk, j % per_part))
    else:
        b_spec = pl.BlockSpec((tk, tn), lambda i, j, kk: (kk, j))
    return pl.pallas_call(
        body, name=name, grid=(m // tm, n // tn, nkk),
        in_specs=[pl.BlockSpec((tk, tm), lambda i, j, kk: (kk, i)), b_spec],
        out_specs=out_spec, out_shape=out_shape,
        scratch_shapes=[pltpu.VMEM((tm, tn), F32)] if transpose_out else [],
        compiler_params=_params(("arbitrary", "arbitrary", "arbitrary")),
    )(a, b)


def ffn_up(h2, w_gu):
    t, k = h2.shape
    dff = w_gu.shape[1] // 2
    tn = _pick(dff, 1408, LANES)
    ncol = dff // tn
    tm = _pick(t, 512, 8)

    def body(a_ref, wg_ref, wu_ref, gu_ref, act_ref):
        a = a_ref[...]
        g = _dg(a, wg_ref[...], 1, 0)
        u = _dg(a, wu_ref[...], 1, 0)
        gu_ref[0] = g.astype(gu_ref.dtype)
        gu_ref[1] = u.astype(gu_ref.dtype)
        act_ref[...] = _swiglu_fn(g, u).astype(act_ref.dtype)

    return pl.pallas_call(
        body, name="ffn_up", grid=(ncol, t // tm),
        in_specs=[pl.BlockSpec((tm, k), lambda s, i: (i, 0)), pl.BlockSpec((k, tn), lambda s, i: (0, s)),
                  pl.BlockSpec((k, tn), lambda s, i: (0, s + ncol))],
        out_specs=[pl.BlockSpec((2, tm, tn), lambda s, i: (0, i, s)), pl.BlockSpec((tm, tn), lambda s, i: (i, s))],
        out_shape=[jax.ShapeDtypeStruct((2, t, dff), BF16), jax.ShapeDtypeStruct((t, dff), BF16)],
        compiler_params=_params(("arbitrary", "arbitrary")),
    )(h2, w_gu, w_gu)


def ffn_down_dx(dx2b, w_down, gu2):
    t, k = dx2b.shape
    dff = w_down.shape[0]
    tn = _pick(dff, 1408, LANES)
    tm = _pick(t, 512, 8)

    def body(a_ref, w_ref, gu_ref, o_ref):
        dact = _dg(a_ref[...], w_ref[...], 1, 1)
        _, vjp = jax.vjp(_swiglu_fn, gu_ref[0].astype(F32), gu_ref[1].astype(F32))
        dg, du = vjp(dact)
        o_ref[0] = dg.astype(o_ref.dtype)
        o_ref[1] = du.astype(o_ref.dtype)

    blk = pl.BlockSpec((2, tm, tn), lambda s, i: (0, i, s))
    return pl.pallas_call(
        body, name="ffn_down_dx", grid=(dff // tn, t // tm),
        in_specs=[pl.BlockSpec((tm, k), lambda s, i: (i, 0)), pl.BlockSpec((tn, k), lambda s, i: (s, 0)), blk],
        out_specs=blk, out_shape=jax.ShapeDtypeStruct((2, t, dff), BF16),
        compiler_params=_params(("arbitrary", "arbitrary")),
    )(dx2b, w_down, gu2)


def mm_res_norm(a, b, res, w, name):
    t, k = a.shape
    d = b.shape[1]
    tm = _pick(t, 512, 8)

    def body(a_ref, b_ref, r_ref, w_ref, x_ref, h_ref):
        xv = _dg(a_ref[...], b_ref[...], 1, 0) + r_ref[...]
        x_ref[...] = xv
        h_ref[...] = _rms(xv, w_ref[...]).astype(h_ref.dtype)

    row = pl.BlockSpec((tm, d), lambda i: (i, 0))
    return pl.pallas_call(
        body, name=name, grid=(t // tm,),
        in_specs=[pl.BlockSpec((tm, k), lambda i: (i, 0)), pl.BlockSpec((k, d), lambda i: (0, 0)), row,
                  pl.BlockSpec((1, d), lambda i: (0, 0))],
        out_specs=[row, row],
        out_shape=[jax.ShapeDtypeStruct((t, d), F32), jax.ShapeDtypeStruct((t, d), BF16)],
        compiler_params=_params(("arbitrary",)),
    )(a, b, res, w)


def mm_res_loss(a, b, res, tgt, w, name):
    t, k = a.shape
    d = b.shape[1]
    tm = _pick(t, 256, 8)

    def body(a_ref, b_ref, r_ref, t_ref, w_ref, dx_ref, dxb_ref, dw_ref, loss_ref):
        @pl.when(pl.program_id(0) == 0)
        def _():
            dw_ref[...] = jnp.zeros(dw_ref.shape, F32)
            loss_ref[...] = jnp.zeros(loss_ref.shape, F32)
        xv = _dg(a_ref[...], b_ref[...], 1, 0) + r_ref[...]
        tg = t_ref[...]
        val, vjp = jax.vjp(lambda x_, w_: _loss_fn(x_, w_, tg), xv, w_ref[...])
        dx, dw = vjp(jnp.ones((1, 1), F32))
        dx_ref[...] = dx
        dxb_ref[...] = dx.astype(dxb_ref.dtype)
        dw_ref[...] += dw
        loss_ref[...] += jnp.broadcast_to(val, (1, LANES))

    row = pl.BlockSpec((tm, d), lambda i: (i, 0))
    vec = pl.BlockSpec((1, d), lambda i: (0, 0))
    return pl.pallas_call(
        body, name=name, grid=(t // tm,),
        in_specs=[pl.BlockSpec((tm, k), lambda i: (i, 0)), pl.BlockSpec((k, d), lambda i: (0, 0)), row, row, vec],
        out_specs=[row, row, vec, pl.BlockSpec((1, LANES), lambda i: (0, 0))],
        out_shape=[jax.ShapeDtypeStruct((t, d), F32), jax.ShapeDtypeStruct((t, d), BF16),
                   jax.ShapeDtypeStruct((1, d), F32), jax.ShapeDtypeStruct((1, LANES), F32)],
        compiler_params=_params(("arbitrary",)),
    )(a, b, res, tgt, w)


def mm_nt_norm_bwd(a, wmat, x, w, dres, name, with_bf16):
    parts = a.shape[0] if a.ndim == 3 else 1
    t, kp = a.shape[-2], a.shape[-1]
    d = wmat.shape[0]
    tm = _pick(t, 256, 8)

    def body(*refs):
        a_ref, w_refs = refs[0], refs[1:1 + parts]
        x_ref, nw_ref, r_ref, dx_ref = refs[1 + parts:5 + parts]
        dw_ref = refs[-1]

        @pl.when(pl.program_id(0) == 0)
        def _():
            dw_ref[...] = jnp.zeros(dw_ref.shape, F32)
        dh = None
        for p in range(parts):
            term = _dg(a_ref[p] if a.ndim == 3 else a_ref[...], w_refs[p][...], 1, 1)
            dh = term if dh is None else dh + term
        _, vjp = jax.vjp(_rms, x_ref[...], nw_ref[...])
        dx, dw = vjp(dh)
        dx = dx + r_ref[...]
        dx_ref[...] = dx
        if with_bf16:
            refs[5 + parts][...] = dx.astype(BF16)
        dw_ref[...] += dw

    row = pl.BlockSpec((tm, d), lambda i: (i, 0))
    vec = pl.BlockSpec((1, d), lambda i: (0, 0))
    a_spec = pl.BlockSpec((parts, tm, kp), lambda i: (0, i, 0)) if a.ndim == 3 else pl.BlockSpec((tm, kp), lambda i: (i, 0))
    w_specs = [pl.BlockSpec((d, kp), lambda i, p=p: (0, p)) for p in range(parts)]
    outs = [row] + ([row] if with_bf16 else []) + [vec]
    shapes = [jax.ShapeDtypeStruct((t, d), F32)] + ([jax.ShapeDtypeStruct((t, d), BF16)] if with_bf16 else [])
    return pl.pallas_call(
        body, name=name, grid=(t // tm,),
        in_specs=[a_spec] + w_specs + [row, vec, row],
        out_specs=outs, out_shape=shapes + [jax.ShapeDtypeStruct((1, d), F32)],
        compiler_params=_params(("arbitrary",)),
    )(a, *([wmat] * parts), x, w, dres)


def _lower_bound(l0, l1):
    m = jnp.maximum(l0, l1)
    e0 = jnp.exp(l0 - m)
    e1 = jnp.exp(l1 - m)
    return e0 / (e0 + e1)


def _gla_consts(rev):
    ri = lax.broadcasted_iota(jnp.int32, (CHUNK, CHUNK), 0)
    ci = lax.broadcasted_iota(jnp.int32, (CHUNK, CHUNK), 1)
    keep = (ci >= ri) if rev else (ci <= ri)
    ref_mask = lax.broadcasted_iota(jnp.int32, (CHUNK, 1), 0) == (CHUNK // 2 if rev else CHUNK // 2 - 1)
    return keep, ref_mask


def _gla_block(uq, uf, ui, l0, l1, st_in, rev):
    ncb = uq.shape[0] // CHUNK
    heads = range(HG_HEADS)
    keep, ref_mask = _gla_consts(rev)
    cum = keep.astype(F32)
    lb = _lower_bound(l0, l1)
    q = uq * _sigmoid(uq)
    k = (1.0 - lb) * _sigmoid(-uf)
    g = jnp.log(lb + (1.0 - lb) * _sigmoid(uf))

    def rows(a, c):
        return a[c * CHUNK:(c + 1) * CHUNK]

    def head(a, h):
        return a[:, h * HG_D:(h + 1) * HG_D]

    bs = [xdot_l(cum, rows(g, c)) for c in range(ncb)]
    q_in, k_in, q_b, k_d, decay = [], [], [], [], []
    for c in range(ncb):
        b = bs[c]
        b_ref = jnp.sum(jnp.where(ref_mask, b, 0.0), axis=0, keepdims=True)
        b_last = jnp.sum(rows(g, c), axis=0, keepdims=True)
        qc, kc = rows(q, c), rows(k, c)
        q_in.append(qc * jnp.exp(b - b_ref))
        k_in.append(kc * jnp.exp(b_ref - b))
        q_b.append(qc * jnp.exp(b))
        k_d.append(kc * jnp.exp(b_last - b))
        decay.append(jnp.exp(b_last))
    scores = [[jnp.where(keep, dot_nt(head(q_in[c], h), head(k_in[c], h)), 0.0) for h in heads] for c in range(ncb)]
    o_intra = [[dot_nn(scores[c][h], head(rows(ui, c), h)) for h in heads] for c in range(ncb)]
    contrib = [[dot_tn(head(rows(ui, c), h), head(k_d[c], h)) for h in heads] for c in range(ncb)]
    st = list(st_in)
    o_rows = [None] * ncb
    for c in (reversed(range(ncb)) if rev else range(ncb)):
        parts = []
        for h in heads:
            parts.append(o_intra[c][h] + dot_nt(head(q_b[c], h), st[h]))
            st[h] = st[h] * head(decay[c], h) + contrib[c][h]
        o_rows[c] = jnp.concatenate(parts, axis=1)
    return jnp.concatenate(o_rows, axis=0), tuple(st)


def _gla_blocks(t):
    tb = min(512, t)
    return tb, t // tb


def gla_fwd(u, l0, l1, fcol, rev, name):
    t = u.shape[0]
    tb, nb = _gla_blocks(t)

    def blk(i):
        return (nb - 1 - i) if rev else i

    def body(uq_ref, uf_ref, ui_ref, l0_ref, l1_ref, o_ref, ss_ref, st_ref):
        @pl.when(pl.program_id(0) == 0)
        def _():
            st_ref[...] = jnp.zeros(st_ref.shape, F32)
        ss_ref[0] = st_ref[...]
        o, st_out = _gla_block(uq_ref[...], uf_ref[...], ui_ref[...], l0_ref[...], l1_ref[...],
                               tuple(st_ref[h] for h in range(HG_HEADS)), rev)
        o_ref[...] = o
        for h in range(HG_HEADS):
            st_ref[h] = st_out[h]

    row = lambda cb: pl.BlockSpec((tb, HG_W), lambda i: (blk(i), cb))
    vec = pl.BlockSpec((1, HG_W), lambda i: (0, 0))
    return pl.pallas_call(
        body, name=name, grid=(nb,),
        in_specs=[row(0), row(fcol), row(3), vec, vec],
        out_specs=[pl.BlockSpec((tb, HG_W), lambda i: (blk(i), 0)),
                   pl.BlockSpec((1, HG_HEADS, HG_D, HG_D), lambda i: (blk(i), 0, 0, 0))],
        out_shape=[jax.ShapeDtypeStruct((t, HG_W), F32),
                   jax.ShapeDtypeStruct((nb, HG_HEADS, HG_D, HG_D), F32)],
        scratch_shapes=[pltpu.VMEM((HG_HEADS, HG_D, HG_D), F32)],
        compiler_params=_params(("arbitrary",)),
    )(u, u, u, l0, l1)


def gla_bwd(u, l0, l1, ss, do, fcol, rev, name):
    t = u.shape[0]
    tb, nb = _gla_blocks(t)

    def blk(i):
        return i if rev else (nb - 1 - i)

    def body(uq_ref, uf_ref, ui_ref, l0_ref, l1_ref, ss_ref, do_ref,
             dq_ref, df_ref, di_ref, dl0_ref, dl1_ref, dst_ref):
        @pl.when(pl.program_id(0) == 0)
        def _():
            dst_ref[...] = jnp.zeros(dst_ref.shape, F32)
            dl0_ref[...] = jnp.zeros(dl0_ref.shape, F32)
            dl1_ref[...] = jnp.zeros(dl1_ref.shape, F32)
        heads = range(HG_HEADS)
        _, vjp = jax.vjp(functools.partial(_gla_block, rev=rev), uq_ref[...], uf_ref[...], ui_ref[...],
                         l0_ref[...], l1_ref[...], tuple(ss_ref[0, h] for h in heads))
        dq, df, di, dl0, dl1, dst = vjp((do_ref[...], tuple(dst_ref[h] for h in heads)))
        dq_ref[...] = dq
        df_ref[...] = df
        di_ref[...] = di
        dl0_ref[...] += dl0
        dl1_ref[...] += dl1
        for h in heads:
            dst_ref[h] = dst[h]

    row = lambda cb: pl.BlockSpec((tb, HG_W), lambda i: (blk(i), cb))
    vec = pl.BlockSpec((1, HG_W), lambda i: (0, 0))
    orow = pl.BlockSpec((tb, HG_W), lambda i: (blk(i), 0))
    return pl.pallas_call(
        body, name=name, grid=(nb,),
        in_specs=[row(0), row(fcol), row(3), vec, vec,
                  pl.BlockSpec((1, HG_HEADS, HG_D, HG_D), lambda i: (blk(i), 0, 0, 0)), orow],
        out_specs=[orow, orow, orow, vec, vec],
        out_shape=[jax.ShapeDtypeStruct((t, HG_W), F32)] * 3 + [jax.ShapeDtypeStruct((1, HG_W), F32)] * 2,
        scratch_shapes=[pltpu.VMEM((HG_HEADS, HG_D, HG_D), F32)],
        compiler_params=_params(("arbitrary",)),
    )(u, u, u, l0, l1, ss, do)


def _rope_tables(t):
    rows = t // GRID_W
    row = jnp.repeat(jnp.arange(rows), GRID_W).astype(F32)
    col = jnp.tile(jnp.arange(GRID_W), rows).astype(F32)
    axis_dim = ATT_DH // 2
    freqs = ROPE_THETA ** (-jnp.arange(0, axis_dim, 2, dtype=F32) / axis_dim)
    ang = jnp.concatenate([row[:, None] * freqs, col[:, None] * freqs], axis=-1)
    cos2 = jnp.repeat(jnp.cos(ang), 2, axis=-1)
    sin2 = jnp.repeat(jnp.sin(ang), 2, axis=-1) * jnp.tile(jnp.array([-1.0, 1.0], F32), ATT_DH // 2)
    return cos2, sin2


def _group_sum_matrix(width):
    idx = np.arange(width) // ATT_DH
    return jnp.asarray((idx[:, None] == idx[None, :]).astype(np.float32))


def _tile_matrix(width):
    m = np.zeros((LANES, width), np.float32)
    m[np.arange(width) % ATT_DH, np.arange(width)] = 1.0
    return jnp.asarray(m)


def _tile_w(w128, tile_m):
    w8 = jnp.broadcast_to(w128, (8, LANES))
    return jnp.sum(xdot_r(w8, tile_m), axis=0, keepdims=True) * 0.125


def _head_norm_rope(a, w128, cos_t, sin_t, gsum, tile_m, scale):
    ssq = xdot_r(a * a, gsum)
    y = a * lax.rsqrt(ssq * (1.0 / ATT_DH) + EPS) * _tile_w(w128, tile_m)
    return (y * cos_t + swap_pairs(y) * sin_t) * scale


def _att_prep_fn(aq, ak, cq, sq, ck, sk, qw, kw, gq, gk, tq, tk):
    q = _head_norm_rope(aq, qw, cq, sq, gq, tq, ATT_DH ** -0.5)
    k = _head_norm_rope(ak, kw, ck, sk, gk, tk, 1.0)
    return q, k


def _head_t(x, heads):
    xt = x.T
    return [xt[h * ATT_DH:(h + 1) * ATT_DH] for h in range(heads)]


def _head_s(x, heads):
    lane = lax.broadcasted_iota(jnp.int32, (x.shape[0], LANES), 1)
    out = []
    for h in range(heads):
        pair = x[:, (h // 2) * LANES:(h // 2 + 1) * LANES]
        if h % 2:
            pair = pltpu.roll(pair, ATT_DH, 1)
        out.append(jnp.where(lane < ATT_DH, pair, 0.0))
    return out


def _from_head_t(tile):
    return jnp.concatenate([tile[h, 0] for h in range(tile.shape[0])], axis=0).T


def _pad_rows(a):
    return jnp.concatenate([a, jnp.zeros(a.shape, a.dtype)], axis=0)


def _col_bcast(row_vec):
    return jnp.broadcast_to(row_vec, (LANES, row_vec.shape[1])).T


FA_K_FWD = 512
FA_K_BWD = 512
FA_STRIPS_FWD = 8
FA_STRIPS_BWD = 8


def _key_block(ref, j, tiles):
    return jnp.concatenate([ref[0, j * tiles + i] for i in range(tiles)], axis=1)


def _riding(body, n_in, n_out, grid, ride):
    if ride is None:
        return body, [], [], [], [], []
    kind, arrays = ride
    n = len(arrays)
    steps = int(np.prod(grid))

    def riding_body(*refs):
        cuts = np.cumsum([0, n_in, n, n_out, n]).tolist()
        own_in, rin, own_out, rout = (refs[a:b] for a, b in zip(cuts[:-1], cuts[1:]))
        start, forward, finish = _exchange_phases(rin, rout, *refs[cuts[-1]:], kind)
        step = 0
        for axis, size in enumerate(grid):
            step = step * size + pl.program_id(axis)
        pl.when(step == 0)(start)
        pl.when(step == steps // 2)(forward)
        body(*own_in, *own_out)
        pl.when(step == steps - 1)(finish)

    return riding_body, _hbm_specs(n), _hbm_specs(n), _exchange_shapes(arrays, kind), _exchange_sems(n), list(arrays)


def fa_fwd(qt, ks, vt, ride=None):
    _, ns, dh, nq = qt.shape
    lk = ks.shape[1]
    bk = min(FA_K_FWD, lk)
    nk = lk // bk
    spg = min(FA_STRIPS_FWD, ns)
    grid = (ATT_KV, ns // spg)

    def body(q_ref, k_ref, v_ref, o_ref, lse_ref):
        qs = [_pad_rows(q_ref[0, c]) for c in range(spg)]

        def keys(j):
            return k_ref[0, j * bk:(j + 1) * bk, :]

        def step(j, carry):
            st0, stats = carry
            kb = keys(j)
            vb = _key_block(v_ref, j, bk // nq)
            sts = [st0] + [_dg(kb, qs[c], 1, 0) for c in range(1, spg)]
            out = []
            for c in range(spg):
                m, l, acc = stats[c]
                m_new = jnp.maximum(m, jnp.max(sts[c], axis=0, keepdims=True))
                alpha = jnp.exp(m - m_new)
                p = jnp.exp(sts[c] - m_new)
                l = alpha * l + jnp.sum(p, axis=0, keepdims=True)
                if c == spg - 1:
                    st0 = _dg(keys(min(j + 1, nk - 1)), qs[0], 1, 0)
                acc = alpha * acc + _dg(vb, p, 1, 0)
                out.append((m_new, l, acc))
            return st0, tuple(out)

        init = tuple((jnp.full((1, nq), -jnp.inf, F32), jnp.zeros((1, nq), F32), jnp.zeros((dh, nq), F32))
                     for _ in range(spg))
        carry = (_dg(keys(0), qs[0], 1, 0), init)
        for j in range(nk):
            carry = step(j, carry)
        _, res = carry
        for c in range(spg):
            m, l, acc = res[c]
            o_ref[0, c] = acc / l
            lse_ref[0, c] = _col_bcast(m + jnp.log(l))

    body, r_in, r_out, r_shapes, r_sems, r_args = _riding(body, 3, 2, grid, ride)
    return pl.pallas_call(
        body, name="fa_fwd", grid=grid,
        in_specs=[pl.BlockSpec((1, spg, dh, nq), lambda g, i: (g, i, 0, 0)),
                  pl.BlockSpec((1, lk, LANES), lambda g, i: (g, 0, 0)),
                  pl.BlockSpec((1, lk // nq, dh, nq), lambda g, i: (g, 0, 0, 0))] + r_in,
        out_specs=[pl.BlockSpec((1, spg, dh, nq), lambda g, i: (g, i, 0, 0)),
                   pl.BlockSpec((1, spg, nq, LANES), lambda g, i: (g, i, 0, 0))] + r_out,
        out_shape=[jax.ShapeDtypeStruct((ATT_KV, ns, dh, nq), F32),
                   jax.ShapeDtypeStruct((ATT_KV, ns, nq, LANES), F32)] + r_shapes,
        scratch_shapes=r_sems,
        compiler_params=_params(("arbitrary", "arbitrary")),
    )(qt, ks, vt, *r_args)


def fa_bwd(qs, qt, dos, dot_, ot, lse, ks, kt, vt, ride=None):
    _, ns, dh, nq = qt.shape
    lk = ks.shape[1]
    bk = min(FA_K_BWD, lk)
    nk = lk // bk
    tiles = bk // nq
    spg = min(FA_STRIPS_BWD, ns)
    nc = bk // LANES
    grid = (ATT_KV, ns // spg)

    def body(qs_ref, qt_ref, dos_ref, dot_ref, ot_ref, lse_ref, ks_ref, kt_ref, vt_ref, dq_ref, dk_ref, dv_ref):
        @pl.when(pl.program_id(1) == 0)
        def _():
            dk_ref[...] = jnp.zeros(dk_ref.shape, F32)
            dv_ref[...] = jnp.zeros(dv_ref.shape, F32)

        strips = range(spg)
        lse_b = [lse_ref[0, c] for c in strips]
        d_b = [_col_bcast(jnp.sum(dot_ref[0, c].astype(F32) * ot_ref[0, c], axis=0, keepdims=True)) for c in strips]

        def step(j, dqs):
            ktb, vtb = _pad_rows(_key_block(kt_ref, j, tiles)), _pad_rows(_key_block(vt_ref, j, tiles))
            kb = ks_ref[0, j * bk:(j + 1) * bk, :]
            prods = [(_dg(qs_ref[0, c], ktb, 1, 0), _dg(dos_ref[0, c], vtb, 1, 0)) for c in strips]
            out = []
            for c in strips:
                s, dp = prods[c]
                ps, dss = [], []
                for cc in range(nc):
                    sl = slice(cc * LANES, (cc + 1) * LANES)
                    pc = jnp.exp(s[:, sl] - lse_b[c])
                    ps.append(pc.astype(BF16))
                    dss.append((pc * (dp[:, sl] - d_b[c])).astype(BF16))
                p, ds = jnp.concatenate(ps, axis=1), jnp.concatenate(dss, axis=1)
                dv = _dg(dot_ref[0, c], p, 1, 0)
                dk = _dg(qt_ref[0, c], ds, 1, 0)
                for i in range(tiles):
                    dv_ref[0, j * tiles + i] += dv[:, i * nq:(i + 1) * nq]
                    dk_ref[0, j * tiles + i] += dk[:, i * nq:(i + 1) * nq]
                out.append(dqs[c] + _dg(ds, kb, 1, 0))
            return tuple(out)

        dqs = tuple(jnp.zeros((nq, LANES), F32) for _ in strips)
        for j in range(nk):
            dqs = step(j, dqs)
        for c in strips:
            dq_ref[0, c] = dqs[c].T[:dh]

    sspec = pl.BlockSpec((1, spg, nq, LANES), lambda g, i: (g, i, 0, 0))
    tspec = pl.BlockSpec((1, spg, dh, nq), lambda g, i: (g, i, 0, 0))
    kspec = pl.BlockSpec((1, lk // nq, dh, nq), lambda g, i: (g, 0, 0, 0))
    body, r_in, r_out, r_shapes, r_sems, r_args = _riding(body, 9, 3, grid, ride)
    return pl.pallas_call(
        body, name="fa_bwd", grid=grid,
        in_specs=[sspec, tspec, sspec, tspec, tspec, sspec, pl.BlockSpec((1, lk, LANES), lambda g, i: (g, 0, 0)),
                  kspec, kspec] + r_in,
        out_specs=[tspec, kspec, kspec] + r_out,
        out_shape=[jax.ShapeDtypeStruct((ATT_KV, ns, dh, nq), F32),
                   jax.ShapeDtypeStruct((ATT_KV, lk // nq, dh, nq), F32),
                   jax.ShapeDtypeStruct((ATT_KV, lk // nq, dh, nq), F32)] + r_shapes,
        scratch_shapes=r_sems,
        compiler_params=_params(("arbitrary", "arbitrary")),
    )(qs, qt, dos, dot_, ot, lse, ks, kt, vt, *r_args)


def _post_mix_fn(of, ob, ug, oa, hgw, attw):
    o = of + ob
    parts = []
    for h in range(HG_HEADS):
        parts.append(_rms(o[:, h * HG_D:(h + 1) * HG_D], hgw))
    hg = jnp.concatenate(parts, axis=1) * (ug * _sigmoid(ug))
    return jnp.concatenate([hg, _rms(oa, attw)], axis=1)


def _swiglu_fn(gate, up):
    return gate * _sigmoid(gate) * up


def _loss_fn(x2, w, tgt):
    e = _rms(x2, w) - tgt
    return 0.5 * jnp.sum(jnp.mean(e * e, axis=-1, keepdims=True), axis=0, keepdims=True)


def _place():
    return lax.axis_index("x"), lax.axis_index("y"), lax.axis_index("c")


def _other_chips(x, y):
    return [(1 - x, y), (x, 1 - y), (1 - x, 1 - y)]


def _hbm_specs(n):
    return [pl.BlockSpec(memory_space=pl.ANY)] * n


SEMS_PER_ARRAY = 7


def _exchange_sems(n):
    return [pltpu.SemaphoreType.DMA((SEMS_PER_ARRAY * n,)), pltpu.SemaphoreType.DMA((SEMS_PER_ARRAY * n,)),
            pltpu.SemaphoreType.DMA((n,))]


def _exchange_phases(srcs, outs, ssem, rsem, lsem, kind):
    n = len(srcs)
    x, y, c = _place()
    k = 2 * x + y
    sib = (x, y, 1 - c)
    if kind == "siblings":
        def swap(a):
            hr = srcs[a].shape[1] // 2
            return pltpu.make_async_remote_copy(src_ref=srcs[a].at[:, pl.ds((1 - c) * hr, hr), :], dst_ref=outs[a],
                                                send_sem=ssem.at[a], recv_sem=rsem.at[a], device_id=sib,
                                                device_id_type=MESH)

        def start_swaps():
            for a in range(n):
                swap(a).start()

        def finish_swaps():
            for a in range(n):
                swap(a).wait()

        return start_swaps, lambda: None, finish_swaps
    reduce = kind == "reduce"
    chips = _other_chips(x, y)
    pairs = [(a, j) for a in range(n) for j in range(3)]

    def hrows(a):
        return srcs[a].shape[1] if reduce else srcs[a].shape[0] // 2

    def half(a, kk, cc):
        return outs[a].at[kk, pl.ds(cc * hrows(a), hrows(a)), :]

    def mine(a, kk):
        return srcs[a].at[kk] if reduce else srcs[a].at[pl.ds(c * hrows(a), hrows(a)), :]

    def copy(a, j, src_ref, dst_ref, to):
        return pltpu.make_async_remote_copy(src_ref=src_ref, dst_ref=dst_ref, send_sem=ssem.at[SEMS_PER_ARRAY * a + j],
                                            recv_sem=rsem.at[SEMS_PER_ARRAY * a + j], device_id=to, device_id_type=MESH)

    def local(a):
        if reduce:
            return pltpu.make_async_copy(srcs[a].at[k], half(a, k, c), lsem.at[a])
        return pltpu.make_async_copy(srcs[a], outs[a].at[k], lsem.at[a])

    def ici(a, j, arriving):
        px, py = chips[j]
        kk = 2 * px + py
        if arriving:
            return copy(a, j, mine(a, k), half(a, kk, c), (px, py, c))
        return copy(a, j, mine(a, kk), half(a, k, c), (px, py, c))

    def passed(a, j, arriving):
        px, py = chips[j]
        kk = 2 * px + py
        return copy(a, 3 + j, half(a, kk, c), half(a, kk, (1 - c) if arriving else c), sib)

    def own(a, arriving):
        return copy(a, 6, mine(a, k), half(a, k, (1 - c) if arriving else c), sib)

    def start():
        for a in range(n):
            local(a).start()
        for a, j in pairs:
            ici(a, j, False).start()
        if reduce:
            for a in range(n):
                own(a, False).start()

    def forward():
        for a, j in pairs:
            ici(a, j, True).wait_recv()
            passed(a, j, False).start()

    def finish():
        for a in range(n):
            if reduce:
                own(a, True).wait_recv()
            for j in range(3):
                passed(a, j, True).wait_recv()
        for a, j in pairs:
            ici(a, j, False).wait_send()
            passed(a, j, False).wait_send()
        for a in range(n):
            if reduce:
                own(a, False).wait_send()
            local(a).wait()

    return start, forward, finish


def _exchange_shapes(arrays, kind):
    if kind == "siblings":
        return [jax.ShapeDtypeStruct((N_CHIPS, g.shape[1] // 2, g.shape[2]), g.dtype) for g in arrays]
    if kind == "reduce":
        return [jax.ShapeDtypeStruct((N_CHIPS, 2 * p.shape[1], p.shape[2]), p.dtype) for p in arrays]
    return [jax.ShapeDtypeStruct((N_CHIPS,) + s.shape, s.dtype) for s in arrays]


def exchange(arrays, kind, name):
    n = len(arrays)

    def body(*refs):
        start, forward, finish = _exchange_phases(refs[:n], refs[n:2 * n], *refs[2 * n:], kind)
        start()
        forward()
        finish()

    return pl.pallas_call(
        body, name=name, in_specs=_hbm_specs(n), out_specs=_hbm_specs(n),
        out_shape=_exchange_shapes(arrays, kind), scratch_shapes=_exchange_sems(n),
    )(*arrays)


def allreduce_small(p, name):
    rows, width = p.shape

    def body(p_ref, s_ref, gath, ssem, rsem):
        x, y, c = _place()
        me = 4 * x + 2 * y + c
        copies = []
        for d in range(1, N_DEV):
            dx, dy, dc = (d >> 2) & 1, (d >> 1) & 1, d & 1
            peer = (x ^ dx, y ^ dy, c ^ dc)
            cp = pltpu.make_async_remote_copy(src_ref=p_ref, dst_ref=gath.at[me], send_sem=ssem.at[d - 1],
                                              recv_sem=rsem.at[d - 1], device_id=peer, device_id_type=MESH)
            cp.start()
            copies.append(cp)
        gath[me] = p_ref[...]
        for d, cp in enumerate(copies, start=1):
            dx, dy, dc = (d >> 2) & 1, (d >> 1) & 1, d & 1
            peer_slot = 4 * (x ^ dx) + 2 * (y ^ dy) + (c ^ dc)
            pltpu.make_async_remote_copy(src_ref=p_ref, dst_ref=gath.at[peer_slot], send_sem=ssem.at[d - 1],
                                         recv_sem=rsem.at[d - 1], device_id=(x ^ dx, y ^ dy, c ^ dc),
                                         device_id_type=MESH).wait_recv()
        for cp in copies:
            cp.wait_send()
        acc = gath[0]
        for d in range(1, N_DEV):
            acc = acc + gath[d]
        s_ref[...] = acc

    return pl.pallas_call(
        body, name=name,
        in_specs=[pl.BlockSpec(memory_space=pltpu.VMEM)],
        out_specs=pl.BlockSpec(memory_space=pltpu.VMEM),
        out_shape=jax.ShapeDtypeStruct((rows, width), F32),
        scratch_shapes=[pltpu.VMEM((N_DEV, rows, width), F32), pltpu.SemaphoreType.DMA((N_DEV - 1,)),
                        pltpu.SemaphoreType.DMA((N_DEV - 1,))],
    )(p)


def add_my_half(g, recv, name):
    _, r, cols = g.shape
    hr = r // 2
    tb = _pick(hr, 512, 8)
    nb = hr // tb
    c_arr = lax.axis_index("c").astype(jnp.int32).reshape(1)

    def body(c_ref, g_ref, r_ref, o_ref):
        o_ref[...] = (g_ref[...] + r_ref[...]).astype(o_ref.dtype)

    return pl.pallas_call(
        body, name=name,
        grid_spec=pltpu.PrefetchScalarGridSpec(
            num_scalar_prefetch=1, grid=(N_CHIPS, nb),
            in_specs=[pl.BlockSpec((1, tb, cols), lambda k, i, c_ref: (k, c_ref[0] * nb + i, 0)),
                      pl.BlockSpec((1, tb, cols), lambda k, i, c_ref: (k, i, 0))],
            out_specs=pl.BlockSpec((1, tb, cols), lambda k, i, c_ref: (k, i, 0))),
        out_shape=jax.ShapeDtypeStruct((N_CHIPS, hr, cols), BF16),
        compiler_params=_params(("arbitrary", "arbitrary")),
    )(c_arr, g, recv)


def sum_chips(parts, name):
    _, hr, cols = parts.shape
    tb = _pick(hr, 512, 8)

    def body(p_ref, o_ref):
        o_ref[...] = ((p_ref[0].astype(F32) + p_ref[1].astype(F32)) + p_ref[2].astype(F32)) + p_ref[3].astype(F32)

    return pl.pallas_call(
        body, name=name, grid=(hr // tb,),
        in_specs=[pl.BlockSpec((N_CHIPS, tb, cols), lambda i: (0, i, 0))],
        out_specs=pl.BlockSpec((tb, cols), lambda i: (i, 0)),
        out_shape=jax.ShapeDtypeStruct((hr, cols), F32),
        compiler_params=_params(("arbitrary",)),
    )(parts)


def adamw(w, g, m, v, name):
    rows, width = w.shape
    tb = _pick(rows, 512, 8)

    def body(w_ref, g_ref, m_ref, v_ref, d_ref, mo_ref, vo_ref):
        gg = g_ref[...]
        m_new = ADAM_B1 * m_ref[...] + (1.0 - ADAM_B1) * gg
        v_new = ADAM_B2 * v_ref[...] + (1.0 - ADAM_B2) * (gg * gg)
        m_hat = m_new / (1.0 - ADAM_B1 ** ADAM_STEP)
        v_hat = v_new / (1.0 - ADAM_B2 ** ADAM_STEP)
        d_ref[...] = -ADAM_LR * (m_hat / (jnp.sqrt(v_hat) + ADAM_EPS) + ADAM_WD * w_ref[...])
        mo_ref[...] = m_new
        vo_ref[...] = v_new

    spec = pl.BlockSpec((tb, width), lambda i: (i, 0))
    return pl.pallas_call(
        body, name=name, grid=(rows // tb,), in_specs=[spec] * 4, out_specs=[spec] * 3,
        out_shape=[jax.ShapeDtypeStruct((rows, width), F32)] * 3,
        compiler_params=_params(("arbitrary",)),
    )(w, g, m, v)


def _pack_rows(vecs, width=1024):
    rows, cur, used = [], [], 0
    for v in vecs:
        n = v.shape[1]
        if used + n > width:
            cur.append(jnp.zeros((1, width - used), F32))
            rows.append(jnp.concatenate(cur, axis=1))
            cur, used = [], 0
        cur.append(v)
        used += n
    cur.append(jnp.zeros((1, width - used), F32))
    rows.append(jnp.concatenate(cur, axis=1))
    return rows


def _pad128(v):
    return jnp.pad(v, ((0, 0), (0, LANES - v.shape[1])))


def kernel(x, norm1_w, w_in, lb_logits, hg_norm_w, q_norm_w, k_norm_w, att_norm_w, w_out, norm2_w, w_gate_up, w_down, final_norm_w, loss_target, m_norm1_w, m_w_in, m_lb_logits, m_hg_norm_w, m_q_norm_w, m_k_norm_w, m_att_norm_w, m_w_out, m_norm2_w, m_w_gate_up, m_w_down, m_final_norm_w, v_norm1_w, v_w_in, v_lb_logits, v_hg_norm_w, v_q_norm_w, v_k_norm_w, v_att_norm_w, v_w_out, v_norm2_w, v_w_gate_up, v_w_down, v_final_norm_w):
    t, d = x.shape[1], x.shape[2]
    xi, yi, ci = _place()
    chip = 2 * xi + yi
    x2d = x.reshape(t, d)
    tgt = loss_target.reshape(t, d)
    tok = min(TOK, t)
    nt = t // tok
    ns = ATT_GROUP * nt

    late_w = [w_out[0].astype(BF16), w_gate_up[0].astype(BF16), w_down[0].astype(BF16)]
    lb_rows = lb_logits.reshape(4, LANES) * (ci == 0).astype(F32)
    lb_pad = lax.dynamic_update_slice(jnp.zeros((8, 1024), F32), lb_rows, (0, chip * LANES))
    lb_full = allreduce_small(lb_pad, "gather_lb")[:4, :HG_W]
    l_f0, l_f1, l_b0, l_b1 = (lb_full[i:i + 1] for i in range(4))

    n1 = norm1_w.reshape(1, d)
    n2 = norm2_w.reshape(1, d)
    nf = final_norm_w.reshape(1, d)
    tm = min(512, t)
    h1, g_in = _rows(lambda a, w: ((_rms(a, w),), ()), "norm1", t, tm, [_rin(x2d, tm)], [n1], [_rout(t, tm, d, BF16)],
                     ride=("gather", [w_in[0].astype(BF16)]))
    wf_in = g_in.transpose(1, 0, 2).reshape(g_in.shape[1], -1)
    u = mm_rows(h1, wf_in, "mm_in")
    o_f, ss_f = gla_fwd(u, l_f0, l_f1, 1, False, "gla_fwd_f")
    o_b, ss_b = gla_fwd(u, l_b0, l_b1, 2, True, "gla_fwd_b")

    cos2, sin2 = _rope_tables(t)
    cq, sq = jnp.tile(cos2, (1, ATT_HEADS)), jnp.tile(sin2, (1, ATT_HEADS))
    ck, sk = jnp.tile(cos2, (1, ATT_KV)), jnp.tile(sin2, (1, ATT_KV))
    qw, kw = _pad128(q_norm_w.reshape(1, ATT_DH)), _pad128(k_norm_w.reshape(1, ATT_DH))
    gq, gk = _group_sum_matrix(ATT_QW), _group_sum_matrix(ATT_KVW)
    tq, tk = _tile_matrix(ATT_QW), _tile_matrix(ATT_KVW)
    prep_in = [_rin(u, tok, ATT_QW, 5), _rin(u, tok, ATT_KVW, 24), _rin(cq, tok), _rin(sq, tok), _rin(ck, tok),
               _rin(sk, tok)]
    prep_consts = [qw, kw, gq, gk, tq, tk]

    def att_prep_fn(aq, ak, av, *rest):
        q, k = _att_prep_fn(aq, ak, *rest)
        return (_head_t(q, ATT_HEADS), _head_s(q, ATT_HEADS), _head_s(k, ATT_KV), _head_t(k, ATT_KV),
                _head_t(av, ATT_KV)), ()

    q_t, q_s, k_s, k_t, v_t = _rows(
        att_prep_fn, "att_prep", t, tok, prep_in[:2] + [_rin(u, tok, ATT_KVW, 25)] + prep_in[2:], prep_consts,
        [_tout(ATT_HEADS, nt, ATT_DH, tok, BF16), _tout(ATT_HEADS, nt, tok, LANES, BF16),
         _tout(ATT_KV, nt, tok, LANES, BF16), _tout(ATT_KV, nt, ATT_DH, tok, BF16),
         _tout(ATT_KV, nt, ATT_DH, tok, BF16)])
    q_t = q_t.reshape(ATT_KV, ns, ATT_DH, tok)
    q_s = q_s.reshape(ATT_KV, ns, tok, LANES)
    k_s = k_s.reshape(ATT_KV, t, LANES)
    o_t, lse, g_out, g_gu, g_down = fa_fwd(q_t, k_s, v_t, ride=("gather", late_w))
    wf_gu = g_gu.transpose(1, 0, 2).reshape(g_gu.shape[1], -1)
    wf_out = g_out.reshape(-1, g_out.shape[2])
    wf_down = g_down.reshape(-1, g_down.shape[2])
    o_tiles = o_t.reshape(ATT_HEADS, nt, ATT_DH, tok)

    hgw = hg_norm_w.reshape(1, HG_D)
    attw = att_norm_w.reshape(1, ATT_QW)
    mix_in = [_rin(o_f, tok), _rin(o_b, tok), _rin(u, tok, HG_W, 4), _tin(o_tiles)]
    (mix,) = _rows(lambda of, ob, ug, ot, hw, aw: ((_post_mix_fn(of, ob, ug, _from_head_t(ot), hw, aw),), ()),
                   "post_mix", t, tok, mix_in, [hgw, attw], [_rout(t, tok, d, BF16)])
    x1, h2 = mm_res_norm(mix, wf_out, x2d, n2, "mm_out")
    gu2, act = ffn_up(h2, wf_gu)

    dx2, dx2b, g_final, loss_part = mm_res_loss(act, wf_down, x1, tgt, nf, "mm_down_loss")
    dgu2 = ffn_down_dx(dx2b, wf_down, gu2)
    gw_down = mm_tn(dx2b, act, "mm_down_dw", transpose_out=True)
    dx1, dx1b, g_norm2 = mm_nt_norm_bwd(dgu2, wf_gu, x1, n2, dx2, "mm_gate_up_dx", True)
    gw_gu = mm_tn(h2, dgu2, "mm_gate_up_dw", col_shards=N_CHIPS)
    dmix = mm_rows(dx1b, wf_out, "mm_out_dx", trans_b=True)
    gw_out = mm_tn(mix, dx1b, "mm_out_dw")

    def post_mix_bwd_fn(of, ob, ug, ot, dm, hgw_, attw_):
        _, vjp = jax.vjp(_post_mix_fn, of, ob, ug, _from_head_t(ot), hgw_, attw_)
        dof, _, dug, doa, dhgw, dattw = vjp(dm)
        return (dof, dug, _head_t(doa, ATT_HEADS), _head_s(doa, ATT_HEADS)), (dhgw, dattw)

    late = ["w_out", "w_gate_up", "w_down"]
    late_g = [gw_out.reshape(N_CHIPS, -1, d), gw_gu, gw_down.reshape(N_CHIPS, -1, d)]
    do_hg, du_g, do_t, do_s, g_hg, g_att, *late_recv = _rows(
        post_mix_bwd_fn, "post_mix_bwd", t, tok, mix_in + [_rin(dmix, tok)], [hgw, attw],
        [_rout(t, tok, HG_W, F32), _rout(t, tok, HG_W, BF16), _tout(ATT_HEADS, nt, ATT_DH, tok, BF16),
         _tout(ATT_HEADS, nt, tok, LANES, BF16)], [HG_D, ATT_QW], ride=("siblings", late_g))
    late_part = [add_my_half(g, r, "add_my_half_" + n) for g, r, n in zip(late_g, late_recv, late)]
    dq_t, dk_t, dv_t, *late_parts = fa_bwd(q_s, q_t, do_s.reshape(q_s.shape), do_t.reshape(q_t.shape), o_t, lse, k_s, k_t,
                                           v_t, ride=("reduce", late_part))

    def att_prep_bwd_fn(aq, ak, cq_, sq_, ck_, sk_, dqt, dkt, dvt, qw_, kw_, gq_, gk_, tq_, tk_):
        _, vjp = jax.vjp(lambda a, b, c_, e: _att_prep_fn(a, b, cq_, sq_, ck_, sk_, c_, e, gq_, gk_, tq_, tk_),
                         aq, ak, qw_, kw_)
        daq, dak, dqw, dkw = vjp((_from_head_t(dqt), _from_head_t(dkt)))
        return (daq, dak, _from_head_t(dvt)), (dqw, dkw)

    da_q, da_k, da_v, g_q, g_k = _rows(
        att_prep_bwd_fn, "att_prep_bwd", t, tok,
        prep_in + [_tin(dq_t.reshape(ATT_HEADS, nt, ATT_DH, tok)), _tin(dk_t), _tin(dv_t)], prep_consts,
        [_rout(t, tok, ATT_QW, BF16), _rout(t, tok, ATT_KVW, BF16), _rout(t, tok, ATT_KVW, BF16)], [LANES, LANES])

    dq_f, df_f, di_f, dl_f0, dl_f1 = gla_bwd(u, l_f0, l_f1, ss_f, do_hg, 1, False, "gla_bwd_f")
    dq_b, df_b, di_b, dl_b0, dl_b1 = gla_bwd(u, l_b0, l_b1, ss_b, do_hg, 2, True, "gla_bwd_b")

    def assemble_fn(qf, qb, ff, fb, i_f, i_b, dg, aq, ak, av):
        parts = [qf + qb, ff, fb, i_f + i_b, dg.astype(F32), aq.astype(F32), ak.astype(F32), av.astype(F32)]
        return (jnp.concatenate(parts, axis=1),), ()

    (du,) = _rows(assemble_fn, "assemble_du", t, tok,
                  [_rin(a, tok) for a in (dq_f, dq_b, df_f, df_b, di_f, di_b, du_g, da_q, da_k, da_v)], [],
                  [_rout(t, tok, u.shape[1], BF16)])
    grad_x, g_norm1 = mm_nt_norm_bwd(du, wf_in, x2d, n1, dx1, "mm_in_dx", False)
    gw_in_t = mm_tn(h1, du, "mm_in_dw", transpose_out=True)

    names = ["w_in"] + late
    g_in4 = gw_in_t.reshape(N_CHIPS, -1, d)
    in_part = add_my_half(g_in4, exchange([g_in4], "siblings", "rs_siblings_w_in")[0], "add_my_half_w_in")
    parts = list(exchange([in_part], "reduce", "rs_chips_w_in")) + late_parts
    g_shard = [sum_chips(p, "sum_chips_" + n) for p, n in zip(parts, names)]
    g_shard[0] = g_shard[0].T
    big = {}
    for n, g, w, m, v in zip(names, g_shard, (w_in, w_out, w_gate_up, w_down), (m_w_in, m_w_out, m_w_gate_up, m_w_down),
                             (v_w_in, v_w_out, v_w_gate_up, v_w_down)):
        dlt, mn, vn = adamw(w[0], g, m[0], v[0], "adamw_" + n)
        big[n] = (g[None], dlt[None], mn[None], vn[None])

    small = [g_norm1, g_norm2, g_final, g_att, g_hg, g_q, g_k, loss_part, dl_f0, dl_f1, dl_b0, dl_b1]
    packed = _pack_rows(small)
    packed += [jnp.zeros((1, 1024), F32)] * (8 - len(packed))
    tot = allreduce_small(jnp.concatenate(packed, axis=0), "allreduce_small")
    s_norm1, s_norm2, s_final = tot[0:1], tot[1:2], tot[2:3]
    s_att, s_hg, s_q, s_k = tot[3:4, 0:512], tot[3:4, 512:640], tot[3:4, 640:704], tot[3:4, 768:832]
    loss = tot[3, 896]
    s_lb = jnp.concatenate([tot[4:5, 0:512], tot[4:5, 512:1024], tot[5:6, 0:512], tot[5:6, 512:1024]], axis=0)
    s_lb = lax.dynamic_slice(s_lb, (0, chip * LANES), (4, LANES)).reshape(1, 512)

    snames = ["norm1_w", "lb_logits", "hg_norm_w", "q_norm_w", "k_norm_w", "att_norm_w", "norm2_w", "final_norm_w"]
    g_small = dict(zip(snames, [s_norm1, s_lb, s_hg, s_q, s_k, s_att, s_norm2, s_final]))
    w_small = dict(zip(snames, [norm1_w, lb_logits, hg_norm_w, q_norm_w, k_norm_w, att_norm_w, norm2_w, final_norm_w]))
    m_small = dict(zip(snames, [m_norm1_w, m_lb_logits, m_hg_norm_w, m_q_norm_w, m_k_norm_w, m_att_norm_w, m_norm2_w, m_final_norm_w]))
    v_small = dict(zip(snames, [v_norm1_w, v_lb_logits, v_hg_norm_w, v_q_norm_w, v_k_norm_w, v_att_norm_w, v_norm2_w, v_final_norm_w]))

    def pack_small(tree):
        rows = _pack_rows([tree[n].reshape(1, -1) for n in snames])
        rows += [jnp.zeros((1, 1024), F32)] * (8 - len(rows))
        return jnp.concatenate(rows, axis=0)

    d_s, m_s, v_s = adamw(pack_small(w_small), pack_small(g_small), pack_small(m_small), pack_small(v_small), "adamw_small")

    def unpack_small(a):
        out, r, used = {}, 0, 0
        for n in snames:
            size = w_small[n].size
            if used + size > 1024:
                r, used = r + 1, 0
            out[n] = a[r, used:used + size].reshape(w_small[n].shape)
            used += size
        return out

    d_sm, m_sm, v_sm = unpack_small(d_s), unpack_small(m_s), unpack_small(v_s)
    g_sm = {n: g_small[n].reshape(w_small[n].shape) for n in snames}

    order = ["norm1_w", "w_in", "lb_logits", "hg_norm_w", "q_norm_w", "k_norm_w", "att_norm_w", "w_out", "norm2_w",
             "w_gate_up", "w_down", "final_norm_w"]

    def pick(small_tree, idx):
        return [big[n][idx] if n in big else small_tree[n] for n in order]

    return (loss, grad_x.reshape(x.shape), *pick(g_sm, 0), *pick(d_sm, 1), *pick(m_sm, 2), *pick(v_sm, 3))
```

```python
import functools

import numpy as np
import jax
import jax.numpy as jnp
from jax import lax
from jax.experimental import pallas as pl
from jax.experimental.pallas import tpu as pltpu

F32 = jnp.float32
BF16 = jnp.bfloat16
MESH = pl.DeviceIdType.MESH

EPS = 1e-6
GRID_W = 64
HG_HEADS = 4
HG_D = 128
HG_W = HG_HEADS * HG_D
CHUNK = 64
ATT_HEADS = 8
ATT_KV = 2
ATT_GROUP = ATT_HEADS // ATT_KV
ATT_DH = 64
ATT_QW = ATT_HEADS * ATT_DH
ATT_KVW = ATT_KV * ATT_DH
ROPE_THETA = 10000.0
N_CHIPS = 4
N_DEV = 8

ADAM_LR = 0.001
ADAM_B1 = 0.9
ADAM_B2 = 0.999
ADAM_EPS = 1e-08
ADAM_WD = 0.01
ADAM_STEP = 10

VMEM_LIMIT = 52 * 1024 * 1024
LANES = 128
TOK = 256


def _params(sem=None):
    return pltpu.CompilerParams(dimension_semantics=sem, vmem_limit_bytes=VMEM_LIMIT)


def _dg(a, b, ca, cb):
    return lax.dot_general(a.astype(BF16), b.astype(BF16), (((ca,), (cb,)), ((), ())),
                           preferred_element_type=F32)


@jax.custom_vjp
def dot_nn(a, b):
    return _dg(a, b, 1, 0)


def _dot_nn_fwd(a, b):
    return _dg(a, b, 1, 0), (a, b)


def _dot_nn_bwd(res, g):
    a, b = res
    return _dg(g, b, 1, 1), _dg(a, g, 0, 0)


dot_nn.defvjp(_dot_nn_fwd, _dot_nn_bwd)


@jax.custom_vjp
def dot_nt(a, b):
    return _dg(a, b, 1, 1)


def _dot_nt_fwd(a, b):
    return _dg(a, b, 1, 1), (a, b)


def _dot_nt_bwd(res, g):
    a, b = res
    return _dg(g, b, 1, 0), _dg(g, a, 0, 0)


dot_nt.defvjp(_dot_nt_fwd, _dot_nt_bwd)


@jax.custom_vjp
def dot_tn(a, b):
    return _dg(a, b, 0, 0)


def _dot_tn_fwd(a, b):
    return _dg(a, b, 0, 0), (a, b)


def _dot_tn_bwd(res, g):
    a, b = res
    return _dg(b, g, 1, 1), _dg(a, g, 1, 0)


dot_tn.defvjp(_dot_tn_fwd, _dot_tn_bwd)


def _split3(a):
    hi = a.astype(BF16)
    r1 = a - hi.astype(F32)
    mid = r1.astype(BF16)
    lo = (r1 - mid.astype(F32)).astype(BF16)
    return lo, mid, hi


def _sum3(terms):
    lo, mid, hi = terms
    return (lo + mid) + hi


@jax.custom_vjp
def xdot_r(a, m):
    return _sum3([_dg(p, m, 1, 0) for p in _split3(a)])


def _xdot_r_fwd(a, m):
    return xdot_r(a, m), m


def _xdot_r_bwd(m, g):
    return _sum3([_dg(p, m, 1, 1) for p in _split3(g)]), jnp.zeros_like(m)


xdot_r.defvjp(_xdot_r_fwd, _xdot_r_bwd)


@jax.custom_vjp
def xdot_l(m, a):
    return _sum3([_dg(m, p, 1, 0) for p in _split3(a)])


def _xdot_l_fwd(m, a):
    return xdot_l(m, a), m


def _xdot_l_bwd(m, g):
    return jnp.zeros_like(m), _sum3([_dg(m, p, 0, 0) for p in _split3(g)])


xdot_l.defvjp(_xdot_l_fwd, _xdot_l_bwd)


@jax.custom_vjp
def swap_pairs(y):
    n = y.shape[-1]
    lane = lax.broadcasted_iota(jnp.int32, y.shape, 1)
    nxt = pltpu.roll(y, n - 1, 1)
    prv = pltpu.roll(y, 1, 1)
    return jnp.where(lane % 2 == 0, nxt, prv)


def _swap_fwd(y):
    return swap_pairs(y), None


def _swap_bwd(_, g):
    return (swap_pairs(g),)


swap_pairs.defvjp(_swap_fwd, _swap_bwd)


def _rms(x, w):
    return x * lax.rsqrt(jnp.mean(x * x, axis=-1, keepdims=True) + EPS) * w


def _sigmoid(x):
    return jax.nn.sigmoid(x)


def _rows(fn, name, t, tm, ins, consts, outs, accs=(), ride=None):
    n_r, n_c, n_o, n_a = len(ins), len(consts), len(outs), len(accs)

    def body(*refs):
        r = refs[:n_r]
        c = refs[n_r:n_r + n_c]
        o = refs[n_r + n_c:n_r + n_c + n_o]
        a = refs[n_r + n_c + n_o:]
        ro, ao = fn(*[x[...] for x in r], *[x[...] for x in c])
        for ref, val in zip(o, ro):
            if isinstance(val, (list, tuple)):
                for h, piece in enumerate(val):
                    ref[h, 0] = piece.astype(ref.dtype)
            else:
                ref[...] = val.astype(ref.dtype)
        if n_a:
            @pl.when(pl.program_id(0) == 0)
            def _():
                for ref in a:
                    ref[...] = jnp.zeros(ref.shape, F32)
            for ref, val in zip(a, ao):
                ref[...] += val

    in_specs = [s for _, s in ins]
    in_specs += [pl.BlockSpec(c.shape, lambda i, nd=c.ndim: (0,) * nd) for c in consts]
    out_specs = [s for _, s in outs] + [pl.BlockSpec((1, w), lambda i: (0, 0)) for w in accs]
    out_shape = [s for s, _ in outs] + [jax.ShapeDtypeStruct((1, w), F32) for w in accs]
    grid = (t // tm,)
    body, r_in, r_out, r_shapes, r_sems, r_args = _riding(body, n_r + n_c, n_o + n_a, grid, ride)
    return pl.pallas_call(
        body, name=name, grid=grid, in_specs=in_specs + r_in, out_specs=out_specs + r_out,
        out_shape=out_shape + r_shapes, scratch_shapes=r_sems, compiler_params=_params(("arbitrary",)),
    )(*[a for a, _ in ins], *consts, *r_args)


def _rin(a, tm, width=None, cb=0):
    width = a.shape[1] if width is None else width
    return a, pl.BlockSpec((tm, width), lambda i, cb=cb: (i, cb))


def _rout(t, tm, width, dtype):
    return jax.ShapeDtypeStruct((t, width), dtype), pl.BlockSpec((tm, width), lambda i: (i, 0))


def _tin(a):
    return a, pl.BlockSpec((a.shape[0], 1) + a.shape[2:], lambda i: (0, i, 0, 0))


def _tout(heads, nt, r, c, dtype):
    return jax.ShapeDtypeStruct((heads, nt, r, c), dtype), pl.BlockSpec((heads, 1, r, c), lambda i: (0, i, 0, 0))


def _pick(n, cap, mult):
    best = None
    for d in range(mult, min(n, cap) + 1, mult):
        if n % d == 0:
            best = d
    return best if best is not None else n


def mm_rows(a, b, name, trans_b=False, res=None, out_dtype=F32):
    m, k = a.shape
    n = b.shape[0] if trans_b else b.shape[1]
    tn = _pick(n, 3328, LANES)
    tm = _pick(m, 512 if k <= 3072 else 256, 8)
    has_res = res is not None

    def body(*refs):
        if has_res:
            a_ref, b_ref, r_ref, o_ref = refs
        else:
            a_ref, b_ref, o_ref = refs
        acc = _dg(a_ref[...], b_ref[...], 1, 1 if trans_b else 0)
        if has_res:
            acc = acc + r_ref[...]
        o_ref[...] = acc.astype(o_ref.dtype)

    in_specs = [pl.BlockSpec((tm, k), lambda j, i: (i, 0))]
    if trans_b:
        in_specs.append(pl.BlockSpec((tn, k), lambda j, i: (j, 0)))
    else:
        in_specs.append(pl.BlockSpec((k, tn), lambda j, i: (0, j)))
    args = [a, b]
    if has_res:
        in_specs.append(pl.BlockSpec((tm, tn), lambda j, i: (i, j)))
        args.append(res)
    return pl.pallas_call(
        body, name=name, grid=(n // tn, m // tm), in_specs=in_specs,
        out_specs=pl.BlockSpec((tm, tn), lambda j, i: (i, j)),
        out_shape=jax.ShapeDtypeStruct((m, n), out_dtype),
        compiler_params=_params(("arbitrary", "arbitrary")),
    )(*args)


def mm_tn(a, b, name, col_shards=None, transpose_out=False):
    t, m = a.shape
    parts = b.shape[0] if b.ndim == 3 else 1
    n = parts * b.shape[-1]
    tm = _pick(m, 512, LANES)
    width = n // col_shards if col_shards else None
    group = max(1, 3328 // width) if col_shards else 1
    tn = width * group if col_shards else _pick(n, 3328, LANES)
    tk = _pick(t, 1024, 8)
    nkk = t // tk
    per_part = b.shape[-1] // tn

    def body(a_ref, b_ref, o_ref, *scratch):
        acc_ref = scratch[0] if transpose_out else o_ref

        @pl.when(pl.program_id(2) == 0)
        def _():
            acc_ref[...] = jnp.zeros(acc_ref.shape, F32)
        acc = _dg(a_ref[...], b_ref[0] if b.ndim == 3 else b_ref[...], 0, 0)
        if col_shards:
            for s in range(group):
                acc_ref[s] += acc[:, s * width:(s + 1) * width]
        else:
            acc_ref[...] += acc
        if transpose_out:
            @pl.when(pl.program_id(2) == nkk - 1)
            def _():
                o_ref[...] = acc_ref[...].T

    if col_shards:
        out_spec = pl.BlockSpec((group, tm, width), lambda i, j, kk: (j, i, 0))
        out_shape = jax.ShapeDtypeStruct((col_shards, m, width), F32)
    elif transpose_out:
        out_spec = pl.BlockSpec((tn, tm), lambda i, j, kk: (j, i))
        out_shape = jax.ShapeDtypeStruct((n, m), F32)
    else:
        out_spec = pl.BlockSpec((tm, tn), lambda i, j, kk: (i, j))
        out_shape = jax.ShapeDtypeStruct((m, n), F32)
    if b.ndim == 3:
        b_spec = pl.BlockSpec((1, tk, tn), lambda i, j, kk: (j // per_part, kk, j % per_part))
    else:
        b_spec = pl.BlockSpec((tk, tn), lambda i, j, kk: (kk, j))
    return pl.pallas_call(
        body, name=name, grid=(m // tm, n // tn, nkk),
        in_specs=[pl.BlockSpec((tk, tm), lambda i, j, kk: (kk, i)), b_spec],
        out_specs=out_spec, out_shape=out_shape,
        scratch_shapes=[pltpu.VMEM((tm, tn), F32)] if transpose_out else [],
        compiler_params=_params(("arbitrary", "arbitrary", "arbitrary")),
    )(a, b)


def ffn_up(h2, w_gu):
    t, k = h2.shape
    dff = w_gu.shape[1] // 2
    tn = _pick(dff, 1408, LANES)
    ncol = dff // tn
    tm = _pick(t, 512, 8)

    def body(a_ref, wg_ref, wu_ref, gu_ref, act_ref):
        a = a_ref[...]
        g = _dg(a, wg_ref[...], 1, 0)
        u = _dg(a, wu_ref[...], 1, 0)
        gu_ref[0] = g.astype(gu_ref.dtype)
        gu_ref[1] = u.astype(gu_ref.dtype)
        act_ref[...] = _swiglu_fn(g, u).astype(act_ref.dtype)

    return pl.pallas_call(
        body, name="ffn_up", grid=(ncol, t // tm),
        in_specs=[pl.BlockSpec((tm, k), lambda s, i: (i, 0)), pl.BlockSpec((k, tn), lambda s, i: (0, s)),
                  pl.BlockSpec((k, tn), lambda s, i: (0, s + ncol))],
        out_specs=[pl.BlockSpec((2, tm, tn), lambda s, i: (0, i, s)), pl.BlockSpec((tm, tn), lambda s, i: (i, s))],
        out_shape=[jax.ShapeDtypeStruct((2, t, dff), BF16), jax.ShapeDtypeStruct((t, dff), BF16)],
        compiler_params=_params(("arbitrary", "arbitrary")),
    )(h2, w_gu, w_gu)


def ffn_down_dx(dx2b, w_down, gu2):
    t, k = dx2b.shape
    dff = w_down.shape[0]
    tn = _pick(dff, 1408, LANES)
    tm = _pick(t, 512, 8)

    def body(a_ref, w_ref, gu_ref, o_ref):
        dact = _dg(a_ref[...], w_ref[...], 1, 1)
        _, vjp = jax.vjp(_swiglu_fn, gu_ref[0].astype(F32), gu_ref[1].astype(F32))
        dg, du = vjp(dact)
        o_ref[0] = dg.astype(o_ref.dtype)
        o_ref[1] = du.astype(o_ref.dtype)

    blk = pl.BlockSpec((2, tm, tn), lambda s, i: (0, i, s))
    return pl.pallas_call(
        body, name="ffn_down_dx", grid=(dff // tn, t // tm),
        in_specs=[pl.BlockSpec((tm, k), lambda s, i: (i, 0)), pl.BlockSpec((tn, k), lambda s, i: (s, 0)), blk],
        out_specs=blk, out_shape=jax.ShapeDtypeStruct((2, t, dff), BF16),
        compiler_params=_params(("arbitrary", "arbitrary")),
    )(dx2b, w_down, gu2)


def mm_res_norm(a, b, res, w, name):
    t, k = a.shape
    d = b.shape[1]
    tm = _pick(t, 512, 8)

    def body(a_ref, b_ref, r_ref, w_ref, x_ref, h_ref):
        xv = _dg(a_ref[...], b_ref[...], 1, 0) + r_ref[...]
        x_ref[...] = xv
        h_ref[...] = _rms(xv, w_ref[...]).astype(h_ref.dtype)

    row = pl.BlockSpec((tm, d), lambda i: (i, 0))
    return pl.pallas_call(
        body, name=name, grid=(t // tm,),
        in_specs=[pl.BlockSpec((tm, k), lambda i: (i, 0)), pl.BlockSpec((k, d), lambda i: (0, 0)), row,
                  pl.BlockSpec((1, d), lambda i: (0, 0))],
        out_specs=[row, row],
        out_shape=[jax.ShapeDtypeStruct((t, d), F32), jax.ShapeDtypeStruct((t, d), BF16)],
        compiler_params=_params(("arbitrary",)),
    )(a, b, res, w)


def mm_res_loss(a, b, res, tgt, w, name):
    t, k = a.shape
    d = b.shape[1]
    tm = _pick(t, 256, 8)

    def body(a_ref, b_ref, r_ref, t_ref, w_ref, dx_ref, dxb_ref, dw_ref, loss_ref):
        @pl.when(pl.program_id(0) == 0)
        def _():
            dw_ref[...] = jnp.zeros(dw_ref.shape, F32)
            loss_ref[...] = jnp.zeros(loss_ref.shape, F32)
        xv = _dg(a_ref[...], b_ref[...], 1, 0) + r_ref[...]
        tg = t_ref[...]
        val, vjp = jax.vjp(lambda x_, w_: _loss_fn(x_, w_, tg), xv, w_ref[...])
        dx, dw = vjp(jnp.ones((1, 1), F32))
        dx_ref[...] = dx
        dxb_ref[...] = dx.astype(dxb_ref.dtype)
        dw_ref[...] += dw
        loss_ref[...] += jnp.broadcast_to(val, (1, LANES))

    row = pl.BlockSpec((tm, d), lambda i: (i, 0))
    vec = pl.BlockSpec((1, d), lambda i: (0, 0))
    return pl.pallas_call(
        body, name=name, grid=(t // tm,),
        in_specs=[pl.BlockSpec((tm, k), lambda i: (i, 0)), pl.BlockSpec((k, d), lambda i: (0, 0)), row, row, vec],
        out_specs=[row, row, vec, pl.BlockSpec((1, LANES), lambda i: (0, 0))],
        out_shape=[jax.ShapeDtypeStruct((t, d), F32), jax.ShapeDtypeStruct((t, d), BF16),
                   jax.ShapeDtypeStruct((1, d), F32), jax.ShapeDtypeStruct((1, LANES), F32)],
        compiler_params=_params(("arbitrary",)),
    )(a, b, res, tgt, w)


def mm_nt_norm_bwd(a, wmat, x, w, dres, name, with_bf16, ride=None):
    parts = a.shape[0] if a.ndim == 3 else 1
    t, kp = a.shape[-2], a.shape[-1]
    d = wmat.shape[0]
    tm = _pick(t, 256, 8)

    n_out = 3 if with_bf16 else 2

    def body(*refs):
        a_ref, w_refs = refs[0], refs[1:1 + parts]
        x_ref, nw_ref, r_ref, dx_ref = refs[1 + parts:5 + parts]
        dw_ref = refs[4 + parts + n_out - 1]

        @pl.when(pl.program_id(0) == 0)
        def _():
            dw_ref[...] = jnp.zeros(dw_ref.shape, F32)
        dh = None
        for p in range(parts):
            term = _dg(a_ref[p] if a.ndim == 3 else a_ref[...], w_refs[p][...], 1, 1)
            dh = term if dh is None else dh + term
        _, vjp = jax.vjp(_rms, x_ref[...], nw_ref[...])
        dx, dw = vjp(dh)
        dx = dx + r_ref[...]
        dx_ref[...] = dx
        if with_bf16:
            refs[5 + parts][...] = dx.astype(BF16)
        dw_ref[...] += dw

    row = pl.BlockSpec((tm, d), lambda i: (i, 0))
    vec = pl.BlockSpec((1, d), lambda i: (0, 0))
    a_spec = pl.BlockSpec((parts, tm, kp), lambda i: (0, i, 0)) if a.ndim == 3 else pl.BlockSpec((tm, kp), lambda i: (i, 0))
    w_specs = [pl.BlockSpec((d, kp), lambda i, p=p: (0, p)) for p in range(parts)]
    outs = [row] + ([row] if with_bf16 else []) + [vec]
    shapes = [jax.ShapeDtypeStruct((t, d), F32)] + ([jax.ShapeDtypeStruct((t, d), BF16)] if with_bf16 else [])
    grid = (t // tm,)
    body, r_in, r_out, r_shapes, r_sems, r_args = _riding(body, 4 + parts, n_out, grid, ride)
    return pl.pallas_call(
        body, name=name, grid=grid,
        in_specs=[a_spec] + w_specs + [row, vec, row] + r_in,
        out_specs=outs + r_out, out_shape=shapes + [jax.ShapeDtypeStruct((1, d), F32)] + r_shapes,
        scratch_shapes=r_sems, compiler_params=_params(("arbitrary",)),
    )(a, *([wmat] * parts), x, w, dres, *r_args)


def _lower_bound(l0, l1):
    m = jnp.maximum(l0, l1)
    e0 = jnp.exp(l0 - m)
    e1 = jnp.exp(l1 - m)
    return e0 / (e0 + e1)


def _gla_consts(rev):
    ri = lax.broadcasted_iota(jnp.int32, (CHUNK, CHUNK), 0)
    ci = lax.broadcasted_iota(jnp.int32, (CHUNK, CHUNK), 1)
    keep = (ci >= ri) if rev else (ci <= ri)
    ref_mask = lax.broadcasted_iota(jnp.int32, (CHUNK, 1), 0) == (CHUNK // 2 if rev else CHUNK // 2 - 1)
    return keep, ref_mask


def _gla_block(uq, uf, ui, l0, l1, st_in, rev):
    ncb = uq.shape[0] // CHUNK
    heads = range(HG_HEADS)
    keep, ref_mask = _gla_consts(rev)
    cum = keep.astype(F32)
    lb = _lower_bound(l0, l1)
    q = uq * _sigmoid(uq)
    k = (1.0 - lb) * _sigmoid(-uf)
    g = jnp.log(lb + (1.0 - lb) * _sigmoid(uf))

    def rows(a, c):
        return a[c * CHUNK:(c + 1) * CHUNK]

    def head(a, h):
        return a[:, h * HG_D:(h + 1) * HG_D]

    bs = [xdot_l(cum, rows(g, c)) for c in range(ncb)]
    q_in, k_in, q_b, k_d, decay = [], [], [], [], []
    for c in range(ncb):
        b = bs[c]
        b_ref = jnp.sum(jnp.where(ref_mask, b, 0.0), axis=0, keepdims=True)
        b_last = jnp.sum(rows(g, c), axis=0, keepdims=True)
        qc, kc = rows(q, c), rows(k, c)
        q_in.append(qc * jnp.exp(b - b_ref))
        k_in.append(kc * jnp.exp(b_ref - b))
        q_b.append(qc * jnp.exp(b))
        k_d.append(kc * jnp.exp(b_last - b))
        decay.append(jnp.exp(b_last))
    scores = [[jnp.where(keep, dot_nt(head(q_in[c], h), head(k_in[c], h)), 0.0) for h in heads] for c in range(ncb)]
    o_intra = [[dot_nn(scores[c][h], head(rows(ui, c), h)) for h in heads] for c in range(ncb)]
    contrib = [[dot_tn(head(rows(ui, c), h), head(k_d[c], h)) for h in heads] for c in range(ncb)]
    st = list(st_in)
    o_rows = [None] * ncb
    for c in (reversed(range(ncb)) if rev else range(ncb)):
        parts = []
        for h in heads:
            parts.append(o_intra[c][h] + dot_nt(head(q_b[c], h), st[h]))
            st[h] = st[h] * head(decay[c], h) + contrib[c][h]
        o_rows[c] = jnp.concatenate(parts, axis=1)
    return jnp.concatenate(o_rows, axis=0), tuple(st)


def _gla_blocks(t):
    tb = min(512, t)
    return tb, t // tb


def gla_fwd(u, l0, l1, fcol, rev, name):
    t = u.shape[0]
    tb, nb = _gla_blocks(t)

    def blk(i):
        return (nb - 1 - i) if rev else i

    def body(uq_ref, uf_ref, ui_ref, l0_ref, l1_ref, o_ref, ss_ref, st_ref):
        @pl.when(pl.program_id(0) == 0)
        def _():
            st_ref[...] = jnp.zeros(st_ref.shape, F32)
        ss_ref[0] = st_ref[...]
        o, st_out = _gla_block(uq_ref[...], uf_ref[...], ui_ref[...], l0_ref[...], l1_ref[...],
                               tuple(st_ref[h] for h in range(HG_HEADS)), rev)
        o_ref[...] = o
        for h in range(HG_HEADS):
            st_ref[h] = st_out[h]

    row = lambda cb: pl.BlockSpec((tb, HG_W), lambda i: (blk(i), cb))
    vec = pl.BlockSpec((1, HG_W), lambda i: (0, 0))
    return pl.pallas_call(
        body, name=name, grid=(nb,),
        in_specs=[row(0), row(fcol), row(3), vec, vec],
        out_specs=[pl.BlockSpec((tb, HG_W), lambda i: (blk(i), 0)),
                   pl.BlockSpec((1, HG_HEADS, HG_D, HG_D), lambda i: (blk(i), 0, 0, 0))],
        out_shape=[jax.ShapeDtypeStruct((t, HG_W), F32),
                   jax.ShapeDtypeStruct((nb, HG_HEADS, HG_D, HG_D), F32)],
        scratch_shapes=[pltpu.VMEM((HG_HEADS, HG_D, HG_D), F32)],
        compiler_params=_params(("arbitrary",)),
    )(u, u, u, l0, l1)


def gla_bwd(u, l0, l1, ss, do, fcol, rev, name):
    t = u.shape[0]
    tb, nb = _gla_blocks(t)

    def blk(i):
        return i if rev else (nb - 1 - i)

    def body(uq_ref, uf_ref, ui_ref, l0_ref, l1_ref, ss_ref, do_ref,
             dq_ref, df_ref, di_ref, dl0_ref, dl1_ref, dst_ref):
        @pl.when(pl.program_id(0) == 0)
        def _():
            dst_ref[...] = jnp.zeros(dst_ref.shape, F32)
            dl0_ref[...] = jnp.zeros(dl0_ref.shape, F32)
            dl1_ref[...] = jnp.zeros(dl1_ref.shape, F32)
        heads = range(HG_HEADS)
        _, vjp = jax.vjp(functools.partial(_gla_block, rev=rev), uq_ref[...], uf_ref[...], ui_ref[...],
                         l0_ref[...], l1_ref[...], tuple(ss_ref[0, h] for h in heads))
        dq, df, di, dl0, dl1, dst = vjp((do_ref[...], tuple(dst_ref[h] for h in heads)))
        dq_ref[...] = dq
        df_ref[...] = df
        di_ref[...] = di
        dl0_ref[...] += dl0
        dl1_ref[...] += dl1
        for h in heads:
            dst_ref[h] = dst[h]

    row = lambda cb: pl.BlockSpec((tb, HG_W), lambda i: (blk(i), cb))
    vec = pl.BlockSpec((1, HG_W), lambda i: (0, 0))
    orow = pl.BlockSpec((tb, HG_W), lambda i: (blk(i), 0))
    return pl.pallas_call(
        body, name=name, grid=(nb,),
        in_specs=[row(0), row(fcol), row(3), vec, vec,
                  pl.BlockSpec((1, HG_HEADS, HG_D, HG_D), lambda i: (blk(i), 0, 0, 0)), orow],
        out_specs=[orow, orow, orow, vec, vec],
        out_shape=[jax.ShapeDtypeStruct((t, HG_W), F32)] * 3 + [jax.ShapeDtypeStruct((1, HG_W), F32)] * 2,
        scratch_shapes=[pltpu.VMEM((HG_HEADS, HG_D, HG_D), F32)],
        compiler_params=_params(("arbitrary",)),
    )(u, u, u, l0, l1, ss, do)


def _rope_tables(t):
    rows = t // GRID_W
    row = jnp.repeat(jnp.arange(rows), GRID_W).astype(F32)
    col = jnp.tile(jnp.arange(GRID_W), rows).astype(F32)
    axis_dim = ATT_DH // 2
    freqs = ROPE_THETA ** (-jnp.arange(0, axis_dim, 2, dtype=F32) / axis_dim)
    ang = jnp.concatenate([row[:, None] * freqs, col[:, None] * freqs], axis=-1)
    cos2 = jnp.repeat(jnp.cos(ang), 2, axis=-1)
    sin2 = jnp.repeat(jnp.sin(ang), 2, axis=-1) * jnp.tile(jnp.array([-1.0, 1.0], F32), ATT_DH // 2)
    return cos2, sin2


def _group_sum_matrix(width):
    idx = np.arange(width) // ATT_DH
    return jnp.asarray((idx[:, None] == idx[None, :]).astype(np.float32))


def _tile_matrix(width):
    m = np.zeros((LANES, width), np.float32)
    m[np.arange(width) % ATT_DH, np.arange(width)] = 1.0
    return jnp.asarray(m)


def _tile_w(w128, tile_m):
    w8 = jnp.broadcast_to(w128, (8, LANES))
    return jnp.sum(xdot_r(w8, tile_m), axis=0, keepdims=True) * 0.125


def _head_norm_rope(a, w128, cos_t, sin_t, gsum, tile_m, scale):
    ssq = xdot_r(a * a, gsum)
    y = a * lax.rsqrt(ssq * (1.0 / ATT_DH) + EPS) * _tile_w(w128, tile_m)
    return (y * cos_t + swap_pairs(y) * sin_t) * scale


def _att_prep_fn(aq, ak, cq, sq, ck, sk, qw, kw, gq, gk, tq, tk):
    q = _head_norm_rope(aq, qw, cq, sq, gq, tq, ATT_DH ** -0.5)
    k = _head_norm_rope(ak, kw, ck, sk, gk, tk, 1.0)
    return q, k


def _head_t(x, heads):
    xt = x.T
    return [xt[h * ATT_DH:(h + 1) * ATT_DH] for h in range(heads)]


def _head_s(x, heads):
    lane = lax.broadcasted_iota(jnp.int32, (x.shape[0], LANES), 1)
    out = []
    for h in range(heads):
        pair = x[:, (h // 2) * LANES:(h // 2 + 1) * LANES]
        if h % 2:
            pair = pltpu.roll(pair, ATT_DH, 1)
        out.append(jnp.where(lane < ATT_DH, pair, 0.0))
    return out


def _from_head_t(tile):
    return jnp.concatenate([tile[h, 0] for h in range(tile.shape[0])], axis=0).T


def _pad_rows(a):
    return jnp.concatenate([a, jnp.zeros(a.shape, a.dtype)], axis=0)


def _col_bcast(row_vec):
    return jnp.broadcast_to(row_vec, (LANES, row_vec.shape[1])).T


FA_K_FWD = 512
FA_K_BWD = 512
FA_STRIPS_FWD = 8
FA_STRIPS_BWD = 8


def _key_block(ref, j, tiles):
    return jnp.concatenate([ref[0, j * tiles + i] for i in range(tiles)], axis=1)


def _riding(body, n_in, n_out, grid, ride):
    if ride is None:
        return body, [], [], [], [], []
    kind, arrays = ride
    n = len(arrays)
    steps = int(np.prod(grid))

    def riding_body(*refs):
        cuts = np.cumsum([0, n_in, n, n_out, n]).tolist()
        own_in, rin, own_out, rout = (refs[a:b] for a, b in zip(cuts[:-1], cuts[1:]))
        start, forward, finish = _exchange_phases(rin, rout, *refs[cuts[-1]:], kind)
        step = 0
        for axis, size in enumerate(grid):
            step = step * size + pl.program_id(axis)
        pl.when(step == 0)(start)
        pl.when(step == steps // 2)(forward)
        body(*own_in, *own_out)
        pl.when(step == steps - 1)(finish)

    return riding_body, _hbm_specs(n), _hbm_specs(n), _exchange_shapes(arrays, kind), _exchange_sems(n), list(arrays)


def fa_fwd(qt, ks, vt, ride=None):
    _, ns, dh, nq = qt.shape
    lk = ks.shape[1]
    bk = min(FA_K_FWD, lk)
    nk = lk // bk
    spg = min(FA_STRIPS_FWD, ns)
    grid = (ATT_KV, ns // spg)

    def body(q_ref, k_ref, v_ref, o_ref, lse_ref):
        qs = [_pad_rows(q_ref[0, c]) for c in range(spg)]

        def keys(j):
            return k_ref[0, j * bk:(j + 1) * bk, :]

        def step(j, carry):
            st0, stats = carry
            kb = keys(j)
            vb = _key_block(v_ref, j, bk // nq)
            sts = [st0] + [_dg(kb, qs[c], 1, 0) for c in range(1, spg)]
            out = []
            for c in range(spg):
                m, l, acc = stats[c]
                m_new = jnp.maximum(m, jnp.max(sts[c], axis=0, keepdims=True))
                alpha = jnp.exp(m - m_new)
                p = jnp.exp(sts[c] - m_new)
                l = alpha * l + jnp.sum(p, axis=0, keepdims=True)
                if c == spg - 1:
                    st0 = _dg(keys(min(j + 1, nk - 1)), qs[0], 1, 0)
                acc = alpha * acc + _dg(vb, p, 1, 0)
                out.append((m_new, l, acc))
            return st0, tuple(out)

        init = tuple((jnp.full((1, nq), -jnp.inf, F32), jnp.zeros((1, nq), F32), jnp.zeros((dh, nq), F32))
                     for _ in range(spg))
        carry = (_dg(keys(0), qs[0], 1, 0), init)
        for j in range(nk):
            carry = step(j, carry)
        _, res = carry
        for c in range(spg):
            m, l, acc = res[c]
            o_ref[0, c] = acc / l
            lse_ref[0, c] = _col_bcast(m + jnp.log(l))

    body, r_in, r_out, r_shapes, r_sems, r_args = _riding(body, 3, 2, grid, ride)
    return pl.pallas_call(
        body, name="fa_fwd", grid=grid,
        in_specs=[pl.BlockSpec((1, spg, dh, nq), lambda g, i: (g, i, 0, 0)),
                  pl.BlockSpec((1, lk, LANES), lambda g, i: (g, 0, 0)),
                  pl.BlockSpec((1, lk // nq, dh, nq), lambda g, i: (g, 0, 0, 0))] + r_in,
        out_specs=[pl.BlockSpec((1, spg, dh, nq), lambda g, i: (g, i, 0, 0)),
                   pl.BlockSpec((1, spg, nq, LANES), lambda g, i: (g, i, 0, 0))] + r_out,
        out_shape=[jax.ShapeDtypeStruct((ATT_KV, ns, dh, nq), F32),
                   jax.ShapeDtypeStruct((ATT_KV, ns, nq, LANES), F32)] + r_shapes,
        scratch_shapes=r_sems,
        compiler_params=_params(("arbitrary", "arbitrary")),
    )(qt, ks, vt, *r_args)


def fa_bwd(qs, qt, dos, dot_, ot, lse, ks, kt, vt, ride=None):
    _, ns, dh, nq = qt.shape
    lk = ks.shape[1]
    bk = min(FA_K_BWD, lk)
    nk = lk // bk
    tiles = bk // nq
    spg = min(FA_STRIPS_BWD, ns)
    nc = bk // LANES
    grid = (ATT_KV, ns // spg)

    def body(qs_ref, qt_ref, dos_ref, dot_ref, ot_ref, lse_ref, ks_ref, kt_ref, vt_ref, dq_ref, dk_ref, dv_ref):
        @pl.when(pl.program_id(1) == 0)
        def _():
            dk_ref[...] = jnp.zeros(dk_ref.shape, F32)
            dv_ref[...] = jnp.zeros(dv_ref.shape, F32)

        strips = range(spg)
        lse_b = [lse_ref[0, c] for c in strips]
        d_b = [_col_bcast(jnp.sum(dot_ref[0, c].astype(F32) * ot_ref[0, c], axis=0, keepdims=True)) for c in strips]

        def step(j, dqs):
            ktb, vtb = _pad_rows(_key_block(kt_ref, j, tiles)), _pad_rows(_key_block(vt_ref, j, tiles))
            kb = ks_ref[0, j * bk:(j + 1) * bk, :]
            prods = [(_dg(qs_ref[0, c], ktb, 1, 0), _dg(dos_ref[0, c], vtb, 1, 0)) for c in strips]
            out = []
            for c in strips:
                s, dp = prods[c]
                ps, dss = [], []
                for cc in range(nc):
                    sl = slice(cc * LANES, (cc + 1) * LANES)
                    pc = jnp.exp(s[:, sl] - lse_b[c])
                    ps.append(pc.astype(BF16))
                    dss.append((pc * (dp[:, sl] - d_b[c])).astype(BF16))
                p, ds = jnp.concatenate(ps, axis=1), jnp.concatenate(dss, axis=1)
                dv = _dg(dot_ref[0, c], p, 1, 0)
                dk = _dg(qt_ref[0, c], ds, 1, 0)
                for i in range(tiles):
                    dv_ref[0, j * tiles + i] += dv[:, i * nq:(i + 1) * nq]
                    dk_ref[0, j * tiles + i] += dk[:, i * nq:(i + 1) * nq]
                out.append(dqs[c] + _dg(ds, kb, 1, 0))
            return tuple(out)

        dqs = tuple(jnp.zeros((nq, LANES), F32) for _ in strips)
        for j in range(nk):
            dqs = step(j, dqs)
        for c in strips:
            dq_ref[0, c] = dqs[c].T[:dh]

    sspec = pl.BlockSpec((1, spg, nq, LANES), lambda g, i: (g, i, 0, 0))
    tspec = pl.BlockSpec((1, spg, dh, nq), lambda g, i: (g, i, 0, 0))
    kspec = pl.BlockSpec((1, lk // nq, dh, nq), lambda g, i: (g, 0, 0, 0))
    body, r_in, r_out, r_shapes, r_sems, r_args = _riding(body, 9, 3, grid, ride)
    return pl.pallas_call(
        body, name="fa_bwd", grid=grid,
        in_specs=[sspec, tspec, sspec, tspec, tspec, sspec, pl.BlockSpec((1, lk, LANES), lambda g, i: (g, 0, 0)),
                  kspec, kspec] + r_in,
        out_specs=[tspec, kspec, kspec] + r_out,
        out_shape=[jax.ShapeDtypeStruct((ATT_KV, ns, dh, nq), F32),
                   jax.ShapeDtypeStruct((ATT_KV, lk // nq, dh, nq), F32),
                   jax.ShapeDtypeStruct((ATT_KV, lk // nq, dh, nq), F32)] + r_shapes,
        scratch_shapes=r_sems,
        compiler_params=_params(("arbitrary", "arbitrary")),
    )(qs, qt, dos, dot_, ot, lse, ks, kt, vt, *r_args)


def _post_mix_fn(of, ob, ug, oa, hgw, attw):
    o = of + ob
    parts = []
    for h in range(HG_HEADS):
        parts.append(_rms(o[:, h * HG_D:(h + 1) * HG_D], hgw))
    hg = jnp.concatenate(parts, axis=1) * (ug * _sigmoid(ug))
    return jnp.concatenate([hg, _rms(oa, attw)], axis=1)


def _swiglu_fn(gate, up):
    return gate * _sigmoid(gate) * up


def _loss_fn(x2, w, tgt):
    e = _rms(x2, w) - tgt
    return 0.5 * jnp.sum(jnp.mean(e * e, axis=-1, keepdims=True), axis=0, keepdims=True)


def _place():
    return lax.axis_index("x"), lax.axis_index("y"), lax.axis_index("c")


def _other_chips(x, y):
    return [(1 - x, y), (x, 1 - y), (1 - x, 1 - y)]


def _hbm_specs(n):
    return [pl.BlockSpec(memory_space=pl.ANY)] * n


SEMS_PER_ARRAY = 7


def _exchange_sems(n):
    return [pltpu.SemaphoreType.DMA((SEMS_PER_ARRAY * n,)), pltpu.SemaphoreType.DMA((SEMS_PER_ARRAY * n,)),
            pltpu.SemaphoreType.DMA((n,))]


def _exchange_phases(srcs, outs, ssem, rsem, lsem, kind):
    n = len(srcs)
    x, y, c = _place()
    k = 2 * x + y
    sib = (x, y, 1 - c)
    if kind == "siblings":
        def swap(a):
            hr = srcs[a].shape[1] // 2
            return pltpu.make_async_remote_copy(src_ref=srcs[a].at[:, pl.ds((1 - c) * hr, hr), :], dst_ref=outs[a],
                                                send_sem=ssem.at[a], recv_sem=rsem.at[a], device_id=sib,
                                                device_id_type=MESH)

        def start_swaps():
            for a in range(n):
                swap(a).start()

        def finish_swaps():
            for a in range(n):
                swap(a).wait()

        return start_swaps, lambda: None, finish_swaps
    reduce = kind == "reduce"
    chips = _other_chips(x, y)
    pairs = [(a, j) for a in range(n) for j in range(3)]

    def hrows(a):
        return srcs[a].shape[1] if reduce else srcs[a].shape[0] // 2

    def half(a, kk, cc):
        return outs[a].at[kk, pl.ds(cc * hrows(a), hrows(a)), :]

    def mine(a, kk):
        return srcs[a].at[kk] if reduce else srcs[a].at[pl.ds(c * hrows(a), hrows(a)), :]

    def copy(a, j, src_ref, dst_ref, to):
        return pltpu.make_async_remote_copy(src_ref=src_ref, dst_ref=dst_ref, send_sem=ssem.at[SEMS_PER_ARRAY * a + j],
                                            recv_sem=rsem.at[SEMS_PER_ARRAY * a + j], device_id=to, device_id_type=MESH)

    def local(a):
        if reduce:
            return pltpu.make_async_copy(srcs[a].at[k], half(a, k, c), lsem.at[a])
        return pltpu.make_async_copy(srcs[a], outs[a].at[k], lsem.at[a])

    def ici(a, j, arriving):
        px, py = chips[j]
        kk = 2 * px + py
        if arriving:
            return copy(a, j, mine(a, k), half(a, kk, c), (px, py, c))
        return copy(a, j, mine(a, kk), half(a, k, c), (px, py, c))

    def passed(a, j, arriving):
        px, py = chips[j]
        kk = 2 * px + py
        return copy(a, 3 + j, half(a, kk, c), half(a, kk, (1 - c) if arriving else c), sib)

    def own(a, arriving):
        return copy(a, 6, mine(a, k), half(a, k, (1 - c) if arriving else c), sib)

    def start():
        for a in range(n):
            local(a).start()
        for a, j in pairs:
            ici(a, j, False).start()
        if reduce:
            for a in range(n):
                own(a, False).start()

    def forward():
        for a, j in pairs:
            ici(a, j, True).wait_recv()
            passed(a, j, False).start()

    def finish():
        for a in range(n):
            if reduce:
                own(a, True).wait_recv()
            for j in range(3):
                passed(a, j, True).wait_recv()
        for a, j in pairs:
            ici(a, j, False).wait_send()
            passed(a, j, False).wait_send()
        for a in range(n):
            if reduce:
                own(a, False).wait_send()
            local(a).wait()

    return start, forward, finish


def _exchange_shapes(arrays, kind):
    if kind == "siblings":
        return [jax.ShapeDtypeStruct((N_CHIPS, g.shape[1] // 2, g.shape[2]), g.dtype) for g in arrays]
    if kind == "reduce":
        return [jax.ShapeDtypeStruct((N_CHIPS, 2 * p.shape[1], p.shape[2]), p.dtype) for p in arrays]
    return [jax.ShapeDtypeStruct((N_CHIPS,) + s.shape, s.dtype) for s in arrays]


def exchange(arrays, kind, name):
    n = len(arrays)

    def body(*refs):
        start, forward, finish = _exchange_phases(refs[:n], refs[n:2 * n], *refs[2 * n:], kind)
        start()
        forward()
        finish()

    return pl.pallas_call(
        body, name=name, in_specs=_hbm_specs(n), out_specs=_hbm_specs(n),
        out_shape=_exchange_shapes(arrays, kind), scratch_shapes=_exchange_sems(n),
    )(*arrays)


def allreduce_small(p, name):
    rows, width = p.shape

    def body(p_ref, s_ref, gath, ssem, rsem):
        x, y, c = _place()
        me = 4 * x + 2 * y + c
        copies = []
        for d in range(1, N_DEV):
            dx, dy, dc = (d >> 2) & 1, (d >> 1) & 1, d & 1
            peer = (x ^ dx, y ^ dy, c ^ dc)
            cp = pltpu.make_async_remote_copy(src_ref=p_ref, dst_ref=gath.at[me], send_sem=ssem.at[d - 1],
                                              recv_sem=rsem.at[d - 1], device_id=peer, device_id_type=MESH)
            cp.start()
            copies.append(cp)
        gath[me] = p_ref[...]
        for d, cp in enumerate(copies, start=1):
            dx, dy, dc = (d >> 2) & 1, (d >> 1) & 1, d & 1
            peer_slot = 4 * (x ^ dx) + 2 * (y ^ dy) + (c ^ dc)
            pltpu.make_async_remote_copy(src_ref=p_ref, dst_ref=gath.at[peer_slot], send_sem=ssem.at[d - 1],
                                         recv_sem=rsem.at[d - 1], device_id=(x ^ dx, y ^ dy, c ^ dc),
                                         device_id_type=MESH).wait_recv()
        for cp in copies:
            cp.wait_send()
        acc = gath[0]
        for d in range(1, N_DEV):
            acc = acc + gath[d]
        s_ref[...] = acc

    return pl.pallas_call(
        body, name=name,
        in_specs=[pl.BlockSpec(memory_space=pltpu.VMEM)],
        out_specs=pl.BlockSpec(memory_space=pltpu.VMEM),
        out_shape=jax.ShapeDtypeStruct((rows, width), F32),
        scratch_shapes=[pltpu.VMEM((N_DEV, rows, width), F32), pltpu.SemaphoreType.DMA((N_DEV - 1,)),
                        pltpu.SemaphoreType.DMA((N_DEV - 1,))],
    )(p)


def add_my_half(g, recv, name):
    _, r, cols = g.shape
    hr = r // 2
    tb = _pick(hr, 512, 8)
    nb = hr // tb
    c_arr = lax.axis_index("c").astype(jnp.int32).reshape(1)

    def body(c_ref, g_ref, r_ref, o_ref):
        o_ref[...] = (g_ref[...] + r_ref[...]).astype(o_ref.dtype)

    return pl.pallas_call(
        body, name=name,
        grid_spec=pltpu.PrefetchScalarGridSpec(
            num_scalar_prefetch=1, grid=(N_CHIPS, nb),
            in_specs=[pl.BlockSpec((1, tb, cols), lambda k, i, c_ref: (k, c_ref[0] * nb + i, 0)),
                      pl.BlockSpec((1, tb, cols), lambda k, i, c_ref: (k, i, 0))],
            out_specs=pl.BlockSpec((1, tb, cols), lambda k, i, c_ref: (k, i, 0))),
        out_shape=jax.ShapeDtypeStruct((N_CHIPS, hr, cols), BF16),
        compiler_params=_params(("arbitrary", "arbitrary")),
    )(c_arr, g, recv)


def sum_chips(parts, name):
    _, hr, cols = parts.shape
    tb = _pick(hr, 512, 8)

    def body(p_ref, o_ref):
        o_ref[...] = ((p_ref[0].astype(F32) + p_ref[1].astype(F32)) + p_ref[2].astype(F32)) + p_ref[3].astype(F32)

    return pl.pallas_call(
        body, name=name, grid=(hr // tb,),
        in_specs=[pl.BlockSpec((N_CHIPS, tb, cols), lambda i: (0, i, 0))],
        out_specs=pl.BlockSpec((tb, cols), lambda i: (i, 0)),
        out_shape=jax.ShapeDtypeStruct((hr, cols), F32),
        compiler_params=_params(("arbitrary",)),
    )(parts)


def adamw(w, g, m, v, name):
    rows, width = w.shape
    tb = _pick(rows, 512, 8)

    def body(w_ref, g_ref, m_ref, v_ref, d_ref, mo_ref, vo_ref):
        gg = g_ref[...]
        m_new = ADAM_B1 * m_ref[...] + (1.0 - ADAM_B1) * gg
        v_new = ADAM_B2 * v_ref[...] + (1.0 - ADAM_B2) * (gg * gg)
        m_hat = m_new / (1.0 - ADAM_B1 ** ADAM_STEP)
        v_hat = v_new / (1.0 - ADAM_B2 ** ADAM_STEP)
        d_ref[...] = -ADAM_LR * (m_hat / (jnp.sqrt(v_hat) + ADAM_EPS) + ADAM_WD * w_ref[...])
        mo_ref[...] = m_new
        vo_ref[...] = v_new

    spec = pl.BlockSpec((tb, width), lambda i: (i, 0))
    return pl.pallas_call(
        body, name=name, grid=(rows // tb,), in_specs=[spec] * 4, out_specs=[spec] * 3,
        out_shape=[jax.ShapeDtypeStruct((rows, width), F32)] * 3,
        compiler_params=_params(("arbitrary",)),
    )(w, g, m, v)


def _pack_rows(vecs, width=1024):
    rows, cur, used = [], [], 0
    for v in vecs:
        n = v.shape[1]
        if used + n > width:
            cur.append(jnp.zeros((1, width - used), F32))
            rows.append(jnp.concatenate(cur, axis=1))
            cur, used = [], 0
        cur.append(v)
        used += n
    cur.append(jnp.zeros((1, width - used), F32))
    rows.append(jnp.concatenate(cur, axis=1))
    return rows


def _pad128(v):
    return jnp.pad(v, ((0, 0), (0, LANES - v.shape[1])))


def kernel(x, norm1_w, w_in, lb_logits, hg_norm_w, q_norm_w, k_norm_w, att_norm_w, w_out, norm2_w, w_gate_up, w_down, final_norm_w, loss_target, m_norm1_w, m_w_in, m_lb_logits, m_hg_norm_w, m_q_norm_w, m_k_norm_w, m_att_norm_w, m_w_out, m_norm2_w, m_w_gate_up, m_w_down, m_final_norm_w, v_norm1_w, v_w_in, v_lb_logits, v_hg_norm_w, v_q_norm_w, v_k_norm_w, v_att_norm_w, v_w_out, v_norm2_w, v_w_gate_up, v_w_down, v_final_norm_w):
    t, d = x.shape[1], x.shape[2]
    xi, yi, ci = _place()
    chip = 2 * xi + yi
    x2d = x.reshape(t, d)
    tgt = loss_target.reshape(t, d)
    tok = min(TOK, t)
    nt = t // tok
    ns = ATT_GROUP * nt

    late_w = [w_out[0].astype(BF16), w_gate_up[0].astype(BF16), w_down[0].astype(BF16)]
    lb_rows = lb_logits.reshape(4, LANES) * (ci == 0).astype(F32)
    lb_pad = lax.dynamic_update_slice(jnp.zeros((8, 1024), F32), lb_rows, (0, chip * LANES))
    lb_full = allreduce_small(lb_pad, "gather_lb")[:4, :HG_W]
    l_f0, l_f1, l_b0, l_b1 = (lb_full[i:i + 1] for i in range(4))

    n1 = norm1_w.reshape(1, d)
    n2 = norm2_w.reshape(1, d)
    nf = final_norm_w.reshape(1, d)
    tm = min(512, t)
    h1, g_in = _rows(lambda a, w: ((_rms(a, w),), ()), "norm1", t, tm, [_rin(x2d, tm)], [n1], [_rout(t, tm, d, BF16)],
                     ride=("gather", [w_in[0].astype(BF16)]))
    wf_in = g_in.transpose(1, 0, 2).reshape(g_in.shape[1], -1)
    u = mm_rows(h1, wf_in, "mm_in")
    o_f, ss_f = gla_fwd(u, l_f0, l_f1, 1, False, "gla_fwd_f")
    o_b, ss_b = gla_fwd(u, l_b0, l_b1, 2, True, "gla_fwd_b")

    cos2, sin2 = _rope_tables(t)
    cq, sq = jnp.tile(cos2, (1, ATT_HEADS)), jnp.tile(sin2, (1, ATT_HEADS))
    ck, sk = jnp.tile(cos2, (1, ATT_KV)), jnp.tile(sin2, (1, ATT_KV))
    qw, kw = _pad128(q_norm_w.reshape(1, ATT_DH)), _pad128(k_norm_w.reshape(1, ATT_DH))
    gq, gk = _group_sum_matrix(ATT_QW), _group_sum_matrix(ATT_KVW)
    tq, tk = _tile_matrix(ATT_QW), _tile_matrix(ATT_KVW)
    prep_in = [_rin(u, tok, ATT_QW, 5), _rin(u, tok, ATT_KVW, 24), _rin(cq, tok), _rin(sq, tok), _rin(ck, tok),
               _rin(sk, tok)]
    prep_consts = [qw, kw, gq, gk, tq, tk]

    def att_prep_fn(aq, ak, av, *rest):
        q, k = _att_prep_fn(aq, ak, *rest)
        return (_head_t(q, ATT_HEADS), _head_s(q, ATT_HEADS), _head_s(k, ATT_KV), _head_t(k, ATT_KV),
                _head_t(av, ATT_KV)), ()

    q_t, q_s, k_s, k_t, v_t = _rows(
        att_prep_fn, "att_prep", t, tok, prep_in[:2] + [_rin(u, tok, ATT_KVW, 25)] + prep_in[2:], prep_consts,
        [_tout(ATT_HEADS, nt, ATT_DH, tok, BF16), _tout(ATT_HEADS, nt, tok, LANES, BF16),
         _tout(ATT_KV, nt, tok, LANES, BF16), _tout(ATT_KV, nt, ATT_DH, tok, BF16),
         _tout(ATT_KV, nt, ATT_DH, tok, BF16)])
    q_t = q_t.reshape(ATT_KV, ns, ATT_DH, tok)
    q_s = q_s.reshape(ATT_KV, ns, tok, LANES)
    k_s = k_s.reshape(ATT_KV, t, LANES)
    o_t, lse, g_out, g_gu, g_down = fa_fwd(q_t, k_s, v_t, ride=("gather", late_w))
    wf_gu = g_gu.transpose(1, 0, 2).reshape(g_gu.shape[1], -1)
    wf_out = g_out.reshape(-1, g_out.shape[2])
    wf_down = g_down.reshape(-1, g_down.shape[2])
    o_tiles = o_t.reshape(ATT_HEADS, nt, ATT_DH, tok)

    hgw = hg_norm_w.reshape(1, HG_D)
    attw = att_norm_w.reshape(1, ATT_QW)
    mix_in = [_rin(o_f, tok), _rin(o_b, tok), _rin(u, tok, HG_W, 4), _tin(o_tiles)]
    (mix,) = _rows(lambda of, ob, ug, ot, hw, aw: ((_post_mix_fn(of, ob, ug, _from_head_t(ot), hw, aw),), ()),
                   "post_mix", t, tok, mix_in, [hgw, attw], [_rout(t, tok, d, BF16)])
    x1, h2 = mm_res_norm(mix, wf_out, x2d, n2, "mm_out")
    gu2, act = ffn_up(h2, wf_gu)

    dx2, dx2b, g_final, loss_part = mm_res_loss(act, wf_down, x1, tgt, nf, "mm_down_loss")
    dgu2 = ffn_down_dx(dx2b, wf_down, gu2)
    gw_down = mm_tn(dx2b, act, "mm_down_dw", transpose_out=True)
    dx1, dx1b, g_norm2 = mm_nt_norm_bwd(dgu2, wf_gu, x1, n2, dx2, "mm_gate_up_dx", True)
    gw_gu = mm_tn(h2, dgu2, "mm_gate_up_dw", col_shards=N_CHIPS)
    dmix = mm_rows(dx1b, wf_out, "mm_out_dx", trans_b=True)
    gw_out = mm_tn(mix, dx1b, "mm_out_dw")

    def post_mix_bwd_fn(of, ob, ug, ot, dm, hgw_, attw_):
        _, vjp = jax.vjp(_post_mix_fn, of, ob, ug, _from_head_t(ot), hgw_, attw_)
        dof, _, dug, doa, dhgw, dattw = vjp(dm)
        return (dof, dug, _head_t(doa, ATT_HEADS), _head_s(doa, ATT_HEADS)), (dhgw, dattw)

    late = ["w_out", "w_gate_up", "w_down"]
    late_g = [gw_out.reshape(N_CHIPS, -1, d), gw_gu, gw_down.reshape(N_CHIPS, -1, d)]
    do_hg, du_g, do_t, do_s, g_hg, g_att, *late_recv = _rows(
        post_mix_bwd_fn, "post_mix_bwd", t, tok, mix_in + [_rin(dmix, tok)], [hgw, attw],
        [_rout(t, tok, HG_W, F32), _rout(t, tok, HG_W, BF16), _tout(ATT_HEADS, nt, ATT_DH, tok, BF16),
         _tout(ATT_HEADS, nt, tok, LANES, BF16)], [HG_D, ATT_QW], ride=("siblings", late_g))
    late_part = [add_my_half(g, r, "add_my_half_" + n) for g, r, n in zip(late_g, late_recv, late)]
    dq_t, dk_t, dv_t, *late_parts = fa_bwd(q_s, q_t, do_s.reshape(q_s.shape), do_t.reshape(q_t.shape), o_t, lse, k_s, k_t,
                                           v_t, ride=("reduce", late_part))

    dq_f, df_f, di_f, dl_f0, dl_f1 = gla_bwd(u, l_f0, l_f1, ss_f, do_hg, 1, False, "gla_bwd_f")
    dq_b, df_b, di_b, dl_b0, dl_b1 = gla_bwd(u, l_b0, l_b1, ss_b, do_hg, 2, True, "gla_bwd_b")

    def att_prep_bwd_fn(aq, ak, cq_, sq_, ck_, sk_, dqt, dkt, dvt, qf, qb, ff, fb, i_f, i_b, dg,
                        qw_, kw_, gq_, gk_, tq_, tk_):
        _, vjp = jax.vjp(lambda a, b, c_, e: _att_prep_fn(a, b, cq_, sq_, ck_, sk_, c_, e, gq_, gk_, tq_, tk_),
                         aq, ak, qw_, kw_)
        daq, dak, dqw, dkw = vjp((_from_head_t(dqt), _from_head_t(dkt)))
        du_tile = jnp.concatenate([qf + qb, ff, fb, i_f + i_b, dg.astype(F32), daq, dak, _from_head_t(dvt)], axis=1)
        return (du_tile,), (dqw, dkw)

    du, g_q, g_k = _rows(
        att_prep_bwd_fn, "att_prep_bwd", t, tok,
        prep_in + [_tin(dq_t.reshape(ATT_HEADS, nt, ATT_DH, tok)), _tin(dk_t), _tin(dv_t)]
        + [_rin(a, tok) for a in (dq_f, dq_b, df_f, df_b, di_f, di_b, du_g)], prep_consts,
        [_rout(t, tok, u.shape[1], BF16)], [LANES, LANES])
    gw_in_t = mm_tn(h1, du, "mm_in_dw", transpose_out=True)

    names = ["w_in"] + late
    g_in4 = gw_in_t.reshape(N_CHIPS, -1, d)
    in_part = add_my_half(g_in4, exchange([g_in4], "siblings", "rs_siblings_w_in")[0], "add_my_half_w_in")
    grad_x, g_norm1, in_parts = mm_nt_norm_bwd(du, wf_in, x2d, n1, dx1, "mm_in_dx", False, ride=("reduce", [in_part]))
    parts = [in_parts] + late_parts
    g_shard = [sum_chips(p, "sum_chips_" + n) for p, n in zip(parts, names)]
    g_shard[0] = g_shard[0].T
    big = {}
    for n, g, w, m, v in zip(names, g_shard, (w_in, w_out, w_gate_up, w_down), (m_w_in, m_w_out, m_w_gate_up, m_w_down),
                             (v_w_in, v_w_out, v_w_gate_up, v_w_down)):
        dlt, mn, vn = adamw(w[0], g, m[0], v[0], "adamw_" + n)
        big[n] = (g[None], dlt[None], mn[None], vn[None])

    small = [g_norm1, g_norm2, g_final, g_att, g_hg, g_q, g_k, loss_part, dl_f0, dl_f1, dl_b0, dl_b1]
    packed = _pack_rows(small)
    packed += [jnp.zeros((1, 1024), F32)] * (8 - len(packed))
    tot = allreduce_small(jnp.concatenate(packed, axis=0), "allreduce_small")
    s_norm1, s_norm2, s_final = tot[0:1], tot[1:2], tot[2:3]
    s_att, s_hg, s_q, s_k = tot[3:4, 0:512], tot[3:4, 512:640], tot[3:4, 640:704], tot[3:4, 768:832]
    loss = tot[3, 896]
    s_lb = jnp.concatenate([tot[4:5, 0:512], tot[4:5, 512:1024], tot[5:6, 0:512], tot[5:6, 512:1024]], axis=0)
    s_lb = lax.dynamic_slice(s_lb, (0, chip * LANES), (4, LANES)).reshape(1, 512)

    snames = ["norm1_w", "lb_logits", "hg_norm_w", "q_norm_w", "k_norm_w", "att_norm_w", "norm2_w", "final_norm_w"]
    g_small = dict(zip(snames, [s_norm1, s_lb, s_hg, s_q, s_k, s_att, s_norm2, s_final]))
    w_small = dict(zip(snames, [norm1_w, lb_logits, hg_norm_w, q_norm_w, k_norm_w, att_norm_w, norm2_w, final_norm_w]))
    m_small = dict(zip(snames, [m_norm1_w, m_lb_logits, m_hg_norm_w, m_q_norm_w, m_k_norm_w, m_att_norm_w, m_norm2_w, m_final_norm_w]))
    v_small = dict(zip(snames, [v_norm1_w, v_lb_logits, v_hg_norm_w, v_q_norm_w, v_k_norm_w, v_att_norm_w, v_norm2_w, v_final_norm_w]))

    def pack_small(tree):
        rows = _pack_rows([tree[n].reshape(1, -1) for n in snames])
        rows += [jnp.zeros((1, 1024), F32)] * (8 - len(rows))
        return jnp.concatenate(rows, axis=0)

    d_s, m_s, v_s = adamw(pack_small(w_small), pack_small(g_small), pack_small(m_small), pack_small(v_small), "adamw_small")

    def unpack_small(a):
        out, r, used = {}, 0, 0
        for n in snames:
            size = w_small[n].size
            if used + size > 1024:
                r, used = r + 1, 0
            out[n] = a[r, used:used + size].reshape(w_small[n].shape)
            used += size
        return out

    d_sm, m_sm, v_sm = unpack_small(d_s), unpack_small(m_s), unpack_small(v_s)
    g_sm = {n: g_small[n].reshape(w_small[n].shape) for n in snames}

    order = ["norm1_w", "w_in", "lb_logits", "hg_norm_w", "q_norm_w", "k_norm_w", "att_norm_w", "w_out", "norm2_w",
             "w_gate_up", "w_down", "final_norm_w"]

    def pick(small_tree, idx):
        return [big[n][idx] if n in big else small_tree[n] for n in order]

    return (loss, grad_x.reshape(x.shape), *pick(g_sm, 0), *pick(d_sm, 1), *pick(m_sm, 2), *pick(v_sm, 3))
```

```python
import functools

import numpy as np
import jax
import jax.numpy as jnp
from jax import lax
from jax.experimental import pallas as pl
from jax.experimental.pallas import tpu as pltpu

F32 = jnp.float32
BF16 = jnp.bfloat16
MESH = pl.DeviceIdType.MESH

EPS = 1e-6
GRID_W = 64
HG_HEADS = 4
HG_D = 128
HG_W = HG_HEADS * HG_D
CHUNK = 64
ATT_HEADS = 8
ATT_KV = 2
ATT_GROUP = ATT_HEADS // ATT_KV
ATT_DH = 64
ATT_QW = ATT_HEADS * ATT_DH
ATT_KVW = ATT_KV * ATT_DH
ROPE_THETA = 10000.0
N_CHIPS = 4
N_DEV = 8

ADAM_LR = 0.001
ADAM_B1 = 0.9
ADAM_B2 = 0.999
ADAM_EPS = 1e-08
ADAM_WD = 0.01
ADAM_STEP = 10

VMEM_LIMIT = 52 * 1024 * 1024
LANES = 128
TOK = 256


def _params(sem=None):
    return pltpu.CompilerParams(dimension_semantics=sem, vmem_limit_bytes=VMEM_LIMIT)


def _dg(a, b, ca, cb):
    return lax.dot_general(a.astype(BF16), b.astype(BF16), (((ca,), (cb,)), ((), ())),
                           preferred_element_type=F32)


@jax.custom_vjp
def dot_nn(a, b):
    return _dg(a, b, 1, 0)


def _dot_nn_fwd(a, b):
    return _dg(a, b, 1, 0), (a, b)


def _dot_nn_bwd(res, g):
    a, b = res
    return _dg(g, b, 1, 1), _dg(a, g, 0, 0)


dot_nn.defvjp(_dot_nn_fwd, _dot_nn_bwd)


@jax.custom_vjp
def dot_nt(a, b):
    return _dg(a, b, 1, 1)


def _dot_nt_fwd(a, b):
    return _dg(a, b, 1, 1), (a, b)


def _dot_nt_bwd(res, g):
    a, b = res
    return _dg(g, b, 1, 0), _dg(g, a, 0, 0)


dot_nt.defvjp(_dot_nt_fwd, _dot_nt_bwd)


@jax.custom_vjp
def dot_tn(a, b):
    return _dg(a, b, 0, 0)


def _dot_tn_fwd(a, b):
    return _dg(a, b, 0, 0), (a, b)


def _dot_tn_bwd(res, g):
    a, b = res
    return _dg(b, g, 1, 1), _dg(a, g, 1, 0)


dot_tn.defvjp(_dot_tn_fwd, _dot_tn_bwd)


def _split3(a):
    hi = a.astype(BF16)
    r1 = a - hi.astype(F32)
    mid = r1.astype(BF16)
    lo = (r1 - mid.astype(F32)).astype(BF16)
    return lo, mid, hi


def _sum3(terms):
    lo, mid, hi = terms
    return (lo + mid) + hi


@jax.custom_vjp
def xdot_r(a, m):
    return _sum3([_dg(p, m, 1, 0) for p in _split3(a)])


def _xdot_r_fwd(a, m):
    return xdot_r(a, m), m


def _xdot_r_bwd(m, g):
    return _sum3([_dg(p, m, 1, 1) for p in _split3(g)]), jnp.zeros_like(m)


xdot_r.defvjp(_xdot_r_fwd, _xdot_r_bwd)


@jax.custom_vjp
def xdot_l(m, a):
    return _sum3([_dg(m, p, 1, 0) for p in _split3(a)])


def _xdot_l_fwd(m, a):
    return xdot_l(m, a), m


def _xdot_l_bwd(m, g):
    return jnp.zeros_like(m), _sum3([_dg(m, p, 0, 0) for p in _split3(g)])


xdot_l.defvjp(_xdot_l_fwd, _xdot_l_bwd)


@jax.custom_vjp
def swap_pairs(y):
    n = y.shape[-1]
    lane = lax.broadcasted_iota(jnp.int32, y.shape, 1)
    nxt = pltpu.roll(y, n - 1, 1)
    prv = pltpu.roll(y, 1, 1)
    return jnp.where(lane % 2 == 0, nxt, prv)


def _swap_fwd(y):
    return swap_pairs(y), None


def _swap_bwd(_, g):
    return (swap_pairs(g),)


swap_pairs.defvjp(_swap_fwd, _swap_bwd)


def _rms(x, w):
    return x * lax.rsqrt(jnp.mean(x * x, axis=-1, keepdims=True) + EPS) * w


def _sigmoid(x):
    return jax.nn.sigmoid(x)


def _rows(fn, name, t, tm, ins, consts, outs, accs=(), ride=None):
    n_r, n_c, n_o, n_a = len(ins), len(consts), len(outs), len(accs)

    def body(*refs):
        r = refs[:n_r]
        c = refs[n_r:n_r + n_c]
        o = refs[n_r + n_c:n_r + n_c + n_o]
        a = refs[n_r + n_c + n_o:]
        ro, ao = fn(*[x[...] for x in r], *[x[...] for x in c])
        for ref, val in zip(o, ro):
            if isinstance(val, (list, tuple)):
                for h, piece in enumerate(val):
                    ref[h, 0] = piece.astype(ref.dtype)
            else:
                ref[...] = val.astype(ref.dtype)
        if n_a:
            @pl.when(pl.program_id(0) == 0)
            def _():
                for ref in a:
                    ref[...] = jnp.zeros(ref.shape, F32)
            for ref, val in zip(a, ao):
                ref[...] += val

    in_specs = [s for _, s in ins]
    in_specs += [pl.BlockSpec(c.shape, lambda i, nd=c.ndim: (0,) * nd) for c in consts]
    out_specs = [s for _, s in outs] + [pl.BlockSpec((1, w), lambda i: (0, 0)) for w in accs]
    out_shape = [s for s, _ in outs] + [jax.ShapeDtypeStruct((1, w), F32) for w in accs]
    grid = (t // tm,)
    body, r_in, r_out, r_shapes, r_sems, r_args = _riding(body, n_r + n_c, n_o + n_a, grid, ride)
    return pl.pallas_call(
        body, name=name, grid=grid, in_specs=in_specs + r_in, out_specs=out_specs + r_out,
        out_shape=out_shape + r_shapes, scratch_shapes=r_sems, compiler_params=_params(("arbitrary",)),
    )(*[a for a, _ in ins], *consts, *r_args)


def _rin(a, tm, width=None, cb=0):
    width = a.shape[1] if width is None else width
    return a, pl.BlockSpec((tm, width), lambda i, cb=cb: (i, cb))


def _rout(t, tm, width, dtype):
    return jax.ShapeDtypeStruct((t, width), dtype), pl.BlockSpec((tm, width), lambda i: (i, 0))


def _tin(a):
    return a, pl.BlockSpec((a.shape[0], 1) + a.shape[2:], lambda i: (0, i, 0, 0))


def _tout(heads, nt, r, c, dtype):
    return jax.ShapeDtypeStruct((heads, nt, r, c), dtype), pl.BlockSpec((heads, 1, r, c), lambda i: (0, i, 0, 0))


def _pick(n, cap, mult):
    best = None
    for d in range(mult, min(n, cap) + 1, mult):
        if n % d == 0:
            best = d
    return best if best is not None else n


def mm_rows(a, b, name, trans_b=False, res=None, out_dtype=F32):
    m, k = a.shape
    n = b.shape[0] if trans_b else b.shape[1]
    tn = _pick(n, 3328, LANES)
    tm = _pick(m, 512 if k <= 3072 else 256, 8)
    has_res = res is not None

    def body(*refs):
        if has_res:
            a_ref, b_ref, r_ref, o_ref = refs
        else:
            a_ref, b_ref, o_ref = refs
        acc = _dg(a_ref[...], b_ref[...], 1, 1 if trans_b else 0)
        if has_res:
            acc = acc + r_ref[...]
        o_ref[...] = acc.astype(o_ref.dtype)

    in_specs = [pl.BlockSpec((tm, k), lambda j, i: (i, 0))]
    if trans_b:
        in_specs.append(pl.BlockSpec((tn, k), lambda j, i: (j, 0)))
    else:
        in_specs.append(pl.BlockSpec((k, tn), lambda j, i: (0, j)))
    args = [a, b]
    if has_res:
        in_specs.append(pl.BlockSpec((tm, tn), lambda j, i: (i, j)))
        args.append(res)
    return pl.pallas_call(
        body, name=name, grid=(n // tn, m // tm), in_specs=in_specs,
        out_specs=pl.BlockSpec((tm, tn), lambda j, i: (i, j)),
        out_shape=jax.ShapeDtypeStruct((m, n), out_dtype),
        compiler_params=_params(("arbitrary", "arbitrary")),
    )(*args)


def mm_tn(a, b, name, col_shards=None, transpose_out=False):
    t, m = a.shape
    parts = b.shape[0] if b.ndim == 3 else 1
    n = parts * b.shape[-1]
    tm = _pick(m, 512, LANES)
    width = n // col_shards if col_shards else None
    group = max(1, 3328 // width) if col_shards else 1
    tn = width * group if col_shards else _pick(n, 3328, LANES)
    tk = _pick(t, 1024, 8)
    nkk = t // tk
    per_part = b.shape[-1] // tn

    def body(a_ref, b_ref, o_ref, *scratch):
        acc_ref = scratch[0] if transpose_out else o_ref

        @pl.when(pl.program_id(2) == 0)
        def _():
            acc_ref[...] = jnp.zeros(acc_ref.shape, F32)
        acc = _dg(a_ref[...], b_ref[0] if b.ndim == 3 else b_ref[...], 0, 0)
        if col_shards:
            for s in range(group):
                acc_ref[s] += acc[:, s * width:(s + 1) * width]
        else:
            acc_ref[...] += acc
        if transpose_out:
            @pl.when(pl.program_id(2) == nkk - 1)
            def _():
                o_ref[...] = acc_ref[...].T

    if col_shards:
        out_spec = pl.BlockSpec((group, tm, width), lambda i, j, kk: (j, i, 0))
        out_shape = jax.ShapeDtypeStruct((col_shards, m, width), F32)
    elif transpose_out:
        out_spec = pl.BlockSpec((tn, tm), lambda i, j, kk: (j, i))
        out_shape = jax.ShapeDtypeStruct((n, m), F32)
    else:
        out_spec = pl.BlockSpec((tm, tn), lambda i, j, kk: (i, j))
        out_shape = jax.ShapeDtypeStruct((m, n), F32)
    if b.ndim == 3:
        b_spec = pl.BlockSpec((1, tk, tn), lambda i, j, kk: (j // per_part, kk, j % per_part))
    else:
        b_spec = pl.BlockSpec((tk, tn), lambda i, j, kk: (kk, j))
    return pl.pallas_call(
        body, name=name, grid=(m // tm, n // tn, nkk),
        in_specs=[pl.BlockSpec((tk, tm), lambda i, j, kk: (kk, i)), b_spec],
        out_specs=out_spec, out_shape=out_shape,
        scratch_shapes=[pltpu.VMEM((tm, tn), F32)] if transpose_out else [],
        compiler_params=_params(("arbitrary", "arbitrary", "arbitrary")),
    )(a, b)


def ffn_up(h2, w_gu):
    t, k = h2.shape
    dff = w_gu.shape[1] // 2
    tn = _pick(dff, 1408, LANES)
    ncol = dff // tn
    tm = _pick(t, 512, 8)

    def body(a_ref, wg_ref, wu_ref, gu_ref, act_ref):
        a = a_ref[...]
        g = _dg(a, wg_ref[...], 1, 0)
        u = _dg(a, wu_ref[...], 1, 0)
        gu_ref[0] = g.astype(gu_ref.dtype)
        gu_ref[1] = u.astype(gu_ref.dtype)
        act_ref[...] = _swiglu_fn(g, u).astype(act_ref.dtype)

    return pl.pallas_call(
        body, name="ffn_up", grid=(ncol, t // tm),
        in_specs=[pl.BlockSpec((tm, k), lambda s, i: (i, 0)), pl.BlockSpec((k, tn), lambda s, i: (0, s)),
                  pl.BlockSpec((k, tn), lambda s, i: (0, s + ncol))],
        out_specs=[pl.BlockSpec((2, tm, tn), lambda s, i: (0, i, s)), pl.BlockSpec((tm, tn), lambda s, i: (i, s))],
        out_shape=[jax.ShapeDtypeStruct((2, t, dff), BF16), jax.ShapeDtypeStruct((t, dff), BF16)],
        compiler_params=_params(("arbitrary", "arbitrary")),
    )(h2, w_gu, w_gu)


def ffn_down_dx(dx2b, w_down, gu2):
    t, k = dx2b.shape
    dff = w_down.shape[0]
    tn = _pick(dff, 1408, LANES)
    tm = _pick(t, 512, 8)

    def body(a_ref, w_ref, gu_ref, o_ref):
        dact = _dg(a_ref[...], w_ref[...], 1, 1)
        _, vjp = jax.vjp(_swiglu_fn, gu_ref[0].astype(F32), gu_ref[1].astype(F32))
        dg, du = vjp(dact)
        o_ref[0] = dg.astype(o_ref.dtype)
        o_ref[1] = du.astype(o_ref.dtype)

    blk = pl.BlockSpec((2, tm, tn), lambda s, i: (0, i, s))
    return pl.pallas_call(
        body, name="ffn_down_dx", grid=(dff // tn, t // tm),
        in_specs=[pl.BlockSpec((tm, k), lambda s, i: (i, 0)), pl.BlockSpec((tn, k), lambda s, i: (s, 0)), blk],
        out_specs=blk, out_shape=jax.ShapeDtypeStruct((2, t, dff), BF16),
        compiler_params=_params(("arbitrary", "arbitrary")),
    )(dx2b, w_down, gu2)


def mm_res_norm(a, b, res, w, name):
    t, k = a.shape
    d = b.shape[1]
    tm = _pick(t, 512, 8)

    def body(a_ref, b_ref, r_ref, w_ref, x_ref, h_ref):
        xv = _dg(a_ref[...], b_ref[...], 1, 0) + r_ref[...]
        x_ref[...] = xv
        h_ref[...] = _rms(xv, w_ref[...]).astype(h_ref.dtype)

    row = pl.BlockSpec((tm, d), lambda i: (i, 0))
    return pl.pallas_call(
        body, name=name, grid=(t // tm,),
        in_specs=[pl.BlockSpec((tm, k), lambda i: (i, 0)), pl.BlockSpec((k, d), lambda i: (0, 0)), row,
                  pl.BlockSpec((1, d), lambda i: (0, 0))],
        out_specs=[row, row],
        out_shape=[jax.ShapeDtypeStruct((t, d), F32), jax.ShapeDtypeStruct((t, d), BF16)],
        compiler_params=_params(("arbitrary",)),
    )(a, b, res, w)


def mm_res_loss(a, b, res, tgt, w, name):
    t, k = a.shape
    d = b.shape[1]
    tm = _pick(t, 256, 8)

    def body(a_ref, b_ref, r_ref, t_ref, w_ref, dx_ref, dxb_ref, dw_ref, loss_ref):
        @pl.when(pl.program_id(0) == 0)
        def _():
            dw_ref[...] = jnp.zeros(dw_ref.shape, F32)
            loss_ref[...] = jnp.zeros(loss_ref.shape, F32)
        xv = _dg(a_ref[...], b_ref[...], 1, 0) + r_ref[...]
        tg = t_ref[...]
        val, vjp = jax.vjp(lambda x_, w_: _loss_fn(x_, w_, tg), xv, w_ref[...])
        dx, dw = vjp(jnp.ones((1, 1), F32))
        dx_ref[...] = dx
        dxb_ref[...] = dx.astype(dxb_ref.dtype)
        dw_ref[...] += dw
        loss_ref[...] += jnp.broadcast_to(val, (1, LANES))

    row = pl.BlockSpec((tm, d), lambda i: (i, 0))
    vec = pl.BlockSpec((1, d), lambda i: (0, 0))
    return pl.pallas_call(
        body, name=name, grid=(t // tm,),
        in_specs=[pl.BlockSpec((tm, k), lambda i: (i, 0)), pl.BlockSpec((k, d), lambda i: (0, 0)), row, row, vec],
        out_specs=[row, row, vec, pl.BlockSpec((1, LANES), lambda i: (0, 0))],
        out_shape=[jax.ShapeDtypeStruct((t, d), F32), jax.ShapeDtypeStruct((t, d), BF16),
                   jax.ShapeDtypeStruct((1, d), F32), jax.ShapeDtypeStruct((1, LANES), F32)],
        compiler_params=_params(("arbitrary",)),
    )(a, b, res, tgt, w)


def mm_nt_norm_bwd(a, wmat, x, w, dres, name, with_bf16, ride=None):
    parts = a.shape[0] if a.ndim == 3 else 1
    t, kp = a.shape[-2], a.shape[-1]
    d = wmat.shape[0]
    tm = _pick(t, 256, 8)

    n_out = 3 if with_bf16 else 2

    def body(*refs):
        a_ref, w_refs = refs[0], refs[1:1 + parts]
        x_ref, nw_ref, r_ref, dx_ref = refs[1 + parts:5 + parts]
        dw_ref = refs[4 + parts + n_out - 1]

        @pl.when(pl.program_id(0) == 0)
        def _():
            dw_ref[...] = jnp.zeros(dw_ref.shape, F32)
        dh = None
        for p in range(parts):
            term = _dg(a_ref[p] if a.ndim == 3 else a_ref[...], w_refs[p][...], 1, 1)
            dh = term if dh is None else dh + term
        _, vjp = jax.vjp(_rms, x_ref[...], nw_ref[...])
        dx, dw = vjp(dh)
        dx = dx + r_ref[...]
        dx_ref[...] = dx
        if with_bf16:
            refs[5 + parts][...] = dx.astype(BF16)
        dw_ref[...] += dw

    row = pl.BlockSpec((tm, d), lambda i: (i, 0))
    vec = pl.BlockSpec((1, d), lambda i: (0, 0))
    a_spec = pl.BlockSpec((parts, tm, kp), lambda i: (0, i, 0)) if a.ndim == 3 else pl.BlockSpec((tm, kp), lambda i: (i, 0))
    w_specs = [pl.BlockSpec((d, kp), lambda i, p=p: (0, p)) for p in range(parts)]
    outs = [row] + ([row] if with_bf16 else []) + [vec]
    shapes = [jax.ShapeDtypeStruct((t, d), F32)] + ([jax.ShapeDtypeStruct((t, d), BF16)] if with_bf16 else [])
    grid = (t // tm,)
    body, r_in, r_out, r_shapes, r_sems, r_args = _riding(body, 4 + parts, n_out, grid, ride)
    return pl.pallas_call(
        body, name=name, grid=grid,
        in_specs=[a_spec] + w_specs + [row, vec, row] + r_in,
        out_specs=outs + r_out, out_shape=shapes + [jax.ShapeDtypeStruct((1, d), F32)] + r_shapes,
        scratch_shapes=r_sems, compiler_params=_params(("arbitrary",)),
    )(a, *([wmat] * parts), x, w, dres, *r_args)


def _lower_bound(l0, l1):
    m = jnp.maximum(l0, l1)
    e0 = jnp.exp(l0 - m)
    e1 = jnp.exp(l1 - m)
    return e0 / (e0 + e1)


def _gla_consts(rev):
    ri = lax.broadcasted_iota(jnp.int32, (CHUNK, CHUNK), 0)
    ci = lax.broadcasted_iota(jnp.int32, (CHUNK, CHUNK), 1)
    keep = (ci >= ri) if rev else (ci <= ri)
    ref_mask = lax.broadcasted_iota(jnp.int32, (CHUNK, 1), 0) == (CHUNK // 2 if rev else CHUNK // 2 - 1)
    return keep, ref_mask


def _gla_block(uq, uf, ui, l0, l1, st_in, rev):
    ncb = uq.shape[0] // CHUNK
    heads = range(HG_HEADS)
    keep, ref_mask = _gla_consts(rev)
    cum = keep.astype(F32)
    lb = _lower_bound(l0, l1)
    q = uq * _sigmoid(uq)
    k = (1.0 - lb) * _sigmoid(-uf)
    g = jnp.log(lb + (1.0 - lb) * _sigmoid(uf))

    def rows(a, c):
        return a[c * CHUNK:(c + 1) * CHUNK]

    def head(a, h):
        return a[:, h * HG_D:(h + 1) * HG_D]

    bs = [xdot_l(cum, rows(g, c)) for c in range(ncb)]
    q_in, k_in, q_b, k_d, decay = [], [], [], [], []
    for c in range(ncb):
        b = bs[c]
        b_ref = jnp.sum(jnp.where(ref_mask, b, 0.0), axis=0, keepdims=True)
        b_last = jnp.sum(rows(g, c), axis=0, keepdims=True)
        qc, kc = rows(q, c), rows(k, c)
        q_in.append(qc * jnp.exp(b - b_ref))
        k_in.append(kc * jnp.exp(b_ref - b))
        q_b.append(qc * jnp.exp(b))
        k_d.append(kc * jnp.exp(b_last - b))
        decay.append(jnp.exp(b_last))
    scores = [[jnp.where(keep, dot_nt(head(q_in[c], h), head(k_in[c], h)), 0.0) for h in heads] for c in range(ncb)]
    o_intra = [[dot_nn(scores[c][h], head(rows(ui, c), h)) for h in heads] for c in range(ncb)]
    contrib = [[dot_tn(head(rows(ui, c), h), head(k_d[c], h)) for h in heads] for c in range(ncb)]
    st = list(st_in)
    o_rows = [None] * ncb
    for c in (reversed(range(ncb)) if rev else range(ncb)):
        parts = []
        for h in heads:
            parts.append(o_intra[c][h] + dot_nt(head(q_b[c], h), st[h]))
            st[h] = st[h] * head(decay[c], h) + contrib[c][h]
        o_rows[c] = jnp.concatenate(parts, axis=1)
    return jnp.concatenate(o_rows, axis=0), tuple(st)


def _gla_blocks(t):
    tb = min(512, t)
    return tb, t // tb


def gla_fwd(u, l0, l1, fcol, rev, name):
    t = u.shape[0]
    tb, nb = _gla_blocks(t)

    def blk(i):
        return (nb - 1 - i) if rev else i

    def body(uq_ref, uf_ref, ui_ref, l0_ref, l1_ref, o_ref, ss_ref, st_ref):
        @pl.when(pl.program_id(0) == 0)
        def _():
            st_ref[...] = jnp.zeros(st_ref.shape, F32)
        ss_ref[0] = st_ref[...]
        o, st_out = _gla_block(uq_ref[...], uf_ref[...], ui_ref[...], l0_ref[...], l1_ref[...],
                               tuple(st_ref[h] for h in range(HG_HEADS)), rev)
        o_ref[...] = o
        for h in range(HG_HEADS):
            st_ref[h] = st_out[h]

    row = lambda cb: pl.BlockSpec((tb, HG_W), lambda i: (blk(i), cb))
    vec = pl.BlockSpec((1, HG_W), lambda i: (0, 0))
    return pl.pallas_call(
        body, name=name, grid=(nb,),
        in_specs=[row(0), row(fcol), row(3), vec, vec],
        out_specs=[pl.BlockSpec((tb, HG_W), lambda i: (blk(i), 0)),
                   pl.BlockSpec((1, HG_HEADS, HG_D, HG_D), lambda i: (blk(i), 0, 0, 0))],
        out_shape=[jax.ShapeDtypeStruct((t, HG_W), F32),
                   jax.ShapeDtypeStruct((nb, HG_HEADS, HG_D, HG_D), F32)],
        scratch_shapes=[pltpu.VMEM((HG_HEADS, HG_D, HG_D), F32)],
        compiler_params=_params(("arbitrary",)),
    )(u, u, u, l0, l1)


def gla_bwd(u, l0, l1, ss, do, fcol, rev, name):
    t = u.shape[0]
    tb, nb = _gla_blocks(t)

    def blk(i):
        return i if rev else (nb - 1 - i)

    def body(uq_ref, uf_ref, ui_ref, l0_ref, l1_ref, ss_ref, do_ref,
             dq_ref, df_ref, di_ref, dl0_ref, dl1_ref, dst_ref):
        @pl.when(pl.program_id(0) == 0)
        def _():
            dst_ref[...] = jnp.zeros(dst_ref.shape, F32)
            dl0_ref[...] = jnp.zeros(dl0_ref.shape, F32)
            dl1_ref[...] = jnp.zeros(dl1_ref.shape, F32)
        heads = range(HG_HEADS)
        _, vjp = jax.vjp(functools.partial(_gla_block, rev=rev), uq_ref[...], uf_ref[...], ui_ref[...],
                         l0_ref[...], l1_ref[...], tuple(ss_ref[0, h] for h in heads))
        dq, df, di, dl0, dl1, dst = vjp((do_ref[...], tuple(dst_ref[h] for h in heads)))
        dq_ref[...] = dq
        df_ref[...] = df
        di_ref[...] = di
        dl0_ref[...] += dl0
        dl1_ref[...] += dl1
        for h in heads:
            dst_ref[h] = dst[h]

    row = lambda cb: pl.BlockSpec((tb, HG_W), lambda i: (blk(i), cb))
    vec = pl.BlockSpec((1, HG_W), lambda i: (0, 0))
    orow = pl.BlockSpec((tb, HG_W), lambda i: (blk(i), 0))
    return pl.pallas_call(
        body, name=name, grid=(nb,),
        in_specs=[row(0), row(fcol), row(3), vec, vec,
                  pl.BlockSpec((1, HG_HEADS, HG_D, HG_D), lambda i: (blk(i), 0, 0, 0)), orow],
        out_specs=[orow, orow, orow, vec, vec],
        out_shape=[jax.ShapeDtypeStruct((t, HG_W), F32)] * 3 + [jax.ShapeDtypeStruct((1, HG_W), F32)] * 2,
        scratch_shapes=[pltpu.VMEM((HG_HEADS, HG_D, HG_D), F32)],
        compiler_params=_params(("arbitrary",)),
    )(u, u, u, l0, l1, ss, do)


def _rope_tables(t):
    rows = t // GRID_W
    row = jnp.repeat(jnp.arange(rows), GRID_W).astype(F32)
    col = jnp.tile(jnp.arange(GRID_W), rows).astype(F32)
    axis_dim = ATT_DH // 2
    freqs = ROPE_THETA ** (-jnp.arange(0, axis_dim, 2, dtype=F32) / axis_dim)
    ang = jnp.concatenate([row[:, None] * freqs, col[:, None] * freqs], axis=-1)
    cos2 = jnp.repeat(jnp.cos(ang), 2, axis=-1)
    sin2 = jnp.repeat(jnp.sin(ang), 2, axis=-1) * jnp.tile(jnp.array([-1.0, 1.0], F32), ATT_DH // 2)
    return cos2, sin2


def _group_sum_matrix(width):
    idx = np.arange(width) // ATT_DH
    return jnp.asarray((idx[:, None] == idx[None, :]).astype(np.float32))


def _tile_matrix(width):
    m = np.zeros((LANES, width), np.float32)
    m[np.arange(width) % ATT_DH, np.arange(width)] = 1.0
    return jnp.asarray(m)


def _tile_w(w128, tile_m):
    w8 = jnp.broadcast_to(w128, (8, LANES))
    return jnp.sum(xdot_r(w8, tile_m), axis=0, keepdims=True) * 0.125


def _head_norm_rope(a, w128, cos_t, sin_t, gsum, tile_m, scale):
    ssq = xdot_r(a * a, gsum)
    y = a * lax.rsqrt(ssq * (1.0 / ATT_DH) + EPS) * _tile_w(w128, tile_m)
    return (y * cos_t + swap_pairs(y) * sin_t) * scale


def _att_prep_fn(aq, ak, ck, sk, qw, kw, gq, gk, tq, tk):
    reps = aq.shape[1] // ck.shape[1]
    cq, sq = jnp.concatenate([ck] * reps, axis=1), jnp.concatenate([sk] * reps, axis=1)
    q = _head_norm_rope(aq, qw, cq, sq, gq, tq, ATT_DH ** -0.5)
    k = _head_norm_rope(ak, kw, ck, sk, gk, tk, 1.0)
    return q, k


def _head_t(x, heads):
    xt = x.T
    return [xt[h * ATT_DH:(h + 1) * ATT_DH] for h in range(heads)]


def _head_s(x, heads):
    lane = lax.broadcasted_iota(jnp.int32, (x.shape[0], LANES), 1)
    out = []
    for h in range(heads):
        pair = x[:, (h // 2) * LANES:(h // 2 + 1) * LANES]
        if h % 2:
            pair = pltpu.roll(pair, ATT_DH, 1)
        out.append(jnp.where(lane < ATT_DH, pair, 0.0))
    return out


def _from_head_t(tile):
    return jnp.concatenate([tile[h, 0] for h in range(tile.shape[0])], axis=0).T


def _pad_rows(a):
    return jnp.concatenate([a, jnp.zeros(a.shape, a.dtype)], axis=0)


def _col_bcast(row_vec):
    return jnp.broadcast_to(row_vec, (LANES, row_vec.shape[1])).T


FA_K_FWD = 512
FA_K_BWD = 512
FA_STRIPS_FWD = 8
FA_STRIPS_BWD = 8


def _key_block(ref, j, tiles):
    return jnp.concatenate([ref[0, j * tiles + i] for i in range(tiles)], axis=1)


def _riding(body, n_in, n_out, grid, ride):
    if ride is None:
        return body, [], [], [], [], []
    kind, arrays = ride
    n = len(arrays)
    steps = int(np.prod(grid))

    def riding_body(*refs):
        cuts = np.cumsum([0, n_in, n, n_out, n]).tolist()
        own_in, rin, own_out, rout = (refs[a:b] for a, b in zip(cuts[:-1], cuts[1:]))
        start, forward, finish = _exchange_phases(rin, rout, *refs[cuts[-1]:], kind)
        step = 0
        for axis, size in enumerate(grid):
            step = step * size + pl.program_id(axis)
        pl.when(step == 0)(start)
        pl.when(step == steps // 2)(forward)
        body(*own_in, *own_out)
        pl.when(step == steps - 1)(finish)

    return riding_body, _hbm_specs(n), _hbm_specs(n), _exchange_shapes(arrays, kind), _exchange_sems(n), list(arrays)


def fa_fwd(qt, ks, vt, ride=None):
    _, ns, dh, nq = qt.shape
    lk = ks.shape[1]
    bk = min(FA_K_FWD, lk)
    nk = lk // bk
    spg = min(FA_STRIPS_FWD, ns)
    grid = (ATT_KV, ns // spg)

    def body(q_ref, k_ref, v_ref, o_ref, lse_ref):
        qs = [_pad_rows(q_ref[0, c]) for c in range(spg)]

        def keys(j):
            return k_ref[0, j * bk:(j + 1) * bk, :]

        def step(j, carry):
            st0, stats = carry
            kb = keys(j)
            vb = _key_block(v_ref, j, bk // nq)
            sts = [st0] + [_dg(kb, qs[c], 1, 0) for c in range(1, spg)]
            out = []
            for c in range(spg):
                m, l, acc = stats[c]
                m_new = jnp.maximum(m, jnp.max(sts[c], axis=0, keepdims=True))
                alpha = jnp.exp(m - m_new)
                p = jnp.exp(sts[c] - m_new)
                l = alpha * l + jnp.sum(p, axis=0, keepdims=True)
                if c == spg - 1:
                    st0 = _dg(keys(min(j + 1, nk - 1)), qs[0], 1, 0)
                acc = alpha * acc + _dg(vb, p, 1, 0)
                out.append((m_new, l, acc))
            return st0, tuple(out)

        init = tuple((jnp.full((1, nq), -jnp.inf, F32), jnp.zeros((1, nq), F32), jnp.zeros((dh, nq), F32))
                     for _ in range(spg))
        carry = (_dg(keys(0), qs[0], 1, 0), init)
        for j in range(nk):
            carry = step(j, carry)
        _, res = carry
        for c in range(spg):
            m, l, acc = res[c]
            o_ref[0, c] = acc / l
            lse_ref[0, c] = _col_bcast(m + jnp.log(l))

    body, r_in, r_out, r_shapes, r_sems, r_args = _riding(body, 3, 2, grid, ride)
    return pl.pallas_call(
        body, name="fa_fwd", grid=grid,
        in_specs=[pl.BlockSpec((1, spg, dh, nq), lambda g, i: (g, i, 0, 0)),
                  pl.BlockSpec((1, lk, LANES), lambda g, i: (g, 0, 0)),
                  pl.BlockSpec((1, lk // nq, dh, nq), lambda g, i: (g, 0, 0, 0))] + r_in,
        out_specs=[pl.BlockSpec((1, spg, dh, nq), lambda g, i: (g, i, 0, 0)),
                   pl.BlockSpec((1, spg, nq, LANES), lambda g, i: (g, i, 0, 0))] + r_out,
        out_shape=[jax.ShapeDtypeStruct((ATT_KV, ns, dh, nq), F32),
                   jax.ShapeDtypeStruct((ATT_KV, ns, nq, LANES), F32)] + r_shapes,
        scratch_shapes=r_sems,
        compiler_params=_params(("arbitrary", "arbitrary")),
    )(qt, ks, vt, *r_args)


def fa_bwd(qs, qt, dos, dot_, ot, lse, ks, kt, vt, ride=None):
    _, ns, dh, nq = qt.shape
    lk = ks.shape[1]
    bk = min(FA_K_BWD, lk)
    nk = lk // bk
    tiles = bk // nq
    spg = min(FA_STRIPS_BWD, ns)
    nc = bk // LANES
    grid = (ATT_KV, ns // spg)

    def body(qs_ref, qt_ref, dos_ref, dot_ref, ot_ref, lse_ref, ks_ref, kt_ref, vt_ref, dq_ref, dk_ref, dv_ref):
        @pl.when(pl.program_id(1) == 0)
        def _():
            dk_ref[...] = jnp.zeros(dk_ref.shape, F32)
            dv_ref[...] = jnp.zeros(dv_ref.shape, F32)

        strips = range(spg)
        lse_b = [lse_ref[0, c] for c in strips]
        d_b = [_col_bcast(jnp.sum(dot_ref[0, c].astype(F32) * ot_ref[0, c], axis=0, keepdims=True)) for c in strips]

        def step(j, dqs):
            ktb, vtb = _pad_rows(_key_block(kt_ref, j, tiles)), _pad_rows(_key_block(vt_ref, j, tiles))
            kb = ks_ref[0, j * bk:(j + 1) * bk, :]
            prods = [(_dg(qs_ref[0, c], ktb, 1, 0), _dg(dos_ref[0, c], vtb, 1, 0)) for c in strips]
            out = []
            for c in strips:
                s, dp = prods[c]
                ps, dss = [], []
                for cc in range(nc):
                    sl = slice(cc * LANES, (cc + 1) * LANES)
                    pc = jnp.exp(s[:, sl] - lse_b[c])
                    ps.append(pc.astype(BF16))
                    dss.append((pc * (dp[:, sl] - d_b[c])).astype(BF16))
                p, ds = jnp.concatenate(ps, axis=1), jnp.concatenate(dss, axis=1)
                dv = _dg(dot_ref[0, c], p, 1, 0)
                dk = _dg(qt_ref[0, c], ds, 1, 0)
                for i in range(tiles):
                    dv_ref[0, j * tiles + i] += dv[:, i * nq:(i + 1) * nq]
                    dk_ref[0, j * tiles + i] += dk[:, i * nq:(i + 1) * nq]
                out.append(dqs[c] + _dg(ds, kb, 1, 0))
            return tuple(out)

        dqs = tuple(jnp.zeros((nq, LANES), F32) for _ in strips)
        for j in range(nk):
            dqs = step(j, dqs)
        for c in strips:
            dq_ref[0, c] = dqs[c].T[:dh]

    sspec = pl.BlockSpec((1, spg, nq, LANES), lambda g, i: (g, i, 0, 0))
    tspec = pl.BlockSpec((1, spg, dh, nq), lambda g, i: (g, i, 0, 0))
    kspec = pl.BlockSpec((1, lk // nq, dh, nq), lambda g, i: (g, 0, 0, 0))
    body, r_in, r_out, r_shapes, r_sems, r_args = _riding(body, 9, 3, grid, ride)
    return pl.pallas_call(
        body, name="fa_bwd", grid=grid,
        in_specs=[sspec, tspec, sspec, tspec, tspec, sspec, pl.BlockSpec((1, lk, LANES), lambda g, i: (g, 0, 0)),
                  kspec, kspec] + r_in,
        out_specs=[tspec, kspec, kspec] + r_out,
        out_shape=[jax.ShapeDtypeStruct((ATT_KV, ns, dh, nq), F32),
                   jax.ShapeDtypeStruct((ATT_KV, lk // nq, dh, nq), F32),
                   jax.ShapeDtypeStruct((ATT_KV, lk // nq, dh, nq), F32)] + r_shapes,
        scratch_shapes=r_sems,
        compiler_params=_params(("arbitrary", "arbitrary")),
    )(qs, qt, dos, dot_, ot, lse, ks, kt, vt, *r_args)


def _post_mix_fn(of, ob, ug, oa, hgw, attw):
    o = of + ob
    parts = []
    for h in range(HG_HEADS):
        parts.append(_rms(o[:, h * HG_D:(h + 1) * HG_D], hgw))
    hg = jnp.concatenate(parts, axis=1) * (ug * _sigmoid(ug))
    return jnp.concatenate([hg, _rms(oa, attw)], axis=1)


def _swiglu_fn(gate, up):
    return gate * _sigmoid(gate) * up


def _loss_fn(x2, w, tgt):
    e = _rms(x2, w) - tgt
    return 0.5 * jnp.sum(jnp.mean(e * e, axis=-1, keepdims=True), axis=0, keepdims=True)


def _place():
    return lax.axis_index("x"), lax.axis_index("y"), lax.axis_index("c")


def _other_chips(x, y):
    return [(1 - x, y), (x, 1 - y), (1 - x, 1 - y)]


def _hbm_specs(n):
    return [pl.BlockSpec(memory_space=pl.ANY)] * n


SEMS_PER_ARRAY = 7


def _exchange_sems(n):
    return [pltpu.SemaphoreType.DMA((SEMS_PER_ARRAY * n,)), pltpu.SemaphoreType.DMA((SEMS_PER_ARRAY * n,)),
            pltpu.SemaphoreType.DMA((n,))]


def _exchange_phases(srcs, outs, ssem, rsem, lsem, kind):
    n = len(srcs)
    x, y, c = _place()
    k = 2 * x + y
    sib = (x, y, 1 - c)
    if kind == "siblings":
        def swap(a):
            hr = srcs[a].shape[1] // 2
            return pltpu.make_async_remote_copy(src_ref=srcs[a].at[:, pl.ds((1 - c) * hr, hr), :], dst_ref=outs[a],
                                                send_sem=ssem.at[a], recv_sem=rsem.at[a], device_id=sib,
                                                device_id_type=MESH)

        def start_swaps():
            for a in range(n):
                swap(a).start()

        def finish_swaps():
            for a in range(n):
                swap(a).wait()

        return start_swaps, lambda: None, finish_swaps
    reduce = kind == "reduce"
    chips = _other_chips(x, y)
    pairs = [(a, j) for a in range(n) for j in range(3)]

    def hrows(a):
        return srcs[a].shape[1] if reduce else srcs[a].shape[0] // 2

    def half(a, kk, cc):
        return outs[a].at[kk, pl.ds(cc * hrows(a), hrows(a)), :]

    def mine(a, kk):
        return srcs[a].at[kk] if reduce else srcs[a].at[pl.ds(c * hrows(a), hrows(a)), :]

    def copy(a, j, src_ref, dst_ref, to):
        return pltpu.make_async_remote_copy(src_ref=src_ref, dst_ref=dst_ref, send_sem=ssem.at[SEMS_PER_ARRAY * a + j],
                                            recv_sem=rsem.at[SEMS_PER_ARRAY * a + j], device_id=to, device_id_type=MESH)

    def local(a):
        if reduce:
            return pltpu.make_async_copy(srcs[a].at[k], half(a, k, c), lsem.at[a])
        return pltpu.make_async_copy(srcs[a], outs[a].at[k], lsem.at[a])

    def ici(a, j, arriving):
        px, py = chips[j]
        kk = 2 * px + py
        if arriving:
            return copy(a, j, mine(a, k), half(a, kk, c), (px, py, c))
        return copy(a, j, mine(a, kk), half(a, k, c), (px, py, c))

    def passed(a, j, arriving):
        px, py = chips[j]
        kk = 2 * px + py
        return copy(a, 3 + j, half(a, kk, c), half(a, kk, (1 - c) if arriving else c), sib)

    def own(a, arriving):
        return copy(a, 6, mine(a, k), half(a, k, (1 - c) if arriving else c), sib)

    def start():
        for a in range(n):
            local(a).start()
        for a, j in pairs:
            ici(a, j, False).start()
        if reduce:
            for a in range(n):
                own(a, False).start()

    def forward():
        for a, j in pairs:
            ici(a, j, True).wait_recv()
            passed(a, j, False).start()

    def finish():
        for a in range(n):
            if reduce:
                own(a, True).wait_recv()
            for j in range(3):
                passed(a, j, True).wait_recv()
        for a, j in pairs:
            ici(a, j, False).wait_send()
            passed(a, j, False).wait_send()
        for a in range(n):
            if reduce:
                own(a, False).wait_send()
            local(a).wait()

    return start, forward, finish


def _exchange_shapes(arrays, kind):
    if kind == "siblings":
        return [jax.ShapeDtypeStruct((N_CHIPS, g.shape[1] // 2, g.shape[2]), g.dtype) for g in arrays]
    if kind == "reduce":
        return [jax.ShapeDtypeStruct((N_CHIPS, 2 * p.shape[1], p.shape[2]), p.dtype) for p in arrays]
    return [jax.ShapeDtypeStruct((N_CHIPS,) + s.shape, s.dtype) for s in arrays]


def exchange(arrays, kind, name):
    n = len(arrays)

    def body(*refs):
        start, forward, finish = _exchange_phases(refs[:n], refs[n:2 * n], *refs[2 * n:], kind)
        start()
        forward()
        finish()

    return pl.pallas_call(
        body, name=name, in_specs=_hbm_specs(n), out_specs=_hbm_specs(n),
        out_shape=_exchange_shapes(arrays, kind), scratch_shapes=_exchange_sems(n),
    )(*arrays)


def allreduce_small(p, name):
    rows, width = p.shape

    def body(p_ref, s_ref, gath, ssem, rsem):
        x, y, c = _place()
        me = 4 * x + 2 * y + c
        copies = []
        for d in range(1, N_DEV):
            dx, dy, dc = (d >> 2) & 1, (d >> 1) & 1, d & 1
            peer = (x ^ dx, y ^ dy, c ^ dc)
            cp = pltpu.make_async_remote_copy(src_ref=p_ref, dst_ref=gath.at[me], send_sem=ssem.at[d - 1],
                                              recv_sem=rsem.at[d - 1], device_id=peer, device_id_type=MESH)
            cp.start()
            copies.append(cp)
        gath[me] = p_ref[...]
        for d, cp in enumerate(copies, start=1):
            dx, dy, dc = (d >> 2) & 1, (d >> 1) & 1, d & 1
            peer_slot = 4 * (x ^ dx) + 2 * (y ^ dy) + (c ^ dc)
            pltpu.make_async_remote_copy(src_ref=p_ref, dst_ref=gath.at[peer_slot], send_sem=ssem.at[d - 1],
                                         recv_sem=rsem.at[d - 1], device_id=(x ^ dx, y ^ dy, c ^ dc),
                                         device_id_type=MESH).wait_recv()
        for cp in copies:
            cp.wait_send()
        acc = gath[0]
        for d in range(1, N_DEV):
            acc = acc + gath[d]
        s_ref[...] = acc

    return pl.pallas_call(
        body, name=name,
        in_specs=[pl.BlockSpec(memory_space=pltpu.VMEM)],
        out_specs=pl.BlockSpec(memory_space=pltpu.VMEM),
        out_shape=jax.ShapeDtypeStruct((rows, width), F32),
        scratch_shapes=[pltpu.VMEM((N_DEV, rows, width), F32), pltpu.SemaphoreType.DMA((N_DEV - 1,)),
                        pltpu.SemaphoreType.DMA((N_DEV - 1,))],
    )(p)


def add_my_half(g, recv, name):
    _, r, cols = g.shape
    hr = r // 2
    tb = _pick(hr, 512, 8)
    nb = hr // tb
    c_arr = lax.axis_index("c").astype(jnp.int32).reshape(1)

    def body(c_ref, g_ref, r_ref, o_ref):
        o_ref[...] = (g_ref[...] + r_ref[...]).astype(o_ref.dtype)

    return pl.pallas_call(
        body, name=name,
        grid_spec=pltpu.PrefetchScalarGridSpec(
            num_scalar_prefetch=1, grid=(N_CHIPS, nb),
            in_specs=[pl.BlockSpec((1, tb, cols), lambda k, i, c_ref: (k, c_ref[0] * nb + i, 0)),
                      pl.BlockSpec((1, tb, cols), lambda k, i, c_ref: (k, i, 0))],
            out_specs=pl.BlockSpec((1, tb, cols), lambda k, i, c_ref: (k, i, 0))),
        out_shape=jax.ShapeDtypeStruct((N_CHIPS, hr, cols), BF16),
        compiler_params=_params(("arbitrary", "arbitrary")),
    )(c_arr, g, recv)


def sum_chips(parts, name):
    _, hr, cols = parts.shape
    tb = _pick(hr, 512, 8)

    def body(p_ref, o_ref):
        o_ref[...] = ((p_ref[0].astype(F32) + p_ref[1].astype(F32)) + p_ref[2].astype(F32)) + p_ref[3].astype(F32)

    return pl.pallas_call(
        body, name=name, grid=(hr // tb,),
        in_specs=[pl.BlockSpec((N_CHIPS, tb, cols), lambda i: (0, i, 0))],
        out_specs=pl.BlockSpec((tb, cols), lambda i: (i, 0)),
        out_shape=jax.ShapeDtypeStruct((hr, cols), F32),
        compiler_params=_params(("arbitrary",)),
    )(parts)


def adamw(w, g, m, v, name):
    rows, width = w.shape
    tb = _pick(rows, 512, 8)

    def body(w_ref, g_ref, m_ref, v_ref, d_ref, mo_ref, vo_ref):
        gg = g_ref[...]
        m_new = ADAM_B1 * m_ref[...] + (1.0 - ADAM_B1) * gg
        v_new = ADAM_B2 * v_ref[...] + (1.0 - ADAM_B2) * (gg * gg)
        m_hat = m_new / (1.0 - ADAM_B1 ** ADAM_STEP)
        v_hat = v_new / (1.0 - ADAM_B2 ** ADAM_STEP)
        d_ref[...] = -ADAM_LR * (m_hat / (jnp.sqrt(v_hat) + ADAM_EPS) + ADAM_WD * w_ref[...])
        mo_ref[...] = m_new
        vo_ref[...] = v_new

    spec = pl.BlockSpec((tb, width), lambda i: (i, 0))
    return pl.pallas_call(
        body, name=name, grid=(rows // tb,), in_specs=[spec] * 4, out_specs=[spec] * 3,
        out_shape=[jax.ShapeDtypeStruct((rows, width), F32)] * 3,
        compiler_params=_params(("arbitrary",)),
    )(w, g, m, v)


def _pack_rows(vecs, width=1024):
    rows, cur, used = [], [], 0
    for v in vecs:
        n = v.shape[1]
        if used + n > width:
            cur.append(jnp.zeros((1, width - used), F32))
            rows.append(jnp.concatenate(cur, axis=1))
            cur, used = [], 0
        cur.append(v)
        used += n
    cur.append(jnp.zeros((1, width - used), F32))
    rows.append(jnp.concatenate(cur, axis=1))
    return rows


def _pad128(v):
    return jnp.pad(v, ((0, 0), (0, LANES - v.shape[1])))


def kernel(x, norm1_w, w_in, lb_logits, hg_norm_w, q_norm_w, k_norm_w, att_norm_w, w_out, norm2_w, w_gate_up, w_down, final_norm_w, loss_target, m_norm1_w, m_w_in, m_lb_logits, m_hg_norm_w, m_q_norm_w, m_k_norm_w, m_att_norm_w, m_w_out, m_norm2_w, m_w_gate_up, m_w_down, m_final_norm_w, v_norm1_w, v_w_in, v_lb_logits, v_hg_norm_w, v_q_norm_w, v_k_norm_w, v_att_norm_w, v_w_out, v_norm2_w, v_w_gate_up, v_w_down, v_final_norm_w):
    t, d = x.shape[1], x.shape[2]
    xi, yi, ci = _place()
    chip = 2 * xi + yi
    x2d = x.reshape(t, d)
    tgt = loss_target.reshape(t, d)
    tok = min(TOK, t)
    nt = t // tok
    ns = ATT_GROUP * nt

    late_w = [w_out[0].astype(BF16), w_gate_up[0].astype(BF16), w_down[0].astype(BF16)]
    lb_rows = lb_logits.reshape(4, LANES) * (ci == 0).astype(F32)
    lb_pad = lax.dynamic_update_slice(jnp.zeros((8, 1024), F32), lb_rows, (0, chip * LANES))
    lb_full = allreduce_small(lb_pad, "gather_lb")[:4, :HG_W]
    l_f0, l_f1, l_b0, l_b1 = (lb_full[i:i + 1] for i in range(4))

    n1 = norm1_w.reshape(1, d)
    n2 = norm2_w.reshape(1, d)
    nf = final_norm_w.reshape(1, d)
    tm = min(512, t)
    h1, g_in = _rows(lambda a, w: ((_rms(a, w),), ()), "norm1", t, tm, [_rin(x2d, tm)], [n1], [_rout(t, tm, d, BF16)],
                     ride=("gather", [w_in[0].astype(BF16)]))
    wf_in = g_in.transpose(1, 0, 2).reshape(g_in.shape[1], -1)
    u = mm_rows(h1, wf_in, "mm_in")
    o_f, ss_f = gla_fwd(u, l_f0, l_f1, 1, False, "gla_fwd_f")
    o_b, ss_b = gla_fwd(u, l_b0, l_b1, 2, True, "gla_fwd_b")

    cos2, sin2 = _rope_tables(t)
    ck, sk = jnp.tile(cos2, (1, ATT_KV)), jnp.tile(sin2, (1, ATT_KV))
    qw, kw = _pad128(q_norm_w.reshape(1, ATT_DH)), _pad128(k_norm_w.reshape(1, ATT_DH))
    gq, gk = _group_sum_matrix(ATT_QW), _group_sum_matrix(ATT_KVW)
    tq, tk = _tile_matrix(ATT_QW), _tile_matrix(ATT_KVW)
    prep_in = [_rin(u, tok, ATT_QW, 5), _rin(u, tok, ATT_KVW, 24), _rin(ck, tok), _rin(sk, tok)]
    prep_consts = [qw, kw, gq, gk, tq, tk]

    def att_prep_fn(aq, ak, av, *rest):
        q, k = _att_prep_fn(aq, ak, *rest)
        return (_head_t(q, ATT_HEADS), _head_s(q, ATT_HEADS), _head_s(k, ATT_KV), _head_t(k, ATT_KV),
                _head_t(av, ATT_KV)), ()

    q_t, q_s, k_s, k_t, v_t = _rows(
        att_prep_fn, "att_prep", t, tok, prep_in[:2] + [_rin(u, tok, ATT_KVW, 25)] + prep_in[2:], prep_consts,
        [_tout(ATT_HEADS, nt, ATT_DH, tok, BF16), _tout(ATT_HEADS, nt, tok, LANES, BF16),
         _tout(ATT_KV, nt, tok, LANES, BF16), _tout(ATT_KV, nt, ATT_DH, tok, BF16),
         _tout(ATT_KV, nt, ATT_DH, tok, BF16)])
    q_t = q_t.reshape(ATT_KV, ns, ATT_DH, tok)
    q_s = q_s.reshape(ATT_KV, ns, tok, LANES)
    k_s = k_s.reshape(ATT_KV, t, LANES)
    o_t, lse, g_out, g_gu, g_down = fa_fwd(q_t, k_s, v_t, ride=("gather", late_w))
    wf_gu = g_gu.transpose(1, 0, 2).reshape(g_gu.shape[1], -1)
    wf_out = g_out.reshape(-1, g_out.shape[2])
    wf_down = g_down.reshape(-1, g_down.shape[2])
    o_tiles = o_t.reshape(ATT_HEADS, nt, ATT_DH, tok)

    hgw = hg_norm_w.reshape(1, HG_D)
    attw = att_norm_w.reshape(1, ATT_QW)
    mix_in = [_rin(o_f, tok), _rin(o_b, tok), _rin(u, tok, HG_W, 4), _tin(o_tiles)]
    (mix,) = _rows(lambda of, ob, ug, ot, hw, aw: ((_post_mix_fn(of, ob, ug, _from_head_t(ot), hw, aw),), ()),
                   "post_mix", t, tok, mix_in, [hgw, attw], [_rout(t, tok, d, BF16)])
    x1, h2 = mm_res_norm(mix, wf_out, x2d, n2, "mm_out")
    gu2, act = ffn_up(h2, wf_gu)

    dx2, dx2b, g_final, loss_part = mm_res_loss(act, wf_down, x1, tgt, nf, "mm_down_loss")
    dgu2 = ffn_down_dx(dx2b, wf_down, gu2)
    gw_down = mm_tn(dx2b, act, "mm_down_dw", transpose_out=True)
    dx1, dx1b, g_norm2 = mm_nt_norm_bwd(dgu2, wf_gu, x1, n2, dx2, "mm_gate_up_dx", True)
    gw_gu = mm_tn(h2, dgu2, "mm_gate_up_dw", col_shards=N_CHIPS)
    dmix = mm_rows(dx1b, wf_out, "mm_out_dx", trans_b=True)
    gw_out = mm_tn(mix, dx1b, "mm_out_dw")

    def post_mix_bwd_fn(of, ob, ug, ot, dm, hgw_, attw_):
        _, vjp = jax.vjp(_post_mix_fn, of, ob, ug, _from_head_t(ot), hgw_, attw_)
        dof, _, dug, doa, dhgw, dattw = vjp(dm)
        return (dof, dug, _head_t(doa, ATT_HEADS), _head_s(doa, ATT_HEADS)), (dhgw, dattw)

    late = ["w_out", "w_gate_up", "w_down"]
    late_g = [gw_out.reshape(N_CHIPS, -1, d), gw_gu, gw_down.reshape(N_CHIPS, -1, d)]
    do_hg, du_g, do_t, do_s, g_hg, g_att, *late_recv = _rows(
        post_mix_bwd_fn, "post_mix_bwd", t, tok, mix_in + [_rin(dmix, tok)], [hgw, attw],
        [_rout(t, tok, HG_W, F32), _rout(t, tok, HG_W, BF16), _tout(ATT_HEADS, nt, ATT_DH, tok, BF16),
         _tout(ATT_HEADS, nt, tok, LANES, BF16)], [HG_D, ATT_QW], ride=("siblings", late_g))
    late_part = [add_my_half(g, r, "add_my_half_" + n) for g, r, n in zip(late_g, late_recv, late)]
    dq_t, dk_t, dv_t, *late_parts = fa_bwd(q_s, q_t, do_s.reshape(q_s.shape), do_t.reshape(q_t.shape), o_t, lse, k_s, k_t,
                                           v_t, ride=("reduce", late_part))

    dq_f, df_f, di_f, dl_f0, dl_f1 = gla_bwd(u, l_f0, l_f1, ss_f, do_hg, 1, False, "gla_bwd_f")
    dq_b, df_b, di_b, dl_b0, dl_b1 = gla_bwd(u, l_b0, l_b1, ss_b, do_hg, 2, True, "gla_bwd_b")

    def att_prep_bwd_fn(aq, ak, ck_, sk_, dqt, dkt, dvt, qf, qb, ff, fb, i_f, i_b, dg, qw_, kw_, gq_, gk_, tq_, tk_):
        _, vjp = jax.vjp(lambda a, b, c_, e: _att_prep_fn(a, b, ck_, sk_, c_, e, gq_, gk_, tq_, tk_),
                         aq, ak, qw_, kw_)
        daq, dak, dqw, dkw = vjp((_from_head_t(dqt), _from_head_t(dkt)))
        du_tile = jnp.concatenate([qf + qb, ff, fb, i_f + i_b, dg.astype(F32), daq, dak, _from_head_t(dvt)], axis=1)
        return (du_tile,), (dqw, dkw)

    du, g_q, g_k = _rows(
        att_prep_bwd_fn, "att_prep_bwd", t, tok,
        prep_in + [_tin(dq_t.reshape(ATT_HEADS, nt, ATT_DH, tok)), _tin(dk_t), _tin(dv_t)]
        + [_rin(a, tok) for a in (dq_f, dq_b, df_f, df_b, di_f, di_b, du_g)], prep_consts,
        [_rout(t, tok, u.shape[1], BF16)], [LANES, LANES])
    gw_in_t = mm_tn(h1, du, "mm_in_dw", transpose_out=True)

    names = ["w_in"] + late
    g_in4 = gw_in_t.reshape(N_CHIPS, -1, d)
    in_part = add_my_half(g_in4, exchange([g_in4], "siblings", "rs_siblings_w_in")[0], "add_my_half_w_in")
    grad_x, g_norm1, in_parts = mm_nt_norm_bwd(du, wf_in, x2d, n1, dx1, "mm_in_dx", False, ride=("reduce", [in_part]))
    parts = [in_parts] + late_parts
    g_shard = [sum_chips(p, "sum_chips_" + n) for p, n in zip(parts, names)]
    g_shard[0] = g_shard[0].T
    big = {}
    for n, g, w, m, v in zip(names, g_shard, (w_in, w_out, w_gate_up, w_down), (m_w_in, m_w_out, m_w_gate_up, m_w_down),
                             (v_w_in, v_w_out, v_w_gate_up, v_w_down)):
        dlt, mn, vn = adamw(w[0], g, m[0], v[0], "adamw_" + n)
        big[n] = (g[None], dlt[None], mn[None], vn[None])

    small = [g_norm1, g_norm2, g_final, g_att, g_hg, g_q, g_k, loss_part, dl_f0, dl_f1, dl_b0, dl_b1]
    packed = _pack_rows(small)
    packed += [jnp.zeros((1, 1024), F32)] * (8 - len(packed))
    tot = allreduce_small(jnp.concatenate(packed, axis=0), "allreduce_small")
    s_norm1, s_norm2, s_final = tot[0:1], tot[1:2], tot[2:3]
    s_att, s_hg, s_q, s_k = tot[3:4, 0:512], tot[3:4, 512:640], tot[3:4, 640:704], tot[3:4, 768:832]
    loss = tot[3, 896]
    s_lb = jnp.concatenate([tot[4:5, 0:512], tot[4:5, 512:1024], tot[5:6, 0:512], tot[5:6, 512:1024]], axis=0)
    s_lb = lax.dynamic_slice(s_lb, (0, chip * LANES), (4, LANES)).reshape(1, 512)

    snames = ["norm1_w", "lb_logits", "hg_norm_w", "q_norm_w", "k_norm_w", "att_norm_w", "norm2_w", "final_norm_w"]
    g_small = dict(zip(snames, [s_norm1, s_lb, s_hg, s_q, s_k, s_att, s_norm2, s_final]))
    w_small = dict(zip(snames, [norm1_w, lb_logits, hg_norm_w, q_norm_w, k_norm_w, att_norm_w, norm2_w, final_norm_w]))
    m_small = dict(zip(snames, [m_norm1_w, m_lb_logits, m_hg_norm_w, m_q_norm_w, m_k_norm_w, m_att_norm_w, m_norm2_w, m_final_norm_w]))
    v_small = dict(zip(snames, [v_norm1_w, v_lb_logits, v_hg_norm_w, v_q_norm_w, v_k_norm_w, v_att_norm_w, v_norm2_w, v_final_norm_w]))

    def pack_small(tree):
        rows = _pack_rows([tree[n].reshape(1, -1) for n in snames])
        rows += [jnp.zeros((1, 1024), F32)] * (8 - len(rows))
        return jnp.concatenate(rows, axis=0)

    d_s, m_s, v_s = adamw(pack_small(w_small), pack_small(g_small), pack_small(m_small), pack_small(v_small), "adamw_small")

    def unpack_small(a):
        out, r, used = {}, 0, 0
        for n in snames:
            size = w_small[n].size
            if used + size > 1024:
                r, used = r + 1, 0
            out[n] = a[r, used:used + size].reshape(w_small[n].shape)
            used += size
        return out

    d_sm, m_sm, v_sm = unpack_small(d_s), unpack_small(m_s), unpack_small(v_s)
    g_sm = {n: g_small[n].reshape(w_small[n].shape) for n in snames}

    order = ["norm1_w", "w_in", "lb_logits", "hg_norm_w", "q_norm_w", "k_norm_w", "att_norm_w", "w_out", "norm2_w",
             "w_gate_up", "w_down", "final_norm_w"]

    def pick(small_tree, idx):
        return [big[n][idx] if n in big else small_tree[n] for n in order]

    return (loss, grad_x.reshape(x.shape), *pick(g_sm, 0), *pick(d_sm, 1), *pick(m_sm, 2), *pick(v_sm, 3))
```

```python
import functools

import numpy as np
import jax
import jax.numpy as jnp
from jax import lax
from jax.experimental import pallas as pl
from jax.experimental.pallas import tpu as pltpu

F32 = jnp.float32
BF16 = jnp.bfloat16
MESH = pl.DeviceIdType.MESH

EPS = 1e-6
GRID_W = 64
HG_HEADS = 4
HG_D = 128
HG_W = HG_HEADS * HG_D
CHUNK = 64
ATT_HEADS = 8
ATT_KV = 2
ATT_GROUP = ATT_HEADS // ATT_KV
ATT_DH = 64
ATT_QW = ATT_HEADS * ATT_DH
ATT_KVW = ATT_KV * ATT_DH
ROPE_THETA = 10000.0
N_CHIPS = 4
N_DEV = 8

ADAM_LR = 0.001
ADAM_B1 = 0.9
ADAM_B2 = 0.999
ADAM_EPS = 1e-08
ADAM_WD = 0.01
ADAM_STEP = 10

VMEM_LIMIT = 52 * 1024 * 1024
LANES = 128
TOK = 256


def _params(sem=None):
    return pltpu.CompilerParams(dimension_semantics=sem, vmem_limit_bytes=VMEM_LIMIT)


def _dg(a, b, ca, cb):
    return lax.dot_general(a.astype(BF16), b.astype(BF16), (((ca,), (cb,)), ((), ())),
                           preferred_element_type=F32)


@jax.custom_vjp
def dot_nn(a, b):
    return _dg(a, b, 1, 0)


def _dot_nn_fwd(a, b):
    return _dg(a, b, 1, 0), (a, b)


def _dot_nn_bwd(res, g):
    a, b = res
    return _dg(g, b, 1, 1), _dg(a, g, 0, 0)


dot_nn.defvjp(_dot_nn_fwd, _dot_nn_bwd)


@jax.custom_vjp
def dot_nt(a, b):
    return _dg(a, b, 1, 1)


def _dot_nt_fwd(a, b):
    return _dg(a, b, 1, 1), (a, b)


def _dot_nt_bwd(res, g):
    a, b = res
    return _dg(g, b, 1, 0), _dg(g, a, 0, 0)


dot_nt.defvjp(_dot_nt_fwd, _dot_nt_bwd)


@jax.custom_vjp
def dot_tn(a, b):
    return _dg(a, b, 0, 0)


def _dot_tn_fwd(a, b):
    return _dg(a, b, 0, 0), (a, b)


def _dot_tn_bwd(res, g):
    a, b = res
    return _dg(b, g, 1, 1), _dg(a, g, 1, 0)


dot_tn.defvjp(_dot_tn_fwd, _dot_tn_bwd)


def _split3(a):
    hi = a.astype(BF16)
    r1 = a - hi.astype(F32)
    mid = r1.astype(BF16)
    lo = (r1 - mid.astype(F32)).astype(BF16)
    return lo, mid, hi


def _sum3(terms):
    lo, mid, hi = terms
    return (lo + mid) + hi


@jax.custom_vjp
def xdot_r(a, m):
    return _sum3([_dg(p, m, 1, 0) for p in _split3(a)])


def _xdot_r_fwd(a, m):
    return xdot_r(a, m), m


def _xdot_r_bwd(m, g):
    return _sum3([_dg(p, m, 1, 1) for p in _split3(g)]), jnp.zeros_like(m)


xdot_r.defvjp(_xdot_r_fwd, _xdot_r_bwd)


@jax.custom_vjp
def xdot_l(m, a):
    return _sum3([_dg(m, p, 1, 0) for p in _split3(a)])


def _xdot_l_fwd(m, a):
    return xdot_l(m, a), m


def _xdot_l_bwd(m, g):
    return jnp.zeros_like(m), _sum3([_dg(m, p, 0, 0) for p in _split3(g)])


xdot_l.defvjp(_xdot_l_fwd, _xdot_l_bwd)


@jax.custom_vjp
def swap_pairs(y):
    n = y.shape[-1]
    lane = lax.broadcasted_iota(jnp.int32, y.shape, 1)
    nxt = pltpu.roll(y, n - 1, 1)
    prv = pltpu.roll(y, 1, 1)
    return jnp.where(lane % 2 == 0, nxt, prv)


def _swap_fwd(y):
    return swap_pairs(y), None


def _swap_bwd(_, g):
    return (swap_pairs(g),)


swap_pairs.defvjp(_swap_fwd, _swap_bwd)


def _rms(x, w):
    return x * lax.rsqrt(jnp.mean(x * x, axis=-1, keepdims=True) + EPS) * w


def _sigmoid(x):
    return jax.nn.sigmoid(x)


def _rows(fn, name, t, tm, ins, consts, outs, accs=(), ride=None):
    n_r, n_c, n_o, n_a = len(ins), len(consts), len(outs), len(accs)

    def body(*refs):
        r = refs[:n_r]
        c = refs[n_r:n_r + n_c]
        o = refs[n_r + n_c:n_r + n_c + n_o]
        a = refs[n_r + n_c + n_o:]
        ro, ao = fn(*[x[...] for x in r], *[x[...] for x in c])
        for ref, val in zip(o, ro):
            if isinstance(val, (list, tuple)):
                for h, piece in enumerate(val):
                    ref[h, 0] = piece.astype(ref.dtype)
            else:
                ref[...] = val.astype(ref.dtype)
        if n_a:
            @pl.when(pl.program_id(0) == 0)
            def _():
                for ref in a:
                    ref[...] = jnp.zeros(ref.shape, F32)
            for ref, val in zip(a, ao):
                ref[...] += val

    in_specs = [s for _, s in ins]
    in_specs += [pl.BlockSpec(c.shape, lambda i, nd=c.ndim: (0,) * nd) for c in consts]
    out_specs = [s for _, s in outs] + [pl.BlockSpec((1, w), lambda i: (0, 0)) for w in accs]
    out_shape = [s for s, _ in outs] + [jax.ShapeDtypeStruct((1, w), F32) for w in accs]
    grid = (t // tm,)
    body, r_in, r_out, r_shapes, r_sems, r_args = _riding(body, n_r + n_c, n_o + n_a, grid, ride)
    return pl.pallas_call(
        body, name=name, grid=grid, in_specs=in_specs + r_in, out_specs=out_specs + r_out,
        out_shape=out_shape + r_shapes, scratch_shapes=r_sems, compiler_params=_params(("arbitrary",)),
    )(*[a for a, _ in ins], *consts, *r_args)


def _rin(a, tm, width=None, cb=0):
    width = a.shape[1] if width is None else width
    return a, pl.BlockSpec((tm, width), lambda i, cb=cb: (i, cb))


def _rout(t, tm, width, dtype):
    return jax.ShapeDtypeStruct((t, width), dtype), pl.BlockSpec((tm, width), lambda i: (i, 0))


def _tin(a):
    return a, pl.BlockSpec((a.shape[0], 1) + a.shape[2:], lambda i: (0, i, 0, 0))


def _tout(heads, nt, r, c, dtype):
    return jax.ShapeDtypeStruct((heads, nt, r, c), dtype), pl.BlockSpec((heads, 1, r, c), lambda i: (0, i, 0, 0))


def _pick(n, cap, mult):
    best = None
    for d in range(mult, min(n, cap) + 1, mult):
        if n % d == 0:
            best = d
    return best if best is not None else n


def mm_rows(a, b, name, trans_b=False, res=None, out_dtype=F32):
    m, k = a.shape
    n = b.shape[0] if trans_b else b.shape[1]
    tn = _pick(n, 3328, LANES)
    tm = _pick(m, 512 if k <= 3072 else 256, 8)
    has_res = res is not None

    def body(*refs):
        if has_res:
            a_ref, b_ref, r_ref, o_ref = refs
        else:
            a_ref, b_ref, o_ref = refs
        acc = _dg(a_ref[...], b_ref[...], 1, 1 if trans_b else 0)
        if has_res:
            acc = acc + r_ref[...]
        o_ref[...] = acc.astype(o_ref.dtype)

    in_specs = [pl.BlockSpec((tm, k), lambda j, i: (i, 0))]
    if trans_b:
        in_specs.append(pl.BlockSpec((tn, k), lambda j, i: (j, 0)))
    else:
        in_specs.append(pl.BlockSpec((k, tn), lambda j, i: (0, j)))
    args = [a, b]
    if has_res:
        in_specs.append(pl.BlockSpec((tm, tn), lambda j, i: (i, j)))
        args.append(res)
    return pl.pallas_call(
        body, name=name, grid=(n // tn, m // tm), in_specs=in_specs,
        out_specs=pl.BlockSpec((tm, tn), lambda j, i: (i, j)),
        out_shape=jax.ShapeDtypeStruct((m, n), out_dtype),
        compiler_params=_params(("arbitrary", "arbitrary")),
    )(*args)


def mm_tn(a, b, name, col_shards=None, transpose_out=False):
    t, m = a.shape
    parts = b.shape[0] if b.ndim == 3 else 1
    n = parts * b.shape[-1]
    tm = _pick(m, 512, LANES)
    width = n // col_shards if col_shards else None
    group = max(1, 3328 // width) if col_shards else 1
    tn = width * group if col_shards else _pick(n, 3328, LANES)
    tk = _pick(t, 1024, 8)
    nkk = t // tk
    per_part = b.shape[-1] // tn

    def body(a_ref, b_ref, o_ref, *scratch):
        acc_ref = scratch[0] if transpose_out else o_ref

        @pl.when(pl.program_id(2) == 0)
        def _():
            acc_ref[...] = jnp.zeros(acc_ref.shape, F32)
        acc = _dg(a_ref[...], b_ref[0] if b.ndim == 3 else b_ref[...], 0, 0)
        if col_shards:
            for s in range(group):
                acc_ref[s] += acc[:, s * width:(s + 1) * width]
        else:
            acc_ref[...] += acc
        if transpose_out:
            @pl.when(pl.program_id(2) == nkk - 1)
            def _():
                o_ref[...] = acc_ref[...].T

    if col_shards:
        out_spec = pl.BlockSpec((group, tm, width), lambda i, j, kk: (j, i, 0))
        out_shape = jax.ShapeDtypeStruct((col_shards, m, width), F32)
    elif transpose_out:
        out_spec = pl.BlockSpec((tn, tm), lambda i, j, kk: (j, i))
        out_shape = jax.ShapeDtypeStruct((n, m), F32)
    else:
        out_spec = pl.BlockSpec((tm, tn), lambda i, j, kk: (i, j))
        out_shape = jax.ShapeDtypeStruct((m, n), F32)
    if b.ndim == 3:
        b_spec = pl.BlockSpec((1, tk, tn), lambda i, j, kk: (j // per_part, kk, j % per_part))
    else:
        b_spec = pl.BlockSpec((tk, tn), lambda i, j, kk: (kk, j))
    return pl.pallas_call(
        body, name=name, grid=(m // tm, n // tn, nkk),
        in_specs=[pl.BlockSpec((tk, tm), lambda i, j, kk: (kk, i)), b_spec],
        out_specs=out_spec, out_shape=out_shape,
        scratch_shapes=[pltpu.VMEM((tm, tn), F32)] if transpose_out else [],
        compiler_params=_params(("arbitrary", "arbitrary", "arbitrary")),
    )(a, b)


def ffn_up(h2, w_gu):
    t, k = h2.shape
    dff = w_gu.shape[1] // 2
    tn = _pick(dff, 1408, LANES)
    ncol = dff // tn
    tm = _pick(t, 512, 8)

    def body(a_ref, wg_ref, wu_ref, gu_ref, act_ref):
        a = a_ref[...]
        g = _dg(a, wg_ref[...], 1, 0)
        u = _dg(a, wu_ref[...], 1, 0)
        gu_ref[0] = g.astype(gu_ref.dtype)
        gu_ref[1] = u.astype(gu_ref.dtype)
        act_ref[...] = _swiglu_fn(g, u).astype(act_ref.dtype)

    return pl.pallas_call(
        body, name="ffn_up", grid=(ncol, t // tm),
        in_specs=[pl.BlockSpec((tm, k), lambda s, i: (i, 0)), pl.BlockSpec((k, tn), lambda s, i: (0, s)),
                  pl.BlockSpec((k, tn), lambda s, i: (0, s + ncol))],
        out_specs=[pl.BlockSpec((2, tm, tn), lambda s, i: (0, i, s)), pl.BlockSpec((tm, tn), lambda s, i: (i, s))],
        out_shape=[jax.ShapeDtypeStruct((2, t, dff), BF16), jax.ShapeDtypeStruct((t, dff), BF16)],
        compiler_params=_params(("arbitrary", "arbitrary")),
    )(h2, w_gu, w_gu)


def ffn_down_dx(dx2b, w_down, gu2):
    t, k = dx2b.shape
    dff = w_down.shape[0]
    tn = _pick(dff, 1408, LANES)
    tm = _pick(t, 512, 8)

    def body(a_ref, w_ref, gu_ref, o_ref):
        pieces = [slice(0, tm // 2), slice(tm // 2, tm)]
        dacts = [_dg(a_ref[rows, :], w_ref[...], 1, 1) for rows in pieces]
        for rows, dact in zip(pieces, dacts):
            _, vjp = jax.vjp(_swiglu_fn, gu_ref[0, rows, :].astype(F32), gu_ref[1, rows, :].astype(F32))
            dg, du = vjp(dact)
            o_ref[0, rows, :] = dg.astype(o_ref.dtype)
            o_ref[1, rows, :] = du.astype(o_ref.dtype)

    blk = pl.BlockSpec((2, tm, tn), lambda s, i: (0, i, s))
    return pl.pallas_call(
        body, name="ffn_down_dx", grid=(dff // tn, t // tm),
        in_specs=[pl.BlockSpec((tm, k), lambda s, i: (i, 0)), pl.BlockSpec((tn, k), lambda s, i: (s, 0)), blk],
        out_specs=blk, out_shape=jax.ShapeDtypeStruct((2, t, dff), BF16),
        compiler_params=_params(("arbitrary", "arbitrary")),
    )(dx2b, w_down, gu2)


def mm_res_norm(a, b, res, w, name):
    t, k = a.shape
    d = b.shape[1]
    tm = _pick(t, 512, 8)

    def body(a_ref, b_ref, r_ref, w_ref, x_ref, h_ref):
        xv = _dg(a_ref[...], b_ref[...], 1, 0) + r_ref[...]
        x_ref[...] = xv
        h_ref[...] = _rms(xv, w_ref[...]).astype(h_ref.dtype)

    row = pl.BlockSpec((tm, d), lambda i: (i, 0))
    return pl.pallas_call(
        body, name=name, grid=(t // tm,),
        in_specs=[pl.BlockSpec((tm, k), lambda i: (i, 0)), pl.BlockSpec((k, d), lambda i: (0, 0)), row,
                  pl.BlockSpec((1, d), lambda i: (0, 0))],
        out_specs=[row, row],
        out_shape=[jax.ShapeDtypeStruct((t, d), F32), jax.ShapeDtypeStruct((t, d), BF16)],
        compiler_params=_params(("arbitrary",)),
    )(a, b, res, w)


def mm_res_loss(a, b, res, tgt, w, name):
    t, k = a.shape
    d = b.shape[1]
    tm = _pick(t, 512, 16)

    def body(a_ref, b_ref, r_ref, t_ref, w_ref, dx_ref, dxb_ref, dw_ref, loss_ref):
        @pl.when(pl.program_id(0) == 0)
        def _():
            dw_ref[...] = jnp.zeros(dw_ref.shape, F32)
            loss_ref[...] = jnp.zeros(loss_ref.shape, F32)
        pieces = [slice(0, tm // 2), slice(tm // 2, tm)]
        accs = [_dg(a_ref[rows, :], b_ref[...], 1, 0) for rows in pieces]
        for rows, acc in zip(pieces, accs):
            tg = t_ref[rows, :]
            val, vjp = jax.vjp(lambda x_, w_: _loss_fn(x_, w_, tg), acc + r_ref[rows, :], w_ref[...])
            dx, dw = vjp(jnp.ones((1, 1), F32))
            dx_ref[rows, :] = dx
            dxb_ref[rows, :] = dx.astype(dxb_ref.dtype)
            dw_ref[...] += dw
            loss_ref[...] += jnp.broadcast_to(val, (1, LANES))

    row = pl.BlockSpec((tm, d), lambda i: (i, 0))
    vec = pl.BlockSpec((1, d), lambda i: (0, 0))
    return pl.pallas_call(
        body, name=name, grid=(t // tm,),
        in_specs=[pl.BlockSpec((tm, k), lambda i: (i, 0)), pl.BlockSpec((k, d), lambda i: (0, 0)), row, row, vec],
        out_specs=[row, row, vec, pl.BlockSpec((1, LANES), lambda i: (0, 0))],
        out_shape=[jax.ShapeDtypeStruct((t, d), F32), jax.ShapeDtypeStruct((t, d), BF16),
                   jax.ShapeDtypeStruct((1, d), F32), jax.ShapeDtypeStruct((1, LANES), F32)],
        compiler_params=_params(("arbitrary",)),
    )(a, b, res, tgt, w)


def mm_nt_norm_bwd(a, wmat, x, w, dres, name, with_bf16, ride=None):
    parts = a.shape[0] if a.ndim == 3 else 1
    t, kp = a.shape[-2], a.shape[-1]
    d = wmat.shape[0]
    tm = _pick(t, 256, 8)

    n_out = 3 if with_bf16 else 2

    def body(*refs):
        a_ref, w_refs = refs[0], refs[1:1 + parts]
        x_ref, nw_ref, r_ref, dx_ref = refs[1 + parts:5 + parts]
        dw_ref = refs[4 + parts + n_out - 1]

        @pl.when(pl.program_id(0) == 0)
        def _():
            dw_ref[...] = jnp.zeros(dw_ref.shape, F32)
        dh = None
        for p in range(parts):
            term = _dg(a_ref[p] if a.ndim == 3 else a_ref[...], w_refs[p][...], 1, 1)
            dh = term if dh is None else dh + term
        _, vjp = jax.vjp(_rms, x_ref[...], nw_ref[...])
        dx, dw = vjp(dh)
        dx = dx + r_ref[...]
        dx_ref[...] = dx
        if with_bf16:
            refs[5 + parts][...] = dx.astype(BF16)
        dw_ref[...] += dw

    row = pl.BlockSpec((tm, d), lambda i: (i, 0))
    vec = pl.BlockSpec((1, d), lambda i: (0, 0))
    a_spec = pl.BlockSpec((parts, tm, kp), lambda i: (0, i, 0)) if a.ndim == 3 else pl.BlockSpec((tm, kp), lambda i: (i, 0))
    w_specs = [pl.BlockSpec((d, kp), lambda i, p=p: (0, p)) for p in range(parts)]
    outs = [row] + ([row] if with_bf16 else []) + [vec]
    shapes = [jax.ShapeDtypeStruct((t, d), F32)] + ([jax.ShapeDtypeStruct((t, d), BF16)] if with_bf16 else [])
    grid = (t // tm,)
    body, r_in, r_out, r_shapes, r_sems, r_args = _riding(body, 4 + parts, n_out, grid, ride)
    return pl.pallas_call(
        body, name=name, grid=grid,
        in_specs=[a_spec] + w_specs + [row, vec, row] + r_in,
        out_specs=outs + r_out, out_shape=shapes + [jax.ShapeDtypeStruct((1, d), F32)] + r_shapes,
        scratch_shapes=r_sems, compiler_params=_params(("arbitrary",)),
    )(a, *([wmat] * parts), x, w, dres, *r_args)


def _lower_bound(l0, l1):
    m = jnp.maximum(l0, l1)
    e0 = jnp.exp(l0 - m)
    e1 = jnp.exp(l1 - m)
    return e0 / (e0 + e1)


def _gla_consts(rev):
    ri = lax.broadcasted_iota(jnp.int32, (CHUNK, CHUNK), 0)
    ci = lax.broadcasted_iota(jnp.int32, (CHUNK, CHUNK), 1)
    keep = (ci >= ri) if rev else (ci <= ri)
    ref_mask = lax.broadcasted_iota(jnp.int32, (CHUNK, 1), 0) == (CHUNK // 2 if rev else CHUNK // 2 - 1)
    return keep, ref_mask


def _gla_block(uq, uf, ui, l0, l1, st_in, rev):
    ncb = uq.shape[0] // CHUNK
    heads = range(HG_HEADS)
    keep, ref_mask = _gla_consts(rev)
    cum = keep.astype(F32)
    lb = _lower_bound(l0, l1)
    q = uq * _sigmoid(uq)
    k = (1.0 - lb) * _sigmoid(-uf)
    g = jnp.log(lb + (1.0 - lb) * _sigmoid(uf))

    def rows(a, c):
        return a[c * CHUNK:(c + 1) * CHUNK]

    def head(a, h):
        return a[:, h * HG_D:(h + 1) * HG_D]

    bs = [xdot_l(cum, rows(g, c)) for c in range(ncb)]
    q_in, k_in, q_b, k_d, decay = [], [], [], [], []
    for c in range(ncb):
        b = bs[c]
        b_ref = jnp.sum(jnp.where(ref_mask, b, 0.0), axis=0, keepdims=True)
        b_last = jnp.sum(rows(g, c), axis=0, keepdims=True)
        qc, kc = rows(q, c), rows(k, c)
        q_in.append(qc * jnp.exp(b - b_ref))
        k_in.append(kc * jnp.exp(b_ref - b))
        q_b.append(qc * jnp.exp(b))
        k_d.append(kc * jnp.exp(b_last - b))
        decay.append(jnp.exp(b_last))
    scores = [[jnp.where(keep, dot_nt(head(q_in[c], h), head(k_in[c], h)), 0.0) for h in heads] for c in range(ncb)]
    o_intra = [[dot_nn(scores[c][h], head(rows(ui, c), h)) for h in heads] for c in range(ncb)]
    contrib = [[dot_tn(head(rows(ui, c), h), head(k_d[c], h)) for h in heads] for c in range(ncb)]
    st = list(st_in)
    o_rows = [None] * ncb
    for c in (reversed(range(ncb)) if rev else range(ncb)):
        parts = []
        for h in heads:
            parts.append(o_intra[c][h] + dot_nt(head(q_b[c], h), st[h]))
            st[h] = st[h] * head(decay[c], h) + contrib[c][h]
        o_rows[c] = jnp.concatenate(parts, axis=1)
    return jnp.concatenate(o_rows, axis=0), tuple(st)


def _gla_blocks(t):
    tb = min(512, t)
    return tb, t // tb


def gla_fwd(u, l0, l1, fcol, rev, name):
    t = u.shape[0]
    tb, nb = _gla_blocks(t)

    def blk(i):
        return (nb - 1 - i) if rev else i

    def body(uq_ref, uf_ref, ui_ref, l0_ref, l1_ref, o_ref, ss_ref, st_ref):
        @pl.when(pl.program_id(0) == 0)
        def _():
            st_ref[...] = jnp.zeros(st_ref.shape, F32)
        ss_ref[0] = st_ref[...]
        o, st_out = _gla_block(uq_ref[...], uf_ref[...], ui_ref[...], l0_ref[...], l1_ref[...],
                               tuple(st_ref[h] for h in range(HG_HEADS)), rev)
        o_ref[...] = o
        for h in range(HG_HEADS):
            st_ref[h] = st_out[h]

    row = lambda cb: pl.BlockSpec((tb, HG_W), lambda i: (blk(i), cb))
    vec = pl.BlockSpec((1, HG_W), lambda i: (0, 0))
    return pl.pallas_call(
        body, name=name, grid=(nb,),
        in_specs=[row(0), row(fcol), row(3), vec, vec],
        out_specs=[pl.BlockSpec((tb, HG_W), lambda i: (blk(i), 0)),
                   pl.BlockSpec((1, HG_HEADS, HG_D, HG_D), lambda i: (blk(i), 0, 0, 0))],
        out_shape=[jax.ShapeDtypeStruct((t, HG_W), F32),
                   jax.ShapeDtypeStruct((nb, HG_HEADS, HG_D, HG_D), F32)],
        scratch_shapes=[pltpu.VMEM((HG_HEADS, HG_D, HG_D), F32)],
        compiler_params=_params(("arbitrary",)),
    )(u, u, u, l0, l1)


def gla_bwd(u, l0, l1, ss, do, fcol, rev, name):
    t = u.shape[0]
    tb, nb = _gla_blocks(t)

    def blk(i):
        return i if rev else (nb - 1 - i)

    def body(uq_ref, uf_ref, ui_ref, l0_ref, l1_ref, ss_ref, do_ref,
             dq_ref, df_ref, di_ref, dl0_ref, dl1_ref, dst_ref):
        @pl.when(pl.program_id(0) == 0)
        def _():
            dst_ref[...] = jnp.zeros(dst_ref.shape, F32)
            dl0_ref[...] = jnp.zeros(dl0_ref.shape, F32)
            dl1_ref[...] = jnp.zeros(dl1_ref.shape, F32)
        heads = range(HG_HEADS)
        _, vjp = jax.vjp(functools.partial(_gla_block, rev=rev), uq_ref[...], uf_ref[...], ui_ref[...],
                         l0_ref[...], l1_ref[...], tuple(ss_ref[0, h] for h in heads))
        dq, df, di, dl0, dl1, dst = vjp((do_ref[...], tuple(dst_ref[h] for h in heads)))
        dq_ref[...] = dq
        df_ref[...] = df
        di_ref[...] = di
        dl0_ref[...] += dl0
        dl1_ref[...] += dl1
        for h in heads:
            dst_ref[h] = dst[h]

    row = lambda cb: pl.BlockSpec((tb, HG_W), lambda i: (blk(i), cb))
    vec = pl.BlockSpec((1, HG_W), lambda i: (0, 0))
    orow = pl.BlockSpec((tb, HG_W), lambda i: (blk(i), 0))
    return pl.pallas_call(
        body, name=name, grid=(nb,),
        in_specs=[row(0), row(fcol), row(3), vec, vec,
                  pl.BlockSpec((1, HG_HEADS, HG_D, HG_D), lambda i: (blk(i), 0, 0, 0)), orow],
        out_specs=[orow, orow, orow, vec, vec],
        out_shape=[jax.ShapeDtypeStruct((t, HG_W), F32)] * 3 + [jax.ShapeDtypeStruct((1, HG_W), F32)] * 2,
        scratch_shapes=[pltpu.VMEM((HG_HEADS, HG_D, HG_D), F32)],
        compiler_params=_params(("arbitrary",)),
    )(u, u, u, l0, l1, ss, do)


def _rope_tables(t):
    rows = t // GRID_W
    row = jnp.repeat(jnp.arange(rows), GRID_W).astype(F32)
    col = jnp.tile(jnp.arange(GRID_W), rows).astype(F32)
    axis_dim = ATT_DH // 2
    freqs = ROPE_THETA ** (-jnp.arange(0, axis_dim, 2, dtype=F32) / axis_dim)
    ang = jnp.concatenate([row[:, None] * freqs, col[:, None] * freqs], axis=-1)
    cos2 = jnp.repeat(jnp.cos(ang), 2, axis=-1)
    sin2 = jnp.repeat(jnp.sin(ang), 2, axis=-1) * jnp.tile(jnp.array([-1.0, 1.0], F32), ATT_DH // 2)
    return cos2, sin2


def _group_sum_matrix(width):
    idx = np.arange(width) // ATT_DH
    return jnp.asarray((idx[:, None] == idx[None, :]).astype(np.float32))


def _tile_matrix(width):
    m = np.zeros((LANES, width), np.float32)
    m[np.arange(width) % ATT_DH, np.arange(width)] = 1.0
    return jnp.asarray(m)


def _tile_w(w128, tile_m):
    w8 = jnp.broadcast_to(w128, (8, LANES))
    return jnp.sum(xdot_r(w8, tile_m), axis=0, keepdims=True) * 0.125


def _head_norm_rope(a, w128, cos_t, sin_t, gsum, tile_m, scale):
    ssq = xdot_r(a * a, gsum)
    y = a * lax.rsqrt(ssq * (1.0 / ATT_DH) + EPS) * _tile_w(w128, tile_m)
    return (y * cos_t + swap_pairs(y) * sin_t) * scale


def _att_prep_fn(aq, ak, ck, sk, qw, kw, gq, gk, tq, tk):
    reps = aq.shape[1] // ck.shape[1]
    cq, sq = jnp.concatenate([ck] * reps, axis=1), jnp.concatenate([sk] * reps, axis=1)
    q = _head_norm_rope(aq, qw, cq, sq, gq, tq, ATT_DH ** -0.5)
    k = _head_norm_rope(ak, kw, ck, sk, gk, tk, 1.0)
    return q, k


def _head_t(x, heads):
    xt = x.T
    return [xt[h * ATT_DH:(h + 1) * ATT_DH] for h in range(heads)]


def _head_s(x, heads):
    lane = lax.broadcasted_iota(jnp.int32, (x.shape[0], LANES), 1)
    out = []
    for h in range(heads):
        pair = x[:, (h // 2) * LANES:(h // 2 + 1) * LANES]
        if h % 2:
            pair = pltpu.roll(pair, ATT_DH, 1)
        out.append(jnp.where(lane < ATT_DH, pair, 0.0))
    return out


def _from_head_t(tile):
    return jnp.concatenate([tile[h, 0] for h in range(tile.shape[0])], axis=0).T


def _pad_rows(a):
    return jnp.concatenate([a, jnp.zeros(a.shape, a.dtype)], axis=0)


def _col_bcast(row_vec):
    return jnp.broadcast_to(row_vec, (LANES, row_vec.shape[1])).T


FA_K_FWD = 512
FA_K_BWD = 512
FA_STRIPS_FWD = 8
FA_STRIPS_BWD = 8


def _key_block(ref, j, tiles):
    return jnp.concatenate([ref[0, j * tiles + i] for i in range(tiles)], axis=1)


def _riding(body, n_in, n_out, grid, ride):
    if ride is None:
        return body, [], [], [], [], []
    kind, arrays = ride
    n = len(arrays)
    steps = int(np.prod(grid))

    def riding_body(*refs):
        cuts = np.cumsum([0, n_in, n, n_out, n]).tolist()
        own_in, rin, own_out, rout = (refs[a:b] for a, b in zip(cuts[:-1], cuts[1:]))
        start, forward, finish = _exchange_phases(rin, rout, *refs[cuts[-1]:], kind)
        step = 0
        for axis, size in enumerate(grid):
            step = step * size + pl.program_id(axis)
        pl.when(step == 0)(start)
        pl.when(step == steps // 2)(forward)
        body(*own_in, *own_out)
        pl.when(step == steps - 1)(finish)

    return riding_body, _hbm_specs(n), _hbm_specs(n), _exchange_shapes(arrays, kind), _exchange_sems(n), list(arrays)


def fa_fwd(qt, ks, vt, ride=None):
    _, ns, dh, nq = qt.shape
    lk = ks.shape[1]
    bk = min(FA_K_FWD, lk)
    nk = lk // bk
    spg = min(FA_STRIPS_FWD, ns)
    grid = (ATT_KV, ns // spg)

    def body(q_ref, k_ref, v_ref, o_ref, lse_ref):
        qs = [_pad_rows(q_ref[0, c]) for c in range(spg)]

        def keys(j):
            return k_ref[0, j * bk:(j + 1) * bk, :]

        def step(j, carry):
            st0, stats = carry
            kb = keys(j)
            vb = _key_block(v_ref, j, bk // nq)
            sts = [st0] + [_dg(kb, qs[c], 1, 0) for c in range(1, spg)]
            out = []
            for c in range(spg):
                m, l, acc = stats[c]
                m_new = jnp.maximum(m, jnp.max(sts[c], axis=0, keepdims=True))
                alpha = jnp.exp(m - m_new)
                p = jnp.exp(sts[c] - m_new)
                l = alpha * l + jnp.sum(p, axis=0, keepdims=True)
                if c == spg - 1:
                    st0 = _dg(keys(min(j + 1, nk - 1)), qs[0], 1, 0)
                acc = alpha * acc + _dg(vb, p, 1, 0)
                out.append((m_new, l, acc))
            return st0, tuple(out)

        init = tuple((jnp.full((1, nq), -jnp.inf, F32), jnp.zeros((1, nq), F32), jnp.zeros((dh, nq), F32))
                     for _ in range(spg))
        carry = (_dg(keys(0), qs[0], 1, 0), init)
        for j in range(nk):
            carry = step(j, carry)
        _, res = carry
        for c in range(spg):
            m, l, acc = res[c]
            o_ref[0, c] = acc / l
            lse_ref[0, c] = _col_bcast(m + jnp.log(l))

    body, r_in, r_out, r_shapes, r_sems, r_args = _riding(body, 3, 2, grid, ride)
    return pl.pallas_call(
        body, name="fa_fwd", grid=grid,
        in_specs=[pl.BlockSpec((1, spg, dh, nq), lambda g, i: (g, i, 0, 0)),
                  pl.BlockSpec((1, lk, LANES), lambda g, i: (g, 0, 0)),
                  pl.BlockSpec((1, lk // nq, dh, nq), lambda g, i: (g, 0, 0, 0))] + r_in,
        out_specs=[pl.BlockSpec((1, spg, dh, nq), lambda g, i: (g, i, 0, 0)),
                   pl.BlockSpec((1, spg, nq, LANES), lambda g, i: (g, i, 0, 0))] + r_out,
        out_shape=[jax.ShapeDtypeStruct((ATT_KV, ns, dh, nq), F32),
                   jax.ShapeDtypeStruct((ATT_KV, ns, nq, LANES), F32)] + r_shapes,
        scratch_shapes=r_sems,
        compiler_params=_params(("arbitrary", "arbitrary")),
    )(qt, ks, vt, *r_args)


def fa_bwd(qs, qt, dos, dot_, ot, lse, ks, kt, vt, ride=None):
    _, ns, dh, nq = qt.shape
    lk = ks.shape[1]
    bk = min(FA_K_BWD, lk)
    nk = lk // bk
    tiles = bk // nq
    spg = min(FA_STRIPS_BWD, ns)
    nc = bk // LANES
    grid = (ATT_KV, ns // spg)

    def body(qs_ref, qt_ref, dos_ref, dot_ref, ot_ref, lse_ref, ks_ref, kt_ref, vt_ref, dq_ref, dk_ref, dv_ref):
        @pl.when(pl.program_id(1) == 0)
        def _():
            dk_ref[...] = jnp.zeros(dk_ref.shape, F32)
            dv_ref[...] = jnp.zeros(dv_ref.shape, F32)

        strips = range(spg)
        lse_b = [lse_ref[0, c] for c in strips]
        d_b = [_col_bcast(jnp.sum(dot_ref[0, c].astype(F32) * ot_ref[0, c], axis=0, keepdims=True)) for c in strips]

        def step(j, dqs):
            ktb, vtb = _pad_rows(_key_block(kt_ref, j, tiles)), _pad_rows(_key_block(vt_ref, j, tiles))
            kb = ks_ref[0, j * bk:(j + 1) * bk, :]
            prods = [(_dg(qs_ref[0, c], ktb, 1, 0), _dg(dos_ref[0, c], vtb, 1, 0)) for c in strips]
            out = []
            for c in strips:
                s, dp = prods[c]
                ps, dss = [], []
                for cc in range(nc):
                    sl = slice(cc * LANES, (cc + 1) * LANES)
                    pc = jnp.exp(s[:, sl] - lse_b[c])
                    ps.append(pc.astype(BF16))
                    dss.append((pc * (dp[:, sl] - d_b[c])).astype(BF16))
                p, ds = jnp.concatenate(ps, axis=1), jnp.concatenate(dss, axis=1)
                dv = _dg(dot_ref[0, c], p, 1, 0)
                dk = _dg(qt_ref[0, c], ds, 1, 0)
                for i in range(tiles):
                    dv_ref[0, j * tiles + i] += dv[:, i * nq:(i + 1) * nq]
                    dk_ref[0, j * tiles + i] += dk[:, i * nq:(i + 1) * nq]
                out.append(dqs[c] + _dg(ds, kb, 1, 0))
            return tuple(out)

        dqs = tuple(jnp.zeros((nq, LANES), F32) for _ in strips)
        for j in range(nk):
            dqs = step(j, dqs)
        for c in strips:
            dq_ref[0, c] = dqs[c].T[:dh]

    sspec = pl.BlockSpec((1, spg, nq, LANES), lambda g, i: (g, i, 0, 0))
    tspec = pl.BlockSpec((1, spg, dh, nq), lambda g, i: (g, i, 0, 0))
    kspec = pl.BlockSpec((1, lk // nq, dh, nq), lambda g, i: (g, 0, 0, 0))
    body, r_in, r_out, r_shapes, r_sems, r_args = _riding(body, 9, 3, grid, ride)
    return pl.pallas_call(
        body, name="fa_bwd", grid=grid,
        in_specs=[sspec, tspec, sspec, tspec, tspec, sspec, pl.BlockSpec((1, lk, LANES), lambda g, i: (g, 0, 0)),
                  kspec, kspec] + r_in,
        out_specs=[tspec, kspec, kspec] + r_out,
        out_shape=[jax.ShapeDtypeStruct((ATT_KV, ns, dh, nq), F32),
                   jax.ShapeDtypeStruct((ATT_KV, lk // nq, dh, nq), F32),
                   jax.ShapeDtypeStruct((ATT_KV, lk // nq, dh, nq), F32)] + r_shapes,
        scratch_shapes=r_sems,
        compiler_params=_params(("arbitrary", "arbitrary")),
    )(qs, qt, dos, dot_, ot, lse, ks, kt, vt, *r_args)


def _post_mix_fn(of, ob, ug, oa, hgw, attw):
    o = of + ob
    parts = []
    for h in range(HG_HEADS):
        parts.append(_rms(o[:, h * HG_D:(h + 1) * HG_D], hgw))
    hg = jnp.concatenate(parts, axis=1) * (ug * _sigmoid(ug))
    return jnp.concatenate([hg, _rms(oa, attw)], axis=1)


def _swiglu_fn(gate, up):
    return gate * _sigmoid(gate) * up


def _loss_fn(x2, w, tgt):
    e = _rms(x2, w) - tgt
    return 0.5 * jnp.sum(jnp.mean(e * e, axis=-1, keepdims=True), axis=0, keepdims=True)


def _place():
    return lax.axis_index("x"), lax.axis_index("y"), lax.axis_index("c")


def _other_chips(x, y):
    return [(1 - x, y), (x, 1 - y), (1 - x, 1 - y)]


def _hbm_specs(n):
    return [pl.BlockSpec(memory_space=pl.ANY)] * n


SEMS_PER_ARRAY = 7


def _exchange_sems(n):
    return [pltpu.SemaphoreType.DMA((SEMS_PER_ARRAY * n,)), pltpu.SemaphoreType.DMA((SEMS_PER_ARRAY * n,)),
            pltpu.SemaphoreType.DMA((n,))]


def _exchange_phases(srcs, outs, ssem, rsem, lsem, kind):
    n = len(srcs)
    x, y, c = _place()
    k = 2 * x + y
    sib = (x, y, 1 - c)
    if kind == "siblings":
        def swap(a):
            hr = srcs[a].shape[1] // 2
            return pltpu.make_async_remote_copy(src_ref=srcs[a].at[:, pl.ds((1 - c) * hr, hr), :], dst_ref=outs[a],
                                                send_sem=ssem.at[a], recv_sem=rsem.at[a], device_id=sib,
                                                device_id_type=MESH)

        def start_swaps():
            for a in range(n):
                swap(a).start()

        def finish_swaps():
            for a in range(n):
                swap(a).wait()

        return start_swaps, lambda: None, finish_swaps
    reduce = kind == "reduce"
    chips = _other_chips(x, y)
    pairs = [(a, j) for a in range(n) for j in range(3)]

    def hrows(a):
        return srcs[a].shape[1] if reduce else srcs[a].shape[0] // 2

    def half(a, kk, cc):
        return outs[a].at[kk, pl.ds(cc * hrows(a), hrows(a)), :]

    def mine(a, kk):
        return srcs[a].at[kk] if reduce else srcs[a].at[pl.ds(c * hrows(a), hrows(a)), :]

    def copy(a, j, src_ref, dst_ref, to):
        return pltpu.make_async_remote_copy(src_ref=src_ref, dst_ref=dst_ref, send_sem=ssem.at[SEMS_PER_ARRAY * a + j],
                                            recv_sem=rsem.at[SEMS_PER_ARRAY * a + j], device_id=to, device_id_type=MESH)

    def local(a):
        if reduce:
            return pltpu.make_async_copy(srcs[a].at[k], half(a, k, c), lsem.at[a])
        return pltpu.make_async_copy(srcs[a], outs[a].at[k], lsem.at[a])

    def ici(a, j, arriving):
        px, py = chips[j]
        kk = 2 * px + py
        if arriving:
            return copy(a, j, mine(a, k), half(a, kk, c), (px, py, c))
        return copy(a, j, mine(a, kk), half(a, k, c), (px, py, c))

    def passed(a, j, arriving):
        px, py = chips[j]
        kk = 2 * px + py
        return copy(a, 3 + j, half(a, kk, c), half(a, kk, (1 - c) if arriving else c), sib)

    def own(a, arriving):
        return copy(a, 6, mine(a, k), half(a, k, (1 - c) if arriving else c), sib)

    def start():
        for a in range(n):
            local(a).start()
        for a, j in pairs:
            ici(a, j, False).start()
        if reduce:
            for a in range(n):
                own(a, False).start()

    def forward():
        for a, j in pairs:
            ici(a, j, True).wait_recv()
            passed(a, j, False).start()

    def finish():
        for a in range(n):
            if reduce:
                own(a, True).wait_recv()
            for j in range(3):
                passed(a, j, True).wait_recv()
        for a, j in pairs:
            ici(a, j, False).wait_send()
            passed(a, j, False).wait_send()
        for a in range(n):
            if reduce:
                own(a, False).wait_send()
            local(a).wait()

    return start, forward, finish


def _exchange_shapes(arrays, kind):
    if kind == "siblings":
        return [jax.ShapeDtypeStruct((N_CHIPS, g.shape[1] // 2, g.shape[2]), g.dtype) for g in arrays]
    if kind == "reduce":
        return [jax.ShapeDtypeStruct((N_CHIPS, 2 * p.shape[1], p.shape[2]), p.dtype) for p in arrays]
    return [jax.ShapeDtypeStruct((N_CHIPS,) + s.shape, s.dtype) for s in arrays]


def exchange(arrays, kind, name):
    n = len(arrays)

    def body(*refs):
        start, forward, finish = _exchange_phases(refs[:n], refs[n:2 * n], *refs[2 * n:], kind)
        start()
        forward()
        finish()

    return pl.pallas_call(
        body, name=name, in_specs=_hbm_specs(n), out_specs=_hbm_specs(n),
        out_shape=_exchange_shapes(arrays, kind), scratch_shapes=_exchange_sems(n),
    )(*arrays)


def allreduce_small(p, name):
    rows, width = p.shape

    def body(p_ref, s_ref, gath, ssem, rsem):
        x, y, c = _place()
        me = 4 * x + 2 * y + c
        copies = []
        for d in range(1, N_DEV):
            dx, dy, dc = (d >> 2) & 1, (d >> 1) & 1, d & 1
            peer = (x ^ dx, y ^ dy, c ^ dc)
            cp = pltpu.make_async_remote_copy(src_ref=p_ref, dst_ref=gath.at[me], send_sem=ssem.at[d - 1],
                                              recv_sem=rsem.at[d - 1], device_id=peer, device_id_type=MESH)
            cp.start()
            copies.append(cp)
        gath[me] = p_ref[...]
        for d, cp in enumerate(copies, start=1):
            dx, dy, dc = (d >> 2) & 1, (d >> 1) & 1, d & 1
            peer_slot = 4 * (x ^ dx) + 2 * (y ^ dy) + (c ^ dc)
            pltpu.make_async_remote_copy(src_ref=p_ref, dst_ref=gath.at[peer_slot], send_sem=ssem.at[d - 1],
                                         recv_sem=rsem.at[d - 1], device_id=(x ^ dx, y ^ dy, c ^ dc),
                                         device_id_type=MESH).wait_recv()
        for cp in copies:
            cp.wait_send()
        acc = gath[0]
        for d in range(1, N_DEV):
            acc = acc + gath[d]
        s_ref[...] = acc

    return pl.pallas_call(
        body, name=name,
        in_specs=[pl.BlockSpec(memory_space=pltpu.VMEM)],
        out_specs=pl.BlockSpec(memory_space=pltpu.VMEM),
        out_shape=jax.ShapeDtypeStruct((rows, width), F32),
        scratch_shapes=[pltpu.VMEM((N_DEV, rows, width), F32), pltpu.SemaphoreType.DMA((N_DEV - 1,)),
                        pltpu.SemaphoreType.DMA((N_DEV - 1,))],
    )(p)


def add_my_half(g, recv, name):
    _, r, cols = g.shape
    hr = r // 2
    tb = _pick(hr, 512, 8)
    nb = hr // tb
    c_arr = lax.axis_index("c").astype(jnp.int32).reshape(1)

    def body(c_ref, g_ref, r_ref, o_ref):
        o_ref[...] = (g_ref[...] + r_ref[...]).astype(o_ref.dtype)

    return pl.pallas_call(
        body, name=name,
        grid_spec=pltpu.PrefetchScalarGridSpec(
            num_scalar_prefetch=1, grid=(N_CHIPS, nb),
            in_specs=[pl.BlockSpec((1, tb, cols), lambda k, i, c_ref: (k, c_ref[0] * nb + i, 0)),
                      pl.BlockSpec((1, tb, cols), lambda k, i, c_ref: (k, i, 0))],
            out_specs=pl.BlockSpec((1, tb, cols), lambda k, i, c_ref: (k, i, 0))),
        out_shape=jax.ShapeDtypeStruct((N_CHIPS, hr, cols), BF16),
        compiler_params=_params(("arbitrary", "arbitrary")),
    )(c_arr, g, recv)


def sum_chips(parts, name):
    _, hr, cols = parts.shape
    tb = _pick(hr, 512, 8)

    def body(p_ref, o_ref):
        o_ref[...] = ((p_ref[0].astype(F32) + p_ref[1].astype(F32)) + p_ref[2].astype(F32)) + p_ref[3].astype(F32)

    return pl.pallas_call(
        body, name=name, grid=(hr // tb,),
        in_specs=[pl.BlockSpec((N_CHIPS, tb, cols), lambda i: (0, i, 0))],
        out_specs=pl.BlockSpec((tb, cols), lambda i: (i, 0)),
        out_shape=jax.ShapeDtypeStruct((hr, cols), F32),
        compiler_params=_params(("arbitrary",)),
    )(parts)


def adamw(w, g, m, v, name):
    rows, width = w.shape
    tb = _pick(rows, 512, 8)

    def body(w_ref, g_ref, m_ref, v_ref, d_ref, mo_ref, vo_ref):
        gg = g_ref[...]
        m_new = ADAM_B1 * m_ref[...] + (1.0 - ADAM_B1) * gg
        v_new = ADAM_B2 * v_ref[...] + (1.0 - ADAM_B2) * (gg * gg)
        m_hat = m_new / (1.0 - ADAM_B1 ** ADAM_STEP)
        v_hat = v_new / (1.0 - ADAM_B2 ** ADAM_STEP)
        d_ref[...] = -ADAM_LR * (m_hat / (jnp.sqrt(v_hat) + ADAM_EPS) + ADAM_WD * w_ref[...])
        mo_ref[...] = m_new
        vo_ref[...] = v_new

    spec = pl.BlockSpec((tb, width), lambda i: (i, 0))
    return pl.pallas_call(
        body, name=name, grid=(rows // tb,), in_specs=[spec] * 4, out_specs=[spec] * 3,
        out_shape=[jax.ShapeDtypeStruct((rows, width), F32)] * 3,
        compiler_params=_params(("arbitrary",)),
    )(w, g, m, v)


def _pack_rows(vecs, width=1024):
    rows, cur, used = [], [], 0
    for v in vecs:
        n = v.shape[1]
        if used + n > width:
            cur.append(jnp.zeros((1, width - used), F32))
            rows.append(jnp.concatenate(cur, axis=1))
            cur, used = [], 0
        cur.append(v)
        used += n
    cur.append(jnp.zeros((1, width - used), F32))
    rows.append(jnp.concatenate(cur, axis=1))
    return rows


def _pad128(v):
    return jnp.pad(v, ((0, 0), (0, LANES - v.shape[1])))


def kernel(x, norm1_w, w_in, lb_logits, hg_norm_w, q_norm_w, k_norm_w, att_norm_w, w_out, norm2_w, w_gate_up, w_down, final_norm_w, loss_target, m_norm1_w, m_w_in, m_lb_logits, m_hg_norm_w, m_q_norm_w, m_k_norm_w, m_att_norm_w, m_w_out, m_norm2_w, m_w_gate_up, m_w_down, m_final_norm_w, v_norm1_w, v_w_in, v_lb_logits, v_hg_norm_w, v_q_norm_w, v_k_norm_w, v_att_norm_w, v_w_out, v_norm2_w, v_w_gate_up, v_w_down, v_final_norm_w):
    t, d = x.shape[1], x.shape[2]
    xi, yi, ci = _place()
    chip = 2 * xi + yi
    x2d = x.reshape(t, d)
    tgt = loss_target.reshape(t, d)
    tok = min(TOK, t)
    nt = t // tok
    ns = ATT_GROUP * nt

    late_w = [w_out[0].astype(BF16), w_gate_up[0].astype(BF16), w_down[0].astype(BF16)]
    lb_rows = lb_logits.reshape(4, LANES) * (ci == 0).astype(F32)
    lb_pad = lax.dynamic_update_slice(jnp.zeros((8, 1024), F32), lb_rows, (0, chip * LANES))
    lb_full = allreduce_small(lb_pad, "gather_lb")[:4, :HG_W]
    l_f0, l_f1, l_b0, l_b1 = (lb_full[i:i + 1] for i in range(4))

    n1 = norm1_w.reshape(1, d)
    n2 = norm2_w.reshape(1, d)
    nf = final_norm_w.reshape(1, d)
    tm = min(512, t)
    h1, g_in = _rows(lambda a, w: ((_rms(a, w),), ()), "norm1", t, tm, [_rin(x2d, tm)], [n1], [_rout(t, tm, d, BF16)],
                     ride=("gather", [w_in[0].astype(BF16)]))
    wf_in = g_in.transpose(1, 0, 2).reshape(g_in.shape[1], -1)
    u = mm_rows(h1, wf_in, "mm_in")
    o_f, ss_f = gla_fwd(u, l_f0, l_f1, 1, False, "gla_fwd_f")
    o_b, ss_b = gla_fwd(u, l_b0, l_b1, 2, True, "gla_fwd_b")

    cos2, sin2 = _rope_tables(t)
    ck, sk = jnp.tile(cos2, (1, ATT_KV)), jnp.tile(sin2, (1, ATT_KV))
    qw, kw = _pad128(q_norm_w.reshape(1, ATT_DH)), _pad128(k_norm_w.reshape(1, ATT_DH))
    gq, gk = _group_sum_matrix(ATT_QW), _group_sum_matrix(ATT_KVW)
    tq, tk = _tile_matrix(ATT_QW), _tile_matrix(ATT_KVW)
    prep_in = [_rin(u, tok, ATT_QW, 5), _rin(u, tok, ATT_KVW, 24), _rin(ck, tok), _rin(sk, tok)]
    prep_consts = [qw, kw, gq, gk, tq, tk]

    def att_prep_fn(aq, ak, av, *rest):
        q, k = _att_prep_fn(aq, ak, *rest)
        return (_head_t(q, ATT_HEADS), _head_s(q, ATT_HEADS), _head_s(k, ATT_KV), _head_t(k, ATT_KV),
                _head_t(av, ATT_KV)), ()

    q_t, q_s, k_s, k_t, v_t = _rows(
        att_prep_fn, "att_prep", t, tok, prep_in[:2] + [_rin(u, tok, ATT_KVW, 25)] + prep_in[2:], prep_consts,
        [_tout(ATT_HEADS, nt, ATT_DH, tok, BF16), _tout(ATT_HEADS, nt, tok, LANES, BF16),
         _tout(ATT_KV, nt, tok, LANES, BF16), _tout(ATT_KV, nt, ATT_DH, tok, BF16),
         _tout(ATT_KV, nt, ATT_DH, tok, BF16)])
    q_t = q_t.reshape(ATT_KV, ns, ATT_DH, tok)
    q_s = q_s.reshape(ATT_KV, ns, tok, LANES)
    k_s = k_s.reshape(ATT_KV, t, LANES)
    o_t, lse, g_out, g_gu, g_down = fa_fwd(q_t, k_s, v_t, ride=("gather", late_w))
    wf_gu = g_gu.transpose(1, 0, 2).reshape(g_gu.shape[1], -1)
    wf_out = g_out.reshape(-1, g_out.shape[2])
    wf_down = g_down.reshape(-1, g_down.shape[2])
    o_tiles = o_t.reshape(ATT_HEADS, nt, ATT_DH, tok)

    hgw = hg_norm_w.reshape(1, HG_D)
    attw = att_norm_w.reshape(1, ATT_QW)
    mix_in = [_rin(o_f, tok), _rin(o_b, tok), _rin(u, tok, HG_W, 4), _tin(o_tiles)]
    (mix,) = _rows(lambda of, ob, ug, ot, hw, aw: ((_post_mix_fn(of, ob, ug, _from_head_t(ot), hw, aw),), ()),
                   "post_mix", t, tok, mix_in, [hgw, attw], [_rout(t, tok, d, BF16)])
    x1, h2 = mm_res_norm(mix, wf_out, x2d, n2, "mm_out")
    gu2, act = ffn_up(h2, wf_gu)

    dx2, dx2b, g_final, loss_part = mm_res_loss(act, wf_down, x1, tgt, nf, "mm_down_loss")
    dgu2 = ffn_down_dx(dx2b, wf_down, gu2)
    gw_down = mm_tn(dx2b, act, "mm_down_dw", transpose_out=True)
    dx1, dx1b, g_norm2 = mm_nt_norm_bwd(dgu2, wf_gu, x1, n2, dx2, "mm_gate_up_dx", True)
    gw_gu = mm_tn(h2, dgu2, "mm_gate_up_dw", col_shards=N_CHIPS)
    dmix = mm_rows(dx1b, wf_out, "mm_out_dx", trans_b=True)
    gw_out = mm_tn(mix, dx1b, "mm_out_dw")

    def post_mix_bwd_fn(of, ob, ug, ot, dm, hgw_, attw_):
        _, vjp = jax.vjp(_post_mix_fn, of, ob, ug, _from_head_t(ot), hgw_, attw_)
        dof, _, dug, doa, dhgw, dattw = vjp(dm)
        return (dof, dug, _head_t(doa, ATT_HEADS), _head_s(doa, ATT_HEADS)), (dhgw, dattw)

    late = ["w_out", "w_gate_up", "w_down"]
    late_g = [gw_out.reshape(N_CHIPS, -1, d), gw_gu, gw_down.reshape(N_CHIPS, -1, d)]
    do_hg, du_g, do_t, do_s, g_hg, g_att, *late_recv = _rows(
        post_mix_bwd_fn, "post_mix_bwd", t, tok, mix_in + [_rin(dmix, tok)], [hgw, attw],
        [_rout(t, tok, HG_W, F32), _rout(t, tok, HG_W, BF16), _tout(ATT_HEADS, nt, ATT_DH, tok, BF16),
         _tout(ATT_HEADS, nt, tok, LANES, BF16)], [HG_D, ATT_QW], ride=("siblings", late_g))
    late_part = [add_my_half(g, r, "add_my_half_" + n) for g, r, n in zip(late_g, late_recv, late)]
    dq_t, dk_t, dv_t, *late_parts = fa_bwd(q_s, q_t, do_s.reshape(q_s.shape), do_t.reshape(q_t.shape), o_t, lse, k_s, k_t,
                                           v_t, ride=("reduce", late_part))

    dq_f, df_f, di_f, dl_f0, dl_f1 = gla_bwd(u, l_f0, l_f1, ss_f, do_hg, 1, False, "gla_bwd_f")
    dq_b, df_b, di_b, dl_b0, dl_b1 = gla_bwd(u, l_b0, l_b1, ss_b, do_hg, 2, True, "gla_bwd_b")

    def att_prep_bwd_fn(aq, ak, ck_, sk_, dqt, dkt, dvt, qf, qb, ff, fb, i_f, i_b, dg, qw_, kw_, gq_, gk_, tq_, tk_):
        _, vjp = jax.vjp(lambda a, b, c_, e: _att_prep_fn(a, b, ck_, sk_, c_, e, gq_, gk_, tq_, tk_),
                         aq, ak, qw_, kw_)
        daq, dak, dqw, dkw = vjp((_from_head_t(dqt), _from_head_t(dkt)))
        du_tile = jnp.concatenate([qf + qb, ff, fb, i_f + i_b, dg.astype(F32), daq, dak, _from_head_t(dvt)], axis=1)
        return (du_tile,), (dqw, dkw)

    du, g_q, g_k = _rows(
        att_prep_bwd_fn, "att_prep_bwd", t, tok,
        prep_in + [_tin(dq_t.reshape(ATT_HEADS, nt, ATT_DH, tok)), _tin(dk_t), _tin(dv_t)]
        + [_rin(a, tok) for a in (dq_f, dq_b, df_f, df_b, di_f, di_b, du_g)], prep_consts,
        [_rout(t, tok, u.shape[1], BF16)], [LANES, LANES])
    gw_in_t = mm_tn(h1, du, "mm_in_dw", transpose_out=True)

    names = ["w_in"] + late
    g_in4 = gw_in_t.reshape(N_CHIPS, -1, d)
    in_part = add_my_half(g_in4, exchange([g_in4], "siblings", "rs_siblings_w_in")[0], "add_my_half_w_in")
    grad_x, g_norm1, in_parts = mm_nt_norm_bwd(du, wf_in, x2d, n1, dx1, "mm_in_dx", False, ride=("reduce", [in_part]))
    parts = [in_parts] + late_parts
    g_shard = [sum_chips(p, "sum_chips_" + n) for p, n in zip(parts, names)]
    g_shard[0] = g_shard[0].T
    big = {}
    for n, g, w, m, v in zip(names, g_shard, (w_in, w_out, w_gate_up, w_down), (m_w_in, m_w_out, m_w_gate_up, m_w_down),
                             (v_w_in, v_w_out, v_w_gate_up, v_w_down)):
        dlt, mn, vn = adamw(w[0], g, m[0], v[0], "adamw_" + n)
        big[n] = (g[None], dlt[None], mn[None], vn[None])

    small = [g_norm1, g_norm2, g_final, g_att, g_hg, g_q, g_k, loss_part, dl_f0, dl_f1, dl_b0, dl_b1]
    packed = _pack_rows(small)
    packed += [jnp.zeros((1, 1024), F32)] * (8 - len(packed))
    tot = allreduce_small(jnp.concatenate(packed, axis=0), "allreduce_small")
    s_norm1, s_norm2, s_final = tot[0:1], tot[1:2], tot[2:3]
    s_att, s_hg, s_q, s_k = tot[3:4, 0:512], tot[3:4, 512:640], tot[3:4, 640:704], tot[3:4, 768:832]
    loss = tot[3, 896]
    s_lb = jnp.concatenate([tot[4:5, 0:512], tot[4:5, 512:1024], tot[5:6, 0:512], tot[5:6, 512:1024]], axis=0)
    s_lb = lax.dynamic_slice(s_lb, (0, chip * LANES), (4, LANES)).reshape(1, 512)

    snames = ["norm1_w", "lb_logits", "hg_norm_w", "q_norm_w", "k_norm_w", "att_norm_w", "norm2_w", "final_norm_w"]
    g_small = dict(zip(snames, [s_norm1, s_lb, s_hg, s_q, s_k, s_att, s_norm2, s_final]))
    w_small = dict(zip(snames, [norm1_w, lb_logits, hg_norm_w, q_norm_w, k_norm_w, att_norm_w, norm2_w, final_norm_w]))
    m_small = dict(zip(snames, [m_norm1_w, m_lb_logits, m_hg_norm_w, m_q_norm_w, m_k_norm_w, m_att_norm_w, m_norm2_w, m_final_norm_w]))
    v_small = dict(zip(snames, [v_norm1_w, v_lb_logits, v_hg_norm_w, v_q_norm_w, v_k_norm_w, v_att_norm_w, v_norm2_w, v_final_norm_w]))

    def pack_small(tree):
        rows = _pack_rows([tree[n].reshape(1, -1) for n in snames])
        rows += [jnp.zeros((1, 1024), F32)] * (8 - len(rows))
        return jnp.concatenate(rows, axis=0)

    d_s, m_s, v_s = adamw(pack_small(w_small), pack_small(g_small), pack_small(m_small), pack_small(v_small), "adamw_small")

    def unpack_small(a):
        out, r, used = {}, 0, 0
        for n in snames:
            size = w_small[n].size
            if used + size > 1024:
                r, used = r + 1, 0
            out[n] = a[r, used:used + size].reshape(w_small[n].shape)
            used += size
        return out

    d_sm, m_sm, v_sm = unpack_small(d_s), unpack_small(m_s), unpack_small(v_s)
    g_sm = {n: g_small[n].reshape(w_small[n].shape) for n in snames}

    order = ["norm1_w", "w_in", "lb_logits", "hg_norm_w", "q_norm_w", "k_norm_w", "att_norm_w", "w_out", "norm2_w",
             "w_gate_up", "w_down", "final_norm_w"]

    def pick(small_tree, idx):
        return [big[n][idx] if n in big else small_tree[n] for n in order]

    return (loss, grad_x.reshape(x.shape), *pick(g_sm, 0), *pick(d_sm, 1), *pick(m_sm, 2), *pick(v_sm, 3))
```

```python
import functools

import numpy as np
import jax
import jax.numpy as jnp
from jax import lax
from jax.experimental import pallas as pl
from jax.experimental.pallas import tpu as pltpu

F32 = jnp.float32
BF16 = jnp.bfloat16
MESH = pl.DeviceIdType.MESH

EPS = 1e-6
GRID_W = 64
HG_HEADS = 4
HG_D = 128
HG_W = HG_HEADS * HG_D
CHUNK = 64
ATT_HEADS = 8
ATT_KV = 2
ATT_GROUP = ATT_HEADS // ATT_KV
ATT_DH = 64
ATT_QW = ATT_HEADS * ATT_DH
ATT_KVW = ATT_KV * ATT_DH
ROPE_THETA = 10000.0
N_CHIPS = 4
N_DEV = 8

ADAM_LR = 0.001
ADAM_B1 = 0.9
ADAM_B2 = 0.999
ADAM_EPS = 1e-08
ADAM_WD = 0.01
ADAM_STEP = 10

VMEM_LIMIT = 52 * 1024 * 1024
LANES = 128
TOK = 256


def _params(sem=None):
    return pltpu.CompilerParams(dimension_semantics=sem, vmem_limit_bytes=VMEM_LIMIT)


def _dg(a, b, ca, cb):
    return lax.dot_general(a.astype(BF16), b.astype(BF16), (((ca,), (cb,)), ((), ())),
                           preferred_element_type=F32)


@jax.custom_vjp
def dot_nn(a, b):
    return _dg(a, b, 1, 0)


def _dot_nn_fwd(a, b):
    return _dg(a, b, 1, 0), (a, b)


def _dot_nn_bwd(res, g):
    a, b = res
    return _dg(g, b, 1, 1), _dg(a, g, 0, 0)


dot_nn.defvjp(_dot_nn_fwd, _dot_nn_bwd)


@jax.custom_vjp
def dot_nt(a, b):
    return _dg(a, b, 1, 1)


def _dot_nt_fwd(a, b):
    return _dg(a, b, 1, 1), (a, b)


def _dot_nt_bwd(res, g):
    a, b = res
    return _dg(g, b, 1, 0), _dg(g, a, 0, 0)


dot_nt.defvjp(_dot_nt_fwd, _dot_nt_bwd)


@jax.custom_vjp
def dot_tn(a, b):
    return _dg(a, b, 0, 0)


def _dot_tn_fwd(a, b):
    return _dg(a, b, 0, 0), (a, b)


def _dot_tn_bwd(res, g):
    a, b = res
    return _dg(b, g, 1, 1), _dg(a, g, 1, 0)


dot_tn.defvjp(_dot_tn_fwd, _dot_tn_bwd)


def _split3(a):
    hi = a.astype(BF16)
    r1 = a - hi.astype(F32)
    mid = r1.astype(BF16)
    lo = (r1 - mid.astype(F32)).astype(BF16)
    return lo, mid, hi


def _sum3(terms):
    lo, mid, hi = terms
    return (lo + mid) + hi


@jax.custom_vjp
def xdot_r(a, m):
    return _sum3([_dg(p, m, 1, 0) for p in _split3(a)])


def _xdot_r_fwd(a, m):
    return xdot_r(a, m), m


def _xdot_r_bwd(m, g):
    return _sum3([_dg(p, m, 1, 1) for p in _split3(g)]), jnp.zeros_like(m)


xdot_r.defvjp(_xdot_r_fwd, _xdot_r_bwd)


@jax.custom_vjp
def xdot_l(m, a):
    return _sum3([_dg(m, p, 1, 0) for p in _split3(a)])


def _xdot_l_fwd(m, a):
    return xdot_l(m, a), m


def _xdot_l_bwd(m, g):
    return jnp.zeros_like(m), _sum3([_dg(m, p, 0, 0) for p in _split3(g)])


xdot_l.defvjp(_xdot_l_fwd, _xdot_l_bwd)


@jax.custom_vjp
def swap_pairs(y):
    n = y.shape[-1]
    lane = lax.broadcasted_iota(jnp.int32, y.shape, 1)
    nxt = pltpu.roll(y, n - 1, 1)
    prv = pltpu.roll(y, 1, 1)
    return jnp.where(lane % 2 == 0, nxt, prv)


def _swap_fwd(y):
    return swap_pairs(y), None


def _swap_bwd(_, g):
    return (swap_pairs(g),)


swap_pairs.defvjp(_swap_fwd, _swap_bwd)


def _rms(x, w):
    return x * lax.rsqrt(jnp.mean(x * x, axis=-1, keepdims=True) + EPS) * w


def _sigmoid(x):
    return jax.nn.sigmoid(x)


def _rows(fn, name, t, tm, ins, consts, outs, accs=(), ride=None):
    n_r, n_c, n_o, n_a = len(ins), len(consts), len(outs), len(accs)

    def body(*refs):
        r = refs[:n_r]
        c = refs[n_r:n_r + n_c]
        o = refs[n_r + n_c:n_r + n_c + n_o]
        a = refs[n_r + n_c + n_o:]
        ro, ao = fn(*[x[...] for x in r], *[x[...] for x in c])
        for ref, val in zip(o, ro):
            if isinstance(val, (list, tuple)):
                for h, piece in enumerate(val):
                    ref[h, 0] = piece.astype(ref.dtype)
            else:
                ref[...] = val.astype(ref.dtype)
        if n_a:
            @pl.when(pl.program_id(0) == 0)
            def _():
                for ref in a:
                    ref[...] = jnp.zeros(ref.shape, F32)
            for ref, val in zip(a, ao):
                ref[...] += val

    in_specs = [s for _, s in ins]
    in_specs += [pl.BlockSpec(c.shape, lambda i, nd=c.ndim: (0,) * nd) for c in consts]
    out_specs = [s for _, s in outs] + [pl.BlockSpec((1, w), lambda i: (0, 0)) for w in accs]
    out_shape = [s for s, _ in outs] + [jax.ShapeDtypeStruct((1, w), F32) for w in accs]
    grid = (t // tm,)
    body, r_in, r_out, r_shapes, r_sems, r_args = _riding(body, n_r + n_c, n_o + n_a, grid, ride)
    return pl.pallas_call(
        body, name=name, grid=grid, in_specs=in_specs + r_in, out_specs=out_specs + r_out,
        out_shape=out_shape + r_shapes, scratch_shapes=r_sems, compiler_params=_params(("arbitrary",)),
    )(*[a for a, _ in ins], *consts, *r_args)


def _rin(a, tm, width=None, cb=0):
    width = a.shape[1] if width is None else width
    return a, pl.BlockSpec((tm, width), lambda i, cb=cb: (i, cb))


def _rout(t, tm, width, dtype):
    return jax.ShapeDtypeStruct((t, width), dtype), pl.BlockSpec((tm, width), lambda i: (i, 0))


def _tin(a):
    return a, pl.BlockSpec((a.shape[0], 1) + a.shape[2:], lambda i: (0, i, 0, 0))


def _tout(heads, nt, r, c, dtype):
    return jax.ShapeDtypeStruct((heads, nt, r, c), dtype), pl.BlockSpec((heads, 1, r, c), lambda i: (0, i, 0, 0))


def _pick(n, cap, mult):
    best = None
    for d in range(mult, min(n, cap) + 1, mult):
        if n % d == 0:
            best = d
    return best if best is not None else n


def mm_rows(a, b, name, trans_b=False, res=None, out_dtype=F32):
    m, k = a.shape
    n = b.shape[0] if trans_b else b.shape[1]
    tn = _pick(n, 3328, LANES)
    tm = _pick(m, 512 if k <= 3072 else 256, 8)
    has_res = res is not None

    def body(*refs):
        if has_res:
            a_ref, b_ref, r_ref, o_ref = refs
        else:
            a_ref, b_ref, o_ref = refs
        acc = _dg(a_ref[...], b_ref[...], 1, 1 if trans_b else 0)
        if has_res:
            acc = acc + r_ref[...]
        o_ref[...] = acc.astype(o_ref.dtype)

    in_specs = [pl.BlockSpec((tm, k), lambda j, i: (i, 0))]
    if trans_b:
        in_specs.append(pl.BlockSpec((tn, k), lambda j, i: (j, 0)))
    else:
        in_specs.append(pl.BlockSpec((k, tn), lambda j, i: (0, j)))
    args = [a, b]
    if has_res:
        in_specs.append(pl.BlockSpec((tm, tn), lambda j, i: (i, j)))
        args.append(res)
    return pl.pallas_call(
        body, name=name, grid=(n // tn, m // tm), in_specs=in_specs,
        out_specs=pl.BlockSpec((tm, tn), lambda j, i: (i, j)),
        out_shape=jax.ShapeDtypeStruct((m, n), out_dtype),
        compiler_params=_params(("arbitrary", "arbitrary")),
    )(*args)


def mm_tn(a, b, name, col_shards=None, transpose_out=False):
    t, m = a.shape
    parts = b.shape[0] if b.ndim == 3 else 1
    n = parts * b.shape[-1]
    tm = _pick(m, 512, LANES)
    width = n // col_shards if col_shards else None
    group = max(1, 3328 // width) if col_shards else 1
    tn = width * group if col_shards else _pick(n, 3328, LANES)
    tk = _pick(t, 1024, 8)
    nkk = t // tk
    per_part = b.shape[-1] // tn

    def body(a_ref, b_ref, o_ref, *scratch):
        acc_ref = scratch[0] if transpose_out else o_ref

        @pl.when(pl.program_id(2) == 0)
        def _():
            acc_ref[...] = jnp.zeros(acc_ref.shape, F32)
        acc = _dg(a_ref[...], b_ref[0] if b.ndim == 3 else b_ref[...], 0, 0)
        if col_shards:
            for s in range(group):
                acc_ref[s] += acc[:, s * width:(s + 1) * width]
        else:
            acc_ref[...] += acc
        if transpose_out:
            @pl.when(pl.program_id(2) == nkk - 1)
            def _():
                o_ref[...] = acc_ref[...].T

    if col_shards:
        out_spec = pl.BlockSpec((group, tm, width), lambda i, j, kk: (j, i, 0))
        out_shape = jax.ShapeDtypeStruct((col_shards, m, width), F32)
    elif transpose_out:
        out_spec = pl.BlockSpec((tn, tm), lambda i, j, kk: (j, i))
        out_shape = jax.ShapeDtypeStruct((n, m), F32)
    else:
        out_spec = pl.BlockSpec((tm, tn), lambda i, j, kk: (i, j))
        out_shape = jax.ShapeDtypeStruct((m, n), F32)
    if b.ndim == 3:
        b_spec = pl.BlockSpec((1, tk, tn), lambda i, j, kk: (j // per_part, kk, j % per_part))
    else:
        b_spec = pl.BlockSpec((tk, tn), lambda i, j, kk: (kk, j))
    return pl.pallas_call(
        body, name=name, grid=(m // tm, n // tn, nkk),
        in_specs=[pl.BlockSpec((tk, tm), lambda i, j, kk: (kk, i)), b_spec],
        out_specs=out_spec, out_shape=out_shape,
        scratch_shapes=[pltpu.VMEM((tm, tn), F32)] if transpose_out else [],
        compiler_params=_params(("arbitrary", "arbitrary", "arbitrary")),
    )(a, b)


def ffn_up(h2, w_gu):
    t, k = h2.shape
    dff = w_gu.shape[1] // 2
    tn = _pick(dff, 1408, LANES)
    ncol = dff // tn
    tm = _pick(t, 512, 8)

    def body(a_ref, wg_ref, wu_ref, gu_ref, act_ref):
        a = a_ref[...]
        g = _dg(a, wg_ref[...], 1, 0)
        u = _dg(a, wu_ref[...], 1, 0)
        gu_ref[0] = g.astype(gu_ref.dtype)
        gu_ref[1] = u.astype(gu_ref.dtype)
        act_ref[...] = _swiglu_fn(g, u).astype(act_ref.dtype)

    return pl.pallas_call(
        body, name="ffn_up", grid=(ncol, t // tm),
        in_specs=[pl.BlockSpec((tm, k), lambda s, i: (i, 0)), pl.BlockSpec((k, tn), lambda s, i: (0, s)),
                  pl.BlockSpec((k, tn), lambda s, i: (0, s + ncol))],
        out_specs=[pl.BlockSpec((2, tm, tn), lambda s, i: (0, i, s)), pl.BlockSpec((tm, tn), lambda s, i: (i, s))],
        out_shape=[jax.ShapeDtypeStruct((2, t, dff), BF16), jax.ShapeDtypeStruct((t, dff), BF16)],
        compiler_params=_params(("arbitrary", "arbitrary")),
    )(h2, w_gu, w_gu)


def ffn_down_dx(dx2b, w_down, gu2):
    t, k = dx2b.shape
    dff = w_down.shape[0]
    tn = _pick(dff, 1408, LANES)
    tm = _pick(t, 512, 8)

    def body(a_ref, w_ref, gu_ref, o_ref):
        pieces = [slice(0, tm // 2), slice(tm // 2, tm)]
        dacts = [_dg(a_ref[rows, :], w_ref[...], 1, 1) for rows in pieces]
        for rows, dact in zip(pieces, dacts):
            _, vjp = jax.vjp(_swiglu_fn, gu_ref[0, rows, :].astype(F32), gu_ref[1, rows, :].astype(F32))
            dg, du = vjp(dact)
            o_ref[0, rows, :] = dg.astype(o_ref.dtype)
            o_ref[1, rows, :] = du.astype(o_ref.dtype)

    blk = pl.BlockSpec((2, tm, tn), lambda s, i: (0, i, s))
    return pl.pallas_call(
        body, name="ffn_down_dx", grid=(dff // tn, t // tm),
        in_specs=[pl.BlockSpec((tm, k), lambda s, i: (i, 0)), pl.BlockSpec((tn, k), lambda s, i: (s, 0)), blk],
        out_specs=blk, out_shape=jax.ShapeDtypeStruct((2, t, dff), BF16),
        compiler_params=_params(("arbitrary", "arbitrary")),
    )(dx2b, w_down, gu2)


def mm_res_norm(a, b, res, w, name):
    t, k = a.shape
    d = b.shape[1]
    tm = _pick(t, 512, 8)

    def body(a_ref, b_ref, r_ref, w_ref, x_ref, h_ref):
        xv = _dg(a_ref[...], b_ref[...], 1, 0) + r_ref[...]
        x_ref[...] = xv
        h_ref[...] = _rms(xv, w_ref[...]).astype(h_ref.dtype)

    row = pl.BlockSpec((tm, d), lambda i: (i, 0))
    return pl.pallas_call(
        body, name=name, grid=(t // tm,),
        in_specs=[pl.BlockSpec((tm, k), lambda i: (i, 0)), pl.BlockSpec((k, d), lambda i: (0, 0)), row,
                  pl.BlockSpec((1, d), lambda i: (0, 0))],
        out_specs=[row, row],
        out_shape=[jax.ShapeDtypeStruct((t, d), F32), jax.ShapeDtypeStruct((t, d), BF16)],
        compiler_params=_params(("arbitrary",)),
    )(a, b, res, w)


def mm_res_loss(a, b, res, tgt, w, name):
    t, k = a.shape
    d = b.shape[1]
    tm = _pick(t, 512, 16)

    def body(a_ref, b_ref, r_ref, t_ref, w_ref, dx_ref, dxb_ref, dw_ref, loss_ref):
        @pl.when(pl.program_id(0) == 0)
        def _():
            dw_ref[...] = jnp.zeros(dw_ref.shape, F32)
            loss_ref[...] = jnp.zeros(loss_ref.shape, F32)
        pieces = [slice(0, tm // 2), slice(tm // 2, tm)]
        accs = [_dg(a_ref[rows, :], b_ref[...], 1, 0) for rows in pieces]
        for rows, acc in zip(pieces, accs):
            tg = t_ref[rows, :]
            val, vjp = jax.vjp(lambda x_, w_: _loss_fn(x_, w_, tg), acc + r_ref[rows, :], w_ref[...])
            dx, dw = vjp(jnp.ones((1, 1), F32))
            dx_ref[rows, :] = dx
            dxb_ref[rows, :] = dx.astype(dxb_ref.dtype)
            dw_ref[...] += dw
            loss_ref[...] += jnp.broadcast_to(val, (1, LANES))

    row = pl.BlockSpec((tm, d), lambda i: (i, 0))
    vec = pl.BlockSpec((1, d), lambda i: (0, 0))
    return pl.pallas_call(
        body, name=name, grid=(t // tm,),
        in_specs=[pl.BlockSpec((tm, k), lambda i: (i, 0)), pl.BlockSpec((k, d), lambda i: (0, 0)), row, row, vec],
        out_specs=[row, row, vec, pl.BlockSpec((1, LANES), lambda i: (0, 0))],
        out_shape=[jax.ShapeDtypeStruct((t, d), F32), jax.ShapeDtypeStruct((t, d), BF16),
                   jax.ShapeDtypeStruct((1, d), F32), jax.ShapeDtypeStruct((1, LANES), F32)],
        compiler_params=_params(("arbitrary",)),
    )(a, b, res, tgt, w)


def mm_nt_norm_bwd(a, wmat, x, w, dres, name, with_bf16, ride=None):
    parts = a.shape[0] if a.ndim == 3 else 1
    t, kp = a.shape[-2], a.shape[-1]
    d = wmat.shape[0]
    tm = _pick(t, 256, 8)

    n_out = 3 if with_bf16 else 2

    def body(*refs):
        a_ref, w_refs = refs[0], refs[1:1 + parts]
        x_ref, nw_ref, r_ref, dx_ref = refs[1 + parts:5 + parts]
        dw_ref = refs[4 + parts + n_out - 1]

        @pl.when(pl.program_id(0) == 0)
        def _():
            dw_ref[...] = jnp.zeros(dw_ref.shape, F32)
        dh = None
        for p in range(parts):
            term = _dg(a_ref[p] if a.ndim == 3 else a_ref[...], w_refs[p][...], 1, 1)
            dh = term if dh is None else dh + term
        _, vjp = jax.vjp(_rms, x_ref[...], nw_ref[...])
        dx, dw = vjp(dh)
        dx = dx + r_ref[...]
        dx_ref[...] = dx
        if with_bf16:
            refs[5 + parts][...] = dx.astype(BF16)
        dw_ref[...] += dw

    row = pl.BlockSpec((tm, d), lambda i: (i, 0))
    vec = pl.BlockSpec((1, d), lambda i: (0, 0))
    a_spec = pl.BlockSpec((parts, tm, kp), lambda i: (0, i, 0)) if a.ndim == 3 else pl.BlockSpec((tm, kp), lambda i: (i, 0))
    w_specs = [pl.BlockSpec((d, kp), lambda i, p=p: (0, p)) for p in range(parts)]
    outs = [row] + ([row] if with_bf16 else []) + [vec]
    shapes = [jax.ShapeDtypeStruct((t, d), F32)] + ([jax.ShapeDtypeStruct((t, d), BF16)] if with_bf16 else [])
    grid = (t // tm,)
    body, r_in, r_out, r_shapes, r_sems, r_args = _riding(body, 4 + parts, n_out, grid, ride)
    return pl.pallas_call(
        body, name=name, grid=grid,
        in_specs=[a_spec] + w_specs + [row, vec, row] + r_in,
        out_specs=outs + r_out, out_shape=shapes + [jax.ShapeDtypeStruct((1, d), F32)] + r_shapes,
        scratch_shapes=r_sems, compiler_params=_params(("arbitrary",)),
    )(a, *([wmat] * parts), x, w, dres, *r_args)


def _lower_bound(l0, l1):
    m = jnp.maximum(l0, l1)
    e0 = jnp.exp(l0 - m)
    e1 = jnp.exp(l1 - m)
    return e0 / (e0 + e1)


def _gla_consts(rev):
    ri = lax.broadcasted_iota(jnp.int32, (CHUNK, CHUNK), 0)
    ci = lax.broadcasted_iota(jnp.int32, (CHUNK, CHUNK), 1)
    keep = (ci >= ri) if rev else (ci <= ri)
    ref_mask = lax.broadcasted_iota(jnp.int32, (CHUNK, 1), 0) == (CHUNK // 2 if rev else CHUNK // 2 - 1)
    return keep, ref_mask


def _gla_block(uq, uf, ui, l0, l1, st_in, rev):
    ncb = uq.shape[0] // CHUNK
    heads = range(HG_HEADS)
    keep, ref_mask = _gla_consts(rev)
    cum = keep.astype(F32)
    lb = _lower_bound(l0, l1)
    q = uq * _sigmoid(uq)
    k = (1.0 - lb) * _sigmoid(-uf)
    g = jnp.log(lb + (1.0 - lb) * _sigmoid(uf))

    def rows(a, c):
        return a[c * CHUNK:(c + 1) * CHUNK]

    def head(a, h):
        return a[:, h * HG_D:(h + 1) * HG_D]

    bs = [xdot_l(cum, rows(g, c)) for c in range(ncb)]
    q_in, k_in, q_b, k_d, decay = [], [], [], [], []
    for c in range(ncb):
        b = bs[c]
        b_ref = jnp.sum(jnp.where(ref_mask, b, 0.0), axis=0, keepdims=True)
        b_last = jnp.sum(rows(g, c), axis=0, keepdims=True)
        qc, kc = rows(q, c), rows(k, c)
        q_in.append(qc * jnp.exp(b - b_ref))
        k_in.append(kc * jnp.exp(b_ref - b))
        q_b.append(q_in[c] * jnp.exp(b_ref))
        k_d.append(k_in[c] * jnp.exp(b_last - b_ref))
        decay.append(jnp.exp(b_last))
    scores = [[jnp.where(keep, dot_nt(head(q_in[c], h), head(k_in[c], h)), 0.0) for h in heads] for c in range(ncb)]
    o_intra = [[dot_nn(scores[c][h], head(rows(ui, c), h)) for h in heads] for c in range(ncb)]
    contrib = [[dot_tn(head(rows(ui, c), h), head(k_d[c], h)) for h in heads] for c in range(ncb)]
    st = list(st_in)
    o_rows = [None] * ncb
    for c in (reversed(range(ncb)) if rev else range(ncb)):
        parts = []
        for h in heads:
            parts.append(o_intra[c][h] + dot_nt(head(q_b[c], h), st[h]))
            st[h] = st[h] * head(decay[c], h) + contrib[c][h]
        o_rows[c] = jnp.concatenate(parts, axis=1)
    return jnp.concatenate(o_rows, axis=0), tuple(st)


def _gla_blocks(t):
    tb = min(512, t)
    return tb, t // tb


def gla_fwd(u, l0, l1, fcol, rev, name):
    t = u.shape[0]
    tb, nb = _gla_blocks(t)

    def blk(i):
        return (nb - 1 - i) if rev else i

    def body(uq_ref, uf_ref, ui_ref, l0_ref, l1_ref, o_ref, ss_ref, st_ref):
        @pl.when(pl.program_id(0) == 0)
        def _():
            st_ref[...] = jnp.zeros(st_ref.shape, F32)
        ss_ref[0] = st_ref[...]
        o, st_out = _gla_block(uq_ref[...], uf_ref[...], ui_ref[...], l0_ref[...], l1_ref[...],
                               tuple(st_ref[h] for h in range(HG_HEADS)), rev)
        o_ref[...] = o
        for h in range(HG_HEADS):
            st_ref[h] = st_out[h]

    row = lambda cb: pl.BlockSpec((tb, HG_W), lambda i: (blk(i), cb))
    vec = pl.BlockSpec((1, HG_W), lambda i: (0, 0))
    return pl.pallas_call(
        body, name=name, grid=(nb,),
        in_specs=[row(0), row(fcol), row(3), vec, vec],
        out_specs=[pl.BlockSpec((tb, HG_W), lambda i: (blk(i), 0)),
                   pl.BlockSpec((1, HG_HEADS, HG_D, HG_D), lambda i: (blk(i), 0, 0, 0))],
        out_shape=[jax.ShapeDtypeStruct((t, HG_W), F32),
                   jax.ShapeDtypeStruct((nb, HG_HEADS, HG_D, HG_D), F32)],
        scratch_shapes=[pltpu.VMEM((HG_HEADS, HG_D, HG_D), F32)],
        compiler_params=_params(("arbitrary",)),
    )(u, u, u, l0, l1)


def gla_bwd(u, l0, l1, ss, do, fcol, rev, name):
    t = u.shape[0]
    tb, nb = _gla_blocks(t)

    def blk(i):
        return i if rev else (nb - 1 - i)

    def body(uq_ref, uf_ref, ui_ref, l0_ref, l1_ref, ss_ref, do_ref,
             dq_ref, df_ref, di_ref, dl0_ref, dl1_ref, dst_ref):
        @pl.when(pl.program_id(0) == 0)
        def _():
            dst_ref[...] = jnp.zeros(dst_ref.shape, F32)
            dl0_ref[...] = jnp.zeros(dl0_ref.shape, F32)
            dl1_ref[...] = jnp.zeros(dl1_ref.shape, F32)
        heads = range(HG_HEADS)
        _, vjp = jax.vjp(functools.partial(_gla_block, rev=rev), uq_ref[...], uf_ref[...], ui_ref[...],
                         l0_ref[...], l1_ref[...], tuple(ss_ref[0, h] for h in heads))
        dq, df, di, dl0, dl1, dst = vjp((do_ref[...], tuple(dst_ref[h] for h in heads)))
        dq_ref[...] = dq
        df_ref[...] = df
        di_ref[...] = di
        dl0_ref[...] += dl0
        dl1_ref[...] += dl1
        for h in heads:
            dst_ref[h] = dst[h]

    row = lambda cb: pl.BlockSpec((tb, HG_W), lambda i: (blk(i), cb))
    vec = pl.BlockSpec((1, HG_W), lambda i: (0, 0))
    orow = pl.BlockSpec((tb, HG_W), lambda i: (blk(i), 0))
    return pl.pallas_call(
        body, name=name, grid=(nb,),
        in_specs=[row(0), row(fcol), row(3), vec, vec,
                  pl.BlockSpec((1, HG_HEADS, HG_D, HG_D), lambda i: (blk(i), 0, 0, 0)), orow],
        out_specs=[orow, orow, orow, vec, vec],
        out_shape=[jax.ShapeDtypeStruct((t, HG_W), F32)] * 3 + [jax.ShapeDtypeStruct((1, HG_W), F32)] * 2,
        scratch_shapes=[pltpu.VMEM((HG_HEADS, HG_D, HG_D), F32)],
        compiler_params=_params(("arbitrary",)),
    )(u, u, u, l0, l1, ss, do)


def _rope_tables(t):
    rows = t // GRID_W
    row = jnp.repeat(jnp.arange(rows), GRID_W).astype(F32)
    col = jnp.tile(jnp.arange(GRID_W), rows).astype(F32)
    axis_dim = ATT_DH // 2
    freqs = ROPE_THETA ** (-jnp.arange(0, axis_dim, 2, dtype=F32) / axis_dim)
    ang = jnp.concatenate([row[:, None] * freqs, col[:, None] * freqs], axis=-1)
    cos2 = jnp.repeat(jnp.cos(ang), 2, axis=-1)
    sin2 = jnp.repeat(jnp.sin(ang), 2, axis=-1) * jnp.tile(jnp.array([-1.0, 1.0], F32), ATT_DH // 2)
    return cos2, sin2


def _group_sum_matrix(width):
    idx = np.arange(width) // ATT_DH
    return jnp.asarray((idx[:, None] == idx[None, :]).astype(np.float32))


def _tile_matrix(width):
    m = np.zeros((LANES, width), np.float32)
    m[np.arange(width) % ATT_DH, np.arange(width)] = 1.0
    return jnp.asarray(m)


def _tile_w(w128, tile_m):
    w8 = jnp.broadcast_to(w128, (8, LANES))
    return jnp.sum(xdot_r(w8, tile_m), axis=0, keepdims=True) * 0.125


def _head_norm_rope(a, w128, cos_t, sin_t, gsum, tile_m, scale):
    ssq = xdot_r(a * a, gsum)
    y = a * lax.rsqrt(ssq * (1.0 / ATT_DH) + EPS) * _tile_w(w128, tile_m)
    return (y * cos_t + swap_pairs(y) * sin_t) * scale


def _att_prep_fn(aq, ak, ck, sk, qw, kw, gq, gk, tq, tk):
    reps = aq.shape[1] // ck.shape[1]
    cq, sq = jnp.concatenate([ck] * reps, axis=1), jnp.concatenate([sk] * reps, axis=1)
    q = _head_norm_rope(aq, qw, cq, sq, gq, tq, ATT_DH ** -0.5)
    k = _head_norm_rope(ak, kw, ck, sk, gk, tk, 1.0)
    return q, k


def _head_t(x, heads):
    xt = x.T
    return [xt[h * ATT_DH:(h + 1) * ATT_DH] for h in range(heads)]


def _head_s(x, heads):
    lane = lax.broadcasted_iota(jnp.int32, (x.shape[0], LANES), 1)
    out = []
    for h in range(heads):
        pair = x[:, (h // 2) * LANES:(h // 2 + 1) * LANES]
        if h % 2:
            pair = pltpu.roll(pair, ATT_DH, 1)
        out.append(jnp.where(lane < ATT_DH, pair, 0.0))
    return out


def _from_head_t(tile):
    return jnp.concatenate([tile[h, 0] for h in range(tile.shape[0])], axis=0).T


def _pad_rows(a):
    return jnp.concatenate([a, jnp.zeros(a.shape, a.dtype)], axis=0)


def _col_bcast(row_vec):
    return jnp.broadcast_to(row_vec, (LANES, row_vec.shape[1])).T


FA_K_FWD = 512
FA_K_BWD = 512
FA_STRIPS_FWD = 8
FA_STRIPS_BWD = 8


def _key_block(ref, j, tiles):
    return jnp.concatenate([ref[0, j * tiles + i] for i in range(tiles)], axis=1)


def _riding(body, n_in, n_out, grid, ride):
    if ride is None:
        return body, [], [], [], [], []
    kind, arrays = ride
    n = len(arrays)
    steps = int(np.prod(grid))

    def riding_body(*refs):
        cuts = np.cumsum([0, n_in, n, n_out, n]).tolist()
        own_in, rin, own_out, rout = (refs[a:b] for a, b in zip(cuts[:-1], cuts[1:]))
        start, forward, finish = _exchange_phases(rin, rout, *refs[cuts[-1]:], kind)
        step = 0
        for axis, size in enumerate(grid):
            step = step * size + pl.program_id(axis)
        pl.when(step == 0)(start)
        pl.when(step == steps // 2)(forward)
        body(*own_in, *own_out)
        pl.when(step == steps - 1)(finish)

    return riding_body, _hbm_specs(n), _hbm_specs(n), _exchange_shapes(arrays, kind), _exchange_sems(n), list(arrays)


def fa_fwd(qt, ks, vt, ride=None):
    _, ns, dh, nq = qt.shape
    lk = ks.shape[1]
    bk = min(FA_K_FWD, lk)
    nk = lk // bk
    spg = min(FA_STRIPS_FWD, ns)
    grid = (ATT_KV, ns // spg)

    def body(q_ref, k_ref, v_ref, o_ref, lse_ref):
        qs = [_pad_rows(q_ref[0, c]) for c in range(spg)]

        def keys(j):
            return k_ref[0, j * bk:(j + 1) * bk, :]

        def step(j, carry):
            st0, stats = carry
            kb = keys(j)
            vb = _key_block(v_ref, j, bk // nq)
            sts = [st0] + [_dg(kb, qs[c], 1, 0) for c in range(1, spg)]
            out = []
            for c in range(spg):
                m, l, acc = stats[c]
                m_new = jnp.maximum(m, jnp.max(sts[c], axis=0, keepdims=True))
                alpha = jnp.exp(m - m_new)
                p = jnp.exp(sts[c] - m_new)
                l = alpha * l + jnp.sum(p, axis=0, keepdims=True)
                if c == spg - 1:
                    st0 = _dg(keys(min(j + 1, nk - 1)), qs[0], 1, 0)
                acc = alpha * acc + _dg(vb, p, 1, 0)
                out.append((m_new, l, acc))
            return st0, tuple(out)

        init = tuple((jnp.full((1, nq), -jnp.inf, F32), jnp.zeros((1, nq), F32), jnp.zeros((dh, nq), F32))
                     for _ in range(spg))
        carry = (_dg(keys(0), qs[0], 1, 0), init)
        for j in range(nk):
            carry = step(j, carry)
        _, res = carry
        for c in range(spg):
            m, l, acc = res[c]
            o_ref[0, c] = acc / l
            lse_ref[0, c] = _col_bcast(m + jnp.log(l))

    body, r_in, r_out, r_shapes, r_sems, r_args = _riding(body, 3, 2, grid, ride)
    return pl.pallas_call(
        body, name="fa_fwd", grid=grid,
        in_specs=[pl.BlockSpec((1, spg, dh, nq), lambda g, i: (g, i, 0, 0)),
                  pl.BlockSpec((1, lk, LANES), lambda g, i: (g, 0, 0)),
                  pl.BlockSpec((1, lk // nq, dh, nq), lambda g, i: (g, 0, 0, 0))] + r_in,
        out_specs=[pl.BlockSpec((1, spg, dh, nq), lambda g, i: (g, i, 0, 0)),
                   pl.BlockSpec((1, spg, nq, LANES), lambda g, i: (g, i, 0, 0))] + r_out,
        out_shape=[jax.ShapeDtypeStruct((ATT_KV, ns, dh, nq), F32),
                   jax.ShapeDtypeStruct((ATT_KV, ns, nq, LANES), F32)] + r_shapes,
        scratch_shapes=r_sems,
        compiler_params=_params(("arbitrary", "arbitrary")),
    )(qt, ks, vt, *r_args)


def fa_bwd(qs, qt, dos, dot_, ot, lse, ks, kt, vt, ride=None):
    _, ns, dh, nq = qt.shape
    lk = ks.shape[1]
    bk = min(FA_K_BWD, lk)
    nk = lk // bk
    tiles = bk // nq
    spg = min(FA_STRIPS_BWD, ns)
    nc = bk // LANES
    grid = (ATT_KV, ns // spg)

    def body(qs_ref, qt_ref, dos_ref, dot_ref, ot_ref, lse_ref, ks_ref, kt_ref, vt_ref, dq_ref, dk_ref, dv_ref):
        @pl.when(pl.program_id(1) == 0)
        def _():
            dk_ref[...] = jnp.zeros(dk_ref.shape, F32)
            dv_ref[...] = jnp.zeros(dv_ref.shape, F32)

        strips = range(spg)
        lse_b = [lse_ref[0, c] for c in strips]
        d_b = [_col_bcast(jnp.sum(dot_ref[0, c].astype(F32) * ot_ref[0, c], axis=0, keepdims=True)) for c in strips]

        def step(j, dqs):
            ktb, vtb = _pad_rows(_key_block(kt_ref, j, tiles)), _pad_rows(_key_block(vt_ref, j, tiles))
            kb = ks_ref[0, j * bk:(j + 1) * bk, :]
            prods = [(_dg(qs_ref[0, c], ktb, 1, 0), _dg(dos_ref[0, c], vtb, 1, 0)) for c in strips]
            out = []
            for c in strips:
                s, dp = prods[c]
                ps, dss = [], []
                for cc in range(nc):
                    sl = slice(cc * LANES, (cc + 1) * LANES)
                    pc = jnp.exp(s[:, sl] - lse_b[c])
                    ps.append(pc.astype(BF16))
                    dss.append((pc * (dp[:, sl] - d_b[c])).astype(BF16))
                p, ds = jnp.concatenate(ps, axis=1), jnp.concatenate(dss, axis=1)
                dv = _dg(dot_ref[0, c], p, 1, 0)
                dk = _dg(qt_ref[0, c], ds, 1, 0)
                for i in range(tiles):
                    dv_ref[0, j * tiles + i] += dv[:, i * nq:(i + 1) * nq]
                    dk_ref[0, j * tiles + i] += dk[:, i * nq:(i + 1) * nq]
                out.append(dqs[c] + _dg(ds, kb, 1, 0))
            return tuple(out)

        dqs = tuple(jnp.zeros((nq, LANES), F32) for _ in strips)
        for j in range(nk):
            dqs = step(j, dqs)
        for c in strips:
            dq_ref[0, c] = dqs[c].T[:dh]

    sspec = pl.BlockSpec((1, spg, nq, LANES), lambda g, i: (g, i, 0, 0))
    tspec = pl.BlockSpec((1, spg, dh, nq), lambda g, i: (g, i, 0, 0))
    kspec = pl.BlockSpec((1, lk // nq, dh, nq), lambda g, i: (g, 0, 0, 0))
    body, r_in, r_out, r_shapes, r_sems, r_args = _riding(body, 9, 3, grid, ride)
    return pl.pallas_call(
        body, name="fa_bwd", grid=grid,
        in_specs=[sspec, tspec, sspec, tspec, tspec, sspec, pl.BlockSpec((1, lk, LANES), lambda g, i: (g, 0, 0)),
                  kspec, kspec] + r_in,
        out_specs=[tspec, kspec, kspec] + r_out,
        out_shape=[jax.ShapeDtypeStruct((ATT_KV, ns, dh, nq), F32),
                   jax.ShapeDtypeStruct((ATT_KV, lk // nq, dh, nq), F32),
                   jax.ShapeDtypeStruct((ATT_KV, lk // nq, dh, nq), F32)] + r_shapes,
        scratch_shapes=r_sems,
        compiler_params=_params(("arbitrary", "arbitrary")),
    )(qs, qt, dos, dot_, ot, lse, ks, kt, vt, *r_args)


def _post_mix_fn(of, ob, ug, oa, hgw, attw):
    o = of + ob
    parts = []
    for h in range(HG_HEADS):
        parts.append(_rms(o[:, h * HG_D:(h + 1) * HG_D], hgw))
    hg = jnp.concatenate(parts, axis=1) * (ug * _sigmoid(ug))
    return jnp.concatenate([hg, _rms(oa, attw)], axis=1)


def _swiglu_fn(gate, up):
    return gate * _sigmoid(gate) * up


def _loss_fn(x2, w, tgt):
    e = _rms(x2, w) - tgt
    return 0.5 * jnp.sum(jnp.mean(e * e, axis=-1, keepdims=True), axis=0, keepdims=True)


def _place():
    return lax.axis_index("x"), lax.axis_index("y"), lax.axis_index("c")


def _other_chips(x, y):
    return [(1 - x, y), (x, 1 - y), (1 - x, 1 - y)]


def _hbm_specs(n):
    return [pl.BlockSpec(memory_space=pl.ANY)] * n


SEMS_PER_ARRAY = 7


def _exchange_sems(n):
    return [pltpu.SemaphoreType.DMA((SEMS_PER_ARRAY * n,)), pltpu.SemaphoreType.DMA((SEMS_PER_ARRAY * n,)),
            pltpu.SemaphoreType.DMA((n,))]


def _exchange_phases(srcs, outs, ssem, rsem, lsem, kind):
    n = len(srcs)
    x, y, c = _place()
    k = 2 * x + y
    sib = (x, y, 1 - c)
    if kind == "siblings":
        def swap(a):
            hr = srcs[a].shape[1] // 2
            return pltpu.make_async_remote_copy(src_ref=srcs[a].at[:, pl.ds((1 - c) * hr, hr), :], dst_ref=outs[a],
                                                send_sem=ssem.at[a], recv_sem=rsem.at[a], device_id=sib,
                                                device_id_type=MESH)

        def start_swaps():
            for a in range(n):
                swap(a).start()

        def finish_swaps():
            for a in range(n):
                swap(a).wait()

        return start_swaps, lambda: None, finish_swaps
    reduce = kind == "reduce"
    chips = _other_chips(x, y)
    pairs = [(a, j) for a in range(n) for j in range(3)]

    def hrows(a):
        return srcs[a].shape[1] if reduce else srcs[a].shape[0] // 2

    def half(a, kk, cc):
        return outs[a].at[kk, pl.ds(cc * hrows(a), hrows(a)), :]

    def mine(a, kk):
        return srcs[a].at[kk] if reduce else srcs[a].at[pl.ds(c * hrows(a), hrows(a)), :]

    def copy(a, j, src_ref, dst_ref, to):
        return pltpu.make_async_remote_copy(src_ref=src_ref, dst_ref=dst_ref, send_sem=ssem.at[SEMS_PER_ARRAY * a + j],
                                            recv_sem=rsem.at[SEMS_PER_ARRAY * a + j], device_id=to, device_id_type=MESH)

    def local(a):
        if reduce:
            return pltpu.make_async_copy(srcs[a].at[k], half(a, k, c), lsem.at[a])
        return pltpu.make_async_copy(srcs[a], outs[a].at[k], lsem.at[a])

    def ici(a, j, arriving):
        px, py = chips[j]
        kk = 2 * px + py
        if arriving:
            return copy(a, j, mine(a, k), half(a, kk, c), (px, py, c))
        return copy(a, j, mine(a, kk), half(a, k, c), (px, py, c))

    def passed(a, j, arriving):
        px, py = chips[j]
        kk = 2 * px + py
        return copy(a, 3 + j, half(a, kk, c), half(a, kk, (1 - c) if arriving else c), sib)

    def own(a, arriving):
        return copy(a, 6, mine(a, k), half(a, k, (1 - c) if arriving else c), sib)

    def start():
        for a in range(n):
            local(a).start()
        for a, j in pairs:
            ici(a, j, False).start()
        if reduce:
            for a in range(n):
                own(a, False).start()

    def forward():
        for a, j in pairs:
            ici(a, j, True).wait_recv()
            passed(a, j, False).start()

    def finish():
        for a in range(n):
            if reduce:
                own(a, True).wait_recv()
            for j in range(3):
                passed(a, j, True).wait_recv()
        for a, j in pairs:
            ici(a, j, False).wait_send()
            passed(a, j, False).wait_send()
        for a in range(n):
            if reduce:
                own(a, False).wait_send()
            local(a).wait()

    return start, forward, finish


def _exchange_shapes(arrays, kind):
    if kind == "siblings":
        return [jax.ShapeDtypeStruct((N_CHIPS, g.shape[1] // 2, g.shape[2]), g.dtype) for g in arrays]
    if kind == "reduce":
        return [jax.ShapeDtypeStruct((N_CHIPS, 2 * p.shape[1], p.shape[2]), p.dtype) for p in arrays]
    return [jax.ShapeDtypeStruct((N_CHIPS,) + s.shape, s.dtype) for s in arrays]


def exchange(arrays, kind, name):
    n = len(arrays)

    def body(*refs):
        start, forward, finish = _exchange_phases(refs[:n], refs[n:2 * n], *refs[2 * n:], kind)
        start()
        forward()
        finish()

    return pl.pallas_call(
        body, name=name, in_specs=_hbm_specs(n), out_specs=_hbm_specs(n),
        out_shape=_exchange_shapes(arrays, kind), scratch_shapes=_exchange_sems(n),
    )(*arrays)


def allreduce_small(p, name):
    rows, width = p.shape

    def body(p_ref, s_ref, gath, ssem, rsem):
        x, y, c = _place()
        me = 4 * x + 2 * y + c
        copies = []
        for d in range(1, N_DEV):
            dx, dy, dc = (d >> 2) & 1, (d >> 1) & 1, d & 1
            peer = (x ^ dx, y ^ dy, c ^ dc)
            cp = pltpu.make_async_remote_copy(src_ref=p_ref, dst_ref=gath.at[me], send_sem=ssem.at[d - 1],
                                              recv_sem=rsem.at[d - 1], device_id=peer, device_id_type=MESH)
            cp.start()
            copies.append(cp)
        gath[me] = p_ref[...]
        for d, cp in enumerate(copies, start=1):
            dx, dy, dc = (d >> 2) & 1, (d >> 1) & 1, d & 1
            peer_slot = 4 * (x ^ dx) + 2 * (y ^ dy) + (c ^ dc)
            pltpu.make_async_remote_copy(src_ref=p_ref, dst_ref=gath.at[peer_slot], send_sem=ssem.at[d - 1],
                                         recv_sem=rsem.at[d - 1], device_id=(x ^ dx, y ^ dy, c ^ dc),
                                         device_id_type=MESH).wait_recv()
        for cp in copies:
            cp.wait_send()
        acc = gath[0]
        for d in range(1, N_DEV):
            acc = acc + gath[d]
        s_ref[...] = acc

    return pl.pallas_call(
        body, name=name,
        in_specs=[pl.BlockSpec(memory_space=pltpu.VMEM)],
        out_specs=pl.BlockSpec(memory_space=pltpu.VMEM),
        out_shape=jax.ShapeDtypeStruct((rows, width), F32),
        scratch_shapes=[pltpu.VMEM((N_DEV, rows, width), F32), pltpu.SemaphoreType.DMA((N_DEV - 1,)),
                        pltpu.SemaphoreType.DMA((N_DEV - 1,))],
    )(p)


def add_my_half(g, recv, name):
    _, r, cols = g.shape
    hr = r // 2
    tb = _pick(hr, 512, 8)
    nb = hr // tb
    c_arr = lax.axis_index("c").astype(jnp.int32).reshape(1)

    def body(c_ref, g_ref, r_ref, o_ref):
        o_ref[...] = (g_ref[...] + r_ref[...]).astype(o_ref.dtype)

    return pl.pallas_call(
        body, name=name,
        grid_spec=pltpu.PrefetchScalarGridSpec(
            num_scalar_prefetch=1, grid=(N_CHIPS, nb),
            in_specs=[pl.BlockSpec((1, tb, cols), lambda k, i, c_ref: (k, c_ref[0] * nb + i, 0)),
                      pl.BlockSpec((1, tb, cols), lambda k, i, c_ref: (k, i, 0))],
            out_specs=pl.BlockSpec((1, tb, cols), lambda k, i, c_ref: (k, i, 0))),
        out_shape=jax.ShapeDtypeStruct((N_CHIPS, hr, cols), BF16),
        compiler_params=_params(("arbitrary", "arbitrary")),
    )(c_arr, g, recv)


def sum_chips(parts, name):
    _, hr, cols = parts.shape
    tb = _pick(hr, 512, 8)

    def body(p_ref, o_ref):
        o_ref[...] = ((p_ref[0].astype(F32) + p_ref[1].astype(F32)) + p_ref[2].astype(F32)) + p_ref[3].astype(F32)

    return pl.pallas_call(
        body, name=name, grid=(hr // tb,),
        in_specs=[pl.BlockSpec((N_CHIPS, tb, cols), lambda i: (0, i, 0))],
        out_specs=pl.BlockSpec((tb, cols), lambda i: (i, 0)),
        out_shape=jax.ShapeDtypeStruct((hr, cols), F32),
        compiler_params=_params(("arbitrary",)),
    )(parts)


def adamw(w, g, m, v, name):
    rows, width = w.shape
    tb = _pick(rows, 512, 8)

    def body(w_ref, g_ref, m_ref, v_ref, d_ref, mo_ref, vo_ref):
        gg = g_ref[...]
        m_new = ADAM_B1 * m_ref[...] + (1.0 - ADAM_B1) * gg
        v_new = ADAM_B2 * v_ref[...] + (1.0 - ADAM_B2) * (gg * gg)
        m_hat = m_new / (1.0 - ADAM_B1 ** ADAM_STEP)
        v_hat = v_new / (1.0 - ADAM_B2 ** ADAM_STEP)
        d_ref[...] = -ADAM_LR * (m_hat / (jnp.sqrt(v_hat) + ADAM_EPS) + ADAM_WD * w_ref[...])
        mo_ref[...] = m_new
        vo_ref[...] = v_new

    spec = pl.BlockSpec((tb, width), lambda i: (i, 0))
    return pl.pallas_call(
        body, name=name, grid=(rows // tb,), in_specs=[spec] * 4, out_specs=[spec] * 3,
        out_shape=[jax.ShapeDtypeStruct((rows, width), F32)] * 3,
        compiler_params=_params(("arbitrary",)),
    )(w, g, m, v)


def _pack_rows(vecs, width=1024):
    rows, cur, used = [], [], 0
    for v in vecs:
        n = v.shape[1]
        if used + n > width:
            cur.append(jnp.zeros((1, width - used), F32))
            rows.append(jnp.concatenate(cur, axis=1))
            cur, used = [], 0
        cur.append(v)
        used += n
    cur.append(jnp.zeros((1, width - used), F32))
    rows.append(jnp.concatenate(cur, axis=1))
    return rows


def _pad128(v):
    return jnp.pad(v, ((0, 0), (0, LANES - v.shape[1])))


def kernel(x, norm1_w, w_in, lb_logits, hg_norm_w, q_norm_w, k_norm_w, att_norm_w, w_out, norm2_w, w_gate_up, w_down, final_norm_w, loss_target, m_norm1_w, m_w_in, m_lb_logits, m_hg_norm_w, m_q_norm_w, m_k_norm_w, m_att_norm_w, m_w_out, m_norm2_w, m_w_gate_up, m_w_down, m_final_norm_w, v_norm1_w, v_w_in, v_lb_logits, v_hg_norm_w, v_q_norm_w, v_k_norm_w, v_att_norm_w, v_w_out, v_norm2_w, v_w_gate_up, v_w_down, v_final_norm_w):
    t, d = x.shape[1], x.shape[2]
    xi, yi, ci = _place()
    chip = 2 * xi + yi
    x2d = x.reshape(t, d)
    tgt = loss_target.reshape(t, d)
    tok = min(TOK, t)
    nt = t // tok
    ns = ATT_GROUP * nt

    late_w = [w_out[0].astype(BF16), w_gate_up[0].astype(BF16), w_down[0].astype(BF16)]
    lb_rows = lb_logits.reshape(4, LANES) * (ci == 0).astype(F32)
    lb_pad = lax.dynamic_update_slice(jnp.zeros((8, 1024), F32), lb_rows, (0, chip * LANES))
    lb_full = allreduce_small(lb_pad, "gather_lb")[:4, :HG_W]
    l_f0, l_f1, l_b0, l_b1 = (lb_full[i:i + 1] for i in range(4))

    n1 = norm1_w.reshape(1, d)
    n2 = norm2_w.reshape(1, d)
    nf = final_norm_w.reshape(1, d)
    tm = min(512, t)
    h1, g_in = _rows(lambda a, w: ((_rms(a, w),), ()), "norm1", t, tm, [_rin(x2d, tm)], [n1], [_rout(t, tm, d, BF16)],
                     ride=("gather", [w_in[0].astype(BF16)]))
    wf_in = g_in.transpose(1, 0, 2).reshape(g_in.shape[1], -1)
    u = mm_rows(h1, wf_in, "mm_in")
    o_f, ss_f = gla_fwd(u, l_f0, l_f1, 1, False, "gla_fwd_f")
    o_b, ss_b = gla_fwd(u, l_b0, l_b1, 2, True, "gla_fwd_b")

    cos2, sin2 = _rope_tables(t)
    ck, sk = jnp.tile(cos2, (1, ATT_KV)), jnp.tile(sin2, (1, ATT_KV))
    qw, kw = _pad128(q_norm_w.reshape(1, ATT_DH)), _pad128(k_norm_w.reshape(1, ATT_DH))
    gq, gk = _group_sum_matrix(ATT_QW), _group_sum_matrix(ATT_KVW)
    tq, tk = _tile_matrix(ATT_QW), _tile_matrix(ATT_KVW)
    prep_in = [_rin(u, tok, ATT_QW, 5), _rin(u, tok, ATT_KVW, 24), _rin(ck, tok), _rin(sk, tok)]
    prep_consts = [qw, kw, gq, gk, tq, tk]

    def att_prep_fn(aq, ak, av, *rest):
        q, k = _att_prep_fn(aq, ak, *rest)
        return (_head_t(q, ATT_HEADS), _head_s(q, ATT_HEADS), _head_s(k, ATT_KV), _head_t(k, ATT_KV),
                _head_t(av, ATT_KV)), ()

    q_t, q_s, k_s, k_t, v_t = _rows(
        att_prep_fn, "att_prep", t, tok, prep_in[:2] + [_rin(u, tok, ATT_KVW, 25)] + prep_in[2:], prep_consts,
        [_tout(ATT_HEADS, nt, ATT_DH, tok, BF16), _tout(ATT_HEADS, nt, tok, LANES, BF16),
         _tout(ATT_KV, nt, tok, LANES, BF16), _tout(ATT_KV, nt, ATT_DH, tok, BF16),
         _tout(ATT_KV, nt, ATT_DH, tok, BF16)])
    q_t = q_t.reshape(ATT_KV, ns, ATT_DH, tok)
    q_s = q_s.reshape(ATT_KV, ns, tok, LANES)
    k_s = k_s.reshape(ATT_KV, t, LANES)
    o_t, lse, g_out, g_gu, g_down = fa_fwd(q_t, k_s, v_t, ride=("gather", late_w))
    wf_gu = g_gu.transpose(1, 0, 2).reshape(g_gu.shape[1], -1)
    wf_out = g_out.reshape(-1, g_out.shape[2])
    wf_down = g_down.reshape(-1, g_down.shape[2])
    o_tiles = o_t.reshape(ATT_HEADS, nt, ATT_DH, tok)

    hgw = hg_norm_w.reshape(1, HG_D)
    attw = att_norm_w.reshape(1, ATT_QW)
    mix_in = [_rin(o_f, tok), _rin(o_b, tok), _rin(u, tok, HG_W, 4), _tin(o_tiles)]
    (mix,) = _rows(lambda of, ob, ug, ot, hw, aw: ((_post_mix_fn(of, ob, ug, _from_head_t(ot), hw, aw),), ()),
                   "post_mix", t, tok, mix_in, [hgw, attw], [_rout(t, tok, d, BF16)])
    x1, h2 = mm_res_norm(mix, wf_out, x2d, n2, "mm_out")
    gu2, act = ffn_up(h2, wf_gu)

    dx2, dx2b, g_final, loss_part = mm_res_loss(act, wf_down, x1, tgt, nf, "mm_down_loss")
    dgu2 = ffn_down_dx(dx2b, wf_down, gu2)
    gw_down = mm_tn(dx2b, act, "mm_down_dw", transpose_out=True)
    dx1, dx1b, g_norm2 = mm_nt_norm_bwd(dgu2, wf_gu, x1, n2, dx2, "mm_gate_up_dx", True)
    gw_gu = mm_tn(h2, dgu2, "mm_gate_up_dw", col_shards=N_CHIPS)
    dmix = mm_rows(dx1b, wf_out, "mm_out_dx", trans_b=True)
    gw_out = mm_tn(mix, dx1b, "mm_out_dw")

    def post_mix_bwd_fn(of, ob, ug, ot, dm, hgw_, attw_):
        _, vjp = jax.vjp(_post_mix_fn, of, ob, ug, _from_head_t(ot), hgw_, attw_)
        dof, _, dug, doa, dhgw, dattw = vjp(dm)
        return (dof, dug, _head_t(doa, ATT_HEADS), _head_s(doa, ATT_HEADS)), (dhgw, dattw)

    late = ["w_out", "w_gate_up", "w_down"]
    late_g = [gw_out.reshape(N_CHIPS, -1, d), gw_gu, gw_down.reshape(N_CHIPS, -1, d)]
    do_hg, du_g, do_t, do_s, g_hg, g_att, *late_recv = _rows(
        post_mix_bwd_fn, "post_mix_bwd", t, tok, mix_in + [_rin(dmix, tok)], [hgw, attw],
        [_rout(t, tok, HG_W, F32), _rout(t, tok, HG_W, BF16), _tout(ATT_HEADS, nt, ATT_DH, tok, BF16),
         _tout(ATT_HEADS, nt, tok, LANES, BF16)], [HG_D, ATT_QW], ride=("siblings", late_g))
    late_part = [add_my_half(g, r, "add_my_half_" + n) for g, r, n in zip(late_g, late_recv, late)]
    dq_t, dk_t, dv_t, *late_parts = fa_bwd(q_s, q_t, do_s.reshape(q_s.shape), do_t.reshape(q_t.shape), o_t, lse, k_s, k_t,
                                           v_t, ride=("reduce", late_part))

    dq_f, df_f, di_f, dl_f0, dl_f1 = gla_bwd(u, l_f0, l_f1, ss_f, do_hg, 1, False, "gla_bwd_f")
    dq_b, df_b, di_b, dl_b0, dl_b1 = gla_bwd(u, l_b0, l_b1, ss_b, do_hg, 2, True, "gla_bwd_b")

    def att_prep_bwd_fn(aq, ak, ck_, sk_, dqt, dkt, dvt, qf, qb, ff, fb, i_f, i_b, dg, qw_, kw_, gq_, gk_, tq_, tk_):
        _, vjp = jax.vjp(lambda a, b, c_, e: _att_prep_fn(a, b, ck_, sk_, c_, e, gq_, gk_, tq_, tk_),
                         aq, ak, qw_, kw_)
        daq, dak, dqw, dkw = vjp((_from_head_t(dqt), _from_head_t(dkt)))
        du_tile = jnp.concatenate([qf + qb, ff, fb, i_f + i_b, dg.astype(F32), daq, dak, _from_head_t(dvt)], axis=1)
        return (du_tile,), (dqw, dkw)

    du, g_q, g_k = _rows(
        att_prep_bwd_fn, "att_prep_bwd", t, tok,
        prep_in + [_tin(dq_t.reshape(ATT_HEADS, nt, ATT_DH, tok)), _tin(dk_t), _tin(dv_t)]
        + [_rin(a, tok) for a in (dq_f, dq_b, df_f, df_b, di_f, di_b, du_g)], prep_consts,
        [_rout(t, tok, u.shape[1], BF16)], [LANES, LANES])
    gw_in_t = mm_tn(h1, du, "mm_in_dw", transpose_out=True)

    names = ["w_in"] + late
    g_in4 = gw_in_t.reshape(N_CHIPS, -1, d)
    in_part = add_my_half(g_in4, exchange([g_in4], "siblings", "rs_siblings_w_in")[0], "add_my_half_w_in")
    grad_x, g_norm1, in_parts = mm_nt_norm_bwd(du, wf_in, x2d, n1, dx1, "mm_in_dx", False, ride=("reduce", [in_part]))
    parts = [in_parts] + late_parts
    g_shard = [sum_chips(p, "sum_chips_" + n) for p, n in zip(parts, names)]
    g_shard[0] = g_shard[0].T
    big = {}
    for n, g, w, m, v in zip(names, g_shard, (w_in, w_out, w_gate_up, w_down), (m_w_in, m_w_out, m_w_gate_up, m_w_down),
                             (v_w_in, v_w_out, v_w_gate_up, v_w_down)):
        dlt, mn, vn = adamw(w[0], g, m[0], v[0], "adamw_" + n)
        big[n] = (g[None], dlt[None], mn[None], vn[None])

    small = [g_norm1, g_norm2, g_final, g_att, g_hg, g_q, g_k, loss_part, dl_f0, dl_f1, dl_b0, dl_b1]
    packed = _pack_rows(small)
    packed += [jnp.zeros((1, 1024), F32)] * (8 - len(packed))
    tot = allreduce_small(jnp.concatenate(packed, axis=0), "allreduce_small")
    s_norm1, s_norm2, s_final = tot[0:1], tot[1:2], tot[2:3]
    s_att, s_hg, s_q, s_k = tot[3:4, 0:512], tot[3:4, 512:640], tot[3:4, 640:704], tot[3:4, 768:832]
    loss = tot[3, 896]
    s_lb = jnp.concatenate([tot[4:5, 0:512], tot[4:5, 512:1024], tot[5:6, 0:512], tot[5:6, 512:1024]], axis=0)
    s_lb = lax.dynamic_slice(s_lb, (0, chip * LANES), (4, LANES)).reshape(1, 512)

    snames = ["norm1_w", "lb_logits", "hg_norm_w", "q_norm_w", "k_norm_w", "att_norm_w", "norm2_w", "final_norm_w"]
    g_small = dict(zip(snames, [s_norm1, s_lb, s_hg, s_q, s_k, s_att, s_norm2, s_final]))
    w_small = dict(zip(snames, [norm1_w, lb_logits, hg_norm_w, q_norm_w, k_norm_w, att_norm_w, norm2_w, final_norm_w]))
    m_small = dict(zip(snames, [m_norm1_w, m_lb_logits, m_hg_norm_w, m_q_norm_w, m_k_norm_w, m_att_norm_w, m_norm2_w, m_final_norm_w]))
    v_small = dict(zip(snames, [v_norm1_w, v_lb_logits, v_hg_norm_w, v_q_norm_w, v_k_norm_w, v_att_norm_w, v_norm2_w, v_final_norm_w]))

    def pack_small(tree):
        rows = _pack_rows([tree[n].reshape(1, -1) for n in snames])
        rows += [jnp.zeros((1, 1024), F32)] * (8 - len(rows))
        return jnp.concatenate(rows, axis=0)

    d_s, m_s, v_s = adamw(pack_small(w_small), pack_small(g_small), pack_small(m_small), pack_small(v_small), "adamw_small")

    def unpack_small(a):
        out, r, used = {}, 0, 0
        for n in snames:
            size = w_small[n].size
            if used + size > 1024:
                r, used = r + 1, 0
            out[n] = a[r, used:used + size].reshape(w_small[n].shape)
            used += size
        return out

    d_sm, m_sm, v_sm = unpack_small(d_s), unpack_small(m_s), unpack_small(v_s)
    g_sm = {n: g_small[n].reshape(w_small[n].shape) for n in snames}

    order = ["norm1_w", "w_in", "lb_logits", "hg_norm_w", "q_norm_w", "k_norm_w", "att_norm_w", "w_out", "norm2_w",
             "w_gate_up", "w_down", "final_norm_w"]

    def pick(small_tree, idx):
        return [big[n][idx] if n in big else small_tree[n] for n in order]

    return (loss, grad_x.reshape(x.shape), *pick(g_sm, 0), *pick(d_sm, 1), *pick(m_sm, 2), *pick(v_sm, 3))
```

```python
import functools

import numpy as np
import jax
import jax.numpy as jnp
from jax import lax
from jax.experimental import pallas as pl
from jax.experimental.pallas import tpu as pltpu

F32 = jnp.float32
BF16 = jnp.bfloat16
MESH = pl.DeviceIdType.MESH

EPS = 1e-6
GRID_W = 64
HG_HEADS = 4
HG_D = 128
HG_W = HG_HEADS * HG_D
CHUNK = 64
ATT_HEADS = 8
ATT_KV = 2
ATT_GROUP = ATT_HEADS // ATT_KV
ATT_DH = 64
ATT_QW = ATT_HEADS * ATT_DH
ATT_KVW = ATT_KV * ATT_DH
ROPE_THETA = 10000.0
N_CHIPS = 4
N_DEV = 8

ADAM_LR = 0.001
ADAM_B1 = 0.9
ADAM_B2 = 0.999
ADAM_EPS = 1e-08
ADAM_WD = 0.01
ADAM_STEP = 10

VMEM_LIMIT = 52 * 1024 * 1024
LANES = 128
TOK = 256


def _params(sem=None):
    return pltpu.CompilerParams(dimension_semantics=sem, vmem_limit_bytes=VMEM_LIMIT)


def _dg(a, b, ca, cb):
    return lax.dot_general(a.astype(BF16), b.astype(BF16), (((ca,), (cb,)), ((), ())),
                           preferred_element_type=F32)


@jax.custom_vjp
def dot_nn(a, b):
    return _dg(a, b, 1, 0)


def _dot_nn_fwd(a, b):
    return _dg(a, b, 1, 0), (a, b)


def _dot_nn_bwd(res, g):
    a, b = res
    return _dg(g, b, 1, 1), _dg(a, g, 0, 0)


dot_nn.defvjp(_dot_nn_fwd, _dot_nn_bwd)


@jax.custom_vjp
def dot_nt(a, b):
    return _dg(a, b, 1, 1)


def _dot_nt_fwd(a, b):
    return _dg(a, b, 1, 1), (a, b)


def _dot_nt_bwd(res, g):
    a, b = res
    return _dg(g, b, 1, 0), _dg(g, a, 0, 0)


dot_nt.defvjp(_dot_nt_fwd, _dot_nt_bwd)


@jax.custom_vjp
def dot_tn(a, b):
    return _dg(a, b, 0, 0)


def _dot_tn_fwd(a, b):
    return _dg(a, b, 0, 0), (a, b)


def _dot_tn_bwd(res, g):
    a, b = res
    return _dg(b, g, 1, 1), _dg(a, g, 1, 0)


dot_tn.defvjp(_dot_tn_fwd, _dot_tn_bwd)


def _split3(a):
    hi = a.astype(BF16)
    r1 = a - hi.astype(F32)
    mid = r1.astype(BF16)
    lo = (r1 - mid.astype(F32)).astype(BF16)
    return lo, mid, hi


def _sum3(terms):
    lo, mid, hi = terms
    return (lo + mid) + hi


@jax.custom_vjp
def xdot_r(a, m):
    return _sum3([_dg(p, m, 1, 0) for p in _split3(a)])


def _xdot_r_fwd(a, m):
    return xdot_r(a, m), m


def _xdot_r_bwd(m, g):
    return _sum3([_dg(p, m, 1, 1) for p in _split3(g)]), jnp.zeros_like(m)


xdot_r.defvjp(_xdot_r_fwd, _xdot_r_bwd)


@jax.custom_vjp
def xdot_l(m, a):
    return _sum3([_dg(m, p, 1, 0) for p in _split3(a)])


def _xdot_l_fwd(m, a):
    return xdot_l(m, a), m


def _xdot_l_bwd(m, g):
    return jnp.zeros_like(m), _sum3([_dg(m, p, 0, 0) for p in _split3(g)])


xdot_l.defvjp(_xdot_l_fwd, _xdot_l_bwd)


@jax.custom_vjp
def swap_pairs(y):
    n = y.shape[-1]
    lane = lax.broadcasted_iota(jnp.int32, y.shape, 1)
    nxt = pltpu.roll(y, n - 1, 1)
    prv = pltpu.roll(y, 1, 1)
    return jnp.where(lane % 2 == 0, nxt, prv)


def _swap_fwd(y):
    return swap_pairs(y), None


def _swap_bwd(_, g):
    return (swap_pairs(g),)


swap_pairs.defvjp(_swap_fwd, _swap_bwd)


def _rms(x, w):
    return x * lax.rsqrt(jnp.mean(x * x, axis=-1, keepdims=True) + EPS) * w


def _sigmoid(x):
    return jax.nn.sigmoid(x)


def _rows(fn, name, t, tm, ins, consts, outs, accs=(), ride=None):
    n_r, n_c, n_o, n_a = len(ins), len(consts), len(outs), len(accs)

    def body(*refs):
        r = refs[:n_r]
        c = refs[n_r:n_r + n_c]
        o = refs[n_r + n_c:n_r + n_c + n_o]
        a = refs[n_r + n_c + n_o:]
        ro, ao = fn(*[x[...] for x in r], *[x[...] for x in c])
        for ref, val in zip(o, ro):
            if isinstance(val, (list, tuple)):
                for h, piece in enumerate(val):
                    ref[h, 0] = piece.astype(ref.dtype)
            else:
                ref[...] = val.astype(ref.dtype)
        if n_a:
            @pl.when(pl.program_id(0) == 0)
            def _():
                for ref in a:
                    ref[...] = jnp.zeros(ref.shape, F32)
            for ref, val in zip(a, ao):
                ref[...] += val

    in_specs = [s for _, s in ins]
    in_specs += [pl.BlockSpec(c.shape, lambda i, nd=c.ndim: (0,) * nd) for c in consts]
    out_specs = [s for _, s in outs] + [pl.BlockSpec((1, w), lambda i: (0, 0)) for w in accs]
    out_shape = [s for s, _ in outs] + [jax.ShapeDtypeStruct((1, w), F32) for w in accs]
    grid = (t // tm,)
    body, r_in, r_out, r_shapes, r_sems, r_args = _riding(body, n_r + n_c, n_o + n_a, grid, ride)
    return pl.pallas_call(
        body, name=name, grid=grid, in_specs=in_specs + r_in, out_specs=out_specs + r_out,
        out_shape=out_shape + r_shapes, scratch_shapes=r_sems, compiler_params=_params(("arbitrary",)),
    )(*[a for a, _ in ins], *consts, *r_args)


def _rin(a, tm, width=None, cb=0):
    width = a.shape[1] if width is None else width
    return a, pl.BlockSpec((tm, width), lambda i, cb=cb: (i, cb))


def _rout(t, tm, width, dtype):
    return jax.ShapeDtypeStruct((t, width), dtype), pl.BlockSpec((tm, width), lambda i: (i, 0))


def _tin(a):
    return a, pl.BlockSpec((a.shape[0], 1) + a.shape[2:], lambda i: (0, i, 0, 0))


def _tout(heads, nt, r, c, dtype):
    return jax.ShapeDtypeStruct((heads, nt, r, c), dtype), pl.BlockSpec((heads, 1, r, c), lambda i: (0, i, 0, 0))


def _pick(n, cap, mult):
    best = None
    for d in range(mult, min(n, cap) + 1, mult):
        if n % d == 0:
            best = d
    return best if best is not None else n


def mm_rows(a, b, name, trans_b=False, res=None, out_dtype=F32):
    m, k = a.shape
    n = b.shape[0] if trans_b else b.shape[1]
    tn = _pick(n, 3328, LANES)
    tm = _pick(m, 512 if k <= 3072 else 256, 8)
    has_res = res is not None

    def body(*refs):
        if has_res:
            a_ref, b_ref, r_ref, o_ref = refs
        else:
            a_ref, b_ref, o_ref = refs
        acc = _dg(a_ref[...], b_ref[...], 1, 1 if trans_b else 0)
        if has_res:
            acc = acc + r_ref[...]
        o_ref[...] = acc.astype(o_ref.dtype)

    in_specs = [pl.BlockSpec((tm, k), lambda j, i: (i, 0))]
    if trans_b:
        in_specs.append(pl.BlockSpec((tn, k), lambda j, i: (j, 0)))
    else:
        in_specs.append(pl.BlockSpec((k, tn), lambda j, i: (0, j)))
    args = [a, b]
    if has_res:
        in_specs.append(pl.BlockSpec((tm, tn), lambda j, i: (i, j)))
        args.append(res)
    return pl.pallas_call(
        body, name=name, grid=(n // tn, m // tm), in_specs=in_specs,
        out_specs=pl.BlockSpec((tm, tn), lambda j, i: (i, j)),
        out_shape=jax.ShapeDtypeStruct((m, n), out_dtype),
        compiler_params=_params(("arbitrary", "arbitrary")),
    )(*args)


def mm_tn(a, b, name, col_shards=None, transpose_out=False):
    t, m = a.shape
    parts = b.shape[0] if b.ndim == 3 else 1
    n = parts * b.shape[-1]
    tm = _pick(m, 512, LANES)
    width = n // col_shards if col_shards else None
    group = max(1, 3328 // width) if col_shards else 1
    tn = width * group if col_shards else _pick(n, 3328, LANES)
    tk = _pick(t, 1024, 8)
    nkk = t // tk
    per_part = b.shape[-1] // tn

    def body(a_ref, b_ref, o_ref, *scratch):
        acc_ref = scratch[0] if transpose_out else o_ref

        @pl.when(pl.program_id(2) == 0)
        def _():
            acc_ref[...] = jnp.zeros(acc_ref.shape, F32)
        acc = _dg(a_ref[...], b_ref[0] if b.ndim == 3 else b_ref[...], 0, 0)
        if col_shards:
            for s in range(group):
                acc_ref[s] += acc[:, s * width:(s + 1) * width]
        else:
            acc_ref[...] += acc
        if transpose_out:
            @pl.when(pl.program_id(2) == nkk - 1)
            def _():
                o_ref[...] = acc_ref[...].T

    if col_shards:
        out_spec = pl.BlockSpec((group, tm, width), lambda i, j, kk: (j, i, 0))
        out_shape = jax.ShapeDtypeStruct((col_shards, m, width), F32)
    elif transpose_out:
        out_spec = pl.BlockSpec((tn, tm), lambda i, j, kk: (j, i))
        out_shape = jax.ShapeDtypeStruct((n, m), F32)
    else:
        out_spec = pl.BlockSpec((tm, tn), lambda i, j, kk: (i, j))
        out_shape = jax.ShapeDtypeStruct((m, n), F32)
    if b.ndim == 3:
        b_spec = pl.BlockSpec((1, tk, tn), lambda i, j, kk: (j // per_part, kk, j % per_part))
    else:
        b_spec = pl.BlockSpec((tk, tn), lambda i, j, kk: (kk, j))
    return pl.pallas_call(
        body, name=name, grid=(m // tm, n // tn, nkk),
        in_specs=[pl.BlockSpec((tk, tm), lambda i, j, kk: (kk, i)), b_spec],
        out_specs=out_spec, out_shape=out_shape,
        scratch_shapes=[pltpu.VMEM((tm, tn), F32)] if transpose_out else [],
        compiler_params=_params(("arbitrary", "arbitrary", "arbitrary")),
    )(a, b)


def ffn_up(h2, w_gu):
    t, k = h2.shape
    dff = w_gu.shape[1] // 2
    tn = _pick(dff, 1408, LANES)
    ncol = dff // tn
    tm = _pick(t, 512, 8)

    def body(a_ref, wg_ref, wu_ref, gu_ref, act_ref):
        a = a_ref[...]
        g = _dg(a, wg_ref[...], 1, 0)
        u = _dg(a, wu_ref[...], 1, 0)
        gu_ref[0] = g.astype(gu_ref.dtype)
        gu_ref[1] = u.astype(gu_ref.dtype)
        act_ref[...] = _swiglu_fn(g, u).astype(act_ref.dtype)

    return pl.pallas_call(
        body, name="ffn_up", grid=(ncol, t // tm),
        in_specs=[pl.BlockSpec((tm, k), lambda s, i: (i, 0)), pl.BlockSpec((k, tn), lambda s, i: (0, s)),
                  pl.BlockSpec((k, tn), lambda s, i: (0, s + ncol))],
        out_specs=[pl.BlockSpec((2, tm, tn), lambda s, i: (0, i, s)), pl.BlockSpec((tm, tn), lambda s, i: (i, s))],
        out_shape=[jax.ShapeDtypeStruct((2, t, dff), BF16), jax.ShapeDtypeStruct((t, dff), BF16)],
        compiler_params=_params(("arbitrary", "arbitrary")),
    )(h2, w_gu, w_gu)


def ffn_down_dx(dx2b, w_down, gu2):
    t, k = dx2b.shape
    dff = w_down.shape[0]
    tn = _pick(dff, 1408, LANES)
    tm = _pick(t, 512, 8)

    def body(a_ref, w_ref, gu_ref, o_ref):
        pieces = [slice(0, tm // 2), slice(tm // 2, tm)]
        dacts = [_dg(a_ref[rows, :], w_ref[...], 1, 1) for rows in pieces]
        for rows, dact in zip(pieces, dacts):
            _, vjp = jax.vjp(_swiglu_fn, gu_ref[0, rows, :].astype(F32), gu_ref[1, rows, :].astype(F32))
            dg, du = vjp(dact)
            o_ref[0, rows, :] = dg.astype(o_ref.dtype)
            o_ref[1, rows, :] = du.astype(o_ref.dtype)

    blk = pl.BlockSpec((2, tm, tn), lambda s, i: (0, i, s))
    return pl.pallas_call(
        body, name="ffn_down_dx", grid=(dff // tn, t // tm),
        in_specs=[pl.BlockSpec((tm, k), lambda s, i: (i, 0)), pl.BlockSpec((tn, k), lambda s, i: (s, 0)), blk],
        out_specs=blk, out_shape=jax.ShapeDtypeStruct((2, t, dff), BF16),
        compiler_params=_params(("arbitrary", "arbitrary")),
    )(dx2b, w_down, gu2)


def mm_res_norm(a, b, res, w, name):
    t, k = a.shape
    d = b.shape[1]
    tm = _pick(t, 512, 8)

    def body(a_ref, b_ref, r_ref, w_ref, x_ref, h_ref):
        xv = _dg(a_ref[...], b_ref[...], 1, 0) + r_ref[...]
        x_ref[...] = xv
        h_ref[...] = _rms(xv, w_ref[...]).astype(h_ref.dtype)

    row = pl.BlockSpec((tm, d), lambda i: (i, 0))
    return pl.pallas_call(
        body, name=name, grid=(t // tm,),
        in_specs=[pl.BlockSpec((tm, k), lambda i: (i, 0)), pl.BlockSpec((k, d), lambda i: (0, 0)), row,
                  pl.BlockSpec((1, d), lambda i: (0, 0))],
        out_specs=[row, row],
        out_shape=[jax.ShapeDtypeStruct((t, d), F32), jax.ShapeDtypeStruct((t, d), BF16)],
        compiler_params=_params(("arbitrary",)),
    )(a, b, res, w)


def mm_res_loss(a, b, res, tgt, w, name):
    t, k = a.shape
    d = b.shape[1]
    tm = _pick(t, 512, 16)

    def body(a_ref, b_ref, r_ref, t_ref, w_ref, dx_ref, dxb_ref, dw_ref, loss_ref):
        @pl.when(pl.program_id(0) == 0)
        def _():
            dw_ref[...] = jnp.zeros(dw_ref.shape, F32)
            loss_ref[...] = jnp.zeros(loss_ref.shape, F32)
        pieces = [slice(0, tm // 2), slice(tm // 2, tm)]
        accs = [_dg(a_ref[rows, :], b_ref[...], 1, 0) for rows in pieces]
        for rows, acc in zip(pieces, accs):
            tg = t_ref[rows, :]
            val, vjp = jax.vjp(lambda x_, w_: _loss_fn(x_, w_, tg), acc + r_ref[rows, :], w_ref[...])
            dx, dw = vjp(jnp.ones((1, 1), F32))
            dx_ref[rows, :] = dx
            dxb_ref[rows, :] = dx.astype(dxb_ref.dtype)
            dw_ref[...] += dw
            loss_ref[...] += jnp.broadcast_to(val, (1, LANES))

    row = pl.BlockSpec((tm, d), lambda i: (i, 0))
    vec = pl.BlockSpec((1, d), lambda i: (0, 0))
    return pl.pallas_call(
        body, name=name, grid=(t // tm,),
        in_specs=[pl.BlockSpec((tm, k), lambda i: (i, 0)), pl.BlockSpec((k, d), lambda i: (0, 0)), row, row, vec],
        out_specs=[row, row, vec, pl.BlockSpec((1, LANES), lambda i: (0, 0))],
        out_shape=[jax.ShapeDtypeStruct((t, d), F32), jax.ShapeDtypeStruct((t, d), BF16),
                   jax.ShapeDtypeStruct((1, d), F32), jax.ShapeDtypeStruct((1, LANES), F32)],
        compiler_params=_params(("arbitrary",)),
    )(a, b, res, tgt, w)


def mm_nt_norm_bwd(a, wmat, x, w, dres, name, with_bf16, ride=None):
    parts = a.shape[0] if a.ndim == 3 else 1
    t, kp = a.shape[-2], a.shape[-1]
    d = wmat.shape[0]
    tm = _pick(t, 256, 8)

    n_out = 3 if with_bf16 else 2

    def body(*refs):
        a_ref, w_refs = refs[0], refs[1:1 + parts]
        x_ref, nw_ref, r_ref, dx_ref = refs[1 + parts:5 + parts]
        dw_ref = refs[4 + parts + n_out - 1]

        @pl.when(pl.program_id(0) == 0)
        def _():
            dw_ref[...] = jnp.zeros(dw_ref.shape, F32)
        dh = None
        for p in range(parts):
            term = _dg(a_ref[p] if a.ndim == 3 else a_ref[...], w_refs[p][...], 1, 1)
            dh = term if dh is None else dh + term
        _, vjp = jax.vjp(_rms, x_ref[...], nw_ref[...])
        dx, dw = vjp(dh)
        dx = dx + r_ref[...]
        dx_ref[...] = dx
        if with_bf16:
            refs[5 + parts][...] = dx.astype(BF16)
        dw_ref[...] += dw

    row = pl.BlockSpec((tm, d), lambda i: (i, 0))
    vec = pl.BlockSpec((1, d), lambda i: (0, 0))
    a_spec = pl.BlockSpec((parts, tm, kp), lambda i: (0, i, 0)) if a.ndim == 3 else pl.BlockSpec((tm, kp), lambda i: (i, 0))
    w_specs = [pl.BlockSpec((d, kp), lambda i, p=p: (0, p)) for p in range(parts)]
    outs = [row] + ([row] if with_bf16 else []) + [vec]
    shapes = [jax.ShapeDtypeStruct((t, d), F32)] + ([jax.ShapeDtypeStruct((t, d), BF16)] if with_bf16 else [])
    grid = (t // tm,)
    body, r_in, r_out, r_shapes, r_sems, r_args = _riding(body, 4 + parts, n_out, grid, ride)
    return pl.pallas_call(
        body, name=name, grid=grid,
        in_specs=[a_spec] + w_specs + [row, vec, row] + r_in,
        out_specs=outs + r_out, out_shape=shapes + [jax.ShapeDtypeStruct((1, d), F32)] + r_shapes,
        scratch_shapes=r_sems, compiler_params=_params(("arbitrary",)),
    )(a, *([wmat] * parts), x, w, dres, *r_args)


def _lower_bound(l0, l1):
    m = jnp.maximum(l0, l1)
    e0 = jnp.exp(l0 - m)
    e1 = jnp.exp(l1 - m)
    return e0 / (e0 + e1)


def _gla_consts(rev):
    ri = lax.broadcasted_iota(jnp.int32, (CHUNK, CHUNK), 0)
    ci = lax.broadcasted_iota(jnp.int32, (CHUNK, CHUNK), 1)
    keep = (ci >= ri) if rev else (ci <= ri)
    ref_mask = lax.broadcasted_iota(jnp.int32, (CHUNK, 1), 0) == (CHUNK // 2 if rev else CHUNK // 2 - 1)
    return keep, ref_mask


def _gla_block(uq, uf, ui, l0, l1, st_in, rev):
    ncb = uq.shape[0] // CHUNK
    heads = range(HG_HEADS)
    keep, ref_mask = _gla_consts(rev)
    cum = keep.astype(F32)
    lb = _lower_bound(l0, l1)
    q = uq * _sigmoid(uq)
    k = (1.0 - lb) * _sigmoid(-uf)
    g = jnp.log(lb + (1.0 - lb) * _sigmoid(uf))

    def rows(a, c):
        return a[c * CHUNK:(c + 1) * CHUNK]

    def head(a, h):
        return a[:, h * HG_D:(h + 1) * HG_D]

    bs = [xdot_l(cum, rows(g, c)) for c in range(ncb)]
    q_in, k_in, q_b, k_d, decay = [], [], [], [], []
    for c in range(ncb):
        b = bs[c]
        b_ref = jnp.sum(jnp.where(ref_mask, b, 0.0), axis=0, keepdims=True)
        b_last = jnp.sum(rows(g, c), axis=0, keepdims=True)
        qc, kc = rows(q, c), rows(k, c)
        q_in.append(qc * jnp.exp(b - b_ref))
        k_in.append(kc * jnp.exp(b_ref - b))
        q_b.append(q_in[c] * jnp.exp(b_ref))
        k_d.append(k_in[c] * jnp.exp(b_last - b_ref))
        decay.append(jnp.exp(b_last))
    scores = [[jnp.where(keep, dot_nt(head(q_in[c], h), head(k_in[c], h)), 0.0) for h in heads] for c in range(ncb)]
    o_intra = [[dot_nn(scores[c][h], head(rows(ui, c), h)) for h in heads] for c in range(ncb)]
    contrib = [[dot_tn(head(rows(ui, c), h), head(k_d[c], h)) for h in heads] for c in range(ncb)]
    st = list(st_in)
    o_rows = [None] * ncb
    for c in (reversed(range(ncb)) if rev else range(ncb)):
        parts = []
        for h in heads:
            parts.append(o_intra[c][h] + dot_nt(head(q_b[c], h), st[h]))
            st[h] = st[h] * head(decay[c], h) + contrib[c][h]
        o_rows[c] = jnp.concatenate(parts, axis=1)
    return jnp.concatenate(o_rows, axis=0), tuple(st)


def _gla_blocks(t):
    tb = min(512, t)
    return tb, t // tb


def gla_fwd(u, l0, l1, fcol, rev, name):
    t = u.shape[0]
    tb, nb = _gla_blocks(t)

    def blk(i):
        return (nb - 1 - i) if rev else i

    def body(uq_ref, uf_ref, ui_ref, l0_ref, l1_ref, o_ref, ss_ref, st_ref):
        @pl.when(pl.program_id(0) == 0)
        def _():
            st_ref[...] = jnp.zeros(st_ref.shape, F32)
        ss_ref[0] = st_ref[...]
        o, st_out = _gla_block(uq_ref[...], uf_ref[...], ui_ref[...], l0_ref[...], l1_ref[...],
                               tuple(st_ref[h] for h in range(HG_HEADS)), rev)
        o_ref[...] = o
        for h in range(HG_HEADS):
            st_ref[h] = st_out[h]

    row = lambda cb: pl.BlockSpec((tb, HG_W), lambda i: (blk(i), cb))
    vec = pl.BlockSpec((1, HG_W), lambda i: (0, 0))
    return pl.pallas_call(
        body, name=name, grid=(nb,),
        in_specs=[row(0), row(fcol), row(3), vec, vec],
        out_specs=[pl.BlockSpec((tb, HG_W), lambda i: (blk(i), 0)),
                   pl.BlockSpec((1, HG_HEADS, HG_D, HG_D), lambda i: (blk(i), 0, 0, 0))],
        out_shape=[jax.ShapeDtypeStruct((t, HG_W), F32),
                   jax.ShapeDtypeStruct((nb, HG_HEADS, HG_D, HG_D), F32)],
        scratch_shapes=[pltpu.VMEM((HG_HEADS, HG_D, HG_D), F32)],
        compiler_params=_params(("arbitrary",)),
    )(u, u, u, l0, l1)


def gla_bwd(u, l0, l1, ss, do, fcol, rev, name):
    t = u.shape[0]
    tb, nb = _gla_blocks(t)

    def blk(i):
        return i if rev else (nb - 1 - i)

    def body(uq_ref, uf_ref, ui_ref, l0_ref, l1_ref, ss_ref, do_ref,
             dq_ref, df_ref, di_ref, dl0_ref, dl1_ref, dst_ref):
        @pl.when(pl.program_id(0) == 0)
        def _():
            dst_ref[...] = jnp.zeros(dst_ref.shape, F32)
            dl0_ref[...] = jnp.zeros(dl0_ref.shape, F32)
            dl1_ref[...] = jnp.zeros(dl1_ref.shape, F32)
        heads = range(HG_HEADS)
        _, vjp = jax.vjp(functools.partial(_gla_block, rev=rev), uq_ref[...], uf_ref[...], ui_ref[...],
                         l0_ref[...], l1_ref[...], tuple(ss_ref[0, h] for h in heads))
        dq, df, di, dl0, dl1, dst = vjp((do_ref[...], tuple(dst_ref[h] for h in heads)))
        dq_ref[...] = dq
        df_ref[...] = df
        di_ref[...] = di
        dl0_ref[...] += dl0
        dl1_ref[...] += dl1
        for h in heads:
            dst_ref[h] = dst[h]

    row = lambda cb: pl.BlockSpec((tb, HG_W), lambda i: (blk(i), cb))
    vec = pl.BlockSpec((1, HG_W), lambda i: (0, 0))
    orow = pl.BlockSpec((tb, HG_W), lambda i: (blk(i), 0))
    return pl.pallas_call(
        body, name=name, grid=(nb,),
        in_specs=[row(0), row(fcol), row(3), vec, vec,
                  pl.BlockSpec((1, HG_HEADS, HG_D, HG_D), lambda i: (blk(i), 0, 0, 0)), orow],
        out_specs=[orow, orow, orow, vec, vec],
        out_shape=[jax.ShapeDtypeStruct((t, HG_W), F32)] * 3 + [jax.ShapeDtypeStruct((1, HG_W), F32)] * 2,
        scratch_shapes=[pltpu.VMEM((HG_HEADS, HG_D, HG_D), F32)],
        compiler_params=_params(("arbitrary",)),
    )(u, u, u, l0, l1, ss, do)


def _rope_tables(t):
    rows = t // GRID_W
    row = jnp.repeat(jnp.arange(rows), GRID_W).astype(F32)
    col = jnp.tile(jnp.arange(GRID_W), rows).astype(F32)
    axis_dim = ATT_DH // 2
    freqs = ROPE_THETA ** (-jnp.arange(0, axis_dim, 2, dtype=F32) / axis_dim)
    ang = jnp.concatenate([row[:, None] * freqs, col[:, None] * freqs], axis=-1)
    cos2 = jnp.repeat(jnp.cos(ang), 2, axis=-1)
    sin2 = jnp.repeat(jnp.sin(ang), 2, axis=-1) * jnp.tile(jnp.array([-1.0, 1.0], F32), ATT_DH // 2)
    return cos2, sin2


def _group_sum_matrix(width):
    idx = np.arange(width) // ATT_DH
    return jnp.asarray((idx[:, None] == idx[None, :]).astype(np.float32))


def _tile_matrix(width):
    m = np.zeros((LANES, width), np.float32)
    m[np.arange(width) % ATT_DH, np.arange(width)] = 1.0
    return jnp.asarray(m)


def _tile_w(w128, tile_m):
    w8 = jnp.broadcast_to(w128, (8, LANES))
    return jnp.sum(xdot_r(w8, tile_m), axis=0, keepdims=True) * 0.125


def _head_norm_rope(a, w128, cos_t, sin_t, gsum, tile_m, scale):
    ssq = xdot_r(a * a, gsum)
    y = a * lax.rsqrt(ssq * (1.0 / ATT_DH) + EPS) * _tile_w(w128, tile_m)
    return (y * cos_t + swap_pairs(y) * sin_t) * scale


def _att_prep_fn(aq, ak, ck, sk, qw, kw, gq, gk, tq, tk):
    reps = aq.shape[1] // ck.shape[1]
    cq, sq = jnp.concatenate([ck] * reps, axis=1), jnp.concatenate([sk] * reps, axis=1)
    q = _head_norm_rope(aq, qw, cq, sq, gq, tq, ATT_DH ** -0.5)
    k = _head_norm_rope(ak, kw, ck, sk, gk, tk, 1.0)
    return q, k


def _head_t(x, heads):
    xt = x.T
    return [xt[h * ATT_DH:(h + 1) * ATT_DH] for h in range(heads)]


def _head_s(x, heads):
    lane = lax.broadcasted_iota(jnp.int32, (x.shape[0], LANES), 1)
    out = []
    for h in range(heads):
        pair = x[:, (h // 2) * LANES:(h // 2 + 1) * LANES]
        if h % 2:
            pair = pltpu.roll(pair, ATT_DH, 1)
        out.append(jnp.where(lane < ATT_DH, pair, 0.0))
    return out


def _from_head_t(tile):
    return jnp.concatenate([tile[h, 0] for h in range(tile.shape[0])], axis=0).T


def _pad_rows(a):
    return jnp.concatenate([a, jnp.zeros(a.shape, a.dtype)], axis=0)


def _col_bcast(row_vec):
    return jnp.broadcast_to(row_vec, (LANES, row_vec.shape[1])).T


FA_K_FWD = 512
FA_K_BWD = 512
FA_STRIPS_FWD = 8
FA_STRIPS_BWD = 8


def _key_block(ref, j, tiles):
    return jnp.concatenate([ref[0, j * tiles + i] for i in range(tiles)], axis=1)


def _riding(body, n_in, n_out, grid, ride):
    if ride is None:
        return body, [], [], [], [], []
    kind, arrays = ride
    n = len(arrays)
    steps = int(np.prod(grid))

    def riding_body(*refs):
        cuts = np.cumsum([0, n_in, n, n_out, n]).tolist()
        own_in, rin, own_out, rout = (refs[a:b] for a, b in zip(cuts[:-1], cuts[1:]))
        start, forward, finish = _exchange_phases(rin, rout, *refs[cuts[-1]:], kind)
        step = 0
        for axis, size in enumerate(grid):
            step = step * size + pl.program_id(axis)
        pl.when(step == 0)(start)
        pl.when(step == steps // 2)(forward)
        body(*own_in, *own_out)
        pl.when(step == steps - 1)(finish)

    return riding_body, _hbm_specs(n), _hbm_specs(n), _exchange_shapes(arrays, kind), _exchange_sems(n), list(arrays)


def fa_fwd(qt, ks, vt, ride=None):
    _, ns, dh, nq = qt.shape
    lk = ks.shape[1]
    bk = min(FA_K_FWD, lk)
    nk = lk // bk
    spg = min(FA_STRIPS_FWD, ns)
    grid = (ATT_KV, ns // spg)

    def body(q_ref, k_ref, v_ref, o_ref, lse_ref):
        qs = [_pad_rows(q_ref[0, c]) for c in range(spg)]

        def keys(j):
            return k_ref[0, j * bk:(j + 1) * bk, :]

        def step(j, carry):
            st0, stats = carry
            kb = keys(j)
            vb = _key_block(v_ref, j, bk // nq)
            sts = [st0] + [_dg(kb, qs[c], 1, 0) for c in range(1, spg)]
            out = []
            for c in range(spg):
                m, l, acc = stats[c]
                m_new = jnp.maximum(m, jnp.max(sts[c], axis=0, keepdims=True))
                alpha = jnp.exp(m - m_new)
                p = jnp.exp(sts[c] - m_new)
                l = alpha * l + jnp.sum(p, axis=0, keepdims=True)
                if c == spg - 1:
                    st0 = _dg(keys(min(j + 1, nk - 1)), qs[0], 1, 0)
                acc = alpha * acc + _dg(vb, p, 1, 0)
                out.append((m_new, l, acc))
            return st0, tuple(out)

        init = tuple((jnp.full((1, nq), -jnp.inf, F32), jnp.zeros((1, nq), F32), jnp.zeros((dh, nq), F32))
                     for _ in range(spg))
        carry = (_dg(keys(0), qs[0], 1, 0), init)
        for j in range(nk):
            carry = step(j, carry)
        _, res = carry
        for c in range(spg):
            m, l, acc = res[c]
            o_ref[0, c] = acc / l
            lse_ref[0, c] = _col_bcast(m + jnp.log(l))

    body, r_in, r_out, r_shapes, r_sems, r_args = _riding(body, 3, 2, grid, ride)
    return pl.pallas_call(
        body, name="fa_fwd", grid=grid,
        in_specs=[pl.BlockSpec((1, spg, dh, nq), lambda g, i: (g, i, 0, 0)),
                  pl.BlockSpec((1, lk, LANES), lambda g, i: (g, 0, 0)),
                  pl.BlockSpec((1, lk // nq, dh, nq), lambda g, i: (g, 0, 0, 0))] + r_in,
        out_specs=[pl.BlockSpec((1, spg, dh, nq), lambda g, i: (g, i, 0, 0)),
                   pl.BlockSpec((1, spg, nq, LANES), lambda g, i: (g, i, 0, 0))] + r_out,
        out_shape=[jax.ShapeDtypeStruct((ATT_KV, ns, dh, nq), F32),
                   jax.ShapeDtypeStruct((ATT_KV, ns, nq, LANES), F32)] + r_shapes,
        scratch_shapes=r_sems,
        compiler_params=_params(("arbitrary", "arbitrary")),
    )(qt, ks, vt, *r_args)


def fa_bwd(qs, qt, dos, dot_, ot, lse, ks, kt, vt, ride=None):
    _, ns, dh, nq = qt.shape
    lk = ks.shape[1]
    bk = min(FA_K_BWD, lk)
    nk = lk // bk
    tiles = bk // nq
    spg = min(FA_STRIPS_BWD, ns)
    nc = bk // LANES
    grid = (ATT_KV, ns // spg)

    def body(qs_ref, qt_ref, dos_ref, dot_ref, ot_ref, lse_ref, ks_ref, kt_ref, vt_ref, dq_ref, dk_ref, dv_ref):
        @pl.when(pl.program_id(1) == 0)
        def _():
            dk_ref[...] = jnp.zeros(dk_ref.shape, F32)
            dv_ref[...] = jnp.zeros(dv_ref.shape, F32)

        strips = range(spg)
        lse_b = [lse_ref[0, c] for c in strips]
        d_b = [_col_bcast(jnp.sum(dot_ref[0, c].astype(F32) * ot_ref[0, c], axis=0, keepdims=True)) for c in strips]

        def step(j, dqs):
            ktb, vtb = _pad_rows(_key_block(kt_ref, j, tiles)), _pad_rows(_key_block(vt_ref, j, tiles))
            kb = ks_ref[0, j * bk:(j + 1) * bk, :]
            prods = [(_dg(qs_ref[0, c], ktb, 1, 0), _dg(dos_ref[0, c], vtb, 1, 0)) for c in strips]
            out = []
            for c in strips:
                s, dp = prods[c]
                ps, dss = [], []
                for cc in range(nc):
                    sl = slice(cc * LANES, (cc + 1) * LANES)
                    pc = jnp.exp(s[:, sl] - lse_b[c])
                    ps.append(pc.astype(BF16))
                    dss.append((pc * (dp[:, sl] - d_b[c])).astype(BF16))
                p, ds = jnp.concatenate(ps, axis=1), jnp.concatenate(dss, axis=1)
                dv = _dg(dot_ref[0, c], p, 1, 0)
                dk = _dg(qt_ref[0, c], ds, 1, 0)
                for i in range(tiles):
                    dv_ref[0, j * tiles + i] += dv[:, i * nq:(i + 1) * nq]
                    dk_ref[0, j * tiles + i] += dk[:, i * nq:(i + 1) * nq]
                out.append(dqs[c] + _dg(ds, kb, 1, 0))
            return tuple(out)

        dqs = tuple(jnp.zeros((nq, LANES), F32) for _ in strips)
        for j in range(nk):
            dqs = step(j, dqs)
        for c in strips:
            dq_ref[0, c] = dqs[c].T[:dh]

    sspec = pl.BlockSpec((1, spg, nq, LANES), lambda g, i: (g, i, 0, 0))
    tspec = pl.BlockSpec((1, spg, dh, nq), lambda g, i: (g, i, 0, 0))
    kspec = pl.BlockSpec((1, lk // nq, dh, nq), lambda g, i: (g, 0, 0, 0))
    body, r_in, r_out, r_shapes, r_sems, r_args = _riding(body, 9, 3, grid, ride)
    return pl.pallas_call(
        body, name="fa_bwd", grid=grid,
        in_specs=[sspec, tspec, sspec, tspec, tspec, sspec, pl.BlockSpec((1, lk, LANES), lambda g, i: (g, 0, 0)),
                  kspec, kspec] + r_in,
        out_specs=[tspec, kspec, kspec] + r_out,
        out_shape=[jax.ShapeDtypeStruct((ATT_KV, ns, dh, nq), F32),
                   jax.ShapeDtypeStruct((ATT_KV, lk // nq, dh, nq), F32),
                   jax.ShapeDtypeStruct((ATT_KV, lk // nq, dh, nq), F32)] + r_shapes,
        scratch_shapes=r_sems,
        compiler_params=_params(("arbitrary", "arbitrary")),
    )(qs, qt, dos, dot_, ot, lse, ks, kt, vt, *r_args)


def _post_mix_fn(of, ob, ug, oa, hgw, attw):
    o = of + ob
    parts = []
    for h in range(HG_HEADS):
        parts.append(_rms(o[:, h * HG_D:(h + 1) * HG_D], hgw))
    hg = jnp.concatenate(parts, axis=1) * (ug * _sigmoid(ug))
    return jnp.concatenate([hg, _rms(oa, attw)], axis=1)


def _swiglu_fn(gate, up):
    return gate * _sigmoid(gate) * up


def _loss_fn(x2, w, tgt):
    e = _rms(x2, w) - tgt
    return 0.5 * jnp.sum(jnp.mean(e * e, axis=-1, keepdims=True), axis=0, keepdims=True)


def _place():
    return lax.axis_index("x"), lax.axis_index("y"), lax.axis_index("c")


def _other_chips(x, y):
    return [(1 - x, y), (x, 1 - y), (1 - x, 1 - y)]


def _hbm_specs(n):
    return [pl.BlockSpec(memory_space=pl.ANY)] * n


SEMS_PER_ARRAY = 7


def _exchange_sems(n):
    return [pltpu.SemaphoreType.DMA((SEMS_PER_ARRAY * n,)), pltpu.SemaphoreType.DMA((SEMS_PER_ARRAY * n,)),
            pltpu.SemaphoreType.DMA((n,))]


def _exchange_phases(srcs, outs, ssem, rsem, lsem, kind):
    n = len(srcs)
    x, y, c = _place()
    k = 2 * x + y
    sib = (x, y, 1 - c)
    if kind == "siblings":
        def swap(a):
            hr = srcs[a].shape[1] // 2
            return pltpu.make_async_remote_copy(src_ref=srcs[a].at[:, pl.ds((1 - c) * hr, hr), :], dst_ref=outs[a],
                                                send_sem=ssem.at[a], recv_sem=rsem.at[a], device_id=sib,
                                                device_id_type=MESH)

        def start_swaps():
            for a in range(n):
                swap(a).start()

        def finish_swaps():
            for a in range(n):
                swap(a).wait()

        return start_swaps, lambda: None, finish_swaps
    reduce = kind == "reduce"
    chips = _other_chips(x, y)
    pairs = [(a, j) for a in range(n) for j in range(3)]

    def hrows(a):
        return srcs[a].shape[1] if reduce else srcs[a].shape[0] // 2

    def half(a, kk, cc):
        return outs[a].at[kk, pl.ds(cc * hrows(a), hrows(a)), :]

    def mine(a, kk):
        return srcs[a].at[kk] if reduce else srcs[a].at[pl.ds(c * hrows(a), hrows(a)), :]

    def copy(a, j, src_ref, dst_ref, to):
        return pltpu.make_async_remote_copy(src_ref=src_ref, dst_ref=dst_ref, send_sem=ssem.at[SEMS_PER_ARRAY * a + j],
                                            recv_sem=rsem.at[SEMS_PER_ARRAY * a + j], device_id=to, device_id_type=MESH)

    def local(a):
        if reduce:
            return pltpu.make_async_copy(srcs[a].at[k], half(a, k, c), lsem.at[a])
        return pltpu.make_async_copy(srcs[a], outs[a].at[k], lsem.at[a])

    def ici(a, j, arriving):
        px, py = chips[j]
        kk = 2 * px + py
        if arriving:
            return copy(a, j, mine(a, k), half(a, kk, c), (px, py, c))
        return copy(a, j, mine(a, kk), half(a, k, c), (px, py, c))

    def passed(a, j, arriving):
        px, py = chips[j]
        kk = 2 * px + py
        return copy(a, 3 + j, half(a, kk, c), half(a, kk, (1 - c) if arriving else c), sib)

    def own(a, arriving):
        return copy(a, 6, mine(a, k), half(a, k, (1 - c) if arriving else c), sib)

    def start():
        for a in range(n):
            local(a).start()
        for a, j in pairs:
            ici(a, j, False).start()
        if reduce:
            for a in range(n):
                own(a, False).start()

    def forward():
        for a, j in pairs:
            ici(a, j, True).wait_recv()
            passed(a, j, False).start()

    def finish():
        for a in range(n):
            if reduce:
                own(a, True).wait_recv()
            for j in range(3):
                passed(a, j, True).wait_recv()
        for a, j in pairs:
            ici(a, j, False).wait_send()
            passed(a, j, False).wait_send()
        for a in range(n):
            if reduce:
                own(a, False).wait_send()
            local(a).wait()

    return start, forward, finish


def _exchange_shapes(arrays, kind):
    if kind == "siblings":
        return [jax.ShapeDtypeStruct((N_CHIPS, g.shape[1] // 2, g.shape[2]), g.dtype) for g in arrays]
    if kind == "reduce":
        return [jax.ShapeDtypeStruct((N_CHIPS, 2 * p.shape[1], p.shape[2]), p.dtype) for p in arrays]
    return [jax.ShapeDtypeStruct((N_CHIPS,) + s.shape, s.dtype) for s in arrays]


def exchange(arrays, kind, name):
    n = len(arrays)

    def body(*refs):
        start, forward, finish = _exchange_phases(refs[:n], refs[n:2 * n], *refs[2 * n:], kind)
        start()
        forward()
        finish()

    return pl.pallas_call(
        body, name=name, in_specs=_hbm_specs(n), out_specs=_hbm_specs(n),
        out_shape=_exchange_shapes(arrays, kind), scratch_shapes=_exchange_sems(n),
    )(*arrays)


def allreduce_small(p, name):
    rows, width = p.shape

    def body(p_ref, s_ref, gath, ssem, rsem):
        x, y, c = _place()
        me = 4 * x + 2 * y + c
        copies = []
        for d in range(1, N_DEV):
            dx, dy, dc = (d >> 2) & 1, (d >> 1) & 1, d & 1
            peer = (x ^ dx, y ^ dy, c ^ dc)
            cp = pltpu.make_async_remote_copy(src_ref=p_ref, dst_ref=gath.at[me], send_sem=ssem.at[d - 1],
                                              recv_sem=rsem.at[d - 1], device_id=peer, device_id_type=MESH)
            cp.start()
            copies.append(cp)
        gath[me] = p_ref[...]
        for d, cp in enumerate(copies, start=1):
            dx, dy, dc = (d >> 2) & 1, (d >> 1) & 1, d & 1
            peer_slot = 4 * (x ^ dx) + 2 * (y ^ dy) + (c ^ dc)
            pltpu.make_async_remote_copy(src_ref=p_ref, dst_ref=gath.at[peer_slot], send_sem=ssem.at[d - 1],
                                         recv_sem=rsem.at[d - 1], device_id=(x ^ dx, y ^ dy, c ^ dc),
                                         device_id_type=MESH).wait_recv()
        for cp in copies:
            cp.wait_send()
        acc = gath[0]
        for d in range(1, N_DEV):
            acc = acc + gath[d]
        s_ref[...] = acc

    return pl.pallas_call(
        body, name=name,
        in_specs=[pl.BlockSpec(memory_space=pltpu.VMEM)],
        out_specs=pl.BlockSpec(memory_space=pltpu.VMEM),
        out_shape=jax.ShapeDtypeStruct((rows, width), F32),
        scratch_shapes=[pltpu.VMEM((N_DEV, rows, width), F32), pltpu.SemaphoreType.DMA((N_DEV - 1,)),
                        pltpu.SemaphoreType.DMA((N_DEV - 1,))],
    )(p)


def add_my_half(g, recv, name):
    _, r, cols = g.shape
    hr = r // 2
    tb = _pick(hr, 512, 8)
    nb = hr // tb
    c_arr = lax.axis_index("c").astype(jnp.int32).reshape(1)

    def body(c_ref, g_ref, r_ref, o_ref):
        o_ref[...] = (g_ref[...] + r_ref[...]).astype(o_ref.dtype)

    return pl.pallas_call(
        body, name=name,
        grid_spec=pltpu.PrefetchScalarGridSpec(
            num_scalar_prefetch=1, grid=(N_CHIPS, nb),
            in_specs=[pl.BlockSpec((1, tb, cols), lambda k, i, c_ref: (k, c_ref[0] * nb + i, 0)),
                      pl.BlockSpec((1, tb, cols), lambda k, i, c_ref: (k, i, 0))],
            out_specs=pl.BlockSpec((1, tb, cols), lambda k, i, c_ref: (k, i, 0))),
        out_shape=jax.ShapeDtypeStruct((N_CHIPS, hr, cols), BF16),
        compiler_params=_params(("arbitrary", "arbitrary")),
    )(c_arr, g, recv)


def sum_chips(parts, name):
    _, hr, cols = parts.shape
    tb = _pick(hr, 512, 8)

    def body(p_ref, o_ref):
        o_ref[...] = ((p_ref[0].astype(F32) + p_ref[1].astype(F32)) + p_ref[2].astype(F32)) + p_ref[3].astype(F32)

    return pl.pallas_call(
        body, name=name, grid=(hr // tb,),
        in_specs=[pl.BlockSpec((N_CHIPS, tb, cols), lambda i: (0, i, 0))],
        out_specs=pl.BlockSpec((tb, cols), lambda i: (i, 0)),
        out_shape=jax.ShapeDtypeStruct((hr, cols), F32),
        compiler_params=_params(("arbitrary",)),
    )(parts)


def adamw(w, g, m, v, name):
    rows, width = w.shape
    tb = _pick(rows, 512, 8)

    def body(w_ref, g_ref, m_ref, v_ref, d_ref, mo_ref, vo_ref):
        gg = g_ref[...]
        m_new = ADAM_B1 * m_ref[...] + (1.0 - ADAM_B1) * gg
        v_new = ADAM_B2 * v_ref[...] + (1.0 - ADAM_B2) * (gg * gg)
        m_hat = m_new / (1.0 - ADAM_B1 ** ADAM_STEP)
        v_hat = v_new / (1.0 - ADAM_B2 ** ADAM_STEP)
        d_ref[...] = -ADAM_LR * (m_hat / (jnp.sqrt(v_hat) + ADAM_EPS) + ADAM_WD * w_ref[...])
        mo_ref[...] = m_new
        vo_ref[...] = v_new

    spec = pl.BlockSpec((tb, width), lambda i: (i, 0))
    return pl.pallas_call(
        body, name=name, grid=(rows // tb,), in_specs=[spec] * 4, out_specs=[spec] * 3,
        out_shape=[jax.ShapeDtypeStruct((rows, width), F32)] * 3,
        compiler_params=_params(("arbitrary",)),
    )(w, g, m, v)


def _pack_rows(vecs, width=1024):
    rows, cur, used = [], [], 0
    for v in vecs:
        n = v.shape[1]
        if used + n > width:
            cur.append(jnp.zeros((1, width - used), F32))
            rows.append(jnp.concatenate(cur, axis=1))
            cur, used = [], 0
        cur.append(v)
        used += n
    cur.append(jnp.zeros((1, width - used), F32))
    rows.append(jnp.concatenate(cur, axis=1))
    return rows


def _pad128(v):
    return jnp.pad(v, ((0, 0), (0, LANES - v.shape[1])))


def kernel(x, norm1_w, w_in, lb_logits, hg_norm_w, q_norm_w, k_norm_w, att_norm_w, w_out, norm2_w, w_gate_up, w_down, final_norm_w, loss_target, m_norm1_w, m_w_in, m_lb_logits, m_hg_norm_w, m_q_norm_w, m_k_norm_w, m_att_norm_w, m_w_out, m_norm2_w, m_w_gate_up, m_w_down, m_final_norm_w, v_norm1_w, v_w_in, v_lb_logits, v_hg_norm_w, v_q_norm_w, v_k_norm_w, v_att_norm_w, v_w_out, v_norm2_w, v_w_gate_up, v_w_down, v_final_norm_w):
    t, d = x.shape[1], x.shape[2]
    xi, yi, _ = _place()
    chip = 2 * xi + yi
    x2d = x.reshape(t, d)
    tgt = loss_target.reshape(t, d)
    tok = min(TOK, t)
    nt = t // tok
    ns = ATT_GROUP * nt

    late_w = [w_out[0].astype(BF16), w_gate_up[0].astype(BF16), w_down[0].astype(BF16)]
    lb_shard = jnp.pad(lb_logits.reshape(4, LANES), ((0, 12), (0, 0)))

    n1 = norm1_w.reshape(1, d)
    n2 = norm2_w.reshape(1, d)
    nf = final_norm_w.reshape(1, d)
    tm = min(512, t)
    h1, g_in, g_lb = _rows(lambda a, w: ((_rms(a, w),), ()), "norm1", t, tm, [_rin(x2d, tm)], [n1],
                           [_rout(t, tm, d, BF16)], ride=("gather", [w_in[0].astype(BF16), lb_shard]))
    wf_in = g_in.transpose(1, 0, 2).reshape(g_in.shape[1], -1)
    lb_full = g_lb[:, :4].transpose(1, 0, 2).reshape(4, HG_W)
    l_f0, l_f1, l_b0, l_b1 = (lb_full[i:i + 1] for i in range(4))
    u = mm_rows(h1, wf_in, "mm_in")
    o_f, ss_f = gla_fwd(u, l_f0, l_f1, 1, False, "gla_fwd_f")
    o_b, ss_b = gla_fwd(u, l_b0, l_b1, 2, True, "gla_fwd_b")

    cos2, sin2 = _rope_tables(t)
    ck, sk = jnp.tile(cos2, (1, ATT_KV)), jnp.tile(sin2, (1, ATT_KV))
    qw, kw = _pad128(q_norm_w.reshape(1, ATT_DH)), _pad128(k_norm_w.reshape(1, ATT_DH))
    gq, gk = _group_sum_matrix(ATT_QW), _group_sum_matrix(ATT_KVW)
    tq, tk = _tile_matrix(ATT_QW), _tile_matrix(ATT_KVW)
    prep_in = [_rin(u, tok, ATT_QW, 5), _rin(u, tok, ATT_KVW, 24), _rin(ck, tok), _rin(sk, tok)]
    prep_consts = [qw, kw, gq, gk, tq, tk]

    def att_prep_fn(aq, ak, av, *rest):
        q, k = _att_prep_fn(aq, ak, *rest)
        return (_head_t(q, ATT_HEADS), _head_s(q, ATT_HEADS), _head_s(k, ATT_KV), _head_t(k, ATT_KV),
                _head_t(av, ATT_KV)), ()

    q_t, q_s, k_s, k_t, v_t = _rows(
        att_prep_fn, "att_prep", t, tok, prep_in[:2] + [_rin(u, tok, ATT_KVW, 25)] + prep_in[2:], prep_consts,
        [_tout(ATT_HEADS, nt, ATT_DH, tok, BF16), _tout(ATT_HEADS, nt, tok, LANES, BF16),
         _tout(ATT_KV, nt, tok, LANES, BF16), _tout(ATT_KV, nt, ATT_DH, tok, BF16),
         _tout(ATT_KV, nt, ATT_DH, tok, BF16)])
    q_t = q_t.reshape(ATT_KV, ns, ATT_DH, tok)
    q_s = q_s.reshape(ATT_KV, ns, tok, LANES)
    k_s = k_s.reshape(ATT_KV, t, LANES)
    o_t, lse, g_out, g_gu, g_down = fa_fwd(q_t, k_s, v_t, ride=("gather", late_w))
    wf_gu = g_gu.transpose(1, 0, 2).reshape(g_gu.shape[1], -1)
    wf_out = g_out.reshape(-1, g_out.shape[2])
    wf_down = g_down.reshape(-1, g_down.shape[2])
    o_tiles = o_t.reshape(ATT_HEADS, nt, ATT_DH, tok)

    hgw = hg_norm_w.reshape(1, HG_D)
    attw = att_norm_w.reshape(1, ATT_QW)
    mix_in = [_rin(o_f, tok), _rin(o_b, tok), _rin(u, tok, HG_W, 4), _tin(o_tiles)]
    (mix,) = _rows(lambda of, ob, ug, ot, hw, aw: ((_post_mix_fn(of, ob, ug, _from_head_t(ot), hw, aw),), ()),
                   "post_mix", t, tok, mix_in, [hgw, attw], [_rout(t, tok, d, BF16)])
    x1, h2 = mm_res_norm(mix, wf_out, x2d, n2, "mm_out")
    gu2, act = ffn_up(h2, wf_gu)

    dx2, dx2b, g_final, loss_part = mm_res_loss(act, wf_down, x1, tgt, nf, "mm_down_loss")
    dgu2 = ffn_down_dx(dx2b, wf_down, gu2)
    gw_down = mm_tn(dx2b, act, "mm_down_dw", transpose_out=True)
    dx1, dx1b, g_norm2 = mm_nt_norm_bwd(dgu2, wf_gu, x1, n2, dx2, "mm_gate_up_dx", True)
    gw_gu = mm_tn(h2, dgu2, "mm_gate_up_dw", col_shards=N_CHIPS)
    dmix = mm_rows(dx1b, wf_out, "mm_out_dx", trans_b=True)
    gw_out = mm_tn(mix, dx1b, "mm_out_dw")

    def post_mix_bwd_fn(of, ob, ug, ot, dm, hgw_, attw_):
        _, vjp = jax.vjp(_post_mix_fn, of, ob, ug, _from_head_t(ot), hgw_, attw_)
        dof, _, dug, doa, dhgw, dattw = vjp(dm)
        return (dof, dug, _head_t(doa, ATT_HEADS), _head_s(doa, ATT_HEADS)), (dhgw, dattw)

    late = ["w_out", "w_gate_up", "w_down"]
    late_g = [gw_out.reshape(N_CHIPS, -1, d), gw_gu, gw_down.reshape(N_CHIPS, -1, d)]
    do_hg, du_g, do_t, do_s, g_hg, g_att, *late_recv = _rows(
        post_mix_bwd_fn, "post_mix_bwd", t, tok, mix_in + [_rin(dmix, tok)], [hgw, attw],
        [_rout(t, tok, HG_W, F32), _rout(t, tok, HG_W, BF16), _tout(ATT_HEADS, nt, ATT_DH, tok, BF16),
         _tout(ATT_HEADS, nt, tok, LANES, BF16)], [HG_D, ATT_QW], ride=("siblings", late_g))
    late_part = [add_my_half(g, r, "add_my_half_" + n) for g, r, n in zip(late_g, late_recv, late)]
    dq_t, dk_t, dv_t, *late_parts = fa_bwd(q_s, q_t, do_s.reshape(q_s.shape), do_t.reshape(q_t.shape), o_t, lse, k_s, k_t,
                                           v_t, ride=("reduce", late_part))

    dq_f, df_f, di_f, dl_f0, dl_f1 = gla_bwd(u, l_f0, l_f1, ss_f, do_hg, 1, False, "gla_bwd_f")
    dq_b, df_b, di_b, dl_b0, dl_b1 = gla_bwd(u, l_b0, l_b1, ss_b, do_hg, 2, True, "gla_bwd_b")

    def att_prep_bwd_fn(aq, ak, ck_, sk_, dqt, dkt, dvt, qf, qb, ff, fb, i_f, i_b, dg, qw_, kw_, gq_, gk_, tq_, tk_):
        _, vjp = jax.vjp(lambda a, b, c_, e: _att_prep_fn(a, b, ck_, sk_, c_, e, gq_, gk_, tq_, tk_),
                         aq, ak, qw_, kw_)
        daq, dak, dqw, dkw = vjp((_from_head_t(dqt), _from_head_t(dkt)))
        du_tile = jnp.concatenate([qf + qb, ff, fb, i_f + i_b, dg.astype(F32), daq, dak, _from_head_t(dvt)], axis=1)
        return (du_tile,), (dqw, dkw)

    du, g_q, g_k = _rows(
        att_prep_bwd_fn, "att_prep_bwd", t, tok,
        prep_in + [_tin(dq_t.reshape(ATT_HEADS, nt, ATT_DH, tok)), _tin(dk_t), _tin(dv_t)]
        + [_rin(a, tok) for a in (dq_f, dq_b, df_f, df_b, di_f, di_b, du_g)], prep_consts,
        [_rout(t, tok, u.shape[1], BF16)], [LANES, LANES])
    gw_in_t = mm_tn(h1, du, "mm_in_dw", transpose_out=True)

    names = ["w_in"] + late
    g_in4 = gw_in_t.reshape(N_CHIPS, -1, d)
    in_part = add_my_half(g_in4, exchange([g_in4], "siblings", "rs_siblings_w_in")[0], "add_my_half_w_in")
    grad_x, g_norm1, in_parts = mm_nt_norm_bwd(du, wf_in, x2d, n1, dx1, "mm_in_dx", False, ride=("reduce", [in_part]))
    parts = [in_parts] + late_parts
    g_shard = [sum_chips(p, "sum_chips_" + n) for p, n in zip(parts, names)]
    g_shard[0] = g_shard[0].T
    big = {}
    for n, g, w, m, v in zip(names, g_shard, (w_in, w_out, w_gate_up, w_down), (m_w_in, m_w_out, m_w_gate_up, m_w_down),
                             (v_w_in, v_w_out, v_w_gate_up, v_w_down)):
        dlt, mn, vn = adamw(w[0], g, m[0], v[0], "adamw_" + n)
        big[n] = (g[None], dlt[None], mn[None], vn[None])

    small = [g_norm1, g_norm2, g_final, g_att, g_hg, g_q, g_k, loss_part, dl_f0, dl_f1, dl_b0, dl_b1]
    packed = _pack_rows(small)
    packed += [jnp.zeros((1, 1024), F32)] * (8 - len(packed))
    tot = allreduce_small(jnp.concatenate(packed, axis=0), "allreduce_small")
    s_norm1, s_norm2, s_final = tot[0:1], tot[1:2], tot[2:3]
    s_att, s_hg, s_q, s_k = tot[3:4, 0:512], tot[3:4, 512:640], tot[3:4, 640:704], tot[3:4, 768:832]
    loss = tot[3, 896]
    s_lb = jnp.concatenate([tot[4:5, 0:512], tot[4:5, 512:1024], tot[5:6, 0:512], tot[5:6, 512:1024]], axis=0)
    s_lb = lax.dynamic_slice(s_lb, (0, chip * LANES), (4, LANES)).reshape(1, 512)

    snames = ["norm1_w", "lb_logits", "hg_norm_w", "q_norm_w", "k_norm_w", "att_norm_w", "norm2_w", "final_norm_w"]
    g_small = dict(zip(snames, [s_norm1, s_lb, s_hg, s_q, s_k, s_att, s_norm2, s_final]))
    w_small = dict(zip(snames, [norm1_w, lb_logits, hg_norm_w, q_norm_w, k_norm_w, att_norm_w, norm2_w, final_norm_w]))
    m_small = dict(zip(snames, [m_norm1_w, m_lb_logits, m_hg_norm_w, m_q_norm_w, m_k_norm_w, m_att_norm_w, m_norm2_w, m_final_norm_w]))
    v_small = dict(zip(snames, [v_norm1_w, v_lb_logits, v_hg_norm_w, v_q_norm_w, v_k_norm_w, v_att_norm_w, v_norm2_w, v_final_norm_w]))

    def pack_small(tree):
        rows = _pack_rows([tree[n].reshape(1, -1) for n in snames])
        rows += [jnp.zeros((1, 1024), F32)] * (8 - len(rows))
        return jnp.concatenate(rows, axis=0)

    d_s, m_s, v_s = adamw(pack_small(w_small), pack_small(g_small), pack_small(m_small), pack_small(v_small), "adamw_small")

    def unpack_small(a):
        out, r, used = {}, 0, 0
        for n in snames:
            size = w_small[n].size
            if used + size > 1024:
                r, used = r + 1, 0
            out[n] = a[r, used:used + size].reshape(w_small[n].shape)
            used += size
        return out

    d_sm, m_sm, v_sm = unpack_small(d_s), unpack_small(m_s), unpack_small(v_s)
    g_sm = {n: g_small[n].reshape(w_small[n].shape) for n in snames}

    order = ["norm1_w", "w_in", "lb_logits", "hg_norm_w", "q_norm_w", "k_norm_w", "att_norm_w", "w_out", "norm2_w",
             "w_gate_up", "w_down", "final_norm_w"]

    def pick(small_tree, idx):
        return [big[n][idx] if n in big else small_tree[n] for n in order]

    return (loss, grad_x.reshape(x.shape), *pick(g_sm, 0), *pick(d_sm, 1), *pick(m_sm, 2), *pick(v_sm, 3))
```

```python
import functools

import numpy as np
import jax
import jax.numpy as jnp
from jax import lax
from jax.experimental import pallas as pl
from jax.experimental.pallas import tpu as pltpu

F32 = jnp.float32
BF16 = jnp.bfloat16
MESH = pl.DeviceIdType.MESH

EPS = 1e-6
GRID_W = 64
HG_HEADS = 4
HG_D = 128
HG_W = HG_HEADS * HG_D
CHUNK = 64
ATT_HEADS = 8
ATT_KV = 2
ATT_GROUP = ATT_HEADS // ATT_KV
ATT_DH = 64
ATT_QW = ATT_HEADS * ATT_DH
ATT_KVW = ATT_KV * ATT_DH
ROPE_THETA = 10000.0
N_CHIPS = 4
N_DEV = 8

ADAM_LR = 0.001
ADAM_B1 = 0.9
ADAM_B2 = 0.999
ADAM_EPS = 1e-08
ADAM_WD = 0.01
ADAM_STEP = 10

VMEM_LIMIT = 52 * 1024 * 1024
LANES = 128
TOK = 256


def _params(sem=None):
    return pltpu.CompilerParams(dimension_semantics=sem, vmem_limit_bytes=VMEM_LIMIT)


def _dg(a, b, ca, cb):
    return lax.dot_general(a.astype(BF16), b.astype(BF16), (((ca,), (cb,)), ((), ())),
                           preferred_element_type=F32)


@jax.custom_vjp
def dot_nn(a, b):
    return _dg(a, b, 1, 0)


def _dot_nn_fwd(a, b):
    return _dg(a, b, 1, 0), (a, b)


def _dot_nn_bwd(res, g):
    a, b = res
    return _dg(g, b, 1, 1), _dg(a, g, 0, 0)


dot_nn.defvjp(_dot_nn_fwd, _dot_nn_bwd)


@jax.custom_vjp
def dot_nt(a, b):
    return _dg(a, b, 1, 1)


def _dot_nt_fwd(a, b):
    return _dg(a, b, 1, 1), (a, b)


def _dot_nt_bwd(res, g):
    a, b = res
    return _dg(g, b, 1, 0), _dg(g, a, 0, 0)


dot_nt.defvjp(_dot_nt_fwd, _dot_nt_bwd)


@jax.custom_vjp
def dot_tn(a, b):
    return _dg(a, b, 0, 0)


def _dot_tn_fwd(a, b):
    return _dg(a, b, 0, 0), (a, b)


def _dot_tn_bwd(res, g):
    a, b = res
    return _dg(b, g, 1, 1), _dg(a, g, 1, 0)


dot_tn.defvjp(_dot_tn_fwd, _dot_tn_bwd)


def _split3(a):
    hi = a.astype(BF16)
    r1 = a - hi.astype(F32)
    mid = r1.astype(BF16)
    lo = (r1 - mid.astype(F32)).astype(BF16)
    return lo, mid, hi


def _sum3(terms):
    lo, mid, hi = terms
    return (lo + mid) + hi


@jax.custom_vjp
def xdot_r(a, m):
    return _sum3([_dg(p, m, 1, 0) for p in _split3(a)])


def _xdot_r_fwd(a, m):
    return xdot_r(a, m), m


def _xdot_r_bwd(m, g):
    return _sum3([_dg(p, m, 1, 1) for p in _split3(g)]), jnp.zeros_like(m)


xdot_r.defvjp(_xdot_r_fwd, _xdot_r_bwd)


@jax.custom_vjp
def xdot_l(m, a):
    return _sum3([_dg(m, p, 1, 0) for p in _split3(a)])


def _xdot_l_fwd(m, a):
    return xdot_l(m, a), m


def _xdot_l_bwd(m, g):
    return jnp.zeros_like(m), _sum3([_dg(m, p, 0, 0) for p in _split3(g)])


xdot_l.defvjp(_xdot_l_fwd, _xdot_l_bwd)


@jax.custom_vjp
def swap_pairs(y):
    n = y.shape[-1]
    lane = lax.broadcasted_iota(jnp.int32, y.shape, 1)
    nxt = pltpu.roll(y, n - 1, 1)
    prv = pltpu.roll(y, 1, 1)
    return jnp.where(lane % 2 == 0, nxt, prv)


def _swap_fwd(y):
    return swap_pairs(y), None


def _swap_bwd(_, g):
    return (swap_pairs(g),)


swap_pairs.defvjp(_swap_fwd, _swap_bwd)


def _rms(x, w):
    return x * lax.rsqrt(jnp.mean(x * x, axis=-1, keepdims=True) + EPS) * w


def _sigmoid(x):
    return jax.nn.sigmoid(x)


def _rows(fn, name, t, tm, ins, consts, outs, accs=(), ride=None):
    n_r, n_c, n_o, n_a = len(ins), len(consts), len(outs), len(accs)

    def body(*refs):
        r = refs[:n_r]
        c = refs[n_r:n_r + n_c]
        o = refs[n_r + n_c:n_r + n_c + n_o]
        a = refs[n_r + n_c + n_o:]
        ro, ao = fn(*[x[...] for x in r], *[x[...] for x in c])
        for ref, val in zip(o, ro):
            if isinstance(val, (list, tuple)):
                for h, piece in enumerate(val):
                    ref[h, 0] = piece.astype(ref.dtype)
            else:
                ref[...] = val.astype(ref.dtype)
        if n_a:
            @pl.when(pl.program_id(0) == 0)
            def _():
                for ref in a:
                    ref[...] = jnp.zeros(ref.shape, F32)
            for ref, val in zip(a, ao):
                ref[...] += val

    in_specs = [s for _, s in ins]
    in_specs += [pl.BlockSpec(c.shape, lambda i, nd=c.ndim: (0,) * nd) for c in consts]
    out_specs = [s for _, s in outs] + [pl.BlockSpec((1, w), lambda i: (0, 0)) for w in accs]
    out_shape = [s for s, _ in outs] + [jax.ShapeDtypeStruct((1, w), F32) for w in accs]
    grid = (t // tm,)
    body, r_in, r_out, r_shapes, r_sems, r_args = _riding(body, n_r + n_c, n_o + n_a, grid, ride)
    return pl.pallas_call(
        body, name=name, grid=grid, in_specs=in_specs + r_in, out_specs=out_specs + r_out,
        out_shape=out_shape + r_shapes, scratch_shapes=r_sems, compiler_params=_params(("arbitrary",)),
    )(*[a for a, _ in ins], *consts, *r_args)


def _rin(a, tm, width=None, cb=0):
    width = a.shape[1] if width is None else width
    return a, pl.BlockSpec((tm, width), lambda i, cb=cb: (i, cb))


def _rout(t, tm, width, dtype):
    return jax.ShapeDtypeStruct((t, width), dtype), pl.BlockSpec((tm, width), lambda i: (i, 0))


def _tin(a):
    return a, pl.BlockSpec((a.shape[0], 1) + a.shape[2:], lambda i: (0, i, 0, 0))


def _tout(heads, nt, r, c, dtype):
    return jax.ShapeDtypeStruct((heads, nt, r, c), dtype), pl.BlockSpec((heads, 1, r, c), lambda i: (0, i, 0, 0))


def _pick(n, cap, mult):
    best = None
    for d in range(mult, min(n, cap) + 1, mult):
        if n % d == 0:
            best = d
    return best if best is not None else n


def mm_rows(a, b, name, trans_b=False, res=None, out_dtype=F32):
    m, k = a.shape
    n = b.shape[0] if trans_b else b.shape[1]
    tn = _pick(n, 3328, LANES)
    tm = _pick(m, 512 if k <= 3072 else 256, 8)
    has_res = res is not None

    def body(*refs):
        if has_res:
            a_ref, b_ref, r_ref, o_ref = refs
        else:
            a_ref, b_ref, o_ref = refs
        acc = _dg(a_ref[...], b_ref[...], 1, 1 if trans_b else 0)
        if has_res:
            acc = acc + r_ref[...]
        o_ref[...] = acc.astype(o_ref.dtype)

    in_specs = [pl.BlockSpec((tm, k), lambda j, i: (i, 0))]
    if trans_b:
        in_specs.append(pl.BlockSpec((tn, k), lambda j, i: (j, 0)))
    else:
        in_specs.append(pl.BlockSpec((k, tn), lambda j, i: (0, j)))
    args = [a, b]
    if has_res:
        in_specs.append(pl.BlockSpec((tm, tn), lambda j, i: (i, j)))
        args.append(res)
    return pl.pallas_call(
        body, name=name, grid=(n // tn, m // tm), in_specs=in_specs,
        out_specs=pl.BlockSpec((tm, tn), lambda j, i: (i, j)),
        out_shape=jax.ShapeDtypeStruct((m, n), out_dtype),
        compiler_params=_params(("arbitrary", "arbitrary")),
    )(*args)


def mm_tn(a, b, name, col_shards=None, transpose_out=False):
    t, m = a.shape
    parts = b.shape[0] if b.ndim == 3 else 1
    n = parts * b.shape[-1]
    tm = _pick(m, 512, LANES)
    width = n // col_shards if col_shards else None
    group = max(1, 3328 // width) if col_shards else 1
    tn = width * group if col_shards else _pick(n, 3328, LANES)
    tk = _pick(t, 1024, 8)
    nkk = t // tk
    per_part = b.shape[-1] // tn

    def body(a_ref, b_ref, o_ref, *scratch):
        acc_ref = scratch[0] if transpose_out else o_ref

        @pl.when(pl.program_id(2) == 0)
        def _():
            acc_ref[...] = jnp.zeros(acc_ref.shape, F32)
        acc = _dg(a_ref[...], b_ref[0] if b.ndim == 3 else b_ref[...], 0, 0)
        if col_shards:
            for s in range(group):
                acc_ref[s] += acc[:, s * width:(s + 1) * width]
        else:
            acc_ref[...] += acc
        if transpose_out:
            @pl.when(pl.program_id(2) == nkk - 1)
            def _():
                o_ref[...] = acc_ref[...].T

    if col_shards:
        out_spec = pl.BlockSpec((group, tm, width), lambda i, j, kk: (j, i, 0))
        out_shape = jax.ShapeDtypeStruct((col_shards, m, width), F32)
    elif transpose_out:
        out_spec = pl.BlockSpec((tn, tm), lambda i, j, kk: (j, i))
        out_shape = jax.ShapeDtypeStruct((n, m), F32)
    else:
        out_spec = pl.BlockSpec((tm, tn), lambda i, j, kk: (i, j))
        out_shape = jax.ShapeDtypeStruct((m, n), F32)
    if b.ndim == 3:
        b_spec = pl.BlockSpec((1, tk, tn), lambda i, j, kk: (j // per_part, kk, j % per_part))
    else:
        b_spec = pl.BlockSpec((tk, tn), lambda i, j, kk: (kk, j))
    return pl.pallas_call(
        body, name=name, grid=(m // tm, n // tn, nkk),
        in_specs=[pl.BlockSpec((tk, tm), lambda i, j, kk: (kk, i)), b_spec],
        out_specs=out_spec, out_shape=out_shape,
        scratch_shapes=[pltpu.VMEM((tm, tn), F32)] if transpose_out else [],
        compiler_params=_params(("arbitrary", "arbitrary", "arbitrary")),
    )(a, b)


def ffn_up(h2, w_gu):
    t, k = h2.shape
    dff = w_gu.shape[1] // 2
    tn = _pick(dff, 1408, LANES)
    ncol = dff // tn
    tm = _pick(t, 512, 8)

    def body(a_ref, wg_ref, wu_ref, gu_ref, act_ref):
        a = a_ref[...]
        g = _dg(a, wg_ref[...], 1, 0)
        u = _dg(a, wu_ref[...], 1, 0)
        gu_ref[0] = g.astype(gu_ref.dtype)
        gu_ref[1] = u.astype(gu_ref.dtype)
        act_ref[...] = _swiglu_fn(g, u).astype(act_ref.dtype)

    return pl.pallas_call(
        body, name="ffn_up", grid=(ncol, t // tm),
        in_specs=[pl.BlockSpec((tm, k), lambda s, i: (i, 0)), pl.BlockSpec((k, tn), lambda s, i: (0, s)),
                  pl.BlockSpec((k, tn), lambda s, i: (0, s + ncol))],
        out_specs=[pl.BlockSpec((2, tm, tn), lambda s, i: (0, i, s)), pl.BlockSpec((tm, tn), lambda s, i: (i, s))],
        out_shape=[jax.ShapeDtypeStruct((2, t, dff), BF16), jax.ShapeDtypeStruct((t, dff), BF16)],
        compiler_params=_params(("arbitrary", "arbitrary")),
    )(h2, w_gu, w_gu)


def ffn_down_dx(dx2b, w_down, gu2):
    t, k = dx2b.shape
    dff = w_down.shape[0]
    tn = _pick(dff, 1408, LANES)
    tm = _pick(t, 512, 8)

    def body(a_ref, w_ref, gu_ref, o_ref):
        pieces = [slice(0, tm // 2), slice(tm // 2, tm)]
        dacts = [_dg(a_ref[rows, :], w_ref[...], 1, 1) for rows in pieces]
        for rows, dact in zip(pieces, dacts):
            _, vjp = jax.vjp(_swiglu_fn, gu_ref[0, rows, :].astype(F32), gu_ref[1, rows, :].astype(F32))
            dg, du = vjp(dact)
            o_ref[0, rows, :] = dg.astype(o_ref.dtype)
            o_ref[1, rows, :] = du.astype(o_ref.dtype)

    blk = pl.BlockSpec((2, tm, tn), lambda s, i: (0, i, s))
    return pl.pallas_call(
        body, name="ffn_down_dx", grid=(dff // tn, t // tm),
        in_specs=[pl.BlockSpec((tm, k), lambda s, i: (i, 0)), pl.BlockSpec((tn, k), lambda s, i: (s, 0)), blk],
        out_specs=blk, out_shape=jax.ShapeDtypeStruct((2, t, dff), BF16),
        compiler_params=_params(("arbitrary", "arbitrary")),
    )(dx2b, w_down, gu2)


def mm_res_norm(a, b, res, w, name):
    t, k = a.shape
    d = b.shape[1]
    tm = _pick(t, 512, 8)

    def body(a_ref, b_ref, r_ref, w_ref, x_ref, h_ref):
        xv = _dg(a_ref[...], b_ref[...], 1, 0) + r_ref[...]
        x_ref[...] = xv
        h_ref[...] = _rms(xv, w_ref[...]).astype(h_ref.dtype)

    row = pl.BlockSpec((tm, d), lambda i: (i, 0))
    return pl.pallas_call(
        body, name=name, grid=(t // tm,),
        in_specs=[pl.BlockSpec((tm, k), lambda i: (i, 0)), pl.BlockSpec((k, d), lambda i: (0, 0)), row,
                  pl.BlockSpec((1, d), lambda i: (0, 0))],
        out_specs=[row, row],
        out_shape=[jax.ShapeDtypeStruct((t, d), F32), jax.ShapeDtypeStruct((t, d), BF16)],
        compiler_params=_params(("arbitrary",)),
    )(a, b, res, w)


def mm_res_loss(a, b, res, tgt, w, name):
    t, k = a.shape
    d = b.shape[1]
    tm = _pick(t, 512, 16)

    def body(a_ref, b_ref, r_ref, t_ref, w_ref, dx_ref, dxb_ref, dw_ref, loss_ref):
        @pl.when(pl.program_id(0) == 0)
        def _():
            dw_ref[...] = jnp.zeros(dw_ref.shape, F32)
            loss_ref[...] = jnp.zeros(loss_ref.shape, F32)
        pieces = [slice(0, tm // 2), slice(tm // 2, tm)]
        accs = [_dg(a_ref[rows, :], b_ref[...], 1, 0) for rows in pieces]
        for rows, acc in zip(pieces, accs):
            tg = t_ref[rows, :]
            val, vjp = jax.vjp(lambda x_, w_: _loss_fn(x_, w_, tg), acc + r_ref[rows, :], w_ref[...])
            dx, dw = vjp(jnp.ones((1, 1), F32))
            dx_ref[rows, :] = dx
            dxb_ref[rows, :] = dx.astype(dxb_ref.dtype)
            dw_ref[...] += dw
            loss_ref[...] += jnp.broadcast_to(val, (1, LANES))

    row = pl.BlockSpec((tm, d), lambda i: (i, 0))
    vec = pl.BlockSpec((1, d), lambda i: (0, 0))
    return pl.pallas_call(
        body, name=name, grid=(t // tm,),
        in_specs=[pl.BlockSpec((tm, k), lambda i: (i, 0)), pl.BlockSpec((k, d), lambda i: (0, 0)), row, row, vec],
        out_specs=[row, row, vec, pl.BlockSpec((1, LANES), lambda i: (0, 0))],
        out_shape=[jax.ShapeDtypeStruct((t, d), F32), jax.ShapeDtypeStruct((t, d), BF16),
                   jax.ShapeDtypeStruct((1, d), F32), jax.ShapeDtypeStruct((1, LANES), F32)],
        compiler_params=_params(("arbitrary",)),
    )(a, b, res, tgt, w)


def mm_nt_norm_bwd(a, wmat, x, w, dres, name, with_bf16, ride=None):
    parts = a.shape[0] if a.ndim == 3 else 1
    t, kp = a.shape[-2], a.shape[-1]
    d = wmat.shape[0]
    tm = _pick(t, 256, 8)

    n_out = 3 if with_bf16 else 2

    def body(*refs):
        a_ref, w_refs = refs[0], refs[1:1 + parts]
        x_ref, nw_ref, r_ref, dx_ref = refs[1 + parts:5 + parts]
        dw_ref = refs[4 + parts + n_out - 1]

        @pl.when(pl.program_id(0) == 0)
        def _():
            dw_ref[...] = jnp.zeros(dw_ref.shape, F32)
        dh = None
        for p in range(parts):
            term = _dg(a_ref[p] if a.ndim == 3 else a_ref[...], w_refs[p][...], 1, 1)
            dh = term if dh is None else dh + term
        _, vjp = jax.vjp(_rms, x_ref[...], nw_ref[...])
        dx, dw = vjp(dh)
        dx = dx + r_ref[...]
        dx_ref[...] = dx
        if with_bf16:
            refs[5 + parts][...] = dx.astype(BF16)
        dw_ref[...] += dw

    row = pl.BlockSpec((tm, d), lambda i: (i, 0))
    vec = pl.BlockSpec((1, d), lambda i: (0, 0))
    a_spec = pl.BlockSpec((parts, tm, kp), lambda i: (0, i, 0)) if a.ndim == 3 else pl.BlockSpec((tm, kp), lambda i: (i, 0))
    w_specs = [pl.BlockSpec((d, kp), lambda i, p=p: (0, p)) for p in range(parts)]
    outs = [row] + ([row] if with_bf16 else []) + [vec]
    shapes = [jax.ShapeDtypeStruct((t, d), F32)] + ([jax.ShapeDtypeStruct((t, d), BF16)] if with_bf16 else [])
    grid = (t // tm,)
    body, r_in, r_out, r_shapes, r_sems, r_args = _riding(body, 4 + parts, n_out, grid, ride)
    return pl.pallas_call(
        body, name=name, grid=grid,
        in_specs=[a_spec] + w_specs + [row, vec, row] + r_in,
        out_specs=outs + r_out, out_shape=shapes + [jax.ShapeDtypeStruct((1, d), F32)] + r_shapes,
        scratch_shapes=r_sems, compiler_params=_params(("arbitrary",)),
    )(a, *([wmat] * parts), x, w, dres, *r_args)


def _lower_bound(l0, l1):
    m = jnp.maximum(l0, l1)
    e0 = jnp.exp(l0 - m)
    e1 = jnp.exp(l1 - m)
    return e0 / (e0 + e1)


def _gla_consts(rev):
    ri = lax.broadcasted_iota(jnp.int32, (CHUNK, CHUNK), 0)
    ci = lax.broadcasted_iota(jnp.int32, (CHUNK, CHUNK), 1)
    keep = (ci >= ri) if rev else (ci <= ri)
    ref_mask = lax.broadcasted_iota(jnp.int32, (CHUNK, 1), 0) == (CHUNK // 2 if rev else CHUNK // 2 - 1)
    return keep, ref_mask


def _gla_block(uq, uf, ui, l0, l1, st_in, rev):
    ncb = uq.shape[0] // CHUNK
    heads = range(HG_HEADS)
    keep, ref_mask = _gla_consts(rev)
    cum = keep.astype(F32)
    lb = _lower_bound(l0, l1)
    q = uq * _sigmoid(uq)
    k = (1.0 - lb) * _sigmoid(-uf)
    g = jnp.log(lb + (1.0 - lb) * _sigmoid(uf))

    def rows(a, c):
        return a[c * CHUNK:(c + 1) * CHUNK]

    def head(a, h):
        return a[:, h * HG_D:(h + 1) * HG_D]

    bs = [xdot_l(cum, rows(g, c)) for c in range(ncb)]
    q_in, k_in, q_b, k_d, decay = [], [], [], [], []
    for c in range(ncb):
        b = bs[c]
        b_ref = jnp.sum(jnp.where(ref_mask, b, 0.0), axis=0, keepdims=True)
        b_last = jnp.sum(rows(g, c), axis=0, keepdims=True)
        qc, kc = rows(q, c), rows(k, c)
        q_in.append(qc * jnp.exp(b - b_ref))
        k_in.append(kc * jnp.exp(b_ref - b))
        q_b.append(q_in[c] * jnp.exp(b_ref))
        k_d.append(k_in[c] * jnp.exp(b_last - b_ref))
        decay.append(jnp.exp(b_last))
    scores = [[jnp.where(keep, dot_nt(head(q_in[c], h), head(k_in[c], h)), 0.0) for h in heads] for c in range(ncb)]
    o_intra = [[dot_nn(scores[c][h], head(rows(ui, c), h)) for h in heads] for c in range(ncb)]
    contrib = [[dot_tn(head(rows(ui, c), h), head(k_d[c], h)) for h in heads] for c in range(ncb)]
    st = list(st_in)
    o_rows = [None] * ncb
    for c in (reversed(range(ncb)) if rev else range(ncb)):
        parts = []
        for h in heads:
            parts.append(o_intra[c][h] + dot_nt(head(q_b[c], h), st[h]))
            st[h] = st[h] * head(decay[c], h) + contrib[c][h]
        o_rows[c] = jnp.concatenate(parts, axis=1)
    return jnp.concatenate(o_rows, axis=0), tuple(st)


def _gla_blocks(t):
    tb = min(512, t)
    return tb, t // tb


def gla_fwd(u, l0, l1, fcol, rev, name):
    t = u.shape[0]
    tb, nb = _gla_blocks(t)

    def blk(i):
        return (nb - 1 - i) if rev else i

    def body(uq_ref, uf_ref, ui_ref, l0_ref, l1_ref, o_ref, ss_ref, st_ref):
        @pl.when(pl.program_id(0) == 0)
        def _():
            st_ref[...] = jnp.zeros(st_ref.shape, F32)
        ss_ref[0] = st_ref[...]
        o, st_out = _gla_block(uq_ref[...], uf_ref[...], ui_ref[...], l0_ref[...], l1_ref[...],
                               tuple(st_ref[h] for h in range(HG_HEADS)), rev)
        o_ref[...] = o
        for h in range(HG_HEADS):
            st_ref[h] = st_out[h]

    row = lambda cb: pl.BlockSpec((tb, HG_W), lambda i: (blk(i), cb))
    vec = pl.BlockSpec((1, HG_W), lambda i: (0, 0))
    return pl.pallas_call(
        body, name=name, grid=(nb,),
        in_specs=[row(0), row(fcol), row(3), vec, vec],
        out_specs=[pl.BlockSpec((tb, HG_W), lambda i: (blk(i), 0)),
                   pl.BlockSpec((1, HG_HEADS, HG_D, HG_D), lambda i: (blk(i), 0, 0, 0))],
        out_shape=[jax.ShapeDtypeStruct((t, HG_W), F32),
                   jax.ShapeDtypeStruct((nb, HG_HEADS, HG_D, HG_D), F32)],
        scratch_shapes=[pltpu.VMEM((HG_HEADS, HG_D, HG_D), F32)],
        compiler_params=_params(("arbitrary",)),
    )(u, u, u, l0, l1)


def gla_bwd(u, l0, l1, ss, do, fcol, rev, name):
    t = u.shape[0]
    tb, nb = _gla_blocks(t)

    def blk(i):
        return i if rev else (nb - 1 - i)

    def body(uq_ref, uf_ref, ui_ref, l0_ref, l1_ref, ss_ref, do_ref,
             dq_ref, df_ref, di_ref, dl0_ref, dl1_ref, dst_ref):
        @pl.when(pl.program_id(0) == 0)
        def _():
            dst_ref[...] = jnp.zeros(dst_ref.shape, F32)
            dl0_ref[...] = jnp.zeros(dl0_ref.shape, F32)
            dl1_ref[...] = jnp.zeros(dl1_ref.shape, F32)
        heads = range(HG_HEADS)
        _, vjp = jax.vjp(functools.partial(_gla_block, rev=rev), uq_ref[...], uf_ref[...], ui_ref[...],
                         l0_ref[...], l1_ref[...], tuple(ss_ref[0, h] for h in heads))
        dq, df, di, dl0, dl1, dst = vjp((do_ref[...], tuple(dst_ref[h] for h in heads)))
        dq_ref[...] = dq
        df_ref[...] = df
        di_ref[...] = di
        dl0_ref[...] += dl0
        dl1_ref[...] += dl1
        for h in heads:
            dst_ref[h] = dst[h]

    row = lambda cb: pl.BlockSpec((tb, HG_W), lambda i: (blk(i), cb))
    vec = pl.BlockSpec((1, HG_W), lambda i: (0, 0))
    orow = pl.BlockSpec((tb, HG_W), lambda i: (blk(i), 0))
    return pl.pallas_call(
        body, name=name, grid=(nb,),
        in_specs=[row(0), row(fcol), row(3), vec, vec,
                  pl.BlockSpec((1, HG_HEADS, HG_D, HG_D), lambda i: (blk(i), 0, 0, 0)), orow],
        out_specs=[orow, orow, orow, vec, vec],
        out_shape=[jax.ShapeDtypeStruct((t, HG_W), F32)] * 3 + [jax.ShapeDtypeStruct((1, HG_W), F32)] * 2,
        scratch_shapes=[pltpu.VMEM((HG_HEADS, HG_D, HG_D), F32)],
        compiler_params=_params(("arbitrary",)),
    )(u, u, u, l0, l1, ss, do)


def _rope_tables(t):
    rows = t // GRID_W
    row = jnp.repeat(jnp.arange(rows), GRID_W).astype(F32)
    col = jnp.tile(jnp.arange(GRID_W), rows).astype(F32)
    axis_dim = ATT_DH // 2
    freqs = ROPE_THETA ** (-jnp.arange(0, axis_dim, 2, dtype=F32) / axis_dim)
    ang = jnp.concatenate([row[:, None] * freqs, col[:, None] * freqs], axis=-1)
    cos2 = jnp.repeat(jnp.cos(ang), 2, axis=-1)
    sin2 = jnp.repeat(jnp.sin(ang), 2, axis=-1) * jnp.tile(jnp.array([-1.0, 1.0], F32), ATT_DH // 2)
    return cos2, sin2


def _group_sum_matrix(width):
    idx = np.arange(width) // ATT_DH
    return jnp.asarray((idx[:, None] == idx[None, :]).astype(np.float32))


def _tile_matrix(width):
    m = np.zeros((LANES, width), np.float32)
    m[np.arange(width) % ATT_DH, np.arange(width)] = 1.0
    return jnp.asarray(m)


def _tile_w(w128, tile_m):
    w8 = jnp.broadcast_to(w128, (8, LANES))
    return jnp.sum(xdot_r(w8, tile_m), axis=0, keepdims=True) * 0.125


def _head_norm_rope(a, w128, cos_t, sin_t, gsum, tile_m, scale):
    ssq = xdot_r(a * a, gsum)
    y = a * lax.rsqrt(ssq * (1.0 / ATT_DH) + EPS) * _tile_w(w128, tile_m)
    return (y * cos_t + swap_pairs(y) * sin_t) * scale


def _att_prep_fn(aq, ak, ck, sk, qw, kw, gq, gk, tq, tk):
    reps = aq.shape[1] // ck.shape[1]
    cq, sq = jnp.concatenate([ck] * reps, axis=1), jnp.concatenate([sk] * reps, axis=1)
    q = _head_norm_rope(aq, qw, cq, sq, gq, tq, ATT_DH ** -0.5 * LOG2E)
    k = _head_norm_rope(ak, kw, ck, sk, gk, tk, 1.0)
    return q, k


def _head_t(x, heads):
    xt = x.T
    return [xt[h * ATT_DH:(h + 1) * ATT_DH] for h in range(heads)]


def _head_s(x, heads):
    lane = lax.broadcasted_iota(jnp.int32, (x.shape[0], LANES), 1)
    out = []
    for h in range(heads):
        pair = x[:, (h // 2) * LANES:(h // 2 + 1) * LANES]
        if h % 2:
            pair = pltpu.roll(pair, ATT_DH, 1)
        out.append(jnp.where(lane < ATT_DH, pair, 0.0))
    return out


def _from_head_t(tile):
    return jnp.concatenate([tile[h, 0] for h in range(tile.shape[0])], axis=0).T


def _pad_rows(a):
    return jnp.concatenate([a, jnp.zeros(a.shape, a.dtype)], axis=0)


def _col_bcast(row_vec):
    return jnp.broadcast_to(row_vec, (LANES, row_vec.shape[1])).T


FA_K_FWD = 512
FA_K_BWD = 512
FA_STRIPS_FWD = 8
FA_STRIPS_BWD = 8
LOG2E = 1.4426950408889634
LN2 = 0.6931471805599453


def _key_block(ref, j, tiles):
    return jnp.concatenate([ref[0, j * tiles + i] for i in range(tiles)], axis=1)


def _riding(body, n_in, n_out, grid, ride):
    if ride is None:
        return body, [], [], [], [], []
    kind, arrays = ride
    n = len(arrays)
    steps = int(np.prod(grid))

    def riding_body(*refs):
        cuts = np.cumsum([0, n_in, n, n_out, n]).tolist()
        own_in, rin, own_out, rout = (refs[a:b] for a, b in zip(cuts[:-1], cuts[1:]))
        start, forward, finish = _exchange_phases(rin, rout, *refs[cuts[-1]:], kind)
        step = 0
        for axis, size in enumerate(grid):
            step = step * size + pl.program_id(axis)
        pl.when(step == 0)(start)
        pl.when(step == steps // 2)(forward)
        body(*own_in, *own_out)
        pl.when(step == steps - 1)(finish)

    return riding_body, _hbm_specs(n), _hbm_specs(n), _exchange_shapes(arrays, kind), _exchange_sems(n), list(arrays)


def fa_fwd(qt, ks, vt, ride=None):
    _, ns, dh, nq = qt.shape
    lk = ks.shape[1]
    bk = min(FA_K_FWD, lk)
    nk = lk // bk
    spg = min(FA_STRIPS_FWD, ns)
    grid = (ATT_KV, ns // spg)

    def body(q_ref, k_ref, v_ref, o_ref, lse_ref):
        qs = [_pad_rows(q_ref[0, c]) for c in range(spg)]

        def keys(j):
            return k_ref[0, j * bk:(j + 1) * bk, :]

        def step(j, carry):
            st0, stats = carry
            kb = keys(j)
            vb = _key_block(v_ref, j, bk // nq)
            sts = [st0] + [_dg(kb, qs[c], 1, 0) for c in range(1, spg)]
            out = []
            for c in range(spg):
                m, l, acc = stats[c]
                m_new = jnp.maximum(m, jnp.max(sts[c], axis=0, keepdims=True))
                alpha = jnp.exp2(m - m_new)
                p = jnp.exp2(sts[c] - m_new)
                l = alpha * l + jnp.sum(p, axis=0, keepdims=True)
                if c == spg - 1:
                    st0 = _dg(keys(min(j + 1, nk - 1)), qs[0], 1, 0)
                acc = alpha * acc + _dg(vb, p, 1, 0)
                out.append((m_new, l, acc))
            return st0, tuple(out)

        init = tuple((jnp.full((1, nq), -jnp.inf, F32), jnp.zeros((1, nq), F32), jnp.zeros((dh, nq), F32))
                     for _ in range(spg))
        carry = (_dg(keys(0), qs[0], 1, 0), init)
        for j in range(nk):
            carry = step(j, carry)
        _, res = carry
        for c in range(spg):
            m, l, acc = res[c]
            o_ref[0, c] = acc / l
            lse_ref[0, c] = _col_bcast(m + jnp.log2(l))

    body, r_in, r_out, r_shapes, r_sems, r_args = _riding(body, 3, 2, grid, ride)
    return pl.pallas_call(
        body, name="fa_fwd", grid=grid,
        in_specs=[pl.BlockSpec((1, spg, dh, nq), lambda g, i: (g, i, 0, 0)),
                  pl.BlockSpec((1, lk, LANES), lambda g, i: (g, 0, 0)),
                  pl.BlockSpec((1, lk // nq, dh, nq), lambda g, i: (g, 0, 0, 0))] + r_in,
        out_specs=[pl.BlockSpec((1, spg, dh, nq), lambda g, i: (g, i, 0, 0)),
                   pl.BlockSpec((1, spg, nq, LANES), lambda g, i: (g, i, 0, 0))] + r_out,
        out_shape=[jax.ShapeDtypeStruct((ATT_KV, ns, dh, nq), F32),
                   jax.ShapeDtypeStruct((ATT_KV, ns, nq, LANES), F32)] + r_shapes,
        scratch_shapes=r_sems,
        compiler_params=_params(("arbitrary", "arbitrary")),
    )(qt, ks, vt, *r_args)


def fa_bwd(qs, qt, dos, dot_, ot, lse, ks, kt, vt, ride=None):
    _, ns, dh, nq = qt.shape
    lk = ks.shape[1]
    bk = min(FA_K_BWD, lk)
    nk = lk // bk
    tiles = bk // nq
    spg = min(FA_STRIPS_BWD, ns)
    nc = bk // LANES
    grid = (ATT_KV, ns // spg)

    def body(qs_ref, qt_ref, dos_ref, dot_ref, ot_ref, lse_ref, ks_ref, kt_ref, vt_ref, dq_ref, dk_ref, dv_ref):
        @pl.when(pl.program_id(1) == 0)
        def _():
            dk_ref[...] = jnp.zeros(dk_ref.shape, F32)
            dv_ref[...] = jnp.zeros(dv_ref.shape, F32)

        strips = range(spg)
        lse_b = [lse_ref[0, c] for c in strips]
        d_b = [_col_bcast(jnp.sum(dot_ref[0, c].astype(F32) * ot_ref[0, c], axis=0, keepdims=True)) for c in strips]

        def step(j, dqs):
            ktb, vtb = _pad_rows(_key_block(kt_ref, j, tiles)), _pad_rows(_key_block(vt_ref, j, tiles))
            kb = ks_ref[0, j * bk:(j + 1) * bk, :]
            prods = [(_dg(qs_ref[0, c], ktb, 1, 0), _dg(dos_ref[0, c], vtb, 1, 0)) for c in strips]
            out = []
            for c in strips:
                s, dp = prods[c]
                ps, dss = [], []
                for cc in range(nc):
                    sl = slice(cc * LANES, (cc + 1) * LANES)
                    pc = jnp.exp2(s[:, sl] - lse_b[c])
                    ps.append(pc.astype(BF16))
                    dss.append((pc * (dp[:, sl] - d_b[c]) * LN2).astype(BF16))
                p, ds = jnp.concatenate(ps, axis=1), jnp.concatenate(dss, axis=1)
                dv = _dg(dot_ref[0, c], p, 1, 0)
                dk = _dg(qt_ref[0, c], ds, 1, 0)
                for i in range(tiles):
                    dv_ref[0, j * tiles + i] += dv[:, i * nq:(i + 1) * nq]
                    dk_ref[0, j * tiles + i] += dk[:, i * nq:(i + 1) * nq]
                out.append(dqs[c] + _dg(ds, kb, 1, 0))
            return tuple(out)

        dqs = tuple(jnp.zeros((nq, LANES), F32) for _ in strips)
        for j in range(nk):
            dqs = step(j, dqs)
        for c in strips:
            dq_ref[0, c] = dqs[c].T[:dh]

    sspec = pl.BlockSpec((1, spg, nq, LANES), lambda g, i: (g, i, 0, 0))
    tspec = pl.BlockSpec((1, spg, dh, nq), lambda g, i: (g, i, 0, 0))
    kspec = pl.BlockSpec((1, lk // nq, dh, nq), lambda g, i: (g, 0, 0, 0))
    body, r_in, r_out, r_shapes, r_sems, r_args = _riding(body, 9, 3, grid, ride)
    return pl.pallas_call(
        body, name="fa_bwd", grid=grid,
        in_specs=[sspec, tspec, sspec, tspec, tspec, sspec, pl.BlockSpec((1, lk, LANES), lambda g, i: (g, 0, 0)),
                  kspec, kspec] + r_in,
        out_specs=[tspec, kspec, kspec] + r_out,
        out_shape=[jax.ShapeDtypeStruct((ATT_KV, ns, dh, nq), F32),
                   jax.ShapeDtypeStruct((ATT_KV, lk // nq, dh, nq), F32),
                   jax.ShapeDtypeStruct((ATT_KV, lk // nq, dh, nq), F32)] + r_shapes,
        scratch_shapes=r_sems,
        compiler_params=_params(("arbitrary", "arbitrary")),
    )(qs, qt, dos, dot_, ot, lse, ks, kt, vt, *r_args)


def _post_mix_fn(of, ob, ug, oa, hgw, attw):
    o = of + ob
    parts = []
    for h in range(HG_HEADS):
        parts.append(_rms(o[:, h * HG_D:(h + 1) * HG_D], hgw))
    hg = jnp.concatenate(parts, axis=1) * (ug * _sigmoid(ug))
    return jnp.concatenate([hg, _rms(oa, attw)], axis=1)


def _swiglu_fn(gate, up):
    return gate * _sigmoid(gate) * up


def _loss_fn(x2, w, tgt):
    e = _rms(x2, w) - tgt
    return 0.5 * jnp.sum(jnp.mean(e * e, axis=-1, keepdims=True), axis=0, keepdims=True)


def _place():
    return lax.axis_index("x"), lax.axis_index("y"), lax.axis_index("c")


def _other_chips(x, y):
    return [(1 - x, y), (x, 1 - y), (1 - x, 1 - y)]


def _hbm_specs(n):
    return [pl.BlockSpec(memory_space=pl.ANY)] * n


SEMS_PER_ARRAY = 7


def _exchange_sems(n):
    return [pltpu.SemaphoreType.DMA((SEMS_PER_ARRAY * n,)), pltpu.SemaphoreType.DMA((SEMS_PER_ARRAY * n,)),
            pltpu.SemaphoreType.DMA((n,))]


def _exchange_phases(srcs, outs, ssem, rsem, lsem, kind):
    n = len(srcs)
    x, y, c = _place()
    k = 2 * x + y
    sib = (x, y, 1 - c)
    if kind == "siblings":
        def swap(a):
            hr = srcs[a].shape[1] // 2
            return pltpu.make_async_remote_copy(src_ref=srcs[a].at[:, pl.ds((1 - c) * hr, hr), :], dst_ref=outs[a],
                                                send_sem=ssem.at[a], recv_sem=rsem.at[a], device_id=sib,
                                                device_id_type=MESH)

        def start_swaps():
            for a in range(n):
                swap(a).start()

        def finish_swaps():
            for a in range(n):
                swap(a).wait()

        return start_swaps, lambda: None, finish_swaps
    reduce = kind == "reduce"
    chips = _other_chips(x, y)
    pairs = [(a, j) for a in range(n) for j in range(3)]

    def hrows(a):
        return srcs[a].shape[1] if reduce else srcs[a].shape[0] // 2

    def half(a, kk, cc):
        return outs[a].at[kk, pl.ds(cc * hrows(a), hrows(a)), :]

    def mine(a, kk):
        return srcs[a].at[kk] if reduce else srcs[a].at[pl.ds(c * hrows(a), hrows(a)), :]

    def copy(a, j, src_ref, dst_ref, to):
        return pltpu.make_async_remote_copy(src_ref=src_ref, dst_ref=dst_ref, send_sem=ssem.at[SEMS_PER_ARRAY * a + j],
                                            recv_sem=rsem.at[SEMS_PER_ARRAY * a + j], device_id=to, device_id_type=MESH)

    def local(a):
        if reduce:
            return pltpu.make_async_copy(srcs[a].at[k], half(a, k, c), lsem.at[a])
        return pltpu.make_async_copy(srcs[a], outs[a].at[k], lsem.at[a])

    def ici(a, j, arriving):
        px, py = chips[j]
        kk = 2 * px + py
        if arriving:
            return copy(a, j, mine(a, k), half(a, kk, c), (px, py, c))
        return copy(a, j, mine(a, kk), half(a, k, c), (px, py, c))

    def passed(a, j, arriving):
        px, py = chips[j]
        kk = 2 * px + py
        return copy(a, 3 + j, half(a, kk, c), half(a, kk, (1 - c) if arriving else c), sib)

    def own(a, arriving):
        return copy(a, 6, mine(a, k), half(a, k, (1 - c) if arriving else c), sib)

    def start():
        for a in range(n):
            local(a).start()
        for a, j in pairs:
            ici(a, j, False).start()
        if reduce:
            for a in range(n):
                own(a, False).start()

    def forward():
        for a, j in pairs:
            ici(a, j, True).wait_recv()
            passed(a, j, False).start()

    def finish():
        for a in range(n):
            if reduce:
                own(a, True).wait_recv()
            for j in range(3):
                passed(a, j, True).wait_recv()
        for a, j in pairs:
            ici(a, j, False).wait_send()
            passed(a, j, False).wait_send()
        for a in range(n):
            if reduce:
                own(a, False).wait_send()
            local(a).wait()

    return start, forward, finish


def _exchange_shapes(arrays, kind):
    if kind == "siblings":
        return [jax.ShapeDtypeStruct((N_CHIPS, g.shape[1] // 2, g.shape[2]), g.dtype) for g in arrays]
    if kind == "reduce":
        return [jax.ShapeDtypeStruct((N_CHIPS, 2 * p.shape[1], p.shape[2]), p.dtype) for p in arrays]
    return [jax.ShapeDtypeStruct((N_CHIPS,) + s.shape, s.dtype) for s in arrays]


def exchange(arrays, kind, name):
    n = len(arrays)

    def body(*refs):
        start, forward, finish = _exchange_phases(refs[:n], refs[n:2 * n], *refs[2 * n:], kind)
        start()
        forward()
        finish()

    return pl.pallas_call(
        body, name=name, in_specs=_hbm_specs(n), out_specs=_hbm_specs(n),
        out_shape=_exchange_shapes(arrays, kind), scratch_shapes=_exchange_sems(n),
    )(*arrays)


def allreduce_small(p, name):
    rows, width = p.shape

    def body(p_ref, s_ref, gath, ssem, rsem):
        x, y, c = _place()
        me = 4 * x + 2 * y + c
        copies = []
        for d in range(1, N_DEV):
            dx, dy, dc = (d >> 2) & 1, (d >> 1) & 1, d & 1
            peer = (x ^ dx, y ^ dy, c ^ dc)
            cp = pltpu.make_async_remote_copy(src_ref=p_ref, dst_ref=gath.at[me], send_sem=ssem.at[d - 1],
                                              recv_sem=rsem.at[d - 1], device_id=peer, device_id_type=MESH)
            cp.start()
            copies.append(cp)
        gath[me] = p_ref[...]
        for d, cp in enumerate(copies, start=1):
            dx, dy, dc = (d >> 2) & 1, (d >> 1) & 1, d & 1
            peer_slot = 4 * (x ^ dx) + 2 * (y ^ dy) + (c ^ dc)
            pltpu.make_async_remote_copy(src_ref=p_ref, dst_ref=gath.at[peer_slot], send_sem=ssem.at[d - 1],
                                         recv_sem=rsem.at[d - 1], device_id=(x ^ dx, y ^ dy, c ^ dc),
                                         device_id_type=MESH).wait_recv()
        for cp in copies:
            cp.wait_send()
        acc = gath[0]
        for d in range(1, N_DEV):
            acc = acc + gath[d]
        s_ref[...] = acc

    return pl.pallas_call(
        body, name=name,
        in_specs=[pl.BlockSpec(memory_space=pltpu.VMEM)],
        out_specs=pl.BlockSpec(memory_space=pltpu.VMEM),
        out_shape=jax.ShapeDtypeStruct((rows, width), F32),
        scratch_shapes=[pltpu.VMEM((N_DEV, rows, width), F32), pltpu.SemaphoreType.DMA((N_DEV - 1,)),
                        pltpu.SemaphoreType.DMA((N_DEV - 1,))],
    )(p)


def add_my_half(g, recv, name):
    _, r, cols = g.shape
    hr = r // 2
    tb = _pick(hr, 512, 8)
    nb = hr // tb
    c_arr = lax.axis_index("c").astype(jnp.int32).reshape(1)

    def body(c_ref, g_ref, r_ref, o_ref):
        o_ref[...] = (g_ref[...] + r_ref[...]).astype(o_ref.dtype)

    return pl.pallas_call(
        body, name=name,
        grid_spec=pltpu.PrefetchScalarGridSpec(
            num_scalar_prefetch=1, grid=(N_CHIPS, nb),
            in_specs=[pl.BlockSpec((1, tb, cols), lambda k, i, c_ref: (k, c_ref[0] * nb + i, 0)),
                      pl.BlockSpec((1, tb, cols), lambda k, i, c_ref: (k, i, 0))],
            out_specs=pl.BlockSpec((1, tb, cols), lambda k, i, c_ref: (k, i, 0))),
        out_shape=jax.ShapeDtypeStruct((N_CHIPS, hr, cols), BF16),
        compiler_params=_params(("arbitrary", "arbitrary")),
    )(c_arr, g, recv)


def sum_chips(parts, name):
    _, hr, cols = parts.shape
    tb = _pick(hr, 512, 8)

    def body(p_ref, o_ref):
        o_ref[...] = ((p_ref[0].astype(F32) + p_ref[1].astype(F32)) + p_ref[2].astype(F32)) + p_ref[3].astype(F32)

    return pl.pallas_call(
        body, name=name, grid=(hr // tb,),
        in_specs=[pl.BlockSpec((N_CHIPS, tb, cols), lambda i: (0, i, 0))],
        out_specs=pl.BlockSpec((tb, cols), lambda i: (i, 0)),
        out_shape=jax.ShapeDtypeStruct((hr, cols), F32),
        compiler_params=_params(("arbitrary",)),
    )(parts)


def adamw(w, g, m, v, name):
    rows, width = w.shape
    tb = _pick(rows, 512, 8)

    def body(w_ref, g_ref, m_ref, v_ref, d_ref, mo_ref, vo_ref):
        gg = g_ref[...]
        m_new = ADAM_B1 * m_ref[...] + (1.0 - ADAM_B1) * gg
        v_new = ADAM_B2 * v_ref[...] + (1.0 - ADAM_B2) * (gg * gg)
        m_hat = m_new / (1.0 - ADAM_B1 ** ADAM_STEP)
        v_hat = v_new / (1.0 - ADAM_B2 ** ADAM_STEP)
        d_ref[...] = -ADAM_LR * (m_hat / (jnp.sqrt(v_hat) + ADAM_EPS) + ADAM_WD * w_ref[...])
        mo_ref[...] = m_new
        vo_ref[...] = v_new

    spec = pl.BlockSpec((tb, width), lambda i: (i, 0))
    return pl.pallas_call(
        body, name=name, grid=(rows // tb,), in_specs=[spec] * 4, out_specs=[spec] * 3,
        out_shape=[jax.ShapeDtypeStruct((rows, width), F32)] * 3,
        compiler_params=_params(("arbitrary",)),
    )(w, g, m, v)


def _pack_rows(vecs, width=1024):
    rows, cur, used = [], [], 0
    for v in vecs:
        n = v.shape[1]
        if used + n > width:
            cur.append(jnp.zeros((1, width - used), F32))
            rows.append(jnp.concatenate(cur, axis=1))
            cur, used = [], 0
        cur.append(v)
        used += n
    cur.append(jnp.zeros((1, width - used), F32))
    rows.append(jnp.concatenate(cur, axis=1))
    return rows


def _pad128(v):
    return jnp.pad(v, ((0, 0), (0, LANES - v.shape[1])))


def kernel(x, norm1_w, w_in, lb_logits, hg_norm_w, q_norm_w, k_norm_w, att_norm_w, w_out, norm2_w, w_gate_up, w_down, final_norm_w, loss_target, m_norm1_w, m_w_in, m_lb_logits, m_hg_norm_w, m_q_norm_w, m_k_norm_w, m_att_norm_w, m_w_out, m_norm2_w, m_w_gate_up, m_w_down, m_final_norm_w, v_norm1_w, v_w_in, v_lb_logits, v_hg_norm_w, v_q_norm_w, v_k_norm_w, v_att_norm_w, v_w_out, v_norm2_w, v_w_gate_up, v_w_down, v_final_norm_w):
    t, d = x.shape[1], x.shape[2]
    xi, yi, ci = _place()
    chip = 2 * xi + yi
    x2d = x.reshape(t, d)
    tgt = loss_target.reshape(t, d)
    tok = min(TOK, t)
    nt = t // tok
    ns = ATT_GROUP * nt

    late_w = [w_out[0].astype(BF16), w_gate_up[0].astype(BF16), w_down[0].astype(BF16)]
    lb_rows = lb_logits.reshape(4, LANES) * (ci == 0).astype(F32)
    lb_pad = lax.dynamic_update_slice(jnp.zeros((8, 1024), F32), lb_rows, (0, chip * LANES))
    lb_full = allreduce_small(lb_pad, "gather_lb")[:4, :HG_W]
    l_f0, l_f1, l_b0, l_b1 = (lb_full[i:i + 1] for i in range(4))

    n1 = norm1_w.reshape(1, d)
    n2 = norm2_w.reshape(1, d)
    nf = final_norm_w.reshape(1, d)
    tm = min(512, t)
    h1, g_in = _rows(lambda a, w: ((_rms(a, w),), ()), "norm1", t, tm, [_rin(x2d, tm)], [n1], [_rout(t, tm, d, BF16)],
                     ride=("gather", [w_in[0].astype(BF16)]))
    wf_in = g_in.transpose(1, 0, 2).reshape(g_in.shape[1], -1)
    u = mm_rows(h1, wf_in, "mm_in")
    o_f, ss_f = gla_fwd(u, l_f0, l_f1, 1, False, "gla_fwd_f")
    o_b, ss_b = gla_fwd(u, l_b0, l_b1, 2, True, "gla_fwd_b")

    cos2, sin2 = _rope_tables(t)
    ck, sk = jnp.tile(cos2, (1, ATT_KV)), jnp.tile(sin2, (1, ATT_KV))
    qw, kw = _pad128(q_norm_w.reshape(1, ATT_DH)), _pad128(k_norm_w.reshape(1, ATT_DH))
    gq, gk = _group_sum_matrix(ATT_QW), _group_sum_matrix(ATT_KVW)
    tq, tk = _tile_matrix(ATT_QW), _tile_matrix(ATT_KVW)
    prep_in = [_rin(u, tok, ATT_QW, 5), _rin(u, tok, ATT_KVW, 24), _rin(ck, tok), _rin(sk, tok)]
    prep_consts = [qw, kw, gq, gk, tq, tk]

    def att_prep_fn(aq, ak, av, *rest):
        q, k = _att_prep_fn(aq, ak, *rest)
        return (_head_t(q, ATT_HEADS), _head_s(q, ATT_HEADS), _head_s(k, ATT_KV), _head_t(k, ATT_KV),
                _head_t(av, ATT_KV)), ()

    q_t, q_s, k_s, k_t, v_t = _rows(
        att_prep_fn, "att_prep", t, tok, prep_in[:2] + [_rin(u, tok, ATT_KVW, 25)] + prep_in[2:], prep_consts,
        [_tout(ATT_HEADS, nt, ATT_DH, tok, BF16), _tout(ATT_HEADS, nt, tok, LANES, BF16),
         _tout(ATT_KV, nt, tok, LANES, BF16), _tout(ATT_KV, nt, ATT_DH, tok, BF16),
         _tout(ATT_KV, nt, ATT_DH, tok, BF16)])
    q_t = q_t.reshape(ATT_KV, ns, ATT_DH, tok)
    q_s = q_s.reshape(ATT_KV, ns, tok, LANES)
    k_s = k_s.reshape(ATT_KV, t, LANES)
    o_t, lse, g_out, g_gu, g_down = fa_fwd(q_t, k_s, v_t, ride=("gather", late_w))
    wf_gu = g_gu.transpose(1, 0, 2).reshape(g_gu.shape[1], -1)
    wf_out = g_out.reshape(-1, g_out.shape[2])
    wf_down = g_down.reshape(-1, g_down.shape[2])
    o_tiles = o_t.reshape(ATT_HEADS, nt, ATT_DH, tok)

    hgw = hg_norm_w.reshape(1, HG_D)
    attw = att_norm_w.reshape(1, ATT_QW)
    mix_in = [_rin(o_f, tok), _rin(o_b, tok), _rin(u, tok, HG_W, 4), _tin(o_tiles)]
    (mix,) = _rows(lambda of, ob, ug, ot, hw, aw: ((_post_mix_fn(of, ob, ug, _from_head_t(ot), hw, aw),), ()),
                   "post_mix", t, tok, mix_in, [hgw, attw], [_rout(t, tok, d, BF16)])
    x1, h2 = mm_res_norm(mix, wf_out, x2d, n2, "mm_out")
    gu2, act = ffn_up(h2, wf_gu)

    dx2, dx2b, g_final, loss_part = mm_res_loss(act, wf_down, x1, tgt, nf, "mm_down_loss")
    dgu2 = ffn_down_dx(dx2b, wf_down, gu2)
    gw_down = mm_tn(dx2b, act, "mm_down_dw", transpose_out=True)
    dx1, dx1b, g_norm2 = mm_nt_norm_bwd(dgu2, wf_gu, x1, n2, dx2, "mm_gate_up_dx", True)
    gw_gu = mm_tn(h2, dgu2, "mm_gate_up_dw", col_shards=N_CHIPS)
    dmix = mm_rows(dx1b, wf_out, "mm_out_dx", trans_b=True)
    gw_out = mm_tn(mix, dx1b, "mm_out_dw")

    def post_mix_bwd_fn(of, ob, ug, ot, dm, hgw_, attw_):
        _, vjp = jax.vjp(_post_mix_fn, of, ob, ug, _from_head_t(ot), hgw_, attw_)
        dof, _, dug, doa, dhgw, dattw = vjp(dm)
        return (dof, dug, _head_t(doa, ATT_HEADS), _head_s(doa, ATT_HEADS)), (dhgw, dattw)

    late = ["w_out", "w_gate_up", "w_down"]
    late_g = [gw_out.reshape(N_CHIPS, -1, d), gw_gu, gw_down.reshape(N_CHIPS, -1, d)]
    do_hg, du_g, do_t, do_s, g_hg, g_att, *late_recv = _rows(
        post_mix_bwd_fn, "post_mix_bwd", t, tok, mix_in + [_rin(dmix, tok)], [hgw, attw],
        [_rout(t, tok, HG_W, F32), _rout(t, tok, HG_W, BF16), _tout(ATT_HEADS, nt, ATT_DH, tok, BF16),
         _tout(ATT_HEADS, nt, tok, LANES, BF16)], [HG_D, ATT_QW], ride=("siblings", late_g))
    late_part = [add_my_half(g, r, "add_my_half_" + n) for g, r, n in zip(late_g, late_recv, late)]
    dq_t, dk_t, dv_t, *late_parts = fa_bwd(q_s, q_t, do_s.reshape(q_s.shape), do_t.reshape(q_t.shape), o_t, lse, k_s, k_t,
                                           v_t, ride=("reduce", late_part))

    dq_f, df_f, di_f, dl_f0, dl_f1 = gla_bwd(u, l_f0, l_f1, ss_f, do_hg, 1, False, "gla_bwd_f")
    dq_b, df_b, di_b, dl_b0, dl_b1 = gla_bwd(u, l_b0, l_b1, ss_b, do_hg, 2, True, "gla_bwd_b")

    def att_prep_bwd_fn(aq, ak, ck_, sk_, dqt, dkt, dvt, qf, qb, ff, fb, i_f, i_b, dg, qw_, kw_, gq_, gk_, tq_, tk_):
        _, vjp = jax.vjp(lambda a, b, c_, e: _att_prep_fn(a, b, ck_, sk_, c_, e, gq_, gk_, tq_, tk_),
                         aq, ak, qw_, kw_)
        daq, dak, dqw, dkw = vjp((_from_head_t(dqt), _from_head_t(dkt)))
        du_tile = jnp.concatenate([qf + qb, ff, fb, i_f + i_b, dg.astype(F32), daq, dak, _from_head_t(dvt)], axis=1)
        return (du_tile,), (dqw, dkw)

    du, g_q, g_k = _rows(
        att_prep_bwd_fn, "att_prep_bwd", t, tok,
        prep_in + [_tin(dq_t.reshape(ATT_HEADS, nt, ATT_DH, tok)), _tin(dk_t), _tin(dv_t)]
        + [_rin(a, tok) for a in (dq_f, dq_b, df_f, df_b, di_f, di_b, du_g)], prep_consts,
        [_rout(t, tok, u.shape[1], BF16)], [LANES, LANES])
    gw_in_t = mm_tn(h1, du, "mm_in_dw", transpose_out=True)

    names = ["w_in"] + late
    g_in4 = gw_in_t.reshape(N_CHIPS, -1, d)
    in_part = add_my_half(g_in4, exchange([g_in4], "siblings", "rs_siblings_w_in")[0], "add_my_half_w_in")
    grad_x, g_norm1, in_parts = mm_nt_norm_bwd(du, wf_in, x2d, n1, dx1, "mm_in_dx", False, ride=("reduce", [in_part]))
    parts = [in_parts] + late_parts
    g_shard = [sum_chips(p, "sum_chips_" + n) for p, n in zip(parts, names)]
    g_shard[0] = g_shard[0].T
    big = {}
    for n, g, w, m, v in zip(names, g_shard, (w_in, w_out, w_gate_up, w_down), (m_w_in, m_w_out, m_w_gate_up, m_w_down),
                             (v_w_in, v_w_out, v_w_gate_up, v_w_down)):
        dlt, mn, vn = adamw(w[0], g, m[0], v[0], "adamw_" + n)
        big[n] = (g[None], dlt[None], mn[None], vn[None])

    small = [g_norm1, g_norm2, g_final, g_att, g_hg, g_q, g_k, loss_part, dl_f0, dl_f1, dl_b0, dl_b1]
    packed = _pack_rows(small)
    packed += [jnp.zeros((1, 1024), F32)] * (8 - len(packed))
    tot = allreduce_small(jnp.concatenate(packed, axis=0), "allreduce_small")
    s_norm1, s_norm2, s_final = tot[0:1], tot[1:2], tot[2:3]
    s_att, s_hg, s_q, s_k = tot[3:4, 0:512], tot[3:4, 512:640], tot[3:4, 640:704], tot[3:4, 768:832]
    loss = tot[3, 896]
    s_lb = jnp.concatenate([tot[4:5, 0:512], tot[4:5, 512:1024], tot[5:6, 0:512], tot[5:6, 512:1024]], axis=0)
    s_lb = lax.dynamic_slice(s_lb, (0, chip * LANES), (4, LANES)).reshape(1, 512)

    snames = ["norm1_w", "lb_logits", "hg_norm_w", "q_norm_w", "k_norm_w", "att_norm_w", "norm2_w", "final_norm_w"]
    g_small = dict(zip(snames, [s_norm1, s_lb, s_hg, s_q, s_k, s_att, s_norm2, s_final]))
    w_small = dict(zip(snames, [norm1_w, lb_logits, hg_norm_w, q_norm_w, k_norm_w, att_norm_w, norm2_w, final_norm_w]))
    m_small = dict(zip(snames, [m_norm1_w, m_lb_logits, m_hg_norm_w, m_q_norm_w, m_k_norm_w, m_att_norm_w, m_norm2_w, m_final_norm_w]))
    v_small = dict(zip(snames, [v_norm1_w, v_lb_logits, v_hg_norm_w, v_q_norm_w, v_k_norm_w, v_att_norm_w, v_norm2_w, v_final_norm_w]))

    def pack_small(tree):
        rows = _pack_rows([tree[n].reshape(1, -1) for n in snames])
        rows += [jnp.zeros((1, 1024), F32)] * (8 - len(rows))
        return jnp.concatenate(rows, axis=0)

    d_s, m_s, v_s = adamw(pack_small(w_small), pack_small(g_small), pack_small(m_small), pack_small(v_small), "adamw_small")

    def unpack_small(a):
        out, r, used = {}, 0, 0
        for n in snames:
            size = w_small[n].size
            if used + size > 1024:
                r, used = r + 1, 0
            out[n] = a[r, used:used + size].reshape(w_small[n].shape)
            used += size
        return out

    d_sm, m_sm, v_sm = unpack_small(d_s), unpack_small(m_s), unpack_small(v_s)
    g_sm = {n: g_small[n].reshape(w_small[n].shape) for n in snames}

    order = ["norm1_w", "w_in", "lb_logits", "hg_norm_w", "q_norm_w", "k_norm_w", "att_norm_w", "w_out", "norm2_w",
             "w_gate_up", "w_down", "final_norm_w"]

    def pick(small_tree, idx):
        return [big[n][idx] if n in big else small_tree[n] for n in order]

    return (loss, grad_x.reshape(x.shape), *pick(g_sm, 0), *pick(d_sm, 1), *pick(m_sm, 2), *pick(v_sm, 3))
```
